```python
import math
import jax, jax.numpy as jnp
from jax import lax
import numpy as np

D_MODEL = 2048
BATCH = 8
SEQ = 8192
DEPTH = 2

MIX_WIDTH = D_MODEL
GW = MIX_WIDTH // 4
GDN_HEAD_DIM = 128
GDN_HEADS = GW // GDN_HEAD_DIM
GDN_CONV = 4
GDN_CHUNK = 64
SSM_HEAD_DIM = 64
SSM_HEADS = GW // SSM_HEAD_DIM
SSM_GROUPS = 2
SSM_STATE = 128
SSM_CONV = 4
SSM_CHUNK = 128
SSM_CONV_DIM = GW + 2 * SSM_GROUPS * SSM_STATE
ATTN_HEAD_DIM = 64
ATTN_Q_HEADS = GW // ATTN_HEAD_DIM
ATTN_KV_HEADS = ATTN_Q_HEADS // 4
WINDOW = 128
ROPE_THETA = 10000.0
SC_CONV = 3
SC_GROUPS = 4
MLP_HIDDEN = 4 * D_MODEL
EPS = 1e-6

IN_WIDTH = (4 * GW + 2 * GDN_HEADS) + (GW + SSM_CONV_DIM + SSM_HEADS) + (GW + 2 * ATTN_KV_HEADS * ATTN_HEAD_DIM) + 3 * GW

kernel_name = 'hymba_style_hybrid_gdn_ssd_swa_shortconv'


def in_proj_split_points():
    sizes = (GW, GW, GW, GW, GDN_HEADS, GDN_HEADS,
             GW, SSM_CONV_DIM, SSM_HEADS,
             GW, ATTN_KV_HEADS * ATTN_HEAD_DIM, ATTN_KV_HEADS * ATTN_HEAD_DIM,
             GW, GW, GW)
    return [int(s) for s in np.cumsum(sizes)[:-1]]


def rms_norm(x, w):
    xf = x.astype(jnp.float32)
    y = xf * lax.rsqrt(jnp.mean(xf * xf, axis=-1, keepdims=True) + EPS)
    return (y * w.astype(jnp.float32)).astype(x.dtype)


def l2_norm(x):
    return x * lax.rsqrt(jnp.sum(x * x, axis=-1, keepdims=True) + EPS)


def causal_dwconv(x, w):
    K, C = w.shape
    return lax.conv_general_dilated(x, w[:, None, :].astype(x.dtype), window_strides=(1,),
                                    padding=[(K - 1, 0)], dimension_numbers=('NWC', 'WIO', 'NWC'),
                                    feature_group_count=C)


def rope_tables(positions):
    inv_freq = ROPE_THETA ** (-jnp.arange(0, ATTN_HEAD_DIM, 2, dtype=jnp.float32) / ATTN_HEAD_DIM)
    ang = positions.astype(jnp.float32)[..., None] * inv_freq
    return jnp.cos(ang), jnp.sin(ang)


def apply_rope(x, cos, sin):
    x1, x2 = jnp.split(x, 2, axis=-1)
    return jnp.concatenate([x1 * cos - x2 * sin, x2 * cos + x1 * sin], axis=-1).astype(x.dtype)


def gated_deltanet(q, k, v, z, b, a, conv_w, a_log, dt_bias, norm_w):
    Bsz, L, _ = q.shape
    H, dk, Cs = GDN_HEADS, GDN_HEAD_DIM, GDN_CHUNK
    Nc = L // Cs
    f32 = jnp.float32
    qkv = jax.nn.silu(causal_dwconv(jnp.concatenate([q, k, v], axis=-1), conv_w))
    q, k, v = [t.reshape(Bsz, L, H, dk).astype(f32) for t in jnp.split(qkv, 3, axis=-1)]
    q = l2_norm(q) * (dk ** -0.5)
    k = l2_norm(k)
    beta = jax.nn.sigmoid(b.astype(f32))
    g = -jnp.exp(a_log.astype(f32)) * jax.nn.softplus(a.astype(f32) + dt_bias.astype(f32))

    def to_chunks(t):
        return t.reshape(Bsz, Nc, Cs, H, -1).transpose(0, 3, 1, 2, 4)

    q, k, v = to_chunks(q), to_chunks(k), to_chunks(v)
    beta = to_chunks(beta[..., None])[..., 0]
    gc = jnp.cumsum(to_chunks(g[..., None])[..., 0], axis=-1)
    tri_incl = jnp.tril(jnp.ones((Cs, Cs), dtype=bool))
    tri_strict = jnp.tril(jnp.ones((Cs, Cs), dtype=bool), -1)
    decay = jnp.exp(jnp.where(tri_incl, gc[..., :, None] - gc[..., None, :], -jnp.inf))
    k_beta = k * beta[..., None]
    m = jnp.where(tri_strict, jnp.einsum('bhncd,bhnsd->bhncs', k_beta, k) * decay, 0.0)
    eye = jnp.eye(Cs, dtype=f32)
    t_inv = lax.linalg.triangular_solve(eye + m, jnp.broadcast_to(eye, m.shape), left_side=True,
                                        lower=True, unit_diagonal=True)
    u = t_inv @ (v * beta[..., None])
    w = t_inv @ (k_beta * jnp.exp(gc)[..., None])
    attn = jnp.where(tri_incl, jnp.einsum('bhncd,bhnsd->bhncs', q, k) * decay, 0.0)
    q_dec = q * jnp.exp(gc)[..., None]
    k_dec = k * jnp.exp(gc[..., -1:] - gc)[..., None]
    chunk_dec = jnp.exp(gc[..., -1])

    def step(S, inp):
        u_c, w_c, qd_c, kd_c, at_c, cd_c = inp
        v_new = u_c - w_c @ S
        o = qd_c @ S + at_c @ v_new
        S = S * cd_c[..., None, None] + jnp.swapaxes(kd_c, -1, -2) @ v_new
        return S, o

    xs = tuple(jnp.moveaxis(t, 2, 0) for t in (u, w, q_dec, k_dec, attn, chunk_dec))
    _, o = lax.scan(step, jnp.zeros((Bsz, H, dk, dk), f32), xs)
    o = o.transpose(1, 0, 3, 2, 4).reshape(Bsz, L, H, dk)
    zf = z.reshape(Bsz, L, H, dk).astype(f32)
    o = rms_norm(o, norm_w) * jax.nn.silu(zf)
    return o.reshape(Bsz, L, GW).astype(z.dtype)


def mamba2_ssd(z, xbc, dt, conv_w, conv_b, a_log, dt_bias, d_skip, norm_w):
    Bsz, L, _ = z.shape
    G, R, P, N, Cs = SSM_GROUPS, SSM_HEADS // SSM_GROUPS, SSM_HEAD_DIM, SSM_STATE, SSM_CHUNK
    Nc = L // Cs
    f32 = jnp.float32
    xbc = jax.nn.silu(causal_dwconv(xbc, conv_w) + conv_b)
    xs, bm, cm = jnp.split(xbc.astype(f32), [GW, GW + G * N], axis=-1)
    xs = xs.reshape(Bsz, Nc, Cs, G, R, P)
    bm = bm.reshape(Bsz, Nc, Cs, G, N)
    cm = cm.reshape(Bsz, Nc, Cs, G, N)
    dt = jax.nn.softplus(dt.astype(f32) + dt_bias.astype(f32)).reshape(Bsz, Nc, Cs, G, R)
    a = -jnp.exp(a_log.astype(f32)).reshape(G, R)
    da_cs = jnp.cumsum((dt * a).transpose(0, 3, 4, 1, 2), axis=-1)
    tri = jnp.tril(jnp.ones((Cs, Cs), dtype=bool))
    lmat = jnp.exp(jnp.where(tri, da_cs[..., :, None] - da_cs[..., None, :], -jnp.inf))
    xdt = xs * dt[..., None]
    cb = jnp.einsum('bclgn,bcsgn->bgcls', cm, bm)
    y_diag = jnp.einsum('bgcls,bgrcls,bcsgrp->bclgrp', cb, lmat, xdt)
    decay_states = jnp.exp(da_cs[..., -1:] - da_cs)
    states = jnp.einsum('bclgn,bgrcl,bclgrp->bcgrpn', bm, decay_states, xdt)
    chunk_dec = jnp.exp(da_cs[..., -1])

    def step(S, inp):
        st, dec = inp
        return S * dec[..., None, None] + st, S

    _, prev = lax.scan(step, jnp.zeros((Bsz, G, R, P, N), f32),
                       (jnp.moveaxis(states, 1, 0), jnp.moveaxis(chunk_dec, 3, 0)))
    prev = jnp.moveaxis(prev, 0, 1)
    y_off = jnp.einsum('bclgn,bcgrpn,bgrcl->bclgrp', cm, prev, jnp.exp(da_cs))
    y = y_diag + y_off + xs * d_skip.astype(f32).reshape(G, R)[:, :, None]
    y = y.reshape(Bsz, L, GW) * jax.nn.silu(z.astype(f32))
    y = rms_norm(y.reshape(Bsz, L, G, GW // G), norm_w.reshape(G, GW // G))
    return y.reshape(Bsz, L, GW).astype(z.dtype)


def band_blocks(t):
    Bsz, L = t.shape[:2]
    cur = t.reshape(Bsz, L // WINDOW, WINDOW, *t.shape[2:])
    prev = jnp.pad(cur[:, :-1], ((0, 0), (1, 0), (0, 0), (0, 0), (0, 0)))
    return jnp.concatenate([prev, cur], axis=2)


def sliding_window_attention(q, k, v, sinks, cos, sin):
    Bsz, L, _ = q.shape
    Hkv, R, hd, W = ATTN_KV_HEADS, ATTN_Q_HEADS // ATTN_KV_HEADS, ATTN_HEAD_DIM, WINDOW
    nb = L // W
    q = apply_rope(q.reshape(Bsz, L, Hkv, R, hd), cos[:, :, None, None], sin[:, :, None, None])
    k = apply_rope(k.reshape(Bsz, L, Hkv, hd), cos[:, :, None], sin[:, :, None])
    v = v.reshape(Bsz, L, Hkv, hd)
    qb = q.reshape(Bsz, nb, W, Hkv, R, hd)
    kb, vb = band_blocks(k), band_blocks(v)
    s = jnp.einsum('bnqgrd,bnkgd->bngrqk', qb, kb, preferred_element_type=jnp.float32) * (hd ** -0.5)
    qi = jnp.arange(W)[:, None]
    kj = jnp.arange(2 * W)[None, :]
    rel = qi + W - kj
    band = (rel >= 0) & (rel < W)
    mask = band[None] & ((jnp.arange(nb)[:, None, None] > 0) | (kj >= W)[None])
    s = jnp.where(mask[None, :, None, None], s, -jnp.inf)
    sink = sinks.astype(jnp.float32).reshape(Hkv, R)[None, None, :, :, None, None]
    mx = jnp.maximum(jnp.max(s, axis=-1, keepdims=True), sink)
    p = jnp.exp(s - mx)
    p = p / (jnp.sum(p, axis=-1, keepdims=True) + jnp.exp(sink - mx))
    o = jnp.einsum('bngrqk,bnkgd->bnqgrd', p.astype(vb.dtype), vb)
    return o.reshape(Bsz, L, Hkv * R * hd)


def short_conv(b, c, h, conv_w):
    return b * causal_dwconv(c * h, conv_w)


def hybrid_mixer(h, cos, sin, w_in, w_out, gdn_conv_w, gdn_a_log, gdn_dt_bias, gdn_norm_w,
                 ssm_conv_w, ssm_conv_b, ssm_a_log, ssm_dt_bias, ssm_d, ssm_norm_w, attn_sinks, sc_conv_w):
    proj = h @ w_in
    (gq, gk, gv, gz, gb, ga, sz, sxbc, sdt, aq, ak, av, cb, cc, ch) = jnp.split(proj, in_proj_split_points(), axis=-1)
    y_a = gated_deltanet(gq, gk, gv, gz, gb, ga, gdn_conv_w, gdn_a_log, gdn_dt_bias, gdn_norm_w)
    y_b = mamba2_ssd(sz, sxbc, sdt, ssm_conv_w, ssm_conv_b, ssm_a_log, ssm_dt_bias, ssm_d, ssm_norm_w)
    y_c = sliding_window_attention(aq, ak, av, attn_sinks, cos, sin)
    y_d = short_conv(cb, cc, ch, sc_conv_w)
    return jnp.concatenate([y_a, y_b, y_c, y_d], axis=-1) @ w_out


def _fwd_setup_inputs(seed: int = 0) -> dict:
    key = jax.random.key(seed)
    ks = jax.random.split(key, 26)
    f32 = jnp.float32

    def nrm(k, shape, s):
        return jax.random.normal(k, shape, f32) * s

    def gain(k, shape):
        return 1.0 + 0.05 * jax.random.normal(k, shape, f32)

    def dt_bias(k, shape):
        dt = jnp.exp(jax.random.uniform(k, shape, f32, math.log(1e-3), math.log(1e-1)))
        return dt + jnp.log(-jnp.expm1(-dt))

    def a_log(k, shape):
        return jnp.log(jax.random.uniform(k, shape, f32, 1.0, 16.0))

    return {
        'x': nrm(ks[0], (BATCH, SEQ, D_MODEL), 1.0),
        'c': nrm(ks[1], (BATCH, D_MODEL), 1.0),
        'positions': jnp.broadcast_to(jnp.arange(SEQ, dtype=jnp.int32), (BATCH, SEQ)),
        'ada_w': nrm(ks[2], (DEPTH, D_MODEL, 6 * D_MODEL), 0.5 * D_MODEL ** -0.5),
        'ada_b': nrm(ks[3], (DEPTH, 6 * D_MODEL), 0.02),
        'norm_pre_mix': gain(ks[4], (DEPTH, D_MODEL)),
        'norm_post_mix': gain(ks[5], (DEPTH, D_MODEL)),
        'norm_pre_mlp': gain(ks[6], (DEPTH, D_MODEL)),
        'norm_post_mlp': gain(ks[7], (DEPTH, D_MODEL)),
        'w_in': nrm(ks[8], (DEPTH, D_MODEL, IN_WIDTH), D_MODEL ** -0.5),
        'w_out': nrm(ks[9], (DEPTH, MIX_WIDTH, D_MODEL), MIX_WIDTH ** -0.5),
        'gdn_conv_w': nrm(ks[10], (DEPTH, GDN_CONV, 3 * GW), GDN_CONV ** -0.5),
        'gdn_a_log': a_log(ks[11], (DEPTH, GDN_HEADS)),
        'gdn_dt_bias': dt_bias(ks[12], (DEPTH, GDN_HEADS)),
        'gdn_norm_w': gain(ks[13], (DEPTH, GDN_HEAD_DIM)),
        'ssm_conv_w': nrm(ks[14], (DEPTH, SSM_CONV, SSM_CONV_DIM), SSM_CONV ** -0.5),
        'ssm_conv_b': nrm(ks[15], (DEPTH, SSM_CONV_DIM), 0.02),
        'ssm_a_log': a_log(ks[16], (DEPTH, SSM_HEADS)),
        'ssm_dt_bias': dt_bias(ks[17], (DEPTH, SSM_HEADS)),
        'ssm_d': gain(ks[18], (DEPTH, SSM_HEADS)),
        'ssm_norm_w': gain(ks[19], (DEPTH, GW)),
        'attn_sinks': nrm(ks[20], (DEPTH, ATTN_Q_HEADS), 0.5),
        'sc_conv_w': nrm(ks[21], (DEPTH, SC_CONV, GW), SC_CONV ** -0.5),
        'w_up': nrm(ks[22], (DEPTH, D_MODEL, MLP_HIDDEN), D_MODEL ** -0.5),
        'w_down': nrm(ks[23], (DEPTH, MLP_HIDDEN, D_MODEL), MLP_HIDDEN ** -0.5),
    }


def _fwd_reference(x, c, positions, ada_w, ada_b, norm_pre_mix, norm_post_mix, norm_pre_mlp, norm_post_mlp,
              w_in, w_out, gdn_conv_w, gdn_a_log, gdn_dt_bias, gdn_norm_w, ssm_conv_w, ssm_conv_b,
              ssm_a_log, ssm_dt_bias, ssm_d, ssm_norm_w, attn_sinks, sc_conv_w, w_up, w_down):
    cos, sin = rope_tables(positions)
    c_act = jax.nn.silu(c)
    for i in range(DEPTH):
        mod = (c_act @ ada_w[i] + ada_b[i])[:, None, :]
        shift_a, scale_a, gate_a, shift_m, scale_m, gate_m = jnp.split(mod, 6, axis=-1)
        h = rms_norm(x, norm_pre_mix[i]) * (1.0 + scale_a) + shift_a
        y = hybrid_mixer(h, cos, sin, w_in[i], w_out[i], gdn_conv_w[i], gdn_a_log[i], gdn_dt_bias[i],
                         gdn_norm_w[i], ssm_conv_w[i], ssm_conv_b[i], ssm_a_log[i], ssm_dt_bias[i],
                         ssm_d[i], ssm_norm_w[i], attn_sinks[i], sc_conv_w[i])
        x = x + gate_a * rms_norm(y, norm_post_mix[i])
        h = rms_norm(x, norm_pre_mlp[i]) * (1.0 + scale_m) + shift_m
        y = jnp.square(jax.nn.relu(h @ w_up[i])) @ w_down[i]
        x = x + gate_m * rms_norm(y, norm_post_mlp[i])
    return x


import jax as _jax
import jax.numpy as _jnp

TWIN_FORMAT = 'train_step'
FWD_PARAMS = ['x', 'c', 'positions', 'ada_w', 'ada_b', 'norm_pre_mix', 'norm_post_mix', 'norm_pre_mlp', 'norm_post_mlp', 'w_in', 'w_out', 'gdn_conv_w', 'gdn_a_log', 'gdn_dt_bias', 'gdn_norm_w', 'ssm_conv_w', 'ssm_conv_b', 'ssm_a_log', 'ssm_dt_bias', 'ssm_d', 'ssm_norm_w', 'attn_sinks', 'sc_conv_w', 'w_up', 'w_down']
TWIN_WEIGHTS = ['ada_w', 'ada_b', 'norm_pre_mix', 'norm_post_mix', 'norm_pre_mlp', 'norm_post_mlp', 'w_in', 'w_out', 'gdn_conv_w', 'gdn_a_log', 'gdn_dt_bias', 'gdn_norm_w', 'ssm_conv_w', 'ssm_conv_b', 'ssm_a_log', 'ssm_dt_bias', 'ssm_d', 'ssm_norm_w', 'attn_sinks', 'sc_conv_w', 'w_up', 'w_down']
TWIN_DIFF_INPUT = 'x'
TWIN_INPUTS = ['x', 'c', 'positions', 'ada_w', 'ada_b', 'norm_pre_mix', 'norm_post_mix', 'norm_pre_mlp', 'norm_post_mlp', 'w_in', 'w_out', 'gdn_conv_w', 'gdn_a_log', 'gdn_dt_bias', 'gdn_norm_w', 'ssm_conv_w', 'ssm_conv_b', 'ssm_a_log', 'ssm_dt_bias', 'ssm_d', 'ssm_norm_w', 'attn_sinks', 'sc_conv_w', 'w_up', 'w_down', 'loss_target', 'm_ada_w', 'm_ada_b', 'm_norm_pre_mix', 'm_norm_post_mix', 'm_norm_pre_mlp', 'm_norm_post_mlp', 'm_w_in', 'm_w_out', 'm_gdn_conv_w', 'm_gdn_a_log', 'm_gdn_dt_bias', 'm_gdn_norm_w', 'm_ssm_conv_w', 'm_ssm_conv_b', 'm_ssm_a_log', 'm_ssm_dt_bias', 'm_ssm_d', 'm_ssm_norm_w', 'm_attn_sinks', 'm_sc_conv_w', 'm_w_up', 'm_w_down', 'v_ada_w', 'v_ada_b', 'v_norm_pre_mix', 'v_norm_post_mix', 'v_norm_pre_mlp', 'v_norm_post_mlp', 'v_w_in', 'v_w_out', 'v_gdn_conv_w', 'v_gdn_a_log', 'v_gdn_dt_bias', 'v_gdn_norm_w', 'v_ssm_conv_w', 'v_ssm_conv_b', 'v_ssm_a_log', 'v_ssm_dt_bias', 'v_ssm_d', 'v_ssm_norm_w', 'v_attn_sinks', 'v_sc_conv_w', 'v_w_up', 'v_w_down']
TWIN_OUTPUTS = ['loss', 'grad_x', 'grad_ada_w', 'grad_ada_b', 'grad_norm_pre_mix', 'grad_norm_post_mix', 'grad_norm_pre_mlp', 'grad_norm_post_mlp', 'grad_w_in', 'grad_w_out', 'grad_gdn_conv_w', 'grad_gdn_a_log', 'grad_gdn_dt_bias', 'grad_gdn_norm_w', 'grad_ssm_conv_w', 'grad_ssm_conv_b', 'grad_ssm_a_log', 'grad_ssm_dt_bias', 'grad_ssm_d', 'grad_ssm_norm_w', 'grad_attn_sinks', 'grad_sc_conv_w', 'grad_w_up', 'grad_w_down', 'delta_ada_w', 'delta_ada_b', 'delta_norm_pre_mix', 'delta_norm_post_mix', 'delta_norm_pre_mlp', 'delta_norm_post_mlp', 'delta_w_in', 'delta_w_out', 'delta_gdn_conv_w', 'delta_gdn_a_log', 'delta_gdn_dt_bias', 'delta_gdn_norm_w', 'delta_ssm_conv_w', 'delta_ssm_conv_b', 'delta_ssm_a_log', 'delta_ssm_dt_bias', 'delta_ssm_d', 'delta_ssm_norm_w', 'delta_attn_sinks', 'delta_sc_conv_w', 'delta_w_up', 'delta_w_down', 'new_m_ada_w', 'new_m_ada_b', 'new_m_norm_pre_mix', 'new_m_norm_post_mix', 'new_m_norm_pre_mlp', 'new_m_norm_post_mlp', 'new_m_w_in', 'new_m_w_out', 'new_m_gdn_conv_w', 'new_m_gdn_a_log', 'new_m_gdn_dt_bias', 'new_m_gdn_norm_w', 'new_m_ssm_conv_w', 'new_m_ssm_conv_b', 'new_m_ssm_a_log', 'new_m_ssm_dt_bias', 'new_m_ssm_d', 'new_m_ssm_norm_w', 'new_m_attn_sinks', 'new_m_sc_conv_w', 'new_m_w_up', 'new_m_w_down', 'new_v_ada_w', 'new_v_ada_b', 'new_v_norm_pre_mix', 'new_v_norm_post_mix', 'new_v_norm_pre_mlp', 'new_v_norm_post_mlp', 'new_v_w_in', 'new_v_w_out', 'new_v_gdn_conv_w', 'new_v_gdn_a_log', 'new_v_gdn_dt_bias', 'new_v_gdn_norm_w', 'new_v_ssm_conv_w', 'new_v_ssm_conv_b', 'new_v_ssm_a_log', 'new_v_ssm_dt_bias', 'new_v_ssm_d', 'new_v_ssm_norm_w', 'new_v_attn_sinks', 'new_v_sc_conv_w', 'new_v_w_up', 'new_v_w_down']
TWIN_LEAF_KINDS = {'loss': 'loss', 'grad_x': 'grad_x', 'grad_ada_w': 'grad_w', 'grad_ada_b': 'grad_w', 'grad_norm_pre_mix': 'grad_w', 'grad_norm_post_mix': 'grad_w', 'grad_norm_pre_mlp': 'grad_w', 'grad_norm_post_mlp': 'grad_w', 'grad_w_in': 'grad_w', 'grad_w_out': 'grad_w', 'grad_gdn_conv_w': 'grad_w', 'grad_gdn_a_log': 'grad_w', 'grad_gdn_dt_bias': 'grad_w', 'grad_gdn_norm_w': 'grad_w', 'grad_ssm_conv_w': 'grad_w', 'grad_ssm_conv_b': 'grad_w', 'grad_ssm_a_log': 'grad_w', 'grad_ssm_dt_bias': 'grad_w', 'grad_ssm_d': 'grad_w', 'grad_ssm_norm_w': 'grad_w', 'grad_attn_sinks': 'grad_w', 'grad_sc_conv_w': 'grad_w', 'grad_w_up': 'grad_w', 'grad_w_down': 'grad_w', 'delta_ada_w': 'delta_w', 'delta_ada_b': 'delta_w', 'delta_norm_pre_mix': 'delta_w', 'delta_norm_post_mix': 'delta_w', 'delta_norm_pre_mlp': 'delta_w', 'delta_norm_post_mlp': 'delta_w', 'delta_w_in': 'delta_w', 'delta_w_out': 'delta_w', 'delta_gdn_conv_w': 'delta_w', 'delta_gdn_a_log': 'delta_w', 'delta_gdn_dt_bias': 'delta_w', 'delta_gdn_norm_w': 'delta_w', 'delta_ssm_conv_w': 'delta_w', 'delta_ssm_conv_b': 'delta_w', 'delta_ssm_a_log': 'delta_w', 'delta_ssm_dt_bias': 'delta_w', 'delta_ssm_d': 'delta_w', 'delta_ssm_norm_w': 'delta_w', 'delta_attn_sinks': 'delta_w', 'delta_sc_conv_w': 'delta_w', 'delta_w_up': 'delta_w', 'delta_w_down': 'delta_w', 'new_m_ada_w': 'new_m', 'new_m_ada_b': 'new_m', 'new_m_norm_pre_mix': 'new_m', 'new_m_norm_post_mix': 'new_m', 'new_m_norm_pre_mlp': 'new_m', 'new_m_norm_post_mlp': 'new_m', 'new_m_w_in': 'new_m', 'new_m_w_out': 'new_m', 'new_m_gdn_conv_w': 'new_m', 'new_m_gdn_a_log': 'new_m', 'new_m_gdn_dt_bias': 'new_m', 'new_m_gdn_norm_w': 'new_m', 'new_m_ssm_conv_w': 'new_m', 'new_m_ssm_conv_b': 'new_m', 'new_m_ssm_a_log': 'new_m', 'new_m_ssm_dt_bias': 'new_m', 'new_m_ssm_d': 'new_m', 'new_m_ssm_norm_w': 'new_m', 'new_m_attn_sinks': 'new_m', 'new_m_sc_conv_w': 'new_m', 'new_m_w_up': 'new_m', 'new_m_w_down': 'new_m', 'new_v_ada_w': 'new_v', 'new_v_ada_b': 'new_v', 'new_v_norm_pre_mix': 'new_v', 'new_v_norm_post_mix': 'new_v', 'new_v_norm_pre_mlp': 'new_v', 'new_v_norm_post_mlp': 'new_v', 'new_v_w_in': 'new_v', 'new_v_w_out': 'new_v', 'new_v_gdn_conv_w': 'new_v', 'new_v_gdn_a_log': 'new_v', 'new_v_gdn_dt_bias': 'new_v', 'new_v_gdn_norm_w': 'new_v', 'new_v_ssm_conv_w': 'new_v', 'new_v_ssm_conv_b': 'new_v', 'new_v_ssm_a_log': 'new_v', 'new_v_ssm_dt_bias': 'new_v', 'new_v_ssm_d': 'new_v', 'new_v_ssm_norm_w': 'new_v', 'new_v_attn_sinks': 'new_v', 'new_v_sc_conv_w': 'new_v', 'new_v_w_up': 'new_v', 'new_v_w_down': 'new_v'}


def _forward(args):
    return _fwd_reference(*[args[k] for k in FWD_PARAMS])


def _output_shape():
    def fwd():
        inp = _fwd_setup_inputs(0)
        return _fwd_reference(*[inp[k] for k in FWD_PARAMS])
    out = _jax.eval_shape(fwd)
    return out.shape, out.dtype

N_MICROBATCH = 1
ADAM_LR = 0.001
ADAM_B1 = 0.9
ADAM_B2 = 0.999
ADAM_EPS = 1e-08
ADAM_WD = 0.01
ADAM_STEP = 10
PER_EXAMPLE_BATCH_AXIS = {'x': 0, 'c': 0, 'positions': 0, 'loss_target': 0}
SHARED_INPUTS = []
_WEIGHT_DTYPES = {'ada_w': _jnp.float32, 'ada_b': _jnp.float32, 'norm_pre_mix': _jnp.float32, 'norm_post_mix': _jnp.float32, 'norm_pre_mlp': _jnp.float32, 'norm_post_mlp': _jnp.float32, 'w_in': _jnp.float32, 'w_out': _jnp.float32, 'gdn_conv_w': _jnp.float32, 'gdn_a_log': _jnp.float32, 'gdn_dt_bias': _jnp.float32, 'gdn_norm_w': _jnp.float32, 'ssm_conv_w': _jnp.float32, 'ssm_conv_b': _jnp.float32, 'ssm_a_log': _jnp.float32, 'ssm_dt_bias': _jnp.float32, 'ssm_d': _jnp.float32, 'ssm_norm_w': _jnp.float32, 'attn_sinks': _jnp.float32, 'sc_conv_w': _jnp.float32, 'w_up': _jnp.float32, 'w_down': _jnp.float32}
MOMENT_SCALE = {'ada_w': 1.254105e+00, 'ada_b': 2.745195e+00, 'norm_pre_mix': 1.099167e-01, 'norm_post_mix': 3.280218e+00, 'norm_pre_mlp': 9.159857e-02, 'norm_post_mlp': 3.371600e+00, 'w_in': 9.452673e-02, 'w_out': 1.694480e-01, 'gdn_conv_w': 7.006473e-02, 'gdn_a_log': 1.917547e-01, 'gdn_dt_bias': 1.913511e-01, 'gdn_norm_w': 3.195776e-01, 'ssm_conv_w': 1.102664e-01, 'ssm_conv_b': 2.604677e-01, 'ssm_a_log': 1.281821e+00, 'ssm_dt_bias': 1.571047e-01, 'ssm_d': 7.005790e-01, 'ssm_norm_w': 2.190532e-01, 'attn_sinks': 1.577487e-02, 'sc_conv_w': 9.916165e-02, 'w_up': 7.396818e-02, 'w_down': 3.598079e-01}


def _to_microbatches(a, axis):
    t = _jnp.moveaxis(a, axis, 0)
    t = t.reshape((N_MICROBATCH, t.shape[0] // N_MICROBATCH) + t.shape[1:])
    return _jnp.moveaxis(t, 1, axis + 1)


def setup_inputs(seed: int = 0) -> dict:
    inp = _fwd_setup_inputs(seed)
    key = _jax.random.fold_in(_jax.random.key(seed), 7919)
    shape, _ = _output_shape()
    out = dict(inp)
    out["loss_target"] = _jax.random.normal(_jax.random.fold_in(key, 0), shape, _jnp.float32)
    for i, name in enumerate(TWIN_WEIGHTS):
        w = inp[name].astype(_jnp.float32)
        if MOMENT_SCALE is None:
            s = _jnp.sqrt(_jnp.mean(_jnp.square(w)) + 1e-30)
        else:
            s = MOMENT_SCALE[name]
        km, kv = _jax.random.split(_jax.random.fold_in(key, i + 1))
        out[name] = w
        out["m_" + name] = s * _jax.random.normal(km, w.shape, _jnp.float32)
        out["v_" + name] = (s * s) * _jax.random.uniform(kv, w.shape, _jnp.float32, 0.5, 1.5)
    if N_MICROBATCH > 1:
        for name, axis in PER_EXAMPLE_BATCH_AXIS.items():
            out[name] = _to_microbatches(out[name], axis)
    return {'x': out['x'], 'c': out['c'], 'positions': out['positions'], 'ada_w': out['ada_w'], 'ada_b': out['ada_b'], 'norm_pre_mix': out['norm_pre_mix'], 'norm_post_mix': out['norm_post_mix'], 'norm_pre_mlp': out['norm_pre_mlp'], 'norm_post_mlp': out['norm_post_mlp'], 'w_in': out['w_in'], 'w_out': out['w_out'], 'gdn_conv_w': out['gdn_conv_w'], 'gdn_a_log': out['gdn_a_log'], 'gdn_dt_bias': out['gdn_dt_bias'], 'gdn_norm_w': out['gdn_norm_w'], 'ssm_conv_w': out['ssm_conv_w'], 'ssm_conv_b': out['ssm_conv_b'], 'ssm_a_log': out['ssm_a_log'], 'ssm_dt_bias': out['ssm_dt_bias'], 'ssm_d': out['ssm_d'], 'ssm_norm_w': out['ssm_norm_w'], 'attn_sinks': out['attn_sinks'], 'sc_conv_w': out['sc_conv_w'], 'w_up': out['w_up'], 'w_down': out['w_down'], 'loss_target': out['loss_target'], 'm_ada_w': out['m_ada_w'], 'm_ada_b': out['m_ada_b'], 'm_norm_pre_mix': out['m_norm_pre_mix'], 'm_norm_post_mix': out['m_norm_post_mix'], 'm_norm_pre_mlp': out['m_norm_pre_mlp'], 'm_norm_post_mlp': out['m_norm_post_mlp'], 'm_w_in': out['m_w_in'], 'm_w_out': out['m_w_out'], 'm_gdn_conv_w': out['m_gdn_conv_w'], 'm_gdn_a_log': out['m_gdn_a_log'], 'm_gdn_dt_bias': out['m_gdn_dt_bias'], 'm_gdn_norm_w': out['m_gdn_norm_w'], 'm_ssm_conv_w': out['m_ssm_conv_w'], 'm_ssm_conv_b': out['m_ssm_conv_b'], 'm_ssm_a_log': out['m_ssm_a_log'], 'm_ssm_dt_bias': out['m_ssm_dt_bias'], 'm_ssm_d': out['m_ssm_d'], 'm_ssm_norm_w': out['m_ssm_norm_w'], 'm_attn_sinks': out['m_attn_sinks'], 'm_sc_conv_w': out['m_sc_conv_w'], 'm_w_up': out['m_w_up'], 'm_w_down': out['m_w_down'], 'v_ada_w': out['v_ada_w'], 'v_ada_b': out['v_ada_b'], 'v_norm_pre_mix': out['v_norm_pre_mix'], 'v_norm_post_mix': out['v_norm_post_mix'], 'v_norm_pre_mlp': out['v_norm_pre_mlp'], 'v_norm_post_mlp': out['v_norm_post_mlp'], 'v_w_in': out['v_w_in'], 'v_w_out': out['v_w_out'], 'v_gdn_conv_w': out['v_gdn_conv_w'], 'v_gdn_a_log': out['v_gdn_a_log'], 'v_gdn_dt_bias': out['v_gdn_dt_bias'], 'v_gdn_norm_w': out['v_gdn_norm_w'], 'v_ssm_conv_w': out['v_ssm_conv_w'], 'v_ssm_conv_b': out['v_ssm_conv_b'], 'v_ssm_a_log': out['v_ssm_a_log'], 'v_ssm_dt_bias': out['v_ssm_dt_bias'], 'v_ssm_d': out['v_ssm_d'], 'v_ssm_norm_w': out['v_ssm_norm_w'], 'v_attn_sinks': out['v_attn_sinks'], 'v_sc_conv_w': out['v_sc_conv_w'], 'v_w_up': out['v_w_up'], 'v_w_down': out['v_w_down']}


def _loss(weights, diff, rest, loss_target):
    with _jax.named_scope("forward"):
        args = {**rest, TWIN_DIFF_INPUT: diff, **{k: w.astype(_WEIGHT_DTYPES[k]) for k, w in weights.items()}}
        y = _forward(args)
    with _jax.named_scope("loss_head"):
        err = _jnp.square(y.astype(_jnp.float32) - loss_target)
        return 0.5 * _jnp.sum(_jnp.mean(err, axis=-1)) if err.ndim else 0.5 * err


def _adamw(w, g, m, v):
    m = ADAM_B1 * m + (1.0 - ADAM_B1) * g
    v = ADAM_B2 * v + (1.0 - ADAM_B2) * _jnp.square(g)
    m_hat = m / (1.0 - ADAM_B1 ** ADAM_STEP)
    v_hat = v / (1.0 - ADAM_B2 ** ADAM_STEP)
    delta = -ADAM_LR * (m_hat / (_jnp.sqrt(v_hat) + ADAM_EPS) + ADAM_WD * w)
    return delta, m, v


def reference(x, c, positions, ada_w, ada_b, norm_pre_mix, norm_post_mix, norm_pre_mlp, norm_post_mlp, w_in, w_out, gdn_conv_w, gdn_a_log, gdn_dt_bias, gdn_norm_w, ssm_conv_w, ssm_conv_b, ssm_a_log, ssm_dt_bias, ssm_d, ssm_norm_w, attn_sinks, sc_conv_w, w_up, w_down, loss_target, m_ada_w, m_ada_b, m_norm_pre_mix, m_norm_post_mix, m_norm_pre_mlp, m_norm_post_mlp, m_w_in, m_w_out, m_gdn_conv_w, m_gdn_a_log, m_gdn_dt_bias, m_gdn_norm_w, m_ssm_conv_w, m_ssm_conv_b, m_ssm_a_log, m_ssm_dt_bias, m_ssm_d, m_ssm_norm_w, m_attn_sinks, m_sc_conv_w, m_w_up, m_w_down, v_ada_w, v_ada_b, v_norm_pre_mix, v_norm_post_mix, v_norm_pre_mlp, v_norm_post_mlp, v_w_in, v_w_out, v_gdn_conv_w, v_gdn_a_log, v_gdn_dt_bias, v_gdn_norm_w, v_ssm_conv_w, v_ssm_conv_b, v_ssm_a_log, v_ssm_dt_bias, v_ssm_d, v_ssm_norm_w, v_attn_sinks, v_sc_conv_w, v_w_up, v_w_down):
    given = dict(x=x, c=c, positions=positions, ada_w=ada_w, ada_b=ada_b, norm_pre_mix=norm_pre_mix, norm_post_mix=norm_post_mix, norm_pre_mlp=norm_pre_mlp, norm_post_mlp=norm_post_mlp, w_in=w_in, w_out=w_out, gdn_conv_w=gdn_conv_w, gdn_a_log=gdn_a_log, gdn_dt_bias=gdn_dt_bias, gdn_norm_w=gdn_norm_w, ssm_conv_w=ssm_conv_w, ssm_conv_b=ssm_conv_b, ssm_a_log=ssm_a_log, ssm_dt_bias=ssm_dt_bias, ssm_d=ssm_d, ssm_norm_w=ssm_norm_w, attn_sinks=attn_sinks, sc_conv_w=sc_conv_w, w_up=w_up, w_down=w_down, loss_target=loss_target, m_ada_w=m_ada_w, m_ada_b=m_ada_b, m_norm_pre_mix=m_norm_pre_mix, m_norm_post_mix=m_norm_post_mix, m_norm_pre_mlp=m_norm_pre_mlp, m_norm_post_mlp=m_norm_post_mlp, m_w_in=m_w_in, m_w_out=m_w_out, m_gdn_conv_w=m_gdn_conv_w, m_gdn_a_log=m_gdn_a_log, m_gdn_dt_bias=m_gdn_dt_bias, m_gdn_norm_w=m_gdn_norm_w, m_ssm_conv_w=m_ssm_conv_w, m_ssm_conv_b=m_ssm_conv_b, m_ssm_a_log=m_ssm_a_log, m_ssm_dt_bias=m_ssm_dt_bias, m_ssm_d=m_ssm_d, m_ssm_norm_w=m_ssm_norm_w, m_attn_sinks=m_attn_sinks, m_sc_conv_w=m_sc_conv_w, m_w_up=m_w_up, m_w_down=m_w_down, v_ada_w=v_ada_w, v_ada_b=v_ada_b, v_norm_pre_mix=v_norm_pre_mix, v_norm_post_mix=v_norm_post_mix, v_norm_pre_mlp=v_norm_pre_mlp, v_norm_post_mlp=v_norm_post_mlp, v_w_in=v_w_in, v_w_out=v_w_out, v_gdn_conv_w=v_gdn_conv_w, v_gdn_a_log=v_gdn_a_log, v_gdn_dt_bias=v_gdn_dt_bias, v_gdn_norm_w=v_gdn_norm_w, v_ssm_conv_w=v_ssm_conv_w, v_ssm_conv_b=v_ssm_conv_b, v_ssm_a_log=v_ssm_a_log, v_ssm_dt_bias=v_ssm_dt_bias, v_ssm_d=v_ssm_d, v_ssm_norm_w=v_ssm_norm_w, v_attn_sinks=v_attn_sinks, v_sc_conv_w=v_sc_conv_w, v_w_up=v_w_up, v_w_down=v_w_down)
    weights = {n: given[n] for n in TWIN_WEIGHTS}
    shared = {n: given[n] for n in SHARED_INPUTS}
    per_example = {n: given[n] for n in ['x', 'c', 'positions']}
    grad_fn = _jax.value_and_grad(_loss, argnums=(0, 1))

    def one_microbatch(ex, loss_target):
        ex = dict(ex)
        diff = ex.pop(TWIN_DIFF_INPUT)
        return grad_fn(weights, diff, {**shared, **ex}, loss_target)

    if N_MICROBATCH == 1:
        loss, (grad_w, grad_x) = one_microbatch(per_example, given["loss_target"])
    else:
        def body(carry, xs):
            loss_sum, grad_sum = carry
            l_k, (gw_k, gx_k) = one_microbatch(xs[0], xs[1])
            with _jax.named_scope("update"):
                return (loss_sum + l_k, _jax.tree.map(_jnp.add, grad_sum, gw_k)), gx_k

        init = (_jnp.zeros((), _jnp.float32), _jax.tree.map(_jnp.zeros_like, weights))
        (loss, grad_w), grad_x = _jax.lax.scan(body, init, (per_example, given["loss_target"]))
    with _jax.named_scope("update"):
        delta_w, new_m, new_v = {}, {}, {}
        for n in TWIN_WEIGHTS:
            delta_w[n], new_m[n], new_v[n] = _adamw(weights[n], grad_w[n], given["m_" + n], given["v_" + n])
    return (loss, grad_x, *[grad_w[n] for n in TWIN_WEIGHTS], *[delta_w[n] for n in TWIN_WEIGHTS],
            *[new_m[n] for n in TWIN_WEIGHTS], *[new_v[n] for n in TWIN_WEIGHTS])
```

```python
import functools

import numpy as np
import jax
import jax.numpy as jnp
from jax import lax
from jax.experimental import pallas as pl
from jax.experimental.pallas import tpu as pltpu

F32, BF16 = jnp.float32, jnp.bfloat16
MESH = pl.DeviceIdType.MESH
HI = lax.Precision.HIGHEST

DEPTH = 2
GW = 512
N_GDN, GDN_CS = 4, 64
SSD_CS, ATT_W = 128, 128
EPS = 1e-6
ROPE_THETA = 10000.0
IN_WIDTH = 5904
B1, B2, LR, AEPS, WD, STEP = 0.9, 0.999, 0.001, 1e-8, 0.01, 10

VMEM_LIMIT = 52 * 1024 * 1024
LANE = 128

O_GQKV, O_CB, O_CC, O_CH, O_XBC, O_AQ, O_GZ, O_SZ, O_AK, O_AV, O_GBA, O_SDT = (
    0, 1536, 2048, 2560, 3072, 4096, 5120, 5632, 6144, 6400, 6656, 6784)
PW = 6912
YCAT = 2560


def _cparams(*sem):
    return pltpu.CompilerParams(dimension_semantics=sem or None, vmem_limit_bytes=VMEM_LIMIT)


def _pick(n, cap, mult):
    if n <= cap:
        return n
    best = None
    for d in range(mult, cap + 1, mult):
        if n % d == 0:
            best = d
    assert best is not None, (n, cap, mult)
    return best


def _dg(a, b, ca, cb, hi):
    dn = (((ca,), (cb,)), ((), ()))
    if hi:
        return lax.dot_general(a, b, dn, precision=HI, preferred_element_type=F32)
    return lax.dot_general(a.astype(BF16), b.astype(BF16), dn, preferred_element_type=F32)


def _make_mm(hi):
    @jax.custom_vjp
    def nn(a, b):
        return _dg(a, b, 1, 0, hi)

    @jax.custom_vjp
    def nt(a, b):
        return _dg(a, b, 1, 1, hi)

    @jax.custom_vjp
    def tn(a, b):
        return _dg(a, b, 0, 0, hi)

    nn.defvjp(lambda a, b: (nn(a, b), (a, b)), lambda r, g: (nt(g, r[1]), tn(r[0], g)))
    nt.defvjp(lambda a, b: (nt(a, b), (a, b)), lambda r, g: (nn(g, r[1]), tn(g, r[0])))
    tn.defvjp(lambda a, b: (tn(a, b), (a, b)), lambda r, g: (nt(r[1], g), nn(r[0], g)))
    return nn, nt, tn


_nn, _nt, _tn = _make_mm(False)
_hnn, _hnt, _htn = _make_mm(True)


def _silu(x):
    return x * jax.nn.sigmoid(x)


def _softplus(x):
    return jnp.maximum(x, 0.0) + jnp.log1p(jnp.exp(-jnp.abs(x)))


def _colsel(v, idx):
    lane = lax.broadcasted_iota(jnp.int32, v.shape, 1)
    return jnp.sum(jnp.where(lane == idx, v, 0.0), axis=1, keepdims=True)


def _tri(n):
    r = lax.broadcasted_iota(jnp.int32, (n, n), 0)
    c = lax.broadcasted_iota(jnp.int32, (n, n), 1)
    return r >= c, r > c


def _tri_inv(m, n):
    r = lax.broadcasted_iota(jnp.int32, (n, n), 0)
    c = lax.broadcasted_iota(jnp.int32, (n, n), 1)
    x = jnp.where(r == c, 1.0, 0.0) - m
    p = _hnn(m, m)
    steps = int(np.log2(n)) - 1
    for s in range(steps):
        x = x + _hnn(x, p)
        if s < steps - 1:
            p = _hnn(p, p)
    return x


def _gdn_chunk(cq, ck, cv, z, ba, s, alog, dtb, nw, *, h):
    n = GDN_CS
    incl, strict = _tri(n)
    ltri = incl.astype(F32)
    q, k, v = _silu(cq), _silu(ck), _silu(cv)
    q = q * lax.rsqrt(jnp.sum(q * q, -1, keepdims=True) + EPS) * (128 ** -0.5)
    k = k * lax.rsqrt(jnp.sum(k * k, -1, keepdims=True) + EPS)
    beta = jax.nn.sigmoid(_colsel(ba, h))
    g = -jnp.exp(_colsel(alog, h)) * _softplus(_colsel(ba, N_GDN + h) + _colsel(dtb, h))
    g128 = jnp.broadcast_to(g, (n, LANE))
    gcb = _hnn(ltri, g128)
    gci = _hnn(ltri, jnp.broadcast_to(g, (n, n)))
    e0 = (lax.broadcasted_iota(jnp.int32, (n, LANE), 1) == 0).astype(F32)
    gcj = _hnt(e0, gcb)
    dec = jnp.where(incl, jnp.exp(jnp.where(incl, gci - gcj, 0.0)), 0.0)
    kb = k * beta
    m = jnp.where(strict, _nt(kb, k) * dec, 0.0)
    tinv = _tri_inv(m, n)
    egc = jnp.exp(gcb)
    u = _nn(tinv, v * beta)
    w = _nn(tinv, kb * egc)
    attn = jnp.where(incl, _nt(q, k) * dec, 0.0)
    glb = _hnn(jnp.ones((n, n), F32), g128)
    kd = k * jnp.exp(glb - gcb)
    cd = jnp.exp(_hnn(jnp.ones((LANE, n), F32), g128))
    v_new = u - _nn(w, s)
    o = _nn(q * egc, s) + _nn(attn, v_new)
    s_new = s * cd + _tn(kd, v_new)
    y = o * lax.rsqrt(jnp.mean(o * o, -1, keepdims=True) + EPS) * nw * _silu(z)
    return y, s_new


def _ssd_chunk(cxs, cb_, cc_, z, dtc, s0, s1, bx, bb, bc, alog, dtb, dsk, nw, *, g):
    n = SSD_CS
    incl, _ = _tri(n)
    ltri = incl.astype(F32)
    xs, bm, cm = _silu(cxs + bx), _silu(cb_ + bb), _silu(cc_ + bc)
    dt_all = _softplus(dtc + dtb)
    da_all = dt_all * (-jnp.exp(alog))
    cbm = _nt(cm, bm)
    e0 = (lax.broadcasted_iota(jnp.int32, (n, LANE), 1) == 0).astype(F32)
    ones = jnp.ones((n, n), F32)
    lane_lo = lax.broadcasted_iota(jnp.int32, (n, LANE), 1) < 64
    ys, s_out = [], []
    for p, s_p in enumerate((s0, s1)):
        xp = xs[:, p * LANE:(p + 1) * LANE]
        halves = []
        for e in range(2):
            h = 4 * g + 2 * p + e
            dt_h = _colsel(dt_all, h)
            da = jnp.broadcast_to(_colsel(da_all, h), (n, LANE))
            cs = _hnn(ltri, da)
            csj = _hnt(e0, cs)
            lm = jnp.where(incl, jnp.exp(jnp.where(incl, cs - csj, 0.0)), 0.0)
            tot = _hnn(ones, da)
            xdt = xp * dt_h
            y_h = _nn(cbm * lm, xdt) + _nn(cm * jnp.exp(cs), s_p) + xp * _colsel(dsk, h)
            s_h = s_p * jnp.exp(tot) + _tn(bm * jnp.exp(tot - cs), xdt)
            halves.append((y_h, s_h))
        ys.append(jnp.where(lane_lo, halves[0][0], halves[1][0]))
        s_out.append(jnp.where(lane_lo, halves[0][1], halves[1][1]))
    y = jnp.concatenate(ys, axis=1) * _silu(z)
    y = y * lax.rsqrt(jnp.mean(y * y, -1, keepdims=True) + EPS) * nw
    return y, s_out[0], s_out[1]


def _swa_block(qg, kp, kc, vp, vc, sink, notfirst, *, g):
    w = ATT_W
    kb = jnp.concatenate([kp, kc], axis=0)
    vb = jnp.concatenate([vp, vc], axis=0)
    qi = lax.broadcasted_iota(jnp.int32, (w, 2 * w), 0)
    kj = lax.broadcasted_iota(jnp.int32, (w, 2 * w), 1)
    rel = qi + w - kj
    mask = (rel >= 0) & (rel < w) & ((kj >= w) | (notfirst > 0.5))
    outs = []
    for r in range(4):
        q = qg[:, r * LANE:(r + 1) * LANE]
        s = _nt(q, kb) * (64 ** -0.5)
        s = jnp.where(mask, s, -1e30)
        snk = _colsel(sink, 4 * g + r)
        mx = lax.stop_gradient(jnp.maximum(jnp.max(s, axis=-1, keepdims=True), snk))
        p = jnp.where(mask, jnp.exp(s - mx), 0.0)
        p = p / (jnp.sum(p, axis=-1, keepdims=True) + jnp.exp(snk - mx))
        outs.append(_nn(p, vb))
    return jnp.concatenate(outs, axis=1)


def _rope(x, cos, sin):
    return x * cos + pltpu.roll(x, 64, 1) * sin


def _unrope(dy, cos, sin):
    return dy * cos + pltpu.roll(dy * sin, 64, 1)


def rowmap(name, fn, rows, consts, outs, accs=(), tile=256, cost_rows=None):
    t = rows[0][0].shape[0]
    tile = min(tile, t)
    assert t % tile == 0
    nr, nc, no, na = len(rows), len(consts), len(outs), len(accs)

    def body(*refs):
        rv = [r[...] for r in refs[:nr]]
        cv = [r[...] for r in refs[nr:nr + nc]]
        ov, av = fn(rv, cv)
        for ref, o in zip(refs[nr + nc:nr + nc + no], ov):
            ref[...] = o.astype(ref.dtype)
        if na:
            a_refs = refs[nr + nc + no:]

            @pl.when(pl.program_id(0) == 0)
            def _():
                for ref in a_refs:
                    ref[...] = jnp.zeros_like(ref)

            for ref, a in zip(a_refs, av):
                ref[...] += a

    def cspec(off, width):
        assert off % width == 0
        cb = off // width
        return pl.BlockSpec((tile, width), lambda i: (i, cb))

    in_specs = [cspec(off, wd) for (_, off, wd) in rows]
    in_specs += [pl.BlockSpec(c.shape, lambda i: (0, 0)) for c in consts]
    out_specs = [pl.BlockSpec((tile, wd), lambda i: (i, 0)) for (wd, _) in outs]
    out_specs += [pl.BlockSpec(s, lambda i: (0, 0)) for s in accs]
    out_shape = [jax.ShapeDtypeStruct((t, wd), dt) for (wd, dt) in outs]
    out_shape += [jax.ShapeDtypeStruct(s, F32) for s in accs]
    res = pl.pallas_call(
        body, name=name, grid=(t // tile,), in_specs=in_specs, out_specs=out_specs, out_shape=out_shape,
        compiler_params=_cparams("arbitrary"),
    )(*[r[0] for r in rows], *consts)
    return res[:no], res[no:]


def rowmap_vjp(name, f, rows, consts, douts, n_diff, drow_dtypes, tile=256, add_rows=()):
    nr, nd, nadd = len(rows), len(douts), len(add_rows)

    def g(vals, cvals):
        rv, dov, addv = vals[:nr], vals[nr:nr + nd], vals[nr + nd:]
        fixed = rv[n_diff:]
        _, vjp = jax.vjp(lambda r, c: f(list(r) + list(fixed), c), rv[:n_diff], cvals)
        dr, dc = vjp([d.astype(F32) for d in dov])
        dr = list(dr)
        for a in addv:
            dr[0] = dr[0] + a
        return dr, dc

    outs = [(rows[i][2], drow_dtypes[i]) for i in range(n_diff)]
    accs = [c.shape for c in consts]
    return rowmap(name, g, list(rows) + list(douts) + list(add_rows), consts, outs, accs, tile=tile)


def mm_nn(name, a, w, out_dtype, a_fn=None, extra=None, epi=None, tm_cap=512):
    m, k = a.shape
    n = w.shape[1]
    tm, tn, tk = _pick(m, tm_cap, 16), _pick(n, 1024, LANE), _pick(k, 2048, LANE)
    nk = k // tk

    def body(*refs):
        a_ref, w_ref = refs[0], refs[1]
        e_ref = refs[2] if extra is not None else None
        o_ref, acc = refs[-2], refs[-1]
        kk = pl.program_id(2)

        @pl.when(kk == 0)
        def _():
            acc[...] = jnp.zeros_like(acc)

        av = a_ref[...]
        if a_fn is not None:
            av = a_fn(av)
        acc[...] += jnp.dot(av.astype(BF16), w_ref[...].astype(BF16), preferred_element_type=F32)

        @pl.when(kk == nk - 1)
        def _():
            r = acc[...]
            if epi is not None:
                r = epi(r, e_ref[...])
            o_ref[...] = r.astype(o_ref.dtype)

    in_specs = [pl.BlockSpec((tm, tk), lambda i, j, kk: (i, kk)), pl.BlockSpec((tk, tn), lambda i, j, kk: (kk, j))]
    ops = [a, w]
    if extra is not None:
        in_specs.append(pl.BlockSpec((tm, tn), lambda i, j, kk: (i, j)))
        ops.append(extra)
    return pl.pallas_call(
        body, name=name, grid=(m // tm, n // tn, nk), in_specs=in_specs,
        out_specs=pl.BlockSpec((tm, tn), lambda i, j, kk: (i, j)),
        out_shape=jax.ShapeDtypeStruct((m, n), out_dtype),
        scratch_shapes=[pltpu.VMEM((tm, tn), F32)],
        compiler_params=_cparams("parallel", "parallel", "arbitrary"),
    )(*ops)


def mm_tn(name, a, b, a_fn=None):
    t, m = a.shape
    n = b.shape[1]
    tm, tn, tk = _pick(m, 1024, LANE), _pick(n, 1024, LANE), _pick(t, 1024, 16)
    nk = t // tk

    def body(a_ref, b_ref, o_ref, acc):
        kk = pl.program_id(2)

        @pl.when(kk == 0)
        def _():
            acc[...] = jnp.zeros_like(acc)

        av = a_ref[...]
        if a_fn is not None:
            av = a_fn(av)
        acc[...] += lax.dot_general(av.astype(BF16), b_ref[...].astype(BF16), (((0,), (0,)), ((), ())),
                                    preferred_element_type=F32)

        @pl.when(kk == nk - 1)
        def _():
            o_ref[...] = acc[...]

    return pl.pallas_call(
        body, name=name, grid=(m // tm, n // tn, nk),
        in_specs=[pl.BlockSpec((tk, tm), lambda i, j, kk: (kk, i)), pl.BlockSpec((tk, tn), lambda i, j, kk: (kk, j))],
        out_specs=pl.BlockSpec((tm, tn), lambda i, j, kk: (i, j)),
        out_shape=jax.ShapeDtypeStruct((m, n), F32),
        scratch_shapes=[pltpu.VMEM((tm, tn), F32)],
        compiler_params=_cparams("parallel", "parallel", "arbitrary"),
    )(a, b)


CONV_TILE = 256
HALO = 8


def conv_fwd(name, xs, w):
    t = xs[0][0].shape[0]
    c = xs[0][2]
    kw = w.shape[0]
    tt = min(CONV_TILE, t)
    nx = len(xs)

    def body(*refs):
        cur_refs, prev_refs, w_ref, y_ref = refs[:nx], refs[nx:2 * nx], refs[2 * nx], refs[2 * nx + 1]
        i = pl.program_id(0)
        cur, prev = cur_refs[0][...], prev_refs[0][...]
        for r, p in zip(cur_refs[1:], prev_refs[1:]):
            cur, prev = cur * r[...], prev * p[...]
        prev = jnp.where(i > 0, prev, 0.0)
        xx = jnp.concatenate([prev, cur], axis=0)
        acc = jnp.zeros((tt, c), F32)
        for k in range(kw):
            sh = kw - 1 - k
            rolled = pltpu.roll(xx, sh, 0) if sh else xx
            acc = acc + rolled[HALO:] * w_ref[k:k + 1, :]
        y_ref[...] = acc

    def cur_spec(off):
        cb = off // c
        return pl.BlockSpec((tt, c), lambda i: (i, cb))

    def prev_spec(off):
        cb = off // c
        return pl.BlockSpec((HALO, c), lambda i: (jnp.maximum(i * (tt // HALO) - 1, 0), cb))

    for (_, off, wd) in xs:
        assert wd == c and off % c == 0
    return pl.pallas_call(
        body, name=name, grid=(t // tt,),
        in_specs=[cur_spec(off) for (_, off, _) in xs] + [prev_spec(off) for (_, off, _) in xs]
        + [pl.BlockSpec(w.shape, lambda i: (0, 0))],
        out_specs=pl.BlockSpec((tt, c), lambda i: (i, 0)),
        out_shape=jax.ShapeDtypeStruct((t, c), F32),
        compiler_params=_cparams("arbitrary"),
    )(*[x[0] for x in xs], *[x[0] for x in xs], w)


def conv_bwd(name, dy, xs, w, dx_dtype):
    t = xs[0][0].shape[0]
    c = xs[0][2]
    kw = w.shape[0]
    tt = min(CONV_TILE, t)
    nt = t // tt
    nx = len(xs)
    n = tt + HALO

    def body(*refs):
        dy_ref = refs[0]
        cur_refs, prev_refs = refs[1:1 + nx], refs[1 + nx:1 + 2 * nx]
        w_ref = refs[1 + 2 * nx]
        dx_refs = refs[2 + 2 * nx:2 + 3 * nx]
        dw_ref, carry = refs[2 + 3 * nx], refs[3 + 3 * nx]
        i = pl.program_id(0)

        @pl.when(i == 0)
        def _():
            carry[...] = jnp.zeros_like(carry)
            dw_ref[...] = jnp.zeros_like(dw_ref)

        curs = [r[...] for r in cur_refs]
        cur, prev = curs[0], prev_refs[0][...]
        for cv, p in zip(curs[1:], prev_refs[1:]):
            cur, prev = cur * cv, prev * p[...]
        prev = jnp.where(i < nt - 1, prev, 0.0)
        xx = jnp.concatenate([prev, cur], axis=0)
        dyv = dy_ref[...]
        dpad = jnp.concatenate([jnp.zeros((HALO, c), F32), dyv], axis=0)
        dxx = jnp.zeros((n, c), F32)
        dws = []
        for k in range(kw):
            sh = kw - 1 - k
            dxx = dxx + (pltpu.roll(dpad, n - sh, 0) if sh else dpad) * w_ref[k:k + 1, :]
            rolled = pltpu.roll(xx, sh, 0) if sh else xx
            dws.append(jnp.sum(dyv * rolled[HALO:], axis=0, keepdims=True))
        dw_ref[...] += jnp.concatenate(dws, axis=0)
        dcur = jnp.concatenate([dxx[HALO:tt], dxx[tt:] + carry[...]], axis=0)
        carry[...] = dxx[:HALO]
        for j, ref in enumerate(dx_refs):
            d = dcur
            for jj, cv in enumerate(curs):
                if jj != j:
                    d = d * cv
            ref[...] = d.astype(ref.dtype)

    def cur_spec(off):
        cb = off // c
        return pl.BlockSpec((tt, c), lambda i: (nt - 1 - i, cb))

    def prev_spec(off):
        cb = off // c
        return pl.BlockSpec((HALO, c), lambda i: (jnp.maximum((nt - 1 - i) * (tt // HALO) - 1, 0), cb))

    res = pl.pallas_call(
        body, name=name, grid=(nt,),
        in_specs=[pl.BlockSpec((tt, c), lambda i: (nt - 1 - i, 0))] + [cur_spec(off) for (_, off, _) in xs]
        + [prev_spec(off) for (_, off, _) in xs] + [pl.BlockSpec(w.shape, lambda i: (0, 0))],
        out_specs=[pl.BlockSpec((tt, c), lambda i: (nt - 1 - i, 0)) for _ in xs] + [pl.BlockSpec(w.shape, lambda i: (0, 0))],
        out_shape=[jax.ShapeDtypeStruct((t, c), dx_dtype) for _ in xs] + [jax.ShapeDtypeStruct(w.shape, F32)],
        scratch_shapes=[pltpu.VMEM((HALO, c), F32)],
        compiler_params=_cparams("arbitrary"),
    )(dy, *[x[0] for x in xs], *[x[0] for x in xs], w)
    return res[:nx], res[nx]


def _const_spec(a):
    return pl.BlockSpec(a.shape, lambda i: (0,) * a.ndim)


def gdn_fwd(name, cq, proj, alog, dtb, nw):
    t = cq.shape[0]
    nc = t // GDN_CS

    def body(cq_ref, z_ref, ba_ref, alog_ref, dtb_ref, nw_ref, y_ref, ssave_ref, s_scr):
        @pl.when(pl.program_id(0) == 0)
        def _():
            s_scr[...] = jnp.zeros_like(s_scr)

        ba = ba_ref[...]
        for h in range(N_GDN):
            sl = slice(h * LANE, (h + 1) * LANE)
            s = s_scr[h]
            ssave_ref[0, h] = s
            y, s_new = _gdn_chunk(cq_ref[:, sl], cq_ref[:, GW + h * LANE:GW + (h + 1) * LANE],
                                  cq_ref[:, 2 * GW + h * LANE:2 * GW + (h + 1) * LANE], z_ref[:, sl], ba, s,
                                  alog_ref[...], dtb_ref[...], nw_ref[...], h=h)
            y_ref[:, sl] = y
            s_scr[h] = s_new

    return pl.pallas_call(
        body, name=name, grid=(nc,),
        in_specs=[pl.BlockSpec((GDN_CS, 3 * GW), lambda i: (i, 0)),
                  pl.BlockSpec((GDN_CS, GW), lambda i: (i, O_GZ // GW)),
                  pl.BlockSpec((GDN_CS, LANE), lambda i: (i, O_GBA // LANE)),
                  _const_spec(alog), _const_spec(dtb), _const_spec(nw)],
        out_specs=[pl.BlockSpec((GDN_CS, GW), lambda i: (i, 0)),
                   pl.BlockSpec((1, N_GDN, LANE, LANE), lambda i: (i, 0, 0, 0))],
        out_shape=[jax.ShapeDtypeStruct((t, GW), F32), jax.ShapeDtypeStruct((nc, N_GDN, LANE, LANE), F32)],
        scratch_shapes=[pltpu.VMEM((N_GDN, LANE, LANE), F32)],
        compiler_params=_cparams("arbitrary"),
    )(cq, proj, proj, alog, dtb, nw)


def gdn_bwd(name, dy, cq, proj, ssave, alog, dtb, nw):
    t = cq.shape[0]
    nc = t // GDN_CS

    def body(dy_ref, cq_ref, z_ref, ba_ref, ssave_ref, alog_ref, dtb_ref, nw_ref,
             dcq_ref, dz_ref, dba_ref, dalog_ref, ddtb_ref, dnw_ref, ds_scr):
        @pl.when(pl.program_id(0) == 0)
        def _():
            ds_scr[...] = jnp.zeros_like(ds_scr)
            dalog_ref[...] = jnp.zeros_like(dalog_ref)
            ddtb_ref[...] = jnp.zeros_like(ddtb_ref)
            dnw_ref[...] = jnp.zeros_like(dnw_ref)

        ba = ba_ref[...]
        dba = jnp.zeros_like(ba)
        for h in range(N_GDN):
            sl = slice(h * LANE, (h + 1) * LANE)
            slk = slice(GW + h * LANE, GW + (h + 1) * LANE)
            slv = slice(2 * GW + h * LANE, 2 * GW + (h + 1) * LANE)
            _, vjp = jax.vjp(functools.partial(_gdn_chunk, h=h), cq_ref[:, sl], cq_ref[:, slk], cq_ref[:, slv],
                             z_ref[:, sl], ba, ssave_ref[0, h], alog_ref[...], dtb_ref[...], nw_ref[...])
            dq, dk, dv, dz, dba_h, ds, dal, ddt, dnw = vjp((dy_ref[:, sl], ds_scr[h]))
            dcq_ref[:, sl] = dq
            dcq_ref[:, slk] = dk
            dcq_ref[:, slv] = dv
            dz_ref[:, sl] = dz.astype(dz_ref.dtype)
            ds_scr[h] = ds
            dba = dba + dba_h
            dalog_ref[...] += dal
            ddtb_ref[...] += ddt
            dnw_ref[...] += dnw
        dba_ref[...] = dba.astype(dba_ref.dtype)

    rev = lambda i: nc - 1 - i
    return pl.pallas_call(
        body, name=name, grid=(nc,),
        in_specs=[pl.BlockSpec((GDN_CS, GW), lambda i: (rev(i), 0)),
                  pl.BlockSpec((GDN_CS, 3 * GW), lambda i: (rev(i), 0)),
                  pl.BlockSpec((GDN_CS, GW), lambda i: (rev(i), O_GZ // GW)),
                  pl.BlockSpec((GDN_CS, LANE), lambda i: (rev(i), O_GBA // LANE)),
                  pl.BlockSpec((1, N_GDN, LANE, LANE), lambda i: (rev(i), 0, 0, 0)),
                  _const_spec(alog), _const_spec(dtb), _const_spec(nw)],
        out_specs=[pl.BlockSpec((GDN_CS, 3 * GW), lambda i: (rev(i), 0)),
                   pl.BlockSpec((GDN_CS, GW), lambda i: (rev(i), 0)),
                   pl.BlockSpec((GDN_CS, LANE), lambda i: (rev(i), 0)),
                   _const_spec(alog), _const_spec(dtb), _const_spec(nw)],
        out_shape=[jax.ShapeDtypeStruct((t, 3 * GW), F32), jax.ShapeDtypeStruct((t, GW), BF16),
                   jax.ShapeDtypeStruct((t, LANE), BF16), jax.ShapeDtypeStruct(alog.shape, F32),
                   jax.ShapeDtypeStruct(dtb.shape, F32), jax.ShapeDtypeStruct(nw.shape, F32)],
        scratch_shapes=[pltpu.VMEM((N_GDN, LANE, LANE), F32)],
        compiler_params=_cparams("arbitrary"),
    )(dy, cq, proj, proj, ssave, alog, dtb, nw)


def _ssd_args(cx_ref, z_ref, dt_ref, s_pair, cb_ref, alog_ref, dtb_ref, dsk_ref, nw_ref, g):
    return (cx_ref[:, g * 256:(g + 1) * 256], cx_ref[:, GW + g * LANE:GW + (g + 1) * LANE],
            cx_ref[:, GW + 256 + g * LANE:GW + 256 + (g + 1) * LANE], z_ref[:, g * 256:(g + 1) * 256], dt_ref[...],
            s_pair[0], s_pair[1], cb_ref[:, g * 256:(g + 1) * 256], cb_ref[:, GW + g * LANE:GW + (g + 1) * LANE],
            cb_ref[:, GW + 256 + g * LANE:GW + 256 + (g + 1) * LANE], alog_ref[...], dtb_ref[...], dsk_ref[...],
            nw_ref[:, g * 256:(g + 1) * 256])


def ssd_fwd(name, cx, proj, convb, alog, dtb, dsk, nw):
    t = cx.shape[0]
    nc = t // SSD_CS

    def body(cx_ref, z_ref, dt_ref, cb_ref, alog_ref, dtb_ref, dsk_ref, nw_ref, y_ref, ssave_ref, s_scr):
        @pl.when(pl.program_id(0) == 0)
        def _():
            s_scr[...] = jnp.zeros_like(s_scr)

        for g in range(2):
            s_pair = (s_scr[2 * g], s_scr[2 * g + 1])
            ssave_ref[0, 2 * g] = s_pair[0]
            ssave_ref[0, 2 * g + 1] = s_pair[1]
            y, s0, s1 = _ssd_chunk(*_ssd_args(cx_ref, z_ref, dt_ref, s_pair, cb_ref, alog_ref, dtb_ref, dsk_ref,
                                              nw_ref, g), g=g)
            y_ref[:, g * 256:(g + 1) * 256] = y
            s_scr[2 * g] = s0
            s_scr[2 * g + 1] = s1

    return pl.pallas_call(
        body, name=name, grid=(nc,),
        in_specs=[pl.BlockSpec((SSD_CS, 1024), lambda i: (i, 0)),
                  pl.BlockSpec((SSD_CS, GW), lambda i: (i, O_SZ // GW)),
                  pl.BlockSpec((SSD_CS, LANE), lambda i: (i, O_SDT // LANE)),
                  _const_spec(convb), _const_spec(alog), _const_spec(dtb), _const_spec(dsk), _const_spec(nw)],
        out_specs=[pl.BlockSpec((SSD_CS, GW), lambda i: (i, 0)),
                   pl.BlockSpec((1, 4, LANE, LANE), lambda i: (i, 0, 0, 0))],
        out_shape=[jax.ShapeDtypeStruct((t, GW), F32), jax.ShapeDtypeStruct((nc, 4, LANE, LANE), F32)],
        scratch_shapes=[pltpu.VMEM((4, LANE, LANE), F32)],
        compiler_params=_cparams("arbitrary"),
    )(cx, proj, proj, convb, alog, dtb, dsk, nw)


def ssd_bwd(name, dy, cx, proj, ssave, convb, alog, dtb, dsk, nw):
    t = cx.shape[0]
    nc = t // SSD_CS

    def body(dy_ref, cx_ref, z_ref, dt_ref, ssave_ref, cb_ref, alog_ref, dtb_ref, dsk_ref, nw_ref,
             dcx_ref, dz_ref, ddt_ref, dcb_ref, dalog_ref, ddtb_ref, ddsk_ref, dnw_ref, ds_scr):
        @pl.when(pl.program_id(0) == 0)
        def _():
            ds_scr[...] = jnp.zeros_like(ds_scr)
            for r in (dcb_ref, dalog_ref, ddtb_ref, ddsk_ref, dnw_ref):
                r[...] = jnp.zeros_like(r)

        ddt = jnp.zeros((SSD_CS, LANE), F32)
        for g in range(2):
            s_pair = (ssave_ref[0, 2 * g], ssave_ref[0, 2 * g + 1])
            _, vjp = jax.vjp(functools.partial(_ssd_chunk, g=g),
                             *_ssd_args(cx_ref, z_ref, dt_ref, s_pair, cb_ref, alog_ref, dtb_ref, dsk_ref, nw_ref, g))
            (dxs, db, dc, dz, ddt_g, ds0, ds1, dbx, dbb, dbc, dal, ddtb, ddsk, dnw) = vjp(
                (dy_ref[:, g * 256:(g + 1) * 256], ds_scr[2 * g], ds_scr[2 * g + 1]))
            sx = slice(g * 256, (g + 1) * 256)
            sb = slice(GW + g * LANE, GW + (g + 1) * LANE)
            sc = slice(GW + 256 + g * LANE, GW + 256 + (g + 1) * LANE)
            dcx_ref[:, sx] = dxs
            dcx_ref[:, sb] = db
            dcx_ref[:, sc] = dc
            dz_ref[:, sx] = dz.astype(dz_ref.dtype)
            ds_scr[2 * g] = ds0
            ds_scr[2 * g + 1] = ds1
            ddt = ddt + ddt_g
            dcb_ref[:, sx] += dbx
            dcb_ref[:, sb] += dbb
            dcb_ref[:, sc] += dbc
            dalog_ref[...] += dal
            ddtb_ref[...] += ddtb
            ddsk_ref[...] += ddsk
            dnw_ref[:, sx] += dnw
        ddt_ref[...] = ddt.astype(ddt_ref.dtype)

    rev = lambda i: nc - 1 - i
    return pl.pallas_call(
        body, name=name, grid=(nc,),
        in_specs=[pl.BlockSpec((SSD_CS, GW), lambda i: (rev(i), 0)),
                  pl.BlockSpec((SSD_CS, 1024), lambda i: (rev(i), 0)),
                  pl.BlockSpec((SSD_CS, GW), lambda i: (rev(i), O_SZ // GW)),
                  pl.BlockSpec((SSD_CS, LANE), lambda i: (rev(i), O_SDT // LANE)),
                  pl.BlockSpec((1, 4, LANE, LANE), lambda i: (rev(i), 0, 0, 0)),
                  _const_spec(convb), _const_spec(alog), _const_spec(dtb), _const_spec(dsk), _const_spec(nw)],
        out_specs=[pl.BlockSpec((SSD_CS, 1024), lambda i: (rev(i), 0)),
                   pl.BlockSpec((SSD_CS, GW), lambda i: (rev(i), 0)),
                   pl.BlockSpec((SSD_CS, LANE), lambda i: (rev(i), 0)),
                   _const_spec(convb), _const_spec(alog), _const_spec(dtb), _const_spec(dsk), _const_spec(nw)],
        out_shape=[jax.ShapeDtypeStruct((t, 1024), F32), jax.ShapeDtypeStruct((t, GW), BF16),
                   jax.ShapeDtypeStruct((t, LANE), BF16), jax.ShapeDtypeStruct(convb.shape, F32),
                   jax.ShapeDtypeStruct(alog.shape, F32), jax.ShapeDtypeStruct(dtb.shape, F32),
                   jax.ShapeDtypeStruct(dsk.shape, F32), jax.ShapeDtypeStruct(nw.shape, F32)],
        scratch_shapes=[pltpu.VMEM((4, LANE, LANE), F32)],
        compiler_params=_cparams("arbitrary"),
    )(dy, cx, proj, proj, ssave, convb, alog, dtb, dsk, nw)


def _swa_specs(nb, rev):
    w = ATT_W
    cur = lambda i: (nb - 1 - i) if rev else i
    prv = lambda i: jnp.maximum(cur(i) - 1, 0)
    return [pl.BlockSpec((w, 1024), lambda i: (cur(i), O_AQ // 1024)),
            pl.BlockSpec((w, 256), lambda i: (cur(i), O_AK // 256)),
            pl.BlockSpec((w, 256), lambda i: (prv(i), O_AK // 256)),
            pl.BlockSpec((w, 256), lambda i: (cur(i), O_AV // 256)),
            pl.BlockSpec((w, 256), lambda i: (prv(i), O_AV // 256)),
            pl.BlockSpec((w, LANE), lambda i: (cur(i), 0)), pl.BlockSpec((w, LANE), lambda i: (prv(i), 0)),
            pl.BlockSpec((w, LANE), lambda i: (cur(i), 0)), pl.BlockSpec((w, LANE), lambda i: (prv(i), 0))]


def swa_fwd(name, proj, cos, sin, sinks):
    t = proj.shape[0]
    nb = t // ATT_W

    def body(q_ref, kc_ref, kp_ref, vc_ref, vp_ref, cc_ref, cp_ref, sc_ref, sp_ref, snk_ref, y_ref):
        notfirst = jnp.where(pl.program_id(0) > 0, 1.0, 0.0).astype(F32)
        cc, cp, sc, sp = cc_ref[...], cp_ref[...], sc_ref[...], sp_ref[...]
        for g in range(2):
            sl = slice(g * LANE, (g + 1) * LANE)
            qg = jnp.concatenate([_rope(q_ref[:, (4 * g + r) * LANE:(4 * g + r + 1) * LANE], cc, sc)
                                  for r in range(4)], axis=1)
            o = _swa_block(qg, _rope(kp_ref[:, sl], cp, sp), _rope(kc_ref[:, sl], cc, sc), vp_ref[:, sl],
                           vc_ref[:, sl], snk_ref[...], notfirst, g=g)
            y_ref[:, g * GW:(g + 1) * GW] = o

    return pl.pallas_call(
        body, name=name, grid=(nb,),
        in_specs=_swa_specs(nb, False) + [_const_spec(sinks)],
        out_specs=pl.BlockSpec((ATT_W, 1024), lambda i: (i, 0)),
        out_shape=jax.ShapeDtypeStruct((t, 1024), F32),
        compiler_params=_cparams("arbitrary"),
    )(proj, proj, proj, proj, proj, cos, cos, sin, sin, sinks)


def swa_bwd(name, dy, proj, cos, sin, sinks):
    t = proj.shape[0]
    nb = t // ATT_W

    def body(dy_ref, q_ref, kc_ref, kp_ref, vc_ref, vp_ref, cc_ref, cp_ref, sc_ref, sp_ref, snk_ref,
             dq_ref, dk_ref, dv_ref, dsnk_ref, ck_scr, cv_scr):
        i = pl.program_id(0)

        @pl.when(i == 0)
        def _():
            ck_scr[...] = jnp.zeros_like(ck_scr)
            cv_scr[...] = jnp.zeros_like(cv_scr)
            dsnk_ref[...] = jnp.zeros_like(dsnk_ref)

        notfirst = jnp.where(i < nb - 1, 1.0, 0.0).astype(F32)
        cc, cp, sc, sp = cc_ref[...], cp_ref[...], sc_ref[...], sp_ref[...]
        for g in range(2):
            sl = slice(g * LANE, (g + 1) * LANE)
            qg = jnp.concatenate([_rope(q_ref[:, (4 * g + r) * LANE:(4 * g + r + 1) * LANE], cc, sc)
                                  for r in range(4)], axis=1)
            _, vjp = jax.vjp(functools.partial(_swa_block, g=g), qg, _rope(kp_ref[:, sl], cp, sp),
                             _rope(kc_ref[:, sl], cc, sc), vp_ref[:, sl], vc_ref[:, sl], snk_ref[...], notfirst)
            dqg, dkp, dkc, dvp, dvc, dsnk, _ = vjp(dy_ref[:, g * GW:(g + 1) * GW])
            for r in range(4):
                hs = slice((4 * g + r) * LANE, (4 * g + r + 1) * LANE)
                dq_ref[:, hs] = _unrope(dqg[:, r * LANE:(r + 1) * LANE], cc, sc).astype(dq_ref.dtype)
            dk_ref[:, sl] = (_unrope(dkc, cc, sc) + ck_scr[:, sl]).astype(dk_ref.dtype)
            dv_ref[:, sl] = (dvc + cv_scr[:, sl]).astype(dv_ref.dtype)
            ck_scr[:, sl] = _unrope(dkp, cp, sp)
            cv_scr[:, sl] = dvp
            dsnk_ref[...] += dsnk

    rev = lambda i: nb - 1 - i
    return pl.pallas_call(
        body, name=name, grid=(nb,),
        in_specs=[pl.BlockSpec((ATT_W, 1024), lambda i: (rev(i), 0))] + _swa_specs(nb, True) + [_const_spec(sinks)],
        out_specs=[pl.BlockSpec((ATT_W, 1024), lambda i: (rev(i), 0)),
                   pl.BlockSpec((ATT_W, 256), lambda i: (rev(i), 0)),
                   pl.BlockSpec((ATT_W, 256), lambda i: (rev(i), 0)), _const_spec(sinks)],
        out_shape=[jax.ShapeDtypeStruct((t, 1024), BF16), jax.ShapeDtypeStruct((t, 256), BF16),
                   jax.ShapeDtypeStruct((t, 256), BF16), jax.ShapeDtypeStruct(sinks.shape, F32)],
        scratch_shapes=[pltpu.VMEM((ATT_W, 256), F32), pltpu.VMEM((ATT_W, 256), F32)],
        compiler_params=_cparams("arbitrary"),
    )(dy, proj, proj, proj, proj, proj, cos, cos, sin, sin, sinks)


def _me():
    return lax.axis_index("x"), lax.axis_index("y"), lax.axis_index("c")


ANY = pl.BlockSpec(memory_space=pl.ANY)


def allgather8(name, buf):
    r = buf.shape[0]

    def body(x_ref, out_ref, send_sems, recv_sems, local_sem):
        x, y, c = _me()
        me = 4 * x + 2 * y + c
        mine = pltpu.make_async_copy(x_ref, out_ref.at[me], local_sem)
        mine.start()
        copies = []
        for k in range(1, 8):
            bx, by, bc = (k >> 2) & 1, (k >> 1) & 1, k & 1
            to = ((1 - x) if bx else x, (1 - y) if by else y, (1 - c) if bc else c)
            cp = pltpu.make_async_remote_copy(src_ref=x_ref, dst_ref=out_ref.at[me], send_sem=send_sems.at[k - 1],
                                              recv_sem=recv_sems.at[k - 1], device_id=to, device_id_type=MESH)
            cp.start()
            copies.append(cp)
        for cp in copies:
            cp.wait_recv()
        for cp in copies:
            cp.wait_send()
        mine.wait()

    return pl.pallas_call(
        body, name=name, out_shape=jax.ShapeDtypeStruct((8, r, LANE), buf.dtype),
        in_specs=[pl.BlockSpec(memory_space=pltpu.VMEM)], out_specs=pl.BlockSpec(memory_space=pltpu.VMEM),
        scratch_shapes=[pltpu.SemaphoreType.DMA((7,)), pltpu.SemaphoreType.DMA((7,)), pltpu.SemaphoreType.DMA],
        compiler_params=pltpu.CompilerParams(vmem_limit_bytes=VMEM_LIMIT),
    )(buf)


def allgather_chips(name, flat):
    r = flat.shape[0]
    rh = r // 2
    assert rh % 16 == 0

    def body(x_ref, out_ref, send_sems, recv_sems, local_sem):
        x, y, c = _me()
        chip = 2 * x + y
        half = pl.ds(pl.multiple_of(c * rh, 16), rh)
        mine = pltpu.make_async_copy(x_ref, out_ref.at[chip], local_sem)
        mine.start()
        chips = [(1 - x, y), (x, 1 - y), (1 - x, 1 - y)]
        first = []
        for k, (cx, cy) in enumerate(chips):
            cp = pltpu.make_async_remote_copy(src_ref=x_ref.at[half], dst_ref=out_ref.at[chip, half],
                                              send_sem=send_sems.at[k], recv_sem=recv_sems.at[k],
                                              device_id=(cx, cy, c), device_id_type=MESH)
            cp.start()
            first.append(cp)
        passed = []
        for k, (cx, cy) in enumerate(chips):
            first[k].wait_recv()
            src = out_ref.at[2 * cx + cy, half]
            cp = pltpu.make_async_remote_copy(src_ref=src, dst_ref=src, send_sem=send_sems.at[3 + k],
                                              recv_sem=recv_sems.at[3 + k], device_id=(x, y, 1 - c),
                                              device_id_type=MESH)
            cp.start()
            passed.append(cp)
        for cp in passed:
            cp.wait_recv()
        for cp in first + passed:
            cp.wait_send()
        mine.wait()

    return pl.pallas_call(
        body, name=name, out_shape=jax.ShapeDtypeStruct((4, r, LANE), flat.dtype),
        in_specs=[ANY], out_specs=ANY,
        scratch_shapes=[pltpu.SemaphoreType.DMA((6,)), pltpu.SemaphoreType.DMA((6,)), pltpu.SemaphoreType.DMA],
    )(flat)


def sibling_swap_halves(name, g):
    _, _, rh, _ = g.shape

    def body(g_ref, out_ref, send_sem, recv_sem):
        x, y, c = _me()
        cp = pltpu.make_async_remote_copy(src_ref=g_ref.at[:, 1 - c], dst_ref=out_ref, send_sem=send_sem,
                                          recv_sem=recv_sem, device_id=(x, y, 1 - c), device_id_type=MESH)
        cp.start()
        cp.wait()

    return pl.pallas_call(
        body, name=name, out_shape=jax.ShapeDtypeStruct((4, rh, LANE), g.dtype), in_specs=[ANY], out_specs=ANY,
        scratch_shapes=[pltpu.SemaphoreType.DMA, pltpu.SemaphoreType.DMA],
    )(g)


def chips_exchange(name, p):
    _, rh, _ = p.shape

    def body(p_ref, out_ref, send_sems, recv_sems, local_sem):
        x, y, c = _me()
        chip = 2 * x + y
        mine = pltpu.make_async_copy(p_ref.at[chip], out_ref.at[chip], local_sem)
        mine.start()
        copies = []
        for k, (cx, cy) in enumerate([(1 - x, y), (x, 1 - y), (1 - x, 1 - y)]):
            cp = pltpu.make_async_remote_copy(src_ref=p_ref.at[2 * cx + cy], dst_ref=out_ref.at[chip],
                                              send_sem=send_sems.at[k], recv_sem=recv_sems.at[k],
                                              device_id=(cx, cy, c), device_id_type=MESH)
            cp.start()
            copies.append(cp)
        for cp in copies:
            cp.wait_recv()
        for cp in copies:
            cp.wait_send()
        mine.wait()

    return pl.pallas_call(
        body, name=name, out_shape=jax.ShapeDtypeStruct((4, rh, LANE), p.dtype), in_specs=[ANY], out_specs=ANY,
        scratch_shapes=[pltpu.SemaphoreType.DMA((3,)), pltpu.SemaphoreType.DMA((3,)), pltpu.SemaphoreType.DMA],
    )(p)


def sibling_join(name, s):
    rh = s.shape[0]

    def body(s_ref, out_ref, send_sem, recv_sem, local_sem):
        x, y, c = _me()
        mine = pltpu.make_async_copy(s_ref, out_ref.at[c], local_sem)
        mine.start()
        cp = pltpu.make_async_remote_copy(src_ref=s_ref, dst_ref=out_ref.at[c], send_sem=send_sem, recv_sem=recv_sem,
                                          device_id=(x, y, 1 - c), device_id_type=MESH)
        cp.start()
        cp.wait()
        mine.wait()

    return pl.pallas_call(
        body, name=name, out_shape=jax.ShapeDtypeStruct((2, rh, LANE), s.dtype), in_specs=[ANY], out_specs=ANY,
        scratch_shapes=[pltpu.SemaphoreType.DMA, pltpu.SemaphoreType.DMA, pltpu.SemaphoreType.DMA],
    )(s)


def add_half(name, g, recv, cidx):
    _, _, rh, _ = g.shape
    tile = _pick(rh, 2048, 8)

    def body(c_ref, g_ref, r_ref, o_ref):
        o_ref[...] = g_ref[...] + r_ref[...]

    return pl.pallas_call(
        body, name=name,
        grid_spec=pltpu.PrefetchScalarGridSpec(
            num_scalar_prefetch=1, grid=(4, rh // tile),
            in_specs=[pl.BlockSpec((None, None, tile, LANE), lambda s, i, c: (s, c[0], i, 0)),
                      pl.BlockSpec((None, tile, LANE), lambda s, i, c: (s, i, 0))],
            out_specs=pl.BlockSpec((None, tile, LANE), lambda s, i, c: (s, i, 0))),
        out_shape=jax.ShapeDtypeStruct((4, rh, LANE), F32),
        compiler_params=_cparams("parallel", "parallel"),
    )(cidx, g, recv)


def sum_slabs(name, a):
    n, r, _ = a.shape
    tile = _pick(r, 2048, 8)

    def body(a_ref, o_ref):
        acc = a_ref[0]
        for s in range(1, n):
            acc = acc + a_ref[s]
        o_ref[...] = acc

    return pl.pallas_call(
        body, name=name, grid=(r // tile,),
        in_specs=[pl.BlockSpec((n, tile, LANE), lambda i: (0, i, 0))],
        out_specs=pl.BlockSpec((tile, LANE), lambda i: (i, 0)),
        out_shape=jax.ShapeDtypeStruct((r, LANE), F32),
        compiler_params=_cparams("parallel"),
    )(a)


def f_premod(rv, cv):
    (x,), (w, scale, shift) = rv, cv
    y = x * lax.rsqrt(jnp.mean(x * x, -1, keepdims=True) + EPS) * w
    return [y * (1.0 + scale) + shift]


def f_postres(rv, cv):
    (x, y), (w, gate) = rv, cv
    return [x + gate * (y * lax.rsqrt(jnp.mean(y * y, -1, keepdims=True) + EPS) * w)]


def f_adamw(rv, cv):
    w, g, m, v = rv
    m = B1 * m + (1.0 - B1) * g
    v = B2 * v + (1.0 - B2) * jnp.square(g)
    m_hat = m / (1.0 - B1 ** STEP)
    v_hat = v / (1.0 - B2 ** STEP)
    return [-LR * (m_hat / (jnp.sqrt(v_hat) + AEPS) + WD * w), m, v], []


def adamw(name, w, g, m, v):
    r, wd = w.shape
    tile = r
    while tile * wd * 4 * 14 > 24 * 1024 * 1024 and tile % 16 == 0:
        tile //= 2
    (d, m2, v2), _ = rowmap(name, f_adamw, [(a, 0, wd) for a in (w, g, m, v)], [], [(wd, F32)] * 3, tile=tile)
    return d, m2, v2


_IN_SIZES = (512, 512, 512, 512, 4, 4, 512, 1024, 8, 512, 128, 128, 512, 512, 512)


def _relayout_w_in(w):
    d = w.shape[0]
    s = np.cumsum((0,) + _IN_SIZES)
    gq, gk, gv, gz, gb, ga, sz, sx, sdt, aq, ak, av, cb, cc, ch = [w[:, s[i]:s[i + 1]] for i in range(15)]

    def rope_pad(t, nh):
        return jnp.pad(t.reshape(d, nh, 2, 32), ((0, 0), (0, 0), (0, 0), (0, 32))).reshape(d, nh * LANE)

    def v_pad(t, nh):
        return jnp.pad(t.reshape(d, nh, 64), ((0, 0), (0, 0), (0, 64))).reshape(d, nh * LANE)

    def pad128(t):
        return jnp.pad(t, ((0, 0), (0, LANE - t.shape[1])))

    return jnp.concatenate([gq, gk, gv, cb, cc, ch, sx, rope_pad(aq, 8), gz, sz, rope_pad(ak, 2), v_pad(av, 2),
                            pad128(jnp.concatenate([gb, ga], axis=1)), pad128(sdt)], axis=1)


def _unlayout_g_in(g):
    d = g.shape[0]

    def rope_unpad(t, nh):
        return t.reshape(d, nh, 2, 64)[..., :32].reshape(d, nh * 64)

    def v_unpad(t, nh):
        return t.reshape(d, nh, LANE)[..., :64].reshape(d, nh * 64)

    return jnp.concatenate([g[:, 0:1536], g[:, O_GZ:O_GZ + 512], g[:, O_GBA:O_GBA + 8], g[:, O_SZ:O_SZ + 512],
                            g[:, O_XBC:O_XBC + 1024], g[:, O_SDT:O_SDT + 8], rope_unpad(g[:, O_AQ:O_AQ + 1024], 8),
                            rope_unpad(g[:, O_AK:O_AK + 256], 2), v_unpad(g[:, O_AV:O_AV + 256], 2),
                            g[:, O_CB:O_CB + 1536]], axis=1)


def _relayout_w_out(w):
    d = w.shape[1]
    yc = jnp.pad(w[1024:1536].reshape(8, 64, d), ((0, 0), (0, 64), (0, 0))).reshape(1024, d)
    return jnp.concatenate([w[:1024], yc, w[1536:]], axis=0)


def _unlayout_g_out(g):
    d = g.shape[1]
    return jnp.concatenate([g[:1024], g[1024:2048].reshape(8, LANE, d)[:, :64].reshape(512, d), g[2048:]], axis=0)


def _pack(arrs):
    rows, spec, off = [], [], 0
    for a in arrs:
        n = int(np.prod(a.shape))
        nr = -(-n // LANE)
        rows.append(jnp.pad(a.reshape(-1).astype(F32), (0, nr * LANE - n)).reshape(nr, LANE))
        spec.append((off, a.shape))
        off += nr
    pad = (-off) % 8
    if pad:
        rows.append(jnp.zeros((pad, LANE), F32))
    return jnp.concatenate(rows, axis=0), spec


def _unpack(buf, spec):
    out = []
    for off, shape in spec:
        n = int(np.prod(shape))
        nr = -(-n // LANE)
        out.append(buf[off:off + nr].reshape(-1)[:n].reshape(shape))
    return out


def _row128(v):
    return jnp.pad(v.astype(F32), (0, LANE - v.shape[0])).reshape(1, LANE)


BIG = ("w_in", "w_out", "w_up", "w_down")
SMALL_REPL = ("ada_b", "norm_pre_mix", "norm_post_mix", "norm_pre_mlp", "norm_post_mlp", "gdn_a_log", "gdn_dt_bias",
              "gdn_norm_w", "ssm_conv_b", "ssm_a_log", "ssm_dt_bias", "ssm_d", "ssm_norm_w", "attn_sinks")
SMALL_SHARD = ("gdn_conv_w", "ssm_conv_w", "sc_conv_w")
WEIGHTS = ("ada_w", "ada_b", "norm_pre_mix", "norm_post_mix", "norm_pre_mlp", "norm_post_mlp", "w_in", "w_out",
           "gdn_conv_w", "gdn_a_log", "gdn_dt_bias", "gdn_norm_w", "ssm_conv_w", "ssm_conv_b", "ssm_a_log",
           "ssm_dt_bias", "ssm_d", "ssm_norm_w", "attn_sinks", "sc_conv_w", "w_up", "w_down")


def _flat_rows(a):
    return a.reshape(-1, LANE)


def _step(p, m, v, x, c, positions, loss_target):
    t, d = x.shape[1], x.shape[2]
    hid = p["w_up"].shape[2] * 4
    xi, yi, ci = _me()
    chip = 2 * xi + yi
    me = 4 * xi + 2 * yi + ci
    x0 = x[0]
    tgt = loss_target[0]

    buf, spec = _pack([c[0], p["gdn_conv_w"], p["ssm_conv_w"], p["sc_conv_w"]])
    got = allgather8("ag_small_in", buf)
    per_dev = [_unpack(got[k], spec) for k in range(8)]
    c_all = jnp.stack([pd[0] for pd in per_dev], axis=0)
    conv_full = [jnp.concatenate([per_dev[4 * a + 2 * b][1 + j] for a in range(2) for b in range(2)], axis=-1)
                 for j in range(3)]
    gdn_conv_w, ssm_conv_w, sc_conv_w = conv_full

    c_pad = jnp.pad(c_all, ((0, 8), (0, 0)))
    silu_bf = lambda a: _silu(a).astype(BF16)
    nsh = p["ada_w"].shape[2]
    mods = []
    for l in range(DEPTH):
        mo = mm_nn(f"ada_fwd{l}", c_pad, p["ada_w"][l], F32, a_fn=silu_bf)
        mods.append(mo[:8] + lax.dynamic_slice(p["ada_b"][l], (chip * nsh,), (nsh,))[None])
    buf, spec = _pack([jnp.stack(mods, axis=0)])
    got = allgather8("ag_mod", buf)
    mod_sh = [_unpack(got[4 * a + 2 * b], spec)[0] for a in range(2) for b in range(2)]
    mod_all = jnp.concatenate(mod_sh, axis=-1)
    mod = lax.dynamic_index_in_dim(mod_all, me, axis=1, keepdims=False)

    flat = jnp.concatenate([_flat_rows(p[n][l]) for l in range(DEPTH) for n in BIG], axis=0)
    rpad = (-flat.shape[0]) % 32
    flat = jnp.pad(flat, ((0, rpad), (0, 0)))
    (flat_bf,), _ = rowmap("cast_w", lambda rv, cv: ([rv[0]], []), [(flat, 0, LANE)], [], [(LANE, BF16)],
                           tile=_pick(flat.shape[0], 4096, 16))
    gathered = allgather_chips("ag_weights", flat_bf)
    sizes = [int(np.prod(p[n].shape[1:])) // LANE for n in BIG]
    wfull = []
    off = 0
    for l in range(DEPTH):
        ws = {}
        for n, sz in zip(BIG, sizes):
            parts = [gathered[s, off:off + sz].reshape(p[n].shape[1:]) for s in range(4)]
            ws[n] = jnp.concatenate(parts, axis=1 if n in ("w_in", "w_up") else 0)
            off += sz
        w_in_r = _relayout_w_in(ws["w_in"])
        w_out_r = _relayout_w_out(ws["w_out"])
        wfull.append(dict(w_in=w_in_r, w_in_t=w_in_r.T, w_out=w_out_r, w_out_t=w_out_r.T, w_up=ws["w_up"],
                          w_up_t=ws["w_up"].T, w_down=ws["w_down"], w_down_t=ws["w_down"].T))

    inv_freq = ROPE_THETA ** (-jnp.arange(0, 64, 2, dtype=F32) / 64)
    ang = positions[0].astype(F32)[:, None] * inv_freq
    z32 = jnp.zeros_like(ang)
    cos_t = jnp.concatenate([jnp.cos(ang), z32, jnp.cos(ang), z32], axis=1)
    sin_t = jnp.concatenate([-jnp.sin(ang), z32, jnp.sin(ang), z32], axis=1)

    relu2 = lambda a: jnp.square(jnp.maximum(a, 0))

    def layer_consts(l):
        row = lambda v_: v_.reshape(1, -1).astype(F32)
        return dict(
            alog=_row128(p["gdn_a_log"][l]), gdtb=_row128(p["gdn_dt_bias"][l]), gnw=row(p["gdn_norm_w"][l]),
            scb=row(p["ssm_conv_b"][l]), salog=_row128(p["ssm_a_log"][l]), sdtb=_row128(p["ssm_dt_bias"][l]),
            sdsk=_row128(p["ssm_d"][l]), snw=row(p["ssm_norm_w"][l]), sinks=_row128(p["attn_sinks"][l]))

    saved = []
    xc = x0
    for l in range(DEPTH):
        w = wfull[l]
        k = layer_consts(l)
        md = [mod[l, j * d:(j + 1) * d].reshape(1, d) for j in range(6)]
        npre, npost = p["norm_pre_mix"][l].reshape(1, d), p["norm_post_mix"][l].reshape(1, d)
        npre2, npost2 = p["norm_pre_mlp"][l].reshape(1, d), p["norm_post_mlp"][l].reshape(1, d)
        (h,), _ = rowmap(f"premix{l}", lambda rv, cv: (f_premod(rv, cv), []), [(xc, 0, d)], [npre, md[1], md[0]],
                         [(d, BF16)])
        proj = mm_nn(f"inproj{l}", h, w["w_in"], F32)
        cq = conv_fwd(f"gdn_conv{l}", [(proj, O_GQKV, 1536)], gdn_conv_w[l])
        ya, gs = gdn_fwd(f"gdn{l}", cq, proj, k["alog"], k["gdtb"], k["gnw"])
        cx = conv_fwd(f"ssm_conv{l}", [(proj, O_XBC, 1024)], ssm_conv_w[l])
        yb, ss = ssd_fwd(f"ssd{l}", cx, proj, k["scb"], k["salog"], k["sdtb"], k["sdsk"], k["snw"])
        yc = swa_fwd(f"swa{l}", proj, cos_t, sin_t, k["sinks"])
        cd = conv_fwd(f"sc_conv{l}", [(proj, O_CC, GW), (proj, O_CH, GW)], sc_conv_w[l])
        (ycat,), _ = rowmap(f"ycat{l}", lambda rv, cv: ([jnp.concatenate([rv[0], rv[1], rv[2], rv[3] * rv[4]], 1)], []),
                            [(ya, 0, GW), (yb, 0, GW), (yc, 0, 1024), (proj, O_CB, GW), (cd, 0, GW)], [],
                            [(YCAT, BF16)])
        y = mm_nn(f"outproj{l}", ycat, w["w_out"], F32)
        (x1,), _ = rowmap(f"postmix{l}", lambda rv, cv: (f_postres(rv, cv), []), [(xc, 0, d), (y, 0, d)],
                          [npost, md[2]], [(d, F32)])
        (h2,), _ = rowmap(f"premlp{l}", lambda rv, cv: (f_premod(rv, cv), []), [(x1, 0, d)], [npre2, md[4], md[3]],
                          [(d, BF16)])
        u = mm_nn(f"up{l}", h2, w["w_up"], BF16)
        y2 = mm_nn(f"down{l}", u, w["w_down"], F32, a_fn=relu2)
        (x2,), _ = rowmap(f"postmlp{l}", lambda rv, cv: (f_postres(rv, cv), []), [(x1, 0, d), (y2, 0, d)],
                          [npost2, md[5]], [(d, F32)])
        saved.append(dict(x=xc, h=h, proj=proj, cq=cq, gs=gs, cx=cx, ss=ss, cd=cd, ycat=ycat, y=y, x1=x1, h2=h2, u=u,
                          y2=y2, md=md, k=k))
        xc = x2

    def f_loss(rv, cv):
        err = rv[0] - rv[1]
        part = 0.5 * jnp.sum(jnp.mean(err * err, axis=-1, keepdims=True), axis=0, keepdims=True)
        return [err * (1.0 / d)], [jnp.broadcast_to(part, (1, LANE))]

    (dx,), (loss_part,) = rowmap("loss", f_loss, [(xc, 0, d), (tgt, 0, d)], [], [(d, F32)], [(1, LANE)])

    gsmall = [None] * DEPTH
    gbig = [None] * DEPTH
    dmods = [None] * DEPTH
    for l in reversed(range(DEPTH)):
        w, s = wfull[l], saved[l]
        k, md = s["k"], s["md"]
        npre, npost = p["norm_pre_mix"][l].reshape(1, d), p["norm_post_mix"][l].reshape(1, d)
        npre2, npost2 = p["norm_pre_mlp"][l].reshape(1, d), p["norm_post_mlp"][l].reshape(1, d)
        (dx1a, dy2), (dnpost2, dgate_m) = rowmap_vjp(f"postmlp_b{l}", f_postres, [(s["x1"], 0, d), (s["y2"], 0, d)],
                                                      [npost2, md[5]], [(dx, 0, d)], 2, [F32, BF16])
        du = mm_nn(f"down_b{l}", dy2, w["w_down_t"], BF16, extra=s["u"],
                   epi=lambda r, uu: r * (2.0 * jnp.maximum(uu.astype(F32), 0.0)))
        g_down = mm_tn(f"down_g{l}", s["u"], dy2, a_fn=relu2)
        dh2 = mm_nn(f"up_b{l}", du, w["w_up_t"], F32)
        g_up = mm_tn(f"up_g{l}", s["h2"], du)
        (dx1,), (dnpre2, dscale_m, dshift_m) = rowmap_vjp(f"premlp_b{l}", f_premod, [(s["x1"], 0, d)],
                                                          [npre2, md[4], md[3]], [(dh2, 0, d)], 1, [F32],
                                                          add_rows=[(dx1a, 0, d)])
        (dxa, dy), (dnpost, dgate_a) = rowmap_vjp(f"postmix_b{l}", f_postres, [(s["x"], 0, d), (s["y"], 0, d)],
                                                  [npost, md[2]], [(dx1, 0, d)], 2, [F32, BF16])
        dycat = mm_nn(f"outproj_b{l}", dy, w["w_out_t"], F32)
        g_out = mm_tn(f"outproj_g{l}", s["ycat"], dy)
        (dcb, dcd), _ = rowmap(f"scgate_b{l}", lambda rv, cv: ([rv[0] * rv[2], rv[0] * rv[1]], []),
                               [(dycat, 2048, GW), (s["proj"], O_CB, GW), (s["cd"], 0, GW)], [],
                               [(GW, BF16), (GW, F32)])
        (dcc, dch), g_scw = conv_bwd(f"sc_conv_b{l}", dcd, [(s["proj"], O_CC, GW), (s["proj"], O_CH, GW)],
                                     sc_conv_w[l], BF16)
        dyc = dycat[:, 1024:2048]
        daq, dak, dav, g_sinks = swa_bwd(f"swa_b{l}", dyc, s["proj"], cos_t, sin_t, k["sinks"])
        dyb = dycat[:, 512:1024]
        dcx, dsz, dsdt, g_scb, g_salog, g_sdtb, g_sdsk, g_snw = ssd_bwd(
            f"ssd_b{l}", dyb, s["cx"], s["proj"], s["ss"], k["scb"], k["salog"], k["sdtb"], k["sdsk"], k["snw"])
        (dxbc,), g_ssmw = conv_bwd(f"ssm_conv_b{l}", dcx, [(s["proj"], O_XBC, 1024)], ssm_conv_w[l], BF16)
        dya = dycat[:, 0:512]
        dcq, dgz, dgba, g_alog, g_gdtb, g_gnw = gdn_bwd(f"gdn_b{l}", dya, s["cq"], s["proj"], s["gs"], k["alog"],
                                                        k["gdtb"], k["gnw"])
        (dgqkv,), g_gdnw = conv_bwd(f"gdn_conv_b{l}", dcq, [(s["proj"], O_GQKV, 1536)], gdn_conv_w[l], BF16)
        dproj = jnp.concatenate([dgqkv, dcb, dcc, dch, dxbc, daq, dgz, dsz, dak, dav, dgba, dsdt], axis=1)
        dh = mm_nn(f"inproj_b{l}", dproj, w["w_in_t"], F32)
        g_in = mm_tn(f"inproj_g{l}", s["h"], dproj)
        (dx,), (dnpre, dscale_a, dshift_a) = rowmap_vjp(f"premix_b{l}", f_premod, [(s["x"], 0, d)],
                                                        [npre, md[1], md[0]], [(dh, 0, d)], 1, [F32],
                                                        add_rows=[(dxa, 0, d)])
        dmods[l] = jnp.concatenate([dshift_a, dscale_a, dgate_a, dshift_m, dscale_m, dgate_m], axis=1)[0]
        gsmall[l] = dict(
            norm_pre_mix=dnpre[0], norm_post_mix=dnpost[0], norm_pre_mlp=dnpre2[0], norm_post_mlp=dnpost2[0],
            gdn_a_log=g_alog[0, :4], gdn_dt_bias=g_gdtb[0, :4], gdn_norm_w=g_gnw[0], ssm_conv_b=g_scb[0],
            ssm_a_log=g_salog[0, :8], ssm_dt_bias=g_sdtb[0, :8], ssm_d=g_sdsk[0, :8], ssm_norm_w=g_snw[0],
            attn_sinks=g_sinks[0, :8], gdn_conv_w=g_gdnw, ssm_conv_w=g_ssmw, sc_conv_w=g_scw)
        gbig[l] = dict(w_in=_unlayout_g_in(g_in), w_out=_unlayout_g_out(g_out), w_up=g_up, w_down=g_down)
    grad_x = dx[None]

    buf, spec = _pack([jnp.stack(dmods, axis=0)])
    got = allgather8("ag_dmod", buf)
    dmod_sum = _unpack(sum_slabs("dmod_sum", got), spec)[0]
    dmod_all = jnp.stack([_unpack(got[kk], spec)[0] for kk in range(8)], axis=1)
    g_ada_w = []
    for l in range(DEPTH):
        dm = lax.dynamic_slice(dmod_all[l], (0, chip * nsh), (8, nsh))
        g_ada_w.append(mm_tn(f"ada_g{l}", c_pad, jnp.pad(dm, ((0, 8), (0, 0))), a_fn=silu_bf))
    g_ada_w = jnp.stack(g_ada_w, axis=0)

    names = [n for n in SMALL_REPL if n != "ada_b"] + list(SMALL_SHARD)
    buf, spec = _pack([jnp.stack([gsmall[l][n] for l in range(DEPTH)], axis=0) for n in names] + [loss_part])
    tot = _unpack(sum_slabs("small_sum", allgather8("ag_small_g", buf)), spec)
    gfull = dict(zip(names, tot[:-1]))
    loss = tot[-1][0, 0]
    grads = {"ada_w": g_ada_w, "ada_b": dmod_sum}
    for n in SMALL_REPL:
        if n != "ada_b":
            grads[n] = gfull[n]
    for n in SMALL_SHARD:
        wdt = p[n].shape[2]
        grads[n] = lax.dynamic_slice_in_dim(gfull[n], chip * wdt, wdt, axis=2)

    def shard_stack(n, g):
        if n in ("w_in", "w_up"):
            parts = jnp.split(g, 4, axis=1)
        else:
            parts = jnp.split(g, 4, axis=0)
        return jnp.stack([_flat_rows(q) for q in parts], axis=0)

    gflat = jnp.concatenate([shard_stack(n, gbig[l][n]) for l in range(DEPTH) for n in BIG], axis=1)
    gflat = jnp.pad(gflat, ((0, 0), (0, rpad), (0, 0)))
    r = gflat.shape[1]
    g4 = gflat.reshape(4, 2, r // 2, LANE)
    recv1 = sibling_swap_halves("rs_sibling", g4)
    part = add_half("rs_add1", g4, recv1, jnp.reshape(ci, (1,)).astype(jnp.int32))
    recv2 = chips_exchange("rs_chips", part)
    mine = sum_slabs("rs_add2", recv2)
    gsh = sibling_join("rs_join", mine).reshape(r, LANE)
    off = 0
    for l in range(DEPTH):
        for n, sz in zip(BIG, sizes):
            grads.setdefault(n, []).append(gsh[off:off + sz].reshape(p[n].shape[1:]))
            off += sz
    for n in BIG:
        grads[n] = jnp.stack(grads[n], axis=0)

    delta, new_m, new_v = {}, {}, {}
    for n in ("ada_w",) + BIG:
        sh = p[n].shape
        two = lambda a: a.reshape(sh[0] * sh[1], sh[2])
        dl, m2, v2 = adamw(f"adamw_{n}", two(p[n]), two(grads[n]), two(m[n]), two(v[n]))
        delta[n], new_m[n], new_v[n] = dl.reshape(sh), m2.reshape(sh), v2.reshape(sh)
    small = list(SMALL_REPL) + list(SMALL_SHARD)
    bufs = []
    for src in (p, grads, m, v):
        b_, spec = _pack([src[n] for n in small])
        bufs.append(b_)
    dl, m2, v2 = adamw("adamw_small", *bufs)
    for n, a, b_, cc_ in zip(small, _unpack(dl, spec), _unpack(m2, spec), _unpack(v2, spec)):
        delta[n], new_m[n], new_v[n] = a, b_, cc_

    return (loss, grad_x, *[grads[n] for n in WEIGHTS], *[delta[n] for n in WEIGHTS], *[new_m[n] for n in WEIGHTS],
            *[new_v[n] for n in WEIGHTS])


def kernel(x, c, positions, ada_w, ada_b, norm_pre_mix, norm_post_mix, norm_pre_mlp, norm_post_mlp, w_in, w_out, gdn_conv_w, gdn_a_log, gdn_dt_bias, gdn_norm_w, ssm_conv_w, ssm_conv_b, ssm_a_log, ssm_dt_bias, ssm_d, ssm_norm_w, attn_sinks, sc_conv_w, w_up, w_down, loss_target, m_ada_w, m_ada_b, m_norm_pre_mix, m_norm_post_mix, m_norm_pre_mlp, m_norm_post_mlp, m_w_in, m_w_out, m_gdn_conv_w, m_gdn_a_log, m_gdn_dt_bias, m_gdn_norm_w, m_ssm_conv_w, m_ssm_conv_b, m_ssm_a_log, m_ssm_dt_bias, m_ssm_d, m_ssm_norm_w, m_attn_sinks, m_sc_conv_w, m_w_up, m_w_down, v_ada_w, v_ada_b, v_norm_pre_mix, v_norm_post_mix, v_norm_pre_mlp, v_norm_post_mlp, v_w_in, v_w_out, v_gdn_conv_w, v_gdn_a_log, v_gdn_dt_bias, v_gdn_norm_w, v_ssm_conv_w, v_ssm_conv_b, v_ssm_a_log, v_ssm_dt_bias, v_ssm_d, v_ssm_norm_w, v_attn_sinks, v_sc_conv_w, v_w_up, v_w_down):
    loc = locals()
    p = {n: loc[n] for n in WEIGHTS}
    m = {n: loc["m_" + n] for n in WEIGHTS}
    v = {n: loc["v_" + n] for n in WEIGHTS}
    return _step(p, m, v, x, c, positions, loss_target)
```

```python
import functools

import numpy as np
import jax
import jax.numpy as jnp
from jax import lax
from jax.experimental import pallas as pl
from jax.experimental.pallas import tpu as pltpu

F32, BF16 = jnp.float32, jnp.bfloat16
MESH = pl.DeviceIdType.MESH

DEPTH = 2
GW = 512
N_GDN, GDN_CS = 4, 64
SSD_CS, ATT_W = 128, 128
EPS = 1e-6
ROPE_THETA = 10000.0
IN_WIDTH = 5904
B1, B2, LR, AEPS, WD, STEP = 0.9, 0.999, 0.001, 1e-8, 0.01, 10

VMEM_LIMIT = 52 * 1024 * 1024
LANE = 128

O_GQKV, O_CB, O_CC, O_CH, O_XBC, O_AQ, O_GZ, O_SZ, O_AK, O_AV, O_GBA, O_SDT = (
    0, 1536, 2048, 2560, 3072, 4096, 5120, 5632, 6144, 6400, 6656, 6784)
PW = 6912
YCAT = 2560


def _cparams(*sem):
    return pltpu.CompilerParams(dimension_semantics=sem or None, vmem_limit_bytes=VMEM_LIMIT)


def _pick(n, cap, mult):
    if n <= cap:
        return n
    best = None
    for d in range(mult, cap + 1, mult):
        if n % d == 0:
            best = d
    assert best is not None, (n, cap, mult)
    return best


def _pieces(x, n):
    out, r = [], x
    for i in range(n):
        h = r.astype(BF16)
        out.append(h)
        if i < n - 1:
            r = r - h.astype(F32)
    return out


def _dgp(a, b, ca, cb, na, nb):
    dn = (((ca,), (cb,)), ((), ()))
    acc = None
    for i, pa in enumerate(_pieces(a, na)):
        for j, pb in enumerate(_pieces(b, nb)):
            if i + j < max(na, nb):
                t = lax.dot_general(pa, pb, dn, preferred_element_type=F32)
                acc = t if acc is None else acc + t
    return acc


_CONTRACT = {"nn": (1, 0), "nt": (1, 1), "tn": (0, 0)}


@functools.lru_cache(maxsize=None)
def _mm(kind, na, nb, const_a=False):
    ca, cb = _CONTRACT[kind]
    ng = max(na, nb)

    @jax.custom_vjp
    def f(a, b):
        return _dgp(a, b, ca, cb, na, nb)

    def bwd(r, g):
        a, b = r
        if kind == "nn":
            da, db = (lambda: _mm("nt", ng, nb)(g, b)), _mm("tn", na, ng, const_a)(a, g)
        elif kind == "nt":
            da, db = (lambda: _mm("nn", ng, nb)(g, b)), _mm("tn", ng, na)(g, a)
        else:
            da, db = (lambda: _mm("nt", nb, ng)(b, g)), _mm("nn", na, ng, const_a)(a, g)
        return (jnp.zeros_like(a) if const_a else da()), db

    f.defvjp(lambda a, b: (f(a, b), (a, b)), bwd)
    return f


_nn, _nt, _tn = _mm("nn", 1, 1), _mm("nt", 1, 1), _mm("tn", 1, 1)
_hnn = _mm("nn", 2, 2)
_xnn, _xnt = _mm("nn", 1, 3, True), _mm("nt", 1, 3, True)


def _silu(x):
    return x * jax.nn.sigmoid(x)


def _softplus(x):
    return jnp.maximum(x, 0.0) + jnp.log1p(jnp.exp(-jnp.abs(x)))


def _colsel(v, idx):
    lane = lax.broadcasted_iota(jnp.int32, v.shape, 1)
    return jnp.sum(jnp.where(lane == idx, v, 0.0), axis=1, keepdims=True)


def _tri(n):
    r = lax.broadcasted_iota(jnp.int32, (n, n), 0)
    c = lax.broadcasted_iota(jnp.int32, (n, n), 1)
    return r >= c, r > c


def _tri_inv(m, n):
    r = lax.broadcasted_iota(jnp.int32, (n, n), 0)
    c = lax.broadcasted_iota(jnp.int32, (n, n), 1)
    x = jnp.where(r == c, 1.0, 0.0) - m
    p = _hnn(m, m)
    steps = int(np.log2(n)) - 1
    for s in range(steps):
        x = x + _hnn(x, p)
        if s < steps - 1:
            p = _hnn(p, p)
    return x


def _gdn_chunk(cq, ck, cv, z, ba, s, alog, dtb, nw, *, h):
    n = GDN_CS
    incl, strict = _tri(n)
    ltri = incl.astype(F32)
    q, k, v = _silu(cq), _silu(ck), _silu(cv)
    q = q * lax.rsqrt(jnp.sum(q * q, -1, keepdims=True) + EPS) * (128 ** -0.5)
    k = k * lax.rsqrt(jnp.sum(k * k, -1, keepdims=True) + EPS)
    beta = jax.nn.sigmoid(_colsel(ba, h))
    g = -jnp.exp(_colsel(alog, h)) * _softplus(_colsel(ba, N_GDN + h) + _colsel(dtb, h))
    g128 = jnp.broadcast_to(g, (n, LANE))
    gcb = _xnn(ltri, g128)
    gci = _xnn(ltri, jnp.broadcast_to(g, (n, n)))
    e0 = (lax.broadcasted_iota(jnp.int32, (n, LANE), 1) == 0).astype(F32)
    gcj = _xnt(e0, gcb)
    dec = jnp.where(incl, jnp.exp(jnp.where(incl, gci - gcj, 0.0)), 0.0)
    kb = k * beta
    m = jnp.where(strict, _nt(kb, k) * dec, 0.0)
    tinv = _tri_inv(m, n)
    egc = jnp.exp(gcb)
    u = _nn(tinv, v * beta)
    w = _nn(tinv, kb * egc)
    attn = jnp.where(incl, _nt(q, k) * dec, 0.0)
    gl = jnp.sum(g, axis=0, keepdims=True)
    kd = k * jnp.exp(gl - gcb)
    cd = jnp.exp(gl)
    v_new = u - _nn(w, s)
    o = _nn(q * egc, s) + _nn(attn, v_new)
    s_new = s * cd + _tn(kd, v_new)
    y = o * lax.rsqrt(jnp.mean(o * o, -1, keepdims=True) + EPS) * nw * _silu(z)
    return y, s_new


def _ssd_chunk(cxs, cb_, cc_, z, dtc, s0, s1, bx, bb, bc, alog, dtb, dsk, nw, *, g):
    n = SSD_CS
    incl, _ = _tri(n)
    ltri = incl.astype(F32)
    xs, bm, cm = _silu(cxs + bx), _silu(cb_ + bb), _silu(cc_ + bc)
    dt_all = _softplus(dtc + dtb)
    da_all = dt_all * (-jnp.exp(alog))
    cbm = _nt(cm, bm)
    e0 = (lax.broadcasted_iota(jnp.int32, (n, LANE), 1) == 0).astype(F32)
    lane_lo = lax.broadcasted_iota(jnp.int32, (n, LANE), 1) < 64
    ys, s_out = [], []
    for p, s_p in enumerate((s0, s1)):
        xp = xs[:, p * LANE:(p + 1) * LANE]
        halves = []
        for e in range(2):
            h = 4 * g + 2 * p + e
            dt_h = _colsel(dt_all, h)
            da_h = _colsel(da_all, h)
            cs = _xnn(ltri, jnp.broadcast_to(da_h, (n, LANE)))
            csj = _xnt(e0, cs)
            lm = jnp.where(incl, jnp.exp(jnp.where(incl, cs - csj, 0.0)), 0.0)
            tot = jnp.sum(da_h, axis=0, keepdims=True)
            xdt = xp * dt_h
            y_h = _nn(cbm * lm, xdt) + _nn(cm * jnp.exp(cs), s_p) + xp * _colsel(dsk, h)
            s_h = s_p * jnp.exp(tot) + _tn(bm * jnp.exp(tot - cs), xdt)
            halves.append((y_h, s_h))
        ys.append(jnp.where(lane_lo, halves[0][0], halves[1][0]))
        s_out.append(jnp.where(lane_lo, halves[0][1], halves[1][1]))
    y = jnp.concatenate(ys, axis=1) * _silu(z)
    y = y * lax.rsqrt(jnp.mean(y * y, -1, keepdims=True) + EPS) * nw
    return y, s_out[0], s_out[1]


def _swa_block(qg, kp, kc, vp, vc, sink, notfirst, *, g):
    w = ATT_W
    kb = jnp.concatenate([kp, kc], axis=0)
    vb = jnp.concatenate([vp, vc], axis=0)
    qi = lax.broadcasted_iota(jnp.int32, (w, 2 * w), 0)
    kj = lax.broadcasted_iota(jnp.int32, (w, 2 * w), 1)
    rel = qi + w - kj
    mask = (rel >= 0) & (rel < w) & ((kj >= w) | (notfirst > 0.5))
    outs = []
    for r in range(4):
        q = qg[:, r * LANE:(r + 1) * LANE]
        s = _nt(q, kb) * (64 ** -0.5)
        s = jnp.where(mask, s, -1e30)
        snk = _colsel(sink, 4 * g + r)
        mx = lax.stop_gradient(jnp.maximum(jnp.max(s, axis=-1, keepdims=True), snk))
        p = jnp.where(mask, jnp.exp(s - mx), 0.0)
        p = p / (jnp.sum(p, axis=-1, keepdims=True) + jnp.exp(snk - mx))
        outs.append(_nn(p, vb))
    return jnp.concatenate(outs, axis=1)


def _rope(x, cos, sin):
    return x * cos + pltpu.roll(x, 64, 1) * sin


def _unrope(dy, cos, sin):
    return dy * cos + pltpu.roll(dy * sin, 64, 1)


def rowmap(name, fn, rows, consts, outs, accs=(), tile=256):
    t = rows[0][0].shape[0]
    tile = min(tile, t)
    assert t % tile == 0
    nr, nc, no, na = len(rows), len(consts), len(outs), len(accs)

    def body(*refs):
        rv = [r[...] for r in refs[:nr]]
        cv = [r[...] for r in refs[nr:nr + nc]]
        ov, av = fn(rv, cv)
        for ref, o in zip(refs[nr + nc:nr + nc + no], ov):
            ref[...] = o.astype(ref.dtype)
        if na:
            a_refs = refs[nr + nc + no:]

            @pl.when(pl.program_id(0) == 0)
            def _():
                for ref in a_refs:
                    ref[...] = jnp.zeros_like(ref)

            for ref, a in zip(a_refs, av):
                ref[...] += a

    def cspec(off, width):
        assert off % width == 0
        cb = off // width
        return pl.BlockSpec((tile, width), lambda i: (i, cb))

    in_specs = [cspec(off, wd) for (_, off, wd) in rows]
    in_specs += [pl.BlockSpec(c.shape, lambda i: (0, 0)) for c in consts]
    out_specs = [pl.BlockSpec((tile, wd), lambda i: (i, 0)) for (wd, _) in outs]
    out_specs += [pl.BlockSpec(s, lambda i: (0, 0)) for s in accs]
    out_shape = [jax.ShapeDtypeStruct((t, wd), dt) for (wd, dt) in outs]
    out_shape += [jax.ShapeDtypeStruct(s, F32) for s in accs]
    res = pl.pallas_call(
        body, name=name, grid=(t // tile,), in_specs=in_specs, out_specs=out_specs, out_shape=out_shape,
        compiler_params=_cparams("arbitrary"),
    )(*[r[0] for r in rows], *consts)
    return res[:no], res[no:]


def rowmap_vjp(name, f, rows, consts, douts, n_diff, drow_dtypes, tile=256, add_rows=()):
    nr, nd, nadd = len(rows), len(douts), len(add_rows)

    def g(vals, cvals):
        rv, dov, addv = vals[:nr], vals[nr:nr + nd], vals[nr + nd:]
        fixed = rv[n_diff:]
        _, vjp = jax.vjp(lambda r, c: f(list(r) + list(fixed), c), rv[:n_diff], cvals)
        dr, dc = vjp([d.astype(F32) for d in dov])
        dr = list(dr)
        for a in addv:
            dr[0] = dr[0] + a
        return dr, dc

    outs = [(rows[i][2], drow_dtypes[i]) for i in range(n_diff)]
    accs = [c.shape for c in consts]
    return rowmap(name, g, list(rows) + list(douts) + list(add_rows), consts, outs, accs, tile=tile)


def mm_nn(name, a, w, out_dtype, a_fn=None, extra=None, epi=None, tm_cap=512, wt=False):
    m, k = a.shape
    stacked = w.ndim == 3
    n = (w.shape[-2] if wt else w.shape[-1] * (w.shape[0] if stacked else 1))
    slab = w.shape[-1]
    tm = _pick(m, tm_cap, 16)
    tn = _pick(n if wt else slab, 1024, LANE)
    tk = _pick(slab if wt else k, 2048, LANE)
    nk = k // tk
    per = slab // (tk if wt else tn)
    wdims = ((1,), (1,)) if wt else ((1,), (0,))

    def body(*refs):
        a_ref, w_ref = refs[0], refs[1]
        e_ref = refs[2] if extra is not None else None
        o_ref, acc = refs[-2], refs[-1]
        kk = pl.program_id(2)

        @pl.when(kk == 0)
        def _():
            acc[...] = jnp.zeros_like(acc)

        av = a_ref[...]
        if a_fn is not None:
            av = a_fn(av)
        acc[...] += lax.dot_general(av.astype(BF16), w_ref[...].astype(BF16), (wdims, ((), ())),
                                    preferred_element_type=F32)

        @pl.when(kk == nk - 1)
        def _():
            r = acc[...]
            if epi is not None:
                r = epi(r, e_ref[...])
            o_ref[...] = r.astype(o_ref.dtype)

    if wt:
        w_spec = (pl.BlockSpec((None, tn, tk), lambda i, j, kk: (kk // per, j, kk % per)) if stacked
                  else pl.BlockSpec((tn, tk), lambda i, j, kk: (j, kk)))
    else:
        w_spec = (pl.BlockSpec((None, tk, tn), lambda i, j, kk: (j // per, kk, j % per)) if stacked
                  else pl.BlockSpec((tk, tn), lambda i, j, kk: (kk, j)))
    in_specs = [pl.BlockSpec((tm, tk), lambda i, j, kk: (i, kk)), w_spec]
    ops = [a, w]
    if extra is not None:
        in_specs.append(pl.BlockSpec((tm, tn), lambda i, j, kk: (i, j)))
        ops.append(extra)
    return pl.pallas_call(
        body, name=name, grid=(m // tm, n // tn, nk), in_specs=in_specs,
        out_specs=pl.BlockSpec((tm, tn), lambda i, j, kk: (i, j)),
        out_shape=jax.ShapeDtypeStruct((m, n), out_dtype),
        scratch_shapes=[pltpu.VMEM((tm, tn), F32)],
        compiler_params=_cparams("parallel", "parallel", "arbitrary"),
    )(*ops)


def mm_tn(name, a, b, a_fn=None, col_slabs=1):
    t, m = a.shape
    n = b.shape[1]
    tm, tn, tk = _pick(m, 1024, LANE), _pick(n // col_slabs, 1024, LANE), _pick(t, 1024, 16)
    nk = t // tk
    per = n // col_slabs // tn

    def body(a_ref, b_ref, o_ref, acc):
        kk = pl.program_id(2)

        @pl.when(kk == 0)
        def _():
            acc[...] = jnp.zeros_like(acc)

        av = a_ref[...]
        if a_fn is not None:
            av = a_fn(av)
        acc[...] += lax.dot_general(av.astype(BF16), b_ref[...].astype(BF16), (((0,), (0,)), ((), ())),
                                    preferred_element_type=F32)

        @pl.when(kk == nk - 1)
        def _():
            o_ref[...] = acc[...]

    return pl.pallas_call(
        body, name=name, grid=(m // tm, n // tn, nk),
        in_specs=[pl.BlockSpec((tk, tm), lambda i, j, kk: (kk, i)), pl.BlockSpec((tk, tn), lambda i, j, kk: (kk, j))],
        out_specs=(pl.BlockSpec((tm, tn), lambda i, j, kk: (i, j)) if col_slabs == 1
                   else pl.BlockSpec((None, tm, tn), lambda i, j, kk: (j // per, i, j % per))),
        out_shape=jax.ShapeDtypeStruct((m, n) if col_slabs == 1 else (col_slabs, m, n // col_slabs), F32),
        scratch_shapes=[pltpu.VMEM((tm, tn), F32)],
        compiler_params=_cparams("parallel", "parallel", "arbitrary"),
    )(a, b)


CONV_TILE = 256
HALO = 8


def conv_fwd(name, xs, w):
    t = xs[0][0].shape[0]
    c = xs[0][2]
    kw = w.shape[0]
    tt = min(CONV_TILE, t)
    nx = len(xs)

    def body(*refs):
        cur_refs, prev_refs, w_ref, y_ref = refs[:nx], refs[nx:2 * nx], refs[2 * nx], refs[2 * nx + 1]
        i = pl.program_id(0)
        cur, prev = cur_refs[0][...], prev_refs[0][...]
        for r, p in zip(cur_refs[1:], prev_refs[1:]):
            cur, prev = cur * r[...], prev * p[...]
        prev = jnp.where(i > 0, prev, 0.0)
        xx = jnp.concatenate([prev, cur], axis=0)
        acc = jnp.zeros((tt, c), F32)
        for k in range(kw):
            sh = kw - 1 - k
            rolled = pltpu.roll(xx, sh, 0) if sh else xx
            acc = acc + rolled[HALO:] * w_ref[k:k + 1, :]
        y_ref[...] = acc

    def cur_spec(off):
        cb = off // c
        return pl.BlockSpec((tt, c), lambda i: (i, cb))

    def prev_spec(off):
        cb = off // c
        return pl.BlockSpec((HALO, c), lambda i: (jnp.maximum(i * (tt // HALO) - 1, 0), cb))

    for (_, off, wd) in xs:
        assert wd == c and off % c == 0
    return pl.pallas_call(
        body, name=name, grid=(t // tt,),
        in_specs=[cur_spec(off) for (_, off, _) in xs] + [prev_spec(off) for (_, off, _) in xs]
        + [pl.BlockSpec(w.shape, lambda i: (0, 0))],
        out_specs=pl.BlockSpec((tt, c), lambda i: (i, 0)),
        out_shape=jax.ShapeDtypeStruct((t, c), F32),
        compiler_params=_cparams("arbitrary"),
    )(*[x[0] for x in xs], *[x[0] for x in xs], w)


def conv_bwd(name, dy, xs, w, dx_dtype):
    t = xs[0][0].shape[0]
    c = xs[0][2]
    kw = w.shape[0]
    tt = min(CONV_TILE, t)
    nt = t // tt
    nx = len(xs)
    n = tt + HALO

    def body(*refs):
        dy_ref = refs[0]
        cur_refs, prev_refs = refs[1:1 + nx], refs[1 + nx:1 + 2 * nx]
        w_ref = refs[1 + 2 * nx]
        dx_refs = refs[2 + 2 * nx:2 + 3 * nx]
        dw_ref, carry = refs[2 + 3 * nx], refs[3 + 3 * nx]
        i = pl.program_id(0)

        @pl.when(i == 0)
        def _():
            carry[...] = jnp.zeros_like(carry)
            dw_ref[...] = jnp.zeros_like(dw_ref)

        curs = [r[...] for r in cur_refs]
        cur, prev = curs[0], prev_refs[0][...]
        for cv, p in zip(curs[1:], prev_refs[1:]):
            cur, prev = cur * cv, prev * p[...]
        prev = jnp.where(i < nt - 1, prev, 0.0)
        xx = jnp.concatenate([prev, cur], axis=0)
        dyv = dy_ref[...]
        dpad = jnp.concatenate([jnp.zeros((HALO, c), F32), dyv], axis=0)
        dxx = jnp.zeros((n, c), F32)
        dws = []
        for k in range(kw):
            sh = kw - 1 - k
            dxx = dxx + (pltpu.roll(dpad, n - sh, 0) if sh else dpad) * w_ref[k:k + 1, :]
            rolled = pltpu.roll(xx, sh, 0) if sh else xx
            dws.append(jnp.sum(dyv * rolled[HALO:], axis=0, keepdims=True))
        dw_ref[...] += jnp.concatenate(dws, axis=0)
        dcur = jnp.concatenate([dxx[HALO:tt], dxx[tt:] + carry[...]], axis=0)
        carry[...] = dxx[:HALO]
        for j, ref in enumerate(dx_refs):
            d = dcur
            for jj, cv in enumerate(curs):
                if jj != j:
                    d = d * cv
            ref[...] = d.astype(ref.dtype)

    def cur_spec(off):
        cb = off // c
        return pl.BlockSpec((tt, c), lambda i: (nt - 1 - i, cb))

    def prev_spec(off):
        cb = off // c
        return pl.BlockSpec((HALO, c), lambda i: (jnp.maximum((nt - 1 - i) * (tt // HALO) - 1, 0), cb))

    res = pl.pallas_call(
        body, name=name, grid=(nt,),
        in_specs=[pl.BlockSpec((tt, c), lambda i: (nt - 1 - i, 0))] + [cur_spec(off) for (_, off, _) in xs]
        + [prev_spec(off) for (_, off, _) in xs] + [pl.BlockSpec(w.shape, lambda i: (0, 0))],
        out_specs=[pl.BlockSpec((tt, c), lambda i: (nt - 1 - i, 0)) for _ in xs] + [pl.BlockSpec(w.shape, lambda i: (0, 0))],
        out_shape=[jax.ShapeDtypeStruct((t, c), dx_dtype) for _ in xs] + [jax.ShapeDtypeStruct(w.shape, F32)],
        scratch_shapes=[pltpu.VMEM((HALO, c), F32)],
        compiler_params=_cparams("arbitrary"),
    )(dy, *[x[0] for x in xs], *[x[0] for x in xs], w)
    return res[:nx], res[nx]


def _const_spec(a):
    return pl.BlockSpec(a.shape, lambda i: (0,) * a.ndim)


def gdn_fwd(name, cq, proj, alog, dtb, nw):
    t = cq.shape[0]
    nc = t // GDN_CS

    def body(cq_ref, z_ref, ba_ref, alog_ref, dtb_ref, nw_ref, y_ref, ssave_ref, s_scr):
        @pl.when(pl.program_id(0) == 0)
        def _():
            s_scr[...] = jnp.zeros_like(s_scr)

        ba = ba_ref[...]
        for h in range(N_GDN):
            sl = slice(h * LANE, (h + 1) * LANE)
            s = s_scr[h]
            ssave_ref[0, h] = s
            y, s_new = _gdn_chunk(cq_ref[:, sl], cq_ref[:, GW + h * LANE:GW + (h + 1) * LANE],
                                  cq_ref[:, 2 * GW + h * LANE:2 * GW + (h + 1) * LANE], z_ref[:, sl], ba, s,
                                  alog_ref[...], dtb_ref[...], nw_ref[...], h=h)
            y_ref[:, sl] = y
            s_scr[h] = s_new

    return pl.pallas_call(
        body, name=name, grid=(nc,),
        in_specs=[pl.BlockSpec((GDN_CS, 3 * GW), lambda i: (i, 0)),
                  pl.BlockSpec((GDN_CS, GW), lambda i: (i, O_GZ // GW)),
                  pl.BlockSpec((GDN_CS, LANE), lambda i: (i, O_GBA // LANE)),
                  _const_spec(alog), _const_spec(dtb), _const_spec(nw)],
        out_specs=[pl.BlockSpec((GDN_CS, GW), lambda i: (i, 0)),
                   pl.BlockSpec((1, N_GDN, LANE, LANE), lambda i: (i, 0, 0, 0))],
        out_shape=[jax.ShapeDtypeStruct((t, GW), F32), jax.ShapeDtypeStruct((nc, N_GDN, LANE, LANE), F32)],
        scratch_shapes=[pltpu.VMEM((N_GDN, LANE, LANE), F32)],
        compiler_params=_cparams("arbitrary"),
    )(cq, proj, proj, alog, dtb, nw)


def gdn_bwd(name, dy, dy_off, cq, proj, ssave, alog, dtb, nw):
    t = cq.shape[0]
    nc = t // GDN_CS

    def body(dy_ref, cq_ref, z_ref, ba_ref, ssave_ref, alog_ref, dtb_ref, nw_ref,
             dcq_ref, dz_ref, dba_ref, dalog_ref, ddtb_ref, dnw_ref, ds_scr):
        @pl.when(pl.program_id(0) == 0)
        def _():
            ds_scr[...] = jnp.zeros_like(ds_scr)
            dalog_ref[...] = jnp.zeros_like(dalog_ref)
            ddtb_ref[...] = jnp.zeros_like(ddtb_ref)
            dnw_ref[...] = jnp.zeros_like(dnw_ref)

        ba = ba_ref[...]
        dba = jnp.zeros_like(ba)
        for h in range(N_GDN):
            sl = slice(h * LANE, (h + 1) * LANE)
            slk = slice(GW + h * LANE, GW + (h + 1) * LANE)
            slv = slice(2 * GW + h * LANE, 2 * GW + (h + 1) * LANE)
            _, vjp = jax.vjp(functools.partial(_gdn_chunk, h=h), cq_ref[:, sl], cq_ref[:, slk], cq_ref[:, slv],
                             z_ref[:, sl], ba, ssave_ref[0, h], alog_ref[...], dtb_ref[...], nw_ref[...])
            dq, dk, dv, dz, dba_h, ds, dal, ddt, dnw = vjp((dy_ref[:, sl], ds_scr[h]))
            dcq_ref[:, sl] = dq
            dcq_ref[:, slk] = dk
            dcq_ref[:, slv] = dv
            dz_ref[:, sl] = dz.astype(dz_ref.dtype)
            ds_scr[h] = ds
            dba = dba + dba_h
            dalog_ref[...] += dal
            ddtb_ref[...] += ddt
            dnw_ref[...] += dnw
        dba_ref[...] = dba.astype(dba_ref.dtype)

    rev = lambda i: nc - 1 - i
    return pl.pallas_call(
        body, name=name, grid=(nc,),
        in_specs=[pl.BlockSpec((GDN_CS, GW), lambda i: (rev(i), dy_off // GW)),
                  pl.BlockSpec((GDN_CS, 3 * GW), lambda i: (rev(i), 0)),
                  pl.BlockSpec((GDN_CS, GW), lambda i: (rev(i), O_GZ // GW)),
                  pl.BlockSpec((GDN_CS, LANE), lambda i: (rev(i), O_GBA // LANE)),
                  pl.BlockSpec((1, N_GDN, LANE, LANE), lambda i: (rev(i), 0, 0, 0)),
                  _const_spec(alog), _const_spec(dtb), _const_spec(nw)],
        out_specs=[pl.BlockSpec((GDN_CS, 3 * GW), lambda i: (rev(i), 0)),
                   pl.BlockSpec((GDN_CS, GW), lambda i: (rev(i), 0)),
                   pl.BlockSpec((GDN_CS, LANE), lambda i: (rev(i), 0)),
                   _const_spec(alog), _const_spec(dtb), _const_spec(nw)],
        out_shape=[jax.ShapeDtypeStruct((t, 3 * GW), F32), jax.ShapeDtypeStruct((t, GW), BF16),
                   jax.ShapeDtypeStruct((t, LANE), BF16), jax.ShapeDtypeStruct(alog.shape, F32),
                   jax.ShapeDtypeStruct(dtb.shape, F32), jax.ShapeDtypeStruct(nw.shape, F32)],
        scratch_shapes=[pltpu.VMEM((N_GDN, LANE, LANE), F32)],
        compiler_params=_cparams("arbitrary"),
    )(dy, cq, proj, proj, ssave, alog, dtb, nw)


def _ssd_args(cx_ref, z_ref, dt_ref, s_pair, cb_ref, alog_ref, dtb_ref, dsk_ref, nw_ref, g):
    return (cx_ref[:, g * 256:(g + 1) * 256], cx_ref[:, GW + g * LANE:GW + (g + 1) * LANE],
            cx_ref[:, GW + 256 + g * LANE:GW + 256 + (g + 1) * LANE], z_ref[:, g * 256:(g + 1) * 256], dt_ref[...],
            s_pair[0], s_pair[1], cb_ref[:, g * 256:(g + 1) * 256], cb_ref[:, GW + g * LANE:GW + (g + 1) * LANE],
            cb_ref[:, GW + 256 + g * LANE:GW + 256 + (g + 1) * LANE], alog_ref[...], dtb_ref[...], dsk_ref[...],
            nw_ref[:, g * 256:(g + 1) * 256])


def ssd_fwd(name, cx, proj, convb, alog, dtb, dsk, nw):
    t = cx.shape[0]
    nc = t // SSD_CS

    def body(cx_ref, z_ref, dt_ref, cb_ref, alog_ref, dtb_ref, dsk_ref, nw_ref, y_ref, ssave_ref, s_scr):
        @pl.when(pl.program_id(0) == 0)
        def _():
            s_scr[...] = jnp.zeros_like(s_scr)

        for g in range(2):
            s_pair = (s_scr[2 * g], s_scr[2 * g + 1])
            ssave_ref[0, 2 * g] = s_pair[0]
            ssave_ref[0, 2 * g + 1] = s_pair[1]
            y, s0, s1 = _ssd_chunk(*_ssd_args(cx_ref, z_ref, dt_ref, s_pair, cb_ref, alog_ref, dtb_ref, dsk_ref,
                                              nw_ref, g), g=g)
            y_ref[:, g * 256:(g + 1) * 256] = y
            s_scr[2 * g] = s0
            s_scr[2 * g + 1] = s1

    return pl.pallas_call(
        body, name=name, grid=(nc,),
        in_specs=[pl.BlockSpec((SSD_CS, 1024), lambda i: (i, 0)),
                  pl.BlockSpec((SSD_CS, GW), lambda i: (i, O_SZ // GW)),
                  pl.BlockSpec((SSD_CS, LANE), lambda i: (i, O_SDT // LANE)),
                  _const_spec(convb), _const_spec(alog), _const_spec(dtb), _const_spec(dsk), _const_spec(nw)],
        out_specs=[pl.BlockSpec((SSD_CS, GW), lambda i: (i, 0)),
                   pl.BlockSpec((1, 4, LANE, LANE), lambda i: (i, 0, 0, 0))],
        out_shape=[jax.ShapeDtypeStruct((t, GW), F32), jax.ShapeDtypeStruct((nc, 4, LANE, LANE), F32)],
        scratch_shapes=[pltpu.VMEM((4, LANE, LANE), F32)],
        compiler_params=_cparams("arbitrary"),
    )(cx, proj, proj, convb, alog, dtb, dsk, nw)


def ssd_bwd(name, dy, dy_off, cx, proj, ssave, convb, alog, dtb, dsk, nw):
    t = cx.shape[0]
    nc = t // SSD_CS

    def body(dy_ref, cx_ref, z_ref, dt_ref, ssave_ref, cb_ref, alog_ref, dtb_ref, dsk_ref, nw_ref,
             dcx_ref, dz_ref, ddt_ref, dcb_ref, dalog_ref, ddtb_ref, ddsk_ref, dnw_ref, ds_scr):
        @pl.when(pl.program_id(0) == 0)
        def _():
            ds_scr[...] = jnp.zeros_like(ds_scr)
            for r in (dcb_ref, dalog_ref, ddtb_ref, ddsk_ref, dnw_ref):
                r[...] = jnp.zeros_like(r)

        ddt = jnp.zeros((SSD_CS, LANE), F32)
        for g in range(2):
            s_pair = (ssave_ref[0, 2 * g], ssave_ref[0, 2 * g + 1])
            _, vjp = jax.vjp(functools.partial(_ssd_chunk, g=g),
                             *_ssd_args(cx_ref, z_ref, dt_ref, s_pair, cb_ref, alog_ref, dtb_ref, dsk_ref, nw_ref, g))
            (dxs, db, dc, dz, ddt_g, ds0, ds1, dbx, dbb, dbc, dal, ddtb, ddsk, dnw) = vjp(
                (dy_ref[:, g * 256:(g + 1) * 256], ds_scr[2 * g], ds_scr[2 * g + 1]))
            sx = slice(g * 256, (g + 1) * 256)
            sb = slice(GW + g * LANE, GW + (g + 1) * LANE)
            sc = slice(GW + 256 + g * LANE, GW + 256 + (g + 1) * LANE)
            dcx_ref[:, sx] = dxs
            dcx_ref[:, sb] = db
            dcx_ref[:, sc] = dc
            dz_ref[:, sx] = dz.astype(dz_ref.dtype)
            ds_scr[2 * g] = ds0
            ds_scr[2 * g + 1] = ds1
            ddt = ddt + ddt_g
            dcb_ref[:, sx] += dbx
            dcb_ref[:, sb] += dbb
            dcb_ref[:, sc] += dbc
            dalog_ref[...] += dal
            ddtb_ref[...] += ddtb
            ddsk_ref[...] += ddsk
            dnw_ref[:, sx] += dnw
        ddt_ref[...] = ddt.astype(ddt_ref.dtype)

    rev = lambda i: nc - 1 - i
    return pl.pallas_call(
        body, name=name, grid=(nc,),
        in_specs=[pl.BlockSpec((SSD_CS, GW), lambda i: (rev(i), dy_off // GW)),
                  pl.BlockSpec((SSD_CS, 1024), lambda i: (rev(i), 0)),
                  pl.BlockSpec((SSD_CS, GW), lambda i: (rev(i), O_SZ // GW)),
                  pl.BlockSpec((SSD_CS, LANE), lambda i: (rev(i), O_SDT // LANE)),
                  pl.BlockSpec((1, 4, LANE, LANE), lambda i: (rev(i), 0, 0, 0)),
                  _const_spec(convb), _const_spec(alog), _const_spec(dtb), _const_spec(dsk), _const_spec(nw)],
        out_specs=[pl.BlockSpec((SSD_CS, 1024), lambda i: (rev(i), 0)),
                   pl.BlockSpec((SSD_CS, GW), lambda i: (rev(i), 0)),
                   pl.BlockSpec((SSD_CS, LANE), lambda i: (rev(i), 0)),
                   _const_spec(convb), _const_spec(alog), _const_spec(dtb), _const_spec(dsk), _const_spec(nw)],
        out_shape=[jax.ShapeDtypeStruct((t, 1024), F32), jax.ShapeDtypeStruct((t, GW), BF16),
                   jax.ShapeDtypeStruct((t, LANE), BF16), jax.ShapeDtypeStruct(convb.shape, F32),
                   jax.ShapeDtypeStruct(alog.shape, F32), jax.ShapeDtypeStruct(dtb.shape, F32),
                   jax.ShapeDtypeStruct(dsk.shape, F32), jax.ShapeDtypeStruct(nw.shape, F32)],
        scratch_shapes=[pltpu.VMEM((4, LANE, LANE), F32)],
        compiler_params=_cparams("arbitrary"),
    )(dy, cx, proj, proj, ssave, convb, alog, dtb, dsk, nw)


def _swa_specs(nb, rev):
    w = ATT_W
    cur = lambda i: (nb - 1 - i) if rev else i
    prv = lambda i: jnp.maximum(cur(i) - 1, 0)
    return [pl.BlockSpec((w, 1024), lambda i: (cur(i), O_AQ // 1024)),
            pl.BlockSpec((w, 256), lambda i: (cur(i), O_AK // 256)),
            pl.BlockSpec((w, 256), lambda i: (prv(i), O_AK // 256)),
            pl.BlockSpec((w, 256), lambda i: (cur(i), O_AV // 256)),
            pl.BlockSpec((w, 256), lambda i: (prv(i), O_AV // 256)),
            pl.BlockSpec((w, LANE), lambda i: (cur(i), 0)), pl.BlockSpec((w, LANE), lambda i: (prv(i), 0)),
            pl.BlockSpec((w, LANE), lambda i: (cur(i), 0)), pl.BlockSpec((w, LANE), lambda i: (prv(i), 0))]


def swa_fwd(name, proj, cos, sin, sinks):
    t = proj.shape[0]
    nb = t // ATT_W

    def body(q_ref, kc_ref, kp_ref, vc_ref, vp_ref, cc_ref, cp_ref, sc_ref, sp_ref, snk_ref, y_ref):
        notfirst = jnp.where(pl.program_id(0) > 0, 1.0, 0.0).astype(F32)
        cc, cp, sc, sp = cc_ref[...], cp_ref[...], sc_ref[...], sp_ref[...]
        for g in range(2):
            sl = slice(g * LANE, (g + 1) * LANE)
            qg = jnp.concatenate([_rope(q_ref[:, (4 * g + r) * LANE:(4 * g + r + 1) * LANE], cc, sc)
                                  for r in range(4)], axis=1)
            o = _swa_block(qg, _rope(kp_ref[:, sl], cp, sp), _rope(kc_ref[:, sl], cc, sc), vp_ref[:, sl],
                           vc_ref[:, sl], snk_ref[...], notfirst, g=g)
            y_ref[:, g * GW:(g + 1) * GW] = o

    return pl.pallas_call(
        body, name=name, grid=(nb,),
        in_specs=_swa_specs(nb, False) + [_const_spec(sinks)],
        out_specs=pl.BlockSpec((ATT_W, 1024), lambda i: (i, 0)),
        out_shape=jax.ShapeDtypeStruct((t, 1024), F32),
        compiler_params=_cparams("arbitrary"),
    )(proj, proj, proj, proj, proj, cos, cos, sin, sin, sinks)


def swa_bwd(name, dy, dy_off, proj, cos, sin, sinks):
    t = proj.shape[0]
    nb = t // ATT_W

    def body(dy_ref, q_ref, kc_ref, kp_ref, vc_ref, vp_ref, cc_ref, cp_ref, sc_ref, sp_ref, snk_ref,
             dq_ref, dk_ref, dv_ref, dsnk_ref, ck_scr, cv_scr):
        i = pl.program_id(0)

        @pl.when(i == 0)
        def _():
            ck_scr[...] = jnp.zeros_like(ck_scr)
            cv_scr[...] = jnp.zeros_like(cv_scr)
            dsnk_ref[...] = jnp.zeros_like(dsnk_ref)

        notfirst = jnp.where(i < nb - 1, 1.0, 0.0).astype(F32)
        cc, cp, sc, sp = cc_ref[...], cp_ref[...], sc_ref[...], sp_ref[...]
        for g in range(2):
            sl = slice(g * LANE, (g + 1) * LANE)
            qg = jnp.concatenate([_rope(q_ref[:, (4 * g + r) * LANE:(4 * g + r + 1) * LANE], cc, sc)
                                  for r in range(4)], axis=1)
            _, vjp = jax.vjp(functools.partial(_swa_block, g=g), qg, _rope(kp_ref[:, sl], cp, sp),
                             _rope(kc_ref[:, sl], cc, sc), vp_ref[:, sl], vc_ref[:, sl], snk_ref[...], notfirst)
            dqg, dkp, dkc, dvp, dvc, dsnk, _ = vjp(dy_ref[:, g * GW:(g + 1) * GW])
            for r in range(4):
                hs = slice((4 * g + r) * LANE, (4 * g + r + 1) * LANE)
                dq_ref[:, hs] = _unrope(dqg[:, r * LANE:(r + 1) * LANE], cc, sc).astype(dq_ref.dtype)
            dk_ref[:, sl] = (_unrope(dkc, cc, sc) + ck_scr[:, sl]).astype(dk_ref.dtype)
            dv_ref[:, sl] = (dvc + cv_scr[:, sl]).astype(dv_ref.dtype)
            ck_scr[:, sl] = _unrope(dkp, cp, sp)
            cv_scr[:, sl] = dvp
            dsnk_ref[...] += dsnk

    rev = lambda i: nb - 1 - i
    return pl.pallas_call(
        body, name=name, grid=(nb,),
        in_specs=[pl.BlockSpec((ATT_W, 1024), lambda i: (rev(i), dy_off // 1024))] + _swa_specs(nb, True)
        + [_const_spec(sinks)],
        out_specs=[pl.BlockSpec((ATT_W, 1024), lambda i: (rev(i), 0)),
                   pl.BlockSpec((ATT_W, 256), lambda i: (rev(i), 0)),
                   pl.BlockSpec((ATT_W, 256), lambda i: (rev(i), 0)), _const_spec(sinks)],
        out_shape=[jax.ShapeDtypeStruct((t, 1024), BF16), jax.ShapeDtypeStruct((t, 256), BF16),
                   jax.ShapeDtypeStruct((t, 256), BF16), jax.ShapeDtypeStruct(sinks.shape, F32)],
        scratch_shapes=[pltpu.VMEM((ATT_W, 256), F32), pltpu.VMEM((ATT_W, 256), F32)],
        compiler_params=_cparams("arbitrary"),
    )(dy, proj, proj, proj, proj, proj, cos, cos, sin, sin, sinks)


def _me():
    return lax.axis_index("x"), lax.axis_index("y"), lax.axis_index("c")


ANY = pl.BlockSpec(memory_space=pl.ANY)


def allgather8(name, buf):
    r = buf.shape[0]

    def body(x_ref, out_ref, send_sems, recv_sems, local_sem):
        x, y, c = _me()
        me = 4 * x + 2 * y + c
        mine = pltpu.make_async_copy(x_ref, out_ref.at[me], local_sem)
        mine.start()
        copies = []
        for k in range(1, 8):
            bx, by, bc = (k >> 2) & 1, (k >> 1) & 1, k & 1
            to = ((1 - x) if bx else x, (1 - y) if by else y, (1 - c) if bc else c)
            cp = pltpu.make_async_remote_copy(src_ref=x_ref, dst_ref=out_ref.at[me], send_sem=send_sems.at[k - 1],
                                              recv_sem=recv_sems.at[k - 1], device_id=to, device_id_type=MESH)
            cp.start()
            copies.append(cp)
        for cp in copies:
            cp.wait_recv()
        for cp in copies:
            cp.wait_send()
        mine.wait()

    return pl.pallas_call(
        body, name=name, out_shape=jax.ShapeDtypeStruct((8, r, LANE), buf.dtype),
        in_specs=[pl.BlockSpec(memory_space=pltpu.VMEM)], out_specs=pl.BlockSpec(memory_space=pltpu.VMEM),
        scratch_shapes=[pltpu.SemaphoreType.DMA((7,)), pltpu.SemaphoreType.DMA((7,)), pltpu.SemaphoreType.DMA],
        compiler_params=pltpu.CompilerParams(vmem_limit_bytes=VMEM_LIMIT),
    )(buf)


def _sems(n):
    return [pltpu.SemaphoreType.DMA((n,)), pltpu.SemaphoreType.DMA((n,))]


def allgather_chips(name, shards):
    n = len(shards)

    def body(*refs):
        x_refs, out_refs, send_sems, recv_sems = refs[:n], refs[n:2 * n], refs[2 * n], refs[2 * n + 1]
        x, y, c = _me()
        chip = 2 * x + y
        chips = [(1 - x, y), (x, 1 - y), (1 - x, 1 - y)]

        def rc(src, dst, k, to):
            return pltpu.make_async_remote_copy(src_ref=src, dst_ref=dst, send_sem=send_sems.at[k],
                                                recv_sem=recv_sems.at[k], device_id=to, device_id_type=MESH)

        halves = [pl.ds(pl.multiple_of(c * (s.shape[0] // 2), 16), s.shape[0] // 2) for s in shards]
        own, first = [], {}
        for i in range(n):
            cp = rc(x_refs[i], out_refs[i].at[chip], 7 * i + 6, (x, y, 1 - c))
            cp.start()
            own.append(cp)
            for k, (cx, cy) in enumerate(chips):
                cp = rc(x_refs[i].at[halves[i]], out_refs[i].at[chip, halves[i]], 7 * i + k, (cx, cy, c))
                cp.start()
                first[i, k] = cp
        passed = []
        for i in range(n):
            for k, (cx, cy) in enumerate(chips):
                first[i, k].wait_recv()
                src = out_refs[i].at[2 * cx + cy, halves[i]]
                cp = rc(src, src, 7 * i + 3 + k, (x, y, 1 - c))
                cp.start()
                passed.append(cp)
        for cp in passed + own:
            cp.wait_recv()
        for cp in list(first.values()) + passed + own:
            cp.wait_send()

    for s in shards:
        assert s.shape[0] % 32 == 0
    return pl.pallas_call(
        body, name=name, out_shape=[jax.ShapeDtypeStruct((4,) + s.shape, s.dtype) for s in shards],
        in_specs=[ANY] * n, out_specs=[ANY] * n, scratch_shapes=_sems(7 * n),
    )(*shards)


def sibling_swap_halves(name, gs):
    n = len(gs)

    def body(*refs):
        g_refs, out_refs, send_sems, recv_sems = refs[:n], refs[n:2 * n], refs[2 * n], refs[2 * n + 1]
        x, y, c = _me()
        cps = [pltpu.make_async_remote_copy(src_ref=g_refs[i].at[:, 1 - c], dst_ref=out_refs[i],
                                            send_sem=send_sems.at[i], recv_sem=recv_sems.at[i],
                                            device_id=(x, y, 1 - c), device_id_type=MESH) for i in range(n)]
        for cp in cps:
            cp.start()
        for cp in cps:
            cp.wait_recv()
        for cp in cps:
            cp.wait_send()

    return pl.pallas_call(
        body, name=name, out_shape=[jax.ShapeDtypeStruct((4,) + g.shape[2:], g.dtype) for g in gs],
        in_specs=[ANY] * n, out_specs=[ANY] * n, scratch_shapes=_sems(n),
    )(*gs)


def chips_exchange(name, ps):
    n = len(ps)

    def body(*refs):
        p_refs, out_refs, send_sems, recv_sems = refs[:n], refs[n:2 * n], refs[2 * n], refs[2 * n + 1]
        x, y, c = _me()
        chip = 2 * x + y
        cps = []
        for i in range(n):
            for k, (cx, cy) in enumerate([(1 - x, y), (x, 1 - y), (1 - x, 1 - y)]):
                cps.append(pltpu.make_async_remote_copy(
                    src_ref=p_refs[i].at[2 * cx + cy], dst_ref=out_refs[i].at[chip], send_sem=send_sems.at[3 * i + k],
                    recv_sem=recv_sems.at[3 * i + k], device_id=(cx, cy, c), device_id_type=MESH))
        for cp in cps:
            cp.start()
        for cp in cps:
            cp.wait_recv()
        for cp in cps:
            cp.wait_send()

    return pl.pallas_call(
        body, name=name, out_shape=[jax.ShapeDtypeStruct(p.shape, p.dtype) for p in ps],
        in_specs=[ANY] * n, out_specs=[ANY] * n, scratch_shapes=_sems(3 * n),
    )(*ps)


def sibling_send(name, ss):
    n = len(ss)

    def body(*refs):
        s_refs, out_refs, send_sems, recv_sems = refs[:n], refs[n:2 * n], refs[2 * n], refs[2 * n + 1]
        x, y, c = _me()
        cps = [pltpu.make_async_remote_copy(src_ref=s_refs[i], dst_ref=out_refs[i], send_sem=send_sems.at[i],
                                            recv_sem=recv_sems.at[i], device_id=(x, y, 1 - c), device_id_type=MESH)
               for i in range(n)]
        for cp in cps:
            cp.start()
        for cp in cps:
            cp.wait_recv()
        for cp in cps:
            cp.wait_send()

    return pl.pallas_call(
        body, name=name, out_shape=[jax.ShapeDtypeStruct(s.shape, s.dtype) for s in ss],
        in_specs=[ANY] * n, out_specs=[ANY] * n, scratch_shapes=_sems(n),
    )(*ss)


def _rs_tile(rh, w):
    return _pick(rh, max(8, (1 << 19) // w // 8 * 8), 8)


def add_half(name, g, recv, idx):
    _, _, rh, w = g.shape
    tile = _rs_tile(rh, w)

    def body(s_ref, g_ref, r_ref, o_ref):
        o_ref[...] = (g_ref[...] + r_ref[...]).astype(o_ref.dtype)

    return pl.pallas_call(
        body, name=name,
        grid_spec=pltpu.PrefetchScalarGridSpec(
            num_scalar_prefetch=1, grid=(4, rh // tile),
            in_specs=[pl.BlockSpec((None, None, tile, w), lambda s, i, ix: (s, ix[1], i, 0)),
                      pl.BlockSpec((None, tile, w), lambda s, i, ix: (s, i, 0))],
            out_specs=pl.BlockSpec((None, tile, w), lambda s, i, ix: (s, i, 0))),
        out_shape=jax.ShapeDtypeStruct((4, rh, w), BF16),
        compiler_params=_cparams("parallel", "parallel"),
    )(idx, g, recv)


def rs_sum(name, g, recv1, recv2, idx):
    _, _, rh, w = g.shape
    tile = _rs_tile(rh, w)

    def body(s_ref, g_ref, r1_ref, a_ref, b_ref, c_ref, o_ref):
        o_ref[...] = (g_ref[...] + r1_ref[...]) + (a_ref[...].astype(F32) + b_ref[...].astype(F32)
                                                    + c_ref[...].astype(F32))

    def other(k):
        return pl.BlockSpec((None, tile, w), lambda i, ix: (jnp.bitwise_xor(ix[0], k), i, 0))

    return pl.pallas_call(
        body, name=name,
        grid_spec=pltpu.PrefetchScalarGridSpec(
            num_scalar_prefetch=1, grid=(rh // tile,),
            in_specs=[pl.BlockSpec((None, None, tile, w), lambda i, ix: (ix[0], ix[1], i, 0)),
                      pl.BlockSpec((None, tile, w), lambda i, ix: (ix[0], i, 0)), other(1), other(2), other(3)],
            out_specs=pl.BlockSpec((tile, w), lambda i, ix: (i, 0))),
        out_shape=jax.ShapeDtypeStruct((rh, w), F32),
        compiler_params=_cparams("parallel"),
    )(idx, g, recv1, recv2, recv2, recv2)


def sum_slabs(name, a):
    n, r, _ = a.shape
    tile = _pick(r, 2048, 8)

    def body(a_ref, o_ref):
        acc = a_ref[0]
        for s in range(1, n):
            acc = acc + a_ref[s]
        o_ref[...] = acc

    return pl.pallas_call(
        body, name=name, grid=(r // tile,),
        in_specs=[pl.BlockSpec((n, tile, LANE), lambda i: (0, i, 0))],
        out_specs=pl.BlockSpec((tile, LANE), lambda i: (i, 0)),
        out_shape=jax.ShapeDtypeStruct((r, LANE), F32),
        compiler_params=_cparams("parallel"),
    )(a)


def f_premod(rv, cv):
    (x,), (w, scale, shift) = rv, cv
    y = x * lax.rsqrt(jnp.mean(x * x, -1, keepdims=True) + EPS) * w
    return [y * (1.0 + scale) + shift]


def f_postres(rv, cv):
    (x, y), (w, gate) = rv, cv
    return [x + gate * (y * lax.rsqrt(jnp.mean(y * y, -1, keepdims=True) + EPS) * w)]


def f_adamw(rv, cv):
    w, g, m, v = rv
    m = B1 * m + (1.0 - B1) * g
    v = B2 * v + (1.0 - B2) * jnp.square(g)
    m_hat = m / (1.0 - B1 ** STEP)
    v_hat = v / (1.0 - B2 ** STEP)
    return [-LR * (m_hat / (jnp.sqrt(v_hat) + AEPS) + WD * w), m, v], []


def adamw(name, w, g, m, v):
    r, wd = w.shape
    tile = r
    while tile * wd * 4 * 14 > 24 * 1024 * 1024 and tile % 16 == 0:
        tile //= 2
    (d, m2, v2), _ = rowmap(name, f_adamw, [(a, 0, wd) for a in (w, g, m, v)], [], [(wd, F32)] * 3, tile=tile)
    return d, m2, v2


_IN_SIZES = (512, 512, 512, 512, 4, 4, 512, 1024, 8, 512, 128, 128, 512, 512, 512)


def _relayout_w_in(w):
    d = w.shape[0]
    s = np.cumsum((0,) + _IN_SIZES)
    gq, gk, gv, gz, gb, ga, sz, sx, sdt, aq, ak, av, cb, cc, ch = [w[:, s[i]:s[i + 1]] for i in range(15)]

    def rope_pad(t, nh):
        return jnp.pad(t.reshape(d, nh, 2, 32), ((0, 0), (0, 0), (0, 0), (0, 32))).reshape(d, nh * LANE)

    def v_pad(t, nh):
        return jnp.pad(t.reshape(d, nh, 64), ((0, 0), (0, 0), (0, 64))).reshape(d, nh * LANE)

    def pad128(t):
        return jnp.pad(t, ((0, 0), (0, LANE - t.shape[1])))

    return jnp.concatenate([gq, gk, gv, cb, cc, ch, sx, rope_pad(aq, 8), gz, sz, rope_pad(ak, 2), v_pad(av, 2),
                            pad128(jnp.concatenate([gb, ga], axis=1)), pad128(sdt)], axis=1)


def _unlayout_g_in(g):
    d = g.shape[0]

    def rope_unpad(t, nh):
        return t.reshape(d, nh, 2, 64)[..., :32].reshape(d, nh * 64)

    def v_unpad(t, nh):
        return t.reshape(d, nh, LANE)[..., :64].reshape(d, nh * 64)

    return jnp.concatenate([g[:, 0:1536], g[:, O_GZ:O_GZ + 512], g[:, O_GBA:O_GBA + 8], g[:, O_SZ:O_SZ + 512],
                            g[:, O_XBC:O_XBC + 1024], g[:, O_SDT:O_SDT + 8], rope_unpad(g[:, O_AQ:O_AQ + 1024], 8),
                            rope_unpad(g[:, O_AK:O_AK + 256], 2), v_unpad(g[:, O_AV:O_AV + 256], 2),
                            g[:, O_CB:O_CB + 1536]], axis=1)


def _relayout_w_out(w):
    d = w.shape[1]
    yc = jnp.pad(w[1024:1536].reshape(8, 64, d), ((0, 0), (0, 64), (0, 0))).reshape(1024, d)
    return jnp.concatenate([w[:1024], yc, w[1536:]], axis=0)


def _unlayout_g_out(g):
    d = g.shape[1]
    return jnp.concatenate([g[:1024], g[1024:2048].reshape(8, LANE, d)[:, :64].reshape(512, d), g[2048:]], axis=0)


def _pack(arrs):
    rows, spec, off = [], [], 0
    for a in arrs:
        n = int(np.prod(a.shape))
        nr = -(-n // LANE)
        rows.append(jnp.pad(a.reshape(-1).astype(F32), (0, nr * LANE - n)).reshape(nr, LANE))
        spec.append((off, a.shape))
        off += nr
    pad = (-off) % 8
    if pad:
        rows.append(jnp.zeros((pad, LANE), F32))
    return jnp.concatenate(rows, axis=0), spec


def _unpack(buf, spec):
    out = []
    for off, shape in spec:
        n = int(np.prod(shape))
        nr = -(-n // LANE)
        out.append(buf[off:off + nr].reshape(-1)[:n].reshape(shape))
    return out


def _row128(v):
    return jnp.pad(v.astype(F32), (0, LANE - v.shape[0])).reshape(1, LANE)


BIG = ("w_in", "w_out", "w_up", "w_down")
SMALL_REPL = ("ada_b", "norm_pre_mix", "norm_post_mix", "norm_pre_mlp", "norm_post_mlp", "gdn_a_log", "gdn_dt_bias",
              "gdn_norm_w", "ssm_conv_b", "ssm_a_log", "ssm_dt_bias", "ssm_d", "ssm_norm_w", "attn_sinks")
SMALL_SHARD = ("gdn_conv_w", "ssm_conv_w", "sc_conv_w")
WEIGHTS = ("ada_w", "ada_b", "norm_pre_mix", "norm_post_mix", "norm_pre_mlp", "norm_post_mlp", "w_in", "w_out",
           "gdn_conv_w", "gdn_a_log", "gdn_dt_bias", "gdn_norm_w", "ssm_conv_w", "ssm_conv_b", "ssm_a_log",
           "ssm_dt_bias", "ssm_d", "ssm_norm_w", "attn_sinks", "sc_conv_w", "w_up", "w_down")


def _step(p, m, v, x, c, positions, loss_target):
    d = x.shape[2]
    xi, yi, ci = _me()
    chip = 2 * xi + yi
    me = 4 * xi + 2 * yi + ci
    x0 = x[0]
    tgt = loss_target[0]

    buf, spec = _pack([c[0], p["gdn_conv_w"], p["ssm_conv_w"], p["sc_conv_w"]])
    got = allgather8("ag_small_in", buf)
    per_dev = [_unpack(got[k], spec) for k in range(8)]
    c_all = jnp.stack([pd[0] for pd in per_dev], axis=0)
    conv_full = [jnp.concatenate([per_dev[4 * a + 2 * b][1 + j] for a in range(2) for b in range(2)], axis=-1)
                 for j in range(3)]
    gdn_conv_w, ssm_conv_w, sc_conv_w = conv_full

    c_pad = jnp.pad(c_all, ((0, 8), (0, 0)))
    silu_bf = lambda a: _silu(a).astype(BF16)
    nsh = p["ada_w"].shape[2]
    mods = []
    for l in range(DEPTH):
        mo = mm_nn(f"ada_fwd{l}", c_pad, p["ada_w"][l], F32, a_fn=silu_bf)
        mods.append(mo[:8] + lax.dynamic_slice(p["ada_b"][l], (chip * nsh,), (nsh,))[None])
    buf, spec = _pack([jnp.stack(mods, axis=0)])
    got = allgather8("ag_mod", buf)
    mod_sh = [_unpack(got[4 * a + 2 * b], spec)[0] for a in range(2) for b in range(2)]
    mod_all = jnp.concatenate(mod_sh, axis=-1)
    mod = lax.dynamic_index_in_dim(mod_all, me, axis=1, keepdims=False)

    items = [(l, n) for l in range(DEPTH) for n in BIG]
    gathered = dict(zip(items, allgather_chips("ag_weights", [p[n][l].astype(BF16) for (l, n) in items])))
    wfull = []
    for l in range(DEPTH):
        w_in = jnp.concatenate([gathered[l, "w_in"][s] for s in range(4)], axis=1)
        wfull.append(dict(w_in=_relayout_w_in(w_in), w_out=_relayout_w_out(gathered[l, "w_out"].reshape(-1, d)),
                          w_up=gathered[l, "w_up"], w_down=gathered[l, "w_down"].reshape(-1, d)))

    inv_freq = ROPE_THETA ** (-jnp.arange(0, 64, 2, dtype=F32) / 64)
    ang = positions[0].astype(F32)[:, None] * inv_freq
    z32 = jnp.zeros_like(ang)
    cos_t = jnp.concatenate([jnp.cos(ang), z32, jnp.cos(ang), z32], axis=1)
    sin_t = jnp.concatenate([-jnp.sin(ang), z32, jnp.sin(ang), z32], axis=1)

    relu2 = lambda a: jnp.square(jnp.maximum(a, 0))

    def layer_consts(l):
        row = lambda v_: v_.reshape(1, -1).astype(F32)
        return dict(
            alog=_row128(p["gdn_a_log"][l]), gdtb=_row128(p["gdn_dt_bias"][l]), gnw=row(p["gdn_norm_w"][l]),
            scb=row(p["ssm_conv_b"][l]), salog=_row128(p["ssm_a_log"][l]), sdtb=_row128(p["ssm_dt_bias"][l]),
            sdsk=_row128(p["ssm_d"][l]), snw=row(p["ssm_norm_w"][l]), sinks=_row128(p["attn_sinks"][l]))

    saved = []
    xc = x0
    for l in range(DEPTH):
        w = wfull[l]
        k = layer_consts(l)
        md = [mod[l, j * d:(j + 1) * d].reshape(1, d) for j in range(6)]
        npre, npost = p["norm_pre_mix"][l].reshape(1, d), p["norm_post_mix"][l].reshape(1, d)
        npre2, npost2 = p["norm_pre_mlp"][l].reshape(1, d), p["norm_post_mlp"][l].reshape(1, d)
        (h,), _ = rowmap(f"premix{l}", lambda rv, cv: (f_premod(rv, cv), []), [(xc, 0, d)], [npre, md[1], md[0]],
                         [(d, BF16)])
        proj = mm_nn(f"inproj{l}", h, w["w_in"], F32)
        cq = conv_fwd(f"gdn_conv{l}", [(proj, O_GQKV, 1536)], gdn_conv_w[l])
        ya, gs = gdn_fwd(f"gdn{l}", cq, proj, k["alog"], k["gdtb"], k["gnw"])
        cx = conv_fwd(f"ssm_conv{l}", [(proj, O_XBC, 1024)], ssm_conv_w[l])
        yb, ss = ssd_fwd(f"ssd{l}", cx, proj, k["scb"], k["salog"], k["sdtb"], k["sdsk"], k["snw"])
        yc = swa_fwd(f"swa{l}", proj, cos_t, sin_t, k["sinks"])
        cd = conv_fwd(f"sc_conv{l}", [(proj, O_CC, GW), (proj, O_CH, GW)], sc_conv_w[l])
        (ycat,), _ = rowmap(f"ycat{l}", lambda rv, cv: ([jnp.concatenate([rv[0], rv[1], rv[2], rv[3] * rv[4]], 1)], []),
                            [(ya, 0, GW), (yb, 0, GW), (yc, 0, 1024), (proj, O_CB, GW), (cd, 0, GW)], [],
                            [(YCAT, BF16)])
        y = mm_nn(f"outproj{l}", ycat, w["w_out"], F32)
        (x1,), _ = rowmap(f"postmix{l}", lambda rv, cv: (f_postres(rv, cv), []), [(xc, 0, d), (y, 0, d)],
                          [npost, md[2]], [(d, F32)])
        (h2,), _ = rowmap(f"premlp{l}", lambda rv, cv: (f_premod(rv, cv), []), [(x1, 0, d)], [npre2, md[4], md[3]],
                          [(d, BF16)])
        u = mm_nn(f"up{l}", h2, w["w_up"], BF16)
        y2 = mm_nn(f"down{l}", u, w["w_down"], F32, a_fn=relu2)
        (x2,), _ = rowmap(f"postmlp{l}", lambda rv, cv: (f_postres(rv, cv), []), [(x1, 0, d), (y2, 0, d)],
                          [npost2, md[5]], [(d, F32)])
        saved.append(dict(x=xc, h=h, proj=proj, cq=cq, gs=gs, cx=cx, ss=ss, cd=cd, ycat=ycat, y=y, x1=x1, h2=h2, u=u,
                          y2=y2, md=md, k=k))
        xc = x2

    def f_loss(rv, cv):
        err = rv[0] - rv[1]
        part = 0.5 * jnp.sum(jnp.mean(err * err, axis=-1, keepdims=True), axis=0, keepdims=True)
        return [err * (1.0 / d)], [jnp.broadcast_to(part, (1, LANE))]

    (dx,), (loss_part,) = rowmap("loss", f_loss, [(xc, 0, d), (tgt, 0, d)], [], [(d, F32)], [(1, LANE)])

    gsmall = [None] * DEPTH
    gbig = [None] * DEPTH
    dmods = [None] * DEPTH
    for l in reversed(range(DEPTH)):
        w, s = wfull[l], saved[l]
        k, md = s["k"], s["md"]
        npre, npost = p["norm_pre_mix"][l].reshape(1, d), p["norm_post_mix"][l].reshape(1, d)
        npre2, npost2 = p["norm_pre_mlp"][l].reshape(1, d), p["norm_post_mlp"][l].reshape(1, d)
        (dx1a, dy2), (dnpost2, dgate_m) = rowmap_vjp(f"postmlp_b{l}", f_postres, [(s["x1"], 0, d), (s["y2"], 0, d)],
                                                      [npost2, md[5]], [(dx, 0, d)], 2, [F32, BF16])
        du = mm_nn(f"down_b{l}", dy2, w["w_down"], BF16, extra=s["u"], wt=True,
                   epi=lambda r, uu: r * (2.0 * jnp.maximum(uu.astype(F32), 0.0)))
        g_down = mm_tn(f"down_g{l}", s["u"], dy2, a_fn=relu2)
        dh2 = mm_nn(f"up_b{l}", du, w["w_up"], F32, wt=True)
        g_up = mm_tn(f"up_g{l}", s["h2"], du, col_slabs=4)
        (dx1,), (dnpre2, dscale_m, dshift_m) = rowmap_vjp(f"premlp_b{l}", f_premod, [(s["x1"], 0, d)],
                                                          [npre2, md[4], md[3]], [(dh2, 0, d)], 1, [F32],
                                                          add_rows=[(dx1a, 0, d)])
        (dxa, dy), (dnpost, dgate_a) = rowmap_vjp(f"postmix_b{l}", f_postres, [(s["x"], 0, d), (s["y"], 0, d)],
                                                  [npost, md[2]], [(dx1, 0, d)], 2, [F32, BF16])
        dycat = mm_nn(f"outproj_b{l}", dy, w["w_out"], F32, wt=True)
        g_out = mm_tn(f"outproj_g{l}", s["ycat"], dy)
        (dcb, dcd), _ = rowmap(f"scgate_b{l}", lambda rv, cv: ([rv[0] * rv[2], rv[0] * rv[1]], []),
                               [(dycat, 2048, GW), (s["proj"], O_CB, GW), (s["cd"], 0, GW)], [],
                               [(GW, BF16), (GW, F32)])
        (dcc, dch), g_scw = conv_bwd(f"sc_conv_b{l}", dcd, [(s["proj"], O_CC, GW), (s["proj"], O_CH, GW)],
                                     sc_conv_w[l], BF16)
        daq, dak, dav, g_sinks = swa_bwd(f"swa_b{l}", dycat, 1024, s["proj"], cos_t, sin_t, k["sinks"])
        dcx, dsz, dsdt, g_scb, g_salog, g_sdtb, g_sdsk, g_snw = ssd_bwd(
            f"ssd_b{l}", dycat, 512, s["cx"], s["proj"], s["ss"], k["scb"], k["salog"], k["sdtb"], k["sdsk"], k["snw"])
        (dxbc,), g_ssmw = conv_bwd(f"ssm_conv_b{l}", dcx, [(s["proj"], O_XBC, 1024)], ssm_conv_w[l], BF16)
        dcq, dgz, dgba, g_alog, g_gdtb, g_gnw = gdn_bwd(f"gdn_b{l}", dycat, 0, s["cq"], s["proj"], s["gs"],
                                                        k["alog"], k["gdtb"], k["gnw"])
        (dgqkv,), g_gdnw = conv_bwd(f"gdn_conv_b{l}", dcq, [(s["proj"], O_GQKV, 1536)], gdn_conv_w[l], BF16)
        dproj = jnp.concatenate([dgqkv, dcb, dcc, dch, dxbc, daq, dgz, dsz, dak, dav, dgba, dsdt], axis=1)
        dh = mm_nn(f"inproj_b{l}", dproj, w["w_in"], F32, wt=True)
        g_in = mm_tn(f"inproj_g{l}", s["h"], dproj)
        (dx,), (dnpre, dscale_a, dshift_a) = rowmap_vjp(f"premix_b{l}", f_premod, [(s["x"], 0, d)],
                                                        [npre, md[1], md[0]], [(dh, 0, d)], 1, [F32],
                                                        add_rows=[(dxa, 0, d)])
        dmods[l] = jnp.concatenate([dshift_a, dscale_a, dgate_a, dshift_m, dscale_m, dgate_m], axis=1)[0]
        gsmall[l] = dict(
            norm_pre_mix=dnpre[0], norm_post_mix=dnpost[0], norm_pre_mlp=dnpre2[0], norm_post_mlp=dnpost2[0],
            gdn_a_log=g_alog[0, :4], gdn_dt_bias=g_gdtb[0, :4], gdn_norm_w=g_gnw[0], ssm_conv_b=g_scb[0],
            ssm_a_log=g_salog[0, :8], ssm_dt_bias=g_sdtb[0, :8], ssm_d=g_sdsk[0, :8], ssm_norm_w=g_snw[0],
            attn_sinks=g_sinks[0, :8], gdn_conv_w=g_gdnw, ssm_conv_w=g_ssmw, sc_conv_w=g_scw)
        gbig[l] = dict(w_in=jnp.stack(jnp.split(_unlayout_g_in(g_in), 4, axis=1), axis=0),
                       w_out=_unlayout_g_out(g_out).reshape(4, -1, d), w_up=g_up,
                       w_down=g_down.reshape(4, -1, d))
    grad_x = dx[None]

    buf, spec = _pack([jnp.stack(dmods, axis=0)])
    got = allgather8("ag_dmod", buf)
    dmod_sum = _unpack(sum_slabs("dmod_sum", got), spec)[0]
    dmod_all = jnp.stack([_unpack(got[kk], spec)[0] for kk in range(8)], axis=1)
    g_ada_w = []
    for l in range(DEPTH):
        dm = lax.dynamic_slice(dmod_all[l], (0, chip * nsh), (8, nsh))
        g_ada_w.append(mm_tn(f"ada_g{l}", c_pad, jnp.pad(dm, ((0, 8), (0, 0))), a_fn=silu_bf))
    g_ada_w = jnp.stack(g_ada_w, axis=0)

    names = [n for n in SMALL_REPL if n != "ada_b"] + list(SMALL_SHARD)
    buf, spec = _pack([jnp.stack([gsmall[l][n] for l in range(DEPTH)], axis=0) for n in names] + [loss_part])
    tot = _unpack(sum_slabs("small_sum", allgather8("ag_small_g", buf)), spec)
    gfull = dict(zip(names, tot[:-1]))
    loss = tot[-1][0, 0]
    grads = {"ada_w": g_ada_w, "ada_b": dmod_sum}
    for n in SMALL_REPL:
        if n != "ada_b":
            grads[n] = gfull[n]
    for n in SMALL_SHARD:
        wdt = p[n].shape[2]
        grads[n] = lax.dynamic_slice_in_dim(gfull[n], chip * wdt, wdt, axis=2)

    idx = jnp.stack([chip, ci]).astype(jnp.int32)
    g4 = [gbig[l][n].reshape(4, 2, gbig[l][n].shape[1] // 2, gbig[l][n].shape[2]) for (l, n) in items]
    recv1 = sibling_swap_halves("rs_sibling", g4)
    part = [add_half(f"rs_add_{n}{l}", g, r1, idx) for (l, n), g, r1 in zip(items, g4, recv1)]
    recv2 = chips_exchange("rs_chips", part)
    mine = [rs_sum(f"rs_sum_{n}{l}", g, r1, r2, idx) for (l, n), g, r1, r2 in zip(items, g4, recv1, recv2)]
    other = sibling_send("rs_join", mine)
    first = ci == 0
    gsh = {it: jnp.concatenate([jnp.where(first, a, b), jnp.where(first, b, a)], axis=0)
           for it, a, b in zip(items, mine, other)}
    for n in BIG:
        grads[n] = jnp.stack([gsh[l, n] for l in range(DEPTH)], axis=0)

    delta, new_m, new_v = {}, {}, {}
    for n in ("ada_w",) + BIG:
        sh = p[n].shape
        two = lambda a: a.reshape(sh[0] * sh[1], sh[2])
        dl, m2, v2 = adamw(f"adamw_{n}", two(p[n]), two(grads[n]), two(m[n]), two(v[n]))
        delta[n], new_m[n], new_v[n] = dl.reshape(sh), m2.reshape(sh), v2.reshape(sh)
    small = list(SMALL_REPL) + list(SMALL_SHARD)
    bufs = []
    for src in (p, grads, m, v):
        b_, spec = _pack([src[n] for n in small])
        bufs.append(b_)
    dl, m2, v2 = adamw("adamw_small", *bufs)
    for n, a, b_, cc_ in zip(small, _unpack(dl, spec), _unpack(m2, spec), _unpack(v2, spec)):
        delta[n], new_m[n], new_v[n] = a, b_, cc_

    return (loss, grad_x, *[grads[n] for n in WEIGHTS], *[delta[n] for n in WEIGHTS], *[new_m[n] for n in WEIGHTS],
            *[new_v[n] for n in WEIGHTS])


def kernel(x, c, positions, ada_w, ada_b, norm_pre_mix, norm_post_mix, norm_pre_mlp, norm_post_mlp, w_in, w_out, gdn_conv_w, gdn_a_log, gdn_dt_bias, gdn_norm_w, ssm_conv_w, ssm_conv_b, ssm_a_log, ssm_dt_bias, ssm_d, ssm_norm_w, attn_sinks, sc_conv_w, w_up, w_down, loss_target, m_ada_w, m_ada_b, m_norm_pre_mix, m_norm_post_mix, m_norm_pre_mlp, m_norm_post_mlp, m_w_in, m_w_out, m_gdn_conv_w, m_gdn_a_log, m_gdn_dt_bias, m_gdn_norm_w, m_ssm_conv_w, m_ssm_conv_b, m_ssm_a_log, m_ssm_dt_bias, m_ssm_d, m_ssm_norm_w, m_attn_sinks, m_sc_conv_w, m_w_up, m_w_down, v_ada_w, v_ada_b, v_norm_pre_mix, v_norm_post_mix, v_norm_pre_mlp, v_norm_post_mlp, v_w_in, v_w_out, v_gdn_conv_w, v_gdn_a_log, v_gdn_dt_bias, v_gdn_norm_w, v_ssm_conv_w, v_ssm_conv_b, v_ssm_a_log, v_ssm_dt_bias, v_ssm_d, v_ssm_norm_w, v_attn_sinks, v_sc_conv_w, v_w_up, v_w_down):
    loc = locals()
    p = {n: loc[n] for n in WEIGHTS}
    m = {n: loc["m_" + n] for n in WEIGHTS}
    v = {n: loc["v_" + n] for n in WEIGHTS}
    return _step(p, m, v, x, c, positions, loss_target)
```

```python
import functools

import numpy as np
import jax
import jax.numpy as jnp
from jax import lax
from jax.experimental import pallas as pl
from jax.experimental.pallas import tpu as pltpu

F32, BF16 = jnp.float32, jnp.bfloat16
MESH = pl.DeviceIdType.MESH

DEPTH = 2
GW = 512
N_GDN, GDN_CS = 4, 64
SSD_CS, ATT_W = 128, 128
EPS = 1e-6
ROPE_THETA = 10000.0
IN_WIDTH = 5904
B1, B2, LR, AEPS, WD, STEP = 0.9, 0.999, 0.001, 1e-8, 0.01, 10

VMEM_LIMIT = 52 * 1024 * 1024
LANE = 128

O_GQKV, O_CB, O_CC, O_CH, O_XBC, O_AQ, O_GZ, O_SZ, O_AK, O_AV, O_GBA, O_SDT = (
    0, 1536, 2048, 2560, 3072, 4096, 5120, 5632, 6144, 6400, 6656, 6784)
PW = 6912
YCAT = 2560


def _cparams(*sem):
    return pltpu.CompilerParams(dimension_semantics=sem or None, vmem_limit_bytes=VMEM_LIMIT)


def _pick(n, cap, mult):
    if n <= cap:
        return n
    best = None
    for d in range(mult, cap + 1, mult):
        if n % d == 0:
            best = d
    assert best is not None, (n, cap, mult)
    return best


def _pieces(x, n):
    out, r = [], x
    for i in range(n):
        h = r.astype(BF16)
        out.append(h)
        if i < n - 1:
            r = r - h.astype(F32)
    return out


def _dgp(a, b, ca, cb, na, nb):
    dn = (((ca,), (cb,)), ((), ()))
    acc = None
    for i, pa in enumerate(_pieces(a, na)):
        for j, pb in enumerate(_pieces(b, nb)):
            if i + j < max(na, nb):
                t = lax.dot_general(pa, pb, dn, preferred_element_type=F32)
                acc = t if acc is None else acc + t
    return acc


_CONTRACT = {"nn": (1, 0), "nt": (1, 1), "tn": (0, 0)}


@functools.lru_cache(maxsize=None)
def _mm(kind, na, nb, const_a=False):
    ca, cb = _CONTRACT[kind]
    ng = max(na, nb)

    @jax.custom_vjp
    def f(a, b):
        return _dgp(a, b, ca, cb, na, nb)

    def bwd(r, g):
        a, b = r
        if kind == "nn":
            da, db = (lambda: _mm("nt", ng, nb)(g, b)), _mm("tn", na, ng, const_a)(a, g)
        elif kind == "nt":
            da, db = (lambda: _mm("nn", ng, nb)(g, b)), _mm("tn", ng, na)(g, a)
        else:
            da, db = (lambda: _mm("nt", nb, ng)(b, g)), _mm("nn", na, ng, const_a)(a, g)
        return (jnp.zeros_like(a) if const_a else da()), db

    f.defvjp(lambda a, b: (f(a, b), (a, b)), bwd)
    return f


_nn, _nt, _tn = _mm("nn", 1, 1), _mm("nt", 1, 1), _mm("tn", 1, 1)
_hnn = _mm("nn", 2, 2)
_xnn = _mm("nn", 1, 3, True)


def _silu(x):
    return x * jax.nn.sigmoid(x)


def _softplus(x):
    return jnp.maximum(x, 0.0) + jnp.log1p(jnp.exp(-jnp.abs(x)))


def _colsel(v, idx):
    lane = lax.broadcasted_iota(jnp.int32, v.shape, 1)
    return jnp.sum(jnp.where(lane == idx, v, 0.0), axis=1, keepdims=True)


def _tri(n):
    r = lax.broadcasted_iota(jnp.int32, (n, n), 0)
    c = lax.broadcasted_iota(jnp.int32, (n, n), 1)
    return r >= c, r > c


def _rowsel(v, idx):
    row = lax.broadcasted_iota(jnp.int32, v.shape, 0)
    return jnp.sum(jnp.where(row == idx, v, 0.0), axis=0, keepdims=True)


def _tri_inv(m, size, nil):
    r = lax.broadcasted_iota(jnp.int32, (size, size), 0)
    c = lax.broadcasted_iota(jnp.int32, (size, size), 1)
    x = jnp.where(r == c, 1.0, 0.0) - m
    p = _hnn(m, m)
    steps = int(np.log2(nil)) - 1
    for s in range(steps):
        x = x + _hnn(x, p)
        if s < steps - 1:
            p = _hnn(p, p)
    return x


def _gdn_chunk(cq, z, ba, s, alog, dtb, nw):
    n, nh = GDN_CS, N_GDN
    rr = n * nh

    def stack(a, off):
        return jnp.concatenate([a[:, off + h * LANE:off + (h + 1) * LANE] for h in range(nh)], axis=0)

    def percol(f):
        return jnp.concatenate([f(h) for h in range(nh)], axis=0)

    def own(a):
        return jnp.concatenate([a[h * n:(h + 1) * n, h * LANE:(h + 1) * LANE] for h in range(nh)], axis=0)

    q, k, v = _silu(stack(cq, 0)), _silu(stack(cq, GW)), _silu(stack(cq, 2 * GW))
    q = q * lax.rsqrt(jnp.sum(q * q, -1, keepdims=True) + EPS) * (128 ** -0.5)
    k = k * lax.rsqrt(jnp.sum(k * k, -1, keepdims=True) + EPS)
    beta = jax.nn.sigmoid(percol(lambda h: _colsel(ba, h)))
    g = percol(lambda h: -jnp.exp(_colsel(alog, h)) * _softplus(_colsel(ba, nh + h) + _colsel(dtb, h)))
    ri = lax.broadcasted_iota(jnp.int32, (rr, rr), 0)
    ci = lax.broadcasted_iota(jnp.int32, (rr, rr), 1)
    same = (ri >> 6) == (ci >> 6)
    incl, strict = same & (ri >= ci), same & (ri > ci)
    gcb = _xnn(incl.astype(F32), jnp.broadcast_to(g, (rr, LANE)))
    gci = jnp.concatenate([gcb, gcb], axis=1)
    dec = jnp.where(incl, jnp.exp(jnp.where(incl, gci - gci.T, 0.0)), 0.0)
    kb = k * beta
    kk = _nt(jnp.concatenate([kb, q], axis=0), k)
    m = jnp.where(strict, kk[:rr] * dec, 0.0)
    attn = jnp.where(incl, kk[rr:] * dec, 0.0)
    tinv = _tri_inv(m, rr, n)
    egc = jnp.exp(gcb)
    uw = _nn(tinv, jnp.concatenate([v * beta, kb * egc], axis=1))
    u, w = uw[:, :LANE], uw[:, LANE:]
    gl = [jnp.sum(g[h * n:(h + 1) * n], axis=0, keepdims=True) for h in range(nh)]
    kd = k * jnp.exp(percol(lambda h: jnp.broadcast_to(gl[h], (n, 1))) - gcb)
    ws = _nn(jnp.concatenate([w, q * egc], axis=0), s)
    v_new = u - own(ws[:rr])
    o = own(ws[rr:]) + _nn(attn, v_new)
    rowh = lax.broadcasted_iota(jnp.int32, (rr, LANE), 0) >> 6
    vexp = jnp.concatenate([jnp.where(rowh == h, v_new, 0.0) for h in range(nh)], axis=1)
    cd = jnp.concatenate([jnp.broadcast_to(jnp.exp(gl[h]), (1, LANE)) for h in range(nh)], axis=1)
    s_new = s * cd + _tn(kd, vexp)
    y = o * lax.rsqrt(jnp.mean(o * o, -1, keepdims=True) + EPS) * nw * _silu(stack(z, 0))
    return jnp.concatenate([y[h * n:(h + 1) * n] for h in range(nh)], axis=1), s_new


def _ssd_chunk(cx, z, dtc, s, convb, alog, dtb, dsk, nw):
    n = SSD_CS
    incl, _ = _tri(n)
    xbc = _silu(cx + convb)
    dt_all = _softplus(dtc + dtb)
    da_all = dt_all * (-jnp.exp(alog))
    cs_all = _xnn(incl.astype(F32), da_all)
    cs_t = cs_all.T
    tot_all = jnp.sum(da_all, axis=0, keepdims=True)
    lane_lo = lax.broadcasted_iota(jnp.int32, (n, LANE), 1) < 64
    ys, s_out = [], []
    for pi in range(4):
        g = pi // 2
        bm = xbc[:, GW + g * LANE:GW + (g + 1) * LANE]
        cm = xbc[:, GW + 256 + g * LANE:GW + 256 + (g + 1) * LANE]
        cbm = _nt(cm, bm)
        xp, s_p = xbc[:, pi * LANE:(pi + 1) * LANE], s[:, pi * LANE:(pi + 1) * LANE]
        a_rows, b_cols, dts, dsks, etot = [], [], [], [], []
        for h in (2 * pi, 2 * pi + 1):
            cs_i = _colsel(cs_all, h)
            lm = jnp.where(incl, jnp.exp(jnp.where(incl, cs_i - _rowsel(cs_t, h), 0.0)), 0.0)
            tot = _colsel(tot_all, h)
            a_rows.append(jnp.concatenate([cbm * lm, cm * jnp.exp(cs_i)], axis=1))
            b_cols.append(bm * jnp.exp(tot - cs_i))
            dts.append(_colsel(dt_all, h))
            dsks.append(_colsel(dsk, h))
            etot.append(jnp.exp(tot))
        xdt = xp * jnp.where(lane_lo, dts[0], dts[1])
        yy = _nn(jnp.concatenate(a_rows, axis=0), jnp.concatenate([xdt, s_p], axis=0))
        ys.append(jnp.where(lane_lo, yy[:n], yy[n:]) + xp * jnp.where(lane_lo[:1], dsks[0], dsks[1]))
        st = _tn(jnp.concatenate(b_cols, axis=1), xdt)
        s_out.append(s_p * jnp.where(lane_lo[:1], etot[0], etot[1]) + jnp.where(lane_lo, st[:n], st[n:]))
    zs = _silu(z)
    out = []
    for g in range(2):
        y = jnp.concatenate(ys[2 * g:2 * g + 2], axis=1) * zs[:, g * 256:(g + 1) * 256]
        out.append(y * lax.rsqrt(jnp.mean(y * y, -1, keepdims=True) + EPS) * nw[:, g * 256:(g + 1) * 256])
    return jnp.concatenate(out, axis=1), jnp.concatenate(s_out, axis=1)


def _swa_block(qg, kp, kc, vp, vc, sink, notfirst, *, g):
    w = ATT_W
    kb = jnp.concatenate([kp, kc], axis=0)
    vb = jnp.concatenate([vp, vc], axis=0)
    qi = lax.broadcasted_iota(jnp.int32, (w, 2 * w), 0)
    kj = lax.broadcasted_iota(jnp.int32, (w, 2 * w), 1)
    rel = qi + w - kj
    mask = (rel >= 0) & (rel < w) & ((kj >= w) | (notfirst > 0.5))
    outs = []
    for r in range(4):
        q = qg[:, r * LANE:(r + 1) * LANE]
        s = jnp.where(mask, _nt(q, kb) * (64 ** -0.5), -1e30)
        snk = _colsel(sink, 4 * g + r)
        mx = lax.stop_gradient(jnp.maximum(jnp.max(s, axis=-1, keepdims=True), snk))
        p = jnp.where(mask, jnp.exp(s - mx), 0.0)
        p = p / (jnp.sum(p, axis=-1, keepdims=True) + jnp.exp(snk - mx))
        outs.append(_nn(p, vb))
    return jnp.concatenate(outs, axis=1)


def _rope(x, cos, sin):
    return x * cos + pltpu.roll(x, 64, 1) * sin


def _unrope(dy, cos, sin):
    return dy * cos + pltpu.roll(dy * sin, 64, 1)


def rowmap(name, fn, rows, consts, outs, accs=(), tile=256):
    t = rows[0][0].shape[0]
    tile = min(tile, t)
    assert t % tile == 0
    nr, nc, no, na = len(rows), len(consts), len(outs), len(accs)

    def body(*refs):
        rv = [r[...] for r in refs[:nr]]
        cv = [r[...] for r in refs[nr:nr + nc]]
        ov, av = fn(rv, cv)
        for ref, o in zip(refs[nr + nc:nr + nc + no], ov):
            ref[...] = o.astype(ref.dtype)
        if na:
            a_refs = refs[nr + nc + no:]

            @pl.when(pl.program_id(0) == 0)
            def _():
                for ref in a_refs:
                    ref[...] = jnp.zeros_like(ref)

            for ref, a in zip(a_refs, av):
                ref[...] += a

    def cspec(off, width):
        assert off % width == 0
        cb = off // width
        return pl.BlockSpec((tile, width), lambda i: (i, cb))

    in_specs = [cspec(off, wd) for (_, off, wd) in rows]
    in_specs += [pl.BlockSpec(c.shape, lambda i: (0, 0)) for c in consts]
    out_specs = [pl.BlockSpec((tile, wd), lambda i: (i, 0)) for (wd, _) in outs]
    out_specs += [pl.BlockSpec(s, lambda i: (0, 0)) for s in accs]
    out_shape = [jax.ShapeDtypeStruct((t, wd), dt) for (wd, dt) in outs]
    out_shape += [jax.ShapeDtypeStruct(s, F32) for s in accs]
    res = pl.pallas_call(
        body, name=name, grid=(t // tile,), in_specs=in_specs, out_specs=out_specs, out_shape=out_shape,
        compiler_params=_cparams("arbitrary"),
    )(*[r[0] for r in rows], *consts)
    return res[:no], res[no:]


def rowmap_vjp(name, f, rows, consts, douts, n_diff, drow_dtypes, tile=256, add_rows=()):
    nr, nd, nadd = len(rows), len(douts), len(add_rows)

    def g(vals, cvals):
        rv, dov, addv = vals[:nr], vals[nr:nr + nd], vals[nr + nd:]
        fixed = rv[n_diff:]
        _, vjp = jax.vjp(lambda r, c: f(list(r) + list(fixed), c), rv[:n_diff], cvals)
        dr, dc = vjp([d.astype(F32) for d in dov])
        dr = list(dr)
        for a in addv:
            dr[0] = dr[0] + a
        return dr, dc

    outs = [(rows[i][2], drow_dtypes[i]) for i in range(n_diff)]
    accs = [c.shape for c in consts]
    return rowmap(name, g, list(rows) + list(douts) + list(add_rows), consts, outs, accs, tile=tile)


def mm_nn(name, a, w, out_dtype, a_fn=None, extra=None, epi=None, tm_cap=1024, wt=False):
    m, k = a.shape
    stacked = w.ndim == 3
    n = (w.shape[-2] if wt else w.shape[-1] * (w.shape[0] if stacked else 1))
    slab = w.shape[-1]
    tm = _pick(m, tm_cap, 16)
    tn = _pick(n if wt else slab, 1024, LANE)
    tk = _pick(slab if wt else k, 2048, LANE)
    nk = k // tk
    per = slab // (tk if wt else tn)
    wdims = ((1,), (1,)) if wt else ((1,), (0,))

    def body(*refs):
        a_ref, w_ref = refs[0], refs[1]
        e_ref = refs[2] if extra is not None else None
        o_ref, acc = refs[-2], refs[-1]
        kk = pl.program_id(2)

        @pl.when(kk == 0)
        def _():
            acc[...] = jnp.zeros_like(acc)

        av = a_ref[...]
        if a_fn is not None:
            av = a_fn(av)
        acc[...] += lax.dot_general(av.astype(BF16), w_ref[...].astype(BF16), (wdims, ((), ())),
                                    preferred_element_type=F32)

        @pl.when(kk == nk - 1)
        def _():
            r = acc[...]
            if epi is not None:
                r = epi(r, e_ref[...])
            o_ref[...] = r.astype(o_ref.dtype)

    if wt:
        w_spec = (pl.BlockSpec((None, tn, tk), lambda i, j, kk: (kk // per, j, kk % per)) if stacked
                  else pl.BlockSpec((tn, tk), lambda i, j, kk: (j, kk)))
    else:
        w_spec = (pl.BlockSpec((None, tk, tn), lambda i, j, kk: (j // per, kk, j % per)) if stacked
                  else pl.BlockSpec((tk, tn), lambda i, j, kk: (kk, j)))
    in_specs = [pl.BlockSpec((tm, tk), lambda i, j, kk: (i, kk)), w_spec]
    ops = [a, w]
    if extra is not None:
        in_specs.append(pl.BlockSpec((tm, tn), lambda i, j, kk: (i, j)))
        ops.append(extra)
    return pl.pallas_call(
        body, name=name, grid=(m // tm, n // tn, nk), in_specs=in_specs,
        out_specs=pl.BlockSpec((tm, tn), lambda i, j, kk: (i, j)),
        out_shape=jax.ShapeDtypeStruct((m, n), out_dtype),
        scratch_shapes=[pltpu.VMEM((tm, tn), F32)],
        compiler_params=_cparams("parallel", "parallel", "arbitrary"),
    )(*ops)


def mm_tn(name, a, b, a_fn=None, col_slabs=1):
    t, m = a.shape
    n = b.shape[1]
    tm, tn, tk = _pick(m, 1024, LANE), _pick(n // col_slabs, 1024, LANE), _pick(t, 2048, 16)
    nk = t // tk
    per = n // col_slabs // tn

    def body(a_ref, b_ref, o_ref, acc):
        kk = pl.program_id(2)

        @pl.when(kk == 0)
        def _():
            acc[...] = jnp.zeros_like(acc)

        av = a_ref[...]
        if a_fn is not None:
            av = a_fn(av)
        acc[...] += lax.dot_general(av.astype(BF16), b_ref[...].astype(BF16), (((0,), (0,)), ((), ())),
                                    preferred_element_type=F32)

        @pl.when(kk == nk - 1)
        def _():
            o_ref[...] = acc[...]

    return pl.pallas_call(
        body, name=name, grid=(m // tm, n // tn, nk),
        in_specs=[pl.BlockSpec((tk, tm), lambda i, j, kk: (kk, i)), pl.BlockSpec((tk, tn), lambda i, j, kk: (kk, j))],
        out_specs=(pl.BlockSpec((tm, tn), lambda i, j, kk: (i, j)) if col_slabs == 1
                   else pl.BlockSpec((None, tm, tn), lambda i, j, kk: (j // per, i, j % per))),
        out_shape=jax.ShapeDtypeStruct((m, n) if col_slabs == 1 else (col_slabs, m, n // col_slabs), F32),
        scratch_shapes=[pltpu.VMEM((tm, tn), F32)],
        compiler_params=_cparams("parallel", "parallel", "arbitrary"),
    )(a, b)


CONV_TILE = 256
HALO = 8


def conv_fwd(name, xs, w):
    t = xs[0][0].shape[0]
    c = xs[0][2]
    kw = w.shape[0]
    tt = min(CONV_TILE, t)
    nx = len(xs)

    def body(*refs):
        cur_refs, prev_refs, w_ref, y_ref = refs[:nx], refs[nx:2 * nx], refs[2 * nx], refs[2 * nx + 1]
        i = pl.program_id(0)
        cur, prev = cur_refs[0][...], prev_refs[0][...]
        for r, p in zip(cur_refs[1:], prev_refs[1:]):
            cur, prev = cur * r[...], prev * p[...]
        prev = jnp.where(i > 0, prev, 0.0)
        xx = jnp.concatenate([prev, cur], axis=0)
        acc = jnp.zeros((tt, c), F32)
        for k in range(kw):
            sh = kw - 1 - k
            rolled = pltpu.roll(xx, sh, 0) if sh else xx
            acc = acc + rolled[HALO:] * w_ref[k:k + 1, :]
        y_ref[...] = acc

    def cur_spec(off):
        cb = off // c
        return pl.BlockSpec((tt, c), lambda i: (i, cb))

    def prev_spec(off):
        cb = off // c
        return pl.BlockSpec((HALO, c), lambda i: (jnp.maximum(i * (tt // HALO) - 1, 0), cb))

    for (_, off, wd) in xs:
        assert wd == c and off % c == 0
    return pl.pallas_call(
        body, name=name, grid=(t // tt,),
        in_specs=[cur_spec(off) for (_, off, _) in xs] + [prev_spec(off) for (_, off, _) in xs]
        + [pl.BlockSpec(w.shape, lambda i: (0, 0))],
        out_specs=pl.BlockSpec((tt, c), lambda i: (i, 0)),
        out_shape=jax.ShapeDtypeStruct((t, c), F32),
        compiler_params=_cparams("arbitrary"),
    )(*[x[0] for x in xs], *[x[0] for x in xs], w)


def conv_bwd(name, dy, xs, w, dx_dtype):
    t = xs[0][0].shape[0]
    c = xs[0][2]
    kw = w.shape[0]
    tt = min(CONV_TILE, t)
    nt = t // tt
    nx = len(xs)
    n = tt + HALO

    def body(*refs):
        dy_ref = refs[0]
        cur_refs, prev_refs = refs[1:1 + nx], refs[1 + nx:1 + 2 * nx]
        w_ref = refs[1 + 2 * nx]
        dx_refs = refs[2 + 2 * nx:2 + 3 * nx]
        dw_ref, carry = refs[2 + 3 * nx], refs[3 + 3 * nx]
        i = pl.program_id(0)

        @pl.when(i == 0)
        def _():
            carry[...] = jnp.zeros_like(carry)
            dw_ref[...] = jnp.zeros_like(dw_ref)

        curs = [r[...] for r in cur_refs]
        cur, prev = curs[0], prev_refs[0][...]
        for cv, p in zip(curs[1:], prev_refs[1:]):
            cur, prev = cur * cv, prev * p[...]
        prev = jnp.where(i < nt - 1, prev, 0.0)
        xx = jnp.concatenate([prev, cur], axis=0)
        dyv = dy_ref[...]
        dpad = jnp.concatenate([jnp.zeros((HALO, c), F32), dyv], axis=0)
        dxx = jnp.zeros((n, c), F32)
        dws = []
        for k in range(kw):
            sh = kw - 1 - k
            dxx = dxx + (pltpu.roll(dpad, n - sh, 0) if sh else dpad) * w_ref[k:k + 1, :]
            rolled = pltpu.roll(xx, sh, 0) if sh else xx
            dws.append(jnp.sum(dyv * rolled[HALO:], axis=0, keepdims=True))
        dw_ref[...] += jnp.concatenate(dws, axis=0)
        dcur = jnp.concatenate([dxx[HALO:tt], dxx[tt:] + carry[...]], axis=0)
        carry[...] = dxx[:HALO]
        for j, ref in enumerate(dx_refs):
            d = dcur
            for jj, cv in enumerate(curs):
                if jj != j:
                    d = d * cv
            ref[...] = d.astype(ref.dtype)

    def cur_spec(off):
        cb = off // c
        return pl.BlockSpec((tt, c), lambda i: (nt - 1 - i, cb))

    def prev_spec(off):
        cb = off // c
        return pl.BlockSpec((HALO, c), lambda i: (jnp.maximum((nt - 1 - i) * (tt // HALO) - 1, 0), cb))

    res = pl.pallas_call(
        body, name=name, grid=(nt,),
        in_specs=[pl.BlockSpec((tt, c), lambda i: (nt - 1 - i, 0))] + [cur_spec(off) for (_, off, _) in xs]
        + [prev_spec(off) for (_, off, _) in xs] + [pl.BlockSpec(w.shape, lambda i: (0, 0))],
        out_specs=[pl.BlockSpec((tt, c), lambda i: (nt - 1 - i, 0)) for _ in xs] + [pl.BlockSpec(w.shape, lambda i: (0, 0))],
        out_shape=[jax.ShapeDtypeStruct((t, c), dx_dtype) for _ in xs] + [jax.ShapeDtypeStruct(w.shape, F32)],
        scratch_shapes=[pltpu.VMEM((HALO, c), F32)],
        compiler_params=_cparams("arbitrary"),
    )(dy, *[x[0] for x in xs], *[x[0] for x in xs], w)
    return res[:nx], res[nx]


def _const_spec(a):
    return pl.BlockSpec(a.shape, lambda i: (0,) * a.ndim)


GDN_STEP = 2


def _gdn_rows(ref, c):
    return ref[c * GDN_CS:(c + 1) * GDN_CS, :]


def gdn_fwd(name, cq, proj, alog, dtb, nw):
    t = cq.shape[0]
    nc = t // GDN_CS
    ns = min(GDN_STEP, nc)
    rows = ns * GDN_CS

    def body(cq_ref, z_ref, ba_ref, alog_ref, dtb_ref, nw_ref, y_ref, ssave_ref, s_scr):
        @pl.when(pl.program_id(0) == 0)
        def _():
            s_scr[...] = jnp.zeros_like(s_scr)

        s = s_scr[...]
        ys, saved = [], []
        for c in range(ns):
            saved.append(s)
            y, s = _gdn_chunk(_gdn_rows(cq_ref, c), _gdn_rows(z_ref, c), _gdn_rows(ba_ref, c), s, alog_ref[...],
                              dtb_ref[...], nw_ref[...])
            ys.append(y)
        y_ref[...] = jnp.concatenate(ys, axis=0)
        ssave_ref[...] = jnp.stack(saved, axis=0)
        s_scr[...] = s

    return pl.pallas_call(
        body, name=name, grid=(nc // ns,),
        in_specs=[pl.BlockSpec((rows, 3 * GW), lambda i: (i, 0)),
                  pl.BlockSpec((rows, GW), lambda i: (i, O_GZ // GW)),
                  pl.BlockSpec((rows, LANE), lambda i: (i, O_GBA // LANE)),
                  _const_spec(alog), _const_spec(dtb), _const_spec(nw)],
        out_specs=[pl.BlockSpec((rows, GW), lambda i: (i, 0)),
                   pl.BlockSpec((ns, LANE, N_GDN * LANE), lambda i: (i, 0, 0))],
        out_shape=[jax.ShapeDtypeStruct((t, GW), F32), jax.ShapeDtypeStruct((nc, LANE, N_GDN * LANE), F32)],
        scratch_shapes=[pltpu.VMEM((LANE, N_GDN * LANE), F32)],
        compiler_params=_cparams("arbitrary"),
    )(cq, proj, proj, alog, dtb, nw)


def gdn_bwd(name, dy, dy_off, cq, proj, ssave, alog, dtb, nw):
    t = cq.shape[0]
    nc = t // GDN_CS
    ns = min(GDN_STEP, nc)
    rows = ns * GDN_CS

    def body(dy_ref, cq_ref, z_ref, ba_ref, ssave_ref, alog_ref, dtb_ref, nw_ref,
             dcq_ref, dz_ref, dba_ref, dalog_ref, ddtb_ref, dnw_ref, ds_scr):
        @pl.when(pl.program_id(0) == 0)
        def _():
            ds_scr[...] = jnp.zeros_like(ds_scr)
            dalog_ref[...] = jnp.zeros_like(dalog_ref)
            ddtb_ref[...] = jnp.zeros_like(ddtb_ref)
            dnw_ref[...] = jnp.zeros_like(dnw_ref)

        ds = ds_scr[...]
        dcq, dzs, dbas = [None] * ns, [None] * ns, [None] * ns
        dal, ddt, dnw = (jnp.zeros(r.shape, F32) for r in (dalog_ref, ddtb_ref, dnw_ref))
        for c in reversed(range(ns)):
            _, vjp = jax.vjp(_gdn_chunk, _gdn_rows(cq_ref, c), _gdn_rows(z_ref, c), _gdn_rows(ba_ref, c), ssave_ref[c],
                             alog_ref[...], dtb_ref[...], nw_ref[...])
            dcq[c], dzs[c], dbas[c], ds, dal_c, ddt_c, dnw_c = vjp((_gdn_rows(dy_ref, c), ds))
            dal, ddt, dnw = dal + dal_c, ddt + ddt_c, dnw + dnw_c
        dcq_ref[...] = jnp.concatenate(dcq, axis=0)
        dz_ref[...] = jnp.concatenate(dzs, axis=0).astype(dz_ref.dtype)
        dba_ref[...] = jnp.concatenate(dbas, axis=0).astype(dba_ref.dtype)
        ds_scr[...] = ds
        dalog_ref[...] += dal
        ddtb_ref[...] += ddt
        dnw_ref[...] += dnw

    rev = lambda i: nc // ns - 1 - i
    return pl.pallas_call(
        body, name=name, grid=(nc // ns,),
        in_specs=[pl.BlockSpec((rows, GW), lambda i: (rev(i), dy_off // GW)),
                  pl.BlockSpec((rows, 3 * GW), lambda i: (rev(i), 0)),
                  pl.BlockSpec((rows, GW), lambda i: (rev(i), O_GZ // GW)),
                  pl.BlockSpec((rows, LANE), lambda i: (rev(i), O_GBA // LANE)),
                  pl.BlockSpec((ns, LANE, N_GDN * LANE), lambda i: (rev(i), 0, 0)),
                  _const_spec(alog), _const_spec(dtb), _const_spec(nw)],
        out_specs=[pl.BlockSpec((rows, 3 * GW), lambda i: (rev(i), 0)),
                   pl.BlockSpec((rows, GW), lambda i: (rev(i), 0)),
                   pl.BlockSpec((rows, LANE), lambda i: (rev(i), 0)),
                   _const_spec(alog), _const_spec(dtb), _const_spec(nw)],
        out_shape=[jax.ShapeDtypeStruct((t, 3 * GW), F32), jax.ShapeDtypeStruct((t, GW), BF16),
                   jax.ShapeDtypeStruct((t, LANE), BF16), jax.ShapeDtypeStruct(alog.shape, F32),
                   jax.ShapeDtypeStruct(dtb.shape, F32), jax.ShapeDtypeStruct(nw.shape, F32)],
        scratch_shapes=[pltpu.VMEM((LANE, N_GDN * LANE), F32)],
        compiler_params=_cparams("arbitrary"),
    )(dy, cq, proj, proj, ssave, alog, dtb, nw)


def ssd_fwd(name, cx, proj, convb, alog, dtb, dsk, nw):
    t = cx.shape[0]
    nc = t // SSD_CS

    def body(cx_ref, z_ref, dt_ref, cb_ref, alog_ref, dtb_ref, dsk_ref, nw_ref, y_ref, ssave_ref, s_scr):
        @pl.when(pl.program_id(0) == 0)
        def _():
            s_scr[...] = jnp.zeros_like(s_scr)

        s_in = s_scr[...]
        y, s_new = _ssd_chunk(cx_ref[...], z_ref[...], dt_ref[...], s_in, cb_ref[...], alog_ref[...], dtb_ref[...],
                              dsk_ref[...], nw_ref[...])
        y_ref[...] = y
        ssave_ref[0] = s_in
        s_scr[...] = s_new

    return pl.pallas_call(
        body, name=name, grid=(nc,),
        in_specs=[pl.BlockSpec((SSD_CS, 1024), lambda i: (i, 0)),
                  pl.BlockSpec((SSD_CS, GW), lambda i: (i, O_SZ // GW)),
                  pl.BlockSpec((SSD_CS, LANE), lambda i: (i, O_SDT // LANE)),
                  _const_spec(convb), _const_spec(alog), _const_spec(dtb), _const_spec(dsk), _const_spec(nw)],
        out_specs=[pl.BlockSpec((SSD_CS, GW), lambda i: (i, 0)),
                   pl.BlockSpec((1, LANE, GW), lambda i: (i, 0, 0))],
        out_shape=[jax.ShapeDtypeStruct((t, GW), F32), jax.ShapeDtypeStruct((nc, LANE, GW), F32)],
        scratch_shapes=[pltpu.VMEM((LANE, GW), F32)],
        compiler_params=_cparams("arbitrary"),
    )(cx, proj, proj, convb, alog, dtb, dsk, nw)


def ssd_bwd(name, dy, dy_off, cx, proj, ssave, convb, alog, dtb, dsk, nw):
    t = cx.shape[0]
    nc = t // SSD_CS

    def body(dy_ref, cx_ref, z_ref, dt_ref, ssave_ref, cb_ref, alog_ref, dtb_ref, dsk_ref, nw_ref,
             dcx_ref, dz_ref, ddt_ref, dcb_ref, dalog_ref, ddtb_ref, ddsk_ref, dnw_ref, ds_scr):
        @pl.when(pl.program_id(0) == 0)
        def _():
            ds_scr[...] = jnp.zeros_like(ds_scr)
            for r in (dcb_ref, dalog_ref, ddtb_ref, ddsk_ref, dnw_ref):
                r[...] = jnp.zeros_like(r)

        _, vjp = jax.vjp(_ssd_chunk, cx_ref[...], z_ref[...], dt_ref[...], ssave_ref[0], cb_ref[...], alog_ref[...],
                         dtb_ref[...], dsk_ref[...], nw_ref[...])
        dcx, dz, ddt, ds, dcb, dal, ddtb, ddsk, dnw = vjp((dy_ref[...], ds_scr[...]))
        dcx_ref[...] = dcx
        dz_ref[...] = dz.astype(dz_ref.dtype)
        ddt_ref[...] = ddt.astype(ddt_ref.dtype)
        ds_scr[...] = ds
        dcb_ref[...] += dcb
        dalog_ref[...] += dal
        ddtb_ref[...] += ddtb
        ddsk_ref[...] += ddsk
        dnw_ref[...] += dnw

    rev = lambda i: nc - 1 - i
    return pl.pallas_call(
        body, name=name, grid=(nc,),
        in_specs=[pl.BlockSpec((SSD_CS, GW), lambda i: (rev(i), dy_off // GW)),
                  pl.BlockSpec((SSD_CS, 1024), lambda i: (rev(i), 0)),
                  pl.BlockSpec((SSD_CS, GW), lambda i: (rev(i), O_SZ // GW)),
                  pl.BlockSpec((SSD_CS, LANE), lambda i: (rev(i), O_SDT // LANE)),
                  pl.BlockSpec((1, LANE, GW), lambda i: (rev(i), 0, 0)),
                  _const_spec(convb), _const_spec(alog), _const_spec(dtb), _const_spec(dsk), _const_spec(nw)],
        out_specs=[pl.BlockSpec((SSD_CS, 1024), lambda i: (rev(i), 0)),
                   pl.BlockSpec((SSD_CS, GW), lambda i: (rev(i), 0)),
                   pl.BlockSpec((SSD_CS, LANE), lambda i: (rev(i), 0)),
                   _const_spec(convb), _const_spec(alog), _const_spec(dtb), _const_spec(dsk), _const_spec(nw)],
        out_shape=[jax.ShapeDtypeStruct((t, 1024), F32), jax.ShapeDtypeStruct((t, GW), BF16),
                   jax.ShapeDtypeStruct((t, LANE), BF16), jax.ShapeDtypeStruct(convb.shape, F32),
                   jax.ShapeDtypeStruct(alog.shape, F32), jax.ShapeDtypeStruct(dtb.shape, F32),
                   jax.ShapeDtypeStruct(dsk.shape, F32), jax.ShapeDtypeStruct(nw.shape, F32)],
        scratch_shapes=[pltpu.VMEM((LANE, GW), F32)],
        compiler_params=_cparams("arbitrary"),
    )(dy, cx, proj, proj, ssave, convb, alog, dtb, dsk, nw)


def _swa_specs(nb, rev):
    w = ATT_W
    cur = lambda i: (nb - 1 - i) if rev else i
    prv = lambda i: jnp.maximum(cur(i) - 1, 0)
    return [pl.BlockSpec((w, 1024), lambda i: (cur(i), O_AQ // 1024)),
            pl.BlockSpec((w, 256), lambda i: (cur(i), O_AK // 256)),
            pl.BlockSpec((w, 256), lambda i: (prv(i), O_AK // 256)),
            pl.BlockSpec((w, 256), lambda i: (cur(i), O_AV // 256)),
            pl.BlockSpec((w, 256), lambda i: (prv(i), O_AV // 256)),
            pl.BlockSpec((w, LANE), lambda i: (cur(i), 0)), pl.BlockSpec((w, LANE), lambda i: (prv(i), 0)),
            pl.BlockSpec((w, LANE), lambda i: (cur(i), 0)), pl.BlockSpec((w, LANE), lambda i: (prv(i), 0))]


def swa_fwd(name, proj, cos, sin, sinks):
    t = proj.shape[0]
    nb = t // ATT_W

    def body(q_ref, kc_ref, kp_ref, vc_ref, vp_ref, cc_ref, cp_ref, sc_ref, sp_ref, snk_ref, y_ref):
        notfirst = jnp.where(pl.program_id(0) > 0, 1.0, 0.0).astype(F32)
        cc, cp, sc, sp = cc_ref[...], cp_ref[...], sc_ref[...], sp_ref[...]
        outs = []
        for g in range(2):
            sl = slice(g * LANE, (g + 1) * LANE)
            qg = jnp.concatenate([_rope(q_ref[:, (4 * g + r) * LANE:(4 * g + r + 1) * LANE], cc, sc)
                                  for r in range(4)], axis=1)
            outs.append(_swa_block(qg, _rope(kp_ref[:, sl], cp, sp), _rope(kc_ref[:, sl], cc, sc), vp_ref[:, sl],
                                   vc_ref[:, sl], snk_ref[...], notfirst, g=g))
        y_ref[...] = jnp.concatenate(outs, axis=1)

    return pl.pallas_call(
        body, name=name, grid=(nb,),
        in_specs=_swa_specs(nb, False) + [_const_spec(sinks)],
        out_specs=pl.BlockSpec((ATT_W, 1024), lambda i: (i, 0)),
        out_shape=jax.ShapeDtypeStruct((t, 1024), F32),
        compiler_params=_cparams("arbitrary"),
    )(proj, proj, proj, proj, proj, cos, cos, sin, sin, sinks)


def swa_bwd(name, dy, dy_off, proj, cos, sin, sinks):
    t = proj.shape[0]
    nb = t // ATT_W

    def body(dy_ref, q_ref, kc_ref, kp_ref, vc_ref, vp_ref, cc_ref, cp_ref, sc_ref, sp_ref, snk_ref,
             dq_ref, dk_ref, dv_ref, dsnk_ref, ck_scr, cv_scr):
        i = pl.program_id(0)

        @pl.when(i == 0)
        def _():
            ck_scr[...] = jnp.zeros_like(ck_scr)
            cv_scr[...] = jnp.zeros_like(cv_scr)
            dsnk_ref[...] = jnp.zeros_like(dsnk_ref)

        notfirst = jnp.where(i < nb - 1, 1.0, 0.0).astype(F32)
        cc, cp, sc, sp = cc_ref[...], cp_ref[...], sc_ref[...], sp_ref[...]
        ck_in, cv_in = ck_scr[...], cv_scr[...]
        dq, dk, dv, ck, cv, dsnk = [], [], [], [], [], jnp.zeros(dsnk_ref.shape, F32)
        for g in range(2):
            sl = slice(g * LANE, (g + 1) * LANE)
            qg = jnp.concatenate([_rope(q_ref[:, (4 * g + r) * LANE:(4 * g + r + 1) * LANE], cc, sc)
                                  for r in range(4)], axis=1)
            _, vjp = jax.vjp(functools.partial(_swa_block, g=g), qg, _rope(kp_ref[:, sl], cp, sp),
                             _rope(kc_ref[:, sl], cc, sc), vp_ref[:, sl], vc_ref[:, sl], snk_ref[...], notfirst)
            dqg, dkp, dkc, dvp, dvc, dsnk_g, _ = vjp(dy_ref[:, g * GW:(g + 1) * GW])
            dq += [_unrope(dqg[:, r * LANE:(r + 1) * LANE], cc, sc) for r in range(4)]
            dk.append(_unrope(dkc, cc, sc) + ck_in[:, sl])
            dv.append(dvc + cv_in[:, sl])
            ck.append(_unrope(dkp, cp, sp))
            cv.append(dvp)
            dsnk = dsnk + dsnk_g
        dq_ref[...] = jnp.concatenate(dq, axis=1).astype(dq_ref.dtype)
        dk_ref[...] = jnp.concatenate(dk, axis=1).astype(dk_ref.dtype)
        dv_ref[...] = jnp.concatenate(dv, axis=1).astype(dv_ref.dtype)
        ck_scr[...] = jnp.concatenate(ck, axis=1)
        cv_scr[...] = jnp.concatenate(cv, axis=1)
        dsnk_ref[...] += dsnk

    rev = lambda i: nb - 1 - i
    return pl.pallas_call(
        body, name=name, grid=(nb,),
        in_specs=[pl.BlockSpec((ATT_W, 1024), lambda i: (rev(i), dy_off // 1024))] + _swa_specs(nb, True)
        + [_const_spec(sinks)],
        out_specs=[pl.BlockSpec((ATT_W, 1024), lambda i: (rev(i), 0)),
                   pl.BlockSpec((ATT_W, 256), lambda i: (rev(i), 0)),
                   pl.BlockSpec((ATT_W, 256), lambda i: (rev(i), 0)), _const_spec(sinks)],
        out_shape=[jax.ShapeDtypeStruct((t, 1024), BF16), jax.ShapeDtypeStruct((t, 256), BF16),
                   jax.ShapeDtypeStruct((t, 256), BF16), jax.ShapeDtypeStruct(sinks.shape, F32)],
        scratch_shapes=[pltpu.VMEM((ATT_W, 256), F32), pltpu.VMEM((ATT_W, 256), F32)],
        compiler_params=_cparams("arbitrary"),
    )(dy, proj, proj, proj, proj, proj, cos, cos, sin, sin, sinks)


def _me():
    return lax.axis_index("x"), lax.axis_index("y"), lax.axis_index("c")


ANY = pl.BlockSpec(memory_space=pl.ANY)


def allgather8(name, buf):
    r = buf.shape[0]

    def body(x_ref, out_ref, send_sems, recv_sems, local_sem):
        x, y, c = _me()
        me = 4 * x + 2 * y + c
        mine = pltpu.make_async_copy(x_ref, out_ref.at[me], local_sem)
        mine.start()
        copies = []
        for k in range(1, 8):
            bx, by, bc = (k >> 2) & 1, (k >> 1) & 1, k & 1
            to = ((1 - x) if bx else x, (1 - y) if by else y, (1 - c) if bc else c)
            cp = pltpu.make_async_remote_copy(src_ref=x_ref, dst_ref=out_ref.at[me], send_sem=send_sems.at[k - 1],
                                              recv_sem=recv_sems.at[k - 1], device_id=to, device_id_type=MESH)
            cp.start()
            copies.append(cp)
        for cp in copies:
            cp.wait_recv()
        for cp in copies:
            cp.wait_send()
        mine.wait()

    return pl.pallas_call(
        body, name=name, out_shape=jax.ShapeDtypeStruct((8, r, LANE), buf.dtype),
        in_specs=[pl.BlockSpec(memory_space=pltpu.VMEM)], out_specs=pl.BlockSpec(memory_space=pltpu.VMEM),
        scratch_shapes=[pltpu.SemaphoreType.DMA((7,)), pltpu.SemaphoreType.DMA((7,)), pltpu.SemaphoreType.DMA],
        compiler_params=pltpu.CompilerParams(vmem_limit_bytes=VMEM_LIMIT),
    )(buf)


def _sems(n):
    return [pltpu.SemaphoreType.DMA((n,)), pltpu.SemaphoreType.DMA((n,))]


def allgather_chips(name, shards):
    n = len(shards)

    def body(*refs):
        x_refs, out_refs, send_sems, recv_sems = refs[:n], refs[n:2 * n], refs[2 * n], refs[2 * n + 1]
        x, y, c = _me()
        chip = 2 * x + y
        chips = [(1 - x, y), (x, 1 - y), (1 - x, 1 - y)]

        def rc(src, dst, k, to):
            return pltpu.make_async_remote_copy(src_ref=src, dst_ref=dst, send_sem=send_sems.at[k],
                                                recv_sem=recv_sems.at[k], device_id=to, device_id_type=MESH)

        halves = [pl.ds(pl.multiple_of(c * (s.shape[0] // 2), 16), s.shape[0] // 2) for s in shards]
        own, first = [], {}
        for i in range(n):
            cp = rc(x_refs[i], out_refs[i].at[chip], 7 * i + 6, (x, y, 1 - c))
            cp.start()
            own.append(cp)
            for k, (cx, cy) in enumerate(chips):
                cp = rc(x_refs[i].at[halves[i]], out_refs[i].at[chip, halves[i]], 7 * i + k, (cx, cy, c))
                cp.start()
                first[i, k] = cp
        passed = []
        for i in range(n):
            for k, (cx, cy) in enumerate(chips):
                first[i, k].wait_recv()
                src = out_refs[i].at[2 * cx + cy, halves[i]]
                cp = rc(src, src, 7 * i + 3 + k, (x, y, 1 - c))
                cp.start()
                passed.append(cp)
        for cp in passed + own:
            cp.wait_recv()
        for cp in list(first.values()) + passed + own:
            cp.wait_send()

    for s in shards:
        assert s.shape[0] % 32 == 0
    return pl.pallas_call(
        body, name=name, out_shape=[jax.ShapeDtypeStruct((4,) + s.shape, s.dtype) for s in shards],
        in_specs=[ANY] * n, out_specs=[ANY] * n, scratch_shapes=_sems(7 * n),
    )(*shards)


def sibling_swap_halves(name, gs):
    n = len(gs)

    def body(*refs):
        g_refs, out_refs, send_sems, recv_sems = refs[:n], refs[n:2 * n], refs[2 * n], refs[2 * n + 1]
        x, y, c = _me()
        cps = [pltpu.make_async_remote_copy(src_ref=g_refs[i].at[:, 1 - c], dst_ref=out_refs[i],
                                            send_sem=send_sems.at[i], recv_sem=recv_sems.at[i],
                                            device_id=(x, y, 1 - c), device_id_type=MESH) for i in range(n)]
        for cp in cps:
            cp.start()
        for cp in cps:
            cp.wait_recv()
        for cp in cps:
            cp.wait_send()

    return pl.pallas_call(
        body, name=name, out_shape=[jax.ShapeDtypeStruct((4,) + g.shape[2:], g.dtype) for g in gs],
        in_specs=[ANY] * n, out_specs=[ANY] * n, scratch_shapes=_sems(n),
    )(*gs)


def chips_exchange(name, ps):
    n = len(ps)

    def body(*refs):
        p_refs, out_refs, send_sems, recv_sems = refs[:n], refs[n:2 * n], refs[2 * n], refs[2 * n + 1]
        x, y, c = _me()
        chip = 2 * x + y
        cps = []
        for i in range(n):
            for k, (cx, cy) in enumerate([(1 - x, y), (x, 1 - y), (1 - x, 1 - y)]):
                cps.append(pltpu.make_async_remote_copy(
                    src_ref=p_refs[i].at[2 * cx + cy], dst_ref=out_refs[i].at[chip], send_sem=send_sems.at[3 * i + k],
                    recv_sem=recv_sems.at[3 * i + k], device_id=(cx, cy, c), device_id_type=MESH))
        for cp in cps:
            cp.start()
        for cp in cps:
            cp.wait_recv()
        for cp in cps:
            cp.wait_send()

    return pl.pallas_call(
        body, name=name, out_shape=[jax.ShapeDtypeStruct(p.shape, p.dtype) for p in ps],
        in_specs=[ANY] * n, out_specs=[ANY] * n, scratch_shapes=_sems(3 * n),
    )(*ps)


def sibling_send(name, ss):
    n = len(ss)

    def body(*refs):
        s_refs, out_refs, send_sems, recv_sems = refs[:n], refs[n:2 * n], refs[2 * n], refs[2 * n + 1]
        x, y, c = _me()
        cps = [pltpu.make_async_remote_copy(src_ref=s_refs[i], dst_ref=out_refs[i], send_sem=send_sems.at[i],
                                            recv_sem=recv_sems.at[i], device_id=(x, y, 1 - c), device_id_type=MESH)
               for i in range(n)]
        for cp in cps:
            cp.start()
        for cp in cps:
            cp.wait_recv()
        for cp in cps:
            cp.wait_send()

    return pl.pallas_call(
        body, name=name, out_shape=[jax.ShapeDtypeStruct(s.shape, s.dtype) for s in ss],
        in_specs=[ANY] * n, out_specs=[ANY] * n, scratch_shapes=_sems(n),
    )(*ss)


def _rs_tile(rh, w):
    return _pick(rh, max(8, (1 << 19) // w // 8 * 8), 8)


def add_half(name, g, recv, idx):
    _, _, rh, w = g.shape
    tile = _rs_tile(rh, w)

    def body(s_ref, g_ref, r_ref, o_ref):
        o_ref[...] = (g_ref[...] + r_ref[...]).astype(o_ref.dtype)

    return pl.pallas_call(
        body, name=name,
        grid_spec=pltpu.PrefetchScalarGridSpec(
            num_scalar_prefetch=1, grid=(4, rh // tile),
            in_specs=[pl.BlockSpec((None, None, tile, w), lambda s, i, ix: (s, ix[1], i, 0)),
                      pl.BlockSpec((None, tile, w), lambda s, i, ix: (s, i, 0))],
            out_specs=pl.BlockSpec((None, tile, w), lambda s, i, ix: (s, i, 0))),
        out_shape=jax.ShapeDtypeStruct((4, rh, w), BF16),
        compiler_params=_cparams("parallel", "parallel"),
    )(idx, g, recv)


def rs_sum(name, g, recv1, recv2, idx):
    _, _, rh, w = g.shape
    tile = _rs_tile(rh, w)

    def body(s_ref, g_ref, r1_ref, a_ref, b_ref, c_ref, o_ref):
        o_ref[...] = (g_ref[...] + r1_ref[...]) + (a_ref[...].astype(F32) + b_ref[...].astype(F32)
                                                    + c_ref[...].astype(F32))

    def other(k):
        return pl.BlockSpec((None, tile, w), lambda i, ix: (jnp.bitwise_xor(ix[0], k), i, 0))

    return pl.pallas_call(
        body, name=name,
        grid_spec=pltpu.PrefetchScalarGridSpec(
            num_scalar_prefetch=1, grid=(rh // tile,),
            in_specs=[pl.BlockSpec((None, None, tile, w), lambda i, ix: (ix[0], ix[1], i, 0)),
                      pl.BlockSpec((None, tile, w), lambda i, ix: (ix[0], i, 0)), other(1), other(2), other(3)],
            out_specs=pl.BlockSpec((tile, w), lambda i, ix: (i, 0))),
        out_shape=jax.ShapeDtypeStruct((rh, w), F32),
        compiler_params=_cparams("parallel"),
    )(idx, g, recv1, recv2, recv2, recv2)


def sum_slabs(name, a):
    n, r, _ = a.shape
    tile = _pick(r, 2048, 8)

    def body(a_ref, o_ref):
        acc = a_ref[0]
        for s in range(1, n):
            acc = acc + a_ref[s]
        o_ref[...] = acc

    return pl.pallas_call(
        body, name=name, grid=(r // tile,),
        in_specs=[pl.BlockSpec((n, tile, LANE), lambda i: (0, i, 0))],
        out_specs=pl.BlockSpec((tile, LANE), lambda i: (i, 0)),
        out_shape=jax.ShapeDtypeStruct((r, LANE), F32),
        compiler_params=_cparams("parallel"),
    )(a)


def f_premod(rv, cv):
    (x,), (w, scale, shift) = rv, cv
    y = x * lax.rsqrt(jnp.mean(x * x, -1, keepdims=True) + EPS) * w
    return [y * (1.0 + scale) + shift]


def f_postres(rv, cv):
    (x, y), (w, gate) = rv, cv
    return [x + gate * (y * lax.rsqrt(jnp.mean(y * y, -1, keepdims=True) + EPS) * w)]


def f_adamw(rv, cv):
    w, g, m, v = rv
    m = B1 * m + (1.0 - B1) * g
    v = B2 * v + (1.0 - B2) * jnp.square(g)
    m_hat = m / (1.0 - B1 ** STEP)
    v_hat = v / (1.0 - B2 ** STEP)
    return [-LR * (m_hat / (jnp.sqrt(v_hat) + AEPS) + WD * w), m, v], []


def adamw(name, w, g, m, v):
    r, wd = w.shape
    tile = r
    while tile * wd * 4 * 14 > 24 * 1024 * 1024 and tile % 16 == 0:
        tile //= 2
    (d, m2, v2), _ = rowmap(name, f_adamw, [(a, 0, wd) for a in (w, g, m, v)], [], [(wd, F32)] * 3, tile=tile)
    return d, m2, v2


_IN_SIZES = (512, 512, 512, 512, 4, 4, 512, 1024, 8, 512, 128, 128, 512, 512, 512)


def _relayout_w_in(w):
    d = w.shape[0]
    s = np.cumsum((0,) + _IN_SIZES)
    gq, gk, gv, gz, gb, ga, sz, sx, sdt, aq, ak, av, cb, cc, ch = [w[:, s[i]:s[i + 1]] for i in range(15)]

    def rope_pad(t, nh):
        return jnp.pad(t.reshape(d, nh, 2, 32), ((0, 0), (0, 0), (0, 0), (0, 32))).reshape(d, nh * LANE)

    def v_pad(t, nh):
        return jnp.pad(t.reshape(d, nh, 64), ((0, 0), (0, 0), (0, 64))).reshape(d, nh * LANE)

    def pad128(t):
        return jnp.pad(t, ((0, 0), (0, LANE - t.shape[1])))

    return jnp.concatenate([gq, gk, gv, cb, cc, ch, sx, rope_pad(aq, 8), gz, sz, rope_pad(ak, 2), v_pad(av, 2),
                            pad128(jnp.concatenate([gb, ga], axis=1)), pad128(sdt)], axis=1)


def _unlayout_g_in(g):
    d = g.shape[0]

    def rope_unpad(t, nh):
        return t.reshape(d, nh, 2, 64)[..., :32].reshape(d, nh * 64)

    def v_unpad(t, nh):
        return t.reshape(d, nh, LANE)[..., :64].reshape(d, nh * 64)

    return jnp.concatenate([g[:, 0:1536], g[:, O_GZ:O_GZ + 512], g[:, O_GBA:O_GBA + 8], g[:, O_SZ:O_SZ + 512],
                            g[:, O_XBC:O_XBC + 1024], g[:, O_SDT:O_SDT + 8], rope_unpad(g[:, O_AQ:O_AQ + 1024], 8),
                            rope_unpad(g[:, O_AK:O_AK + 256], 2), v_unpad(g[:, O_AV:O_AV + 256], 2),
                            g[:, O_CB:O_CB + 1536]], axis=1)


def _relayout_w_out(w):
    d = w.shape[1]
    yc = jnp.pad(w[1024:1536].reshape(8, 64, d), ((0, 0), (0, 64), (0, 0))).reshape(1024, d)
    return jnp.concatenate([w[:1024], yc, w[1536:]], axis=0)


def _unlayout_g_out(g):
    d = g.shape[1]
    return jnp.concatenate([g[:1024], g[1024:2048].reshape(8, LANE, d)[:, :64].reshape(512, d), g[2048:]], axis=0)


def _pack(arrs):
    rows, spec, off = [], [], 0
    for a in arrs:
        n = int(np.prod(a.shape))
        nr = -(-n // LANE)
        rows.append(jnp.pad(a.reshape(-1).astype(F32), (0, nr * LANE - n)).reshape(nr, LANE))
        spec.append((off, a.shape))
        off += nr
    pad = (-off) % 8
    if pad:
        rows.append(jnp.zeros((pad, LANE), F32))
    return jnp.concatenate(rows, axis=0), spec


def _unpack(buf, spec):
    out = []
    for off, shape in spec:
        n = int(np.prod(shape))
        nr = -(-n // LANE)
        out.append(buf[off:off + nr].reshape(-1)[:n].reshape(shape))
    return out


def _row128(v):
    return jnp.pad(v.astype(F32), (0, LANE - v.shape[0])).reshape(1, LANE)


BIG = ("w_in", "w_out", "w_up", "w_down")
SMALL_REPL = ("ada_b", "norm_pre_mix", "norm_post_mix", "norm_pre_mlp", "norm_post_mlp", "gdn_a_log", "gdn_dt_bias",
              "gdn_norm_w", "ssm_conv_b", "ssm_a_log", "ssm_dt_bias", "ssm_d", "ssm_norm_w", "attn_sinks")
SMALL_SHARD = ("gdn_conv_w", "ssm_conv_w", "sc_conv_w")
WEIGHTS = ("ada_w", "ada_b", "norm_pre_mix", "norm_post_mix", "norm_pre_mlp", "norm_post_mlp", "w_in", "w_out",
           "gdn_conv_w", "gdn_a_log", "gdn_dt_bias", "gdn_norm_w", "ssm_conv_w", "ssm_conv_b", "ssm_a_log",
           "ssm_dt_bias", "ssm_d", "ssm_norm_w", "attn_sinks", "sc_conv_w", "w_up", "w_down")


def _step(p, m, v, x, c, positions, loss_target):
    d = x.shape[2]
    xi, yi, ci = _me()
    chip = 2 * xi + yi
    me = 4 * xi + 2 * yi + ci
    x0 = x[0]
    tgt = loss_target[0]

    buf, spec = _pack([c[0], p["gdn_conv_w"], p["ssm_conv_w"], p["sc_conv_w"]])
    got = allgather8("ag_small_in", buf)
    per_dev = [_unpack(got[k], spec) for k in range(8)]
    c_all = jnp.stack([pd[0] for pd in per_dev], axis=0)
    conv_full = [jnp.concatenate([per_dev[4 * a + 2 * b][1 + j] for a in range(2) for b in range(2)], axis=-1)
                 for j in range(3)]
    gdn_conv_w, ssm_conv_w, sc_conv_w = conv_full

    c_pad = jnp.pad(c_all, ((0, 8), (0, 0)))
    silu_bf = lambda a: _silu(a).astype(BF16)
    nsh = p["ada_w"].shape[2]
    mods = []
    for l in range(DEPTH):
        mo = mm_nn(f"ada_fwd{l}", c_pad, p["ada_w"][l], F32, a_fn=silu_bf)
        mods.append(mo[:8] + lax.dynamic_slice(p["ada_b"][l], (chip * nsh,), (nsh,))[None])
    buf, spec = _pack([jnp.stack(mods, axis=0)])
    got = allgather8("ag_mod", buf)
    mod_sh = [_unpack(got[4 * a + 2 * b], spec)[0] for a in range(2) for b in range(2)]
    mod_all = jnp.concatenate(mod_sh, axis=-1)
    mod = lax.dynamic_index_in_dim(mod_all, me, axis=1, keepdims=False)

    items = [(l, n) for l in range(DEPTH) for n in BIG]
    gathered = dict(zip(items, allgather_chips("ag_weights", [p[n][l].astype(BF16) for (l, n) in items])))
    wfull = []
    for l in range(DEPTH):
        w_in = jnp.concatenate([gathered[l, "w_in"][s] for s in range(4)], axis=1)
        wfull.append(dict(w_in=_relayout_w_in(w_in), w_out=_relayout_w_out(gathered[l, "w_out"].reshape(-1, d)),
                          w_up=gathered[l, "w_up"], w_down=gathered[l, "w_down"].reshape(-1, d)))

    inv_freq = ROPE_THETA ** (-jnp.arange(0, 64, 2, dtype=F32) / 64)
    ang = positions[0].astype(F32)[:, None] * inv_freq
    z32 = jnp.zeros_like(ang)
    cos_t = jnp.concatenate([jnp.cos(ang), z32, jnp.cos(ang), z32], axis=1)
    sin_t = jnp.concatenate([-jnp.sin(ang), z32, jnp.sin(ang), z32], axis=1)

    relu2 = lambda a: jnp.square(jnp.maximum(a, 0))

    def layer_consts(l):
        row = lambda v_: v_.reshape(1, -1).astype(F32)
        return dict(
            alog=_row128(p["gdn_a_log"][l]), gdtb=_row128(p["gdn_dt_bias"][l]), gnw=row(p["gdn_norm_w"][l]),
            scb=row(p["ssm_conv_b"][l]), salog=_row128(p["ssm_a_log"][l]), sdtb=_row128(p["ssm_dt_bias"][l]),
            sdsk=_row128(p["ssm_d"][l]), snw=row(p["ssm_norm_w"][l]), sinks=_row128(p["attn_sinks"][l]))

    saved = []
    xc = x0
    for l in range(DEPTH):
        w = wfull[l]
        k = layer_consts(l)
        md = [mod[l, j * d:(j + 1) * d].reshape(1, d) for j in range(6)]
        npre, npost = p["norm_pre_mix"][l].reshape(1, d), p["norm_post_mix"][l].reshape(1, d)
        npre2, npost2 = p["norm_pre_mlp"][l].reshape(1, d), p["norm_post_mlp"][l].reshape(1, d)
        (h,), _ = rowmap(f"premix{l}", lambda rv, cv: (f_premod(rv, cv), []), [(xc, 0, d)], [npre, md[1], md[0]],
                         [(d, BF16)])
        proj = mm_nn(f"inproj{l}", h, w["w_in"], F32)
        cq = conv_fwd(f"gdn_conv{l}", [(proj, O_GQKV, 1536)], gdn_conv_w[l])
        ya, gs = gdn_fwd(f"gdn{l}", cq, proj, k["alog"], k["gdtb"], k["gnw"])
        cx = conv_fwd(f"ssm_conv{l}", [(proj, O_XBC, 1024)], ssm_conv_w[l])
        yb, ss = ssd_fwd(f"ssd{l}", cx, proj, k["scb"], k["salog"], k["sdtb"], k["sdsk"], k["snw"])
        yc = swa_fwd(f"swa{l}", proj, cos_t, sin_t, k["sinks"])
        cd = conv_fwd(f"sc_conv{l}", [(proj, O_CC, GW), (proj, O_CH, GW)], sc_conv_w[l])
        (ycat,), _ = rowmap(f"ycat{l}", lambda rv, cv: ([jnp.concatenate([rv[0], rv[1], rv[2], rv[3] * rv[4]], 1)], []),
                            [(ya, 0, GW), (yb, 0, GW), (yc, 0, 1024), (proj, O_CB, GW), (cd, 0, GW)], [],
                            [(YCAT, BF16)])
        y = mm_nn(f"outproj{l}", ycat, w["w_out"], F32)
        (x1,), _ = rowmap(f"postmix{l}", lambda rv, cv: (f_postres(rv, cv), []), [(xc, 0, d), (y, 0, d)],
                          [npost, md[2]], [(d, F32)])
        (h2,), _ = rowmap(f"premlp{l}", lambda rv, cv: (f_premod(rv, cv), []), [(x1, 0, d)], [npre2, md[4], md[3]],
                          [(d, BF16)])
        u = mm_nn(f"up{l}", h2, w["w_up"], BF16)
        y2 = mm_nn(f"down{l}", u, w["w_down"], F32, a_fn=relu2)
        (x2,), _ = rowmap(f"postmlp{l}", lambda rv, cv: (f_postres(rv, cv), []), [(x1, 0, d), (y2, 0, d)],
                          [npost2, md[5]], [(d, F32)])
        saved.append(dict(x=xc, h=h, proj=proj, cq=cq, gs=gs, cx=cx, ss=ss, cd=cd, ycat=ycat, y=y, x1=x1, h2=h2, u=u,
                          y2=y2, md=md, k=k))
        xc = x2

    def f_loss(rv, cv):
        err = rv[0] - rv[1]
        part = 0.5 * jnp.sum(jnp.mean(err * err, axis=-1, keepdims=True), axis=0, keepdims=True)
        return [err * (1.0 / d)], [jnp.broadcast_to(part, (1, LANE))]

    (dx,), (loss_part,) = rowmap("loss", f_loss, [(xc, 0, d), (tgt, 0, d)], [], [(d, F32)], [(1, LANE)])

    gsmall = [None] * DEPTH
    gbig = [None] * DEPTH
    dmods = [None] * DEPTH
    for l in reversed(range(DEPTH)):
        w, s = wfull[l], saved[l]
        k, md = s["k"], s["md"]
        npre, npost = p["norm_pre_mix"][l].reshape(1, d), p["norm_post_mix"][l].reshape(1, d)
        npre2, npost2 = p["norm_pre_mlp"][l].reshape(1, d), p["norm_post_mlp"][l].reshape(1, d)
        (dx1a, dy2), (dnpost2, dgate_m) = rowmap_vjp(f"postmlp_b{l}", f_postres, [(s["x1"], 0, d), (s["y2"], 0, d)],
                                                      [npost2, md[5]], [(dx, 0, d)], 2, [F32, BF16])
        du = mm_nn(f"down_b{l}", dy2, w["w_down"], BF16, extra=s["u"], wt=True,
                   epi=lambda r, uu: r * (2.0 * jnp.maximum(uu.astype(F32), 0.0)))
        g_down = mm_tn(f"down_g{l}", s["u"], dy2, a_fn=relu2)
        dh2 = mm_nn(f"up_b{l}", du, w["w_up"], F32, wt=True)
        g_up = mm_tn(f"up_g{l}", s["h2"], du, col_slabs=4)
        (dx1,), (dnpre2, dscale_m, dshift_m) = rowmap_vjp(f"premlp_b{l}", f_premod, [(s["x1"], 0, d)],
                                                          [npre2, md[4], md[3]], [(dh2, 0, d)], 1, [F32],
                                                          add_rows=[(dx1a, 0, d)])
        (dxa, dy), (dnpost, dgate_a) = rowmap_vjp(f"postmix_b{l}", f_postres, [(s["x"], 0, d), (s["y"], 0, d)],
                                                  [npost, md[2]], [(dx1, 0, d)], 2, [F32, BF16])
        dycat = mm_nn(f"outproj_b{l}", dy, w["w_out"], F32, wt=True)
        g_out = mm_tn(f"outproj_g{l}", s["ycat"], dy)
        (dcb, dcd), _ = rowmap(f"scgate_b{l}", lambda rv, cv: ([rv[0] * rv[2], rv[0] * rv[1]], []),
                               [(dycat, 2048, GW), (s["proj"], O_CB, GW), (s["cd"], 0, GW)], [],
                               [(GW, BF16), (GW, F32)])
        (dcc, dch), g_scw = conv_bwd(f"sc_conv_b{l}", dcd, [(s["proj"], O_CC, GW), (s["proj"], O_CH, GW)],
                                     sc_conv_w[l], BF16)
        daq, dak, dav, g_sinks = swa_bwd(f"swa_b{l}", dycat, 1024, s["proj"], cos_t, sin_t, k["sinks"])
        dcx, dsz, dsdt, g_scb, g_salog, g_sdtb, g_sdsk, g_snw = ssd_bwd(
            f"ssd_b{l}", dycat, 512, s["cx"], s["proj"], s["ss"], k["scb"], k["salog"], k["sdtb"], k["sdsk"], k["snw"])
        (dxbc,), g_ssmw = conv_bwd(f"ssm_conv_b{l}", dcx, [(s["proj"], O_XBC, 1024)], ssm_conv_w[l], BF16)
        dcq, dgz, dgba, g_alog, g_gdtb, g_gnw = gdn_bwd(f"gdn_b{l}", dycat, 0, s["cq"], s["proj"], s["gs"],
                                                        k["alog"], k["gdtb"], k["gnw"])
        (dgqkv,), g_gdnw = conv_bwd(f"gdn_conv_b{l}", dcq, [(s["proj"], O_GQKV, 1536)], gdn_conv_w[l], BF16)
        dproj = jnp.concatenate([dgqkv, dcb, dcc, dch, dxbc, daq, dgz, dsz, dak, dav, dgba, dsdt], axis=1)
        dh = mm_nn(f"inproj_b{l}", dproj, w["w_in"], F32, wt=True)
        g_in = mm_tn(f"inproj_g{l}", s["h"], dproj)
        (dx,), (dnpre, dscale_a, dshift_a) = rowmap_vjp(f"premix_b{l}", f_premod, [(s["x"], 0, d)],
                                                        [npre, md[1], md[0]], [(dh, 0, d)], 1, [F32],
                                                        add_rows=[(dxa, 0, d)])
        dmods[l] = jnp.concatenate([dshift_a, dscale_a, dgate_a, dshift_m, dscale_m, dgate_m], axis=1)[0]
        gsmall[l] = dict(
            norm_pre_mix=dnpre[0], norm_post_mix=dnpost[0], norm_pre_mlp=dnpre2[0], norm_post_mlp=dnpost2[0],
            gdn_a_log=g_alog[0, :4], gdn_dt_bias=g_gdtb[0, :4], gdn_norm_w=g_gnw[0], ssm_conv_b=g_scb[0],
            ssm_a_log=g_salog[0, :8], ssm_dt_bias=g_sdtb[0, :8], ssm_d=g_sdsk[0, :8], ssm_norm_w=g_snw[0],
            attn_sinks=g_sinks[0, :8], gdn_conv_w=g_gdnw, ssm_conv_w=g_ssmw, sc_conv_w=g_scw)
        gbig[l] = dict(w_in=jnp.stack(jnp.split(_unlayout_g_in(g_in), 4, axis=1), axis=0),
                       w_out=_unlayout_g_out(g_out).reshape(4, -1, d), w_up=g_up,
                       w_down=g_down.reshape(4, -1, d))
    grad_x = dx[None]

    buf, spec = _pack([jnp.stack(dmods, axis=0)])
    got = allgather8("ag_dmod", buf)
    dmod_sum = _unpack(sum_slabs("dmod_sum", got), spec)[0]
    dmod_all = jnp.stack([_unpack(got[kk], spec)[0] for kk in range(8)], axis=1)
    g_ada_w = []
    for l in range(DEPTH):
        dm = lax.dynamic_slice(dmod_all[l], (0, chip * nsh), (8, nsh))
        g_ada_w.append(mm_tn(f"ada_g{l}", c_pad, jnp.pad(dm, ((0, 8), (0, 0))), a_fn=silu_bf))
    g_ada_w = jnp.stack(g_ada_w, axis=0)

    names = [n for n in SMALL_REPL if n != "ada_b"] + list(SMALL_SHARD)
    buf, spec = _pack([jnp.stack([gsmall[l][n] for l in range(DEPTH)], axis=0) for n in names] + [loss_part])
    tot = _unpack(sum_slabs("small_sum", allgather8("ag_small_g", buf)), spec)
    gfull = dict(zip(names, tot[:-1]))
    loss = tot[-1][0, 0]
    grads = {"ada_w": g_ada_w, "ada_b": dmod_sum}
    for n in SMALL_REPL:
        if n != "ada_b":
            grads[n] = gfull[n]
    for n in SMALL_SHARD:
        wdt = p[n].shape[2]
        grads[n] = lax.dynamic_slice_in_dim(gfull[n], chip * wdt, wdt, axis=2)

    idx = jnp.stack([chip, ci]).astype(jnp.int32)
    g4 = [gbig[l][n].reshape(4, 2, gbig[l][n].shape[1] // 2, gbig[l][n].shape[2]) for (l, n) in items]
    recv1 = sibling_swap_halves("rs_sibling", g4)
    part = [add_half(f"rs_add_{n}{l}", g, r1, idx) for (l, n), g, r1 in zip(items, g4, recv1)]
    recv2 = chips_exchange("rs_chips", part)
    mine = [rs_sum(f"rs_sum_{n}{l}", g, r1, r2, idx) for (l, n), g, r1, r2 in zip(items, g4, recv1, recv2)]
    other = sibling_send("rs_join", mine)
    first = ci == 0
    gsh = {it: jnp.concatenate([jnp.where(first, a, b), jnp.where(first, b, a)], axis=0)
           for it, a, b in zip(items, mine, other)}
    for n in BIG:
        grads[n] = jnp.stack([gsh[l, n] for l in range(DEPTH)], axis=0)

    delta, new_m, new_v = {}, {}, {}
    for n in ("ada_w",) + BIG:
        sh = p[n].shape
        two = lambda a: a.reshape(sh[0] * sh[1], sh[2])
        dl, m2, v2 = adamw(f"adamw_{n}", two(p[n]), two(grads[n]), two(m[n]), two(v[n]))
        delta[n], new_m[n], new_v[n] = dl.reshape(sh), m2.reshape(sh), v2.reshape(sh)
    small = list(SMALL_REPL) + list(SMALL_SHARD)
    bufs = []
    for src in (p, grads, m, v):
        b_, spec = _pack([src[n] for n in small])
        bufs.append(b_)
    dl, m2, v2 = adamw("adamw_small", *bufs)
    for n, a, b_, cc_ in zip(small, _unpack(dl, spec), _unpack(m2, spec), _unpack(v2, spec)):
        delta[n], new_m[n], new_v[n] = a, b_, cc_

    return (loss, grad_x, *[grads[n] for n in WEIGHTS], *[delta[n] for n in WEIGHTS], *[new_m[n] for n in WEIGHTS],
            *[new_v[n] for n in WEIGHTS])


def kernel(x, c, positions, ada_w, ada_b, norm_pre_mix, norm_post_mix, norm_pre_mlp, norm_post_mlp, w_in, w_out, gdn_conv_w, gdn_a_log, gdn_dt_bias, gdn_norm_w, ssm_conv_w, ssm_conv_b, ssm_a_log, ssm_dt_bias, ssm_d, ssm_norm_w, attn_sinks, sc_conv_w, w_up, w_down, loss_target, m_ada_w, m_ada_b, m_norm_pre_mix, m_norm_post_mix, m_norm_pre_mlp, m_norm_post_mlp, m_w_in, m_w_out, m_gdn_conv_w, m_gdn_a_log, m_gdn_dt_bias, m_gdn_norm_w, m_ssm_conv_w, m_ssm_conv_b, m_ssm_a_log, m_ssm_dt_bias, m_ssm_d, m_ssm_norm_w, m_attn_sinks, m_sc_conv_w, m_w_up, m_w_down, v_ada_w, v_ada_b, v_norm_pre_mix, v_norm_post_mix, v_norm_pre_mlp, v_norm_post_mlp, v_w_in, v_w_out, v_gdn_conv_w, v_gdn_a_log, v_gdn_dt_bias, v_gdn_norm_w, v_ssm_conv_w, v_ssm_conv_b, v_ssm_a_log, v_ssm_dt_bias, v_ssm_d, v_ssm_norm_w, v_attn_sinks, v_sc_conv_w, v_w_up, v_w_down):
    loc = locals()
    p = {n: loc[n] for n in WEIGHTS}
    m = {n: loc["m_" + n] for n in WEIGHTS}
    v = {n: loc["v_" + n] for n in WEIGHTS}
    return _step(p, m, v, x, c, positions, loss_target)
```

```python
import functools

import numpy as np
import jax
import jax.numpy as jnp
from jax import lax
from jax.experimental import pallas as pl
from jax.experimental.pallas import tpu as pltpu

F32, BF16 = jnp.float32, jnp.bfloat16
MESH = pl.DeviceIdType.MESH

DEPTH = 2
GW = 512
N_GDN, GDN_CS = 4, 64
SSD_CS, ATT_W = 128, 128
EPS = 1e-6
ROPE_THETA = 10000.0
IN_WIDTH = 5904
B1, B2, LR, AEPS, WD, STEP = 0.9, 0.999, 0.001, 1e-8, 0.01, 10

VMEM_LIMIT = 52 * 1024 * 1024
LANE = 128

O_GQKV, O_CB, O_CC, O_CH, O_XBC, O_AQ, O_GZ, O_SZ, O_AK, O_AV, O_GBA, O_SDT = (
    0, 1536, 2048, 2560, 3072, 4096, 5120, 5632, 6144, 6400, 6656, 6784)
PW = 6912
YCAT = 2560


def _cparams(*sem):
    return pltpu.CompilerParams(dimension_semantics=sem or None, vmem_limit_bytes=VMEM_LIMIT)


def _pick(n, cap, mult):
    if n <= cap:
        return n
    best = None
    for d in range(mult, cap + 1, mult):
        if n % d == 0:
            best = d
    assert best is not None, (n, cap, mult)
    return best


def _pieces(x, n):
    out, r = [], x
    for i in range(n):
        h = r.astype(BF16)
        out.append(h)
        if i < n - 1:
            r = r - h.astype(F32)
    return out


def _dgp(a, b, ca, cb, na, nb):
    dn = (((ca,), (cb,)), ((), ()))
    acc = None
    for i, pa in enumerate(_pieces(a, na)):
        for j, pb in enumerate(_pieces(b, nb)):
            if i + j < max(na, nb):
                t = lax.dot_general(pa, pb, dn, preferred_element_type=F32)
                acc = t if acc is None else acc + t
    return acc


_CONTRACT = {"nn": (1, 0), "nt": (1, 1), "tn": (0, 0)}


@functools.lru_cache(maxsize=None)
def _mm(kind, na, nb, const_a=False):
    ca, cb = _CONTRACT[kind]
    ng = max(na, nb)

    @jax.custom_vjp
    def f(a, b):
        return _dgp(a, b, ca, cb, na, nb)

    def bwd(r, g):
        a, b = r
        if kind == "nn":
            da, db = (lambda: _mm("nt", ng, nb)(g, b)), _mm("tn", na, ng, const_a)(a, g)
        elif kind == "nt":
            da, db = (lambda: _mm("nn", ng, nb)(g, b)), _mm("tn", ng, na)(g, a)
        else:
            da, db = (lambda: _mm("nt", nb, ng)(b, g)), _mm("nn", na, ng, const_a)(a, g)
        return (jnp.zeros_like(a) if const_a else da()), db

    f.defvjp(lambda a, b: (f(a, b), (a, b)), bwd)
    return f


_nn, _nt, _tn = _mm("nn", 1, 1), _mm("nt", 1, 1), _mm("tn", 1, 1)
_hnn, _hnt, _htn = _mm("nn", 2, 2), _mm("nt", 2, 2), _mm("tn", 2, 2)
_xnn = _mm("nn", 1, 3, True)


def _silu(x):
    return x * jax.nn.sigmoid(x)


def _softplus(x):
    return jnp.maximum(x, 0.0) + jnp.log1p(jnp.exp(-jnp.abs(x)))


def _colsel(v, idx):
    lane = lax.broadcasted_iota(jnp.int32, v.shape, 1)
    return jnp.sum(jnp.where(lane == idx, v, 0.0), axis=1, keepdims=True)


def _tri(n):
    r = lax.broadcasted_iota(jnp.int32, (n, n), 0)
    c = lax.broadcasted_iota(jnp.int32, (n, n), 1)
    return r >= c, r > c


def _rowsel(v, idx):
    row = lax.broadcasted_iota(jnp.int32, v.shape, 0)
    return jnp.sum(jnp.where(row == idx, v, 0.0), axis=0, keepdims=True)


def _tri_inv(m, size, nil):
    r = lax.broadcasted_iota(jnp.int32, (size, size), 0)
    c = lax.broadcasted_iota(jnp.int32, (size, size), 1)
    x = jnp.where(r == c, 1.0, 0.0) - m
    p = _hnn(m, m)
    steps = int(np.log2(nil)) - 1
    for s in range(steps):
        x = x + _hnn(x, p)
        if s < steps - 1:
            p = _hnn(p, p)
    return x


@jax.custom_vjp
def _inv_given(m, tinv):
    return tinv


_inv_given.defvjp(lambda m, tinv: (tinv, tinv),
                  lambda tinv, g: (-_hnt(_htn(tinv, g), tinv), jnp.zeros_like(tinv)))


def _gdn_chunk(cq, z, ba, s, alog, dtb, nw, tinv_saved=None):
    n, nh = GDN_CS, N_GDN
    rr = n * nh

    def stack(a, off):
        return jnp.concatenate([a[:, off + h * LANE:off + (h + 1) * LANE] for h in range(nh)], axis=0)

    def percol(f):
        return jnp.concatenate([f(h) for h in range(nh)], axis=0)

    def own(a):
        return jnp.concatenate([a[h * n:(h + 1) * n, h * LANE:(h + 1) * LANE] for h in range(nh)], axis=0)

    q, k, v = _silu(stack(cq, 0)), _silu(stack(cq, GW)), _silu(stack(cq, 2 * GW))
    q = q * lax.rsqrt(jnp.sum(q * q, -1, keepdims=True) + EPS) * (128 ** -0.5)
    k = k * lax.rsqrt(jnp.sum(k * k, -1, keepdims=True) + EPS)
    beta = jax.nn.sigmoid(percol(lambda h: _colsel(ba, h)))
    g = percol(lambda h: -jnp.exp(_colsel(alog, h)) * _softplus(_colsel(ba, nh + h) + _colsel(dtb, h)))
    ri = lax.broadcasted_iota(jnp.int32, (rr, rr), 0)
    ci = lax.broadcasted_iota(jnp.int32, (rr, rr), 1)
    same = (ri >> 6) == (ci >> 6)
    incl, strict = same & (ri >= ci), same & (ri > ci)
    gcb = _xnn(incl.astype(F32), jnp.broadcast_to(g, (rr, LANE)))
    gci = jnp.concatenate([gcb, gcb], axis=1)
    dec = jnp.where(incl, jnp.exp(jnp.where(incl, gci - gci.T, 0.0)), 0.0)
    kb = k * beta
    kk = _nt(jnp.concatenate([kb, q], axis=0), k)
    m = jnp.where(strict, kk[:rr] * dec, 0.0)
    attn = jnp.where(incl, kk[rr:] * dec, 0.0)
    tinv = _tri_inv(m, rr, n) if tinv_saved is None else _inv_given(m, tinv_saved)
    egc = jnp.exp(gcb)
    uw = _nn(tinv, jnp.concatenate([v * beta, kb * egc], axis=1))
    u, w = uw[:, :LANE], uw[:, LANE:]
    gl = [jnp.sum(g[h * n:(h + 1) * n], axis=0, keepdims=True) for h in range(nh)]
    kd = k * jnp.exp(percol(lambda h: jnp.broadcast_to(gl[h], (n, 1))) - gcb)
    ws = _nn(jnp.concatenate([w, q * egc], axis=0), s)
    v_new = u - own(ws[:rr])
    o = own(ws[rr:]) + _nn(attn, v_new)
    rowh = lax.broadcasted_iota(jnp.int32, (rr, LANE), 0) >> 6
    vexp = jnp.concatenate([jnp.where(rowh == h, v_new, 0.0) for h in range(nh)], axis=1)
    cd = jnp.concatenate([jnp.broadcast_to(jnp.exp(gl[h]), (1, LANE)) for h in range(nh)], axis=1)
    s_new = s * cd + _tn(kd, vexp)
    y = o * lax.rsqrt(jnp.mean(o * o, -1, keepdims=True) + EPS) * nw * _silu(stack(z, 0))
    y = jnp.concatenate([y[h * n:(h + 1) * n] for h in range(nh)], axis=1)
    return (y, s_new, tinv) if tinv_saved is None else (y, s_new)


def _ssd_chunk(cx, z, dtc, s, convb, alog, dtb, dsk, nw):
    n = SSD_CS
    incl, _ = _tri(n)
    xbc = _silu(cx + convb)
    dt_all = _softplus(dtc + dtb)
    da_all = dt_all * (-jnp.exp(alog))
    cs_all = _xnn(incl.astype(F32), da_all)
    cs_t = cs_all.T
    tot_all = jnp.sum(da_all, axis=0, keepdims=True)
    lane_lo = lax.broadcasted_iota(jnp.int32, (n, LANE), 1) < 64
    ys, s_out = [], []
    for pi in range(4):
        g = pi // 2
        bm = xbc[:, GW + g * LANE:GW + (g + 1) * LANE]
        cm = xbc[:, GW + 256 + g * LANE:GW + 256 + (g + 1) * LANE]
        cbm = _nt(cm, bm)
        xp, s_p = xbc[:, pi * LANE:(pi + 1) * LANE], s[:, pi * LANE:(pi + 1) * LANE]
        a_rows, b_cols, dts, dsks, etot = [], [], [], [], []
        for h in (2 * pi, 2 * pi + 1):
            cs_i = _colsel(cs_all, h)
            lm = jnp.where(incl, jnp.exp(jnp.where(incl, cs_i - _rowsel(cs_t, h), 0.0)), 0.0)
            tot = _colsel(tot_all, h)
            a_rows.append(jnp.concatenate([cbm * lm, cm * jnp.exp(cs_i)], axis=1))
            b_cols.append(bm * jnp.exp(tot - cs_i))
            dts.append(_colsel(dt_all, h))
            dsks.append(_colsel(dsk, h))
            etot.append(jnp.exp(tot))
        xdt = xp * jnp.where(lane_lo, dts[0], dts[1])
        yy = _nn(jnp.concatenate(a_rows, axis=0), jnp.concatenate([xdt, s_p], axis=0))
        ys.append(jnp.where(lane_lo, yy[:n], yy[n:]) + xp * jnp.where(lane_lo[:1], dsks[0], dsks[1]))
        st = _tn(jnp.concatenate(b_cols, axis=1), xdt)
        s_out.append(s_p * jnp.where(lane_lo[:1], etot[0], etot[1]) + jnp.where(lane_lo, st[:n], st[n:]))
    zs = _silu(z)
    out = []
    for g in range(2):
        y = jnp.concatenate(ys[2 * g:2 * g + 2], axis=1) * zs[:, g * 256:(g + 1) * 256]
        out.append(y * lax.rsqrt(jnp.mean(y * y, -1, keepdims=True) + EPS) * nw[:, g * 256:(g + 1) * 256])
    return jnp.concatenate(out, axis=1), jnp.concatenate(s_out, axis=1)


SWA_STACK = 1


def _swa_block(qg, kp, kc, vp, vc, sink, notfirst, *, g):
    w, ns = ATT_W, SWA_STACK
    kb = jnp.concatenate([kp, kc], axis=0)
    vb = jnp.concatenate([vp, vc], axis=0)
    qi = lax.broadcasted_iota(jnp.int32, (ns * w, 2 * w), 0) & (w - 1)
    kj = lax.broadcasted_iota(jnp.int32, (ns * w, 2 * w), 1)
    rel = qi + w - kj
    mask = (rel >= 0) & (rel < w) & ((kj >= w) | (notfirst > 0.5))
    outs = []
    for r0 in range(0, 4, ns):
        q = jnp.concatenate([qg[:, r * LANE:(r + 1) * LANE] for r in range(r0, r0 + ns)], axis=0)
        snk = jnp.concatenate([jnp.broadcast_to(_colsel(sink, 4 * g + r), (w, 1)) for r in range(r0, r0 + ns)], axis=0)
        s = jnp.where(mask, _nt(q, kb) * (64 ** -0.5), -1e30)
        mx = lax.stop_gradient(jnp.maximum(jnp.max(s, axis=-1, keepdims=True), snk))
        p = jnp.where(mask, jnp.exp(s - mx), 0.0)
        p = p / (jnp.sum(p, axis=-1, keepdims=True) + jnp.exp(snk - mx))
        o = _nn(p, vb)
        outs += [o[j * w:(j + 1) * w] for j in range(ns)]
    return jnp.concatenate(outs, axis=1)


def _rope(x, cos, sin):
    return x * cos + pltpu.roll(x, 64, 1) * sin


def _unrope(dy, cos, sin):
    return dy * cos + pltpu.roll(dy * sin, 64, 1)


def rowmap(name, fn, rows, consts, outs, accs=(), tile=256):
    t = rows[0][0].shape[0]
    tile = min(tile, t)
    assert t % tile == 0
    nr, nc, no, na = len(rows), len(consts), len(outs), len(accs)

    def body(*refs):
        rv = [r[...] for r in refs[:nr]]
        cv = [r[...] for r in refs[nr:nr + nc]]
        ov, av = fn(rv, cv)
        for ref, o in zip(refs[nr + nc:nr + nc + no], ov):
            ref[...] = o.astype(ref.dtype)
        if na:
            a_refs = refs[nr + nc + no:]

            @pl.when(pl.program_id(0) == 0)
            def _():
                for ref in a_refs:
                    ref[...] = jnp.zeros_like(ref)

            for ref, a in zip(a_refs, av):
                ref[...] += a

    def cspec(off, width):
        assert off % width == 0
        cb = off // width
        return pl.BlockSpec((tile, width), lambda i: (i, cb))

    in_specs = [cspec(off, wd) for (_, off, wd) in rows]
    in_specs += [pl.BlockSpec(c.shape, lambda i: (0, 0)) for c in consts]
    out_specs = [pl.BlockSpec((tile, wd), lambda i: (i, 0)) for (wd, _) in outs]
    out_specs += [pl.BlockSpec(s, lambda i: (0, 0)) for s in accs]
    out_shape = [jax.ShapeDtypeStruct((t, wd), dt) for (wd, dt) in outs]
    out_shape += [jax.ShapeDtypeStruct(s, F32) for s in accs]
    res = pl.pallas_call(
        body, name=name, grid=(t // tile,), in_specs=in_specs, out_specs=out_specs, out_shape=out_shape,
        compiler_params=_cparams("arbitrary"),
    )(*[r[0] for r in rows], *consts)
    return res[:no], res[no:]


def rowmap_vjp(name, f, rows, consts, douts, n_diff, drow_dtypes, tile=256, add_rows=()):
    nr, nd, nadd = len(rows), len(douts), len(add_rows)

    def g(vals, cvals):
        rv, dov, addv = vals[:nr], vals[nr:nr + nd], vals[nr + nd:]
        fixed = rv[n_diff:]
        _, vjp = jax.vjp(lambda r, c: f(list(r) + list(fixed), c), rv[:n_diff], cvals)
        dr, dc = vjp([d.astype(F32) for d in dov])
        dr = list(dr)
        for a in addv:
            dr[0] = dr[0] + a
        return dr, dc

    outs = [(rows[i][2], drow_dtypes[i]) for i in range(n_diff)]
    accs = [c.shape for c in consts]
    return rowmap(name, g, list(rows) + list(douts) + list(add_rows), consts, outs, accs, tile=tile)


def mm_nn(name, a, w, out_dtype, a_fn=None, extra=None, epi=None, tm_cap=1024, wt=False):
    m, k = a.shape
    stacked = w.ndim == 3
    n = (w.shape[-2] if wt else w.shape[-1] * (w.shape[0] if stacked else 1))
    slab = w.shape[-1]
    tm = _pick(m, tm_cap, 16)
    tn = _pick(n if wt else slab, 1024, LANE)
    tk = _pick(slab if wt else k, 2048, LANE)
    nk = k // tk
    per = slab // (tk if wt else tn)
    wdims = ((1,), (1,)) if wt else ((1,), (0,))

    def body(*refs):
        a_ref, w_ref = refs[0], refs[1]
        e_ref = refs[2] if extra is not None else None
        o_ref, acc = refs[-2], refs[-1]
        kk = pl.program_id(2)

        @pl.when(kk == 0)
        def _():
            acc[...] = jnp.zeros_like(acc)

        av = a_ref[...]
        if a_fn is not None:
            av = a_fn(av)
        acc[...] += lax.dot_general(av.astype(BF16), w_ref[...].astype(BF16), (wdims, ((), ())),
                                    preferred_element_type=F32)

        @pl.when(kk == nk - 1)
        def _():
            r = acc[...]
            if epi is not None:
                r = epi(r, e_ref[...])
            o_ref[...] = r.astype(o_ref.dtype)

    if wt:
        w_spec = (pl.BlockSpec((None, tn, tk), lambda i, j, kk: (kk // per, j, kk % per)) if stacked
                  else pl.BlockSpec((tn, tk), lambda i, j, kk: (j, kk)))
    else:
        w_spec = (pl.BlockSpec((None, tk, tn), lambda i, j, kk: (j // per, kk, j % per)) if stacked
                  else pl.BlockSpec((tk, tn), lambda i, j, kk: (kk, j)))
    in_specs = [pl.BlockSpec((tm, tk), lambda i, j, kk: (i, kk)), w_spec]
    ops = [a, w]
    if extra is not None:
        in_specs.append(pl.BlockSpec((tm, tn), lambda i, j, kk: (i, j)))
        ops.append(extra)
    return pl.pallas_call(
        body, name=name, grid=(m // tm, n // tn, nk), in_specs=in_specs,
        out_specs=pl.BlockSpec((tm, tn), lambda i, j, kk: (i, j)),
        out_shape=jax.ShapeDtypeStruct((m, n), out_dtype),
        scratch_shapes=[pltpu.VMEM((tm, tn), F32)],
        compiler_params=_cparams("parallel", "parallel", "arbitrary"),
    )(*ops)


def mm_tn(name, a, b, a_fn=None, col_slabs=1):
    t, m = a.shape
    n = b.shape[1]
    tm, tn, tk = _pick(m, 1024, LANE), _pick(n // col_slabs, 1024, LANE), _pick(t, 2048, 16)
    nk = t // tk
    per = n // col_slabs // tn

    def body(a_ref, b_ref, o_ref, acc):
        kk = pl.program_id(2)

        @pl.when(kk == 0)
        def _():
            acc[...] = jnp.zeros_like(acc)

        av = a_ref[...]
        if a_fn is not None:
            av = a_fn(av)
        acc[...] += lax.dot_general(av.astype(BF16), b_ref[...].astype(BF16), (((0,), (0,)), ((), ())),
                                    preferred_element_type=F32)

        @pl.when(kk == nk - 1)
        def _():
            o_ref[...] = acc[...]

    return pl.pallas_call(
        body, name=name, grid=(m // tm, n // tn, nk),
        in_specs=[pl.BlockSpec((tk, tm), lambda i, j, kk: (kk, i)), pl.BlockSpec((tk, tn), lambda i, j, kk: (kk, j))],
        out_specs=(pl.BlockSpec((tm, tn), lambda i, j, kk: (i, j)) if col_slabs == 1
                   else pl.BlockSpec((None, tm, tn), lambda i, j, kk: (j // per, i, j % per))),
        out_shape=jax.ShapeDtypeStruct((m, n) if col_slabs == 1 else (col_slabs, m, n // col_slabs), F32),
        scratch_shapes=[pltpu.VMEM((tm, tn), F32)],
        compiler_params=_cparams("parallel", "parallel", "arbitrary"),
    )(a, b)


CONV_TILE = 256
HALO = 8


def conv_fwd(name, xs, w):
    t = xs[0][0].shape[0]
    c = xs[0][2]
    kw = w.shape[0]
    tt = min(CONV_TILE, t)
    nx = len(xs)

    def body(*refs):
        cur_refs, prev_refs, w_ref, y_ref = refs[:nx], refs[nx:2 * nx], refs[2 * nx], refs[2 * nx + 1]
        i = pl.program_id(0)
        cur, prev = cur_refs[0][...], prev_refs[0][...]
        for r, p in zip(cur_refs[1:], prev_refs[1:]):
            cur, prev = cur * r[...], prev * p[...]
        prev = jnp.where(i > 0, prev, 0.0)
        xx = jnp.concatenate([prev, cur], axis=0)
        acc = jnp.zeros((tt, c), F32)
        for k in range(kw):
            sh = kw - 1 - k
            rolled = pltpu.roll(xx, sh, 0) if sh else xx
            acc = acc + rolled[HALO:] * w_ref[k:k + 1, :]
        y_ref[...] = acc

    def cur_spec(off):
        cb = off // c
        return pl.BlockSpec((tt, c), lambda i: (i, cb))

    def prev_spec(off):
        cb = off // c
        return pl.BlockSpec((HALO, c), lambda i: (jnp.maximum(i * (tt // HALO) - 1, 0), cb))

    for (_, off, wd) in xs:
        assert wd == c and off % c == 0
    return pl.pallas_call(
        body, name=name, grid=(t // tt,),
        in_specs=[cur_spec(off) for (_, off, _) in xs] + [prev_spec(off) for (_, off, _) in xs]
        + [pl.BlockSpec(w.shape, lambda i: (0, 0))],
        out_specs=pl.BlockSpec((tt, c), lambda i: (i, 0)),
        out_shape=jax.ShapeDtypeStruct((t, c), F32),
        compiler_params=_cparams("arbitrary"),
    )(*[x[0] for x in xs], *[x[0] for x in xs], w)


def conv_bwd(name, dy, xs, w, dx_dtype):
    t = xs[0][0].shape[0]
    c = xs[0][2]
    kw = w.shape[0]
    tt = min(CONV_TILE, t)
    nt = t // tt
    nx = len(xs)
    n = tt + HALO

    def body(*refs):
        dy_ref = refs[0]
        cur_refs, prev_refs = refs[1:1 + nx], refs[1 + nx:1 + 2 * nx]
        w_ref = refs[1 + 2 * nx]
        dx_refs = refs[2 + 2 * nx:2 + 3 * nx]
        dw_ref, carry = refs[2 + 3 * nx], refs[3 + 3 * nx]
        i = pl.program_id(0)

        @pl.when(i == 0)
        def _():
            carry[...] = jnp.zeros_like(carry)
            dw_ref[...] = jnp.zeros_like(dw_ref)

        curs = [r[...] for r in cur_refs]
        cur, prev = curs[0], prev_refs[0][...]
        for cv, p in zip(curs[1:], prev_refs[1:]):
            cur, prev = cur * cv, prev * p[...]
        prev = jnp.where(i < nt - 1, prev, 0.0)
        xx = jnp.concatenate([prev, cur], axis=0)
        dyv = dy_ref[...]
        dpad = jnp.concatenate([jnp.zeros((HALO, c), F32), dyv], axis=0)
        dxx = jnp.zeros((n, c), F32)
        dws = []
        for k in range(kw):
            sh = kw - 1 - k
            dxx = dxx + (pltpu.roll(dpad, n - sh, 0) if sh else dpad) * w_ref[k:k + 1, :]
            rolled = pltpu.roll(xx, sh, 0) if sh else xx
            dws.append(jnp.sum(dyv * rolled[HALO:], axis=0, keepdims=True))
        dw_ref[...] += jnp.concatenate(dws, axis=0)
        dcur = jnp.concatenate([dxx[HALO:tt], dxx[tt:] + carry[...]], axis=0)
        carry[...] = dxx[:HALO]
        for j, ref in enumerate(dx_refs):
            d = dcur
            for jj, cv in enumerate(curs):
                if jj != j:
                    d = d * cv
            ref[...] = d.astype(ref.dtype)

    def cur_spec(off):
        cb = off // c
        return pl.BlockSpec((tt, c), lambda i: (nt - 1 - i, cb))

    def prev_spec(off):
        cb = off // c
        return pl.BlockSpec((HALO, c), lambda i: (jnp.maximum((nt - 1 - i) * (tt // HALO) - 1, 0), cb))

    res = pl.pallas_call(
        body, name=name, grid=(nt,),
        in_specs=[pl.BlockSpec((tt, c), lambda i: (nt - 1 - i, 0))] + [cur_spec(off) for (_, off, _) in xs]
        + [prev_spec(off) for (_, off, _) in xs] + [pl.BlockSpec(w.shape, lambda i: (0, 0))],
        out_specs=[pl.BlockSpec((tt, c), lambda i: (nt - 1 - i, 0)) for _ in xs] + [pl.BlockSpec(w.shape, lambda i: (0, 0))],
        out_shape=[jax.ShapeDtypeStruct((t, c), dx_dtype) for _ in xs] + [jax.ShapeDtypeStruct(w.shape, F32)],
        scratch_shapes=[pltpu.VMEM((HALO, c), F32)],
        compiler_params=_cparams("arbitrary"),
    )(dy, *[x[0] for x in xs], *[x[0] for x in xs], w)
    return res[:nx], res[nx]


def _const_spec(a):
    return pl.BlockSpec(a.shape, lambda i: (0,) * a.ndim)


GDN_STEP = 2


def _gdn_rows(ref, c):
    return ref[c * GDN_CS:(c + 1) * GDN_CS, :]


def gdn_fwd(name, cq, proj, alog, dtb, nw):
    t = cq.shape[0]
    nc = t // GDN_CS
    ns = min(GDN_STEP, nc)
    rows = ns * GDN_CS

    def body(cq_ref, z_ref, ba_ref, alog_ref, dtb_ref, nw_ref, y_ref, ssave_ref, tsave_ref, s_scr):
        @pl.when(pl.program_id(0) == 0)
        def _():
            s_scr[...] = jnp.zeros_like(s_scr)

        s = s_scr[...]
        ys, saved, tinvs = [], [], []
        for c in range(ns):
            saved.append(s)
            y, s, tinv = _gdn_chunk(_gdn_rows(cq_ref, c), _gdn_rows(z_ref, c), _gdn_rows(ba_ref, c), s, alog_ref[...],
                                    dtb_ref[...], nw_ref[...])
            ys.append(y)
            tinvs.append(tinv)
        y_ref[...] = jnp.concatenate(ys, axis=0)
        ssave_ref[...] = jnp.stack(saved, axis=0)
        tsave_ref[...] = jnp.stack(tinvs, axis=0)
        s_scr[...] = s

    return pl.pallas_call(
        body, name=name, grid=(nc // ns,),
        in_specs=[pl.BlockSpec((rows, 3 * GW), lambda i: (i, 0)),
                  pl.BlockSpec((rows, GW), lambda i: (i, O_GZ // GW)),
                  pl.BlockSpec((rows, LANE), lambda i: (i, O_GBA // LANE)),
                  _const_spec(alog), _const_spec(dtb), _const_spec(nw)],
        out_specs=[pl.BlockSpec((rows, GW), lambda i: (i, 0)),
                   pl.BlockSpec((ns, LANE, N_GDN * LANE), lambda i: (i, 0, 0)),
                   pl.BlockSpec((ns, N_GDN * GDN_CS, N_GDN * GDN_CS), lambda i: (i, 0, 0))],
        out_shape=[jax.ShapeDtypeStruct((t, GW), F32), jax.ShapeDtypeStruct((nc, LANE, N_GDN * LANE), F32),
                   jax.ShapeDtypeStruct((nc, N_GDN * GDN_CS, N_GDN * GDN_CS), F32)],
        scratch_shapes=[pltpu.VMEM((LANE, N_GDN * LANE), F32)],
        compiler_params=_cparams("arbitrary"),
    )(cq, proj, proj, alog, dtb, nw)


def gdn_bwd(name, dy, dy_off, cq, proj, ssave, tsave, alog, dtb, nw):
    t = cq.shape[0]
    nc = t // GDN_CS
    ns = min(GDN_STEP, nc)
    rows = ns * GDN_CS

    def body(dy_ref, cq_ref, z_ref, ba_ref, ssave_ref, tsave_ref, alog_ref, dtb_ref, nw_ref,
             dcq_ref, dz_ref, dba_ref, dalog_ref, ddtb_ref, dnw_ref, ds_scr):
        @pl.when(pl.program_id(0) == 0)
        def _():
            ds_scr[...] = jnp.zeros_like(ds_scr)
            dalog_ref[...] = jnp.zeros_like(dalog_ref)
            ddtb_ref[...] = jnp.zeros_like(ddtb_ref)
            dnw_ref[...] = jnp.zeros_like(dnw_ref)

        ds = ds_scr[...]
        dcq, dzs, dbas = [None] * ns, [None] * ns, [None] * ns
        dal, ddt, dnw = (jnp.zeros(r.shape, F32) for r in (dalog_ref, ddtb_ref, dnw_ref))
        for c in reversed(range(ns)):
            _, vjp = jax.vjp(functools.partial(_gdn_chunk, tinv_saved=tsave_ref[c]), _gdn_rows(cq_ref, c),
                             _gdn_rows(z_ref, c), _gdn_rows(ba_ref, c), ssave_ref[c], alog_ref[...], dtb_ref[...],
                             nw_ref[...])
            dcq[c], dzs[c], dbas[c], ds, dal_c, ddt_c, dnw_c = vjp((_gdn_rows(dy_ref, c), ds))
            dal, ddt, dnw = dal + dal_c, ddt + ddt_c, dnw + dnw_c
        dcq_ref[...] = jnp.concatenate(dcq, axis=0)
        dz_ref[...] = jnp.concatenate(dzs, axis=0).astype(dz_ref.dtype)
        dba_ref[...] = jnp.concatenate(dbas, axis=0).astype(dba_ref.dtype)
        ds_scr[...] = ds
        dalog_ref[...] += dal
        ddtb_ref[...] += ddt
        dnw_ref[...] += dnw

    rev = lambda i: nc // ns - 1 - i
    return pl.pallas_call(
        body, name=name, grid=(nc // ns,),
        in_specs=[pl.BlockSpec((rows, GW), lambda i: (rev(i), dy_off // GW)),
                  pl.BlockSpec((rows, 3 * GW), lambda i: (rev(i), 0)),
                  pl.BlockSpec((rows, GW), lambda i: (rev(i), O_GZ // GW)),
                  pl.BlockSpec((rows, LANE), lambda i: (rev(i), O_GBA // LANE)),
                  pl.BlockSpec((ns, LANE, N_GDN * LANE), lambda i: (rev(i), 0, 0)),
                  pl.BlockSpec((ns, N_GDN * GDN_CS, N_GDN * GDN_CS), lambda i: (rev(i), 0, 0)),
                  _const_spec(alog), _const_spec(dtb), _const_spec(nw)],
        out_specs=[pl.BlockSpec((rows, 3 * GW), lambda i: (rev(i), 0)),
                   pl.BlockSpec((rows, GW), lambda i: (rev(i), 0)),
                   pl.BlockSpec((rows, LANE), lambda i: (rev(i), 0)),
                   _const_spec(alog), _const_spec(dtb), _const_spec(nw)],
        out_shape=[jax.ShapeDtypeStruct((t, 3 * GW), F32), jax.ShapeDtypeStruct((t, GW), BF16),
                   jax.ShapeDtypeStruct((t, LANE), BF16), jax.ShapeDtypeStruct(alog.shape, F32),
                   jax.ShapeDtypeStruct(dtb.shape, F32), jax.ShapeDtypeStruct(nw.shape, F32)],
        scratch_shapes=[pltpu.VMEM((LANE, N_GDN * LANE), F32)],
        compiler_params=_cparams("arbitrary"),
    )(dy, cq, proj, proj, ssave, tsave, alog, dtb, nw)


def ssd_fwd(name, cx, proj, convb, alog, dtb, dsk, nw):
    t = cx.shape[0]
    nc = t // SSD_CS

    def body(cx_ref, z_ref, dt_ref, cb_ref, alog_ref, dtb_ref, dsk_ref, nw_ref, y_ref, ssave_ref, s_scr):
        @pl.when(pl.program_id(0) == 0)
        def _():
            s_scr[...] = jnp.zeros_like(s_scr)

        s_in = s_scr[...]
        y, s_new = _ssd_chunk(cx_ref[...], z_ref[...], dt_ref[...], s_in, cb_ref[...], alog_ref[...], dtb_ref[...],
                              dsk_ref[...], nw_ref[...])
        y_ref[...] = y
        ssave_ref[0] = s_in
        s_scr[...] = s_new

    return pl.pallas_call(
        body, name=name, grid=(nc,),
        in_specs=[pl.BlockSpec((SSD_CS, 1024), lambda i: (i, 0)),
                  pl.BlockSpec((SSD_CS, GW), lambda i: (i, O_SZ // GW)),
                  pl.BlockSpec((SSD_CS, LANE), lambda i: (i, O_SDT // LANE)),
                  _const_spec(convb), _const_spec(alog), _const_spec(dtb), _const_spec(dsk), _const_spec(nw)],
        out_specs=[pl.BlockSpec((SSD_CS, GW), lambda i: (i, 0)),
                   pl.BlockSpec((1, LANE, GW), lambda i: (i, 0, 0))],
        out_shape=[jax.ShapeDtypeStruct((t, GW), F32), jax.ShapeDtypeStruct((nc, LANE, GW), F32)],
        scratch_shapes=[pltpu.VMEM((LANE, GW), F32)],
        compiler_params=_cparams("arbitrary"),
    )(cx, proj, proj, convb, alog, dtb, dsk, nw)


def ssd_bwd(name, dy, dy_off, cx, proj, ssave, convb, alog, dtb, dsk, nw):
    t = cx.shape[0]
    nc = t // SSD_CS

    def body(dy_ref, cx_ref, z_ref, dt_ref, ssave_ref, cb_ref, alog_ref, dtb_ref, dsk_ref, nw_ref,
             dcx_ref, dz_ref, ddt_ref, dcb_ref, dalog_ref, ddtb_ref, ddsk_ref, dnw_ref, ds_scr):
        @pl.when(pl.program_id(0) == 0)
        def _():
            ds_scr[...] = jnp.zeros_like(ds_scr)
            for r in (dcb_ref, dalog_ref, ddtb_ref, ddsk_ref, dnw_ref):
                r[...] = jnp.zeros_like(r)

        _, vjp = jax.vjp(_ssd_chunk, cx_ref[...], z_ref[...], dt_ref[...], ssave_ref[0], cb_ref[...], alog_ref[...],
                         dtb_ref[...], dsk_ref[...], nw_ref[...])
        dcx, dz, ddt, ds, dcb, dal, ddtb, ddsk, dnw = vjp((dy_ref[...], ds_scr[...]))
        dcx_ref[...] = dcx
        dz_ref[...] = dz.astype(dz_ref.dtype)
        ddt_ref[...] = ddt.astype(ddt_ref.dtype)
        ds_scr[...] = ds
        dcb_ref[...] += dcb
        dalog_ref[...] += dal
        ddtb_ref[...] += ddtb
        ddsk_ref[...] += ddsk
        dnw_ref[...] += dnw

    rev = lambda i: nc - 1 - i
    return pl.pallas_call(
        body, name=name, grid=(nc,),
        in_specs=[pl.BlockSpec((SSD_CS, GW), lambda i: (rev(i), dy_off // GW)),
                  pl.BlockSpec((SSD_CS, 1024), lambda i: (rev(i), 0)),
                  pl.BlockSpec((SSD_CS, GW), lambda i: (rev(i), O_SZ // GW)),
                  pl.BlockSpec((SSD_CS, LANE), lambda i: (rev(i), O_SDT // LANE)),
                  pl.BlockSpec((1, LANE, GW), lambda i: (rev(i), 0, 0)),
                  _const_spec(convb), _const_spec(alog), _const_spec(dtb), _const_spec(dsk), _const_spec(nw)],
        out_specs=[pl.BlockSpec((SSD_CS, 1024), lambda i: (rev(i), 0)),
                   pl.BlockSpec((SSD_CS, GW), lambda i: (rev(i), 0)),
                   pl.BlockSpec((SSD_CS, LANE), lambda i: (rev(i), 0)),
                   _const_spec(convb), _const_spec(alog), _const_spec(dtb), _const_spec(dsk), _const_spec(nw)],
        out_shape=[jax.ShapeDtypeStruct((t, 1024), F32), jax.ShapeDtypeStruct((t, GW), BF16),
                   jax.ShapeDtypeStruct((t, LANE), BF16), jax.ShapeDtypeStruct(convb.shape, F32),
                   jax.ShapeDtypeStruct(alog.shape, F32), jax.ShapeDtypeStruct(dtb.shape, F32),
                   jax.ShapeDtypeStruct(dsk.shape, F32), jax.ShapeDtypeStruct(nw.shape, F32)],
        scratch_shapes=[pltpu.VMEM((LANE, GW), F32)],
        compiler_params=_cparams("arbitrary"),
    )(dy, cx, proj, proj, ssave, convb, alog, dtb, dsk, nw)


def _swa_specs(nb, rev):
    w = ATT_W
    cur = lambda i: (nb - 1 - i) if rev else i
    prv = lambda i: jnp.maximum(cur(i) - 1, 0)
    return [pl.BlockSpec((w, 1024), lambda i: (cur(i), O_AQ // 1024)),
            pl.BlockSpec((w, 256), lambda i: (cur(i), O_AK // 256)),
            pl.BlockSpec((w, 256), lambda i: (prv(i), O_AK // 256)),
            pl.BlockSpec((w, 256), lambda i: (cur(i), O_AV // 256)),
            pl.BlockSpec((w, 256), lambda i: (prv(i), O_AV // 256)),
            pl.BlockSpec((w, LANE), lambda i: (cur(i), 0)), pl.BlockSpec((w, LANE), lambda i: (prv(i), 0)),
            pl.BlockSpec((w, LANE), lambda i: (cur(i), 0)), pl.BlockSpec((w, LANE), lambda i: (prv(i), 0))]


def swa_fwd(name, proj, cos, sin, sinks):
    t = proj.shape[0]
    nb = t // ATT_W

    def body(q_ref, kc_ref, kp_ref, vc_ref, vp_ref, cc_ref, cp_ref, sc_ref, sp_ref, snk_ref, y_ref):
        notfirst = jnp.where(pl.program_id(0) > 0, 1.0, 0.0).astype(F32)
        cc, cp, sc, sp = cc_ref[...], cp_ref[...], sc_ref[...], sp_ref[...]
        outs = []
        for g in range(2):
            sl = slice(g * LANE, (g + 1) * LANE)
            qg = jnp.concatenate([_rope(q_ref[:, (4 * g + r) * LANE:(4 * g + r + 1) * LANE], cc, sc)
                                  for r in range(4)], axis=1)
            outs.append(_swa_block(qg, _rope(kp_ref[:, sl], cp, sp), _rope(kc_ref[:, sl], cc, sc), vp_ref[:, sl],
                                   vc_ref[:, sl], snk_ref[...], notfirst, g=g))
        y_ref[...] = jnp.concatenate(outs, axis=1)

    return pl.pallas_call(
        body, name=name, grid=(nb,),
        in_specs=_swa_specs(nb, False) + [_const_spec(sinks)],
        out_specs=pl.BlockSpec((ATT_W, 1024), lambda i: (i, 0)),
        out_shape=jax.ShapeDtypeStruct((t, 1024), F32),
        compiler_params=_cparams("arbitrary"),
    )(proj, proj, proj, proj, proj, cos, cos, sin, sin, sinks)


def swa_bwd(name, dy, dy_off, proj, cos, sin, sinks):
    t = proj.shape[0]
    nb = t // ATT_W

    def body(dy_ref, q_ref, kc_ref, kp_ref, vc_ref, vp_ref, cc_ref, cp_ref, sc_ref, sp_ref, snk_ref,
             dq_ref, dk_ref, dv_ref, dsnk_ref, ck_scr, cv_scr):
        i = pl.program_id(0)

        @pl.when(i == 0)
        def _():
            ck_scr[...] = jnp.zeros_like(ck_scr)
            cv_scr[...] = jnp.zeros_like(cv_scr)
            dsnk_ref[...] = jnp.zeros_like(dsnk_ref)

        notfirst = jnp.where(i < nb - 1, 1.0, 0.0).astype(F32)
        cc, cp, sc, sp = cc_ref[...], cp_ref[...], sc_ref[...], sp_ref[...]
        ck_in, cv_in = ck_scr[...], cv_scr[...]
        dq, dk, dv, ck, cv, dsnk = [], [], [], [], [], jnp.zeros(dsnk_ref.shape, F32)
        for g in range(2):
            sl = slice(g * LANE, (g + 1) * LANE)
            qg = jnp.concatenate([_rope(q_ref[:, (4 * g + r) * LANE:(4 * g + r + 1) * LANE], cc, sc)
                                  for r in range(4)], axis=1)
            _, vjp = jax.vjp(functools.partial(_swa_block, g=g), qg, _rope(kp_ref[:, sl], cp, sp),
                             _rope(kc_ref[:, sl], cc, sc), vp_ref[:, sl], vc_ref[:, sl], snk_ref[...], notfirst)
            dqg, dkp, dkc, dvp, dvc, dsnk_g, _ = vjp(dy_ref[:, g * GW:(g + 1) * GW])
            dq += [_unrope(dqg[:, r * LANE:(r + 1) * LANE], cc, sc) for r in range(4)]
            dk.append(_unrope(dkc, cc, sc) + ck_in[:, sl])
            dv.append(dvc + cv_in[:, sl])
            ck.append(_unrope(dkp, cp, sp))
            cv.append(dvp)
            dsnk = dsnk + dsnk_g
        dq_ref[...] = jnp.concatenate(dq, axis=1).astype(dq_ref.dtype)
        dk_ref[...] = jnp.concatenate(dk, axis=1).astype(dk_ref.dtype)
        dv_ref[...] = jnp.concatenate(dv, axis=1).astype(dv_ref.dtype)
        ck_scr[...] = jnp.concatenate(ck, axis=1)
        cv_scr[...] = jnp.concatenate(cv, axis=1)
        dsnk_ref[...] += dsnk

    rev = lambda i: nb - 1 - i
    return pl.pallas_call(
        body, name=name, grid=(nb,),
        in_specs=[pl.BlockSpec((ATT_W, 1024), lambda i: (rev(i), dy_off // 1024))] + _swa_specs(nb, True)
        + [_const_spec(sinks)],
        out_specs=[pl.BlockSpec((ATT_W, 1024), lambda i: (rev(i), 0)),
                   pl.BlockSpec((ATT_W, 256), lambda i: (rev(i), 0)),
                   pl.BlockSpec((ATT_W, 256), lambda i: (rev(i), 0)), _const_spec(sinks)],
        out_shape=[jax.ShapeDtypeStruct((t, 1024), BF16), jax.ShapeDtypeStruct((t, 256), BF16),
                   jax.ShapeDtypeStruct((t, 256), BF16), jax.ShapeDtypeStruct(sinks.shape, F32)],
        scratch_shapes=[pltpu.VMEM((ATT_W, 256), F32), pltpu.VMEM((ATT_W, 256), F32)],
        compiler_params=_cparams("arbitrary"),
    )(dy, proj, proj, proj, proj, proj, cos, cos, sin, sin, sinks)


def _me():
    return lax.axis_index("x"), lax.axis_index("y"), lax.axis_index("c")


ANY = pl.BlockSpec(memory_space=pl.ANY)


def allgather8(name, buf):
    r = buf.shape[0]

    def body(x_ref, out_ref, send_sems, recv_sems, local_sem):
        x, y, c = _me()
        me = 4 * x + 2 * y + c
        mine = pltpu.make_async_copy(x_ref, out_ref.at[me], local_sem)
        mine.start()
        copies = []
        for k in range(1, 8):
            bx, by, bc = (k >> 2) & 1, (k >> 1) & 1, k & 1
            to = ((1 - x) if bx else x, (1 - y) if by else y, (1 - c) if bc else c)
            cp = pltpu.make_async_remote_copy(src_ref=x_ref, dst_ref=out_ref.at[me], send_sem=send_sems.at[k - 1],
                                              recv_sem=recv_sems.at[k - 1], device_id=to, device_id_type=MESH)
            cp.start()
            copies.append(cp)
        for cp in copies:
            cp.wait_recv()
        for cp in copies:
            cp.wait_send()
        mine.wait()

    return pl.pallas_call(
        body, name=name, out_shape=jax.ShapeDtypeStruct((8, r, LANE), buf.dtype),
        in_specs=[pl.BlockSpec(memory_space=pltpu.VMEM)], out_specs=pl.BlockSpec(memory_space=pltpu.VMEM),
        scratch_shapes=[pltpu.SemaphoreType.DMA((7,)), pltpu.SemaphoreType.DMA((7,)), pltpu.SemaphoreType.DMA],
        compiler_params=pltpu.CompilerParams(vmem_limit_bytes=VMEM_LIMIT),
    )(buf)


def _sems(n):
    return [pltpu.SemaphoreType.DMA((n,)), pltpu.SemaphoreType.DMA((n,))]


def allgather_chips(name, shards):
    n = len(shards)

    def body(*refs):
        x_refs, out_refs, send_sems, recv_sems = refs[:n], refs[n:2 * n], refs[2 * n], refs[2 * n + 1]
        x, y, c = _me()
        chip = 2 * x + y
        chips = [(1 - x, y), (x, 1 - y), (1 - x, 1 - y)]

        def rc(src, dst, k, to):
            return pltpu.make_async_remote_copy(src_ref=src, dst_ref=dst, send_sem=send_sems.at[k],
                                                recv_sem=recv_sems.at[k], device_id=to, device_id_type=MESH)

        halves = [pl.ds(pl.multiple_of(c * (s.shape[0] // 2), 16), s.shape[0] // 2) for s in shards]
        own, first = [], {}
        for i in range(n):
            cp = rc(x_refs[i], out_refs[i].at[chip], 7 * i + 6, (x, y, 1 - c))
            cp.start()
            own.append(cp)
            for k, (cx, cy) in enumerate(chips):
                cp = rc(x_refs[i].at[halves[i]], out_refs[i].at[chip, halves[i]], 7 * i + k, (cx, cy, c))
                cp.start()
                first[i, k] = cp
        passed = []
        for i in range(n):
            for k, (cx, cy) in enumerate(chips):
                first[i, k].wait_recv()
                src = out_refs[i].at[2 * cx + cy, halves[i]]
                cp = rc(src, src, 7 * i + 3 + k, (x, y, 1 - c))
                cp.start()
                passed.append(cp)
        for cp in passed + own:
            cp.wait_recv()
        for cp in list(first.values()) + passed + own:
            cp.wait_send()

    for s in shards:
        assert s.shape[0] % 32 == 0
    return pl.pallas_call(
        body, name=name, out_shape=[jax.ShapeDtypeStruct((4,) + s.shape, s.dtype) for s in shards],
        in_specs=[ANY] * n, out_specs=[ANY] * n, scratch_shapes=_sems(7 * n),
    )(*shards)


def sibling_swap_halves(name, gs):
    n = len(gs)

    def body(*refs):
        g_refs, out_refs, send_sems, recv_sems = refs[:n], refs[n:2 * n], refs[2 * n], refs[2 * n + 1]
        x, y, c = _me()
        cps = [pltpu.make_async_remote_copy(src_ref=g_refs[i].at[:, 1 - c], dst_ref=out_refs[i],
                                            send_sem=send_sems.at[i], recv_sem=recv_sems.at[i],
                                            device_id=(x, y, 1 - c), device_id_type=MESH) for i in range(n)]
        for cp in cps:
            cp.start()
        for cp in cps:
            cp.wait_recv()
        for cp in cps:
            cp.wait_send()

    return pl.pallas_call(
        body, name=name, out_shape=[jax.ShapeDtypeStruct((4,) + g.shape[2:], g.dtype) for g in gs],
        in_specs=[ANY] * n, out_specs=[ANY] * n, scratch_shapes=_sems(n),
    )(*gs)


def chips_exchange(name, ps):
    n = len(ps)

    def body(*refs):
        p_refs, out_refs, send_sems, recv_sems = refs[:n], refs[n:2 * n], refs[2 * n], refs[2 * n + 1]
        x, y, c = _me()
        chip = 2 * x + y
        cps = []
        for i in range(n):
            for k, (cx, cy) in enumerate([(1 - x, y), (x, 1 - y), (1 - x, 1 - y)]):
                cps.append(pltpu.make_async_remote_copy(
                    src_ref=p_refs[i].at[2 * cx + cy], dst_ref=out_refs[i].at[chip], send_sem=send_sems.at[3 * i + k],
                    recv_sem=recv_sems.at[3 * i + k], device_id=(cx, cy, c), device_id_type=MESH))
        for cp in cps:
            cp.start()
        for cp in cps:
            cp.wait_recv()
        for cp in cps:
            cp.wait_send()

    return pl.pallas_call(
        body, name=name, out_shape=[jax.ShapeDtypeStruct(p.shape, p.dtype) for p in ps],
        in_specs=[ANY] * n, out_specs=[ANY] * n, scratch_shapes=_sems(3 * n),
    )(*ps)


def sibling_send(name, ss):
    n = len(ss)

    def body(*refs):
        s_refs, out_refs, send_sems, recv_sems = refs[:n], refs[n:2 * n], refs[2 * n], refs[2 * n + 1]
        x, y, c = _me()
        cps = [pltpu.make_async_remote_copy(src_ref=s_refs[i], dst_ref=out_refs[i], send_sem=send_sems.at[i],
                                            recv_sem=recv_sems.at[i], device_id=(x, y, 1 - c), device_id_type=MESH)
               for i in range(n)]
        for cp in cps:
            cp.start()
        for cp in cps:
            cp.wait_recv()
        for cp in cps:
            cp.wait_send()

    return pl.pallas_call(
        body, name=name, out_shape=[jax.ShapeDtypeStruct(s.shape, s.dtype) for s in ss],
        in_specs=[ANY] * n, out_specs=[ANY] * n, scratch_shapes=_sems(n),
    )(*ss)


def _rs_tile(rh, w):
    return _pick(rh, max(8, (1 << 19) // w // 8 * 8), 8)


def add_half(name, g, recv, idx):
    _, _, rh, w = g.shape
    tile = _rs_tile(rh, w)

    def body(s_ref, g_ref, r_ref, o_ref):
        o_ref[...] = (g_ref[...] + r_ref[...]).astype(o_ref.dtype)

    return pl.pallas_call(
        body, name=name,
        grid_spec=pltpu.PrefetchScalarGridSpec(
            num_scalar_prefetch=1, grid=(4, rh // tile),
            in_specs=[pl.BlockSpec((None, None, tile, w), lambda s, i, ix: (s, ix[1], i, 0)),
                      pl.BlockSpec((None, tile, w), lambda s, i, ix: (s, i, 0))],
            out_specs=pl.BlockSpec((None, tile, w), lambda s, i, ix: (s, i, 0))),
        out_shape=jax.ShapeDtypeStruct((4, rh, w), BF16),
        compiler_params=_cparams("parallel", "parallel"),
    )(idx, g, recv)


def rs_sum(name, g, recv1, recv2, idx):
    _, _, rh, w = g.shape
    tile = _rs_tile(rh, w)

    def body(s_ref, g_ref, r1_ref, a_ref, b_ref, c_ref, o_ref):
        o_ref[...] = (g_ref[...] + r1_ref[...]) + (a_ref[...].astype(F32) + b_ref[...].astype(F32)
                                                    + c_ref[...].astype(F32))

    def other(k):
        return pl.BlockSpec((None, tile, w), lambda i, ix: (jnp.bitwise_xor(ix[0], k), i, 0))

    return pl.pallas_call(
        body, name=name,
        grid_spec=pltpu.PrefetchScalarGridSpec(
            num_scalar_prefetch=1, grid=(rh // tile,),
            in_specs=[pl.BlockSpec((None, None, tile, w), lambda i, ix: (ix[0], ix[1], i, 0)),
                      pl.BlockSpec((None, tile, w), lambda i, ix: (ix[0], i, 0)), other(1), other(2), other(3)],
            out_specs=pl.BlockSpec((tile, w), lambda i, ix: (i, 0))),
        out_shape=jax.ShapeDtypeStruct((rh, w), F32),
        compiler_params=_cparams("parallel"),
    )(idx, g, recv1, recv2, recv2, recv2)


def sum_slabs(name, a):
    n, r, _ = a.shape
    tile = _pick(r, 2048, 8)

    def body(a_ref, o_ref):
        acc = a_ref[0]
        for s in range(1, n):
            acc = acc + a_ref[s]
        o_ref[...] = acc

    return pl.pallas_call(
        body, name=name, grid=(r // tile,),
        in_specs=[pl.BlockSpec((n, tile, LANE), lambda i: (0, i, 0))],
        out_specs=pl.BlockSpec((tile, LANE), lambda i: (i, 0)),
        out_shape=jax.ShapeDtypeStruct((r, LANE), F32),
        compiler_params=_cparams("parallel"),
    )(a)


def f_premod(rv, cv):
    (x,), (w, scale, shift) = rv, cv
    y = x * lax.rsqrt(jnp.mean(x * x, -1, keepdims=True) + EPS) * w
    return [y * (1.0 + scale) + shift]


def f_postres(rv, cv):
    (x, y), (w, gate) = rv, cv
    return [x + gate * (y * lax.rsqrt(jnp.mean(y * y, -1, keepdims=True) + EPS) * w)]


def f_adamw(rv, cv):
    w, g, m, v = rv
    m = B1 * m + (1.0 - B1) * g
    v = B2 * v + (1.0 - B2) * jnp.square(g)
    m_hat = m / (1.0 - B1 ** STEP)
    v_hat = v / (1.0 - B2 ** STEP)
    return [-LR * (m_hat / (jnp.sqrt(v_hat) + AEPS) + WD * w), m, v], []


def adamw(name, w, g, m, v):
    r, wd = w.shape
    tile = r
    while tile * wd * 4 * 14 > 24 * 1024 * 1024 and tile % 16 == 0:
        tile //= 2
    (d, m2, v2), _ = rowmap(name, f_adamw, [(a, 0, wd) for a in (w, g, m, v)], [], [(wd, F32)] * 3, tile=tile)
    return d, m2, v2


_IN_SIZES = (512, 512, 512, 512, 4, 4, 512, 1024, 8, 512, 128, 128, 512, 512, 512)


def _relayout_w_in(w):
    d = w.shape[0]
    s = np.cumsum((0,) + _IN_SIZES)
    gq, gk, gv, gz, gb, ga, sz, sx, sdt, aq, ak, av, cb, cc, ch = [w[:, s[i]:s[i + 1]] for i in range(15)]

    def rope_pad(t, nh):
        return jnp.pad(t.reshape(d, nh, 2, 32), ((0, 0), (0, 0), (0, 0), (0, 32))).reshape(d, nh * LANE)

    def v_pad(t, nh):
        return jnp.pad(t.reshape(d, nh, 64), ((0, 0), (0, 0), (0, 64))).reshape(d, nh * LANE)

    def pad128(t):
        return jnp.pad(t, ((0, 0), (0, LANE - t.shape[1])))

    return jnp.concatenate([gq, gk, gv, cb, cc, ch, sx, rope_pad(aq, 8), gz, sz, rope_pad(ak, 2), v_pad(av, 2),
                            pad128(jnp.concatenate([gb, ga], axis=1)), pad128(sdt)], axis=1)


def _unlayout_g_in(g):
    d = g.shape[0]

    def rope_unpad(t, nh):
        return t.reshape(d, nh, 2, 64)[..., :32].reshape(d, nh * 64)

    def v_unpad(t, nh):
        return t.reshape(d, nh, LANE)[..., :64].reshape(d, nh * 64)

    return jnp.concatenate([g[:, 0:1536], g[:, O_GZ:O_GZ + 512], g[:, O_GBA:O_GBA + 8], g[:, O_SZ:O_SZ + 512],
                            g[:, O_XBC:O_XBC + 1024], g[:, O_SDT:O_SDT + 8], rope_unpad(g[:, O_AQ:O_AQ + 1024], 8),
                            rope_unpad(g[:, O_AK:O_AK + 256], 2), v_unpad(g[:, O_AV:O_AV + 256], 2),
                            g[:, O_CB:O_CB + 1536]], axis=1)


def _relayout_w_out(w):
    d = w.shape[1]
    yc = jnp.pad(w[1024:1536].reshape(8, 64, d), ((0, 0), (0, 64), (0, 0))).reshape(1024, d)
    return jnp.concatenate([w[:1024], yc, w[1536:]], axis=0)


def _unlayout_g_out(g):
    d = g.shape[1]
    return jnp.concatenate([g[:1024], g[1024:2048].reshape(8, LANE, d)[:, :64].reshape(512, d), g[2048:]], axis=0)


def _pack(arrs):
    rows, spec, off = [], [], 0
    for a in arrs:
        n = int(np.prod(a.shape))
        nr = -(-n // LANE)
        rows.append(jnp.pad(a.reshape(-1).astype(F32), (0, nr * LANE - n)).reshape(nr, LANE))
        spec.append((off, a.shape))
        off += nr
    pad = (-off) % 8
    if pad:
        rows.append(jnp.zeros((pad, LANE), F32))
    return jnp.concatenate(rows, axis=0), spec


def _unpack(buf, spec):
    out = []
    for off, shape in spec:
        n = int(np.prod(shape))
        nr = -(-n // LANE)
        out.append(buf[off:off + nr].reshape(-1)[:n].reshape(shape))
    return out


def _row128(v):
    return jnp.pad(v.astype(F32), (0, LANE - v.shape[0])).reshape(1, LANE)


BIG = ("w_in", "w_out", "w_up", "w_down")
SMALL_REPL = ("ada_b", "norm_pre_mix", "norm_post_mix", "norm_pre_mlp", "norm_post_mlp", "gdn_a_log", "gdn_dt_bias",
              "gdn_norm_w", "ssm_conv_b", "ssm_a_log", "ssm_dt_bias", "ssm_d", "ssm_norm_w", "attn_sinks")
SMALL_SHARD = ("gdn_conv_w", "ssm_conv_w", "sc_conv_w")
WEIGHTS = ("ada_w", "ada_b", "norm_pre_mix", "norm_post_mix", "norm_pre_mlp", "norm_post_mlp", "w_in", "w_out",
           "gdn_conv_w", "gdn_a_log", "gdn_dt_bias", "gdn_norm_w", "ssm_conv_w", "ssm_conv_b", "ssm_a_log",
           "ssm_dt_bias", "ssm_d", "ssm_norm_w", "attn_sinks", "sc_conv_w", "w_up", "w_down")


def _step(p, m, v, x, c, positions, loss_target):
    d = x.shape[2]
    xi, yi, ci = _me()
    chip = 2 * xi + yi
    me = 4 * xi + 2 * yi + ci
    x0 = x[0]
    tgt = loss_target[0]

    buf, spec = _pack([c[0], p["gdn_conv_w"], p["ssm_conv_w"], p["sc_conv_w"]])
    got = allgather8("ag_small_in", buf)
    per_dev = [_unpack(got[k], spec) for k in range(8)]
    c_all = jnp.stack([pd[0] for pd in per_dev], axis=0)
    conv_full = [jnp.concatenate([per_dev[4 * a + 2 * b][1 + j] for a in range(2) for b in range(2)], axis=-1)
                 for j in range(3)]
    gdn_conv_w, ssm_conv_w, sc_conv_w = conv_full

    c_pad = jnp.pad(c_all, ((0, 8), (0, 0)))
    silu_bf = lambda a: _silu(a).astype(BF16)
    nsh = p["ada_w"].shape[2]
    mods = []
    for l in range(DEPTH):
        mo = mm_nn(f"ada_fwd{l}", c_pad, p["ada_w"][l], F32, a_fn=silu_bf)
        mods.append(mo[:8] + lax.dynamic_slice(p["ada_b"][l], (chip * nsh,), (nsh,))[None])
    buf, spec = _pack([jnp.stack(mods, axis=0)])
    got = allgather8("ag_mod", buf)
    mod_sh = [_unpack(got[4 * a + 2 * b], spec)[0] for a in range(2) for b in range(2)]
    mod_all = jnp.concatenate(mod_sh, axis=-1)
    mod = lax.dynamic_index_in_dim(mod_all, me, axis=1, keepdims=False)

    items = [(l, n) for l in range(DEPTH) for n in BIG]
    gathered = dict(zip(items, allgather_chips("ag_weights", [p[n][l].astype(BF16) for (l, n) in items])))
    wfull = []
    for l in range(DEPTH):
        w_in = jnp.concatenate([gathered[l, "w_in"][s] for s in range(4)], axis=1)
        wfull.append(dict(w_in=_relayout_w_in(w_in), w_out=_relayout_w_out(gathered[l, "w_out"].reshape(-1, d)),
                          w_up=gathered[l, "w_up"], w_down=gathered[l, "w_down"].reshape(-1, d)))

    inv_freq = ROPE_THETA ** (-jnp.arange(0, 64, 2, dtype=F32) / 64)
    ang = positions[0].astype(F32)[:, None] * inv_freq
    z32 = jnp.zeros_like(ang)
    cos_t = jnp.concatenate([jnp.cos(ang), z32, jnp.cos(ang), z32], axis=1)
    sin_t = jnp.concatenate([-jnp.sin(ang), z32, jnp.sin(ang), z32], axis=1)

    relu2 = lambda a: jnp.square(jnp.maximum(a, 0))

    def layer_consts(l):
        row = lambda v_: v_.reshape(1, -1).astype(F32)
        return dict(
            alog=_row128(p["gdn_a_log"][l]), gdtb=_row128(p["gdn_dt_bias"][l]), gnw=row(p["gdn_norm_w"][l]),
            scb=row(p["ssm_conv_b"][l]), salog=_row128(p["ssm_a_log"][l]), sdtb=_row128(p["ssm_dt_bias"][l]),
            sdsk=_row128(p["ssm_d"][l]), snw=row(p["ssm_norm_w"][l]), sinks=_row128(p["attn_sinks"][l]))

    saved = []
    xc = x0
    for l in range(DEPTH):
        w = wfull[l]
        k = layer_consts(l)
        md = [mod[l, j * d:(j + 1) * d].reshape(1, d) for j in range(6)]
        npre, npost = p["norm_pre_mix"][l].reshape(1, d), p["norm_post_mix"][l].reshape(1, d)
        npre2, npost2 = p["norm_pre_mlp"][l].reshape(1, d), p["norm_post_mlp"][l].reshape(1, d)
        (h,), _ = rowmap(f"premix{l}", lambda rv, cv: (f_premod(rv, cv), []), [(xc, 0, d)], [npre, md[1], md[0]],
                         [(d, BF16)])
        proj = mm_nn(f"inproj{l}", h, w["w_in"], F32)
        cq = conv_fwd(f"gdn_conv{l}", [(proj, O_GQKV, 1536)], gdn_conv_w[l])
        ya, gs, gt = gdn_fwd(f"gdn{l}", cq, proj, k["alog"], k["gdtb"], k["gnw"])
        cx = conv_fwd(f"ssm_conv{l}", [(proj, O_XBC, 1024)], ssm_conv_w[l])
        yb, ss = ssd_fwd(f"ssd{l}", cx, proj, k["scb"], k["salog"], k["sdtb"], k["sdsk"], k["snw"])
        yc = swa_fwd(f"swa{l}", proj, cos_t, sin_t, k["sinks"])
        cd = conv_fwd(f"sc_conv{l}", [(proj, O_CC, GW), (proj, O_CH, GW)], sc_conv_w[l])
        (ycat,), _ = rowmap(f"ycat{l}", lambda rv, cv: ([jnp.concatenate([rv[0], rv[1], rv[2], rv[3] * rv[4]], 1)], []),
                            [(ya, 0, GW), (yb, 0, GW), (yc, 0, 1024), (proj, O_CB, GW), (cd, 0, GW)], [],
                            [(YCAT, BF16)])
        y = mm_nn(f"outproj{l}", ycat, w["w_out"], F32)
        (x1,), _ = rowmap(f"postmix{l}", lambda rv, cv: (f_postres(rv, cv), []), [(xc, 0, d), (y, 0, d)],
                          [npost, md[2]], [(d, F32)])
        (h2,), _ = rowmap(f"premlp{l}", lambda rv, cv: (f_premod(rv, cv), []), [(x1, 0, d)], [npre2, md[4], md[3]],
                          [(d, BF16)])
        u = mm_nn(f"up{l}", h2, w["w_up"], BF16)
        y2 = mm_nn(f"down{l}", u, w["w_down"], F32, a_fn=relu2)
        (x2,), _ = rowmap(f"postmlp{l}", lambda rv, cv: (f_postres(rv, cv), []), [(x1, 0, d), (y2, 0, d)],
                          [npost2, md[5]], [(d, F32)])
        saved.append(dict(x=xc, h=h, proj=proj, cq=cq, gs=gs, gt=gt, cx=cx, ss=ss, cd=cd, ycat=ycat, y=y, x1=x1, h2=h2, u=u,
                          y2=y2, md=md, k=k))
        xc = x2

    def f_loss(rv, cv):
        err = rv[0] - rv[1]
        part = 0.5 * jnp.sum(jnp.mean(err * err, axis=-1, keepdims=True), axis=0, keepdims=True)
        return [err * (1.0 / d)], [jnp.broadcast_to(part, (1, LANE))]

    (dx,), (loss_part,) = rowmap("loss", f_loss, [(xc, 0, d), (tgt, 0, d)], [], [(d, F32)], [(1, LANE)])

    gsmall = [None] * DEPTH
    gbig = [None] * DEPTH
    dmods = [None] * DEPTH
    for l in reversed(range(DEPTH)):
        w, s = wfull[l], saved[l]
        k, md = s["k"], s["md"]
        npre, npost = p["norm_pre_mix"][l].reshape(1, d), p["norm_post_mix"][l].reshape(1, d)
        npre2, npost2 = p["norm_pre_mlp"][l].reshape(1, d), p["norm_post_mlp"][l].reshape(1, d)
        (dx1a, dy2), (dnpost2, dgate_m) = rowmap_vjp(f"postmlp_b{l}", f_postres, [(s["x1"], 0, d), (s["y2"], 0, d)],
                                                      [npost2, md[5]], [(dx, 0, d)], 2, [F32, BF16])
        du = mm_nn(f"down_b{l}", dy2, w["w_down"], BF16, extra=s["u"], wt=True,
                   epi=lambda r, uu: r * (2.0 * jnp.maximum(uu.astype(F32), 0.0)))
        g_down = mm_tn(f"down_g{l}", s["u"], dy2, a_fn=relu2)
        dh2 = mm_nn(f"up_b{l}", du, w["w_up"], F32, wt=True)
        g_up = mm_tn(f"up_g{l}", s["h2"], du, col_slabs=4)
        (dx1,), (dnpre2, dscale_m, dshift_m) = rowmap_vjp(f"premlp_b{l}", f_premod, [(s["x1"], 0, d)],
                                                          [npre2, md[4], md[3]], [(dh2, 0, d)], 1, [F32],
                                                          add_rows=[(dx1a, 0, d)])
        (dxa, dy), (dnpost, dgate_a) = rowmap_vjp(f"postmix_b{l}", f_postres, [(s["x"], 0, d), (s["y"], 0, d)],
                                                  [npost, md[2]], [(dx1, 0, d)], 2, [F32, BF16])
        dycat = mm_nn(f"outproj_b{l}", dy, w["w_out"], F32, wt=True)
        g_out = mm_tn(f"outproj_g{l}", s["ycat"], dy)
        (dcb, dcd), _ = rowmap(f"scgate_b{l}", lambda rv, cv: ([rv[0] * rv[2], rv[0] * rv[1]], []),
                               [(dycat, 2048, GW), (s["proj"], O_CB, GW), (s["cd"], 0, GW)], [],
                               [(GW, BF16), (GW, F32)])
        (dcc, dch), g_scw = conv_bwd(f"sc_conv_b{l}", dcd, [(s["proj"], O_CC, GW), (s["proj"], O_CH, GW)],
                                     sc_conv_w[l], BF16)
        daq, dak, dav, g_sinks = swa_bwd(f"swa_b{l}", dycat, 1024, s["proj"], cos_t, sin_t, k["sinks"])
        dcx, dsz, dsdt, g_scb, g_salog, g_sdtb, g_sdsk, g_snw = ssd_bwd(
            f"ssd_b{l}", dycat, 512, s["cx"], s["proj"], s["ss"], k["scb"], k["salog"], k["sdtb"], k["sdsk"], k["snw"])
        (dxbc,), g_ssmw = conv_bwd(f"ssm_conv_b{l}", dcx, [(s["proj"], O_XBC, 1024)], ssm_conv_w[l], BF16)
        dcq, dgz, dgba, g_alog, g_gdtb, g_gnw = gdn_bwd(f"gdn_b{l}", dycat, 0, s["cq"], s["proj"], s["gs"], s["gt"],
                                                        k["alog"], k["gdtb"], k["gnw"])
        (dgqkv,), g_gdnw = conv_bwd(f"gdn_conv_b{l}", dcq, [(s["proj"], O_GQKV, 1536)], gdn_conv_w[l], BF16)
        dproj = jnp.concatenate([dgqkv, dcb, dcc, dch, dxbc, daq, dgz, dsz, dak, dav, dgba, dsdt], axis=1)
        dh = mm_nn(f"inproj_b{l}", dproj, w["w_in"], F32, wt=True)
        g_in = mm_tn(f"inproj_g{l}", s["h"], dproj)
        (dx,), (dnpre, dscale_a, dshift_a) = rowmap_vjp(f"premix_b{l}", f_premod, [(s["x"], 0, d)],
                                                        [npre, md[1], md[0]], [(dh, 0, d)], 1, [F32],
                                                        add_rows=[(dxa, 0, d)])
        dmods[l] = jnp.concatenate([dshift_a, dscale_a, dgate_a, dshift_m, dscale_m, dgate_m], axis=1)[0]
        gsmall[l] = dict(
            norm_pre_mix=dnpre[0], norm_post_mix=dnpost[0], norm_pre_mlp=dnpre2[0], norm_post_mlp=dnpost2[0],
            gdn_a_log=g_alog[0, :4], gdn_dt_bias=g_gdtb[0, :4], gdn_norm_w=g_gnw[0], ssm_conv_b=g_scb[0],
            ssm_a_log=g_salog[0, :8], ssm_dt_bias=g_sdtb[0, :8], ssm_d=g_sdsk[0, :8], ssm_norm_w=g_snw[0],
            attn_sinks=g_sinks[0, :8], gdn_conv_w=g_gdnw, ssm_conv_w=g_ssmw, sc_conv_w=g_scw)
        gbig[l] = dict(w_in=jnp.stack(jnp.split(_unlayout_g_in(g_in), 4, axis=1), axis=0),
                       w_out=_unlayout_g_out(g_out).reshape(4, -1, d), w_up=g_up,
                       w_down=g_down.reshape(4, -1, d))
    grad_x = dx[None]

    buf, spec = _pack([jnp.stack(dmods, axis=0)])
    got = allgather8("ag_dmod", buf)
    dmod_sum = _unpack(sum_slabs("dmod_sum", got), spec)[0]
    dmod_all = jnp.stack([_unpack(got[kk], spec)[0] for kk in range(8)], axis=1)
    g_ada_w = []
    for l in range(DEPTH):
        dm = lax.dynamic_slice(dmod_all[l], (0, chip * nsh), (8, nsh))
        g_ada_w.append(mm_tn(f"ada_g{l}", c_pad, jnp.pad(dm, ((0, 8), (0, 0))), a_fn=silu_bf))
    g_ada_w = jnp.stack(g_ada_w, axis=0)

    names = [n for n in SMALL_REPL if n != "ada_b"] + list(SMALL_SHARD)
    buf, spec = _pack([jnp.stack([gsmall[l][n] for l in range(DEPTH)], axis=0) for n in names] + [loss_part])
    tot = _unpack(sum_slabs("small_sum", allgather8("ag_small_g", buf)), spec)
    gfull = dict(zip(names, tot[:-1]))
    loss = tot[-1][0, 0]
    grads = {"ada_w": g_ada_w, "ada_b": dmod_sum}
    for n in SMALL_REPL:
        if n != "ada_b":
            grads[n] = gfull[n]
    for n in SMALL_SHARD:
        wdt = p[n].shape[2]
        grads[n] = lax.dynamic_slice_in_dim(gfull[n], chip * wdt, wdt, axis=2)

    idx = jnp.stack([chip, ci]).astype(jnp.int32)
    g4 = [gbig[l][n].reshape(4, 2, gbig[l][n].shape[1] // 2, gbig[l][n].shape[2]) for (l, n) in items]
    recv1 = sibling_swap_halves("rs_sibling", g4)
    part = [add_half(f"rs_add_{n}{l}", g, r1, idx) for (l, n), g, r1 in zip(items, g4, recv1)]
    recv2 = chips_exchange("rs_chips", part)
    mine = [rs_sum(f"rs_sum_{n}{l}", g, r1, r2, idx) for (l, n), g, r1, r2 in zip(items, g4, recv1, recv2)]
    other = sibling_send("rs_join", mine)
    first = ci == 0
    gsh = {it: jnp.concatenate([jnp.where(first, a, b), jnp.where(first, b, a)], axis=0)
           for it, a, b in zip(items, mine, other)}
    for n in BIG:
        grads[n] = jnp.stack([gsh[l, n] for l in range(DEPTH)], axis=0)

    delta, new_m, new_v = {}, {}, {}
    for n in ("ada_w",) + BIG:
        sh = p[n].shape
        two = lambda a: a.reshape(sh[0] * sh[1], sh[2])
        dl, m2, v2 = adamw(f"adamw_{n}", two(p[n]), two(grads[n]), two(m[n]), two(v[n]))
        delta[n], new_m[n], new_v[n] = dl.reshape(sh), m2.reshape(sh), v2.reshape(sh)
    small = list(SMALL_REPL) + list(SMALL_SHARD)
    bufs = []
    for src in (p, grads, m, v):
        b_, spec = _pack([src[n] for n in small])
        bufs.append(b_)
    dl, m2, v2 = adamw("adamw_small", *bufs)
    for n, a, b_, cc_ in zip(small, _unpack(dl, spec), _unpack(m2, spec), _unpack(v2, spec)):
        delta[n], new_m[n], new_v[n] = a, b_, cc_

    return (loss, grad_x, *[grads[n] for n in WEIGHTS], *[delta[n] for n in WEIGHTS], *[new_m[n] for n in WEIGHTS],
            *[new_v[n] for n in WEIGHTS])


def kernel(x, c, positions, ada_w, ada_b, norm_pre_mix, norm_post_mix, norm_pre_mlp, norm_post_mlp, w_in, w_out, gdn_conv_w, gdn_a_log, gdn_dt_bias, gdn_norm_w, ssm_conv_w, ssm_conv_b, ssm_a_log, ssm_dt_bias, ssm_d, ssm_norm_w, attn_sinks, sc_conv_w, w_up, w_down, loss_target, m_ada_w, m_ada_b, m_norm_pre_mix, m_norm_post_mix, m_norm_pre_mlp, m_norm_post_mlp, m_w_in, m_w_out, m_gdn_conv_w, m_gdn_a_log, m_gdn_dt_bias, m_gdn_norm_w, m_ssm_conv_w, m_ssm_conv_b, m_ssm_a_log, m_ssm_dt_bias, m_ssm_d, m_ssm_norm_w, m_attn_sinks, m_sc_conv_w, m_w_up, m_w_down, v_ada_w, v_ada_b, v_norm_pre_mix, v_norm_post_mix, v_norm_pre_mlp, v_norm_post_mlp, v_w_in, v_w_out, v_gdn_conv_w, v_gdn_a_log, v_gdn_dt_bias, v_gdn_norm_w, v_ssm_conv_w, v_ssm_conv_b, v_ssm_a_log, v_ssm_dt_bias, v_ssm_d, v_ssm_norm_w, v_attn_sinks, v_sc_conv_w, v_w_up, v_w_down):
    loc = locals()
    p = {n: loc[n] for n in WEIGHTS}
    m = {n: loc["m_" + n] for n in WEIGHTS}
    v = {n: loc["v_" + n] for n in WEIGHTS}
    return _step(p, m, v, x, c, positions, loss_target)
```

```python
import functools

import numpy as np
import jax
import jax.numpy as jnp
from jax import lax
from jax.experimental import pallas as pl
from jax.experimental.pallas import tpu as pltpu

F32, BF16 = jnp.float32, jnp.bfloat16
MESH = pl.DeviceIdType.MESH

DEPTH = 2
GW = 512
N_GDN, GDN_CS = 4, 64
SSD_CS, ATT_W = 128, 128
EPS = 1e-6
ROPE_THETA = 10000.0
IN_WIDTH = 5904
B1, B2, LR, AEPS, WD, STEP = 0.9, 0.999, 0.001, 1e-8, 0.01, 10

VMEM_LIMIT = 52 * 1024 * 1024
LANE = 128

O_GQKV, O_CB, O_CC, O_CH, O_XBC, O_AQ, O_GZ, O_SZ, O_AK, O_AV, O_GBA, O_SDT = (
    0, 1536, 2048, 2560, 3072, 4096, 5120, 5632, 6144, 6400, 6656, 6784)
PW = 6912
YCAT = 2560


def _cparams(*sem):
    return pltpu.CompilerParams(dimension_semantics=sem or None, vmem_limit_bytes=VMEM_LIMIT)


def _pick(n, cap, mult):
    if n <= cap:
        return n
    best = None
    for d in range(mult, cap + 1, mult):
        if n % d == 0:
            best = d
    assert best is not None, (n, cap, mult)
    return best


def _pieces(x, n):
    out, r = [], x
    for i in range(n):
        h = r.astype(BF16)
        out.append(h)
        if i < n - 1:
            r = r - h.astype(F32)
    return out


def _dgp(a, b, ca, cb, na, nb):
    dn = (((ca,), (cb,)), ((), ()))
    acc = None
    for i, pa in enumerate(_pieces(a, na)):
        for j, pb in enumerate(_pieces(b, nb)):
            if i + j < max(na, nb):
                t = lax.dot_general(pa, pb, dn, preferred_element_type=F32)
                acc = t if acc is None else acc + t
    return acc


_CONTRACT = {"nn": (1, 0), "nt": (1, 1), "tn": (0, 0)}


@functools.lru_cache(maxsize=None)
def _mm(kind, na, nb, const_a=False):
    ca, cb = _CONTRACT[kind]
    ng = max(na, nb)

    @jax.custom_vjp
    def f(a, b):
        return _dgp(a, b, ca, cb, na, nb)

    def bwd(r, g):
        a, b = r
        if kind == "nn":
            da, db = (lambda: _mm("nt", ng, nb)(g, b)), _mm("tn", na, ng, const_a)(a, g)
        elif kind == "nt":
            da, db = (lambda: _mm("nn", ng, nb)(g, b)), _mm("tn", ng, na)(g, a)
        else:
            da, db = (lambda: _mm("nt", nb, ng)(b, g)), _mm("nn", na, ng, const_a)(a, g)
        return (jnp.zeros_like(a) if const_a else da()), db

    f.defvjp(lambda a, b: (f(a, b), (a, b)), bwd)
    return f


_nn, _nt, _tn = _mm("nn", 1, 1), _mm("nt", 1, 1), _mm("tn", 1, 1)
_hnn, _hnt, _htn = _mm("nn", 2, 2), _mm("nt", 2, 2), _mm("tn", 2, 2)
_xnn = _mm("nn", 1, 3, True)


def _silu(x):
    return x * jax.nn.sigmoid(x)


def _softplus(x):
    return jnp.maximum(x, 0.0) + jnp.log1p(jnp.exp(-jnp.abs(x)))


def _colsel(v, idx):
    lane = lax.broadcasted_iota(jnp.int32, v.shape, 1)
    return jnp.sum(jnp.where(lane == idx, v, 0.0), axis=1, keepdims=True)


def _tri(n):
    r = lax.broadcasted_iota(jnp.int32, (n, n), 0)
    c = lax.broadcasted_iota(jnp.int32, (n, n), 1)
    return r >= c, r > c


def _rowsel(v, idx):
    row = lax.broadcasted_iota(jnp.int32, v.shape, 0)
    return jnp.sum(jnp.where(row == idx, v, 0.0), axis=0, keepdims=True)


def _tri_inv(m, size, nil):
    r = lax.broadcasted_iota(jnp.int32, (size, size), 0)
    c = lax.broadcasted_iota(jnp.int32, (size, size), 1)
    x = jnp.where(r == c, 1.0, 0.0) - m
    p = _hnn(m, m)
    steps = int(np.log2(nil)) - 1
    for s in range(steps):
        x = x + _hnn(x, p)
        if s < steps - 1:
            p = _hnn(p, p)
    return x


@jax.custom_vjp
def _inv_given(m, tinv):
    return tinv


_inv_given.defvjp(lambda m, tinv: (tinv, tinv),
                  lambda tinv, g: (-_hnt(_htn(tinv, g), tinv), jnp.zeros_like(tinv)))


def _gdn_chunk(cq, z, ba, s, alog, dtb, nw, tinv_saved=None):
    n, nh = GDN_CS, N_GDN
    rr = n * nh

    def stack(a, off):
        return jnp.concatenate([a[:, off + h * LANE:off + (h + 1) * LANE] for h in range(nh)], axis=0)

    def percol(f):
        return jnp.concatenate([f(h) for h in range(nh)], axis=0)

    def own(a):
        return jnp.concatenate([a[h * n:(h + 1) * n, h * LANE:(h + 1) * LANE] for h in range(nh)], axis=0)

    q, k, v = _silu(stack(cq, 0)), _silu(stack(cq, GW)), _silu(stack(cq, 2 * GW))
    q = q * lax.rsqrt(jnp.sum(q * q, -1, keepdims=True) + EPS) * (128 ** -0.5)
    k = k * lax.rsqrt(jnp.sum(k * k, -1, keepdims=True) + EPS)
    beta = jax.nn.sigmoid(percol(lambda h: _colsel(ba, h)))
    g = percol(lambda h: -jnp.exp(_colsel(alog, h)) * _softplus(_colsel(ba, nh + h) + _colsel(dtb, h)))
    ri = lax.broadcasted_iota(jnp.int32, (rr, rr), 0)
    ci = lax.broadcasted_iota(jnp.int32, (rr, rr), 1)
    same = (ri >> 6) == (ci >> 6)
    incl, strict = same & (ri >= ci), same & (ri > ci)
    gcb = _xnn(incl.astype(F32), jnp.broadcast_to(g, (rr, LANE)))
    gci = jnp.concatenate([gcb, gcb], axis=1)
    dec = jnp.where(incl, jnp.exp(jnp.where(incl, gci - gci.T, 0.0)), 0.0)
    kb = k * beta
    kk = _nt(jnp.concatenate([kb, q], axis=0), k)
    m = jnp.where(strict, kk[:rr] * dec, 0.0)
    attn = jnp.where(incl, kk[rr:] * dec, 0.0)
    tinv = _tri_inv(m, rr, n) if tinv_saved is None else _inv_given(m, tinv_saved)
    egc = jnp.exp(gcb)
    uw = _nn(tinv, jnp.concatenate([v * beta, kb * egc], axis=1))
    u, w = uw[:, :LANE], uw[:, LANE:]
    gl = [jnp.sum(g[h * n:(h + 1) * n], axis=0, keepdims=True) for h in range(nh)]
    kd = k * jnp.exp(percol(lambda h: jnp.broadcast_to(gl[h], (n, 1))) - gcb)
    ws = _nn(jnp.concatenate([w, q * egc], axis=0), s)
    v_new = u - own(ws[:rr])
    o = own(ws[rr:]) + _nn(attn, v_new)
    rowh = lax.broadcasted_iota(jnp.int32, (rr, LANE), 0) >> 6
    vexp = jnp.concatenate([jnp.where(rowh == h, v_new, 0.0) for h in range(nh)], axis=1)
    cd = jnp.concatenate([jnp.broadcast_to(jnp.exp(gl[h]), (1, LANE)) for h in range(nh)], axis=1)
    s_new = s * cd + _tn(kd, vexp)
    y = o * lax.rsqrt(jnp.mean(o * o, -1, keepdims=True) + EPS) * nw * _silu(stack(z, 0))
    y = jnp.concatenate([y[h * n:(h + 1) * n] for h in range(nh)], axis=1)
    return (y, s_new, tinv) if tinv_saved is None else (y, s_new)


def _ssd_chunk(cx, z, dtc, s, convb, alog, dtb, dsk, nw):
    n = SSD_CS
    incl, _ = _tri(n)
    xbc = _silu(cx + convb)
    dt_all = _softplus(dtc + dtb)
    da_all = dt_all * (-jnp.exp(alog))
    cs_all = _xnn(incl.astype(F32), da_all)
    cs_t = cs_all.T
    tot_all = jnp.sum(da_all, axis=0, keepdims=True)
    lane_lo = lax.broadcasted_iota(jnp.int32, (n, LANE), 1) < 64
    ys, s_out = [], []
    for pi in range(4):
        g = pi // 2
        bm = xbc[:, GW + g * LANE:GW + (g + 1) * LANE]
        cm = xbc[:, GW + 256 + g * LANE:GW + 256 + (g + 1) * LANE]
        cbm = _nt(cm, bm)
        xp, s_p = xbc[:, pi * LANE:(pi + 1) * LANE], s[:, pi * LANE:(pi + 1) * LANE]
        a_rows, b_cols, dts, dsks, etot = [], [], [], [], []
        for h in (2 * pi, 2 * pi + 1):
            cs_i = _colsel(cs_all, h)
            lm = jnp.where(incl, jnp.exp(jnp.where(incl, cs_i - _rowsel(cs_t, h), 0.0)), 0.0)
            tot = _colsel(tot_all, h)
            a_rows.append(jnp.concatenate([cbm * lm, cm * jnp.exp(cs_i)], axis=1))
            b_cols.append(bm * jnp.exp(tot - cs_i))
            dts.append(_colsel(dt_all, h))
            dsks.append(_colsel(dsk, h))
            etot.append(jnp.exp(tot))
        xdt = xp * jnp.where(lane_lo, dts[0], dts[1])
        yy = _nn(jnp.concatenate(a_rows, axis=0), jnp.concatenate([xdt, s_p], axis=0))
        ys.append(jnp.where(lane_lo, yy[:n], yy[n:]) + xp * jnp.where(lane_lo[:1], dsks[0], dsks[1]))
        st = _tn(jnp.concatenate(b_cols, axis=1), xdt)
        s_out.append(s_p * jnp.where(lane_lo[:1], etot[0], etot[1]) + jnp.where(lane_lo, st[:n], st[n:]))
    zs = _silu(z)
    out = []
    for g in range(2):
        y = jnp.concatenate(ys[2 * g:2 * g + 2], axis=1) * zs[:, g * 256:(g + 1) * 256]
        out.append(y * lax.rsqrt(jnp.mean(y * y, -1, keepdims=True) + EPS) * nw[:, g * 256:(g + 1) * 256])
    return jnp.concatenate(out, axis=1), jnp.concatenate(s_out, axis=1)


SWA_STACK = 1


def _swa_block(qg, kp, kc, vp, vc, sink, notfirst, *, g):
    w, ns = ATT_W, SWA_STACK
    kb = jnp.concatenate([kp, kc], axis=0)
    vb = jnp.concatenate([vp, vc], axis=0)
    qi = lax.broadcasted_iota(jnp.int32, (ns * w, 2 * w), 0) & (w - 1)
    kj = lax.broadcasted_iota(jnp.int32, (ns * w, 2 * w), 1)
    rel = qi + w - kj
    mask = (rel >= 0) & (rel < w) & ((kj >= w) | (notfirst > 0.5))
    outs = []
    for r0 in range(0, 4, ns):
        q = jnp.concatenate([qg[:, r * LANE:(r + 1) * LANE] for r in range(r0, r0 + ns)], axis=0)
        snk = jnp.concatenate([jnp.broadcast_to(_colsel(sink, 4 * g + r), (w, 1)) for r in range(r0, r0 + ns)], axis=0)
        s = jnp.where(mask, _nt(q, kb) * (64 ** -0.5), -1e30)
        mx = lax.stop_gradient(jnp.maximum(jnp.max(s, axis=-1, keepdims=True), snk))
        p = jnp.where(mask, jnp.exp(s - mx), 0.0)
        p = p / (jnp.sum(p, axis=-1, keepdims=True) + jnp.exp(snk - mx))
        o = _nn(p, vb)
        outs += [o[j * w:(j + 1) * w] for j in range(ns)]
    return jnp.concatenate(outs, axis=1)


def _rope(x, cos, sin):
    return x * cos + pltpu.roll(x, 64, 1) * sin


def _unrope(dy, cos, sin):
    return dy * cos + pltpu.roll(dy * sin, 64, 1)


def rowmap(name, fn, rows, consts, outs, accs=(), tile=256):
    t = rows[0][0].shape[0]
    tile = min(tile, t)
    assert t % tile == 0
    nr, nc, no, na = len(rows), len(consts), len(outs), len(accs)

    def body(*refs):
        rv = [r[...] for r in refs[:nr]]
        cv = [r[...] for r in refs[nr:nr + nc]]
        ov, av = fn(rv, cv)
        for ref, o in zip(refs[nr + nc:nr + nc + no], ov):
            ref[...] = o.astype(ref.dtype)
        if na:
            a_refs = refs[nr + nc + no:]

            @pl.when(pl.program_id(0) == 0)
            def _():
                for ref in a_refs:
                    ref[...] = jnp.zeros_like(ref)

            for ref, a in zip(a_refs, av):
                ref[...] += a

    def cspec(off, width):
        assert off % width == 0
        cb = off // width
        return pl.BlockSpec((tile, width), lambda i: (i, cb))

    in_specs = [cspec(off, wd) for (_, off, wd) in rows]
    in_specs += [pl.BlockSpec(c.shape, lambda i: (0, 0)) for c in consts]
    out_specs = [pl.BlockSpec((tile, wd), lambda i: (i, 0)) for (wd, _) in outs]
    out_specs += [pl.BlockSpec(s, lambda i: (0, 0)) for s in accs]
    out_shape = [jax.ShapeDtypeStruct((t, wd), dt) for (wd, dt) in outs]
    out_shape += [jax.ShapeDtypeStruct(s, F32) for s in accs]
    res = pl.pallas_call(
        body, name=name, grid=(t // tile,), in_specs=in_specs, out_specs=out_specs, out_shape=out_shape,
        compiler_params=_cparams("arbitrary"),
    )(*[r[0] for r in rows], *consts)
    return res[:no], res[no:]


def rowmap_vjp(name, f, rows, consts, douts, n_diff, drow_dtypes, tile=256, add_rows=()):
    nr, nd, nadd = len(rows), len(douts), len(add_rows)

    def g(vals, cvals):
        rv, dov, addv = vals[:nr], vals[nr:nr + nd], vals[nr + nd:]
        fixed = rv[n_diff:]
        _, vjp = jax.vjp(lambda r, c: f(list(r) + list(fixed), c), rv[:n_diff], cvals)
        dr, dc = vjp([d.astype(F32) for d in dov])
        dr = list(dr)
        for a in addv:
            dr[0] = dr[0] + a
        return dr, dc

    outs = [(rows[i][2], drow_dtypes[i]) for i in range(n_diff)]
    accs = [c.shape for c in consts]
    return rowmap(name, g, list(rows) + list(douts) + list(add_rows), consts, outs, accs, tile=tile)


def _carried_copies(kind, in_refs, out_refs, send_sems, recv_sems):
    x, y, c = _me()
    chip = 2 * x + y
    chips = [(1 - x, y), (x, 1 - y), (1 - x, 1 - y)]
    cps = []

    def rc(src, dst, k, to):
        return pltpu.make_async_remote_copy(src_ref=src, dst_ref=dst, send_sem=send_sems.at[k], recv_sem=recv_sems.at[k],
                                            device_id=to, device_id_type=MESH)

    for i, (src, dst) in enumerate(zip(in_refs, out_refs)):
        if kind == "gather":
            rh = src.shape[0] // 2
            half = pl.ds(pl.multiple_of(c * rh, 16), rh)
            cps.append(rc(src, dst.at[chip], 4 * i + 3, (x, y, 1 - c)))
            cps += [rc(src.at[half], dst.at[chip, half], 4 * i + k, (cx, cy, c)) for k, (cx, cy) in enumerate(chips)]
        else:
            cps += [rc(src.at[2 * cx + cy], dst.at[chip], 4 * i + k, (cx, cy, c)) for k, (cx, cy) in enumerate(chips)]
    return cps


def mm_nn(name, a, w, out_dtype, a_fn=None, extra=None, epi=None, tm_cap=1024, wt=False, carry=None):
    nc = len(carry[1]) if carry else 0
    m, k = a.shape
    stacked = w.ndim == 3
    n = (w.shape[-2] if wt else w.shape[-1] * (w.shape[0] if stacked else 1))
    slab = w.shape[-1]
    tm = _pick(m, tm_cap, 16)
    tn = _pick(n if wt else slab, 1024, LANE)
    tk = _pick(slab if wt else k, 2048, LANE)
    nk = k // tk
    per = slab // (tk if wt else tn)
    wdims = ((1,), (1,)) if wt else ((1,), (0,))

    gi, gj = m // tm, n // tn
    n_in = 2 + (extra is not None)

    def body(*refs):
        a_ref, w_ref = refs[0], refs[1]
        e_ref = refs[2] if extra is not None else None
        o_ref, acc = refs[n_in + nc], refs[n_in + 2 * nc + 1]
        kk = pl.program_id(2)
        if nc:
            def copies():
                return _carried_copies(carry[0], refs[n_in:n_in + nc], refs[n_in + nc + 1:n_in + 2 * nc + 1],
                                       refs[-2], refs[-1])

            @pl.when((pl.program_id(0) == 0) & (pl.program_id(1) == 0) & (kk == 0))
            def _():
                for cp in copies():
                    cp.start()

        @pl.when(kk == 0)
        def _():
            acc[...] = jnp.zeros_like(acc)

        av = a_ref[...]
        if a_fn is not None:
            av = a_fn(av)
        acc[...] += lax.dot_general(av.astype(BF16), w_ref[...].astype(BF16), (wdims, ((), ())),
                                    preferred_element_type=F32)

        @pl.when(kk == nk - 1)
        def _():
            r = acc[...]
            if epi is not None:
                r = epi(r, e_ref[...])
            o_ref[...] = r.astype(o_ref.dtype)

        if nc:
            @pl.when((pl.program_id(0) == gi - 1) & (pl.program_id(1) == gj - 1) & (kk == nk - 1))
            def _():
                cps = copies()
                for cp in cps:
                    cp.wait_recv()
                for cp in cps:
                    cp.wait_send()

    if wt:
        w_spec = (pl.BlockSpec((None, tn, tk), lambda i, j, kk: (kk // per, j, kk % per)) if stacked
                  else pl.BlockSpec((tn, tk), lambda i, j, kk: (j, kk)))
    else:
        w_spec = (pl.BlockSpec((None, tk, tn), lambda i, j, kk: (j // per, kk, j % per)) if stacked
                  else pl.BlockSpec((tk, tn), lambda i, j, kk: (kk, j)))
    in_specs = [pl.BlockSpec((tm, tk), lambda i, j, kk: (i, kk)), w_spec]
    ops = [a, w]
    if extra is not None:
        in_specs.append(pl.BlockSpec((tm, tn), lambda i, j, kk: (i, j)))
        ops.append(extra)
    out_spec = pl.BlockSpec((tm, tn), lambda i, j, kk: (i, j))
    out_shape = jax.ShapeDtypeStruct((m, n), out_dtype)
    if not nc:
        return pl.pallas_call(
            body, name=name, grid=(gi, gj, nk), in_specs=in_specs, out_specs=out_spec, out_shape=out_shape,
            scratch_shapes=[pltpu.VMEM((tm, tn), F32)],
            compiler_params=_cparams("parallel", "parallel", "arbitrary"),
        )(*ops)
    landing = [jax.ShapeDtypeStruct((4,) + c.shape if carry[0] == "gather" else c.shape, c.dtype) for c in carry[1]]
    res = pl.pallas_call(
        body, name=name, grid=(gi, gj, nk), in_specs=in_specs + [ANY] * nc, out_specs=[out_spec] + [ANY] * nc,
        out_shape=[out_shape] + landing, scratch_shapes=[pltpu.VMEM((tm, tn), F32)] + _sems(4 * nc),
        compiler_params=_cparams("arbitrary", "arbitrary", "arbitrary"),
    )(*ops, *carry[1])
    return res[0], list(res[1:])


def mm_tn(name, a, b, a_fn=None, col_slabs=1):
    t, m = a.shape
    n = b.shape[1]
    tm, tn, tk = _pick(m, 1024, LANE), _pick(n // col_slabs, 1024, LANE), _pick(t, 2048, 16)
    nk = t // tk
    per = n // col_slabs // tn

    def body(a_ref, b_ref, o_ref, acc):
        kk = pl.program_id(2)

        @pl.when(kk == 0)
        def _():
            acc[...] = jnp.zeros_like(acc)

        av = a_ref[...]
        if a_fn is not None:
            av = a_fn(av)
        acc[...] += lax.dot_general(av.astype(BF16), b_ref[...].astype(BF16), (((0,), (0,)), ((), ())),
                                    preferred_element_type=F32)

        @pl.when(kk == nk - 1)
        def _():
            o_ref[...] = acc[...]

    return pl.pallas_call(
        body, name=name, grid=(m // tm, n // tn, nk),
        in_specs=[pl.BlockSpec((tk, tm), lambda i, j, kk: (kk, i)), pl.BlockSpec((tk, tn), lambda i, j, kk: (kk, j))],
        out_specs=(pl.BlockSpec((tm, tn), lambda i, j, kk: (i, j)) if col_slabs == 1
                   else pl.BlockSpec((None, tm, tn), lambda i, j, kk: (j // per, i, j % per))),
        out_shape=jax.ShapeDtypeStruct((m, n) if col_slabs == 1 else (col_slabs, m, n // col_slabs), F32),
        scratch_shapes=[pltpu.VMEM((tm, tn), F32)],
        compiler_params=_cparams("parallel", "parallel", "arbitrary"),
    )(a, b)


CONV_TILE = 256
HALO = 8


def conv_fwd(name, xs, w):
    t = xs[0][0].shape[0]
    c = xs[0][2]
    kw = w.shape[0]
    tt = min(CONV_TILE, t)
    nx = len(xs)

    def body(*refs):
        cur_refs, prev_refs, w_ref, y_ref = refs[:nx], refs[nx:2 * nx], refs[2 * nx], refs[2 * nx + 1]
        i = pl.program_id(0)
        cur, prev = cur_refs[0][...], prev_refs[0][...]
        for r, p in zip(cur_refs[1:], prev_refs[1:]):
            cur, prev = cur * r[...], prev * p[...]
        prev = jnp.where(i > 0, prev, 0.0)
        xx = jnp.concatenate([prev, cur], axis=0)
        acc = jnp.zeros((tt, c), F32)
        for k in range(kw):
            sh = kw - 1 - k
            rolled = pltpu.roll(xx, sh, 0) if sh else xx
            acc = acc + rolled[HALO:] * w_ref[k:k + 1, :]
        y_ref[...] = acc

    def cur_spec(off):
        cb = off // c
        return pl.BlockSpec((tt, c), lambda i: (i, cb))

    def prev_spec(off):
        cb = off // c
        return pl.BlockSpec((HALO, c), lambda i: (jnp.maximum(i * (tt // HALO) - 1, 0), cb))

    for (_, off, wd) in xs:
        assert wd == c and off % c == 0
    return pl.pallas_call(
        body, name=name, grid=(t // tt,),
        in_specs=[cur_spec(off) for (_, off, _) in xs] + [prev_spec(off) for (_, off, _) in xs]
        + [pl.BlockSpec(w.shape, lambda i: (0, 0))],
        out_specs=pl.BlockSpec((tt, c), lambda i: (i, 0)),
        out_shape=jax.ShapeDtypeStruct((t, c), F32),
        compiler_params=_cparams("arbitrary"),
    )(*[x[0] for x in xs], *[x[0] for x in xs], w)


def conv_bwd(name, dy, xs, w, dx_dtype):
    t = xs[0][0].shape[0]
    c = xs[0][2]
    kw = w.shape[0]
    tt = min(CONV_TILE, t)
    nt = t // tt
    nx = len(xs)
    n = tt + HALO

    def body(*refs):
        dy_ref = refs[0]
        cur_refs, prev_refs = refs[1:1 + nx], refs[1 + nx:1 + 2 * nx]
        w_ref = refs[1 + 2 * nx]
        dx_refs = refs[2 + 2 * nx:2 + 3 * nx]
        dw_ref, carry = refs[2 + 3 * nx], refs[3 + 3 * nx]
        i = pl.program_id(0)

        @pl.when(i == 0)
        def _():
            carry[...] = jnp.zeros_like(carry)
            dw_ref[...] = jnp.zeros_like(dw_ref)

        curs = [r[...] for r in cur_refs]
        cur, prev = curs[0], prev_refs[0][...]
        for cv, p in zip(curs[1:], prev_refs[1:]):
            cur, prev = cur * cv, prev * p[...]
        prev = jnp.where(i < nt - 1, prev, 0.0)
        xx = jnp.concatenate([prev, cur], axis=0)
        dyv = dy_ref[...]
        dpad = jnp.concatenate([jnp.zeros((HALO, c), F32), dyv], axis=0)
        dxx = jnp.zeros((n, c), F32)
        dws = []
        for k in range(kw):
            sh = kw - 1 - k
            dxx = dxx + (pltpu.roll(dpad, n - sh, 0) if sh else dpad) * w_ref[k:k + 1, :]
            rolled = pltpu.roll(xx, sh, 0) if sh else xx
            dws.append(jnp.sum(dyv * rolled[HALO:], axis=0, keepdims=True))
        dw_ref[...] += jnp.concatenate(dws, axis=0)
        dcur = jnp.concatenate([dxx[HALO:tt], dxx[tt:] + carry[...]], axis=0)
        carry[...] = dxx[:HALO]
        for j, ref in enumerate(dx_refs):
            d = dcur
            for jj, cv in enumerate(curs):
                if jj != j:
                    d = d * cv
            ref[...] = d.astype(ref.dtype)

    def cur_spec(off):
        cb = off // c
        return pl.BlockSpec((tt, c), lambda i: (nt - 1 - i, cb))

    def prev_spec(off):
        cb = off // c
        return pl.BlockSpec((HALO, c), lambda i: (jnp.maximum((nt - 1 - i) * (tt // HALO) - 1, 0), cb))

    res = pl.pallas_call(
        body, name=name, grid=(nt,),
        in_specs=[pl.BlockSpec((tt, c), lambda i: (nt - 1 - i, 0))] + [cur_spec(off) for (_, off, _) in xs]
        + [prev_spec(off) for (_, off, _) in xs] + [pl.BlockSpec(w.shape, lambda i: (0, 0))],
        out_specs=[pl.BlockSpec((tt, c), lambda i: (nt - 1 - i, 0)) for _ in xs] + [pl.BlockSpec(w.shape, lambda i: (0, 0))],
        out_shape=[jax.ShapeDtypeStruct((t, c), dx_dtype) for _ in xs] + [jax.ShapeDtypeStruct(w.shape, F32)],
        scratch_shapes=[pltpu.VMEM((HALO, c), F32)],
        compiler_params=_cparams("arbitrary"),
    )(dy, *[x[0] for x in xs], *[x[0] for x in xs], w)
    return res[:nx], res[nx]


def _const_spec(a):
    return pl.BlockSpec(a.shape, lambda i: (0,) * a.ndim)


GDN_STEP = 2


def _gdn_rows(ref, c):
    return ref[c * GDN_CS:(c + 1) * GDN_CS, :]


def gdn_fwd(name, cq, proj, alog, dtb, nw):
    t = cq.shape[0]
    nc = t // GDN_CS
    ns = min(GDN_STEP, nc)
    rows = ns * GDN_CS

    def body(cq_ref, z_ref, ba_ref, alog_ref, dtb_ref, nw_ref, y_ref, ssave_ref, tsave_ref, s_scr):
        @pl.when(pl.program_id(0) == 0)
        def _():
            s_scr[...] = jnp.zeros_like(s_scr)

        s = s_scr[...]
        ys, saved, tinvs = [], [], []
        for c in range(ns):
            saved.append(s)
            y, s, tinv = _gdn_chunk(_gdn_rows(cq_ref, c), _gdn_rows(z_ref, c), _gdn_rows(ba_ref, c), s, alog_ref[...],
                                    dtb_ref[...], nw_ref[...])
            ys.append(y)
            tinvs.append(tinv)
        y_ref[...] = jnp.concatenate(ys, axis=0)
        ssave_ref[...] = jnp.stack(saved, axis=0)
        tsave_ref[...] = jnp.stack(tinvs, axis=0)
        s_scr[...] = s

    return pl.pallas_call(
        body, name=name, grid=(nc // ns,),
        in_specs=[pl.BlockSpec((rows, 3 * GW), lambda i: (i, 0)),
                  pl.BlockSpec((rows, GW), lambda i: (i, O_GZ // GW)),
                  pl.BlockSpec((rows, LANE), lambda i: (i, O_GBA // LANE)),
                  _const_spec(alog), _const_spec(dtb), _const_spec(nw)],
        out_specs=[pl.BlockSpec((rows, GW), lambda i: (i, 0)),
                   pl.BlockSpec((ns, LANE, N_GDN * LANE), lambda i: (i, 0, 0)),
                   pl.BlockSpec((ns, N_GDN * GDN_CS, N_GDN * GDN_CS), lambda i: (i, 0, 0))],
        out_shape=[jax.ShapeDtypeStruct((t, GW), F32), jax.ShapeDtypeStruct((nc, LANE, N_GDN * LANE), F32),
                   jax.ShapeDtypeStruct((nc, N_GDN * GDN_CS, N_GDN * GDN_CS), F32)],
        scratch_shapes=[pltpu.VMEM((LANE, N_GDN * LANE), F32)],
        compiler_params=_cparams("arbitrary"),
    )(cq, proj, proj, alog, dtb, nw)


def gdn_bwd(name, dy, dy_off, cq, proj, ssave, tsave, alog, dtb, nw):
    t = cq.shape[0]
    nc = t // GDN_CS
    ns = min(GDN_STEP, nc)
    rows = ns * GDN_CS

    def body(dy_ref, cq_ref, z_ref, ba_ref, ssave_ref, tsave_ref, alog_ref, dtb_ref, nw_ref,
             dcq_ref, dz_ref, dba_ref, dalog_ref, ddtb_ref, dnw_ref, ds_scr):
        @pl.when(pl.program_id(0) == 0)
        def _():
            ds_scr[...] = jnp.zeros_like(ds_scr)
            dalog_ref[...] = jnp.zeros_like(dalog_ref)
            ddtb_ref[...] = jnp.zeros_like(ddtb_ref)
            dnw_ref[...] = jnp.zeros_like(dnw_ref)

        ds = ds_scr[...]
        dcq, dzs, dbas = [None] * ns, [None] * ns, [None] * ns
        dal, ddt, dnw = (jnp.zeros(r.shape, F32) for r in (dalog_ref, ddtb_ref, dnw_ref))
        for c in reversed(range(ns)):
            _, vjp = jax.vjp(functools.partial(_gdn_chunk, tinv_saved=tsave_ref[c]), _gdn_rows(cq_ref, c),
                             _gdn_rows(z_ref, c), _gdn_rows(ba_ref, c), ssave_ref[c], alog_ref[...], dtb_ref[...],
                             nw_ref[...])
            dcq[c], dzs[c], dbas[c], ds, dal_c, ddt_c, dnw_c = vjp((_gdn_rows(dy_ref, c), ds))
            dal, ddt, dnw = dal + dal_c, ddt + ddt_c, dnw + dnw_c
        dcq_ref[...] = jnp.concatenate(dcq, axis=0)
        dz_ref[...] = jnp.concatenate(dzs, axis=0).astype(dz_ref.dtype)
        dba_ref[...] = jnp.concatenate(dbas, axis=0).astype(dba_ref.dtype)
        ds_scr[...] = ds
        dalog_ref[...] += dal
        ddtb_ref[...] += ddt
        dnw_ref[...] += dnw

    rev = lambda i: nc // ns - 1 - i
    return pl.pallas_call(
        body, name=name, grid=(nc // ns,),
        in_specs=[pl.BlockSpec((rows, GW), lambda i: (rev(i), dy_off // GW)),
                  pl.BlockSpec((rows, 3 * GW), lambda i: (rev(i), 0)),
                  pl.BlockSpec((rows, GW), lambda i: (rev(i), O_GZ // GW)),
                  pl.BlockSpec((rows, LANE), lambda i: (rev(i), O_GBA // LANE)),
                  pl.BlockSpec((ns, LANE, N_GDN * LANE), lambda i: (rev(i), 0, 0)),
                  pl.BlockSpec((ns, N_GDN * GDN_CS, N_GDN * GDN_CS), lambda i: (rev(i), 0, 0)),
                  _const_spec(alog), _const_spec(dtb), _const_spec(nw)],
        out_specs=[pl.BlockSpec((rows, 3 * GW), lambda i: (rev(i), 0)),
                   pl.BlockSpec((rows, GW), lambda i: (rev(i), 0)),
                   pl.BlockSpec((rows, LANE), lambda i: (rev(i), 0)),
                   _const_spec(alog), _const_spec(dtb), _const_spec(nw)],
        out_shape=[jax.ShapeDtypeStruct((t, 3 * GW), F32), jax.ShapeDtypeStruct((t, GW), BF16),
                   jax.ShapeDtypeStruct((t, LANE), BF16), jax.ShapeDtypeStruct(alog.shape, F32),
                   jax.ShapeDtypeStruct(dtb.shape, F32), jax.ShapeDtypeStruct(nw.shape, F32)],
        scratch_shapes=[pltpu.VMEM((LANE, N_GDN * LANE), F32)],
        compiler_params=_cparams("arbitrary"),
    )(dy, cq, proj, proj, ssave, tsave, alog, dtb, nw)


def ssd_fwd(name, cx, proj, convb, alog, dtb, dsk, nw):
    t = cx.shape[0]
    nc = t // SSD_CS

    def body(cx_ref, z_ref, dt_ref, cb_ref, alog_ref, dtb_ref, dsk_ref, nw_ref, y_ref, ssave_ref, s_scr):
        @pl.when(pl.program_id(0) == 0)
        def _():
            s_scr[...] = jnp.zeros_like(s_scr)

        s_in = s_scr[...]
        y, s_new = _ssd_chunk(cx_ref[...], z_ref[...], dt_ref[...], s_in, cb_ref[...], alog_ref[...], dtb_ref[...],
                              dsk_ref[...], nw_ref[...])
        y_ref[...] = y
        ssave_ref[0] = s_in
        s_scr[...] = s_new

    return pl.pallas_call(
        body, name=name, grid=(nc,),
        in_specs=[pl.BlockSpec((SSD_CS, 1024), lambda i: (i, 0)),
                  pl.BlockSpec((SSD_CS, GW), lambda i: (i, O_SZ // GW)),
                  pl.BlockSpec((SSD_CS, LANE), lambda i: (i, O_SDT // LANE)),
                  _const_spec(convb), _const_spec(alog), _const_spec(dtb), _const_spec(dsk), _const_spec(nw)],
        out_specs=[pl.BlockSpec((SSD_CS, GW), lambda i: (i, 0)),
                   pl.BlockSpec((1, LANE, GW), lambda i: (i, 0, 0))],
        out_shape=[jax.ShapeDtypeStruct((t, GW), F32), jax.ShapeDtypeStruct((nc, LANE, GW), F32)],
        scratch_shapes=[pltpu.VMEM((LANE, GW), F32)],
        compiler_params=_cparams("arbitrary"),
    )(cx, proj, proj, convb, alog, dtb, dsk, nw)


def ssd_bwd(name, dy, dy_off, cx, proj, ssave, convb, alog, dtb, dsk, nw):
    t = cx.shape[0]
    nc = t // SSD_CS

    def body(dy_ref, cx_ref, z_ref, dt_ref, ssave_ref, cb_ref, alog_ref, dtb_ref, dsk_ref, nw_ref,
             dcx_ref, dz_ref, ddt_ref, dcb_ref, dalog_ref, ddtb_ref, ddsk_ref, dnw_ref, ds_scr):
        @pl.when(pl.program_id(0) == 0)
        def _():
            ds_scr[...] = jnp.zeros_like(ds_scr)
            for r in (dcb_ref, dalog_ref, ddtb_ref, ddsk_ref, dnw_ref):
                r[...] = jnp.zeros_like(r)

        _, vjp = jax.vjp(_ssd_chunk, cx_ref[...], z_ref[...], dt_ref[...], ssave_ref[0], cb_ref[...], alog_ref[...],
                         dtb_ref[...], dsk_ref[...], nw_ref[...])
        dcx, dz, ddt, ds, dcb, dal, ddtb, ddsk, dnw = vjp((dy_ref[...], ds_scr[...]))
        dcx_ref[...] = dcx
        dz_ref[...] = dz.astype(dz_ref.dtype)
        ddt_ref[...] = ddt.astype(ddt_ref.dtype)
        ds_scr[...] = ds
        dcb_ref[...] += dcb
        dalog_ref[...] += dal
        ddtb_ref[...] += ddtb
        ddsk_ref[...] += ddsk
        dnw_ref[...] += dnw

    rev = lambda i: nc - 1 - i
    return pl.pallas_call(
        body, name=name, grid=(nc,),
        in_specs=[pl.BlockSpec((SSD_CS, GW), lambda i: (rev(i), dy_off // GW)),
                  pl.BlockSpec((SSD_CS, 1024), lambda i: (rev(i), 0)),
                  pl.BlockSpec((SSD_CS, GW), lambda i: (rev(i), O_SZ // GW)),
                  pl.BlockSpec((SSD_CS, LANE), lambda i: (rev(i), O_SDT // LANE)),
                  pl.BlockSpec((1, LANE, GW), lambda i: (rev(i), 0, 0)),
                  _const_spec(convb), _const_spec(alog), _const_spec(dtb), _const_spec(dsk), _const_spec(nw)],
        out_specs=[pl.BlockSpec((SSD_CS, 1024), lambda i: (rev(i), 0)),
                   pl.BlockSpec((SSD_CS, GW), lambda i: (rev(i), 0)),
                   pl.BlockSpec((SSD_CS, LANE), lambda i: (rev(i), 0)),
                   _const_spec(convb), _const_spec(alog), _const_spec(dtb), _const_spec(dsk), _const_spec(nw)],
        out_shape=[jax.ShapeDtypeStruct((t, 1024), F32), jax.ShapeDtypeStruct((t, GW), BF16),
                   jax.ShapeDtypeStruct((t, LANE), BF16), jax.ShapeDtypeStruct(convb.shape, F32),
                   jax.ShapeDtypeStruct(alog.shape, F32), jax.ShapeDtypeStruct(dtb.shape, F32),
                   jax.ShapeDtypeStruct(dsk.shape, F32), jax.ShapeDtypeStruct(nw.shape, F32)],
        scratch_shapes=[pltpu.VMEM((LANE, GW), F32)],
        compiler_params=_cparams("arbitrary"),
    )(dy, cx, proj, proj, ssave, convb, alog, dtb, dsk, nw)


def _swa_specs(nb, rev):
    w = ATT_W
    cur = lambda i: (nb - 1 - i) if rev else i
    prv = lambda i: jnp.maximum(cur(i) - 1, 0)
    return [pl.BlockSpec((w, 1024), lambda i: (cur(i), O_AQ // 1024)),
            pl.BlockSpec((w, 256), lambda i: (cur(i), O_AK // 256)),
            pl.BlockSpec((w, 256), lambda i: (prv(i), O_AK // 256)),
            pl.BlockSpec((w, 256), lambda i: (cur(i), O_AV // 256)),
            pl.BlockSpec((w, 256), lambda i: (prv(i), O_AV // 256)),
            pl.BlockSpec((w, LANE), lambda i: (cur(i), 0)), pl.BlockSpec((w, LANE), lambda i: (prv(i), 0)),
            pl.BlockSpec((w, LANE), lambda i: (cur(i), 0)), pl.BlockSpec((w, LANE), lambda i: (prv(i), 0))]


def swa_fwd(name, proj, cos, sin, sinks):
    t = proj.shape[0]
    nb = t // ATT_W

    def body(q_ref, kc_ref, kp_ref, vc_ref, vp_ref, cc_ref, cp_ref, sc_ref, sp_ref, snk_ref, y_ref):
        notfirst = jnp.where(pl.program_id(0) > 0, 1.0, 0.0).astype(F32)
        cc, cp, sc, sp = cc_ref[...], cp_ref[...], sc_ref[...], sp_ref[...]
        outs = []
        for g in range(2):
            sl = slice(g * LANE, (g + 1) * LANE)
            qg = jnp.concatenate([_rope(q_ref[:, (4 * g + r) * LANE:(4 * g + r + 1) * LANE], cc, sc)
                                  for r in range(4)], axis=1)
            outs.append(_swa_block(qg, _rope(kp_ref[:, sl], cp, sp), _rope(kc_ref[:, sl], cc, sc), vp_ref[:, sl],
                                   vc_ref[:, sl], snk_ref[...], notfirst, g=g))
        y_ref[...] = jnp.concatenate(outs, axis=1)

    return pl.pallas_call(
        body, name=name, grid=(nb,),
        in_specs=_swa_specs(nb, False) + [_const_spec(sinks)],
        out_specs=pl.BlockSpec((ATT_W, 1024), lambda i: (i, 0)),
        out_shape=jax.ShapeDtypeStruct((t, 1024), F32),
        compiler_params=_cparams("arbitrary"),
    )(proj, proj, proj, proj, proj, cos, cos, sin, sin, sinks)


def swa_bwd(name, dy, dy_off, proj, cos, sin, sinks):
    t = proj.shape[0]
    nb = t // ATT_W

    def body(dy_ref, q_ref, kc_ref, kp_ref, vc_ref, vp_ref, cc_ref, cp_ref, sc_ref, sp_ref, snk_ref,
             dq_ref, dk_ref, dv_ref, dsnk_ref, ck_scr, cv_scr):
        i = pl.program_id(0)

        @pl.when(i == 0)
        def _():
            ck_scr[...] = jnp.zeros_like(ck_scr)
            cv_scr[...] = jnp.zeros_like(cv_scr)
            dsnk_ref[...] = jnp.zeros_like(dsnk_ref)

        notfirst = jnp.where(i < nb - 1, 1.0, 0.0).astype(F32)
        cc, cp, sc, sp = cc_ref[...], cp_ref[...], sc_ref[...], sp_ref[...]
        ck_in, cv_in = ck_scr[...], cv_scr[...]
        dq, dk, dv, ck, cv, dsnk = [], [], [], [], [], jnp.zeros(dsnk_ref.shape, F32)
        for g in range(2):
            sl = slice(g * LANE, (g + 1) * LANE)
            qg = jnp.concatenate([_rope(q_ref[:, (4 * g + r) * LANE:(4 * g + r + 1) * LANE], cc, sc)
                                  for r in range(4)], axis=1)
            _, vjp = jax.vjp(functools.partial(_swa_block, g=g), qg, _rope(kp_ref[:, sl], cp, sp),
                             _rope(kc_ref[:, sl], cc, sc), vp_ref[:, sl], vc_ref[:, sl], snk_ref[...], notfirst)
            dqg, dkp, dkc, dvp, dvc, dsnk_g, _ = vjp(dy_ref[:, g * GW:(g + 1) * GW])
            dq += [_unrope(dqg[:, r * LANE:(r + 1) * LANE], cc, sc) for r in range(4)]
            dk.append(_unrope(dkc, cc, sc) + ck_in[:, sl])
            dv.append(dvc + cv_in[:, sl])
            ck.append(_unrope(dkp, cp, sp))
            cv.append(dvp)
            dsnk = dsnk + dsnk_g
        dq_ref[...] = jnp.concatenate(dq, axis=1).astype(dq_ref.dtype)
        dk_ref[...] = jnp.concatenate(dk, axis=1).astype(dk_ref.dtype)
        dv_ref[...] = jnp.concatenate(dv, axis=1).astype(dv_ref.dtype)
        ck_scr[...] = jnp.concatenate(ck, axis=1)
        cv_scr[...] = jnp.concatenate(cv, axis=1)
        dsnk_ref[...] += dsnk

    rev = lambda i: nb - 1 - i
    return pl.pallas_call(
        body, name=name, grid=(nb,),
        in_specs=[pl.BlockSpec((ATT_W, 1024), lambda i: (rev(i), dy_off // 1024))] + _swa_specs(nb, True)
        + [_const_spec(sinks)],
        out_specs=[pl.BlockSpec((ATT_W, 1024), lambda i: (rev(i), 0)),
                   pl.BlockSpec((ATT_W, 256), lambda i: (rev(i), 0)),
                   pl.BlockSpec((ATT_W, 256), lambda i: (rev(i), 0)), _const_spec(sinks)],
        out_shape=[jax.ShapeDtypeStruct((t, 1024), BF16), jax.ShapeDtypeStruct((t, 256), BF16),
                   jax.ShapeDtypeStruct((t, 256), BF16), jax.ShapeDtypeStruct(sinks.shape, F32)],
        scratch_shapes=[pltpu.VMEM((ATT_W, 256), F32), pltpu.VMEM((ATT_W, 256), F32)],
        compiler_params=_cparams("arbitrary"),
    )(dy, proj, proj, proj, proj, proj, cos, cos, sin, sin, sinks)


def _me():
    return lax.axis_index("x"), lax.axis_index("y"), lax.axis_index("c")


ANY = pl.BlockSpec(memory_space=pl.ANY)


def allgather8(name, buf):
    r = buf.shape[0]

    def body(x_ref, out_ref, send_sems, recv_sems, local_sem):
        x, y, c = _me()
        me = 4 * x + 2 * y + c
        mine = pltpu.make_async_copy(x_ref, out_ref.at[me], local_sem)
        mine.start()
        copies = []
        for k in range(1, 8):
            bx, by, bc = (k >> 2) & 1, (k >> 1) & 1, k & 1
            to = ((1 - x) if bx else x, (1 - y) if by else y, (1 - c) if bc else c)
            cp = pltpu.make_async_remote_copy(src_ref=x_ref, dst_ref=out_ref.at[me], send_sem=send_sems.at[k - 1],
                                              recv_sem=recv_sems.at[k - 1], device_id=to, device_id_type=MESH)
            cp.start()
            copies.append(cp)
        for cp in copies:
            cp.wait_recv()
        for cp in copies:
            cp.wait_send()
        mine.wait()

    return pl.pallas_call(
        body, name=name, out_shape=jax.ShapeDtypeStruct((8, r, LANE), buf.dtype),
        in_specs=[pl.BlockSpec(memory_space=pltpu.VMEM)], out_specs=pl.BlockSpec(memory_space=pltpu.VMEM),
        scratch_shapes=[pltpu.SemaphoreType.DMA((7,)), pltpu.SemaphoreType.DMA((7,)), pltpu.SemaphoreType.DMA],
        compiler_params=pltpu.CompilerParams(vmem_limit_bytes=VMEM_LIMIT),
    )(buf)


def _sems(n):
    return [pltpu.SemaphoreType.DMA((n,)), pltpu.SemaphoreType.DMA((n,))]


def allgather_chips(name, shards):
    n = len(shards)

    def body(*refs):
        x_refs, out_refs, send_sems, recv_sems = refs[:n], refs[n:2 * n], refs[2 * n], refs[2 * n + 1]
        x, y, c = _me()
        chip = 2 * x + y
        chips = [(1 - x, y), (x, 1 - y), (1 - x, 1 - y)]

        def rc(src, dst, k, to):
            return pltpu.make_async_remote_copy(src_ref=src, dst_ref=dst, send_sem=send_sems.at[k],
                                                recv_sem=recv_sems.at[k], device_id=to, device_id_type=MESH)

        halves = [pl.ds(pl.multiple_of(c * (s.shape[0] // 2), 16), s.shape[0] // 2) for s in shards]
        own, first = [], {}
        for i in range(n):
            cp = rc(x_refs[i], out_refs[i].at[chip], 7 * i + 6, (x, y, 1 - c))
            cp.start()
            own.append(cp)
            for k, (cx, cy) in enumerate(chips):
                cp = rc(x_refs[i].at[halves[i]], out_refs[i].at[chip, halves[i]], 7 * i + k, (cx, cy, c))
                cp.start()
                first[i, k] = cp
        passed = []
        for i in range(n):
            for k, (cx, cy) in enumerate(chips):
                first[i, k].wait_recv()
                src = out_refs[i].at[2 * cx + cy, halves[i]]
                cp = rc(src, src, 7 * i + 3 + k, (x, y, 1 - c))
                cp.start()
                passed.append(cp)
        for cp in passed + own:
            cp.wait_recv()
        for cp in list(first.values()) + passed + own:
            cp.wait_send()

    for s in shards:
        assert s.shape[0] % 32 == 0
    return pl.pallas_call(
        body, name=name, out_shape=[jax.ShapeDtypeStruct((4,) + s.shape, s.dtype) for s in shards],
        in_specs=[ANY] * n, out_specs=[ANY] * n, scratch_shapes=_sems(7 * n),
    )(*shards)


def ag_forward(name, bufs):
    n = len(bufs)

    def body(*refs):
        out_refs, send_sems, recv_sems = refs[n:2 * n], refs[2 * n], refs[2 * n + 1]
        x, y, c = _me()
        cps = []
        for i in range(n):
            rh = bufs[i].shape[1] // 2
            half = pl.ds(pl.multiple_of(c * rh, 16), rh)
            for k, (cx, cy) in enumerate([(1 - x, y), (x, 1 - y), (1 - x, 1 - y)]):
                src = out_refs[i].at[2 * cx + cy, half]
                cps.append(pltpu.make_async_remote_copy(src_ref=src, dst_ref=src, send_sem=send_sems.at[3 * i + k],
                                                        recv_sem=recv_sems.at[3 * i + k], device_id=(x, y, 1 - c),
                                                        device_id_type=MESH))
        for cp in cps:
            cp.start()
        for cp in cps:
            cp.wait_recv()
        for cp in cps:
            cp.wait_send()

    return pl.pallas_call(
        body, name=name, out_shape=[jax.ShapeDtypeStruct(b.shape, b.dtype) for b in bufs],
        in_specs=[ANY] * n, out_specs=[ANY] * n, scratch_shapes=_sems(3 * n),
        input_output_aliases={i: i for i in range(n)},
    )(*bufs)


def sibling_swap_halves(name, gs):
    n = len(gs)

    def body(*refs):
        g_refs, out_refs, send_sems, recv_sems = refs[:n], refs[n:2 * n], refs[2 * n], refs[2 * n + 1]
        x, y, c = _me()
        cps = [pltpu.make_async_remote_copy(src_ref=g_refs[i].at[:, 1 - c], dst_ref=out_refs[i],
                                            send_sem=send_sems.at[i], recv_sem=recv_sems.at[i],
                                            device_id=(x, y, 1 - c), device_id_type=MESH) for i in range(n)]
        for cp in cps:
            cp.start()
        for cp in cps:
            cp.wait_recv()
        for cp in cps:
            cp.wait_send()

    return pl.pallas_call(
        body, name=name, out_shape=[jax.ShapeDtypeStruct((4,) + g.shape[2:], g.dtype) for g in gs],
        in_specs=[ANY] * n, out_specs=[ANY] * n, scratch_shapes=_sems(n),
    )(*gs)


def chips_exchange(name, ps):
    n = len(ps)

    def body(*refs):
        p_refs, out_refs, send_sems, recv_sems = refs[:n], refs[n:2 * n], refs[2 * n], refs[2 * n + 1]
        x, y, c = _me()
        chip = 2 * x + y
        cps = []
        for i in range(n):
            for k, (cx, cy) in enumerate([(1 - x, y), (x, 1 - y), (1 - x, 1 - y)]):
                cps.append(pltpu.make_async_remote_copy(
                    src_ref=p_refs[i].at[2 * cx + cy], dst_ref=out_refs[i].at[chip], send_sem=send_sems.at[3 * i + k],
                    recv_sem=recv_sems.at[3 * i + k], device_id=(cx, cy, c), device_id_type=MESH))
        for cp in cps:
            cp.start()
        for cp in cps:
            cp.wait_recv()
        for cp in cps:
            cp.wait_send()

    return pl.pallas_call(
        body, name=name, out_shape=[jax.ShapeDtypeStruct(p.shape, p.dtype) for p in ps],
        in_specs=[ANY] * n, out_specs=[ANY] * n, scratch_shapes=_sems(3 * n),
    )(*ps)


def sibling_send(name, ss):
    n = len(ss)

    def body(*refs):
        s_refs, out_refs, send_sems, recv_sems = refs[:n], refs[n:2 * n], refs[2 * n], refs[2 * n + 1]
        x, y, c = _me()
        cps = [pltpu.make_async_remote_copy(src_ref=s_refs[i], dst_ref=out_refs[i], send_sem=send_sems.at[i],
                                            recv_sem=recv_sems.at[i], device_id=(x, y, 1 - c), device_id_type=MESH)
               for i in range(n)]
        for cp in cps:
            cp.start()
        for cp in cps:
            cp.wait_recv()
        for cp in cps:
            cp.wait_send()

    return pl.pallas_call(
        body, name=name, out_shape=[jax.ShapeDtypeStruct(s.shape, s.dtype) for s in ss],
        in_specs=[ANY] * n, out_specs=[ANY] * n, scratch_shapes=_sems(n),
    )(*ss)


def _rs_tile(rh, w):
    return _pick(rh, max(8, (1 << 19) // w // 8 * 8), 8)


def add_half(name, g, recv, idx):
    _, _, rh, w = g.shape
    tile = _rs_tile(rh, w)

    def body(s_ref, g_ref, r_ref, o_ref):
        o_ref[...] = (g_ref[...] + r_ref[...]).astype(o_ref.dtype)

    return pl.pallas_call(
        body, name=name,
        grid_spec=pltpu.PrefetchScalarGridSpec(
            num_scalar_prefetch=1, grid=(4, rh // tile),
            in_specs=[pl.BlockSpec((None, None, tile, w), lambda s, i, ix: (s, ix[1], i, 0)),
                      pl.BlockSpec((None, tile, w), lambda s, i, ix: (s, i, 0))],
            out_specs=pl.BlockSpec((None, tile, w), lambda s, i, ix: (s, i, 0))),
        out_shape=jax.ShapeDtypeStruct((4, rh, w), BF16),
        compiler_params=_cparams("parallel", "parallel"),
    )(idx, g, recv)


def rs_sum(name, g, recv1, recv2, idx):
    _, _, rh, w = g.shape
    tile = _rs_tile(rh, w)

    def body(s_ref, g_ref, r1_ref, a_ref, b_ref, c_ref, o_ref):
        o_ref[...] = (g_ref[...] + r1_ref[...]) + (a_ref[...].astype(F32) + b_ref[...].astype(F32)
                                                    + c_ref[...].astype(F32))

    def other(k):
        return pl.BlockSpec((None, tile, w), lambda i, ix: (jnp.bitwise_xor(ix[0], k), i, 0))

    return pl.pallas_call(
        body, name=name,
        grid_spec=pltpu.PrefetchScalarGridSpec(
            num_scalar_prefetch=1, grid=(rh // tile,),
            in_specs=[pl.BlockSpec((None, None, tile, w), lambda i, ix: (ix[0], ix[1], i, 0)),
                      pl.BlockSpec((None, tile, w), lambda i, ix: (ix[0], i, 0)), other(1), other(2), other(3)],
            out_specs=pl.BlockSpec((tile, w), lambda i, ix: (i, 0))),
        out_shape=jax.ShapeDtypeStruct((rh, w), F32),
        compiler_params=_cparams("parallel"),
    )(idx, g, recv1, recv2, recv2, recv2)


def sum_slabs(name, a):
    n, r, _ = a.shape
    tile = _pick(r, 2048, 8)

    def body(a_ref, o_ref):
        acc = a_ref[0]
        for s in range(1, n):
            acc = acc + a_ref[s]
        o_ref[...] = acc

    return pl.pallas_call(
        body, name=name, grid=(r // tile,),
        in_specs=[pl.BlockSpec((n, tile, LANE), lambda i: (0, i, 0))],
        out_specs=pl.BlockSpec((tile, LANE), lambda i: (i, 0)),
        out_shape=jax.ShapeDtypeStruct((r, LANE), F32),
        compiler_params=_cparams("parallel"),
    )(a)


def f_premod(rv, cv):
    (x,), (w, scale, shift) = rv, cv
    y = x * lax.rsqrt(jnp.mean(x * x, -1, keepdims=True) + EPS) * w
    return [y * (1.0 + scale) + shift]


def f_postres(rv, cv):
    (x, y), (w, gate) = rv, cv
    return [x + gate * (y * lax.rsqrt(jnp.mean(y * y, -1, keepdims=True) + EPS) * w)]


def f_adamw(rv, cv):
    w, g, m, v = rv
    m = B1 * m + (1.0 - B1) * g
    v = B2 * v + (1.0 - B2) * jnp.square(g)
    m_hat = m / (1.0 - B1 ** STEP)
    v_hat = v / (1.0 - B2 ** STEP)
    return [-LR * (m_hat / (jnp.sqrt(v_hat) + AEPS) + WD * w), m, v], []


def adamw(name, w, g, m, v):
    r, wd = w.shape
    tile = r
    while tile * wd * 4 * 14 > 24 * 1024 * 1024 and tile % 16 == 0:
        tile //= 2
    (d, m2, v2), _ = rowmap(name, f_adamw, [(a, 0, wd) for a in (w, g, m, v)], [], [(wd, F32)] * 3, tile=tile)
    return d, m2, v2


_IN_SIZES = (512, 512, 512, 512, 4, 4, 512, 1024, 8, 512, 128, 128, 512, 512, 512)


def _relayout_w_in(w):
    d = w.shape[0]
    s = np.cumsum((0,) + _IN_SIZES)
    gq, gk, gv, gz, gb, ga, sz, sx, sdt, aq, ak, av, cb, cc, ch = [w[:, s[i]:s[i + 1]] for i in range(15)]

    def rope_pad(t, nh):
        return jnp.pad(t.reshape(d, nh, 2, 32), ((0, 0), (0, 0), (0, 0), (0, 32))).reshape(d, nh * LANE)

    def v_pad(t, nh):
        return jnp.pad(t.reshape(d, nh, 64), ((0, 0), (0, 0), (0, 64))).reshape(d, nh * LANE)

    def pad128(t):
        return jnp.pad(t, ((0, 0), (0, LANE - t.shape[1])))

    return jnp.concatenate([gq, gk, gv, cb, cc, ch, sx, rope_pad(aq, 8), gz, sz, rope_pad(ak, 2), v_pad(av, 2),
                            pad128(jnp.concatenate([gb, ga], axis=1)), pad128(sdt)], axis=1)


def _unlayout_g_in(g):
    d = g.shape[0]

    def rope_unpad(t, nh):
        return t.reshape(d, nh, 2, 64)[..., :32].reshape(d, nh * 64)

    def v_unpad(t, nh):
        return t.reshape(d, nh, LANE)[..., :64].reshape(d, nh * 64)

    return jnp.concatenate([g[:, 0:1536], g[:, O_GZ:O_GZ + 512], g[:, O_GBA:O_GBA + 8], g[:, O_SZ:O_SZ + 512],
                            g[:, O_XBC:O_XBC + 1024], g[:, O_SDT:O_SDT + 8], rope_unpad(g[:, O_AQ:O_AQ + 1024], 8),
                            rope_unpad(g[:, O_AK:O_AK + 256], 2), v_unpad(g[:, O_AV:O_AV + 256], 2),
                            g[:, O_CB:O_CB + 1536]], axis=1)


def _relayout_w_out(w):
    d = w.shape[1]
    yc = jnp.pad(w[1024:1536].reshape(8, 64, d), ((0, 0), (0, 64), (0, 0))).reshape(1024, d)
    return jnp.concatenate([w[:1024], yc, w[1536:]], axis=0)


def _unlayout_g_out(g):
    d = g.shape[1]
    return jnp.concatenate([g[:1024], g[1024:2048].reshape(8, LANE, d)[:, :64].reshape(512, d), g[2048:]], axis=0)


def _pack(arrs):
    rows, spec, off = [], [], 0
    for a in arrs:
        n = int(np.prod(a.shape))
        nr = -(-n // LANE)
        rows.append(jnp.pad(a.reshape(-1).astype(F32), (0, nr * LANE - n)).reshape(nr, LANE))
        spec.append((off, a.shape))
        off += nr
    pad = (-off) % 8
    if pad:
        rows.append(jnp.zeros((pad, LANE), F32))
    return jnp.concatenate(rows, axis=0), spec


def _unpack(buf, spec):
    out = []
    for off, shape in spec:
        n = int(np.prod(shape))
        nr = -(-n // LANE)
        out.append(buf[off:off + nr].reshape(-1)[:n].reshape(shape))
    return out


def _row128(v):
    return jnp.pad(v.astype(F32), (0, LANE - v.shape[0])).reshape(1, LANE)


BIG = ("w_in", "w_out", "w_up", "w_down")
SMALL_REPL = ("ada_b", "norm_pre_mix", "norm_post_mix", "norm_pre_mlp", "norm_post_mlp", "gdn_a_log", "gdn_dt_bias",
              "gdn_norm_w", "ssm_conv_b", "ssm_a_log", "ssm_dt_bias", "ssm_d", "ssm_norm_w", "attn_sinks")
SMALL_SHARD = ("gdn_conv_w", "ssm_conv_w", "sc_conv_w")
WEIGHTS = ("ada_w", "ada_b", "norm_pre_mix", "norm_post_mix", "norm_pre_mlp", "norm_post_mlp", "w_in", "w_out",
           "gdn_conv_w", "gdn_a_log", "gdn_dt_bias", "gdn_norm_w", "ssm_conv_w", "ssm_conv_b", "ssm_a_log",
           "ssm_dt_bias", "ssm_d", "ssm_norm_w", "attn_sinks", "sc_conv_w", "w_up", "w_down")


def _step(p, m, v, x, c, positions, loss_target):
    d = x.shape[2]
    xi, yi, ci = _me()
    chip = 2 * xi + yi
    me = 4 * xi + 2 * yi + ci
    x0 = x[0]
    tgt = loss_target[0]

    buf, spec = _pack([c[0], p["gdn_conv_w"], p["ssm_conv_w"], p["sc_conv_w"]])
    got = allgather8("ag_small_in", buf)
    per_dev = [_unpack(got[k], spec) for k in range(8)]
    c_all = jnp.stack([pd[0] for pd in per_dev], axis=0)
    conv_full = [jnp.concatenate([per_dev[4 * a + 2 * b][1 + j] for a in range(2) for b in range(2)], axis=-1)
                 for j in range(3)]
    gdn_conv_w, ssm_conv_w, sc_conv_w = conv_full

    c_pad = jnp.pad(c_all, ((0, 8), (0, 0)))
    silu_bf = lambda a: _silu(a).astype(BF16)
    nsh = p["ada_w"].shape[2]
    mods = []
    for l in range(DEPTH):
        mo = mm_nn(f"ada_fwd{l}", c_pad, p["ada_w"][l], F32, a_fn=silu_bf)
        mods.append(mo[:8] + lax.dynamic_slice(p["ada_b"][l], (chip * nsh,), (nsh,))[None])
    buf, spec = _pack([jnp.stack(mods, axis=0)])
    got = allgather8("ag_mod", buf)
    mod_sh = [_unpack(got[4 * a + 2 * b], spec)[0] for a in range(2) for b in range(2)]
    mod_all = jnp.concatenate(mod_sh, axis=-1)
    mod = lax.dynamic_index_in_dim(mod_all, me, axis=1, keepdims=False)

    items = [(l, n) for l in range(DEPTH) for n in BIG]
    shards = {(l, n): p[n][l].astype(BF16) for (l, n) in items}
    gathered = {(0, "w_in"): allgather_chips("ag_w_in0", [shards[0, "w_in"]])[0]}

    def gather_behind(keys):
        return ("gather", [shards[kk] for kk in keys])

    def settle(name, keys, landed):
        gathered.update(zip(keys, ag_forward(name, landed)))

    wfull = []

    inv_freq = ROPE_THETA ** (-jnp.arange(0, 64, 2, dtype=F32) / 64)
    ang = positions[0].astype(F32)[:, None] * inv_freq
    z32 = jnp.zeros_like(ang)
    cos_t = jnp.concatenate([jnp.cos(ang), z32, jnp.cos(ang), z32], axis=1)
    sin_t = jnp.concatenate([-jnp.sin(ang), z32, jnp.sin(ang), z32], axis=1)

    relu2 = lambda a: jnp.square(jnp.maximum(a, 0))

    def layer_consts(l):
        row = lambda v_: v_.reshape(1, -1).astype(F32)
        return dict(
            alog=_row128(p["gdn_a_log"][l]), gdtb=_row128(p["gdn_dt_bias"][l]), gnw=row(p["gdn_norm_w"][l]),
            scb=row(p["ssm_conv_b"][l]), salog=_row128(p["ssm_a_log"][l]), sdtb=_row128(p["ssm_dt_bias"][l]),
            sdsk=_row128(p["ssm_d"][l]), snw=row(p["ssm_norm_w"][l]), sinks=_row128(p["attn_sinks"][l]))

    saved = []
    xc = x0
    for l in range(DEPTH):
        w = dict(w_in=_relayout_w_in(jnp.concatenate([gathered[l, "w_in"][s] for s in range(4)], axis=1)))
        wfull.append(w)
        k = layer_consts(l)
        md = [mod[l, j * d:(j + 1) * d].reshape(1, d) for j in range(6)]
        npre, npost = p["norm_pre_mix"][l].reshape(1, d), p["norm_post_mix"][l].reshape(1, d)
        npre2, npost2 = p["norm_pre_mlp"][l].reshape(1, d), p["norm_post_mlp"][l].reshape(1, d)
        (h,), _ = rowmap(f"premix{l}", lambda rv, cv: (f_premod(rv, cv), []), [(xc, 0, d)], [npre, md[1], md[0]],
                         [(d, BF16)])
        behind = [(0, "w_out"), (0, "w_up")] if l == 0 else [(l, "w_up")]
        proj, landed = mm_nn(f"inproj{l}", h, w["w_in"], F32, carry=gather_behind(behind))
        settle(f"ag_forward_a{l}", behind, landed)
        w["w_out"] = _relayout_w_out(gathered[l, "w_out"].reshape(-1, d))
        w["w_up"] = gathered[l, "w_up"]
        cq = conv_fwd(f"gdn_conv{l}", [(proj, O_GQKV, 1536)], gdn_conv_w[l])
        ya, gs, gt = gdn_fwd(f"gdn{l}", cq, proj, k["alog"], k["gdtb"], k["gnw"])
        cx = conv_fwd(f"ssm_conv{l}", [(proj, O_XBC, 1024)], ssm_conv_w[l])
        yb, ss = ssd_fwd(f"ssd{l}", cx, proj, k["scb"], k["salog"], k["sdtb"], k["sdsk"], k["snw"])
        yc = swa_fwd(f"swa{l}", proj, cos_t, sin_t, k["sinks"])
        cd = conv_fwd(f"sc_conv{l}", [(proj, O_CC, GW), (proj, O_CH, GW)], sc_conv_w[l])
        (ycat,), _ = rowmap(f"ycat{l}", lambda rv, cv: ([jnp.concatenate([rv[0], rv[1], rv[2], rv[3] * rv[4]], 1)], []),
                            [(ya, 0, GW), (yb, 0, GW), (yc, 0, 1024), (proj, O_CB, GW), (cd, 0, GW)], [],
                            [(YCAT, BF16)])
        y = mm_nn(f"outproj{l}", ycat, w["w_out"], F32)
        (x1,), _ = rowmap(f"postmix{l}", lambda rv, cv: (f_postres(rv, cv), []), [(xc, 0, d), (y, 0, d)],
                          [npost, md[2]], [(d, F32)])
        (h2,), _ = rowmap(f"premlp{l}", lambda rv, cv: (f_premod(rv, cv), []), [(x1, 0, d)], [npre2, md[4], md[3]],
                          [(d, BF16)])
        u, landed = mm_nn(f"up{l}", h2, w["w_up"], BF16, carry=gather_behind([(l, "w_down")]))
        settle(f"ag_forward_b{l}", [(l, "w_down")], landed)
        w["w_down"] = gathered[l, "w_down"].reshape(-1, d)
        if l + 1 < DEPTH:
            behind = [(l + 1, "w_in"), (l + 1, "w_out")]
            y2, landed = mm_nn(f"down{l}", u, w["w_down"], F32, a_fn=relu2, carry=gather_behind(behind))
            settle(f"ag_forward_c{l}", behind, landed)
        else:
            y2 = mm_nn(f"down{l}", u, w["w_down"], F32, a_fn=relu2)
        (x2,), _ = rowmap(f"postmlp{l}", lambda rv, cv: (f_postres(rv, cv), []), [(x1, 0, d), (y2, 0, d)],
                          [npost2, md[5]], [(d, F32)])
        saved.append(dict(x=xc, h=h, proj=proj, cq=cq, gs=gs, gt=gt, cx=cx, ss=ss, cd=cd, ycat=ycat, y=y, x1=x1, h2=h2, u=u,
                          y2=y2, md=md, k=k))
        xc = x2

    def f_loss(rv, cv):
        err = rv[0] - rv[1]
        part = 0.5 * jnp.sum(jnp.mean(err * err, axis=-1, keepdims=True), axis=0, keepdims=True)
        return [err * (1.0 / d)], [jnp.broadcast_to(part, (1, LANE))]

    (dx,), (loss_part,) = rowmap("loss", f_loss, [(xc, 0, d), (tgt, 0, d)], [], [(d, F32)], [(1, LANE)])

    gsmall = [None] * DEPTH
    dmods = [None] * DEPTH
    idx = jnp.stack([chip, ci]).astype(jnp.int32)
    g4s, recv1s, recv2s, waiting = {}, {}, {}, []

    def reduce_start(key, g):
        l_, n_ = key
        g4 = g.reshape(4, 2, g.shape[1] // 2, g.shape[2])
        r1 = sibling_swap_halves(f"rs_sibling_{n_}{l_}", [g4])[0]
        g4s[key], recv1s[key] = g4, r1
        waiting.append((key, add_half(f"rs_add_{n_}{l_}", g4, r1, idx)))

    def mm_carrying(name, *args, **kw):
        if not waiting:
            return mm_nn(name, *args, **kw)
        out, landed = mm_nn(name, *args, carry=("exchange", [pt for _, pt in waiting]), **kw)
        recv2s.update(zip([kk for kk, _ in waiting], landed))
        waiting.clear()
        return out

    for l in reversed(range(DEPTH)):
        w, s = wfull[l], saved[l]
        k, md = s["k"], s["md"]
        npre, npost = p["norm_pre_mix"][l].reshape(1, d), p["norm_post_mix"][l].reshape(1, d)
        npre2, npost2 = p["norm_pre_mlp"][l].reshape(1, d), p["norm_post_mlp"][l].reshape(1, d)
        (dx1a, dy2), (dnpost2, dgate_m) = rowmap_vjp(f"postmlp_b{l}", f_postres, [(s["x1"], 0, d), (s["y2"], 0, d)],
                                                      [npost2, md[5]], [(dx, 0, d)], 2, [F32, BF16])
        du = mm_carrying(f"down_b{l}", dy2, w["w_down"], BF16, extra=s["u"], wt=True,
                         epi=lambda r, uu: r * (2.0 * jnp.maximum(uu.astype(F32), 0.0)))
        reduce_start((l, "w_down"), mm_tn(f"down_g{l}", s["u"], dy2, a_fn=relu2).reshape(4, -1, d))
        dh2 = mm_carrying(f"up_b{l}", du, w["w_up"], F32, wt=True)
        reduce_start((l, "w_up"), mm_tn(f"up_g{l}", s["h2"], du, col_slabs=4))
        (dx1,), (dnpre2, dscale_m, dshift_m) = rowmap_vjp(f"premlp_b{l}", f_premod, [(s["x1"], 0, d)],
                                                          [npre2, md[4], md[3]], [(dh2, 0, d)], 1, [F32],
                                                          add_rows=[(dx1a, 0, d)])
        (dxa, dy), (dnpost, dgate_a) = rowmap_vjp(f"postmix_b{l}", f_postres, [(s["x"], 0, d), (s["y"], 0, d)],
                                                  [npost, md[2]], [(dx1, 0, d)], 2, [F32, BF16])
        dycat = mm_nn(f"outproj_b{l}", dy, w["w_out"], F32, wt=True)
        reduce_start((l, "w_out"), _unlayout_g_out(mm_tn(f"outproj_g{l}", s["ycat"], dy)).reshape(4, -1, d))
        (dcb, dcd), _ = rowmap(f"scgate_b{l}", lambda rv, cv: ([rv[0] * rv[2], rv[0] * rv[1]], []),
                               [(dycat, 2048, GW), (s["proj"], O_CB, GW), (s["cd"], 0, GW)], [],
                               [(GW, BF16), (GW, F32)])
        (dcc, dch), g_scw = conv_bwd(f"sc_conv_b{l}", dcd, [(s["proj"], O_CC, GW), (s["proj"], O_CH, GW)],
                                     sc_conv_w[l], BF16)
        daq, dak, dav, g_sinks = swa_bwd(f"swa_b{l}", dycat, 1024, s["proj"], cos_t, sin_t, k["sinks"])
        dcx, dsz, dsdt, g_scb, g_salog, g_sdtb, g_sdsk, g_snw = ssd_bwd(
            f"ssd_b{l}", dycat, 512, s["cx"], s["proj"], s["ss"], k["scb"], k["salog"], k["sdtb"], k["sdsk"], k["snw"])
        (dxbc,), g_ssmw = conv_bwd(f"ssm_conv_b{l}", dcx, [(s["proj"], O_XBC, 1024)], ssm_conv_w[l], BF16)
        dcq, dgz, dgba, g_alog, g_gdtb, g_gnw = gdn_bwd(f"gdn_b{l}", dycat, 0, s["cq"], s["proj"], s["gs"], s["gt"],
                                                        k["alog"], k["gdtb"], k["gnw"])
        (dgqkv,), g_gdnw = conv_bwd(f"gdn_conv_b{l}", dcq, [(s["proj"], O_GQKV, 1536)], gdn_conv_w[l], BF16)
        dproj = jnp.concatenate([dgqkv, dcb, dcc, dch, dxbc, daq, dgz, dsz, dak, dav, dgba, dsdt], axis=1)
        dh = mm_carrying(f"inproj_b{l}", dproj, w["w_in"], F32, wt=True)
        g_in = mm_tn(f"inproj_g{l}", s["h"], dproj)
        reduce_start((l, "w_in"), jnp.stack(jnp.split(_unlayout_g_in(g_in), 4, axis=1), axis=0))
        (dx,), (dnpre, dscale_a, dshift_a) = rowmap_vjp(f"premix_b{l}", f_premod, [(s["x"], 0, d)],
                                                        [npre, md[1], md[0]], [(dh, 0, d)], 1, [F32],
                                                        add_rows=[(dxa, 0, d)])
        dmods[l] = jnp.concatenate([dshift_a, dscale_a, dgate_a, dshift_m, dscale_m, dgate_m], axis=1)[0]
        gsmall[l] = dict(
            norm_pre_mix=dnpre[0], norm_post_mix=dnpost[0], norm_pre_mlp=dnpre2[0], norm_post_mlp=dnpost2[0],
            gdn_a_log=g_alog[0, :4], gdn_dt_bias=g_gdtb[0, :4], gdn_norm_w=g_gnw[0], ssm_conv_b=g_scb[0],
            ssm_a_log=g_salog[0, :8], ssm_dt_bias=g_sdtb[0, :8], ssm_d=g_sdsk[0, :8], ssm_norm_w=g_snw[0],
            attn_sinks=g_sinks[0, :8], gdn_conv_w=g_gdnw, ssm_conv_w=g_ssmw, sc_conv_w=g_scw)
    grad_x = dx[None]

    buf, spec = _pack([jnp.stack(dmods, axis=0)])
    got = allgather8("ag_dmod", buf)
    dmod_sum = _unpack(sum_slabs("dmod_sum", got), spec)[0]
    dmod_all = jnp.stack([_unpack(got[kk], spec)[0] for kk in range(8)], axis=1)
    g_ada_w = []
    for l in range(DEPTH):
        dm = lax.dynamic_slice(dmod_all[l], (0, chip * nsh), (8, nsh))
        g_ada_w.append(mm_tn(f"ada_g{l}", c_pad, jnp.pad(dm, ((0, 8), (0, 0))), a_fn=silu_bf))
    g_ada_w = jnp.stack(g_ada_w, axis=0)

    names = [n for n in SMALL_REPL if n != "ada_b"] + list(SMALL_SHARD)
    buf, spec = _pack([jnp.stack([gsmall[l][n] for l in range(DEPTH)], axis=0) for n in names] + [loss_part])
    tot = _unpack(sum_slabs("small_sum", allgather8("ag_small_g", buf)), spec)
    gfull = dict(zip(names, tot[:-1]))
    loss = tot[-1][0, 0]
    grads = {"ada_w": g_ada_w, "ada_b": dmod_sum}
    for n in SMALL_REPL:
        if n != "ada_b":
            grads[n] = gfull[n]
    for n in SMALL_SHARD:
        wdt = p[n].shape[2]
        grads[n] = lax.dynamic_slice_in_dim(gfull[n], chip * wdt, wdt, axis=2)

    recv2s.update(zip([kk for kk, _ in waiting], chips_exchange("rs_chips_last", [pt for _, pt in waiting])))
    mine = [rs_sum(f"rs_sum_{n}{l}", g4s[l, n], recv1s[l, n], recv2s[l, n], idx) for (l, n) in items]
    other = sibling_send("rs_join", mine)
    first = ci == 0
    gsh = {it: jnp.concatenate([jnp.where(first, a, b), jnp.where(first, b, a)], axis=0)
           for it, a, b in zip(items, mine, other)}
    for n in BIG:
        grads[n] = jnp.stack([gsh[l, n] for l in range(DEPTH)], axis=0)

    delta, new_m, new_v = {}, {}, {}
    for n in ("ada_w",) + BIG:
        sh = p[n].shape
        two = lambda a: a.reshape(sh[0] * sh[1], sh[2])
        dl, m2, v2 = adamw(f"adamw_{n}", two(p[n]), two(grads[n]), two(m[n]), two(v[n]))
        delta[n], new_m[n], new_v[n] = dl.reshape(sh), m2.reshape(sh), v2.reshape(sh)
    small = list(SMALL_REPL) + list(SMALL_SHARD)
    bufs = []
    for src in (p, grads, m, v):
        b_, spec = _pack([src[n] for n in small])
        bufs.append(b_)
    dl, m2, v2 = adamw("adamw_small", *bufs)
    for n, a, b_, cc_ in zip(small, _unpack(dl, spec), _unpack(m2, spec), _unpack(v2, spec)):
        delta[n], new_m[n], new_v[n] = a, b_, cc_

    return (loss, grad_x, *[grads[n] for n in WEIGHTS], *[delta[n] for n in WEIGHTS], *[new_m[n] for n in WEIGHTS],
            *[new_v[n] for n in WEIGHTS])


def kernel(x, c, positions, ada_w, ada_b, norm_pre_mix, norm_post_mix, norm_pre_mlp, norm_post_mlp, w_in, w_out, gdn_conv_w, gdn_a_log, gdn_dt_bias, gdn_norm_w, ssm_conv_w, ssm_conv_b, ssm_a_log, ssm_dt_bias, ssm_d, ssm_norm_w, attn_sinks, sc_conv_w, w_up, w_down, loss_target, m_ada_w, m_ada_b, m_norm_pre_mix, m_norm_post_mix, m_norm_pre_mlp, m_norm_post_mlp, m_w_in, m_w_out, m_gdn_conv_w, m_gdn_a_log, m_gdn_dt_bias, m_gdn_norm_w, m_ssm_conv_w, m_ssm_conv_b, m_ssm_a_log, m_ssm_dt_bias, m_ssm_d, m_ssm_norm_w, m_attn_sinks, m_sc_conv_w, m_w_up, m_w_down, v_ada_w, v_ada_b, v_norm_pre_mix, v_norm_post_mix, v_norm_pre_mlp, v_norm_post_mlp, v_w_in, v_w_out, v_gdn_conv_w, v_gdn_a_log, v_gdn_dt_bias, v_gdn_norm_w, v_ssm_conv_w, v_ssm_conv_b, v_ssm_a_log, v_ssm_dt_bias, v_ssm_d, v_ssm_norm_w, v_attn_sinks, v_sc_conv_w, v_w_up, v_w_down):
    loc = locals()
    p = {n: loc[n] for n in WEIGHTS}
    m = {n: loc["m_" + n] for n in WEIGHTS}
    v = {n: loc["v_" + n] for n in WEIGHTS}
    return _step(p, m, v, x, c, positions, loss_target)
```

```python
import functools

import numpy as np
import jax
import jax.numpy as jnp
from jax import lax
from jax.experimental import pallas as pl
from jax.experimental.pallas import tpu as pltpu

F32, BF16 = jnp.float32, jnp.bfloat16
MESH = pl.DeviceIdType.MESH

DEPTH = 2
GW = 512
N_GDN, GDN_CS = 4, 64
SSD_CS, ATT_W = 128, 128
EPS = 1e-6
ROPE_THETA = 10000.0
IN_WIDTH = 5904
B1, B2, LR, AEPS, WD, STEP = 0.9, 0.999, 0.001, 1e-8, 0.01, 10

VMEM_LIMIT = 52 * 1024 * 1024
LANE = 128

O_GQKV, O_CB, O_CC, O_CH, O_XBC, O_AQ, O_GZ, O_SZ, O_AK, O_AV, O_GBA, O_SDT = (
    0, 1536, 2048, 2560, 3072, 4096, 5120, 5632, 6144, 6400, 6656, 6784)
PW = 6912
YCAT = 2560


def _cparams(*sem):
    return pltpu.CompilerParams(dimension_semantics=sem or None, vmem_limit_bytes=VMEM_LIMIT)


def _pick(n, cap, mult):
    if n <= cap:
        return n
    best = None
    for d in range(mult, cap + 1, mult):
        if n % d == 0:
            best = d
    assert best is not None, (n, cap, mult)
    return best


def _pieces(x, n):
    out, r = [], x
    for i in range(n):
        h = r.astype(BF16)
        out.append(h)
        if i < n - 1:
            r = r - h.astype(F32)
    return out


def _dgp(a, b, ca, cb, na, nb):
    dn = (((ca,), (cb,)), ((), ()))
    acc = None
    for i, pa in enumerate(_pieces(a, na)):
        for j, pb in enumerate(_pieces(b, nb)):
            if i + j < max(na, nb):
                t = lax.dot_general(pa, pb, dn, preferred_element_type=F32)
                acc = t if acc is None else acc + t
    return acc


_CONTRACT = {"nn": (1, 0), "nt": (1, 1), "tn": (0, 0)}


@functools.lru_cache(maxsize=None)
def _mm(kind, na, nb, const_a=False):
    ca, cb = _CONTRACT[kind]
    ng = max(na, nb)

    @jax.custom_vjp
    def f(a, b):
        return _dgp(a, b, ca, cb, na, nb)

    def bwd(r, g):
        a, b = r
        if kind == "nn":
            da, db = (lambda: _mm("nt", ng, nb)(g, b)), _mm("tn", na, ng, const_a)(a, g)
        elif kind == "nt":
            da, db = (lambda: _mm("nn", ng, nb)(g, b)), _mm("tn", ng, na)(g, a)
        else:
            da, db = (lambda: _mm("nt", nb, ng)(b, g)), _mm("nn", na, ng, const_a)(a, g)
        return (jnp.zeros_like(a) if const_a else da()), db

    f.defvjp(lambda a, b: (f(a, b), (a, b)), bwd)
    return f


_nn, _nt, _tn = _mm("nn", 1, 1), _mm("nt", 1, 1), _mm("tn", 1, 1)
_hnn, _hnt, _htn = _mm("nn", 2, 2), _mm("nt", 2, 2), _mm("tn", 2, 2)
_xnn = _mm("nn", 1, 3, True)


def _silu(x):
    return x * jax.nn.sigmoid(x)


def _softplus(x):
    return jnp.maximum(x, 0.0) + jnp.log1p(jnp.exp(-jnp.abs(x)))


def _colsel(v, idx):
    lane = lax.broadcasted_iota(jnp.int32, v.shape, 1)
    return jnp.sum(jnp.where(lane == idx, v, 0.0), axis=1, keepdims=True)


def _tri(n):
    r = lax.broadcasted_iota(jnp.int32, (n, n), 0)
    c = lax.broadcasted_iota(jnp.int32, (n, n), 1)
    return r >= c, r > c


def _rowsel(v, idx):
    row = lax.broadcasted_iota(jnp.int32, v.shape, 0)
    return jnp.sum(jnp.where(row == idx, v, 0.0), axis=0, keepdims=True)


def _tri_inv(m, size, nil):
    r = lax.broadcasted_iota(jnp.int32, (size, size), 0)
    c = lax.broadcasted_iota(jnp.int32, (size, size), 1)
    x = jnp.where(r == c, 1.0, 0.0) - m
    p = _hnn(m, m)
    steps = int(np.log2(nil)) - 1
    for s in range(steps):
        x = x + _hnn(x, p)
        if s < steps - 1:
            p = _hnn(p, p)
    return x


@jax.custom_vjp
def _inv_given(m, tinv):
    return tinv


_inv_given.defvjp(lambda m, tinv: (tinv, tinv),
                  lambda tinv, g: (-_hnt(_htn(tinv, g), tinv), jnp.zeros_like(tinv)))


def _gdn_chunk(cq, z, ba, s, alog, dtb, nw, tinv_saved=None):
    n, nh = GDN_CS, N_GDN
    rr = n * nh

    def stack(a, off):
        return jnp.concatenate([a[:, off + h * LANE:off + (h + 1) * LANE] for h in range(nh)], axis=0)

    def percol(f):
        return jnp.concatenate([f(h) for h in range(nh)], axis=0)

    def own(a):
        return jnp.concatenate([a[h * n:(h + 1) * n, h * LANE:(h + 1) * LANE] for h in range(nh)], axis=0)

    q, k, v = _silu(stack(cq, 0)), _silu(stack(cq, GW)), _silu(stack(cq, 2 * GW))
    q = q * lax.rsqrt(jnp.sum(q * q, -1, keepdims=True) + EPS) * (128 ** -0.5)
    k = k * lax.rsqrt(jnp.sum(k * k, -1, keepdims=True) + EPS)
    beta = jax.nn.sigmoid(percol(lambda h: _colsel(ba, h)))
    g = percol(lambda h: -jnp.exp(_colsel(alog, h)) * _softplus(_colsel(ba, nh + h) + _colsel(dtb, h)))
    ri = lax.broadcasted_iota(jnp.int32, (rr, rr), 0)
    ci = lax.broadcasted_iota(jnp.int32, (rr, rr), 1)
    same = (ri >> 6) == (ci >> 6)
    incl, strict = same & (ri >= ci), same & (ri > ci)
    gcb = _xnn(incl.astype(F32), jnp.broadcast_to(g, (rr, LANE)))
    gci = jnp.concatenate([gcb, gcb], axis=1)
    dec = jnp.where(incl, jnp.exp(jnp.where(incl, gci - gci.T, 0.0)), 0.0)
    kb = k * beta
    kk = _nt(jnp.concatenate([kb, q], axis=0), k)
    m = jnp.where(strict, kk[:rr] * dec, 0.0)
    attn = jnp.where(incl, kk[rr:] * dec, 0.0)
    tinv = _tri_inv(m, rr, n) if tinv_saved is None else _inv_given(m, tinv_saved)
    egc = jnp.exp(gcb)
    uw = _nn(tinv, jnp.concatenate([v * beta, kb * egc], axis=1))
    u, w = uw[:, :LANE], uw[:, LANE:]
    gl = [jnp.sum(g[h * n:(h + 1) * n], axis=0, keepdims=True) for h in range(nh)]
    kd = k * jnp.exp(percol(lambda h: jnp.broadcast_to(gl[h], (n, 1))) - gcb)
    ws = _nn(jnp.concatenate([w, q * egc], axis=0), s)
    v_new = u - own(ws[:rr])
    o = own(ws[rr:]) + _nn(attn, v_new)
    rowh = lax.broadcasted_iota(jnp.int32, (rr, LANE), 0) >> 6
    vexp = jnp.concatenate([jnp.where(rowh == h, v_new, 0.0) for h in range(nh)], axis=1)
    cd = jnp.concatenate([jnp.broadcast_to(jnp.exp(gl[h]), (1, LANE)) for h in range(nh)], axis=1)
    s_new = s * cd + _tn(kd, vexp)
    y = o * lax.rsqrt(jnp.mean(o * o, -1, keepdims=True) + EPS) * nw * _silu(stack(z, 0))
    y = jnp.concatenate([y[h * n:(h + 1) * n] for h in range(nh)], axis=1)
    return (y, s_new, tinv) if tinv_saved is None else (y, s_new)


def _ssd_chunk(cx, z, dtc, s, convb, alog, dtb, dsk, nw):
    n = SSD_CS
    incl, _ = _tri(n)
    xbc = _silu(cx + convb)
    dt_all = _softplus(dtc + dtb)
    da_all = dt_all * (-jnp.exp(alog))
    cs_all = _xnn(incl.astype(F32), da_all)
    cs_t = cs_all.T
    tot_all = jnp.sum(da_all, axis=0, keepdims=True)
    lane_lo = lax.broadcasted_iota(jnp.int32, (n, LANE), 1) < 64
    ys, s_out = [], []
    for pi in range(4):
        g = pi // 2
        bm = xbc[:, GW + g * LANE:GW + (g + 1) * LANE]
        cm = xbc[:, GW + 256 + g * LANE:GW + 256 + (g + 1) * LANE]
        cbm = _nt(cm, bm)
        xp, s_p = xbc[:, pi * LANE:(pi + 1) * LANE], s[:, pi * LANE:(pi + 1) * LANE]
        a_rows, b_cols, dts, dsks, etot = [], [], [], [], []
        for h in (2 * pi, 2 * pi + 1):
            cs_i = _colsel(cs_all, h)
            lm = jnp.where(incl, jnp.exp(jnp.where(incl, cs_i - _rowsel(cs_t, h), 0.0)), 0.0)
            tot = _colsel(tot_all, h)
            a_rows.append(jnp.concatenate([cbm * lm, cm * jnp.exp(cs_i)], axis=1))
            b_cols.append(bm * jnp.exp(tot - cs_i))
            dts.append(_colsel(dt_all, h))
            dsks.append(_colsel(dsk, h))
            etot.append(jnp.exp(tot))
        xdt = xp * jnp.where(lane_lo, dts[0], dts[1])
        yy = _nn(jnp.concatenate(a_rows, axis=0), jnp.concatenate([xdt, s_p], axis=0))
        ys.append(jnp.where(lane_lo, yy[:n], yy[n:]) + xp * jnp.where(lane_lo[:1], dsks[0], dsks[1]))
        st = _tn(jnp.concatenate(b_cols, axis=1), xdt)
        s_out.append(s_p * jnp.where(lane_lo[:1], etot[0], etot[1]) + jnp.where(lane_lo, st[:n], st[n:]))
    zs = _silu(z)
    out = []
    for g in range(2):
        y = jnp.concatenate(ys[2 * g:2 * g + 2], axis=1) * zs[:, g * 256:(g + 1) * 256]
        out.append(y * lax.rsqrt(jnp.mean(y * y, -1, keepdims=True) + EPS) * nw[:, g * 256:(g + 1) * 256])
    return jnp.concatenate(out, axis=1), jnp.concatenate(s_out, axis=1)


SWA_STACK = 1


def _swa_block(qg, kp, kc, vp, vc, sink, notfirst, *, g):
    w, ns = ATT_W, SWA_STACK
    kb = jnp.concatenate([kp, kc], axis=0)
    vb = jnp.concatenate([vp, vc], axis=0)
    qi = lax.broadcasted_iota(jnp.int32, (ns * w, 2 * w), 0) & (w - 1)
    kj = lax.broadcasted_iota(jnp.int32, (ns * w, 2 * w), 1)
    rel = qi + w - kj
    mask = (rel >= 0) & (rel < w) & ((kj >= w) | (notfirst > 0.5))
    outs = []
    for r0 in range(0, 4, ns):
        q = jnp.concatenate([qg[:, r * LANE:(r + 1) * LANE] for r in range(r0, r0 + ns)], axis=0)
        snk = jnp.concatenate([jnp.broadcast_to(_colsel(sink, 4 * g + r), (w, 1)) for r in range(r0, r0 + ns)], axis=0)
        s = jnp.where(mask, _nt(q, kb) * (64 ** -0.5), -1e30)
        mx = lax.stop_gradient(jnp.maximum(jnp.max(s, axis=-1, keepdims=True), snk))
        p = jnp.where(mask, jnp.exp(s - mx), 0.0)
        p = p / (jnp.sum(p, axis=-1, keepdims=True) + jnp.exp(snk - mx))
        o = _nn(p, vb)
        outs += [o[j * w:(j + 1) * w] for j in range(ns)]
    return jnp.concatenate(outs, axis=1)


def _rope(x, cos, sin):
    return x * cos + pltpu.roll(x, 64, 1) * sin


def _unrope(dy, cos, sin):
    return dy * cos + pltpu.roll(dy * sin, 64, 1)


def rowmap(name, fn, rows, consts, outs, accs=(), tile=256):
    t = rows[0][0].shape[0]
    tile = min(tile, t)
    assert t % tile == 0
    nr, nc, no, na = len(rows), len(consts), len(outs), len(accs)

    def body(*refs):
        rv = [r[...] for r in refs[:nr]]
        cv = [r[...] for r in refs[nr:nr + nc]]
        ov, av = fn(rv, cv)
        for ref, o in zip(refs[nr + nc:nr + nc + no], ov):
            ref[...] = o.astype(ref.dtype)
        if na:
            a_refs = refs[nr + nc + no:]

            @pl.when(pl.program_id(0) == 0)
            def _():
                for ref in a_refs:
                    ref[...] = jnp.zeros_like(ref)

            for ref, a in zip(a_refs, av):
                ref[...] += a

    def cspec(off, width):
        assert off % width == 0
        cb = off // width
        return pl.BlockSpec((tile, width), lambda i: (i, cb))

    in_specs = [cspec(off, wd) for (_, off, wd) in rows]
    in_specs += [pl.BlockSpec(c.shape, lambda i: (0, 0)) for c in consts]
    out_specs = [pl.BlockSpec((tile, wd), lambda i: (i, 0)) for (wd, _) in outs]
    out_specs += [pl.BlockSpec(s, lambda i: (0, 0)) for s in accs]
    out_shape = [jax.ShapeDtypeStruct((t, wd), dt) for (wd, dt) in outs]
    out_shape += [jax.ShapeDtypeStruct(s, F32) for s in accs]
    res = pl.pallas_call(
        body, name=name, grid=(t // tile,), in_specs=in_specs, out_specs=out_specs, out_shape=out_shape,
        compiler_params=_cparams("arbitrary"),
    )(*[r[0] for r in rows], *consts)
    return res[:no], res[no:]


def rowmap_vjp(name, f, rows, consts, douts, n_diff, drow_dtypes, tile=256, add_rows=()):
    nr, nd, nadd = len(rows), len(douts), len(add_rows)

    def g(vals, cvals):
        rv, dov, addv = vals[:nr], vals[nr:nr + nd], vals[nr + nd:]
        fixed = rv[n_diff:]
        _, vjp = jax.vjp(lambda r, c: f(list(r) + list(fixed), c), rv[:n_diff], cvals)
        dr, dc = vjp([d.astype(F32) for d in dov])
        dr = list(dr)
        for a in addv:
            dr[0] = dr[0] + a
        return dr, dc

    outs = [(rows[i][2], drow_dtypes[i]) for i in range(n_diff)]
    accs = [c.shape for c in consts]
    return rowmap(name, g, list(rows) + list(douts) + list(add_rows), consts, outs, accs, tile=tile)


def _carried_copies(kind, in_refs, out_refs, send_sems, recv_sems):
    x, y, c = _me()
    chip = 2 * x + y
    chips = [(1 - x, y), (x, 1 - y), (1 - x, 1 - y)]
    cps = []

    def rc(src, dst, k, to):
        return pltpu.make_async_remote_copy(src_ref=src, dst_ref=dst, send_sem=send_sems.at[k], recv_sem=recv_sems.at[k],
                                            device_id=to, device_id_type=MESH)

    for i, (src, dst) in enumerate(zip(in_refs, out_refs)):
        if kind == "gather":
            rh = src.shape[0] // 2
            half = pl.ds(pl.multiple_of(c * rh, 16), rh)
            cps.append(rc(src, dst.at[chip], 4 * i + 3, (x, y, 1 - c)))
            cps += [rc(src.at[half], dst.at[chip, half], 4 * i + k, (cx, cy, c)) for k, (cx, cy) in enumerate(chips)]
        else:
            cps += [rc(src.at[2 * cx + cy], dst.at[chip], 4 * i + k, (cx, cy, c)) for k, (cx, cy) in enumerate(chips)]
    return cps


def mm_nn(name, a, w, out_dtype, a_fn=None, extra=None, epi=None, tm_cap=1024, wt=False, carry=None):
    nc = len(carry[1]) if carry else 0
    m, k = a.shape
    stacked = w.ndim == 3
    n = (w.shape[-2] if wt else w.shape[-1] * (w.shape[0] if stacked else 1))
    slab = w.shape[-1]
    tm = _pick(m, tm_cap, 16)
    tn = _pick(n if wt else slab, 1024, LANE)
    tk = _pick(slab if wt else k, 2048, LANE)
    nk = k // tk
    per = slab // (tk if wt else tn)
    wdims = ((1,), (1,)) if wt else ((1,), (0,))

    gi, gj = m // tm, n // tn
    n_in = 2 + (extra is not None)

    def body(*refs):
        a_ref, w_ref = refs[0], refs[1]
        e_ref = refs[2] if extra is not None else None
        o_ref, acc = refs[n_in + nc], refs[n_in + 2 * nc + 1]
        kk = pl.program_id(2)
        if nc:
            def copies():
                return _carried_copies(carry[0], refs[n_in:n_in + nc], refs[n_in + nc + 1:n_in + 2 * nc + 1],
                                       refs[-2], refs[-1])

            @pl.when((pl.program_id(0) == 0) & (pl.program_id(1) == 0) & (kk == 0))
            def _():
                for cp in copies():
                    cp.start()

        @pl.when(kk == 0)
        def _():
            acc[...] = jnp.zeros_like(acc)

        av = a_ref[...]
        if a_fn is not None:
            av = a_fn(av)
        acc[...] += lax.dot_general(av.astype(BF16), w_ref[...].astype(BF16), (wdims, ((), ())),
                                    preferred_element_type=F32)

        @pl.when(kk == nk - 1)
        def _():
            r = acc[...]
            if epi is not None:
                r = epi(r, e_ref[...])
            o_ref[...] = r.astype(o_ref.dtype)

        if nc:
            @pl.when((pl.program_id(0) == gi - 1) & (pl.program_id(1) == gj - 1) & (kk == nk - 1))
            def _():
                cps = copies()
                for cp in cps:
                    cp.wait_recv()
                for cp in cps:
                    cp.wait_send()

    if wt:
        w_spec = (pl.BlockSpec((None, tn, tk), lambda i, j, kk: (kk // per, j, kk % per)) if stacked
                  else pl.BlockSpec((tn, tk), lambda i, j, kk: (j, kk)))
    else:
        w_spec = (pl.BlockSpec((None, tk, tn), lambda i, j, kk: (j // per, kk, j % per)) if stacked
                  else pl.BlockSpec((tk, tn), lambda i, j, kk: (kk, j)))
    in_specs = [pl.BlockSpec((tm, tk), lambda i, j, kk: (i, kk)), w_spec]
    ops = [a, w]
    if extra is not None:
        in_specs.append(pl.BlockSpec((tm, tn), lambda i, j, kk: (i, j)))
        ops.append(extra)
    out_spec = pl.BlockSpec((tm, tn), lambda i, j, kk: (i, j))
    out_shape = jax.ShapeDtypeStruct((m, n), out_dtype)
    if not nc:
        return pl.pallas_call(
            body, name=name, grid=(gi, gj, nk), in_specs=in_specs, out_specs=out_spec, out_shape=out_shape,
            scratch_shapes=[pltpu.VMEM((tm, tn), F32)],
            compiler_params=_cparams("parallel", "parallel", "arbitrary"),
        )(*ops)
    landing = [jax.ShapeDtypeStruct((4,) + c.shape if carry[0] == "gather" else c.shape, c.dtype) for c in carry[1]]
    res = pl.pallas_call(
        body, name=name, grid=(gi, gj, nk), in_specs=in_specs + [ANY] * nc, out_specs=[out_spec] + [ANY] * nc,
        out_shape=[out_shape] + landing, scratch_shapes=[pltpu.VMEM((tm, tn), F32)] + _sems(4 * nc),
        compiler_params=_cparams("arbitrary", "arbitrary", "arbitrary"),
    )(*ops, *carry[1])
    return res[0], list(res[1:])


def mm_tn(name, a, b, a_fn=None, col_slabs=1):
    t, m = a.shape
    n = b.shape[1]
    tm, tn, tk = _pick(m, 1024, LANE), _pick(n // col_slabs, 1024, LANE), _pick(t, 2048, 16)
    nk = t // tk
    per = n // col_slabs // tn

    def body(a_ref, b_ref, o_ref, acc):
        kk = pl.program_id(2)

        @pl.when(kk == 0)
        def _():
            acc[...] = jnp.zeros_like(acc)

        av = a_ref[...]
        if a_fn is not None:
            av = a_fn(av)
        acc[...] += lax.dot_general(av.astype(BF16), b_ref[...].astype(BF16), (((0,), (0,)), ((), ())),
                                    preferred_element_type=F32)

        @pl.when(kk == nk - 1)
        def _():
            o_ref[...] = acc[...]

    return pl.pallas_call(
        body, name=name, grid=(m // tm, n // tn, nk),
        in_specs=[pl.BlockSpec((tk, tm), lambda i, j, kk: (kk, i)), pl.BlockSpec((tk, tn), lambda i, j, kk: (kk, j))],
        out_specs=(pl.BlockSpec((tm, tn), lambda i, j, kk: (i, j)) if col_slabs == 1
                   else pl.BlockSpec((None, tm, tn), lambda i, j, kk: (j // per, i, j % per))),
        out_shape=jax.ShapeDtypeStruct((m, n) if col_slabs == 1 else (col_slabs, m, n // col_slabs), F32),
        scratch_shapes=[pltpu.VMEM((tm, tn), F32)],
        compiler_params=_cparams("parallel", "parallel", "arbitrary"),
    )(a, b)


CONV_TILE = 256
HALO = 8


def conv_fwd(name, xs, w):
    t = xs[0][0].shape[0]
    c = xs[0][2]
    kw = w.shape[0]
    tt = min(CONV_TILE, t)
    nx = len(xs)

    def body(*refs):
        cur_refs, prev_refs, w_ref, y_ref = refs[:nx], refs[nx:2 * nx], refs[2 * nx], refs[2 * nx + 1]
        i = pl.program_id(0)
        cur, prev = cur_refs[0][...], prev_refs[0][...]
        for r, p in zip(cur_refs[1:], prev_refs[1:]):
            cur, prev = cur * r[...], prev * p[...]
        prev = jnp.where(i > 0, prev, 0.0)
        xx = jnp.concatenate([prev, cur], axis=0)
        acc = jnp.zeros((tt, c), F32)
        for k in range(kw):
            sh = kw - 1 - k
            rolled = pltpu.roll(xx, sh, 0) if sh else xx
            acc = acc + rolled[HALO:] * w_ref[k:k + 1, :]
        y_ref[...] = acc

    def cur_spec(off):
        cb = off // c
        return pl.BlockSpec((tt, c), lambda i: (i, cb))

    def prev_spec(off):
        cb = off // c
        return pl.BlockSpec((HALO, c), lambda i: (jnp.maximum(i * (tt // HALO) - 1, 0), cb))

    for (_, off, wd) in xs:
        assert wd == c and off % c == 0
    return pl.pallas_call(
        body, name=name, grid=(t // tt,),
        in_specs=[cur_spec(off) for (_, off, _) in xs] + [prev_spec(off) for (_, off, _) in xs]
        + [pl.BlockSpec(w.shape, lambda i: (0, 0))],
        out_specs=pl.BlockSpec((tt, c), lambda i: (i, 0)),
        out_shape=jax.ShapeDtypeStruct((t, c), F32),
        compiler_params=_cparams("arbitrary"),
    )(*[x[0] for x in xs], *[x[0] for x in xs], w)


def conv_bwd(name, dy, xs, w, dx_dtype):
    t = xs[0][0].shape[0]
    c = xs[0][2]
    kw = w.shape[0]
    tt = min(CONV_TILE, t)
    nt = t // tt
    nx = len(xs)
    n = tt + HALO

    def body(*refs):
        dy_ref = refs[0]
        cur_refs, prev_refs = refs[1:1 + nx], refs[1 + nx:1 + 2 * nx]
        w_ref = refs[1 + 2 * nx]
        dx_refs = refs[2 + 2 * nx:2 + 3 * nx]
        dw_ref, carry = refs[2 + 3 * nx], refs[3 + 3 * nx]
        i = pl.program_id(0)

        @pl.when(i == 0)
        def _():
            carry[...] = jnp.zeros_like(carry)
            dw_ref[...] = jnp.zeros_like(dw_ref)

        curs = [r[...] for r in cur_refs]
        cur, prev = curs[0], prev_refs[0][...]
        for cv, p in zip(curs[1:], prev_refs[1:]):
            cur, prev = cur * cv, prev * p[...]
        prev = jnp.where(i < nt - 1, prev, 0.0)
        xx = jnp.concatenate([prev, cur], axis=0)
        dyv = dy_ref[...]
        dpad = jnp.concatenate([jnp.zeros((HALO, c), F32), dyv], axis=0)
        dxx = jnp.zeros((n, c), F32)
        dws = []
        for k in range(kw):
            sh = kw - 1 - k
            dxx = dxx + (pltpu.roll(dpad, n - sh, 0) if sh else dpad) * w_ref[k:k + 1, :]
            rolled = pltpu.roll(xx, sh, 0) if sh else xx
            dws.append(jnp.sum(dyv * rolled[HALO:], axis=0, keepdims=True))
        dw_ref[...] += jnp.concatenate(dws, axis=0)
        dcur = jnp.concatenate([dxx[HALO:tt], dxx[tt:] + carry[...]], axis=0)
        carry[...] = dxx[:HALO]
        for j, ref in enumerate(dx_refs):
            d = dcur
            for jj, cv in enumerate(curs):
                if jj != j:
                    d = d * cv
            ref[...] = d.astype(ref.dtype)

    def cur_spec(off):
        cb = off // c
        return pl.BlockSpec((tt, c), lambda i: (nt - 1 - i, cb))

    def prev_spec(off):
        cb = off // c
        return pl.BlockSpec((HALO, c), lambda i: (jnp.maximum((nt - 1 - i) * (tt // HALO) - 1, 0), cb))

    res = pl.pallas_call(
        body, name=name, grid=(nt,),
        in_specs=[pl.BlockSpec((tt, c), lambda i: (nt - 1 - i, 0))] + [cur_spec(off) for (_, off, _) in xs]
        + [prev_spec(off) for (_, off, _) in xs] + [pl.BlockSpec(w.shape, lambda i: (0, 0))],
        out_specs=[pl.BlockSpec((tt, c), lambda i: (nt - 1 - i, 0)) for _ in xs] + [pl.BlockSpec(w.shape, lambda i: (0, 0))],
        out_shape=[jax.ShapeDtypeStruct((t, c), dx_dtype) for _ in xs] + [jax.ShapeDtypeStruct(w.shape, F32)],
        scratch_shapes=[pltpu.VMEM((HALO, c), F32)],
        compiler_params=_cparams("arbitrary"),
    )(dy, *[x[0] for x in xs], *[x[0] for x in xs], w)
    return res[:nx], res[nx]


def _const_spec(a):
    return pl.BlockSpec(a.shape, lambda i: (0,) * a.ndim)


GDN_STEP = 2


def _gdn_rows(ref, c):
    return ref[c * GDN_CS:(c + 1) * GDN_CS, :]


def gdn_fwd(name, cq, proj, alog, dtb, nw):
    t = cq.shape[0]
    nc = t // GDN_CS
    ns = min(GDN_STEP, nc)
    rows = ns * GDN_CS

    def body(cq_ref, z_ref, ba_ref, alog_ref, dtb_ref, nw_ref, y_ref, ssave_ref, tsave_ref, s_scr):
        @pl.when(pl.program_id(0) == 0)
        def _():
            s_scr[...] = jnp.zeros_like(s_scr)

        s = s_scr[...]
        ys, saved, tinvs = [], [], []
        for c in range(ns):
            saved.append(s)
            y, s, tinv = _gdn_chunk(_gdn_rows(cq_ref, c), _gdn_rows(z_ref, c), _gdn_rows(ba_ref, c), s, alog_ref[...],
                                    dtb_ref[...], nw_ref[...])
            ys.append(y)
            tinvs.append(tinv)
        y_ref[...] = jnp.concatenate(ys, axis=0)
        ssave_ref[...] = jnp.stack(saved, axis=0)
        tsave_ref[...] = jnp.stack(tinvs, axis=0)
        s_scr[...] = s

    return pl.pallas_call(
        body, name=name, grid=(nc // ns,),
        in_specs=[pl.BlockSpec((rows, 3 * GW), lambda i: (i, 0)),
                  pl.BlockSpec((rows, GW), lambda i: (i, O_GZ // GW)),
                  pl.BlockSpec((rows, LANE), lambda i: (i, O_GBA // LANE)),
                  _const_spec(alog), _const_spec(dtb), _const_spec(nw)],
        out_specs=[pl.BlockSpec((rows, GW), lambda i: (i, 0)),
                   pl.BlockSpec((ns, LANE, N_GDN * LANE), lambda i: (i, 0, 0)),
                   pl.BlockSpec((ns, N_GDN * GDN_CS, N_GDN * GDN_CS), lambda i: (i, 0, 0))],
        out_shape=[jax.ShapeDtypeStruct((t, GW), F32), jax.ShapeDtypeStruct((nc, LANE, N_GDN * LANE), F32),
                   jax.ShapeDtypeStruct((nc, N_GDN * GDN_CS, N_GDN * GDN_CS), F32)],
        scratch_shapes=[pltpu.VMEM((LANE, N_GDN * LANE), F32)],
        compiler_params=_cparams("arbitrary"),
    )(cq, proj, proj, alog, dtb, nw)


def gdn_bwd(name, dy, dy_off, cq, proj, ssave, tsave, alog, dtb, nw):
    t = cq.shape[0]
    nc = t // GDN_CS
    ns = min(GDN_STEP, nc)
    rows = ns * GDN_CS

    def body(dy_ref, cq_ref, z_ref, ba_ref, ssave_ref, tsave_ref, alog_ref, dtb_ref, nw_ref,
             dcq_ref, dz_ref, dba_ref, dalog_ref, ddtb_ref, dnw_ref, ds_scr):
        @pl.when(pl.program_id(0) == 0)
        def _():
            ds_scr[...] = jnp.zeros_like(ds_scr)
            dalog_ref[...] = jnp.zeros_like(dalog_ref)
            ddtb_ref[...] = jnp.zeros_like(ddtb_ref)
            dnw_ref[...] = jnp.zeros_like(dnw_ref)

        ds = ds_scr[...]
        dcq, dzs, dbas = [None] * ns, [None] * ns, [None] * ns
        dal, ddt, dnw = (jnp.zeros(r.shape, F32) for r in (dalog_ref, ddtb_ref, dnw_ref))
        for c in reversed(range(ns)):
            _, vjp = jax.vjp(functools.partial(_gdn_chunk, tinv_saved=tsave_ref[c]), _gdn_rows(cq_ref, c),
                             _gdn_rows(z_ref, c), _gdn_rows(ba_ref, c), ssave_ref[c], alog_ref[...], dtb_ref[...],
                             nw_ref[...])
            dcq[c], dzs[c], dbas[c], ds, dal_c, ddt_c, dnw_c = vjp((_gdn_rows(dy_ref, c), ds))
            dal, ddt, dnw = dal + dal_c, ddt + ddt_c, dnw + dnw_c
        dcq_ref[...] = jnp.concatenate(dcq, axis=0)
        dz_ref[...] = jnp.concatenate(dzs, axis=0).astype(dz_ref.dtype)
        dba_ref[...] = jnp.concatenate(dbas, axis=0).astype(dba_ref.dtype)
        ds_scr[...] = ds
        dalog_ref[...] += dal
        ddtb_ref[...] += ddt
        dnw_ref[...] += dnw

    rev = lambda i: nc // ns - 1 - i
    return pl.pallas_call(
        body, name=name, grid=(nc // ns,),
        in_specs=[pl.BlockSpec((rows, GW), lambda i: (rev(i), dy_off // GW)),
                  pl.BlockSpec((rows, 3 * GW), lambda i: (rev(i), 0)),
                  pl.BlockSpec((rows, GW), lambda i: (rev(i), O_GZ // GW)),
                  pl.BlockSpec((rows, LANE), lambda i: (rev(i), O_GBA // LANE)),
                  pl.BlockSpec((ns, LANE, N_GDN * LANE), lambda i: (rev(i), 0, 0)),
                  pl.BlockSpec((ns, N_GDN * GDN_CS, N_GDN * GDN_CS), lambda i: (rev(i), 0, 0)),
                  _const_spec(alog), _const_spec(dtb), _const_spec(nw)],
        out_specs=[pl.BlockSpec((rows, 3 * GW), lambda i: (rev(i), 0)),
                   pl.BlockSpec((rows, GW), lambda i: (rev(i), 0)),
                   pl.BlockSpec((rows, LANE), lambda i: (rev(i), 0)),
                   _const_spec(alog), _const_spec(dtb), _const_spec(nw)],
        out_shape=[jax.ShapeDtypeStruct((t, 3 * GW), F32), jax.ShapeDtypeStruct((t, GW), BF16),
                   jax.ShapeDtypeStruct((t, LANE), BF16), jax.ShapeDtypeStruct(alog.shape, F32),
                   jax.ShapeDtypeStruct(dtb.shape, F32), jax.ShapeDtypeStruct(nw.shape, F32)],
        scratch_shapes=[pltpu.VMEM((LANE, N_GDN * LANE), F32)],
        compiler_params=_cparams("arbitrary"),
    )(dy, cq, proj, proj, ssave, tsave, alog, dtb, nw)


def ssd_fwd(name, cx, proj, convb, alog, dtb, dsk, nw):
    t = cx.shape[0]
    nc = t // SSD_CS

    def body(cx_ref, z_ref, dt_ref, cb_ref, alog_ref, dtb_ref, dsk_ref, nw_ref, y_ref, ssave_ref, s_scr):
        @pl.when(pl.program_id(0) == 0)
        def _():
            s_scr[...] = jnp.zeros_like(s_scr)

        s_in = s_scr[...]
        y, s_new = _ssd_chunk(cx_ref[...], z_ref[...], dt_ref[...], s_in, cb_ref[...], alog_ref[...], dtb_ref[...],
                              dsk_ref[...], nw_ref[...])
        y_ref[...] = y
        ssave_ref[0] = s_in
        s_scr[...] = s_new

    return pl.pallas_call(
        body, name=name, grid=(nc,),
        in_specs=[pl.BlockSpec((SSD_CS, 1024), lambda i: (i, 0)),
                  pl.BlockSpec((SSD_CS, GW), lambda i: (i, O_SZ // GW)),
                  pl.BlockSpec((SSD_CS, LANE), lambda i: (i, O_SDT // LANE)),
                  _const_spec(convb), _const_spec(alog), _const_spec(dtb), _const_spec(dsk), _const_spec(nw)],
        out_specs=[pl.BlockSpec((SSD_CS, GW), lambda i: (i, 0)),
                   pl.BlockSpec((1, LANE, GW), lambda i: (i, 0, 0))],
        out_shape=[jax.ShapeDtypeStruct((t, GW), F32), jax.ShapeDtypeStruct((nc, LANE, GW), F32)],
        scratch_shapes=[pltpu.VMEM((LANE, GW), F32)],
        compiler_params=_cparams("arbitrary"),
    )(cx, proj, proj, convb, alog, dtb, dsk, nw)


def ssd_bwd(name, dy, dy_off, cx, proj, ssave, convb, alog, dtb, dsk, nw):
    t = cx.shape[0]
    nc = t // SSD_CS

    def body(dy_ref, cx_ref, z_ref, dt_ref, ssave_ref, cb_ref, alog_ref, dtb_ref, dsk_ref, nw_ref,
             dcx_ref, dz_ref, ddt_ref, dcb_ref, dalog_ref, ddtb_ref, ddsk_ref, dnw_ref, ds_scr):
        @pl.when(pl.program_id(0) == 0)
        def _():
            ds_scr[...] = jnp.zeros_like(ds_scr)
            for r in (dcb_ref, dalog_ref, ddtb_ref, ddsk_ref, dnw_ref):
                r[...] = jnp.zeros_like(r)

        _, vjp = jax.vjp(_ssd_chunk, cx_ref[...], z_ref[...], dt_ref[...], ssave_ref[0], cb_ref[...], alog_ref[...],
                         dtb_ref[...], dsk_ref[...], nw_ref[...])
        dcx, dz, ddt, ds, dcb, dal, ddtb, ddsk, dnw = vjp((dy_ref[...], ds_scr[...]))
        dcx_ref[...] = dcx
        dz_ref[...] = dz.astype(dz_ref.dtype)
        ddt_ref[...] = ddt.astype(ddt_ref.dtype)
        ds_scr[...] = ds
        dcb_ref[...] += dcb
        dalog_ref[...] += dal
        ddtb_ref[...] += ddtb
        ddsk_ref[...] += ddsk
        dnw_ref[...] += dnw

    rev = lambda i: nc - 1 - i
    return pl.pallas_call(
        body, name=name, grid=(nc,),
        in_specs=[pl.BlockSpec((SSD_CS, GW), lambda i: (rev(i), dy_off // GW)),
                  pl.BlockSpec((SSD_CS, 1024), lambda i: (rev(i), 0)),
                  pl.BlockSpec((SSD_CS, GW), lambda i: (rev(i), O_SZ // GW)),
                  pl.BlockSpec((SSD_CS, LANE), lambda i: (rev(i), O_SDT // LANE)),
                  pl.BlockSpec((1, LANE, GW), lambda i: (rev(i), 0, 0)),
                  _const_spec(convb), _const_spec(alog), _const_spec(dtb), _const_spec(dsk), _const_spec(nw)],
        out_specs=[pl.BlockSpec((SSD_CS, 1024), lambda i: (rev(i), 0)),
                   pl.BlockSpec((SSD_CS, GW), lambda i: (rev(i), 0)),
                   pl.BlockSpec((SSD_CS, LANE), lambda i: (rev(i), 0)),
                   _const_spec(convb), _const_spec(alog), _const_spec(dtb), _const_spec(dsk), _const_spec(nw)],
        out_shape=[jax.ShapeDtypeStruct((t, 1024), F32), jax.ShapeDtypeStruct((t, GW), BF16),
                   jax.ShapeDtypeStruct((t, LANE), BF16), jax.ShapeDtypeStruct(convb.shape, F32),
                   jax.ShapeDtypeStruct(alog.shape, F32), jax.ShapeDtypeStruct(dtb.shape, F32),
                   jax.ShapeDtypeStruct(dsk.shape, F32), jax.ShapeDtypeStruct(nw.shape, F32)],
        scratch_shapes=[pltpu.VMEM((LANE, GW), F32)],
        compiler_params=_cparams("arbitrary"),
    )(dy, cx, proj, proj, ssave, convb, alog, dtb, dsk, nw)


def _swa_specs(nb, rev):
    w = ATT_W
    cur = lambda i: (nb - 1 - i) if rev else i
    prv = lambda i: jnp.maximum(cur(i) - 1, 0)
    return [pl.BlockSpec((w, 1024), lambda i: (cur(i), O_AQ // 1024)),
            pl.BlockSpec((w, 256), lambda i: (cur(i), O_AK // 256)),
            pl.BlockSpec((w, 256), lambda i: (prv(i), O_AK // 256)),
            pl.BlockSpec((w, 256), lambda i: (cur(i), O_AV // 256)),
            pl.BlockSpec((w, 256), lambda i: (prv(i), O_AV // 256)),
            pl.BlockSpec((w, LANE), lambda i: (cur(i), 0)), pl.BlockSpec((w, LANE), lambda i: (prv(i), 0)),
            pl.BlockSpec((w, LANE), lambda i: (cur(i), 0)), pl.BlockSpec((w, LANE), lambda i: (prv(i), 0))]


def swa_fwd(name, proj, cos, sin, sinks):
    t = proj.shape[0]
    nb = t // ATT_W

    def body(q_ref, kc_ref, kp_ref, vc_ref, vp_ref, cc_ref, cp_ref, sc_ref, sp_ref, snk_ref, y_ref):
        notfirst = jnp.where(pl.program_id(0) > 0, 1.0, 0.0).astype(F32)
        cc, cp, sc, sp = cc_ref[...], cp_ref[...], sc_ref[...], sp_ref[...]
        outs = []
        for g in range(2):
            sl = slice(g * LANE, (g + 1) * LANE)
            qg = jnp.concatenate([_rope(q_ref[:, (4 * g + r) * LANE:(4 * g + r + 1) * LANE], cc, sc)
                                  for r in range(4)], axis=1)
            outs.append(_swa_block(qg, _rope(kp_ref[:, sl], cp, sp), _rope(kc_ref[:, sl], cc, sc), vp_ref[:, sl],
                                   vc_ref[:, sl], snk_ref[...], notfirst, g=g))
        y_ref[...] = jnp.concatenate(outs, axis=1)

    return pl.pallas_call(
        body, name=name, grid=(nb,),
        in_specs=_swa_specs(nb, False) + [_const_spec(sinks)],
        out_specs=pl.BlockSpec((ATT_W, 1024), lambda i: (i, 0)),
        out_shape=jax.ShapeDtypeStruct((t, 1024), F32),
        compiler_params=_cparams("arbitrary"),
    )(proj, proj, proj, proj, proj, cos, cos, sin, sin, sinks)


def swa_bwd(name, dy, dy_off, proj, cos, sin, sinks):
    t = proj.shape[0]
    nb = t // ATT_W

    def body(dy_ref, q_ref, kc_ref, kp_ref, vc_ref, vp_ref, cc_ref, cp_ref, sc_ref, sp_ref, snk_ref,
             dq_ref, dk_ref, dv_ref, dsnk_ref, ck_scr, cv_scr):
        i = pl.program_id(0)

        @pl.when(i == 0)
        def _():
            ck_scr[...] = jnp.zeros_like(ck_scr)
            cv_scr[...] = jnp.zeros_like(cv_scr)
            dsnk_ref[...] = jnp.zeros_like(dsnk_ref)

        notfirst = jnp.where(i < nb - 1, 1.0, 0.0).astype(F32)
        cc, cp, sc, sp = cc_ref[...], cp_ref[...], sc_ref[...], sp_ref[...]
        ck_in, cv_in = ck_scr[...], cv_scr[...]
        dq, dk, dv, ck, cv, dsnk = [], [], [], [], [], jnp.zeros(dsnk_ref.shape, F32)
        for g in range(2):
            sl = slice(g * LANE, (g + 1) * LANE)
            qg = jnp.concatenate([_rope(q_ref[:, (4 * g + r) * LANE:(4 * g + r + 1) * LANE], cc, sc)
                                  for r in range(4)], axis=1)
            _, vjp = jax.vjp(functools.partial(_swa_block, g=g), qg, _rope(kp_ref[:, sl], cp, sp),
                             _rope(kc_ref[:, sl], cc, sc), vp_ref[:, sl], vc_ref[:, sl], snk_ref[...], notfirst)
            dqg, dkp, dkc, dvp, dvc, dsnk_g, _ = vjp(dy_ref[:, g * GW:(g + 1) * GW])
            dq += [_unrope(dqg[:, r * LANE:(r + 1) * LANE], cc, sc) for r in range(4)]
            dk.append(_unrope(dkc, cc, sc) + ck_in[:, sl])
            dv.append(dvc + cv_in[:, sl])
            ck.append(_unrope(dkp, cp, sp))
            cv.append(dvp)
            dsnk = dsnk + dsnk_g
        dq_ref[...] = jnp.concatenate(dq, axis=1).astype(dq_ref.dtype)
        dk_ref[...] = jnp.concatenate(dk, axis=1).astype(dk_ref.dtype)
        dv_ref[...] = jnp.concatenate(dv, axis=1).astype(dv_ref.dtype)
        ck_scr[...] = jnp.concatenate(ck, axis=1)
        cv_scr[...] = jnp.concatenate(cv, axis=1)
        dsnk_ref[...] += dsnk

    rev = lambda i: nb - 1 - i
    return pl.pallas_call(
        body, name=name, grid=(nb,),
        in_specs=[pl.BlockSpec((ATT_W, 1024), lambda i: (rev(i), dy_off // 1024))] + _swa_specs(nb, True)
        + [_const_spec(sinks)],
        out_specs=[pl.BlockSpec((ATT_W, 1024), lambda i: (rev(i), 0)),
                   pl.BlockSpec((ATT_W, 256), lambda i: (rev(i), 0)),
                   pl.BlockSpec((ATT_W, 256), lambda i: (rev(i), 0)), _const_spec(sinks)],
        out_shape=[jax.ShapeDtypeStruct((t, 1024), BF16), jax.ShapeDtypeStruct((t, 256), BF16),
                   jax.ShapeDtypeStruct((t, 256), BF16), jax.ShapeDtypeStruct(sinks.shape, F32)],
        scratch_shapes=[pltpu.VMEM((ATT_W, 256), F32), pltpu.VMEM((ATT_W, 256), F32)],
        compiler_params=_cparams("arbitrary"),
    )(dy, proj, proj, proj, proj, proj, cos, cos, sin, sin, sinks)


def _me():
    return lax.axis_index("x"), lax.axis_index("y"), lax.axis_index("c")


ANY = pl.BlockSpec(memory_space=pl.ANY)


def allgather8(name, buf):
    r = buf.shape[0]

    def body(x_ref, out_ref, send_sems, recv_sems, local_sem):
        x, y, c = _me()
        me = 4 * x + 2 * y + c
        mine = pltpu.make_async_copy(x_ref, out_ref.at[me], local_sem)
        mine.start()
        copies = []
        for k in range(1, 8):
            bx, by, bc = (k >> 2) & 1, (k >> 1) & 1, k & 1
            to = ((1 - x) if bx else x, (1 - y) if by else y, (1 - c) if bc else c)
            cp = pltpu.make_async_remote_copy(src_ref=x_ref, dst_ref=out_ref.at[me], send_sem=send_sems.at[k - 1],
                                              recv_sem=recv_sems.at[k - 1], device_id=to, device_id_type=MESH)
            cp.start()
            copies.append(cp)
        for cp in copies:
            cp.wait_recv()
        for cp in copies:
            cp.wait_send()
        mine.wait()

    return pl.pallas_call(
        body, name=name, out_shape=jax.ShapeDtypeStruct((8, r, LANE), buf.dtype),
        in_specs=[pl.BlockSpec(memory_space=pltpu.VMEM)], out_specs=pl.BlockSpec(memory_space=pltpu.VMEM),
        scratch_shapes=[pltpu.SemaphoreType.DMA((7,)), pltpu.SemaphoreType.DMA((7,)), pltpu.SemaphoreType.DMA],
        compiler_params=pltpu.CompilerParams(vmem_limit_bytes=VMEM_LIMIT),
    )(buf)


def _sems(n):
    return [pltpu.SemaphoreType.DMA((n,)), pltpu.SemaphoreType.DMA((n,))]


def allgather_chips(name, shards):
    n = len(shards)

    def body(*refs):
        x_refs, out_refs, send_sems, recv_sems = refs[:n], refs[n:2 * n], refs[2 * n], refs[2 * n + 1]
        x, y, c = _me()
        chip = 2 * x + y
        chips = [(1 - x, y), (x, 1 - y), (1 - x, 1 - y)]

        def rc(src, dst, k, to):
            return pltpu.make_async_remote_copy(src_ref=src, dst_ref=dst, send_sem=send_sems.at[k],
                                                recv_sem=recv_sems.at[k], device_id=to, device_id_type=MESH)

        halves = [pl.ds(pl.multiple_of(c * (s.shape[0] // 2), 16), s.shape[0] // 2) for s in shards]
        own, first = [], {}
        for i in range(n):
            cp = rc(x_refs[i], out_refs[i].at[chip], 7 * i + 6, (x, y, 1 - c))
            cp.start()
            own.append(cp)
            for k, (cx, cy) in enumerate(chips):
                cp = rc(x_refs[i].at[halves[i]], out_refs[i].at[chip, halves[i]], 7 * i + k, (cx, cy, c))
                cp.start()
                first[i, k] = cp
        passed = []
        for i in range(n):
            for k, (cx, cy) in enumerate(chips):
                first[i, k].wait_recv()
                src = out_refs[i].at[2 * cx + cy, halves[i]]
                cp = rc(src, src, 7 * i + 3 + k, (x, y, 1 - c))
                cp.start()
                passed.append(cp)
        for cp in passed + own:
            cp.wait_recv()
        for cp in list(first.values()) + passed + own:
            cp.wait_send()

    for s in shards:
        assert s.shape[0] % 32 == 0
    return pl.pallas_call(
        body, name=name, out_shape=[jax.ShapeDtypeStruct((4,) + s.shape, s.dtype) for s in shards],
        in_specs=[ANY] * n, out_specs=[ANY] * n, scratch_shapes=_sems(7 * n),
    )(*shards)


def ag_forward(name, bufs):
    n = len(bufs)

    def body(*refs):
        out_refs, send_sems, recv_sems = refs[n:2 * n], refs[2 * n], refs[2 * n + 1]
        x, y, c = _me()
        cps = []
        for i in range(n):
            rh = bufs[i].shape[1] // 2
            half = pl.ds(pl.multiple_of(c * rh, 16), rh)
            for k, (cx, cy) in enumerate([(1 - x, y), (x, 1 - y), (1 - x, 1 - y)]):
                src = out_refs[i].at[2 * cx + cy, half]
                cps.append(pltpu.make_async_remote_copy(src_ref=src, dst_ref=src, send_sem=send_sems.at[3 * i + k],
                                                        recv_sem=recv_sems.at[3 * i + k], device_id=(x, y, 1 - c),
                                                        device_id_type=MESH))
        for cp in cps:
            cp.start()
        for cp in cps:
            cp.wait_recv()
        for cp in cps:
            cp.wait_send()

    return pl.pallas_call(
        body, name=name, out_shape=[jax.ShapeDtypeStruct(b.shape, b.dtype) for b in bufs],
        in_specs=[ANY] * n, out_specs=[ANY] * n, scratch_shapes=_sems(3 * n),
        input_output_aliases={i: i for i in range(n)},
    )(*bufs)


def sibling_swap_halves(name, gs):
    n = len(gs)

    def body(*refs):
        g_refs, out_refs, send_sems, recv_sems = refs[:n], refs[n:2 * n], refs[2 * n], refs[2 * n + 1]
        x, y, c = _me()
        cps = [pltpu.make_async_remote_copy(src_ref=g_refs[i].at[:, 1 - c], dst_ref=out_refs[i],
                                            send_sem=send_sems.at[i], recv_sem=recv_sems.at[i],
                                            device_id=(x, y, 1 - c), device_id_type=MESH) for i in range(n)]
        for cp in cps:
            cp.start()
        for cp in cps:
            cp.wait_recv()
        for cp in cps:
            cp.wait_send()

    return pl.pallas_call(
        body, name=name, out_shape=[jax.ShapeDtypeStruct((4,) + g.shape[2:], g.dtype) for g in gs],
        in_specs=[ANY] * n, out_specs=[ANY] * n, scratch_shapes=_sems(n),
    )(*gs)


def chips_exchange(name, ps):
    n = len(ps)

    def body(*refs):
        p_refs, out_refs, send_sems, recv_sems = refs[:n], refs[n:2 * n], refs[2 * n], refs[2 * n + 1]
        x, y, c = _me()
        chip = 2 * x + y
        cps = []
        for i in range(n):
            for k, (cx, cy) in enumerate([(1 - x, y), (x, 1 - y), (1 - x, 1 - y)]):
                cps.append(pltpu.make_async_remote_copy(
                    src_ref=p_refs[i].at[2 * cx + cy], dst_ref=out_refs[i].at[chip], send_sem=send_sems.at[3 * i + k],
                    recv_sem=recv_sems.at[3 * i + k], device_id=(cx, cy, c), device_id_type=MESH))
        for cp in cps:
            cp.start()
        for cp in cps:
            cp.wait_recv()
        for cp in cps:
            cp.wait_send()

    return pl.pallas_call(
        body, name=name, out_shape=[jax.ShapeDtypeStruct(p.shape, p.dtype) for p in ps],
        in_specs=[ANY] * n, out_specs=[ANY] * n, scratch_shapes=_sems(3 * n),
    )(*ps)


def sibling_fill(name, ss):
    n = len(ss)

    def body(*refs):
        out_refs, send_sems, recv_sems = refs[n:2 * n], refs[2 * n], refs[2 * n + 1]
        x, y, c = _me()
        cps = []
        for i in range(n):
            rh = ss[i].shape[1] // 2
            mine = out_refs[i].at[:, pl.ds(pl.multiple_of(c * rh, 8), rh)]
            cps.append(pltpu.make_async_remote_copy(src_ref=mine, dst_ref=mine, send_sem=send_sems.at[i],
                                                    recv_sem=recv_sems.at[i], device_id=(x, y, 1 - c),
                                                    device_id_type=MESH))
        for cp in cps:
            cp.start()
        for cp in cps:
            cp.wait_recv()
        for cp in cps:
            cp.wait_send()

    return pl.pallas_call(
        body, name=name, out_shape=[jax.ShapeDtypeStruct(s.shape, s.dtype) for s in ss],
        in_specs=[ANY] * n, out_specs=[ANY] * n, scratch_shapes=_sems(n),
        input_output_aliases={i: i for i in range(n)},
    )(*ss)


def _rs_tile(rh, w):
    return _pick(rh, max(8, (1 << 19) // w // 8 * 8), 8)


def add_half(name, g, recv, idx):
    _, _, rh, w = g.shape
    tile = _rs_tile(rh, w)

    def body(s_ref, g_ref, r_ref, o_ref):
        o_ref[...] = (g_ref[...] + r_ref[...]).astype(o_ref.dtype)

    return pl.pallas_call(
        body, name=name,
        grid_spec=pltpu.PrefetchScalarGridSpec(
            num_scalar_prefetch=1, grid=(4, rh // tile),
            in_specs=[pl.BlockSpec((None, None, tile, w), lambda s, i, ix: (s, ix[1], i, 0)),
                      pl.BlockSpec((None, tile, w), lambda s, i, ix: (s, i, 0))],
            out_specs=pl.BlockSpec((None, tile, w), lambda s, i, ix: (s, i, 0))),
        out_shape=jax.ShapeDtypeStruct((4, rh, w), BF16),
        compiler_params=_cparams("parallel", "parallel"),
    )(idx, g, recv)


def rs_sum(name, g, recv1, recv2, idx, layer, acc=None):
    _, _, rh, w = g.shape
    tile = _rs_tile(rh, w)
    nt = rh // tile

    def body(s_ref, g_ref, r1_ref, a_ref, b_ref, c_ref, *rest):
        rest[-1][...] = (g_ref[...] + r1_ref[...]) + (a_ref[...].astype(F32) + b_ref[...].astype(F32)
                                                       + c_ref[...].astype(F32))

    def other(k):
        return pl.BlockSpec((None, tile, w), lambda i, ix: (jnp.bitwise_xor(ix[0], k), i, 0))

    ops = [idx, g, recv1, recv2, recv2, recv2] + ([acc] if acc is not None else [])
    return pl.pallas_call(
        body, name=name,
        grid_spec=pltpu.PrefetchScalarGridSpec(
            num_scalar_prefetch=1, grid=(rh // tile,),
            in_specs=[pl.BlockSpec((None, None, tile, w), lambda i, ix: (ix[0], ix[1], i, 0)),
                      pl.BlockSpec((None, tile, w), lambda i, ix: (ix[0], i, 0)), other(1), other(2), other(3)]
            + ([ANY] if acc is not None else []),
            out_specs=pl.BlockSpec((None, tile, w), lambda i, ix: (layer, ix[1] * nt + i, 0))),
        out_shape=jax.ShapeDtypeStruct((DEPTH, 2 * rh, w), F32),
        input_output_aliases={6: 0} if acc is not None else {},
        compiler_params=_cparams("parallel"),
    )(*ops)


def sum_slabs(name, a):
    n, r, _ = a.shape
    tile = _pick(r, 2048, 8)

    def body(a_ref, o_ref):
        acc = a_ref[0]
        for s in range(1, n):
            acc = acc + a_ref[s]
        o_ref[...] = acc

    return pl.pallas_call(
        body, name=name, grid=(r // tile,),
        in_specs=[pl.BlockSpec((n, tile, LANE), lambda i: (0, i, 0))],
        out_specs=pl.BlockSpec((tile, LANE), lambda i: (i, 0)),
        out_shape=jax.ShapeDtypeStruct((r, LANE), F32),
        compiler_params=_cparams("parallel"),
    )(a)


def f_premod(rv, cv):
    (x,), (w, scale, shift) = rv, cv
    y = x * lax.rsqrt(jnp.mean(x * x, -1, keepdims=True) + EPS) * w
    return [y * (1.0 + scale) + shift]


def f_postres(rv, cv):
    (x, y), (w, gate) = rv, cv
    return [x + gate * (y * lax.rsqrt(jnp.mean(y * y, -1, keepdims=True) + EPS) * w)]


def f_res_premod(rv, cv):
    (x, y), (w_post, gate, w_pre, scale, shift) = rv, cv
    x1, = f_postres([x, y], [w_post, gate])
    h, = f_premod([x1], [w_pre, scale, shift])
    return [x1, h]


def f_adamw(rv, cv):
    w, g, m, v = rv
    m = B1 * m + (1.0 - B1) * g
    v = B2 * v + (1.0 - B2) * jnp.square(g)
    m_hat = m / (1.0 - B1 ** STEP)
    v_hat = v / (1.0 - B2 ** STEP)
    return [-LR * (m_hat / (jnp.sqrt(v_hat) + AEPS) + WD * w), m, v], []


def adamw_layers(name, w, g, m, v):
    nl, r, wd = w.shape
    tile = r
    while tile * wd * 4 * 14 > 24 * 1024 * 1024 and tile % 16 == 0:
        tile //= 2

    outs = None
    for l in range(nl):
        def body(w_ref, g_ref, m_ref, v_ref, *rest):
            (dl, m2, v2), _ = f_adamw([w_ref[...], g_ref[...], m_ref[...], v_ref[...]], [])
            for ref, val in zip(rest[-3:], (dl, m2, v2)):
                ref[...] = val

        blk = pl.BlockSpec((None, tile, wd), lambda i, l=l: (l, i, 0))
        outs = pl.pallas_call(
            body, name=f"{name}{l}", grid=(r // tile,), in_specs=[blk] * 4 + ([ANY] * 3 if outs else []),
            out_specs=[blk] * 3, out_shape=[jax.ShapeDtypeStruct(w.shape, F32)] * 3,
            input_output_aliases={4: 0, 5: 1, 6: 2} if outs else {}, compiler_params=_cparams("parallel"),
        )(w, g, m, v, *(outs or ()))
    return outs


def adamw(name, w, g, m, v):
    r, wd = w.shape
    tile = r
    while tile * wd * 4 * 14 > 24 * 1024 * 1024 and tile % 16 == 0:
        tile //= 2
    (d, m2, v2), _ = rowmap(name, f_adamw, [(a, 0, wd) for a in (w, g, m, v)], [], [(wd, F32)] * 3, tile=tile)
    return d, m2, v2


_IN_SIZES = (512, 512, 512, 512, 4, 4, 512, 1024, 8, 512, 128, 128, 512, 512, 512)


def _relayout_w_in(w):
    d = w.shape[0]
    s = np.cumsum((0,) + _IN_SIZES)
    gq, gk, gv, gz, gb, ga, sz, sx, sdt, aq, ak, av, cb, cc, ch = [w[:, s[i]:s[i + 1]] for i in range(15)]

    def rope_pad(t, nh):
        return jnp.pad(t.reshape(d, nh, 2, 32), ((0, 0), (0, 0), (0, 0), (0, 32))).reshape(d, nh * LANE)

    def v_pad(t, nh):
        return jnp.pad(t.reshape(d, nh, 64), ((0, 0), (0, 0), (0, 64))).reshape(d, nh * LANE)

    def pad128(t):
        return jnp.pad(t, ((0, 0), (0, LANE - t.shape[1])))

    return jnp.concatenate([gq, gk, gv, cb, cc, ch, sx, rope_pad(aq, 8), gz, sz, rope_pad(ak, 2), v_pad(av, 2),
                            pad128(jnp.concatenate([gb, ga], axis=1)), pad128(sdt)], axis=1)


def _unlayout_g_in(g):
    d = g.shape[0]

    def rope_unpad(t, nh):
        return t.reshape(d, nh, 2, 64)[..., :32].reshape(d, nh * 64)

    def v_unpad(t, nh):
        return t.reshape(d, nh, LANE)[..., :64].reshape(d, nh * 64)

    return jnp.concatenate([g[:, 0:1536], g[:, O_GZ:O_GZ + 512], g[:, O_GBA:O_GBA + 8], g[:, O_SZ:O_SZ + 512],
                            g[:, O_XBC:O_XBC + 1024], g[:, O_SDT:O_SDT + 8], rope_unpad(g[:, O_AQ:O_AQ + 1024], 8),
                            rope_unpad(g[:, O_AK:O_AK + 256], 2), v_unpad(g[:, O_AV:O_AV + 256], 2),
                            g[:, O_CB:O_CB + 1536]], axis=1)


def _relayout_w_out(w):
    d = w.shape[1]
    yc = jnp.pad(w[1024:1536].reshape(8, 64, d), ((0, 0), (0, 64), (0, 0))).reshape(1024, d)
    return jnp.concatenate([w[:1024], yc, w[1536:]], axis=0)


def _unlayout_g_out(g):
    d = g.shape[1]
    return jnp.concatenate([g[:1024], g[1024:2048].reshape(8, LANE, d)[:, :64].reshape(512, d), g[2048:]], axis=0)


def _pack(arrs):
    rows, spec, off = [], [], 0
    for a in arrs:
        n = int(np.prod(a.shape))
        nr = -(-n // LANE)
        rows.append(jnp.pad(a.reshape(-1).astype(F32), (0, nr * LANE - n)).reshape(nr, LANE))
        spec.append((off, a.shape))
        off += nr
    pad = (-off) % 8
    if pad:
        rows.append(jnp.zeros((pad, LANE), F32))
    return jnp.concatenate(rows, axis=0), spec


def _unpack(buf, spec):
    out = []
    for off, shape in spec:
        n = int(np.prod(shape))
        nr = -(-n // LANE)
        out.append(buf[off:off + nr].reshape(-1)[:n].reshape(shape))
    return out


def _row128(v):
    return jnp.pad(v.astype(F32), (0, LANE - v.shape[0])).reshape(1, LANE)


BIG = ("w_in", "w_out", "w_up", "w_down")
SMALL_REPL = ("ada_b", "norm_pre_mix", "norm_post_mix", "norm_pre_mlp", "norm_post_mlp", "gdn_a_log", "gdn_dt_bias",
              "gdn_norm_w", "ssm_conv_b", "ssm_a_log", "ssm_dt_bias", "ssm_d", "ssm_norm_w", "attn_sinks")
SMALL_SHARD = ("gdn_conv_w", "ssm_conv_w", "sc_conv_w")
WEIGHTS = ("ada_w", "ada_b", "norm_pre_mix", "norm_post_mix", "norm_pre_mlp", "norm_post_mlp", "w_in", "w_out",
           "gdn_conv_w", "gdn_a_log", "gdn_dt_bias", "gdn_norm_w", "ssm_conv_w", "ssm_conv_b", "ssm_a_log",
           "ssm_dt_bias", "ssm_d", "ssm_norm_w", "attn_sinks", "sc_conv_w", "w_up", "w_down")


def _step(p, m, v, x, c, positions, loss_target):
    d = x.shape[2]
    xi, yi, ci = _me()
    chip = 2 * xi + yi
    me = 4 * xi + 2 * yi + ci
    x0 = x[0]
    tgt = loss_target[0]

    buf, spec = _pack([c[0], p["gdn_conv_w"], p["ssm_conv_w"], p["sc_conv_w"]])
    got = allgather8("ag_small_in", buf)
    per_dev = [_unpack(got[k], spec) for k in range(8)]
    c_all = jnp.stack([pd[0] for pd in per_dev], axis=0)
    conv_full = [jnp.concatenate([per_dev[4 * a + 2 * b][1 + j] for a in range(2) for b in range(2)], axis=-1)
                 for j in range(3)]
    gdn_conv_w, ssm_conv_w, sc_conv_w = conv_full

    c_pad = jnp.pad(c_all, ((0, 8), (0, 0)))
    silu_bf = lambda a: _silu(a).astype(BF16)
    nsh = p["ada_w"].shape[2]
    mods = []
    for l in range(DEPTH):
        mo = mm_nn(f"ada_fwd{l}", c_pad, p["ada_w"][l], F32, a_fn=silu_bf)
        mods.append(mo[:8] + lax.dynamic_slice(p["ada_b"][l], (chip * nsh,), (nsh,))[None])
    buf, spec = _pack([jnp.stack(mods, axis=0)])
    got = allgather8("ag_mod", buf)
    mod_sh = [_unpack(got[4 * a + 2 * b], spec)[0] for a in range(2) for b in range(2)]
    mod_all = jnp.concatenate(mod_sh, axis=-1)
    mod = lax.dynamic_index_in_dim(mod_all, me, axis=1, keepdims=False)

    items = [(l, n) for l in range(DEPTH) for n in BIG]
    shards = {(l, n): p[n][l].astype(BF16) for (l, n) in items}
    gathered = {(0, "w_in"): allgather_chips("ag_w_in0", [shards[0, "w_in"]])[0]}

    def gather_behind(keys):
        return ("gather", [shards[kk] for kk in keys])

    def settle(name, keys, landed):
        gathered.update(zip(keys, ag_forward(name, landed)))

    wfull = []

    inv_freq = ROPE_THETA ** (-jnp.arange(0, 64, 2, dtype=F32) / 64)
    ang = positions[0].astype(F32)[:, None] * inv_freq
    z32 = jnp.zeros_like(ang)
    cos_t = jnp.concatenate([jnp.cos(ang), z32, jnp.cos(ang), z32], axis=1)
    sin_t = jnp.concatenate([-jnp.sin(ang), z32, jnp.sin(ang), z32], axis=1)

    relu2 = lambda a: jnp.square(jnp.maximum(a, 0))

    def layer_consts(l):
        row = lambda v_: v_.reshape(1, -1).astype(F32)
        return dict(
            alog=_row128(p["gdn_a_log"][l]), gdtb=_row128(p["gdn_dt_bias"][l]), gnw=row(p["gdn_norm_w"][l]),
            scb=row(p["ssm_conv_b"][l]), salog=_row128(p["ssm_a_log"][l]), sdtb=_row128(p["ssm_dt_bias"][l]),
            sdsk=_row128(p["ssm_d"][l]), snw=row(p["ssm_norm_w"][l]), sinks=_row128(p["attn_sinks"][l]))

    saved = []
    xc = x0
    for l in range(DEPTH):
        w = dict(w_in=_relayout_w_in(jnp.concatenate([gathered[l, "w_in"][s] for s in range(4)], axis=1)))
        wfull.append(w)
        k = layer_consts(l)
        md = [mod[l, j * d:(j + 1) * d].reshape(1, d) for j in range(6)]
        npre, npost = p["norm_pre_mix"][l].reshape(1, d), p["norm_post_mix"][l].reshape(1, d)
        npre2, npost2 = p["norm_pre_mlp"][l].reshape(1, d), p["norm_post_mlp"][l].reshape(1, d)
        if l == 0:
            (h,), _ = rowmap("premix0", lambda rv, cv: (f_premod(rv, cv), []), [(xc, 0, d)], [npre, md[1], md[0]],
                             [(d, BF16)])
        behind = [(0, "w_out"), (0, "w_up")] if l == 0 else [(l, "w_up")]
        proj, landed = mm_nn(f"inproj{l}", h, w["w_in"], F32, carry=gather_behind(behind))
        settle(f"ag_forward_a{l}", behind, landed)
        w["w_out"] = _relayout_w_out(gathered[l, "w_out"].reshape(-1, d))
        w["w_up"] = gathered[l, "w_up"]
        cq = conv_fwd(f"gdn_conv{l}", [(proj, O_GQKV, 1536)], gdn_conv_w[l])
        ya, gs, gt = gdn_fwd(f"gdn{l}", cq, proj, k["alog"], k["gdtb"], k["gnw"])
        cx = conv_fwd(f"ssm_conv{l}", [(proj, O_XBC, 1024)], ssm_conv_w[l])
        yb, ss = ssd_fwd(f"ssd{l}", cx, proj, k["scb"], k["salog"], k["sdtb"], k["sdsk"], k["snw"])
        yc = swa_fwd(f"swa{l}", proj, cos_t, sin_t, k["sinks"])
        cd = conv_fwd(f"sc_conv{l}", [(proj, O_CC, GW), (proj, O_CH, GW)], sc_conv_w[l])
        (ycat,), _ = rowmap(f"ycat{l}", lambda rv, cv: ([jnp.concatenate([rv[0], rv[1], rv[2], rv[3] * rv[4]], 1)], []),
                            [(ya, 0, GW), (yb, 0, GW), (yc, 0, 1024), (proj, O_CB, GW), (cd, 0, GW)], [],
                            [(YCAT, BF16)])
        y = mm_nn(f"outproj{l}", ycat, w["w_out"], F32)
        (x1, h2), _ = rowmap(f"mix_mlp{l}", lambda rv, cv: (f_res_premod(rv, cv), []), [(xc, 0, d), (y, 0, d)],
                             [npost, md[2], npre2, md[4], md[3]], [(d, F32), (d, BF16)], tile=128)
        u, landed = mm_nn(f"up{l}", h2, w["w_up"], BF16, carry=gather_behind([(l, "w_down")]))
        settle(f"ag_forward_b{l}", [(l, "w_down")], landed)
        w["w_down"] = gathered[l, "w_down"].reshape(-1, d)
        if l + 1 < DEPTH:
            behind = [(l + 1, "w_in"), (l + 1, "w_out")]
            y2, landed = mm_nn(f"down{l}", u, w["w_down"], F32, a_fn=relu2, carry=gather_behind(behind))
            settle(f"ag_forward_c{l}", behind, landed)
        else:
            y2 = mm_nn(f"down{l}", u, w["w_down"], F32, a_fn=relu2)
        saved.append(dict(x=xc, h=h, proj=proj, cq=cq, gs=gs, gt=gt, cx=cx, ss=ss, cd=cd, ycat=ycat, y=y, x1=x1, h2=h2, u=u,
                          y2=y2, md=md, k=k))
        if l + 1 < DEPTH:
            (xc, h), _ = rowmap(f"mlp_mix{l}", lambda rv, cv: (f_res_premod(rv, cv), []), [(x1, 0, d), (y2, 0, d)],
                                [npost2, md[5], p["norm_pre_mix"][l + 1].reshape(1, d),
                                 mod[l + 1, d:2 * d].reshape(1, d), mod[l + 1, 0:d].reshape(1, d)],
                                [(d, F32), (d, BF16)], tile=128)
        else:
            (xc,), _ = rowmap(f"postmlp{l}", lambda rv, cv: (f_postres(rv, cv), []), [(x1, 0, d), (y2, 0, d)],
                              [npost2, md[5]], [(d, F32)])

    def f_loss(rv, cv):
        err = rv[0] - rv[1]
        part = 0.5 * jnp.sum(jnp.mean(err * err, axis=-1, keepdims=True), axis=0, keepdims=True)
        return [err * (1.0 / d)], [jnp.broadcast_to(part, (1, LANE))]

    (dx,), (loss_part,) = rowmap("loss", f_loss, [(xc, 0, d), (tgt, 0, d)], [], [(d, F32)], [(1, LANE)])

    idx = jnp.stack([chip, ci]).astype(jnp.int32)
    g4s, recv1s, recv2s, waiting = {}, {}, {}, []

    def reduce_start(key, g):
        l_, n_ = key
        g4 = g.reshape(4, 2, g.shape[1] // 2, g.shape[2])
        r1 = sibling_swap_halves(f"rs_sibling_{n_}{l_}", [g4])[0]
        g4s[key], recv1s[key] = g4, r1
        waiting.append((key, add_half(f"rs_add_{n_}{l_}", g4, r1, idx)))

    def mm_carrying(name, *args, **kw):
        if not waiting:
            return mm_nn(name, *args, **kw)
        out, landed = mm_nn(name, *args, carry=("exchange", [pt for _, pt in waiting]), **kw)
        recv2s.update(zip([kk for kk, _ in waiting], landed))
        waiting.clear()
        return out

    gsm, dmd = {}, {}
    for l in reversed(range(DEPTH)):
        w, s = wfull[l], saved[l]
        k, md = s["k"], s["md"]
        npre, npost = p["norm_pre_mix"][l].reshape(1, d), p["norm_post_mix"][l].reshape(1, d)
        npre2, npost2 = p["norm_pre_mlp"][l].reshape(1, d), p["norm_post_mlp"][l].reshape(1, d)
        if l == DEPTH - 1:
            (dx1a, dy2), (dnpost2, dgate_m) = rowmap_vjp(f"postmlp_b{l}", f_postres, [(s["x1"], 0, d), (s["y2"], 0, d)],
                                                          [npost2, md[5]], [(dx, 0, d)], 2, [F32, BF16])
            gsm[l, "norm_post_mlp"], dmd[l, 5] = dnpost2[0], dgate_m
        du = mm_carrying(f"down_b{l}", dy2, w["w_down"], BF16, extra=s["u"], wt=True,
                         epi=lambda r, uu: r * (2.0 * jnp.maximum(uu.astype(F32), 0.0)))
        reduce_start((l, "w_down"), mm_tn(f"down_g{l}", s["u"], dy2, a_fn=relu2).reshape(4, -1, d))
        dh2 = mm_carrying(f"up_b{l}", du, w["w_up"], F32, wt=True)
        reduce_start((l, "w_up"), mm_tn(f"up_g{l}", s["h2"], du, col_slabs=4))
        (dxa, dy), (dnpost, dgate_a, dnpre2, dscale_m, dshift_m) = rowmap_vjp(
            f"mix_mlp_b{l}", f_res_premod, [(s["x"], 0, d), (s["y"], 0, d)], [npost, md[2], npre2, md[4], md[3]],
            [(dx1a, 0, d), (dh2, 0, d)], 2, [F32, BF16], tile=128)
        dycat = mm_nn(f"outproj_b{l}", dy, w["w_out"], F32, wt=True)
        reduce_start((l, "w_out"), _unlayout_g_out(mm_tn(f"outproj_g{l}", s["ycat"], dy)).reshape(4, -1, d))
        (dcb, dcd), _ = rowmap(f"scgate_b{l}", lambda rv, cv: ([rv[0] * rv[2], rv[0] * rv[1]], []),
                               [(dycat, 2048, GW), (s["proj"], O_CB, GW), (s["cd"], 0, GW)], [],
                               [(GW, BF16), (GW, F32)])
        (dcc, dch), g_scw = conv_bwd(f"sc_conv_b{l}", dcd, [(s["proj"], O_CC, GW), (s["proj"], O_CH, GW)],
                                     sc_conv_w[l], BF16)
        daq, dak, dav, g_sinks = swa_bwd(f"swa_b{l}", dycat, 1024, s["proj"], cos_t, sin_t, k["sinks"])
        dcx, dsz, dsdt, g_scb, g_salog, g_sdtb, g_sdsk, g_snw = ssd_bwd(
            f"ssd_b{l}", dycat, 512, s["cx"], s["proj"], s["ss"], k["scb"], k["salog"], k["sdtb"], k["sdsk"], k["snw"])
        (dxbc,), g_ssmw = conv_bwd(f"ssm_conv_b{l}", dcx, [(s["proj"], O_XBC, 1024)], ssm_conv_w[l], BF16)
        dcq, dgz, dgba, g_alog, g_gdtb, g_gnw = gdn_bwd(f"gdn_b{l}", dycat, 0, s["cq"], s["proj"], s["gs"], s["gt"],
                                                        k["alog"], k["gdtb"], k["gnw"])
        (dgqkv,), g_gdnw = conv_bwd(f"gdn_conv_b{l}", dcq, [(s["proj"], O_GQKV, 1536)], gdn_conv_w[l], BF16)
        dproj = jnp.concatenate([dgqkv, dcb, dcc, dch, dxbc, daq, dgz, dsz, dak, dav, dgba, dsdt], axis=1)
        def start_w_in():
            g_in = mm_tn(f"inproj_g{l}", s["h"], dproj)
            reduce_start((l, "w_in"), jnp.stack(jnp.split(_unlayout_g_in(g_in), 4, axis=1), axis=0))

        if l == 0:
            start_w_in()
        dh = mm_carrying(f"inproj_b{l}", dproj, w["w_in"], F32, wt=True)
        if l > 0:
            start_w_in()
            sp = saved[l - 1]
            (dx1a, dy2), (dnpost2, dgate_m, dnpre, dscale_a, dshift_a) = rowmap_vjp(
                f"mlp_mix_b{l - 1}", f_res_premod, [(sp["x1"], 0, d), (sp["y2"], 0, d)],
                [p["norm_post_mlp"][l - 1].reshape(1, d), sp["md"][5], npre, md[1], md[0]],
                [(dxa, 0, d), (dh, 0, d)], 2, [F32, BF16], tile=128)
            gsm[l - 1, "norm_post_mlp"], dmd[l - 1, 5] = dnpost2[0], dgate_m
        else:
            (dx,), (dnpre, dscale_a, dshift_a) = rowmap_vjp("premix_b0", f_premod, [(s["x"], 0, d)],
                                                            [npre, md[1], md[0]], [(dh, 0, d)], 1, [F32],
                                                            add_rows=[(dxa, 0, d)])
        dmd.update({(l, 0): dshift_a, (l, 1): dscale_a, (l, 2): dgate_a, (l, 3): dshift_m, (l, 4): dscale_m})
        gsm.update({(l, n_): g_ for n_, g_ in dict(
            norm_pre_mix=dnpre[0], norm_post_mix=dnpost[0], norm_pre_mlp=dnpre2[0],
            gdn_a_log=g_alog[0, :4], gdn_dt_bias=g_gdtb[0, :4], gdn_norm_w=g_gnw[0], ssm_conv_b=g_scb[0],
            ssm_a_log=g_salog[0, :8], ssm_dt_bias=g_sdtb[0, :8], ssm_d=g_sdsk[0, :8], ssm_norm_w=g_snw[0],
            attn_sinks=g_sinks[0, :8], gdn_conv_w=g_gdnw, ssm_conv_w=g_ssmw, sc_conv_w=g_scw).items()})
    grad_x = dx[None]
    dmods = [jnp.concatenate([dmd[l, j] for j in range(6)], axis=1)[0] for l in range(DEPTH)]
    gsmall = [{n_: gsm[l, n_] for n_ in SMALL_REPL + SMALL_SHARD if n_ != "ada_b"} for l in range(DEPTH)]

    buf, spec = _pack([jnp.stack(dmods, axis=0)])
    got = allgather8("ag_dmod", buf)
    dmod_sum = _unpack(sum_slabs("dmod_sum", got), spec)[0]
    dmod_all = jnp.stack([_unpack(got[kk], spec)[0] for kk in range(8)], axis=1)
    g_ada_w = []
    for l in range(DEPTH):
        dm = lax.dynamic_slice(dmod_all[l], (0, chip * nsh), (8, nsh))
        g_ada_w.append(mm_tn(f"ada_g{l}", c_pad, jnp.pad(dm, ((0, 8), (0, 0))), a_fn=silu_bf))
    g_ada_w = jnp.stack(g_ada_w, axis=0)

    names = [n for n in SMALL_REPL if n != "ada_b"] + list(SMALL_SHARD)
    buf, spec = _pack([jnp.stack([gsmall[l][n] for l in range(DEPTH)], axis=0) for n in names] + [loss_part])
    tot = _unpack(sum_slabs("small_sum", allgather8("ag_small_g", buf)), spec)
    gfull = dict(zip(names, tot[:-1]))
    loss = tot[-1][0, 0]
    grads = {"ada_w": g_ada_w, "ada_b": dmod_sum}
    for n in SMALL_REPL:
        if n != "ada_b":
            grads[n] = gfull[n]
    for n in SMALL_SHARD:
        wdt = p[n].shape[2]
        grads[n] = lax.dynamic_slice_in_dim(gfull[n], chip * wdt, wdt, axis=2)

    if waiting:
        recv2s.update(zip([kk for kk, _ in waiting], chips_exchange("rs_chips_last", [pt for _, pt in waiting])))
    half = []
    for n in BIG:
        acc = None
        for l in range(DEPTH):
            acc = rs_sum(f"rs_sum_{n}{l}", g4s[l, n], recv1s[l, n], recv2s[l, n], idx, l, acc)
        half.append(acc)
    grads.update(zip(BIG, sibling_fill("rs_join", half)))

    delta, new_m, new_v = {}, {}, {}
    for n in ("ada_w",) + BIG:
        delta[n], new_m[n], new_v[n] = adamw_layers(f"adamw_{n}", p[n], grads[n], m[n], v[n])
    small = list(SMALL_REPL) + list(SMALL_SHARD)
    bufs = []
    for src in (p, grads, m, v):
        b_, spec = _pack([src[n] for n in small])
        bufs.append(b_)
    dl, m2, v2 = adamw("adamw_small", *bufs)
    for n, a, b_, cc_ in zip(small, _unpack(dl, spec), _unpack(m2, spec), _unpack(v2, spec)):
        delta[n], new_m[n], new_v[n] = a, b_, cc_

    return (loss, grad_x, *[grads[n] for n in WEIGHTS], *[delta[n] for n in WEIGHTS], *[new_m[n] for n in WEIGHTS],
            *[new_v[n] for n in WEIGHTS])


def kernel(x, c, positions, ada_w, ada_b, norm_pre_mix, norm_post_mix, norm_pre_mlp, norm_post_mlp, w_in, w_out, gdn_conv_w, gdn_a_log, gdn_dt_bias, gdn_norm_w, ssm_conv_w, ssm_conv_b, ssm_a_log, ssm_dt_bias, ssm_d, ssm_norm_w, attn_sinks, sc_conv_w, w_up, w_down, loss_target, m_ada_w, m_ada_b, m_norm_pre_mix, m_norm_post_mix, m_norm_pre_mlp, m_norm_post_mlp, m_w_in, m_w_out, m_gdn_conv_w, m_gdn_a_log, m_gdn_dt_bias, m_gdn_norm_w, m_ssm_conv_w, m_ssm_conv_b, m_ssm_a_log, m_ssm_dt_bias, m_ssm_d, m_ssm_norm_w, m_attn_sinks, m_sc_conv_w, m_w_up, m_w_down, v_ada_w, v_ada_b, v_norm_pre_mix, v_norm_post_mix, v_norm_pre_mlp, v_norm_post_mlp, v_w_in, v_w_out, v_gdn_conv_w, v_gdn_a_log, v_gdn_dt_bias, v_gdn_norm_w, v_ssm_conv_w, v_ssm_conv_b, v_ssm_a_log, v_ssm_dt_bias, v_ssm_d, v_ssm_norm_w, v_attn_sinks, v_sc_conv_w, v_w_up, v_w_down):
    loc = locals()
    p = {n: loc[n] for n in WEIGHTS}
    m = {n: loc["m_" + n] for n in WEIGHTS}
    v = {n: loc["v_" + n] for n in WEIGHTS}
    return _step(p, m, v, x, c, positions, loss_target)
```

```python
import functools

import numpy as np
import jax
import jax.numpy as jnp
from jax import lax
from jax.experimental import pallas as pl
from jax.experimental.pallas import tpu as pltpu

F32, BF16 = jnp.float32, jnp.bfloat16
MESH = pl.DeviceIdType.MESH

DEPTH = 2
GW = 512
N_GDN, GDN_CS = 4, 64
SSD_CS, ATT_W = 128, 128
EPS = 1e-6
ROPE_THETA = 10000.0
IN_WIDTH = 5904
B1, B2, LR, AEPS, WD, STEP = 0.9, 0.999, 0.001, 1e-8, 0.01, 10

VMEM_LIMIT = 52 * 1024 * 1024
LANE = 128

O_GQKV, O_CB, O_CC, O_CH, O_XBC, O_AQ, O_GZ, O_SZ, O_AK, O_AV, O_GBA, O_SDT = (
    0, 1536, 2048, 2560, 3072, 4096, 5120, 5632, 6144, 6400, 6656, 6784)
PW = 6912
YCAT = 2560


def _cparams(*sem):
    return pltpu.CompilerParams(dimension_semantics=sem or None, vmem_limit_bytes=VMEM_LIMIT)


def _pick(n, cap, mult):
    if n <= cap:
        return n
    best = None
    for d in range(mult, cap + 1, mult):
        if n % d == 0:
            best = d
    assert best is not None, (n, cap, mult)
    return best


def _pieces(x, n):
    out, r = [], x
    for i in range(n):
        h = r.astype(BF16)
        out.append(h)
        if i < n - 1:
            r = r - h.astype(F32)
    return out


def _dgp(a, b, ca, cb, na, nb):
    dn = (((ca,), (cb,)), ((), ()))
    acc = None
    for i, pa in enumerate(_pieces(a, na)):
        for j, pb in enumerate(_pieces(b, nb)):
            if i + j < max(na, nb):
                t = lax.dot_general(pa, pb, dn, preferred_element_type=F32)
                acc = t if acc is None else acc + t
    return acc


_CONTRACT = {"nn": (1, 0), "nt": (1, 1), "tn": (0, 0)}


@functools.lru_cache(maxsize=None)
def _mm(kind, na, nb, const_a=False):
    ca, cb = _CONTRACT[kind]
    ng = max(na, nb)

    @jax.custom_vjp
    def f(a, b):
        return _dgp(a, b, ca, cb, na, nb)

    def bwd(r, g):
        a, b = r
        if kind == "nn":
            da, db = (lambda: _mm("nt", ng, nb)(g, b)), _mm("tn", na, ng, const_a)(a, g)
        elif kind == "nt":
            da, db = (lambda: _mm("nn", ng, nb)(g, b)), _mm("tn", ng, na)(g, a)
        else:
            da, db = (lambda: _mm("nt", nb, ng)(b, g)), _mm("nn", na, ng, const_a)(a, g)
        return (jnp.zeros_like(a) if const_a else da()), db

    f.defvjp(lambda a, b: (f(a, b), (a, b)), bwd)
    return f


_nn, _nt, _tn = _mm("nn", 1, 1), _mm("nt", 1, 1), _mm("tn", 1, 1)
_hnn, _hnt, _htn = _mm("nn", 2, 2), _mm("nt", 2, 2), _mm("tn", 2, 2)
_xnn = _mm("nn", 1, 3, True)


def _silu(x):
    return x * jax.nn.sigmoid(x)


def _softplus(x):
    return jnp.maximum(x, 0.0) + jnp.log1p(jnp.exp(-jnp.abs(x)))


def _colsel(v, idx):
    lane = lax.broadcasted_iota(jnp.int32, v.shape, 1)
    return jnp.sum(jnp.where(lane == idx, v, 0.0), axis=1, keepdims=True)


def _tri(n):
    r = lax.broadcasted_iota(jnp.int32, (n, n), 0)
    c = lax.broadcasted_iota(jnp.int32, (n, n), 1)
    return r >= c, r > c


def _rowsel(v, idx):
    row = lax.broadcasted_iota(jnp.int32, v.shape, 0)
    return jnp.sum(jnp.where(row == idx, v, 0.0), axis=0, keepdims=True)


def _tri_inv(m, size, nil):
    r = lax.broadcasted_iota(jnp.int32, (size, size), 0)
    c = lax.broadcasted_iota(jnp.int32, (size, size), 1)
    x = jnp.where(r == c, 1.0, 0.0) - m
    p = _hnn(m, m)
    steps = int(np.log2(nil)) - 1
    for s in range(steps):
        x = x + _hnn(x, p)
        if s < steps - 1:
            p = _hnn(p, p)
    return x


@jax.custom_vjp
def _inv_given(m, tinv):
    return tinv


_inv_given.defvjp(lambda m, tinv: (tinv, tinv),
                  lambda tinv, g: (-_hnt(_htn(tinv, g), tinv), jnp.zeros_like(tinv)))


def _gdn_chunk(cq, z, ba, s, alog, dtb, nw, tinv_saved=None):
    n, nh = GDN_CS, N_GDN
    rr = n * nh

    def stack(a, off):
        return jnp.concatenate([a[:, off + h * LANE:off + (h + 1) * LANE] for h in range(nh)], axis=0)

    def percol(f):
        return jnp.concatenate([f(h) for h in range(nh)], axis=0)

    def own(a):
        return jnp.concatenate([a[h * n:(h + 1) * n, h * LANE:(h + 1) * LANE] for h in range(nh)], axis=0)

    q, k, v = _silu(stack(cq, 0)), _silu(stack(cq, GW)), _silu(stack(cq, 2 * GW))
    q = q * lax.rsqrt(jnp.sum(q * q, -1, keepdims=True) + EPS) * (128 ** -0.5)
    k = k * lax.rsqrt(jnp.sum(k * k, -1, keepdims=True) + EPS)
    beta = jax.nn.sigmoid(percol(lambda h: _colsel(ba, h)))
    g = percol(lambda h: -jnp.exp(_colsel(alog, h)) * _softplus(_colsel(ba, nh + h) + _colsel(dtb, h)))
    ri = lax.broadcasted_iota(jnp.int32, (rr, rr), 0)
    ci = lax.broadcasted_iota(jnp.int32, (rr, rr), 1)
    same = (ri >> 6) == (ci >> 6)
    incl, strict = same & (ri >= ci), same & (ri > ci)
    gcb = _xnn(incl.astype(F32), jnp.broadcast_to(g, (rr, LANE)))
    gci = jnp.concatenate([gcb, gcb], axis=1)
    dec = jnp.where(incl, jnp.exp(jnp.where(incl, gci - gci.T, 0.0)), 0.0)
    kb = k * beta
    kk = _nt(jnp.concatenate([kb, q], axis=0), k)
    m = jnp.where(strict, kk[:rr] * dec, 0.0)
    attn = jnp.where(incl, kk[rr:] * dec, 0.0)
    tinv = _tri_inv(m, rr, n) if tinv_saved is None else _inv_given(m, tinv_saved)
    egc = jnp.exp(gcb)
    uw = _nn(tinv, jnp.concatenate([v * beta, kb * egc], axis=1))
    u, w = uw[:, :LANE], uw[:, LANE:]
    gl = [jnp.sum(g[h * n:(h + 1) * n], axis=0, keepdims=True) for h in range(nh)]
    kd = k * jnp.exp(percol(lambda h: jnp.broadcast_to(gl[h], (n, 1))) - gcb)
    ws = _nn(jnp.concatenate([w, q * egc], axis=0), s)
    v_new = u - own(ws[:rr])
    o = own(ws[rr:]) + _nn(attn, v_new)
    rowh = lax.broadcasted_iota(jnp.int32, (rr, LANE), 0) >> 6
    vexp = jnp.concatenate([jnp.where(rowh == h, v_new, 0.0) for h in range(nh)], axis=1)
    cd = jnp.concatenate([jnp.broadcast_to(jnp.exp(gl[h]), (1, LANE)) for h in range(nh)], axis=1)
    s_new = s * cd + _tn(kd, vexp)
    y = o * lax.rsqrt(jnp.mean(o * o, -1, keepdims=True) + EPS) * nw * _silu(stack(z, 0))
    y = jnp.concatenate([y[h * n:(h + 1) * n] for h in range(nh)], axis=1)
    return (y, s_new, tinv) if tinv_saved is None else (y, s_new)


def _ssd_chunk(cx, z, dtc, s, convb, alog, dtb, dsk, nw):
    n = SSD_CS
    incl, _ = _tri(n)
    xbc = _silu(cx + convb)
    dt_all = _softplus(dtc + dtb)
    da_all = dt_all * (-jnp.exp(alog))
    cs_all = _xnn(incl.astype(F32), da_all)
    cs_t = cs_all.T
    tot_all = jnp.sum(da_all, axis=0, keepdims=True)
    lane_lo = lax.broadcasted_iota(jnp.int32, (n, LANE), 1) < 64
    ys, s_out = [], []
    for pi in range(4):
        g = pi // 2
        bm = xbc[:, GW + g * LANE:GW + (g + 1) * LANE]
        cm = xbc[:, GW + 256 + g * LANE:GW + 256 + (g + 1) * LANE]
        cbm = _nt(cm, bm)
        xp, s_p = xbc[:, pi * LANE:(pi + 1) * LANE], s[:, pi * LANE:(pi + 1) * LANE]
        a_rows, b_cols, dts, dsks, etot = [], [], [], [], []
        for h in (2 * pi, 2 * pi + 1):
            cs_i = _colsel(cs_all, h)
            lm = jnp.where(incl, jnp.exp(jnp.where(incl, cs_i - _rowsel(cs_t, h), 0.0)), 0.0)
            tot = _colsel(tot_all, h)
            a_rows.append(jnp.concatenate([cbm * lm, cm * jnp.exp(cs_i)], axis=1))
            b_cols.append(bm * jnp.exp(tot - cs_i))
            dts.append(_colsel(dt_all, h))
            dsks.append(_colsel(dsk, h))
            etot.append(jnp.exp(tot))
        xdt = xp * jnp.where(lane_lo, dts[0], dts[1])
        yy = _nn(jnp.concatenate(a_rows, axis=0), jnp.concatenate([xdt, s_p], axis=0))
        ys.append(jnp.where(lane_lo, yy[:n], yy[n:]) + xp * jnp.where(lane_lo[:1], dsks[0], dsks[1]))
        st = _tn(jnp.concatenate(b_cols, axis=1), xdt)
        s_out.append(s_p * jnp.where(lane_lo[:1], etot[0], etot[1]) + jnp.where(lane_lo, st[:n], st[n:]))
    zs = _silu(z)
    out = []
    for g in range(2):
        y = jnp.concatenate(ys[2 * g:2 * g + 2], axis=1) * zs[:, g * 256:(g + 1) * 256]
        out.append(y * lax.rsqrt(jnp.mean(y * y, -1, keepdims=True) + EPS) * nw[:, g * 256:(g + 1) * 256])
    return jnp.concatenate(out, axis=1), jnp.concatenate(s_out, axis=1)


SWA_STACK = 1


def _swa_block(qg, kp, kc, vp, vc, sink, notfirst, *, g):
    w, ns = ATT_W, SWA_STACK
    kb = jnp.concatenate([kp, kc], axis=0)
    vb = jnp.concatenate([vp, vc], axis=0)
    qi = lax.broadcasted_iota(jnp.int32, (ns * w, 2 * w), 0) & (w - 1)
    kj = lax.broadcasted_iota(jnp.int32, (ns * w, 2 * w), 1)
    rel = qi + w - kj
    mask = (rel >= 0) & (rel < w) & ((kj >= w) | (notfirst > 0.5))
    outs = []
    for r0 in range(0, 4, ns):
        q = jnp.concatenate([qg[:, r * LANE:(r + 1) * LANE] for r in range(r0, r0 + ns)], axis=0)
        snk = jnp.concatenate([jnp.broadcast_to(_colsel(sink, 4 * g + r), (w, 1)) for r in range(r0, r0 + ns)], axis=0)
        s = jnp.where(mask, _nt(q, kb) * (64 ** -0.5), -1e30)
        mx = lax.stop_gradient(jnp.maximum(jnp.max(s, axis=-1, keepdims=True), snk))
        p = jnp.where(mask, jnp.exp(s - mx), 0.0)
        p = p / (jnp.sum(p, axis=-1, keepdims=True) + jnp.exp(snk - mx))
        o = _nn(p, vb)
        outs += [o[j * w:(j + 1) * w] for j in range(ns)]
    return jnp.concatenate(outs, axis=1)


def _rope(x, cos, sin):
    return x * cos + pltpu.roll(x, 64, 1) * sin


def _unrope(dy, cos, sin):
    return dy * cos + pltpu.roll(dy * sin, 64, 1)


def rowmap(name, fn, rows, consts, outs, accs=(), tile=256):
    t = rows[0][0].shape[0]
    tile = min(tile, t)
    assert t % tile == 0
    nr, nc, no, na = len(rows), len(consts), len(outs), len(accs)

    def body(*refs):
        rv = [r[...] for r in refs[:nr]]
        cv = [r[...] for r in refs[nr:nr + nc]]
        ov, av = fn(rv, cv)
        for ref, o in zip(refs[nr + nc:nr + nc + no], ov):
            ref[...] = o.astype(ref.dtype)
        if na:
            a_refs = refs[nr + nc + no:]

            @pl.when(pl.program_id(0) == 0)
            def _():
                for ref in a_refs:
                    ref[...] = jnp.zeros_like(ref)

            for ref, a in zip(a_refs, av):
                ref[...] += a

    def cspec(off, width):
        assert off % width == 0
        cb = off // width
        return pl.BlockSpec((tile, width), lambda i: (i, cb))

    in_specs = [cspec(off, wd) for (_, off, wd) in rows]
    in_specs += [pl.BlockSpec(c.shape, lambda i: (0, 0)) for c in consts]
    out_specs = [pl.BlockSpec((tile, wd), lambda i: (i, 0)) for (wd, _) in outs]
    out_specs += [pl.BlockSpec(s, lambda i: (0, 0)) for s in accs]
    out_shape = [jax.ShapeDtypeStruct((t, wd), dt) for (wd, dt) in outs]
    out_shape += [jax.ShapeDtypeStruct(s, F32) for s in accs]
    res = pl.pallas_call(
        body, name=name, grid=(t // tile,), in_specs=in_specs, out_specs=out_specs, out_shape=out_shape,
        compiler_params=_cparams("arbitrary"),
    )(*[r[0] for r in rows], *consts)
    return res[:no], res[no:]


def rowmap_vjp(name, f, rows, consts, douts, n_diff, drow_dtypes, tile=256, add_rows=()):
    nr, nd, nadd = len(rows), len(douts), len(add_rows)

    def g(vals, cvals):
        rv, dov, addv = vals[:nr], vals[nr:nr + nd], vals[nr + nd:]
        fixed = rv[n_diff:]
        _, vjp = jax.vjp(lambda r, c: f(list(r) + list(fixed), c), rv[:n_diff], cvals)
        dr, dc = vjp([d.astype(F32) for d in dov])
        dr = list(dr)
        for a in addv:
            dr[0] = dr[0] + a
        return dr, dc

    outs = [(rows[i][2], drow_dtypes[i]) for i in range(n_diff)]
    accs = [c.shape for c in consts]
    return rowmap(name, g, list(rows) + list(douts) + list(add_rows), consts, outs, accs, tile=tile)


def _carried_copies(kind, in_refs, out_refs, send_sems, recv_sems):
    x, y, c = _me()
    chip = 2 * x + y
    chips = [(1 - x, y), (x, 1 - y), (1 - x, 1 - y)]
    cps = []

    def rc(src, dst, k, to):
        return pltpu.make_async_remote_copy(src_ref=src, dst_ref=dst, send_sem=send_sems.at[k], recv_sem=recv_sems.at[k],
                                            device_id=to, device_id_type=MESH)

    for i, (src, dst) in enumerate(zip(in_refs, out_refs)):
        if kind == "gather":
            rh = src.shape[0] // 2
            half = pl.ds(pl.multiple_of(c * rh, 16), rh)
            cps.append(rc(src, dst.at[chip], 4 * i + 3, (x, y, 1 - c)))
            cps += [rc(src.at[half], dst.at[chip, half], 4 * i + k, (cx, cy, c)) for k, (cx, cy) in enumerate(chips)]
        else:
            cps += [rc(src.at[2 * cx + cy], dst.at[chip], 4 * i + k, (cx, cy, c)) for k, (cx, cy) in enumerate(chips)]
    return cps


def mm_nn(name, a, w, out_dtype, a_fn=None, extra=None, epi=None, tm_cap=1024, wt=False, carry=None):
    nc = len(carry[1]) if carry else 0
    m, k = a.shape
    stacked = w.ndim == 3
    n = (w.shape[-2] if wt else w.shape[-1] * (w.shape[0] if stacked else 1))
    slab = w.shape[-1]
    tm = _pick(m, tm_cap, 16)
    tn = _pick(n if wt else slab, 1024, LANE)
    tk = _pick(slab if wt else k, 2048, LANE)
    nk = k // tk
    per = slab // (tk if wt else tn)
    wdims = ((1,), (1,)) if wt else ((1,), (0,))

    gi, gj = m // tm, n // tn
    n_in = 2 + (extra is not None)

    def body(*refs):
        a_ref, w_ref = refs[0], refs[1]
        e_ref = refs[2] if extra is not None else None
        o_ref, acc = refs[n_in + nc], refs[n_in + 2 * nc + 1]
        kk = pl.program_id(2)
        if nc:
            def copies():
                return _carried_copies(carry[0], refs[n_in:n_in + nc], refs[n_in + nc + 1:n_in + 2 * nc + 1],
                                       refs[-2], refs[-1])

            @pl.when((pl.program_id(0) == 0) & (pl.program_id(1) == 0) & (kk == 0))
            def _():
                for cp in copies():
                    cp.start()

        @pl.when(kk == 0)
        def _():
            acc[...] = jnp.zeros_like(acc)

        av = a_ref[...]
        if a_fn is not None:
            av = a_fn(av)
        acc[...] += lax.dot_general(av.astype(BF16), w_ref[...].astype(BF16), (wdims, ((), ())),
                                    preferred_element_type=F32)

        @pl.when(kk == nk - 1)
        def _():
            r = acc[...]
            if epi is not None:
                r = epi(r, e_ref[...])
            o_ref[...] = r.astype(o_ref.dtype)

        if nc:
            @pl.when((pl.program_id(0) == gi - 1) & (pl.program_id(1) == gj - 1) & (kk == nk - 1))
            def _():
                cps = copies()
                for cp in cps:
                    cp.wait_recv()
                for cp in cps:
                    cp.wait_send()

    if wt:
        w_spec = (pl.BlockSpec((None, tn, tk), lambda i, j, kk: (kk // per, j, kk % per)) if stacked
                  else pl.BlockSpec((tn, tk), lambda i, j, kk: (j, kk)))
    else:
        w_spec = (pl.BlockSpec((None, tk, tn), lambda i, j, kk: (j // per, kk, j % per)) if stacked
                  else pl.BlockSpec((tk, tn), lambda i, j, kk: (kk, j)))
    in_specs = [pl.BlockSpec((tm, tk), lambda i, j, kk: (i, kk)), w_spec]
    ops = [a, w]
    if extra is not None:
        in_specs.append(pl.BlockSpec((tm, tn), lambda i, j, kk: (i, j)))
        ops.append(extra)
    out_spec = pl.BlockSpec((tm, tn), lambda i, j, kk: (i, j))
    out_shape = jax.ShapeDtypeStruct((m, n), out_dtype)
    if not nc:
        return pl.pallas_call(
            body, name=name, grid=(gi, gj, nk), in_specs=in_specs, out_specs=out_spec, out_shape=out_shape,
            scratch_shapes=[pltpu.VMEM((tm, tn), F32)],
            compiler_params=_cparams("parallel", "parallel", "arbitrary"),
        )(*ops)
    landing = [jax.ShapeDtypeStruct((4,) + c.shape if carry[0] == "gather" else c.shape, c.dtype) for c in carry[1]]
    res = pl.pallas_call(
        body, name=name, grid=(gi, gj, nk), in_specs=in_specs + [ANY] * nc, out_specs=[out_spec] + [ANY] * nc,
        out_shape=[out_shape] + landing, scratch_shapes=[pltpu.VMEM((tm, tn), F32)] + _sems(4 * nc),
        compiler_params=_cparams("arbitrary", "arbitrary", "arbitrary"),
    )(*ops, *carry[1])
    return res[0], list(res[1:])


def mm_tn(name, a, b, a_fn=None, col_slabs=1):
    t, m = a.shape
    n = b.shape[1]
    tm, tn, tk = _pick(m, 1024, LANE), _pick(n // col_slabs, 1024, LANE), _pick(t, 2048, 16)
    nk = t // tk
    per = n // col_slabs // tn

    def body(a_ref, b_ref, o_ref, acc):
        kk = pl.program_id(2)

        @pl.when(kk == 0)
        def _():
            acc[...] = jnp.zeros_like(acc)

        av = a_ref[...]
        if a_fn is not None:
            av = a_fn(av)
        acc[...] += lax.dot_general(av.astype(BF16), b_ref[...].astype(BF16), (((0,), (0,)), ((), ())),
                                    preferred_element_type=F32)

        @pl.when(kk == nk - 1)
        def _():
            o_ref[...] = acc[...]

    return pl.pallas_call(
        body, name=name, grid=(m // tm, n // tn, nk),
        in_specs=[pl.BlockSpec((tk, tm), lambda i, j, kk: (kk, i)), pl.BlockSpec((tk, tn), lambda i, j, kk: (kk, j))],
        out_specs=(pl.BlockSpec((tm, tn), lambda i, j, kk: (i, j)) if col_slabs == 1
                   else pl.BlockSpec((None, tm, tn), lambda i, j, kk: (j // per, i, j % per))),
        out_shape=jax.ShapeDtypeStruct((m, n) if col_slabs == 1 else (col_slabs, m, n // col_slabs), F32),
        scratch_shapes=[pltpu.VMEM((tm, tn), F32)],
        compiler_params=_cparams("parallel", "parallel", "arbitrary"),
    )(a, b)


CONV_TILE = 256
HALO = 8


def conv_fwd(name, xs, w):
    t = xs[0][0].shape[0]
    c = xs[0][2]
    kw = w.shape[0]
    tt = min(CONV_TILE, t)
    nx = len(xs)

    def body(*refs):
        cur_refs, prev_refs, w_ref, y_ref = refs[:nx], refs[nx:2 * nx], refs[2 * nx], refs[2 * nx + 1]
        i = pl.program_id(0)
        cur, prev = cur_refs[0][...], prev_refs[0][...]
        for r, p in zip(cur_refs[1:], prev_refs[1:]):
            cur, prev = cur * r[...], prev * p[...]
        prev = jnp.where(i > 0, prev, 0.0)
        xx = jnp.concatenate([prev, cur], axis=0)
        acc = jnp.zeros((tt, c), F32)
        for k in range(kw):
            sh = kw - 1 - k
            rolled = pltpu.roll(xx, sh, 0) if sh else xx
            acc = acc + rolled[HALO:] * w_ref[k:k + 1, :]
        y_ref[...] = acc

    def cur_spec(off):
        cb = off // c
        return pl.BlockSpec((tt, c), lambda i: (i, cb))

    def prev_spec(off):
        cb = off // c
        return pl.BlockSpec((HALO, c), lambda i: (jnp.maximum(i * (tt // HALO) - 1, 0), cb))

    for (_, off, wd) in xs:
        assert wd == c and off % c == 0
    return pl.pallas_call(
        body, name=name, grid=(t // tt,),
        in_specs=[cur_spec(off) for (_, off, _) in xs] + [prev_spec(off) for (_, off, _) in xs]
        + [pl.BlockSpec(w.shape, lambda i: (0, 0))],
        out_specs=pl.BlockSpec((tt, c), lambda i: (i, 0)),
        out_shape=jax.ShapeDtypeStruct((t, c), F32),
        compiler_params=_cparams("arbitrary"),
    )(*[x[0] for x in xs], *[x[0] for x in xs], w)


def conv_bwd(name, dy, xs, w, dx_dtype):
    t = xs[0][0].shape[0]
    c = xs[0][2]
    kw = w.shape[0]
    tt = min(CONV_TILE, t)
    nt = t // tt
    nx = len(xs)
    n = tt + HALO

    def body(*refs):
        dy_ref = refs[0]
        cur_refs, prev_refs = refs[1:1 + nx], refs[1 + nx:1 + 2 * nx]
        w_ref = refs[1 + 2 * nx]
        dx_refs = refs[2 + 2 * nx:2 + 3 * nx]
        dw_ref, carry = refs[2 + 3 * nx], refs[3 + 3 * nx]
        i = pl.program_id(0)

        @pl.when(i == 0)
        def _():
            carry[...] = jnp.zeros_like(carry)
            dw_ref[...] = jnp.zeros_like(dw_ref)

        curs = [r[...] for r in cur_refs]
        cur, prev = curs[0], prev_refs[0][...]
        for cv, p in zip(curs[1:], prev_refs[1:]):
            cur, prev = cur * cv, prev * p[...]
        prev = jnp.where(i < nt - 1, prev, 0.0)
        xx = jnp.concatenate([prev, cur], axis=0)
        dyv = dy_ref[...]
        dpad = jnp.concatenate([jnp.zeros((HALO, c), F32), dyv], axis=0)
        dxx = jnp.zeros((n, c), F32)
        dws = []
        for k in range(kw):
            sh = kw - 1 - k
            dxx = dxx + (pltpu.roll(dpad, n - sh, 0) if sh else dpad) * w_ref[k:k + 1, :]
            rolled = pltpu.roll(xx, sh, 0) if sh else xx
            dws.append(jnp.sum(dyv * rolled[HALO:], axis=0, keepdims=True))
        dw_ref[...] += jnp.concatenate(dws, axis=0)
        dcur = jnp.concatenate([dxx[HALO:tt], dxx[tt:] + carry[...]], axis=0)
        carry[...] = dxx[:HALO]
        for j, ref in enumerate(dx_refs):
            d = dcur
            for jj, cv in enumerate(curs):
                if jj != j:
                    d = d * cv
            ref[...] = d.astype(ref.dtype)

    def cur_spec(off):
        cb = off // c
        return pl.BlockSpec((tt, c), lambda i: (nt - 1 - i, cb))

    def prev_spec(off):
        cb = off // c
        return pl.BlockSpec((HALO, c), lambda i: (jnp.maximum((nt - 1 - i) * (tt // HALO) - 1, 0), cb))

    res = pl.pallas_call(
        body, name=name, grid=(nt,),
        in_specs=[pl.BlockSpec((tt, c), lambda i: (nt - 1 - i, 0))] + [cur_spec(off) for (_, off, _) in xs]
        + [prev_spec(off) for (_, off, _) in xs] + [pl.BlockSpec(w.shape, lambda i: (0, 0))],
        out_specs=[pl.BlockSpec((tt, c), lambda i: (nt - 1 - i, 0)) for _ in xs] + [pl.BlockSpec(w.shape, lambda i: (0, 0))],
        out_shape=[jax.ShapeDtypeStruct((t, c), dx_dtype) for _ in xs] + [jax.ShapeDtypeStruct(w.shape, F32)],
        scratch_shapes=[pltpu.VMEM((HALO, c), F32)],
        compiler_params=_cparams("arbitrary"),
    )(dy, *[x[0] for x in xs], *[x[0] for x in xs], w)
    return res[:nx], res[nx]


def _const_spec(a):
    return pl.BlockSpec(a.shape, lambda i: (0,) * a.ndim)


GDN_STEP = 2


def _gdn_rows(ref, c):
    return ref[c * GDN_CS:(c + 1) * GDN_CS, :]


def gdn_fwd(name, cq, proj, alog, dtb, nw):
    t = cq.shape[0]
    nc = t // GDN_CS
    ns = min(GDN_STEP, nc)
    rows = ns * GDN_CS

    def body(cq_ref, z_ref, ba_ref, alog_ref, dtb_ref, nw_ref, y_ref, ssave_ref, tsave_ref, s_scr):
        @pl.when(pl.program_id(0) == 0)
        def _():
            s_scr[...] = jnp.zeros_like(s_scr)

        s = s_scr[...]
        ys, saved, tinvs = [], [], []
        for c in range(ns):
            saved.append(s)
            y, s, tinv = _gdn_chunk(_gdn_rows(cq_ref, c), _gdn_rows(z_ref, c), _gdn_rows(ba_ref, c), s, alog_ref[...],
                                    dtb_ref[...], nw_ref[...])
            ys.append(y)
            tinvs.append(tinv)
        y_ref[...] = jnp.concatenate(ys, axis=0)
        ssave_ref[...] = jnp.stack(saved, axis=0)
        tsave_ref[...] = jnp.stack(tinvs, axis=0)
        s_scr[...] = s

    return pl.pallas_call(
        body, name=name, grid=(nc // ns,),
        in_specs=[pl.BlockSpec((rows, 3 * GW), lambda i: (i, 0)),
                  pl.BlockSpec((rows, GW), lambda i: (i, O_GZ // GW)),
                  pl.BlockSpec((rows, LANE), lambda i: (i, O_GBA // LANE)),
                  _const_spec(alog), _const_spec(dtb), _const_spec(nw)],
        out_specs=[pl.BlockSpec((rows, GW), lambda i: (i, 0)),
                   pl.BlockSpec((ns, LANE, N_GDN * LANE), lambda i: (i, 0, 0)),
                   pl.BlockSpec((ns, N_GDN * GDN_CS, N_GDN * GDN_CS), lambda i: (i, 0, 0))],
        out_shape=[jax.ShapeDtypeStruct((t, GW), F32), jax.ShapeDtypeStruct((nc, LANE, N_GDN * LANE), F32),
                   jax.ShapeDtypeStruct((nc, N_GDN * GDN_CS, N_GDN * GDN_CS), F32)],
        scratch_shapes=[pltpu.VMEM((LANE, N_GDN * LANE), F32)],
        compiler_params=_cparams("arbitrary"),
    )(cq, proj, proj, alog, dtb, nw)


def gdn_bwd(name, dy, dy_off, cq, proj, ssave, tsave, alog, dtb, nw):
    t = cq.shape[0]
    nc = t // GDN_CS
    ns = min(GDN_STEP, nc)
    rows = ns * GDN_CS

    def body(dy_ref, cq_ref, z_ref, ba_ref, ssave_ref, tsave_ref, alog_ref, dtb_ref, nw_ref,
             dcq_ref, dz_ref, dba_ref, dalog_ref, ddtb_ref, dnw_ref, ds_scr):
        @pl.when(pl.program_id(0) == 0)
        def _():
            ds_scr[...] = jnp.zeros_like(ds_scr)
            dalog_ref[...] = jnp.zeros_like(dalog_ref)
            ddtb_ref[...] = jnp.zeros_like(ddtb_ref)
            dnw_ref[...] = jnp.zeros_like(dnw_ref)

        ds = ds_scr[...]
        dcq, dzs, dbas = [None] * ns, [None] * ns, [None] * ns
        dal, ddt, dnw = (jnp.zeros(r.shape, F32) for r in (dalog_ref, ddtb_ref, dnw_ref))
        for c in reversed(range(ns)):
            _, vjp = jax.vjp(functools.partial(_gdn_chunk, tinv_saved=tsave_ref[c]), _gdn_rows(cq_ref, c),
                             _gdn_rows(z_ref, c), _gdn_rows(ba_ref, c), ssave_ref[c], alog_ref[...], dtb_ref[...],
                             nw_ref[...])
            dcq[c], dzs[c], dbas[c], ds, dal_c, ddt_c, dnw_c = vjp((_gdn_rows(dy_ref, c), ds))
            dal, ddt, dnw = dal + dal_c, ddt + ddt_c, dnw + dnw_c
        dcq_ref[...] = jnp.concatenate(dcq, axis=0)
        dz_ref[...] = jnp.concatenate(dzs, axis=0).astype(dz_ref.dtype)
        dba_ref[...] = jnp.concatenate(dbas, axis=0).astype(dba_ref.dtype)
        ds_scr[...] = ds
        dalog_ref[...] += dal
        ddtb_ref[...] += ddt
        dnw_ref[...] += dnw

    rev = lambda i: nc // ns - 1 - i
    return pl.pallas_call(
        body, name=name, grid=(nc // ns,),
        in_specs=[pl.BlockSpec((rows, GW), lambda i: (rev(i), dy_off // GW)),
                  pl.BlockSpec((rows, 3 * GW), lambda i: (rev(i), 0)),
                  pl.BlockSpec((rows, GW), lambda i: (rev(i), O_GZ // GW)),
                  pl.BlockSpec((rows, LANE), lambda i: (rev(i), O_GBA // LANE)),
                  pl.BlockSpec((ns, LANE, N_GDN * LANE), lambda i: (rev(i), 0, 0)),
                  pl.BlockSpec((ns, N_GDN * GDN_CS, N_GDN * GDN_CS), lambda i: (rev(i), 0, 0)),
                  _const_spec(alog), _const_spec(dtb), _const_spec(nw)],
        out_specs=[pl.BlockSpec((rows, 3 * GW), lambda i: (rev(i), 0)),
                   pl.BlockSpec((rows, GW), lambda i: (rev(i), 0)),
                   pl.BlockSpec((rows, LANE), lambda i: (rev(i), 0)),
                   _const_spec(alog), _const_spec(dtb), _const_spec(nw)],
        out_shape=[jax.ShapeDtypeStruct((t, 3 * GW), F32), jax.ShapeDtypeStruct((t, GW), BF16),
                   jax.ShapeDtypeStruct((t, LANE), BF16), jax.ShapeDtypeStruct(alog.shape, F32),
                   jax.ShapeDtypeStruct(dtb.shape, F32), jax.ShapeDtypeStruct(nw.shape, F32)],
        scratch_shapes=[pltpu.VMEM((LANE, N_GDN * LANE), F32)],
        compiler_params=_cparams("arbitrary"),
    )(dy, cq, proj, proj, ssave, tsave, alog, dtb, nw)


def ssd_fwd(name, cx, proj, convb, alog, dtb, dsk, nw):
    t = cx.shape[0]
    nc = t // SSD_CS

    def body(cx_ref, z_ref, dt_ref, cb_ref, alog_ref, dtb_ref, dsk_ref, nw_ref, y_ref, ssave_ref, s_scr):
        @pl.when(pl.program_id(0) == 0)
        def _():
            s_scr[...] = jnp.zeros_like(s_scr)

        s_in = s_scr[...]
        y, s_new = _ssd_chunk(cx_ref[...], z_ref[...], dt_ref[...], s_in, cb_ref[...], alog_ref[...], dtb_ref[...],
                              dsk_ref[...], nw_ref[...])
        y_ref[...] = y
        ssave_ref[0] = s_in
        s_scr[...] = s_new

    return pl.pallas_call(
        body, name=name, grid=(nc,),
        in_specs=[pl.BlockSpec((SSD_CS, 1024), lambda i: (i, 0)),
                  pl.BlockSpec((SSD_CS, GW), lambda i: (i, O_SZ // GW)),
                  pl.BlockSpec((SSD_CS, LANE), lambda i: (i, O_SDT // LANE)),
                  _const_spec(convb), _const_spec(alog), _const_spec(dtb), _const_spec(dsk), _const_spec(nw)],
        out_specs=[pl.BlockSpec((SSD_CS, GW), lambda i: (i, 0)),
                   pl.BlockSpec((1, LANE, GW), lambda i: (i, 0, 0))],
        out_shape=[jax.ShapeDtypeStruct((t, GW), F32), jax.ShapeDtypeStruct((nc, LANE, GW), F32)],
        scratch_shapes=[pltpu.VMEM((LANE, GW), F32)],
        compiler_params=_cparams("arbitrary"),
    )(cx, proj, proj, convb, alog, dtb, dsk, nw)


def ssd_bwd(name, dy, dy_off, cx, proj, ssave, convb, alog, dtb, dsk, nw):
    t = cx.shape[0]
    nc = t // SSD_CS

    def body(dy_ref, cx_ref, z_ref, dt_ref, ssave_ref, cb_ref, alog_ref, dtb_ref, dsk_ref, nw_ref,
             dcx_ref, dz_ref, ddt_ref, dcb_ref, dalog_ref, ddtb_ref, ddsk_ref, dnw_ref, ds_scr):
        @pl.when(pl.program_id(0) == 0)
        def _():
            ds_scr[...] = jnp.zeros_like(ds_scr)
            for r in (dcb_ref, dalog_ref, ddtb_ref, ddsk_ref, dnw_ref):
                r[...] = jnp.zeros_like(r)

        _, vjp = jax.vjp(_ssd_chunk, cx_ref[...], z_ref[...], dt_ref[...], ssave_ref[0], cb_ref[...], alog_ref[...],
                         dtb_ref[...], dsk_ref[...], nw_ref[...])
        dcx, dz, ddt, ds, dcb, dal, ddtb, ddsk, dnw = vjp((dy_ref[...], ds_scr[...]))
        dcx_ref[...] = dcx
        dz_ref[...] = dz.astype(dz_ref.dtype)
        ddt_ref[...] = ddt.astype(ddt_ref.dtype)
        ds_scr[...] = ds
        dcb_ref[...] += dcb
        dalog_ref[...] += dal
        ddtb_ref[...] += ddtb
        ddsk_ref[...] += ddsk
        dnw_ref[...] += dnw

    rev = lambda i: nc - 1 - i
    return pl.pallas_call(
        body, name=name, grid=(nc,),
        in_specs=[pl.BlockSpec((SSD_CS, GW), lambda i: (rev(i), dy_off // GW)),
                  pl.BlockSpec((SSD_CS, 1024), lambda i: (rev(i), 0)),
                  pl.BlockSpec((SSD_CS, GW), lambda i: (rev(i), O_SZ // GW)),
                  pl.BlockSpec((SSD_CS, LANE), lambda i: (rev(i), O_SDT // LANE)),
                  pl.BlockSpec((1, LANE, GW), lambda i: (rev(i), 0, 0)),
                  _const_spec(convb), _const_spec(alog), _const_spec(dtb), _const_spec(dsk), _const_spec(nw)],
        out_specs=[pl.BlockSpec((SSD_CS, 1024), lambda i: (rev(i), 0)),
                   pl.BlockSpec((SSD_CS, GW), lambda i: (rev(i), 0)),
                   pl.BlockSpec((SSD_CS, LANE), lambda i: (rev(i), 0)),
                   _const_spec(convb), _const_spec(alog), _const_spec(dtb), _const_spec(dsk), _const_spec(nw)],
        out_shape=[jax.ShapeDtypeStruct((t, 1024), F32), jax.ShapeDtypeStruct((t, GW), BF16),
                   jax.ShapeDtypeStruct((t, LANE), BF16), jax.ShapeDtypeStruct(convb.shape, F32),
                   jax.ShapeDtypeStruct(alog.shape, F32), jax.ShapeDtypeStruct(dtb.shape, F32),
                   jax.ShapeDtypeStruct(dsk.shape, F32), jax.ShapeDtypeStruct(nw.shape, F32)],
        scratch_shapes=[pltpu.VMEM((LANE, GW), F32)],
        compiler_params=_cparams("arbitrary"),
    )(dy, cx, proj, proj, ssave, convb, alog, dtb, dsk, nw)


def _swa_specs(nb, rev):
    w = ATT_W
    cur = lambda i: (nb - 1 - i) if rev else i
    prv = lambda i: jnp.maximum(cur(i) - 1, 0)
    return [pl.BlockSpec((w, 1024), lambda i: (cur(i), O_AQ // 1024)),
            pl.BlockSpec((w, 256), lambda i: (cur(i), O_AK // 256)),
            pl.BlockSpec((w, 256), lambda i: (prv(i), O_AK // 256)),
            pl.BlockSpec((w, 256), lambda i: (cur(i), O_AV // 256)),
            pl.BlockSpec((w, 256), lambda i: (prv(i), O_AV // 256)),
            pl.BlockSpec((w, LANE), lambda i: (cur(i), 0)), pl.BlockSpec((w, LANE), lambda i: (prv(i), 0)),
            pl.BlockSpec((w, LANE), lambda i: (cur(i), 0)), pl.BlockSpec((w, LANE), lambda i: (prv(i), 0))]


def swa_fwd(name, proj, cos, sin, sinks):
    t = proj.shape[0]
    nb = t // ATT_W

    def body(q_ref, kc_ref, kp_ref, vc_ref, vp_ref, cc_ref, cp_ref, sc_ref, sp_ref, snk_ref, y_ref):
        notfirst = jnp.where(pl.program_id(0) > 0, 1.0, 0.0).astype(F32)
        cc, cp, sc, sp = cc_ref[...], cp_ref[...], sc_ref[...], sp_ref[...]
        outs = []
        for g in range(2):
            sl = slice(g * LANE, (g + 1) * LANE)
            qg = jnp.concatenate([_rope(q_ref[:, (4 * g + r) * LANE:(4 * g + r + 1) * LANE], cc, sc)
                                  for r in range(4)], axis=1)
            outs.append(_swa_block(qg, _rope(kp_ref[:, sl], cp, sp), _rope(kc_ref[:, sl], cc, sc), vp_ref[:, sl],
                                   vc_ref[:, sl], snk_ref[...], notfirst, g=g))
        y_ref[...] = jnp.concatenate(outs, axis=1)

    return pl.pallas_call(
        body, name=name, grid=(nb,),
        in_specs=_swa_specs(nb, False) + [_const_spec(sinks)],
        out_specs=pl.BlockSpec((ATT_W, 1024), lambda i: (i, 0)),
        out_shape=jax.ShapeDtypeStruct((t, 1024), F32),
        compiler_params=_cparams("arbitrary"),
    )(proj, proj, proj, proj, proj, cos, cos, sin, sin, sinks)


def swa_bwd(name, dy, dy_off, proj, cos, sin, sinks):
    t = proj.shape[0]
    nb = t // ATT_W

    def body(dy_ref, q_ref, kc_ref, kp_ref, vc_ref, vp_ref, cc_ref, cp_ref, sc_ref, sp_ref, snk_ref,
             dq_ref, dk_ref, dv_ref, dsnk_ref, ck_scr, cv_scr):
        i = pl.program_id(0)

        @pl.when(i == 0)
        def _():
            ck_scr[...] = jnp.zeros_like(ck_scr)
            cv_scr[...] = jnp.zeros_like(cv_scr)
            dsnk_ref[...] = jnp.zeros_like(dsnk_ref)

        notfirst = jnp.where(i < nb - 1, 1.0, 0.0).astype(F32)
        cc, cp, sc, sp = cc_ref[...], cp_ref[...], sc_ref[...], sp_ref[...]
        ck_in, cv_in = ck_scr[...], cv_scr[...]
        dq, dk, dv, ck, cv, dsnk = [], [], [], [], [], jnp.zeros(dsnk_ref.shape, F32)
        for g in range(2):
            sl = slice(g * LANE, (g + 1) * LANE)
            qg = jnp.concatenate([_rope(q_ref[:, (4 * g + r) * LANE:(4 * g + r + 1) * LANE], cc, sc)
                                  for r in range(4)], axis=1)
            _, vjp = jax.vjp(functools.partial(_swa_block, g=g), qg, _rope(kp_ref[:, sl], cp, sp),
                             _rope(kc_ref[:, sl], cc, sc), vp_ref[:, sl], vc_ref[:, sl], snk_ref[...], notfirst)
            dqg, dkp, dkc, dvp, dvc, dsnk_g, _ = vjp(dy_ref[:, g * GW:(g + 1) * GW])
            dq += [_unrope(dqg[:, r * LANE:(r + 1) * LANE], cc, sc) for r in range(4)]
            dk.append(_unrope(dkc, cc, sc) + ck_in[:, sl])
            dv.append(dvc + cv_in[:, sl])
            ck.append(_unrope(dkp, cp, sp))
            cv.append(dvp)
            dsnk = dsnk + dsnk_g
        dq_ref[...] = jnp.concatenate(dq, axis=1).astype(dq_ref.dtype)
        dk_ref[...] = jnp.concatenate(dk, axis=1).astype(dk_ref.dtype)
        dv_ref[...] = jnp.concatenate(dv, axis=1).astype(dv_ref.dtype)
        ck_scr[...] = jnp.concatenate(ck, axis=1)
        cv_scr[...] = jnp.concatenate(cv, axis=1)
        dsnk_ref[...] += dsnk

    rev = lambda i: nb - 1 - i
    return pl.pallas_call(
        body, name=name, grid=(nb,),
        in_specs=[pl.BlockSpec((ATT_W, 1024), lambda i: (rev(i), dy_off // 1024))] + _swa_specs(nb, True)
        + [_const_spec(sinks)],
        out_specs=[pl.BlockSpec((ATT_W, 1024), lambda i: (rev(i), 0)),
                   pl.BlockSpec((ATT_W, 256), lambda i: (rev(i), 0)),
                   pl.BlockSpec((ATT_W, 256), lambda i: (rev(i), 0)), _const_spec(sinks)],
        out_shape=[jax.ShapeDtypeStruct((t, 1024), BF16), jax.ShapeDtypeStruct((t, 256), BF16),
                   jax.ShapeDtypeStruct((t, 256), BF16), jax.ShapeDtypeStruct(sinks.shape, F32)],
        scratch_shapes=[pltpu.VMEM((ATT_W, 256), F32), pltpu.VMEM((ATT_W, 256), F32)],
        compiler_params=_cparams("arbitrary"),
    )(dy, proj, proj, proj, proj, proj, cos, cos, sin, sin, sinks)


def _me():
    return lax.axis_index("x"), lax.axis_index("y"), lax.axis_index("c")


ANY = pl.BlockSpec(memory_space=pl.ANY)


def allgather8(name, buf):
    r = buf.shape[0]

    def body(x_ref, out_ref, send_sems, recv_sems, local_sem):
        x, y, c = _me()
        me = 4 * x + 2 * y + c
        mine = pltpu.make_async_copy(x_ref, out_ref.at[me], local_sem)
        mine.start()
        copies = []
        for k in range(1, 8):
            bx, by, bc = (k >> 2) & 1, (k >> 1) & 1, k & 1
            to = ((1 - x) if bx else x, (1 - y) if by else y, (1 - c) if bc else c)
            cp = pltpu.make_async_remote_copy(src_ref=x_ref, dst_ref=out_ref.at[me], send_sem=send_sems.at[k - 1],
                                              recv_sem=recv_sems.at[k - 1], device_id=to, device_id_type=MESH)
            cp.start()
            copies.append(cp)
        for cp in copies:
            cp.wait_recv()
        for cp in copies:
            cp.wait_send()
        mine.wait()

    return pl.pallas_call(
        body, name=name, out_shape=jax.ShapeDtypeStruct((8, r, LANE), buf.dtype),
        in_specs=[pl.BlockSpec(memory_space=pltpu.VMEM)], out_specs=pl.BlockSpec(memory_space=pltpu.VMEM),
        scratch_shapes=[pltpu.SemaphoreType.DMA((7,)), pltpu.SemaphoreType.DMA((7,)), pltpu.SemaphoreType.DMA],
        compiler_params=pltpu.CompilerParams(vmem_limit_bytes=VMEM_LIMIT),
    )(buf)


def _sems(n):
    return [pltpu.SemaphoreType.DMA((n,)), pltpu.SemaphoreType.DMA((n,))]


def allgather_chips(name, shards):
    n = len(shards)

    def body(*refs):
        x_refs, out_refs, send_sems, recv_sems = refs[:n], refs[n:2 * n], refs[2 * n], refs[2 * n + 1]
        x, y, c = _me()
        chip = 2 * x + y
        chips = [(1 - x, y), (x, 1 - y), (1 - x, 1 - y)]

        def rc(src, dst, k, to):
            return pltpu.make_async_remote_copy(src_ref=src, dst_ref=dst, send_sem=send_sems.at[k],
                                                recv_sem=recv_sems.at[k], device_id=to, device_id_type=MESH)

        halves = [pl.ds(pl.multiple_of(c * (s.shape[0] // 2), 16), s.shape[0] // 2) for s in shards]
        own, first = [], {}
        for i in range(n):
            cp = rc(x_refs[i], out_refs[i].at[chip], 7 * i + 6, (x, y, 1 - c))
            cp.start()
            own.append(cp)
            for k, (cx, cy) in enumerate(chips):
                cp = rc(x_refs[i].at[halves[i]], out_refs[i].at[chip, halves[i]], 7 * i + k, (cx, cy, c))
                cp.start()
                first[i, k] = cp
        passed = []
        for i in range(n):
            for k, (cx, cy) in enumerate(chips):
                first[i, k].wait_recv()
                src = out_refs[i].at[2 * cx + cy, halves[i]]
                cp = rc(src, src, 7 * i + 3 + k, (x, y, 1 - c))
                cp.start()
                passed.append(cp)
        for cp in passed + own:
            cp.wait_recv()
        for cp in list(first.values()) + passed + own:
            cp.wait_send()

    for s in shards:
        assert s.shape[0] % 32 == 0
    return pl.pallas_call(
        body, name=name, out_shape=[jax.ShapeDtypeStruct((4,) + s.shape, s.dtype) for s in shards],
        in_specs=[ANY] * n, out_specs=[ANY] * n, scratch_shapes=_sems(7 * n),
    )(*shards)


def ag_forward(name, bufs):
    n = len(bufs)

    def body(*refs):
        out_refs, send_sems, recv_sems = refs[n:2 * n], refs[2 * n], refs[2 * n + 1]
        x, y, c = _me()
        cps = []
        for i in range(n):
            rh = bufs[i].shape[1] // 2
            half = pl.ds(pl.multiple_of(c * rh, 16), rh)
            for k, (cx, cy) in enumerate([(1 - x, y), (x, 1 - y), (1 - x, 1 - y)]):
                src = out_refs[i].at[2 * cx + cy, half]
                cps.append(pltpu.make_async_remote_copy(src_ref=src, dst_ref=src, send_sem=send_sems.at[3 * i + k],
                                                        recv_sem=recv_sems.at[3 * i + k], device_id=(x, y, 1 - c),
                                                        device_id_type=MESH))
        for cp in cps:
            cp.start()
        for cp in cps:
            cp.wait_recv()
        for cp in cps:
            cp.wait_send()

    return pl.pallas_call(
        body, name=name, out_shape=[jax.ShapeDtypeStruct(b.shape, b.dtype) for b in bufs],
        in_specs=[ANY] * n, out_specs=[ANY] * n, scratch_shapes=_sems(3 * n),
        input_output_aliases={i: i for i in range(n)},
    )(*bufs)


def sibling_swap_halves(name, gs):
    n = len(gs)

    def body(*refs):
        g_refs, out_refs, send_sems, recv_sems = refs[:n], refs[n:2 * n], refs[2 * n], refs[2 * n + 1]
        x, y, c = _me()
        cps = [pltpu.make_async_remote_copy(src_ref=g_refs[i].at[:, 1 - c], dst_ref=out_refs[i],
                                            send_sem=send_sems.at[i], recv_sem=recv_sems.at[i],
                                            device_id=(x, y, 1 - c), device_id_type=MESH) for i in range(n)]
        for cp in cps:
            cp.start()
        for cp in cps:
            cp.wait_recv()
        for cp in cps:
            cp.wait_send()

    return pl.pallas_call(
        body, name=name, out_shape=[jax.ShapeDtypeStruct((4,) + g.shape[2:], g.dtype) for g in gs],
        in_specs=[ANY] * n, out_specs=[ANY] * n, scratch_shapes=_sems(n),
    )(*gs)


def chips_exchange(name, ps):
    n = len(ps)

    def body(*refs):
        p_refs, out_refs, send_sems, recv_sems = refs[:n], refs[n:2 * n], refs[2 * n], refs[2 * n + 1]
        x, y, c = _me()
        chip = 2 * x + y
        cps = []
        for i in range(n):
            for k, (cx, cy) in enumerate([(1 - x, y), (x, 1 - y), (1 - x, 1 - y)]):
                cps.append(pltpu.make_async_remote_copy(
                    src_ref=p_refs[i].at[2 * cx + cy], dst_ref=out_refs[i].at[chip], send_sem=send_sems.at[3 * i + k],
                    recv_sem=recv_sems.at[3 * i + k], device_id=(cx, cy, c), device_id_type=MESH))
        for cp in cps:
            cp.start()
        for cp in cps:
            cp.wait_recv()
        for cp in cps:
            cp.wait_send()

    return pl.pallas_call(
        body, name=name, out_shape=[jax.ShapeDtypeStruct(p.shape, p.dtype) for p in ps],
        in_specs=[ANY] * n, out_specs=[ANY] * n, scratch_shapes=_sems(3 * n),
    )(*ps)


def sibling_fill(name, ss):
    n = len(ss)

    def body(*refs):
        out_refs, send_sems, recv_sems = refs[n:2 * n], refs[2 * n], refs[2 * n + 1]
        x, y, c = _me()
        cps = []
        for i in range(n):
            rh = ss[i].shape[1] // 2
            mine = out_refs[i].at[:, pl.ds(pl.multiple_of(c * rh, 8), rh)]
            cps.append(pltpu.make_async_remote_copy(src_ref=mine, dst_ref=mine, send_sem=send_sems.at[i],
                                                    recv_sem=recv_sems.at[i], device_id=(x, y, 1 - c),
                                                    device_id_type=MESH))
        for cp in cps:
            cp.start()
        for cp in cps:
            cp.wait_recv()
        for cp in cps:
            cp.wait_send()

    return pl.pallas_call(
        body, name=name, out_shape=[jax.ShapeDtypeStruct(s.shape, s.dtype) for s in ss],
        in_specs=[ANY] * n, out_specs=[ANY] * n, scratch_shapes=_sems(n),
        input_output_aliases={i: i for i in range(n)},
    )(*ss)


def _rs_tile(rh, w):
    return _pick(rh, max(8, (1 << 19) // w // 8 * 8), 8)


def add_half(name, g, recv, idx):
    _, _, rh, w = g.shape
    tile = _rs_tile(rh, w)

    def body(s_ref, g_ref, r_ref, o_ref):
        o_ref[...] = (g_ref[...] + r_ref[...]).astype(o_ref.dtype)

    return pl.pallas_call(
        body, name=name,
        grid_spec=pltpu.PrefetchScalarGridSpec(
            num_scalar_prefetch=1, grid=(4, rh // tile),
            in_specs=[pl.BlockSpec((None, None, tile, w), lambda s, i, ix: (s, ix[1], i, 0)),
                      pl.BlockSpec((None, tile, w), lambda s, i, ix: (s, i, 0))],
            out_specs=pl.BlockSpec((None, tile, w), lambda s, i, ix: (s, i, 0))),
        out_shape=jax.ShapeDtypeStruct((4, rh, w), BF16),
        compiler_params=_cparams("parallel", "parallel"),
    )(idx, g, recv)


def rs_sum(name, g, recv1, recv2, idx, layer, acc=None):
    _, _, rh, w = g.shape
    tile = _rs_tile(rh, w)
    nt = rh // tile

    def body(s_ref, g_ref, r1_ref, a_ref, b_ref, c_ref, *rest):
        rest[-1][...] = (g_ref[...] + r1_ref[...]) + (a_ref[...].astype(F32) + b_ref[...].astype(F32)
                                                       + c_ref[...].astype(F32))

    def other(k):
        return pl.BlockSpec((None, tile, w), lambda i, ix: (jnp.bitwise_xor(ix[0], k), i, 0))

    ops = [idx, g, recv1, recv2, recv2, recv2] + ([acc] if acc is not None else [])
    return pl.pallas_call(
        body, name=name,
        grid_spec=pltpu.PrefetchScalarGridSpec(
            num_scalar_prefetch=1, grid=(rh // tile,),
            in_specs=[pl.BlockSpec((None, None, tile, w), lambda i, ix: (ix[0], ix[1], i, 0)),
                      pl.BlockSpec((None, tile, w), lambda i, ix: (ix[0], i, 0)), other(1), other(2), other(3)]
            + ([ANY] if acc is not None else []),
            out_specs=pl.BlockSpec((None, tile, w), lambda i, ix: (layer, ix[1] * nt + i, 0))),
        out_shape=jax.ShapeDtypeStruct((DEPTH, 2 * rh, w), F32),
        input_output_aliases={6: 0} if acc is not None else {},
        compiler_params=_cparams("parallel"),
    )(*ops)


def sum_slabs(name, a):
    n, r, _ = a.shape
    tile = _pick(r, 2048, 8)

    def body(a_ref, o_ref):
        acc = a_ref[0]
        for s in range(1, n):
            acc = acc + a_ref[s]
        o_ref[...] = acc

    return pl.pallas_call(
        body, name=name, grid=(r // tile,),
        in_specs=[pl.BlockSpec((n, tile, LANE), lambda i: (0, i, 0))],
        out_specs=pl.BlockSpec((tile, LANE), lambda i: (i, 0)),
        out_shape=jax.ShapeDtypeStruct((r, LANE), F32),
        compiler_params=_cparams("parallel"),
    )(a)


def f_premod(rv, cv):
    (x,), (w, scale, shift) = rv, cv
    y = x * lax.rsqrt(jnp.mean(x * x, -1, keepdims=True) + EPS) * w
    return [y * (1.0 + scale) + shift]


def f_postres(rv, cv):
    (x, y), (w, gate) = rv, cv
    return [x + gate * (y * lax.rsqrt(jnp.mean(y * y, -1, keepdims=True) + EPS) * w)]


def f_res_premod(rv, cv):
    (x, y), (w_post, gate, w_pre, scale, shift) = rv, cv
    x1, = f_postres([x, y], [w_post, gate])
    h, = f_premod([x1], [w_pre, scale, shift])
    return [x1, h]


def f_adamw(rv, cv):
    w, g, m, v = rv
    m = B1 * m + (1.0 - B1) * g
    v = B2 * v + (1.0 - B2) * jnp.square(g)
    m_hat = m / (1.0 - B1 ** STEP)
    v_hat = v / (1.0 - B2 ** STEP)
    return [-LR * (m_hat / (jnp.sqrt(v_hat) + AEPS) + WD * w), m, v], []


def adamw_layers(name, w, g, m, v):
    nl, r, wd = w.shape
    tile = r
    while tile * wd * 4 * 16 > 24 * 1024 * 1024 and tile % 16 == 0:
        tile //= 2

    outs = None
    for l in range(nl):
        def body(w_ref, g_ref, m_ref, v_ref, *rest):
            gv = g_ref[...]
            (dl, m2, v2), _ = f_adamw([w_ref[...], gv, m_ref[...], v_ref[...]], [])
            for ref, val in zip(rest[-4:], (gv, dl, m2, v2)):
                ref[...] = val

        blk = pl.BlockSpec((None, tile, wd), lambda i, l=l: (l, i, 0))
        outs = pl.pallas_call(
            body, name=f"{name}{l}", grid=(r // tile,), in_specs=[blk] * 4 + ([ANY] * 4 if outs else []),
            out_specs=[blk] * 4, out_shape=[jax.ShapeDtypeStruct(w.shape, F32)] * 4,
            input_output_aliases={4: 0, 5: 1, 6: 2, 7: 3} if outs else {}, compiler_params=_cparams("parallel"),
        )(w, g, m, v, *(outs or ()))
    return outs


def adamw(name, w, g, m, v):
    r, wd = w.shape
    tile = r
    while tile * wd * 4 * 14 > 24 * 1024 * 1024 and tile % 16 == 0:
        tile //= 2
    (d, m2, v2), _ = rowmap(name, f_adamw, [(a, 0, wd) for a in (w, g, m, v)], [], [(wd, F32)] * 3, tile=tile)
    return d, m2, v2


_IN_SIZES = (512, 512, 512, 512, 4, 4, 512, 1024, 8, 512, 128, 128, 512, 512, 512)


def _relayout_w_in(w):
    d = w.shape[0]
    s = np.cumsum((0,) + _IN_SIZES)
    gq, gk, gv, gz, gb, ga, sz, sx, sdt, aq, ak, av, cb, cc, ch = [w[:, s[i]:s[i + 1]] for i in range(15)]

    def rope_pad(t, nh):
        return jnp.pad(t.reshape(d, nh, 2, 32), ((0, 0), (0, 0), (0, 0), (0, 32))).reshape(d, nh * LANE)

    def v_pad(t, nh):
        return jnp.pad(t.reshape(d, nh, 64), ((0, 0), (0, 0), (0, 64))).reshape(d, nh * LANE)

    def pad128(t):
        return jnp.pad(t, ((0, 0), (0, LANE - t.shape[1])))

    return jnp.concatenate([gq, gk, gv, cb, cc, ch, sx, rope_pad(aq, 8), gz, sz, rope_pad(ak, 2), v_pad(av, 2),
                            pad128(jnp.concatenate([gb, ga], axis=1)), pad128(sdt)], axis=1)


def _unlayout_g_in(g):
    d = g.shape[0]

    def rope_unpad(t, nh):
        return t.reshape(d, nh, 2, 64)[..., :32].reshape(d, nh * 64)

    def v_unpad(t, nh):
        return t.reshape(d, nh, LANE)[..., :64].reshape(d, nh * 64)

    return jnp.concatenate([g[:, 0:1536], g[:, O_GZ:O_GZ + 512], g[:, O_GBA:O_GBA + 8], g[:, O_SZ:O_SZ + 512],
                            g[:, O_XBC:O_XBC + 1024], g[:, O_SDT:O_SDT + 8], rope_unpad(g[:, O_AQ:O_AQ + 1024], 8),
                            rope_unpad(g[:, O_AK:O_AK + 256], 2), v_unpad(g[:, O_AV:O_AV + 256], 2),
                            g[:, O_CB:O_CB + 1536]], axis=1)


def _relayout_w_out(w):
    d = w.shape[1]
    yc = jnp.pad(w[1024:1536].reshape(8, 64, d), ((0, 0), (0, 64), (0, 0))).reshape(1024, d)
    return jnp.concatenate([w[:1024], yc, w[1536:]], axis=0)


def _unlayout_g_out(g):
    d = g.shape[1]
    return jnp.concatenate([g[:1024], g[1024:2048].reshape(8, LANE, d)[:, :64].reshape(512, d), g[2048:]], axis=0)


def _pack(arrs):
    rows, spec, off = [], [], 0
    for a in arrs:
        n = int(np.prod(a.shape))
        nr = -(-n // LANE)
        rows.append(jnp.pad(a.reshape(-1).astype(F32), (0, nr * LANE - n)).reshape(nr, LANE))
        spec.append((off, a.shape))
        off += nr
    pad = (-off) % 8
    if pad:
        rows.append(jnp.zeros((pad, LANE), F32))
    return jnp.concatenate(rows, axis=0), spec


def _unpack(buf, spec):
    out = []
    for off, shape in spec:
        n = int(np.prod(shape))
        nr = -(-n // LANE)
        out.append(buf[off:off + nr].reshape(-1)[:n].reshape(shape))
    return out


def _row128(v):
    return jnp.pad(v.astype(F32), (0, LANE - v.shape[0])).reshape(1, LANE)


BIG = ("w_in", "w_out", "w_up", "w_down")
SMALL_REPL = ("ada_b", "norm_pre_mix", "norm_post_mix", "norm_pre_mlp", "norm_post_mlp", "gdn_a_log", "gdn_dt_bias",
              "gdn_norm_w", "ssm_conv_b", "ssm_a_log", "ssm_dt_bias", "ssm_d", "ssm_norm_w", "attn_sinks")
SMALL_SHARD = ("gdn_conv_w", "ssm_conv_w", "sc_conv_w")
WEIGHTS = ("ada_w", "ada_b", "norm_pre_mix", "norm_post_mix", "norm_pre_mlp", "norm_post_mlp", "w_in", "w_out",
           "gdn_conv_w", "gdn_a_log", "gdn_dt_bias", "gdn_norm_w", "ssm_conv_w", "ssm_conv_b", "ssm_a_log",
           "ssm_dt_bias", "ssm_d", "ssm_norm_w", "attn_sinks", "sc_conv_w", "w_up", "w_down")


def _step(p, m, v, x, c, positions, loss_target):
    d = x.shape[2]
    xi, yi, ci = _me()
    chip = 2 * xi + yi
    me = 4 * xi + 2 * yi + ci
    x0 = x[0]
    tgt = loss_target[0]

    buf, spec = _pack([c[0], p["gdn_conv_w"], p["ssm_conv_w"], p["sc_conv_w"]])
    got = allgather8("ag_small_in", buf)
    per_dev = [_unpack(got[k], spec) for k in range(8)]
    c_all = jnp.stack([pd[0] for pd in per_dev], axis=0)
    conv_full = [jnp.concatenate([per_dev[4 * a + 2 * b][1 + j] for a in range(2) for b in range(2)], axis=-1)
                 for j in range(3)]
    gdn_conv_w, ssm_conv_w, sc_conv_w = conv_full

    c_pad = jnp.pad(c_all, ((0, 8), (0, 0)))
    silu_bf = lambda a: _silu(a).astype(BF16)
    nsh = p["ada_w"].shape[2]
    mods = []
    for l in range(DEPTH):
        mo = mm_nn(f"ada_fwd{l}", c_pad, p["ada_w"][l], F32, a_fn=silu_bf)
        mods.append(mo[:8] + lax.dynamic_slice(p["ada_b"][l], (chip * nsh,), (nsh,))[None])
    buf, spec = _pack([jnp.stack(mods, axis=0)])
    got = allgather8("ag_mod", buf)
    mod_sh = [_unpack(got[4 * a + 2 * b], spec)[0] for a in range(2) for b in range(2)]
    mod_all = jnp.concatenate(mod_sh, axis=-1)
    mod = lax.dynamic_index_in_dim(mod_all, me, axis=1, keepdims=False)

    items = [(l, n) for l in range(DEPTH) for n in BIG]
    shards = {(l, n): p[n][l].astype(BF16) for (l, n) in items}
    gathered = {(0, "w_in"): allgather_chips("ag_w_in0", [shards[0, "w_in"]])[0]}

    def gather_behind(keys):
        return ("gather", [shards[kk] for kk in keys])

    def settle(name, keys, landed):
        gathered.update(zip(keys, ag_forward(name, landed)))

    wfull = []

    inv_freq = ROPE_THETA ** (-jnp.arange(0, 64, 2, dtype=F32) / 64)
    ang = positions[0].astype(F32)[:, None] * inv_freq
    z32 = jnp.zeros_like(ang)
    cos_t = jnp.concatenate([jnp.cos(ang), z32, jnp.cos(ang), z32], axis=1)
    sin_t = jnp.concatenate([-jnp.sin(ang), z32, jnp.sin(ang), z32], axis=1)

    relu2 = lambda a: jnp.square(jnp.maximum(a, 0))

    def layer_consts(l):
        row = lambda v_: v_.reshape(1, -1).astype(F32)
        return dict(
            alog=_row128(p["gdn_a_log"][l]), gdtb=_row128(p["gdn_dt_bias"][l]), gnw=row(p["gdn_norm_w"][l]),
            scb=row(p["ssm_conv_b"][l]), salog=_row128(p["ssm_a_log"][l]), sdtb=_row128(p["ssm_dt_bias"][l]),
            sdsk=_row128(p["ssm_d"][l]), snw=row(p["ssm_norm_w"][l]), sinks=_row128(p["attn_sinks"][l]))

    saved = []
    xc = x0
    for l in range(DEPTH):
        w = dict(w_in=_relayout_w_in(jnp.concatenate([gathered[l, "w_in"][s] for s in range(4)], axis=1)))
        wfull.append(w)
        k = layer_consts(l)
        md = [mod[l, j * d:(j + 1) * d].reshape(1, d) for j in range(6)]
        npre, npost = p["norm_pre_mix"][l].reshape(1, d), p["norm_post_mix"][l].reshape(1, d)
        npre2, npost2 = p["norm_pre_mlp"][l].reshape(1, d), p["norm_post_mlp"][l].reshape(1, d)
        if l == 0:
            (h,), _ = rowmap("premix0", lambda rv, cv: (f_premod(rv, cv), []), [(xc, 0, d)], [npre, md[1], md[0]],
                             [(d, BF16)])
        behind = [(0, "w_out"), (0, "w_up")] if l == 0 else [(l, "w_up")]
        proj, landed = mm_nn(f"inproj{l}", h, w["w_in"], F32, carry=gather_behind(behind))
        settle(f"ag_forward_a{l}", behind, landed)
        w["w_out"] = _relayout_w_out(gathered[l, "w_out"].reshape(-1, d))
        w["w_up"] = gathered[l, "w_up"]
        cq = conv_fwd(f"gdn_conv{l}", [(proj, O_GQKV, 1536)], gdn_conv_w[l])
        ya, gs, gt = gdn_fwd(f"gdn{l}", cq, proj, k["alog"], k["gdtb"], k["gnw"])
        cx = conv_fwd(f"ssm_conv{l}", [(proj, O_XBC, 1024)], ssm_conv_w[l])
        yb, ss = ssd_fwd(f"ssd{l}", cx, proj, k["scb"], k["salog"], k["sdtb"], k["sdsk"], k["snw"])
        yc = swa_fwd(f"swa{l}", proj, cos_t, sin_t, k["sinks"])
        cd = conv_fwd(f"sc_conv{l}", [(proj, O_CC, GW), (proj, O_CH, GW)], sc_conv_w[l])
        (ycat,), _ = rowmap(f"ycat{l}", lambda rv, cv: ([jnp.concatenate([rv[0], rv[1], rv[2], rv[3] * rv[4]], 1)], []),
                            [(ya, 0, GW), (yb, 0, GW), (yc, 0, 1024), (proj, O_CB, GW), (cd, 0, GW)], [],
                            [(YCAT, BF16)])
        y = mm_nn(f"outproj{l}", ycat, w["w_out"], F32)
        (x1, h2), _ = rowmap(f"mix_mlp{l}", lambda rv, cv: (f_res_premod(rv, cv), []), [(xc, 0, d), (y, 0, d)],
                             [npost, md[2], npre2, md[4], md[3]], [(d, F32), (d, BF16)], tile=128)
        u, landed = mm_nn(f"up{l}", h2, w["w_up"], BF16, carry=gather_behind([(l, "w_down")]))
        settle(f"ag_forward_b{l}", [(l, "w_down")], landed)
        w["w_down"] = gathered[l, "w_down"].reshape(-1, d)
        if l + 1 < DEPTH:
            behind = [(l + 1, "w_in"), (l + 1, "w_out")]
            y2, landed = mm_nn(f"down{l}", u, w["w_down"], F32, a_fn=relu2, carry=gather_behind(behind))
            settle(f"ag_forward_c{l}", behind, landed)
        else:
            y2 = mm_nn(f"down{l}", u, w["w_down"], F32, a_fn=relu2)
        saved.append(dict(x=xc, h=h, proj=proj, cq=cq, gs=gs, gt=gt, cx=cx, ss=ss, cd=cd, ycat=ycat, y=y, x1=x1, h2=h2, u=u,
                          y2=y2, md=md, k=k))
        if l + 1 < DEPTH:
            (xc, h), _ = rowmap(f"mlp_mix{l}", lambda rv, cv: (f_res_premod(rv, cv), []), [(x1, 0, d), (y2, 0, d)],
                                [npost2, md[5], p["norm_pre_mix"][l + 1].reshape(1, d),
                                 mod[l + 1, d:2 * d].reshape(1, d), mod[l + 1, 0:d].reshape(1, d)],
                                [(d, F32), (d, BF16)], tile=128)

    def f_loss(vals, cvals):
        x1_, y2_, tgt_ = vals
        (x2,), vjp = jax.vjp(lambda r, c: f_postres(list(r), c), [x1_, y2_], cvals)
        err = x2 - tgt_
        part = 0.5 * jnp.sum(jnp.mean(err * err, axis=-1, keepdims=True), axis=0, keepdims=True)
        dr, dc = vjp([err * (1.0 / d)])
        return list(dr), list(dc) + [jnp.broadcast_to(part, (1, LANE))]

    s = saved[-1]
    (dx1a, dy2), (dnpost2, dgate_m, loss_part) = rowmap(
        "loss_postmlp_b", f_loss, [(s["x1"], 0, d), (s["y2"], 0, d), (tgt, 0, d)],
        [p["norm_post_mlp"][DEPTH - 1].reshape(1, d), s["md"][5]], [(d, F32), (d, BF16)],
        [(1, d), (1, d), (1, LANE)], tile=128)

    idx = jnp.stack([chip, ci]).astype(jnp.int32)
    g4s, recv1s, recv2s, waiting = {}, {}, {}, []

    def reduce_start(key, g):
        l_, n_ = key
        g4 = g.reshape(4, 2, g.shape[1] // 2, g.shape[2])
        r1 = sibling_swap_halves(f"rs_sibling_{n_}{l_}", [g4])[0]
        g4s[key], recv1s[key] = g4, r1
        waiting.append((key, add_half(f"rs_add_{n_}{l_}", g4, r1, idx)))

    def mm_carrying(name, *args, **kw):
        if not waiting:
            return mm_nn(name, *args, **kw)
        out, landed = mm_nn(name, *args, carry=("exchange", [pt for _, pt in waiting]), **kw)
        recv2s.update(zip([kk for kk, _ in waiting], landed))
        waiting.clear()
        return out

    gsm, dmd = {}, {}
    for l in reversed(range(DEPTH)):
        w, s = wfull[l], saved[l]
        k, md = s["k"], s["md"]
        npre, npost = p["norm_pre_mix"][l].reshape(1, d), p["norm_post_mix"][l].reshape(1, d)
        npre2, npost2 = p["norm_pre_mlp"][l].reshape(1, d), p["norm_post_mlp"][l].reshape(1, d)
        if l == DEPTH - 1:
            gsm[l, "norm_post_mlp"], dmd[l, 5] = dnpost2[0], dgate_m
        du = mm_carrying(f"down_b{l}", dy2, w["w_down"], BF16, extra=s["u"], wt=True,
                         epi=lambda r, uu: r * (2.0 * jnp.maximum(uu.astype(F32), 0.0)))
        reduce_start((l, "w_down"), mm_tn(f"down_g{l}", s["u"], dy2, a_fn=relu2).reshape(4, -1, d))
        dh2 = mm_carrying(f"up_b{l}", du, w["w_up"], F32, wt=True)
        reduce_start((l, "w_up"), mm_tn(f"up_g{l}", s["h2"], du, col_slabs=4))
        (dxa, dy), (dnpost, dgate_a, dnpre2, dscale_m, dshift_m) = rowmap_vjp(
            f"mix_mlp_b{l}", f_res_premod, [(s["x"], 0, d), (s["y"], 0, d)], [npost, md[2], npre2, md[4], md[3]],
            [(dx1a, 0, d), (dh2, 0, d)], 2, [F32, BF16], tile=128)
        dycat = mm_nn(f"outproj_b{l}", dy, w["w_out"], F32, wt=True)
        reduce_start((l, "w_out"), _unlayout_g_out(mm_tn(f"outproj_g{l}", s["ycat"], dy)).reshape(4, -1, d))
        (dcb, dcd), _ = rowmap(f"scgate_b{l}", lambda rv, cv: ([rv[0] * rv[2], rv[0] * rv[1]], []),
                               [(dycat, 2048, GW), (s["proj"], O_CB, GW), (s["cd"], 0, GW)], [],
                               [(GW, BF16), (GW, F32)])
        (dcc, dch), g_scw = conv_bwd(f"sc_conv_b{l}", dcd, [(s["proj"], O_CC, GW), (s["proj"], O_CH, GW)],
                                     sc_conv_w[l], BF16)
        daq, dak, dav, g_sinks = swa_bwd(f"swa_b{l}", dycat, 1024, s["proj"], cos_t, sin_t, k["sinks"])
        dcx, dsz, dsdt, g_scb, g_salog, g_sdtb, g_sdsk, g_snw = ssd_bwd(
            f"ssd_b{l}", dycat, 512, s["cx"], s["proj"], s["ss"], k["scb"], k["salog"], k["sdtb"], k["sdsk"], k["snw"])
        (dxbc,), g_ssmw = conv_bwd(f"ssm_conv_b{l}", dcx, [(s["proj"], O_XBC, 1024)], ssm_conv_w[l], BF16)
        dcq, dgz, dgba, g_alog, g_gdtb, g_gnw = gdn_bwd(f"gdn_b{l}", dycat, 0, s["cq"], s["proj"], s["gs"], s["gt"],
                                                        k["alog"], k["gdtb"], k["gnw"])
        (dgqkv,), g_gdnw = conv_bwd(f"gdn_conv_b{l}", dcq, [(s["proj"], O_GQKV, 1536)], gdn_conv_w[l], BF16)
        dproj = jnp.concatenate([dgqkv, dcb, dcc, dch, dxbc, daq, dgz, dsz, dak, dav, dgba, dsdt], axis=1)
        def start_w_in():
            g_in = mm_tn(f"inproj_g{l}", s["h"], dproj)
            reduce_start((l, "w_in"), jnp.stack(jnp.split(_unlayout_g_in(g_in), 4, axis=1), axis=0))

        if l == 0:
            start_w_in()
        dh = mm_carrying(f"inproj_b{l}", dproj, w["w_in"], F32, wt=True)
        if l > 0:
            start_w_in()
            sp = saved[l - 1]
            (dx1a, dy2), (dnpost2, dgate_m, dnpre, dscale_a, dshift_a) = rowmap_vjp(
                f"mlp_mix_b{l - 1}", f_res_premod, [(sp["x1"], 0, d), (sp["y2"], 0, d)],
                [p["norm_post_mlp"][l - 1].reshape(1, d), sp["md"][5], npre, md[1], md[0]],
                [(dxa, 0, d), (dh, 0, d)], 2, [F32, BF16], tile=128)
            gsm[l - 1, "norm_post_mlp"], dmd[l - 1, 5] = dnpost2[0], dgate_m
        else:
            (dx,), (dnpre, dscale_a, dshift_a) = rowmap_vjp("premix_b0", f_premod, [(s["x"], 0, d)],
                                                            [npre, md[1], md[0]], [(dh, 0, d)], 1, [F32],
                                                            add_rows=[(dxa, 0, d)])
        dmd.update({(l, 0): dshift_a, (l, 1): dscale_a, (l, 2): dgate_a, (l, 3): dshift_m, (l, 4): dscale_m})
        gsm.update({(l, n_): g_ for n_, g_ in dict(
            norm_pre_mix=dnpre[0], norm_post_mix=dnpost[0], norm_pre_mlp=dnpre2[0],
            gdn_a_log=g_alog[0, :4], gdn_dt_bias=g_gdtb[0, :4], gdn_norm_w=g_gnw[0], ssm_conv_b=g_scb[0],
            ssm_a_log=g_salog[0, :8], ssm_dt_bias=g_sdtb[0, :8], ssm_d=g_sdsk[0, :8], ssm_norm_w=g_snw[0],
            attn_sinks=g_sinks[0, :8], gdn_conv_w=g_gdnw, ssm_conv_w=g_ssmw, sc_conv_w=g_scw).items()})
    grad_x = dx[None]
    dmods = [jnp.concatenate([dmd[l, j] for j in range(6)], axis=1)[0] for l in range(DEPTH)]
    gsmall = [{n_: gsm[l, n_] for n_ in SMALL_REPL + SMALL_SHARD if n_ != "ada_b"} for l in range(DEPTH)]

    buf, spec = _pack([jnp.stack(dmods, axis=0)])
    got = allgather8("ag_dmod", buf)
    dmod_sum = _unpack(sum_slabs("dmod_sum", got), spec)[0]
    dmod_all = jnp.stack([_unpack(got[kk], spec)[0] for kk in range(8)], axis=1)
    g_ada_w = []
    for l in range(DEPTH):
        dm = lax.dynamic_slice(dmod_all[l], (0, chip * nsh), (8, nsh))
        g_ada_w.append(mm_tn(f"ada_g{l}", c_pad, jnp.pad(dm, ((0, 8), (0, 0))), a_fn=silu_bf))
    g_ada_w = jnp.stack(g_ada_w, axis=0)

    names = [n for n in SMALL_REPL if n != "ada_b"] + list(SMALL_SHARD)
    buf, spec = _pack([jnp.stack([gsmall[l][n] for l in range(DEPTH)], axis=0) for n in names] + [loss_part])
    tot = _unpack(sum_slabs("small_sum", allgather8("ag_small_g", buf)), spec)
    gfull = dict(zip(names, tot[:-1]))
    loss = tot[-1][0, 0]
    grads = {"ada_w": g_ada_w, "ada_b": dmod_sum}
    for n in SMALL_REPL:
        if n != "ada_b":
            grads[n] = gfull[n]
    for n in SMALL_SHARD:
        wdt = p[n].shape[2]
        grads[n] = lax.dynamic_slice_in_dim(gfull[n], chip * wdt, wdt, axis=2)

    if waiting:
        recv2s.update(zip([kk for kk, _ in waiting], chips_exchange("rs_chips_last", [pt for _, pt in waiting])))
    half = []
    for n in BIG:
        acc = None
        for l in range(DEPTH):
            acc = rs_sum(f"rs_sum_{n}{l}", g4s[l, n], recv1s[l, n], recv2s[l, n], idx, l, acc)
        half.append(acc)
    grads.update(zip(BIG, sibling_fill("rs_join", half)))

    delta, new_m, new_v = {}, {}, {}
    for n in ("ada_w",) + BIG:
        grads[n], delta[n], new_m[n], new_v[n] = adamw_layers(f"adamw_{n}", p[n], grads[n], m[n], v[n])
    small = list(SMALL_REPL) + list(SMALL_SHARD)
    bufs = []
    for src in (p, grads, m, v):
        b_, spec = _pack([src[n] for n in small])
        bufs.append(b_)
    dl, m2, v2 = adamw("adamw_small", *bufs)
    for n, a, b_, cc_ in zip(small, _unpack(dl, spec), _unpack(m2, spec), _unpack(v2, spec)):
        delta[n], new_m[n], new_v[n] = a, b_, cc_

    return (loss, grad_x, *[grads[n] for n in WEIGHTS], *[delta[n] for n in WEIGHTS], *[new_m[n] for n in WEIGHTS],
            *[new_v[n] for n in WEIGHTS])


def kernel(x, c, positions, ada_w, ada_b, norm_pre_mix, norm_post_mix, norm_pre_mlp, norm_post_mlp, w_in, w_out, gdn_conv_w, gdn_a_log, gdn_dt_bias, gdn_norm_w, ssm_conv_w, ssm_conv_b, ssm_a_log, ssm_dt_bias, ssm_d, ssm_norm_w, attn_sinks, sc_conv_w, w_up, w_down, loss_target, m_ada_w, m_ada_b, m_norm_pre_mix, m_norm_post_mix, m_norm_pre_mlp, m_norm_post_mlp, m_w_in, m_w_out, m_gdn_conv_w, m_gdn_a_log, m_gdn_dt_bias, m_gdn_norm_w, m_ssm_conv_w, m_ssm_conv_b, m_ssm_a_log, m_ssm_dt_bias, m_ssm_d, m_ssm_norm_w, m_attn_sinks, m_sc_conv_w, m_w_up, m_w_down, v_ada_w, v_ada_b, v_norm_pre_mix, v_norm_post_mix, v_norm_pre_mlp, v_norm_post_mlp, v_w_in, v_w_out, v_gdn_conv_w, v_gdn_a_log, v_gdn_dt_bias, v_gdn_norm_w, v_ssm_conv_w, v_ssm_conv_b, v_ssm_a_log, v_ssm_dt_bias, v_ssm_d, v_ssm_norm_w, v_attn_sinks, v_sc_conv_w, v_w_up, v_w_down):
    loc = locals()
    p = {n: loc[n] for n in WEIGHTS}
    m = {n: loc["m_" + n] for n in WEIGHTS}
    v = {n: loc["v_" + n] for n in WEIGHTS}
    return _step(p, m, v, x, c, positions, loss_target)
```

```python
import functools

import numpy as np
import jax
import jax.numpy as jnp
from jax import lax
from jax.experimental import pallas as pl
from jax.experimental.pallas import tpu as pltpu

F32, BF16 = jnp.float32, jnp.bfloat16
MESH = pl.DeviceIdType.MESH

DEPTH = 2
GW = 512
N_GDN, GDN_CS = 4, 64
SSD_CS, ATT_W = 128, 128
EPS = 1e-6
ROPE_THETA = 10000.0
IN_WIDTH = 5904
B1, B2, LR, AEPS, WD, STEP = 0.9, 0.999, 0.001, 1e-8, 0.01, 10

VMEM_LIMIT = 52 * 1024 * 1024
LANE = 128

O_GQKV, O_CB, O_CC, O_CH, O_XBC, O_AQ, O_GZ, O_SZ, O_AK, O_AV, O_GBA, O_SDT = (
    0, 1536, 2048, 2560, 3072, 4096, 5120, 5632, 6144, 6400, 6656, 6784)
PW = 6912
YCAT = 2560


def _cparams(*sem):
    return pltpu.CompilerParams(dimension_semantics=sem or None, vmem_limit_bytes=VMEM_LIMIT)


def _pick(n, cap, mult):
    if n <= cap:
        return n
    best = None
    for d in range(mult, cap + 1, mult):
        if n % d == 0:
            best = d
    assert best is not None, (n, cap, mult)
    return best


def _pieces(x, n):
    out, r = [], x
    for i in range(n):
        h = r.astype(BF16)
        out.append(h)
        if i < n - 1:
            r = r - h.astype(F32)
    return out


def _dgp(a, b, ca, cb, na, nb):
    dn = (((ca,), (cb,)), ((), ()))
    acc = None
    for i, pa in enumerate(_pieces(a, na)):
        for j, pb in enumerate(_pieces(b, nb)):
            if i + j < max(na, nb):
                t = lax.dot_general(pa, pb, dn, preferred_element_type=F32)
                acc = t if acc is None else acc + t
    return acc


_CONTRACT = {"nn": (1, 0), "nt": (1, 1), "tn": (0, 0)}


@functools.lru_cache(maxsize=None)
def _mm(kind, na, nb, const_a=False):
    ca, cb = _CONTRACT[kind]
    ng = max(na, nb)

    @jax.custom_vjp
    def f(a, b):
        return _dgp(a, b, ca, cb, na, nb)

    def bwd(r, g):
        a, b = r
        if kind == "nn":
            da, db = (lambda: _mm("nt", ng, nb)(g, b)), _mm("tn", na, ng, const_a)(a, g)
        elif kind == "nt":
            da, db = (lambda: _mm("nn", ng, nb)(g, b)), _mm("tn", ng, na)(g, a)
        else:
            da, db = (lambda: _mm("nt", nb, ng)(b, g)), _mm("nn", na, ng, const_a)(a, g)
        return (jnp.zeros_like(a) if const_a else da()), db

    f.defvjp(lambda a, b: (f(a, b), (a, b)), bwd)
    return f


_nn, _nt, _tn = _mm("nn", 1, 1), _mm("nt", 1, 1), _mm("tn", 1, 1)
_hnn, _hnt, _htn = _mm("nn", 2, 2), _mm("nt", 2, 2), _mm("tn", 2, 2)
_h2nn = _mm("nn", 2, 1)
_xnn = _mm("nn", 1, 3, True)


def _silu(x):
    return x * jax.nn.sigmoid(x)


def _softplus(x):
    return jnp.maximum(x, 0.0) + jnp.log1p(jnp.exp(-jnp.abs(x)))


def _colsel(v, idx):
    lane = lax.broadcasted_iota(jnp.int32, v.shape, 1)
    return jnp.sum(jnp.where(lane == idx, v, 0.0), axis=1, keepdims=True)


def _tri(n):
    r = lax.broadcasted_iota(jnp.int32, (n, n), 0)
    c = lax.broadcasted_iota(jnp.int32, (n, n), 1)
    return r >= c, r > c


def _rowsel(v, idx):
    row = lax.broadcasted_iota(jnp.int32, v.shape, 0)
    return jnp.sum(jnp.where(row == idx, v, 0.0), axis=0, keepdims=True)


def _tri_inv(m, size, nil):
    r = lax.broadcasted_iota(jnp.int32, (size, size), 0)
    c = lax.broadcasted_iota(jnp.int32, (size, size), 1)
    x = jnp.where(r == c, 1.0, 0.0) - m
    p = _h2nn(m, m)
    steps = int(np.log2(nil)) - 1
    for s in range(steps):
        x = x + _h2nn(x, p)
        if s < steps - 1:
            p = _h2nn(p, p)
    return x


@jax.custom_vjp
def _inv_given(m, tinv):
    return tinv


_inv_given.defvjp(lambda m, tinv: (tinv, tinv),
                  lambda tinv, g: (-_hnt(_htn(tinv, g), tinv), jnp.zeros_like(tinv)))


def _gdn_chunk(cq, z, ba, s, alog, dtb, nw, tinv_saved=None):
    n, nh = GDN_CS, N_GDN
    rr = n * nh

    def stack(a, off):
        return jnp.concatenate([a[:, off + h * LANE:off + (h + 1) * LANE] for h in range(nh)], axis=0)

    def percol(f):
        return jnp.concatenate([f(h) for h in range(nh)], axis=0)

    def own(a):
        return jnp.concatenate([a[h * n:(h + 1) * n, h * LANE:(h + 1) * LANE] for h in range(nh)], axis=0)

    q, k, v = _silu(stack(cq, 0)), _silu(stack(cq, GW)), _silu(stack(cq, 2 * GW))
    q = q * lax.rsqrt(jnp.sum(q * q, -1, keepdims=True) + EPS) * (128 ** -0.5)
    k = k * lax.rsqrt(jnp.sum(k * k, -1, keepdims=True) + EPS)
    beta = jax.nn.sigmoid(percol(lambda h: _colsel(ba, h)))
    g = percol(lambda h: -jnp.exp(_colsel(alog, h)) * _softplus(_colsel(ba, nh + h) + _colsel(dtb, h)))
    ri = lax.broadcasted_iota(jnp.int32, (rr, rr), 0)
    ci = lax.broadcasted_iota(jnp.int32, (rr, rr), 1)
    same = (ri >> 6) == (ci >> 6)
    incl, strict = same & (ri >= ci), same & (ri > ci)
    gcb = _xnn(incl.astype(F32), jnp.broadcast_to(g, (rr, LANE)))
    gci = jnp.concatenate([gcb, gcb], axis=1)
    dec = jnp.where(incl, jnp.exp(jnp.where(incl, gci - gci.T, 0.0)), 0.0)
    kb = k * beta
    kk = _nt(jnp.concatenate([kb, q], axis=0), k)
    m = jnp.where(strict, kk[:rr] * dec, 0.0)
    attn = jnp.where(incl, kk[rr:] * dec, 0.0)
    tinv = _tri_inv(m, rr, n) if tinv_saved is None else _inv_given(m, tinv_saved)
    egc = jnp.exp(gcb)
    uw = _nn(tinv, jnp.concatenate([v * beta, kb * egc], axis=1))
    u, w = uw[:, :LANE], uw[:, LANE:]
    gl = [jnp.sum(g[h * n:(h + 1) * n], axis=0, keepdims=True) for h in range(nh)]
    kd = k * jnp.exp(percol(lambda h: jnp.broadcast_to(gl[h], (n, 1))) - gcb)
    ws = _nn(jnp.concatenate([w, q * egc], axis=0), s)
    v_new = u - own(ws[:rr])
    o = own(ws[rr:]) + _nn(attn, v_new)
    rowh = lax.broadcasted_iota(jnp.int32, (rr, LANE), 0) >> 6
    vexp = jnp.concatenate([jnp.where(rowh == h, v_new, 0.0) for h in range(nh)], axis=1)
    cd = jnp.concatenate([jnp.broadcast_to(jnp.exp(gl[h]), (1, LANE)) for h in range(nh)], axis=1)
    s_new = s * cd + _tn(kd, vexp)
    y = o * lax.rsqrt(jnp.mean(o * o, -1, keepdims=True) + EPS) * nw * _silu(stack(z, 0))
    y = jnp.concatenate([y[h * n:(h + 1) * n] for h in range(nh)], axis=1)
    return (y, s_new, tinv) if tinv_saved is None else (y, s_new)


def _ssd_chunk(cx, z, dtc, s, convb, alog, dtb, dsk, nw):
    n = SSD_CS
    incl, _ = _tri(n)
    xbc = _silu(cx + convb)
    dt_all = _softplus(dtc + dtb)
    da_all = dt_all * (-jnp.exp(alog))
    cs_all = _xnn(incl.astype(F32), da_all)
    cs_t = cs_all.T
    tot_all = jnp.sum(da_all, axis=0, keepdims=True)
    lane_lo = lax.broadcasted_iota(jnp.int32, (n, LANE), 1) < 64
    ys, s_out = [], []
    for pi in range(4):
        g = pi // 2
        bm = xbc[:, GW + g * LANE:GW + (g + 1) * LANE]
        cm = xbc[:, GW + 256 + g * LANE:GW + 256 + (g + 1) * LANE]
        cbm = _nt(cm, bm)
        xp, s_p = xbc[:, pi * LANE:(pi + 1) * LANE], s[:, pi * LANE:(pi + 1) * LANE]
        a_rows, b_cols, dts, dsks, etot = [], [], [], [], []
        for h in (2 * pi, 2 * pi + 1):
            cs_i = _colsel(cs_all, h)
            lm = jnp.where(incl, jnp.exp(jnp.where(incl, cs_i - _rowsel(cs_t, h), 0.0)), 0.0)
            tot = _colsel(tot_all, h)
            a_rows.append(jnp.concatenate([cbm * lm, cm * jnp.exp(cs_i)], axis=1))
            b_cols.append(bm * jnp.exp(tot - cs_i))
            dts.append(_colsel(dt_all, h))
            dsks.append(_colsel(dsk, h))
            etot.append(jnp.exp(tot))
        xdt = xp * jnp.where(lane_lo, dts[0], dts[1])
        yy = _nn(jnp.concatenate(a_rows, axis=0), jnp.concatenate([xdt, s_p], axis=0))
        ys.append(jnp.where(lane_lo, yy[:n], yy[n:]) + xp * jnp.where(lane_lo[:1], dsks[0], dsks[1]))
        st = _tn(jnp.concatenate(b_cols, axis=1), xdt)
        s_out.append(s_p * jnp.where(lane_lo[:1], etot[0], etot[1]) + jnp.where(lane_lo, st[:n], st[n:]))
    zs = _silu(z)
    out = []
    for g in range(2):
        y = jnp.concatenate(ys[2 * g:2 * g + 2], axis=1) * zs[:, g * 256:(g + 1) * 256]
        out.append(y * lax.rsqrt(jnp.mean(y * y, -1, keepdims=True) + EPS) * nw[:, g * 256:(g + 1) * 256])
    return jnp.concatenate(out, axis=1), jnp.concatenate(s_out, axis=1)


SWA_STACK = 1


def _swa_block(qg, kp, kc, vp, vc, sink, notfirst, *, g):
    w, ns = ATT_W, SWA_STACK
    kb = jnp.concatenate([kp, kc], axis=0)
    vb = jnp.concatenate([vp, vc], axis=0)
    qi = lax.broadcasted_iota(jnp.int32, (ns * w, 2 * w), 0) & (w - 1)
    kj = lax.broadcasted_iota(jnp.int32, (ns * w, 2 * w), 1)
    rel = qi + w - kj
    mask = (rel >= 0) & (rel < w) & ((kj >= w) | (notfirst > 0.5))
    outs = []
    for r0 in range(0, 4, ns):
        q = jnp.concatenate([qg[:, r * LANE:(r + 1) * LANE] for r in range(r0, r0 + ns)], axis=0)
        snk = jnp.concatenate([jnp.broadcast_to(_colsel(sink, 4 * g + r), (w, 1)) for r in range(r0, r0 + ns)], axis=0)
        s = jnp.where(mask, _nt(q, kb) * (64 ** -0.5), -1e30)
        mx = lax.stop_gradient(jnp.maximum(jnp.max(s, axis=-1, keepdims=True), snk))
        p = jnp.where(mask, jnp.exp(s - mx), 0.0)
        p = p / (jnp.sum(p, axis=-1, keepdims=True) + jnp.exp(snk - mx))
        o = _nn(p, vb)
        outs += [o[j * w:(j + 1) * w] for j in range(ns)]
    return jnp.concatenate(outs, axis=1)


def _rope(x, cos, sin):
    return x * cos + pltpu.roll(x, 64, 1) * sin


def _unrope(dy, cos, sin):
    return dy * cos + pltpu.roll(dy * sin, 64, 1)


def rowmap(name, fn, rows, consts, outs, accs=(), tile=256):
    t = rows[0][0].shape[0]
    tile = min(tile, t)
    assert t % tile == 0
    nr, nc, no, na = len(rows), len(consts), len(outs), len(accs)

    def body(*refs):
        rv = [r[...] for r in refs[:nr]]
        cv = [r[...] for r in refs[nr:nr + nc]]
        ov, av = fn(rv, cv)
        for ref, o in zip(refs[nr + nc:nr + nc + no], ov):
            ref[...] = o.astype(ref.dtype)
        if na:
            a_refs = refs[nr + nc + no:]

            @pl.when(pl.program_id(0) == 0)
            def _():
                for ref in a_refs:
                    ref[...] = jnp.zeros_like(ref)

            for ref, a in zip(a_refs, av):
                ref[...] += a

    def cspec(off, width):
        assert off % width == 0
        cb = off // width
        return pl.BlockSpec((tile, width), lambda i: (i, cb))

    in_specs = [cspec(off, wd) for (_, off, wd) in rows]
    in_specs += [pl.BlockSpec(c.shape, lambda i: (0, 0)) for c in consts]
    out_specs = [pl.BlockSpec((tile, wd), lambda i: (i, 0)) for (wd, _) in outs]
    out_specs += [pl.BlockSpec(s, lambda i: (0, 0)) for s in accs]
    out_shape = [jax.ShapeDtypeStruct((t, wd), dt) for (wd, dt) in outs]
    out_shape += [jax.ShapeDtypeStruct(s, F32) for s in accs]
    res = pl.pallas_call(
        body, name=name, grid=(t // tile,), in_specs=in_specs, out_specs=out_specs, out_shape=out_shape,
        compiler_params=_cparams("arbitrary"),
    )(*[r[0] for r in rows], *consts)
    return res[:no], res[no:]


def rowmap_vjp(name, f, rows, consts, douts, n_diff, drow_dtypes, tile=256, add_rows=()):
    nr, nd, nadd = len(rows), len(douts), len(add_rows)

    def g(vals, cvals):
        rv, dov, addv = vals[:nr], vals[nr:nr + nd], vals[nr + nd:]
        fixed = rv[n_diff:]
        _, vjp = jax.vjp(lambda r, c: f(list(r) + list(fixed), c), rv[:n_diff], cvals)
        dr, dc = vjp([d.astype(F32) for d in dov])
        dr = list(dr)
        for a in addv:
            dr[0] = dr[0] + a
        return dr, dc

    outs = [(rows[i][2], drow_dtypes[i]) for i in range(n_diff)]
    accs = [c.shape for c in consts]
    return rowmap(name, g, list(rows) + list(douts) + list(add_rows), consts, outs, accs, tile=tile)


def _carried_copies(kind, in_refs, out_refs, send_sems, recv_sems):
    x, y, c = _me()
    chip = 2 * x + y
    chips = [(1 - x, y), (x, 1 - y), (1 - x, 1 - y)]
    cps = []

    def rc(src, dst, k, to):
        return pltpu.make_async_remote_copy(src_ref=src, dst_ref=dst, send_sem=send_sems.at[k], recv_sem=recv_sems.at[k],
                                            device_id=to, device_id_type=MESH)

    for i, (src, dst) in enumerate(zip(in_refs, out_refs)):
        if kind == "gather":
            rh = src.shape[0] // 2
            half = pl.ds(pl.multiple_of(c * rh, 16), rh)
            cps.append(rc(src, dst.at[chip], 4 * i + 3, (x, y, 1 - c)))
            cps += [rc(src.at[half], dst.at[chip, half], 4 * i + k, (cx, cy, c)) for k, (cx, cy) in enumerate(chips)]
        else:
            cps += [rc(src.at[2 * cx + cy], dst.at[chip], 4 * i + k, (cx, cy, c)) for k, (cx, cy) in enumerate(chips)]
    return cps


def mm_nn(name, a, w, out_dtype, a_fn=None, extra=None, epi=None, tm_cap=1024, wt=False, carry=None):
    nc = len(carry[1]) if carry else 0
    m, k = a.shape
    stacked = w.ndim == 3
    n = (w.shape[-2] if wt else w.shape[-1] * (w.shape[0] if stacked else 1))
    slab = w.shape[-1]
    tm = _pick(m, tm_cap, 16)
    tn = _pick(n if wt else slab, 1024, LANE)
    tk = _pick(slab if wt else k, 2048, LANE)
    nk = k // tk
    per = slab // (tk if wt else tn)
    wdims = ((1,), (1,)) if wt else ((1,), (0,))

    gi, gj = m // tm, n // tn
    n_in = 2 + (extra is not None)

    def body(*refs):
        a_ref, w_ref = refs[0], refs[1]
        e_ref = refs[2] if extra is not None else None
        o_ref, acc = refs[n_in + nc], refs[n_in + 2 * nc + 1]
        kk = pl.program_id(2)
        if nc:
            def copies():
                return _carried_copies(carry[0], refs[n_in:n_in + nc], refs[n_in + nc + 1:n_in + 2 * nc + 1],
                                       refs[-2], refs[-1])

            @pl.when((pl.program_id(0) == 0) & (pl.program_id(1) == 0) & (kk == 0))
            def _():
                for cp in copies():
                    cp.start()

        @pl.when(kk == 0)
        def _():
            acc[...] = jnp.zeros_like(acc)

        av = a_ref[...]
        if a_fn is not None:
            av = a_fn(av)
        acc[...] += lax.dot_general(av.astype(BF16), w_ref[...].astype(BF16), (wdims, ((), ())),
                                    preferred_element_type=F32)

        @pl.when(kk == nk - 1)
        def _():
            r = acc[...]
            if epi is not None:
                r = epi(r, e_ref[...])
            o_ref[...] = r.astype(o_ref.dtype)

        if nc:
            @pl.when((pl.program_id(0) == gi - 1) & (pl.program_id(1) == gj - 1) & (kk == nk - 1))
            def _():
                cps = copies()
                for cp in cps:
                    cp.wait_recv()
                for cp in cps:
                    cp.wait_send()

    if wt:
        w_spec = (pl.BlockSpec((None, tn, tk), lambda i, j, kk: (kk // per, j, kk % per)) if stacked
                  else pl.BlockSpec((tn, tk), lambda i, j, kk: (j, kk)))
    else:
        w_spec = (pl.BlockSpec((None, tk, tn), lambda i, j, kk: (j // per, kk, j % per)) if stacked
                  else pl.BlockSpec((tk, tn), lambda i, j, kk: (kk, j)))
    in_specs = [pl.BlockSpec((tm, tk), lambda i, j, kk: (i, kk)), w_spec]
    ops = [a, w]
    if extra is not None:
        in_specs.append(pl.BlockSpec((tm, tn), lambda i, j, kk: (i, j)))
        ops.append(extra)
    out_spec = pl.BlockSpec((tm, tn), lambda i, j, kk: (i, j))
    out_shape = jax.ShapeDtypeStruct((m, n), out_dtype)
    if not nc:
        return pl.pallas_call(
            body, name=name, grid=(gi, gj, nk), in_specs=in_specs, out_specs=out_spec, out_shape=out_shape,
            scratch_shapes=[pltpu.VMEM((tm, tn), F32)],
            compiler_params=_cparams("parallel", "parallel", "arbitrary"),
        )(*ops)
    landing = [jax.ShapeDtypeStruct((4,) + c.shape if carry[0] == "gather" else c.shape, c.dtype) for c in carry[1]]
    res = pl.pallas_call(
        body, name=name, grid=(gi, gj, nk), in_specs=in_specs + [ANY] * nc, out_specs=[out_spec] + [ANY] * nc,
        out_shape=[out_shape] + landing, scratch_shapes=[pltpu.VMEM((tm, tn), F32)] + _sems(4 * nc),
        compiler_params=_cparams("arbitrary", "arbitrary", "arbitrary"),
    )(*ops, *carry[1])
    return res[0], list(res[1:])


def mm_tn(name, a, b, a_fn=None, col_slabs=1):
    t, m = a.shape
    n = b.shape[1]
    tm, tn, tk = _pick(m, 1024, LANE), _pick(n // col_slabs, 1024, LANE), _pick(t, 2048, 16)
    nk = t // tk
    per = n // col_slabs // tn

    def body(a_ref, b_ref, o_ref, acc):
        kk = pl.program_id(2)

        @pl.when(kk == 0)
        def _():
            acc[...] = jnp.zeros_like(acc)

        av = a_ref[...]
        if a_fn is not None:
            av = a_fn(av)
        acc[...] += lax.dot_general(av.astype(BF16), b_ref[...].astype(BF16), (((0,), (0,)), ((), ())),
                                    preferred_element_type=F32)

        @pl.when(kk == nk - 1)
        def _():
            o_ref[...] = acc[...]

    return pl.pallas_call(
        body, name=name, grid=(m // tm, n // tn, nk),
        in_specs=[pl.BlockSpec((tk, tm), lambda i, j, kk: (kk, i)), pl.BlockSpec((tk, tn), lambda i, j, kk: (kk, j))],
        out_specs=(pl.BlockSpec((tm, tn), lambda i, j, kk: (i, j)) if col_slabs == 1
                   else pl.BlockSpec((None, tm, tn), lambda i, j, kk: (j // per, i, j % per))),
        out_shape=jax.ShapeDtypeStruct((m, n) if col_slabs == 1 else (col_slabs, m, n // col_slabs), F32),
        scratch_shapes=[pltpu.VMEM((tm, tn), F32)],
        compiler_params=_cparams("parallel", "parallel", "arbitrary"),
    )(a, b)


CONV_TILE = 256
HALO = 8


def conv_fwd(name, xs, w):
    t = xs[0][0].shape[0]
    c = xs[0][2]
    kw = w.shape[0]
    tt = min(CONV_TILE, t)
    nx = len(xs)

    def body(*refs):
        cur_refs, prev_refs, w_ref, y_ref = refs[:nx], refs[nx:2 * nx], refs[2 * nx], refs[2 * nx + 1]
        i = pl.program_id(0)
        cur, prev = cur_refs[0][...], prev_refs[0][...]
        for r, p in zip(cur_refs[1:], prev_refs[1:]):
            cur, prev = cur * r[...], prev * p[...]
        prev = jnp.where(i > 0, prev, 0.0)
        xx = jnp.concatenate([prev, cur], axis=0)
        acc = jnp.zeros((tt, c), F32)
        for k in range(kw):
            sh = kw - 1 - k
            rolled = pltpu.roll(xx, sh, 0) if sh else xx
            acc = acc + rolled[HALO:] * w_ref[k:k + 1, :]
        y_ref[...] = acc

    def cur_spec(off):
        cb = off // c
        return pl.BlockSpec((tt, c), lambda i: (i, cb))

    def prev_spec(off):
        cb = off // c
        return pl.BlockSpec((HALO, c), lambda i: (jnp.maximum(i * (tt // HALO) - 1, 0), cb))

    for (_, off, wd) in xs:
        assert wd == c and off % c == 0
    return pl.pallas_call(
        body, name=name, grid=(t // tt,),
        in_specs=[cur_spec(off) for (_, off, _) in xs] + [prev_spec(off) for (_, off, _) in xs]
        + [pl.BlockSpec(w.shape, lambda i: (0, 0))],
        out_specs=pl.BlockSpec((tt, c), lambda i: (i, 0)),
        out_shape=jax.ShapeDtypeStruct((t, c), F32),
        compiler_params=_cparams("arbitrary"),
    )(*[x[0] for x in xs], *[x[0] for x in xs], w)


def conv_bwd(name, dy, xs, w, dx_dtype):
    t = xs[0][0].shape[0]
    c = xs[0][2]
    kw = w.shape[0]
    tt = min(CONV_TILE, t)
    nt = t // tt
    nx = len(xs)
    n = tt + HALO

    def body(*refs):
        dy_ref = refs[0]
        cur_refs, prev_refs = refs[1:1 + nx], refs[1 + nx:1 + 2 * nx]
        w_ref = refs[1 + 2 * nx]
        dx_refs = refs[2 + 2 * nx:2 + 3 * nx]
        dw_ref, carry = refs[2 + 3 * nx], refs[3 + 3 * nx]
        i = pl.program_id(0)

        @pl.when(i == 0)
        def _():
            carry[...] = jnp.zeros_like(carry)
            dw_ref[...] = jnp.zeros_like(dw_ref)

        curs = [r[...] for r in cur_refs]
        cur, prev = curs[0], prev_refs[0][...]
        for cv, p in zip(curs[1:], prev_refs[1:]):
            cur, prev = cur * cv, prev * p[...]
        prev = jnp.where(i < nt - 1, prev, 0.0)
        xx = jnp.concatenate([prev, cur], axis=0)
        dyv = dy_ref[...]
        dpad = jnp.concatenate([jnp.zeros((HALO, c), F32), dyv], axis=0)
        dxx = jnp.zeros((n, c), F32)
        dws = []
        for k in range(kw):
            sh = kw - 1 - k
            dxx = dxx + (pltpu.roll(dpad, n - sh, 0) if sh else dpad) * w_ref[k:k + 1, :]
            rolled = pltpu.roll(xx, sh, 0) if sh else xx
            dws.append(jnp.sum(dyv * rolled[HALO:], axis=0, keepdims=True))
        dw_ref[...] += jnp.concatenate(dws, axis=0)
        dcur = jnp.concatenate([dxx[HALO:tt], dxx[tt:] + carry[...]], axis=0)
        carry[...] = dxx[:HALO]
        for j, ref in enumerate(dx_refs):
            d = dcur
            for jj, cv in enumerate(curs):
                if jj != j:
                    d = d * cv
            ref[...] = d.astype(ref.dtype)

    def cur_spec(off):
        cb = off // c
        return pl.BlockSpec((tt, c), lambda i: (nt - 1 - i, cb))

    def prev_spec(off):
        cb = off // c
        return pl.BlockSpec((HALO, c), lambda i: (jnp.maximum((nt - 1 - i) * (tt // HALO) - 1, 0), cb))

    res = pl.pallas_call(
        body, name=name, grid=(nt,),
        in_specs=[pl.BlockSpec((tt, c), lambda i: (nt - 1 - i, 0))] + [cur_spec(off) for (_, off, _) in xs]
        + [prev_spec(off) for (_, off, _) in xs] + [pl.BlockSpec(w.shape, lambda i: (0, 0))],
        out_specs=[pl.BlockSpec((tt, c), lambda i: (nt - 1 - i, 0)) for _ in xs] + [pl.BlockSpec(w.shape, lambda i: (0, 0))],
        out_shape=[jax.ShapeDtypeStruct((t, c), dx_dtype) for _ in xs] + [jax.ShapeDtypeStruct(w.shape, F32)],
        scratch_shapes=[pltpu.VMEM((HALO, c), F32)],
        compiler_params=_cparams("arbitrary"),
    )(dy, *[x[0] for x in xs], *[x[0] for x in xs], w)
    return res[:nx], res[nx]


def _const_spec(a):
    return pl.BlockSpec(a.shape, lambda i: (0,) * a.ndim)


GDN_STEP = 2


def _gdn_rows(ref, c):
    return ref[c * GDN_CS:(c + 1) * GDN_CS, :]


def gdn_fwd(name, cq, proj, alog, dtb, nw):
    t = cq.shape[0]
    nc = t // GDN_CS
    ns = min(GDN_STEP, nc)
    rows = ns * GDN_CS

    def body(cq_ref, z_ref, ba_ref, alog_ref, dtb_ref, nw_ref, y_ref, ssave_ref, tsave_ref, s_scr):
        @pl.when(pl.program_id(0) == 0)
        def _():
            s_scr[...] = jnp.zeros_like(s_scr)

        s = s_scr[...]
        ys, saved, tinvs = [], [], []
        for c in range(ns):
            saved.append(s)
            y, s, tinv = _gdn_chunk(_gdn_rows(cq_ref, c), _gdn_rows(z_ref, c), _gdn_rows(ba_ref, c), s, alog_ref[...],
                                    dtb_ref[...], nw_ref[...])
            ys.append(y)
            tinvs.append(tinv)
        y_ref[...] = jnp.concatenate(ys, axis=0)
        ssave_ref[...] = jnp.stack(saved, axis=0)
        tsave_ref[...] = jnp.stack(tinvs, axis=0)
        s_scr[...] = s

    return pl.pallas_call(
        body, name=name, grid=(nc // ns,),
        in_specs=[pl.BlockSpec((rows, 3 * GW), lambda i: (i, 0)),
                  pl.BlockSpec((rows, GW), lambda i: (i, O_GZ // GW)),
                  pl.BlockSpec((rows, LANE), lambda i: (i, O_GBA // LANE)),
                  _const_spec(alog), _const_spec(dtb), _const_spec(nw)],
        out_specs=[pl.BlockSpec((rows, GW), lambda i: (i, 0)),
                   pl.BlockSpec((ns, LANE, N_GDN * LANE), lambda i: (i, 0, 0)),
                   pl.BlockSpec((ns, N_GDN * GDN_CS, N_GDN * GDN_CS), lambda i: (i, 0, 0))],
        out_shape=[jax.ShapeDtypeStruct((t, GW), F32), jax.ShapeDtypeStruct((nc, LANE, N_GDN * LANE), F32),
                   jax.ShapeDtypeStruct((nc, N_GDN * GDN_CS, N_GDN * GDN_CS), F32)],
        scratch_shapes=[pltpu.VMEM((LANE, N_GDN * LANE), F32)],
        compiler_params=_cparams("arbitrary"),
    )(cq, proj, proj, alog, dtb, nw)


def gdn_bwd(name, dy, dy_off, cq, proj, ssave, tsave, alog, dtb, nw):
    t = cq.shape[0]
    nc = t // GDN_CS
    ns = min(GDN_STEP, nc)
    rows = ns * GDN_CS

    def body(dy_ref, cq_ref, z_ref, ba_ref, ssave_ref, tsave_ref, alog_ref, dtb_ref, nw_ref,
             dcq_ref, dz_ref, dba_ref, dalog_ref, ddtb_ref, dnw_ref, ds_scr):
        @pl.when(pl.program_id(0) == 0)
        def _():
            ds_scr[...] = jnp.zeros_like(ds_scr)
            dalog_ref[...] = jnp.zeros_like(dalog_ref)
            ddtb_ref[...] = jnp.zeros_like(ddtb_ref)
            dnw_ref[...] = jnp.zeros_like(dnw_ref)

        ds = ds_scr[...]
        dcq, dzs, dbas = [None] * ns, [None] * ns, [None] * ns
        dal, ddt, dnw = (jnp.zeros(r.shape, F32) for r in (dalog_ref, ddtb_ref, dnw_ref))
        for c in reversed(range(ns)):
            _, vjp = jax.vjp(functools.partial(_gdn_chunk, tinv_saved=tsave_ref[c]), _gdn_rows(cq_ref, c),
                             _gdn_rows(z_ref, c), _gdn_rows(ba_ref, c), ssave_ref[c], alog_ref[...], dtb_ref[...],
                             nw_ref[...])
            dcq[c], dzs[c], dbas[c], ds, dal_c, ddt_c, dnw_c = vjp((_gdn_rows(dy_ref, c), ds))
            dal, ddt, dnw = dal + dal_c, ddt + ddt_c, dnw + dnw_c
        dcq_ref[...] = jnp.concatenate(dcq, axis=0)
        dz_ref[...] = jnp.concatenate(dzs, axis=0).astype(dz_ref.dtype)
        dba_ref[...] = jnp.concatenate(dbas, axis=0).astype(dba_ref.dtype)
        ds_scr[...] = ds
        dalog_ref[...] += dal
        ddtb_ref[...] += ddt
        dnw_ref[...] += dnw

    rev = lambda i: nc // ns - 1 - i
    return pl.pallas_call(
        body, name=name, grid=(nc // ns,),
        in_specs=[pl.BlockSpec((rows, GW), lambda i: (rev(i), dy_off // GW)),
                  pl.BlockSpec((rows, 3 * GW), lambda i: (rev(i), 0)),
                  pl.BlockSpec((rows, GW), lambda i: (rev(i), O_GZ // GW)),
                  pl.BlockSpec((rows, LANE), lambda i: (rev(i), O_GBA // LANE)),
                  pl.BlockSpec((ns, LANE, N_GDN * LANE), lambda i: (rev(i), 0, 0)),
                  pl.BlockSpec((ns, N_GDN * GDN_CS, N_GDN * GDN_CS), lambda i: (rev(i), 0, 0)),
                  _const_spec(alog), _const_spec(dtb), _const_spec(nw)],
        out_specs=[pl.BlockSpec((rows, 3 * GW), lambda i: (rev(i), 0)),
                   pl.BlockSpec((rows, GW), lambda i: (rev(i), 0)),
                   pl.BlockSpec((rows, LANE), lambda i: (rev(i), 0)),
                   _const_spec(alog), _const_spec(dtb), _const_spec(nw)],
        out_shape=[jax.ShapeDtypeStruct((t, 3 * GW), F32), jax.ShapeDtypeStruct((t, GW), BF16),
                   jax.ShapeDtypeStruct((t, LANE), BF16), jax.ShapeDtypeStruct(alog.shape, F32),
                   jax.ShapeDtypeStruct(dtb.shape, F32), jax.ShapeDtypeStruct(nw.shape, F32)],
        scratch_shapes=[pltpu.VMEM((LANE, N_GDN * LANE), F32)],
        compiler_params=_cparams("arbitrary"),
    )(dy, cq, proj, proj, ssave, tsave, alog, dtb, nw)


def ssd_fwd(name, cx, proj, convb, alog, dtb, dsk, nw):
    t = cx.shape[0]
    nc = t // SSD_CS

    def body(cx_ref, z_ref, dt_ref, cb_ref, alog_ref, dtb_ref, dsk_ref, nw_ref, y_ref, ssave_ref, s_scr):
        @pl.when(pl.program_id(0) == 0)
        def _():
            s_scr[...] = jnp.zeros_like(s_scr)

        s_in = s_scr[...]
        y, s_new = _ssd_chunk(cx_ref[...], z_ref[...], dt_ref[...], s_in, cb_ref[...], alog_ref[...], dtb_ref[...],
                              dsk_ref[...], nw_ref[...])
        y_ref[...] = y
        ssave_ref[0] = s_in
        s_scr[...] = s_new

    return pl.pallas_call(
        body, name=name, grid=(nc,),
        in_specs=[pl.BlockSpec((SSD_CS, 1024), lambda i: (i, 0)),
                  pl.BlockSpec((SSD_CS, GW), lambda i: (i, O_SZ // GW)),
                  pl.BlockSpec((SSD_CS, LANE), lambda i: (i, O_SDT // LANE)),
                  _const_spec(convb), _const_spec(alog), _const_spec(dtb), _const_spec(dsk), _const_spec(nw)],
        out_specs=[pl.BlockSpec((SSD_CS, GW), lambda i: (i, 0)),
                   pl.BlockSpec((1, LANE, GW), lambda i: (i, 0, 0))],
        out_shape=[jax.ShapeDtypeStruct((t, GW), F32), jax.ShapeDtypeStruct((nc, LANE, GW), F32)],
        scratch_shapes=[pltpu.VMEM((LANE, GW), F32)],
        compiler_params=_cparams("arbitrary"),
    )(cx, proj, proj, convb, alog, dtb, dsk, nw)


def ssd_bwd(name, dy, dy_off, cx, proj, ssave, convb, alog, dtb, dsk, nw):
    t = cx.shape[0]
    nc = t // SSD_CS

    def body(dy_ref, cx_ref, z_ref, dt_ref, ssave_ref, cb_ref, alog_ref, dtb_ref, dsk_ref, nw_ref,
             dcx_ref, dz_ref, ddt_ref, dcb_ref, dalog_ref, ddtb_ref, ddsk_ref, dnw_ref, ds_scr):
        @pl.when(pl.program_id(0) == 0)
        def _():
            ds_scr[...] = jnp.zeros_like(ds_scr)
            for r in (dcb_ref, dalog_ref, ddtb_ref, ddsk_ref, dnw_ref):
                r[...] = jnp.zeros_like(r)

        _, vjp = jax.vjp(_ssd_chunk, cx_ref[...], z_ref[...], dt_ref[...], ssave_ref[0], cb_ref[...], alog_ref[...],
                         dtb_ref[...], dsk_ref[...], nw_ref[...])
        dcx, dz, ddt, ds, dcb, dal, ddtb, ddsk, dnw = vjp((dy_ref[...], ds_scr[...]))
        dcx_ref[...] = dcx
        dz_ref[...] = dz.astype(dz_ref.dtype)
        ddt_ref[...] = ddt.astype(ddt_ref.dtype)
        ds_scr[...] = ds
        dcb_ref[...] += dcb
        dalog_ref[...] += dal
        ddtb_ref[...] += ddtb
        ddsk_ref[...] += ddsk
        dnw_ref[...] += dnw

    rev = lambda i: nc - 1 - i
    return pl.pallas_call(
        body, name=name, grid=(nc,),
        in_specs=[pl.BlockSpec((SSD_CS, GW), lambda i: (rev(i), dy_off // GW)),
                  pl.BlockSpec((SSD_CS, 1024), lambda i: (rev(i), 0)),
                  pl.BlockSpec((SSD_CS, GW), lambda i: (rev(i), O_SZ // GW)),
                  pl.BlockSpec((SSD_CS, LANE), lambda i: (rev(i), O_SDT // LANE)),
                  pl.BlockSpec((1, LANE, GW), lambda i: (rev(i), 0, 0)),
                  _const_spec(convb), _const_spec(alog), _const_spec(dtb), _const_spec(dsk), _const_spec(nw)],
        out_specs=[pl.BlockSpec((SSD_CS, 1024), lambda i: (rev(i), 0)),
                   pl.BlockSpec((SSD_CS, GW), lambda i: (rev(i), 0)),
                   pl.BlockSpec((SSD_CS, LANE), lambda i: (rev(i), 0)),
                   _const_spec(convb), _const_spec(alog), _const_spec(dtb), _const_spec(dsk), _const_spec(nw)],
        out_shape=[jax.ShapeDtypeStruct((t, 1024), F32), jax.ShapeDtypeStruct((t, GW), BF16),
                   jax.ShapeDtypeStruct((t, LANE), BF16), jax.ShapeDtypeStruct(convb.shape, F32),
                   jax.ShapeDtypeStruct(alog.shape, F32), jax.ShapeDtypeStruct(dtb.shape, F32),
                   jax.ShapeDtypeStruct(dsk.shape, F32), jax.ShapeDtypeStruct(nw.shape, F32)],
        scratch_shapes=[pltpu.VMEM((LANE, GW), F32)],
        compiler_params=_cparams("arbitrary"),
    )(dy, cx, proj, proj, ssave, convb, alog, dtb, dsk, nw)


def _swa_specs(nb, rev):
    w = ATT_W
    cur = lambda i: (nb - 1 - i) if rev else i
    prv = lambda i: jnp.maximum(cur(i) - 1, 0)
    return [pl.BlockSpec((w, 1024), lambda i: (cur(i), O_AQ // 1024)),
            pl.BlockSpec((w, 256), lambda i: (cur(i), O_AK // 256)),
            pl.BlockSpec((w, 256), lambda i: (prv(i), O_AK // 256)),
            pl.BlockSpec((w, 256), lambda i: (cur(i), O_AV // 256)),
            pl.BlockSpec((w, 256), lambda i: (prv(i), O_AV // 256)),
            pl.BlockSpec((w, LANE), lambda i: (cur(i), 0)), pl.BlockSpec((w, LANE), lambda i: (prv(i), 0)),
            pl.BlockSpec((w, LANE), lambda i: (cur(i), 0)), pl.BlockSpec((w, LANE), lambda i: (prv(i), 0))]


def swa_fwd(name, proj, cos, sin, sinks):
    t = proj.shape[0]
    nb = t // ATT_W

    def body(q_ref, kc_ref, kp_ref, vc_ref, vp_ref, cc_ref, cp_ref, sc_ref, sp_ref, snk_ref, y_ref):
        notfirst = jnp.where(pl.program_id(0) > 0, 1.0, 0.0).astype(F32)
        cc, cp, sc, sp = cc_ref[...], cp_ref[...], sc_ref[...], sp_ref[...]
        outs = []
        for g in range(2):
            sl = slice(g * LANE, (g + 1) * LANE)
            qg = jnp.concatenate([_rope(q_ref[:, (4 * g + r) * LANE:(4 * g + r + 1) * LANE], cc, sc)
                                  for r in range(4)], axis=1)
            outs.append(_swa_block(qg, _rope(kp_ref[:, sl], cp, sp), _rope(kc_ref[:, sl], cc, sc), vp_ref[:, sl],
                                   vc_ref[:, sl], snk_ref[...], notfirst, g=g))
        y_ref[...] = jnp.concatenate(outs, axis=1)

    return pl.pallas_call(
        body, name=name, grid=(nb,),
        in_specs=_swa_specs(nb, False) + [_const_spec(sinks)],
        out_specs=pl.BlockSpec((ATT_W, 1024), lambda i: (i, 0)),
        out_shape=jax.ShapeDtypeStruct((t, 1024), F32),
        compiler_params=_cparams("arbitrary"),
    )(proj, proj, proj, proj, proj, cos, cos, sin, sin, sinks)


def swa_bwd(name, dy, dy_off, proj, cos, sin, sinks):
    t = proj.shape[0]
    nb = t // ATT_W

    def body(dy_ref, q_ref, kc_ref, kp_ref, vc_ref, vp_ref, cc_ref, cp_ref, sc_ref, sp_ref, snk_ref,
             dq_ref, dk_ref, dv_ref, dsnk_ref, ck_scr, cv_scr):
        i = pl.program_id(0)

        @pl.when(i == 0)
        def _():
            ck_scr[...] = jnp.zeros_like(ck_scr)
            cv_scr[...] = jnp.zeros_like(cv_scr)
            dsnk_ref[...] = jnp.zeros_like(dsnk_ref)

        notfirst = jnp.where(i < nb - 1, 1.0, 0.0).astype(F32)
        cc, cp, sc, sp = cc_ref[...], cp_ref[...], sc_ref[...], sp_ref[...]
        ck_in, cv_in = ck_scr[...], cv_scr[...]
        dq, dk, dv, ck, cv, dsnk = [], [], [], [], [], jnp.zeros(dsnk_ref.shape, F32)
        for g in range(2):
            sl = slice(g * LANE, (g + 1) * LANE)
            qg = jnp.concatenate([_rope(q_ref[:, (4 * g + r) * LANE:(4 * g + r + 1) * LANE], cc, sc)
                                  for r in range(4)], axis=1)
            _, vjp = jax.vjp(functools.partial(_swa_block, g=g), qg, _rope(kp_ref[:, sl], cp, sp),
                             _rope(kc_ref[:, sl], cc, sc), vp_ref[:, sl], vc_ref[:, sl], snk_ref[...], notfirst)
            dqg, dkp, dkc, dvp, dvc, dsnk_g, _ = vjp(dy_ref[:, g * GW:(g + 1) * GW])
            dq += [_unrope(dqg[:, r * LANE:(r + 1) * LANE], cc, sc) for r in range(4)]
            dk.append(_unrope(dkc, cc, sc) + ck_in[:, sl])
            dv.append(dvc + cv_in[:, sl])
            ck.append(_unrope(dkp, cp, sp))
            cv.append(dvp)
            dsnk = dsnk + dsnk_g
        dq_ref[...] = jnp.concatenate(dq, axis=1).astype(dq_ref.dtype)
        dk_ref[...] = jnp.concatenate(dk, axis=1).astype(dk_ref.dtype)
        dv_ref[...] = jnp.concatenate(dv, axis=1).astype(dv_ref.dtype)
        ck_scr[...] = jnp.concatenate(ck, axis=1)
        cv_scr[...] = jnp.concatenate(cv, axis=1)
        dsnk_ref[...] += dsnk

    rev = lambda i: nb - 1 - i
    return pl.pallas_call(
        body, name=name, grid=(nb,),
        in_specs=[pl.BlockSpec((ATT_W, 1024), lambda i: (rev(i), dy_off // 1024))] + _swa_specs(nb, True)
        + [_const_spec(sinks)],
        out_specs=[pl.BlockSpec((ATT_W, 1024), lambda i: (rev(i), 0)),
                   pl.BlockSpec((ATT_W, 256), lambda i: (rev(i), 0)),
                   pl.BlockSpec((ATT_W, 256), lambda i: (rev(i), 0)), _const_spec(sinks)],
        out_shape=[jax.ShapeDtypeStruct((t, 1024), BF16), jax.ShapeDtypeStruct((t, 256), BF16),
                   jax.ShapeDtypeStruct((t, 256), BF16), jax.ShapeDtypeStruct(sinks.shape, F32)],
        scratch_shapes=[pltpu.VMEM((ATT_W, 256), F32), pltpu.VMEM((ATT_W, 256), F32)],
        compiler_params=_cparams("arbitrary"),
    )(dy, proj, proj, proj, proj, proj, cos, cos, sin, sin, sinks)


def _me():
    return lax.axis_index("x"), lax.axis_index("y"), lax.axis_index("c")


ANY = pl.BlockSpec(memory_space=pl.ANY)


def allgather8(name, buf):
    r = buf.shape[0]

    def body(x_ref, out_ref, send_sems, recv_sems, local_sem):
        x, y, c = _me()
        me = 4 * x + 2 * y + c
        mine = pltpu.make_async_copy(x_ref, out_ref.at[me], local_sem)
        mine.start()
        copies = []
        for k in range(1, 8):
            bx, by, bc = (k >> 2) & 1, (k >> 1) & 1, k & 1
            to = ((1 - x) if bx else x, (1 - y) if by else y, (1 - c) if bc else c)
            cp = pltpu.make_async_remote_copy(src_ref=x_ref, dst_ref=out_ref.at[me], send_sem=send_sems.at[k - 1],
                                              recv_sem=recv_sems.at[k - 1], device_id=to, device_id_type=MESH)
            cp.start()
            copies.append(cp)
        for cp in copies:
            cp.wait_recv()
        for cp in copies:
            cp.wait_send()
        mine.wait()

    return pl.pallas_call(
        body, name=name, out_shape=jax.ShapeDtypeStruct((8, r, LANE), buf.dtype),
        in_specs=[pl.BlockSpec(memory_space=pltpu.VMEM)], out_specs=pl.BlockSpec(memory_space=pltpu.VMEM),
        scratch_shapes=[pltpu.SemaphoreType.DMA((7,)), pltpu.SemaphoreType.DMA((7,)), pltpu.SemaphoreType.DMA],
        compiler_params=pltpu.CompilerParams(vmem_limit_bytes=VMEM_LIMIT),
    )(buf)


def _sems(n):
    return [pltpu.SemaphoreType.DMA((n,)), pltpu.SemaphoreType.DMA((n,))]


def allgather_chips(name, shards):
    n = len(shards)

    def body(*refs):
        x_refs, out_refs, send_sems, recv_sems = refs[:n], refs[n:2 * n], refs[2 * n], refs[2 * n + 1]
        x, y, c = _me()
        chip = 2 * x + y
        chips = [(1 - x, y), (x, 1 - y), (1 - x, 1 - y)]

        def rc(src, dst, k, to):
            return pltpu.make_async_remote_copy(src_ref=src, dst_ref=dst, send_sem=send_sems.at[k],
                                                recv_sem=recv_sems.at[k], device_id=to, device_id_type=MESH)

        halves = [pl.ds(pl.multiple_of(c * (s.shape[0] // 2), 16), s.shape[0] // 2) for s in shards]
        own, first = [], {}
        for i in range(n):
            cp = rc(x_refs[i], out_refs[i].at[chip], 7 * i + 6, (x, y, 1 - c))
            cp.start()
            own.append(cp)
            for k, (cx, cy) in enumerate(chips):
                cp = rc(x_refs[i].at[halves[i]], out_refs[i].at[chip, halves[i]], 7 * i + k, (cx, cy, c))
                cp.start()
                first[i, k] = cp
        passed = []
        for i in range(n):
            for k, (cx, cy) in enumerate(chips):
                first[i, k].wait_recv()
                src = out_refs[i].at[2 * cx + cy, halves[i]]
                cp = rc(src, src, 7 * i + 3 + k, (x, y, 1 - c))
                cp.start()
                passed.append(cp)
        for cp in passed + own:
            cp.wait_recv()
        for cp in list(first.values()) + passed + own:
            cp.wait_send()

    for s in shards:
        assert s.shape[0] % 32 == 0
    return pl.pallas_call(
        body, name=name, out_shape=[jax.ShapeDtypeStruct((4,) + s.shape, s.dtype) for s in shards],
        in_specs=[ANY] * n, out_specs=[ANY] * n, scratch_shapes=_sems(7 * n),
    )(*shards)


def ag_forward(name, bufs):
    n = len(bufs)

    def body(*refs):
        out_refs, send_sems, recv_sems = refs[n:2 * n], refs[2 * n], refs[2 * n + 1]
        x, y, c = _me()
        cps = []
        for i in range(n):
            rh = bufs[i].shape[1] // 2
            half = pl.ds(pl.multiple_of(c * rh, 16), rh)
            for k, (cx, cy) in enumerate([(1 - x, y), (x, 1 - y), (1 - x, 1 - y)]):
                src = out_refs[i].at[2 * cx + cy, half]
                cps.append(pltpu.make_async_remote_copy(src_ref=src, dst_ref=src, send_sem=send_sems.at[3 * i + k],
                                                        recv_sem=recv_sems.at[3 * i + k], device_id=(x, y, 1 - c),
                                                        device_id_type=MESH))
        for cp in cps:
            cp.start()
        for cp in cps:
            cp.wait_recv()
        for cp in cps:
            cp.wait_send()

    return pl.pallas_call(
        body, name=name, out_shape=[jax.ShapeDtypeStruct(b.shape, b.dtype) for b in bufs],
        in_specs=[ANY] * n, out_specs=[ANY] * n, scratch_shapes=_sems(3 * n),
        input_output_aliases={i: i for i in range(n)},
    )(*bufs)


def sibling_swap_halves(name, gs):
    n = len(gs)

    def body(*refs):
        g_refs, out_refs, send_sems, recv_sems = refs[:n], refs[n:2 * n], refs[2 * n], refs[2 * n + 1]
        x, y, c = _me()
        cps = [pltpu.make_async_remote_copy(src_ref=g_refs[i].at[:, 1 - c], dst_ref=out_refs[i],
                                            send_sem=send_sems.at[i], recv_sem=recv_sems.at[i],
                                            device_id=(x, y, 1 - c), device_id_type=MESH) for i in range(n)]
        for cp in cps:
            cp.start()
        for cp in cps:
            cp.wait_recv()
        for cp in cps:
            cp.wait_send()

    return pl.pallas_call(
        body, name=name, out_shape=[jax.ShapeDtypeStruct((4,) + g.shape[2:], g.dtype) for g in gs],
        in_specs=[ANY] * n, out_specs=[ANY] * n, scratch_shapes=_sems(n),
    )(*gs)


def chips_exchange(name, ps):
    n = len(ps)

    def body(*refs):
        p_refs, out_refs, send_sems, recv_sems = refs[:n], refs[n:2 * n], refs[2 * n], refs[2 * n + 1]
        x, y, c = _me()
        chip = 2 * x + y
        cps = []
        for i in range(n):
            for k, (cx, cy) in enumerate([(1 - x, y), (x, 1 - y), (1 - x, 1 - y)]):
                cps.append(pltpu.make_async_remote_copy(
                    src_ref=p_refs[i].at[2 * cx + cy], dst_ref=out_refs[i].at[chip], send_sem=send_sems.at[3 * i + k],
                    recv_sem=recv_sems.at[3 * i + k], device_id=(cx, cy, c), device_id_type=MESH))
        for cp in cps:
            cp.start()
        for cp in cps:
            cp.wait_recv()
        for cp in cps:
            cp.wait_send()

    return pl.pallas_call(
        body, name=name, out_shape=[jax.ShapeDtypeStruct(p.shape, p.dtype) for p in ps],
        in_specs=[ANY] * n, out_specs=[ANY] * n, scratch_shapes=_sems(3 * n),
    )(*ps)


def sibling_fill(name, ss):
    n = len(ss)

    def body(*refs):
        out_refs, send_sems, recv_sems = refs[n:2 * n], refs[2 * n], refs[2 * n + 1]
        x, y, c = _me()
        cps = []
        for i in range(n):
            rh = ss[i].shape[1] // 2
            mine = out_refs[i].at[:, pl.ds(pl.multiple_of(c * rh, 8), rh)]
            cps.append(pltpu.make_async_remote_copy(src_ref=mine, dst_ref=mine, send_sem=send_sems.at[i],
                                                    recv_sem=recv_sems.at[i], device_id=(x, y, 1 - c),
                                                    device_id_type=MESH))
        for cp in cps:
            cp.start()
        for cp in cps:
            cp.wait_recv()
        for cp in cps:
            cp.wait_send()

    return pl.pallas_call(
        body, name=name, out_shape=[jax.ShapeDtypeStruct(s.shape, s.dtype) for s in ss],
        in_specs=[ANY] * n, out_specs=[ANY] * n, scratch_shapes=_sems(n),
        input_output_aliases={i: i for i in range(n)},
    )(*ss)


def _rs_tile(rh, w):
    return _pick(rh, max(8, (1 << 19) // w // 8 * 8), 8)


def add_half(name, g, recv, idx):
    _, _, rh, w = g.shape
    tile = _rs_tile(rh, w)

    def body(s_ref, g_ref, r_ref, o_ref):
        o_ref[...] = (g_ref[...] + r_ref[...]).astype(o_ref.dtype)

    return pl.pallas_call(
        body, name=name,
        grid_spec=pltpu.PrefetchScalarGridSpec(
            num_scalar_prefetch=1, grid=(4, rh // tile),
            in_specs=[pl.BlockSpec((None, None, tile, w), lambda s, i, ix: (s, ix[1], i, 0)),
                      pl.BlockSpec((None, tile, w), lambda s, i, ix: (s, i, 0))],
            out_specs=pl.BlockSpec((None, tile, w), lambda s, i, ix: (s, i, 0))),
        out_shape=jax.ShapeDtypeStruct((4, rh, w), BF16),
        compiler_params=_cparams("parallel", "parallel"),
    )(idx, g, recv)


def rs_sum(name, g, recv1, recv2, idx, layer, acc=None):
    _, _, rh, w = g.shape
    tile = _rs_tile(rh, w)
    nt = rh // tile

    def body(s_ref, g_ref, r1_ref, a_ref, b_ref, c_ref, *rest):
        rest[-1][...] = (g_ref[...] + r1_ref[...]) + (a_ref[...].astype(F32) + b_ref[...].astype(F32)
                                                       + c_ref[...].astype(F32))

    def other(k):
        return pl.BlockSpec((None, tile, w), lambda i, ix: (jnp.bitwise_xor(ix[0], k), i, 0))

    ops = [idx, g, recv1, recv2, recv2, recv2] + ([acc] if acc is not None else [])
    return pl.pallas_call(
        body, name=name,
        grid_spec=pltpu.PrefetchScalarGridSpec(
            num_scalar_prefetch=1, grid=(rh // tile,),
            in_specs=[pl.BlockSpec((None, None, tile, w), lambda i, ix: (ix[0], ix[1], i, 0)),
                      pl.BlockSpec((None, tile, w), lambda i, ix: (ix[0], i, 0)), other(1), other(2), other(3)]
            + ([ANY] if acc is not None else []),
            out_specs=pl.BlockSpec((None, tile, w), lambda i, ix: (layer, ix[1] * nt + i, 0))),
        out_shape=jax.ShapeDtypeStruct((DEPTH, 2 * rh, w), F32),
        input_output_aliases={6: 0} if acc is not None else {},
        compiler_params=_cparams("parallel"),
    )(*ops)


def sum_slabs(name, a):
    n, r, _ = a.shape
    tile = _pick(r, 2048, 8)

    def body(a_ref, o_ref):
        acc = a_ref[0]
        for s in range(1, n):
            acc = acc + a_ref[s]
        o_ref[...] = acc

    return pl.pallas_call(
        body, name=name, grid=(r // tile,),
        in_specs=[pl.BlockSpec((n, tile, LANE), lambda i: (0, i, 0))],
        out_specs=pl.BlockSpec((tile, LANE), lambda i: (i, 0)),
        out_shape=jax.ShapeDtypeStruct((r, LANE), F32),
        compiler_params=_cparams("parallel"),
    )(a)


def f_premod(rv, cv):
    (x,), (w, scale, shift) = rv, cv
    y = x * lax.rsqrt(jnp.mean(x * x, -1, keepdims=True) + EPS) * w
    return [y * (1.0 + scale) + shift]


def f_postres(rv, cv):
    (x, y), (w, gate) = rv, cv
    return [x + gate * (y * lax.rsqrt(jnp.mean(y * y, -1, keepdims=True) + EPS) * w)]


def f_res_premod(rv, cv):
    (x, y), (w_post, gate, w_pre, scale, shift) = rv, cv
    x1, = f_postres([x, y], [w_post, gate])
    h, = f_premod([x1], [w_pre, scale, shift])
    return [x1, h]


def f_adamw(rv, cv):
    w, g, m, v = rv
    m = B1 * m + (1.0 - B1) * g
    v = B2 * v + (1.0 - B2) * jnp.square(g)
    m_hat = m / (1.0 - B1 ** STEP)
    v_hat = v / (1.0 - B2 ** STEP)
    return [-LR * (m_hat / (jnp.sqrt(v_hat) + AEPS) + WD * w), m, v], []


def adamw_layers(name, w, g, m, v):
    nl, r, wd = w.shape
    tile = r
    while tile * wd * 4 * 16 > 24 * 1024 * 1024 and tile % 16 == 0:
        tile //= 2

    outs = None
    for l in range(nl):
        def body(w_ref, g_ref, m_ref, v_ref, *rest):
            gv = g_ref[...]
            (dl, m2, v2), _ = f_adamw([w_ref[...], gv, m_ref[...], v_ref[...]], [])
            for ref, val in zip(rest[-4:], (gv, dl, m2, v2)):
                ref[...] = val

        blk = pl.BlockSpec((None, tile, wd), lambda i, l=l: (l, i, 0))
        outs = pl.pallas_call(
            body, name=f"{name}{l}", grid=(r // tile,), in_specs=[blk] * 4 + ([ANY] * 4 if outs else []),
            out_specs=[blk] * 4, out_shape=[jax.ShapeDtypeStruct(w.shape, F32)] * 4,
            input_output_aliases={4: 0, 5: 1, 6: 2, 7: 3} if outs else {}, compiler_params=_cparams("parallel"),
        )(w, g, m, v, *(outs or ()))
    return outs


def adamw(name, w, g, m, v):
    r, wd = w.shape
    tile = r
    while tile * wd * 4 * 14 > 24 * 1024 * 1024 and tile % 16 == 0:
        tile //= 2
    (d, m2, v2), _ = rowmap(name, f_adamw, [(a, 0, wd) for a in (w, g, m, v)], [], [(wd, F32)] * 3, tile=tile)
    return d, m2, v2


_IN_SIZES = (512, 512, 512, 512, 4, 4, 512, 1024, 8, 512, 128, 128, 512, 512, 512)


def _relayout_w_in(w):
    d = w.shape[0]
    s = np.cumsum((0,) + _IN_SIZES)
    gq, gk, gv, gz, gb, ga, sz, sx, sdt, aq, ak, av, cb, cc, ch = [w[:, s[i]:s[i + 1]] for i in range(15)]

    def rope_pad(t, nh):
        return jnp.pad(t.reshape(d, nh, 2, 32), ((0, 0), (0, 0), (0, 0), (0, 32))).reshape(d, nh * LANE)

    def v_pad(t, nh):
        return jnp.pad(t.reshape(d, nh, 64), ((0, 0), (0, 0), (0, 64))).reshape(d, nh * LANE)

    def pad128(t):
        return jnp.pad(t, ((0, 0), (0, LANE - t.shape[1])))

    return jnp.concatenate([gq, gk, gv, cb, cc, ch, sx, rope_pad(aq, 8), gz, sz, rope_pad(ak, 2), v_pad(av, 2),
                            pad128(jnp.concatenate([gb, ga], axis=1)), pad128(sdt)], axis=1)


def _unlayout_g_in(g):
    d = g.shape[0]

    def rope_unpad(t, nh):
        return t.reshape(d, nh, 2, 64)[..., :32].reshape(d, nh * 64)

    def v_unpad(t, nh):
        return t.reshape(d, nh, LANE)[..., :64].reshape(d, nh * 64)

    return jnp.concatenate([g[:, 0:1536], g[:, O_GZ:O_GZ + 512], g[:, O_GBA:O_GBA + 8], g[:, O_SZ:O_SZ + 512],
                            g[:, O_XBC:O_XBC + 1024], g[:, O_SDT:O_SDT + 8], rope_unpad(g[:, O_AQ:O_AQ + 1024], 8),
                            rope_unpad(g[:, O_AK:O_AK + 256], 2), v_unpad(g[:, O_AV:O_AV + 256], 2),
                            g[:, O_CB:O_CB + 1536]], axis=1)


def _relayout_w_out(w):
    d = w.shape[1]
    yc = jnp.pad(w[1024:1536].reshape(8, 64, d), ((0, 0), (0, 64), (0, 0))).reshape(1024, d)
    return jnp.concatenate([w[:1024], yc, w[1536:]], axis=0)


def _unlayout_g_out(g):
    d = g.shape[1]
    return jnp.concatenate([g[:1024], g[1024:2048].reshape(8, LANE, d)[:, :64].reshape(512, d), g[2048:]], axis=0)


def _pack(arrs):
    rows, spec, off = [], [], 0
    for a in arrs:
        n = int(np.prod(a.shape))
        nr = -(-n // LANE)
        rows.append(jnp.pad(a.reshape(-1).astype(F32), (0, nr * LANE - n)).reshape(nr, LANE))
        spec.append((off, a.shape))
        off += nr
    pad = (-off) % 8
    if pad:
        rows.append(jnp.zeros((pad, LANE), F32))
    return jnp.concatenate(rows, axis=0), spec


def _unpack(buf, spec):
    out = []
    for off, shape in spec:
        n = int(np.prod(shape))
        nr = -(-n // LANE)
        out.append(buf[off:off + nr].reshape(-1)[:n].reshape(shape))
    return out


def _row128(v):
    return jnp.pad(v.astype(F32), (0, LANE - v.shape[0])).reshape(1, LANE)


BIG = ("w_in", "w_out", "w_up", "w_down")
SMALL_REPL = ("ada_b", "norm_pre_mix", "norm_post_mix", "norm_pre_mlp", "norm_post_mlp", "gdn_a_log", "gdn_dt_bias",
              "gdn_norm_w", "ssm_conv_b", "ssm_a_log", "ssm_dt_bias", "ssm_d", "ssm_norm_w", "attn_sinks")
SMALL_SHARD = ("gdn_conv_w", "ssm_conv_w", "sc_conv_w")
WEIGHTS = ("ada_w", "ada_b", "norm_pre_mix", "norm_post_mix", "norm_pre_mlp", "norm_post_mlp", "w_in", "w_out",
           "gdn_conv_w", "gdn_a_log", "gdn_dt_bias", "gdn_norm_w", "ssm_conv_w", "ssm_conv_b", "ssm_a_log",
           "ssm_dt_bias", "ssm_d", "ssm_norm_w", "attn_sinks", "sc_conv_w", "w_up", "w_down")


def _step(p, m, v, x, c, positions, loss_target):
    d = x.shape[2]
    xi, yi, ci = _me()
    chip = 2 * xi + yi
    me = 4 * xi + 2 * yi + ci
    x0 = x[0]
    tgt = loss_target[0]

    buf, spec = _pack([c[0], p["gdn_conv_w"], p["ssm_conv_w"], p["sc_conv_w"]])
    got = allgather8("ag_small_in", buf)
    per_dev = [_unpack(got[k], spec) for k in range(8)]
    c_all = jnp.stack([pd[0] for pd in per_dev], axis=0)
    conv_full = [jnp.concatenate([per_dev[4 * a + 2 * b][1 + j] for a in range(2) for b in range(2)], axis=-1)
                 for j in range(3)]
    gdn_conv_w, ssm_conv_w, sc_conv_w = conv_full

    c_pad = jnp.pad(c_all, ((0, 8), (0, 0)))
    silu_bf = lambda a: _silu(a).astype(BF16)
    nsh = p["ada_w"].shape[2]
    mods = []
    for l in range(DEPTH):
        mo = mm_nn(f"ada_fwd{l}", c_pad, p["ada_w"][l], F32, a_fn=silu_bf)
        mods.append(mo[:8] + lax.dynamic_slice(p["ada_b"][l], (chip * nsh,), (nsh,))[None])
    buf, spec = _pack([jnp.stack(mods, axis=0)])
    got = allgather8("ag_mod", buf)
    mod_sh = [_unpack(got[4 * a + 2 * b], spec)[0] for a in range(2) for b in range(2)]
    mod_all = jnp.concatenate(mod_sh, axis=-1)
    mod = lax.dynamic_index_in_dim(mod_all, me, axis=1, keepdims=False)

    items = [(l, n) for l in range(DEPTH) for n in BIG]
    shards = {(l, n): p[n][l].astype(BF16) for (l, n) in items}
    gathered = {(0, "w_in"): allgather_chips("ag_w_in0", [shards[0, "w_in"]])[0]}

    def gather_behind(keys):
        return ("gather", [shards[kk] for kk in keys])

    def settle(name, keys, landed):
        gathered.update(zip(keys, ag_forward(name, landed)))

    wfull = []

    inv_freq = ROPE_THETA ** (-jnp.arange(0, 64, 2, dtype=F32) / 64)
    ang = positions[0].astype(F32)[:, None] * inv_freq
    z32 = jnp.zeros_like(ang)
    cos_t = jnp.concatenate([jnp.cos(ang), z32, jnp.cos(ang), z32], axis=1)
    sin_t = jnp.concatenate([-jnp.sin(ang), z32, jnp.sin(ang), z32], axis=1)

    relu2 = lambda a: jnp.square(jnp.maximum(a, 0))

    def layer_consts(l):
        row = lambda v_: v_.reshape(1, -1).astype(F32)
        return dict(
            alog=_row128(p["gdn_a_log"][l]), gdtb=_row128(p["gdn_dt_bias"][l]), gnw=row(p["gdn_norm_w"][l]),
            scb=row(p["ssm_conv_b"][l]), salog=_row128(p["ssm_a_log"][l]), sdtb=_row128(p["ssm_dt_bias"][l]),
            sdsk=_row128(p["ssm_d"][l]), snw=row(p["ssm_norm_w"][l]), sinks=_row128(p["attn_sinks"][l]))

    saved = []
    xc = x0
    for l in range(DEPTH):
        w = dict(w_in=_relayout_w_in(jnp.concatenate([gathered[l, "w_in"][s] for s in range(4)], axis=1)))
        wfull.append(w)
        k = layer_consts(l)
        md = [mod[l, j * d:(j + 1) * d].reshape(1, d) for j in range(6)]
        npre, npost = p["norm_pre_mix"][l].reshape(1, d), p["norm_post_mix"][l].reshape(1, d)
        npre2, npost2 = p["norm_pre_mlp"][l].reshape(1, d), p["norm_post_mlp"][l].reshape(1, d)
        if l == 0:
            (h,), _ = rowmap("premix0", lambda rv, cv: (f_premod(rv, cv), []), [(xc, 0, d)], [npre, md[1], md[0]],
                             [(d, BF16)])
        behind = [(0, "w_out"), (0, "w_up")] if l == 0 else [(l, "w_up")]
        proj, landed = mm_nn(f"inproj{l}", h, w["w_in"], F32, carry=gather_behind(behind))
        settle(f"ag_forward_a{l}", behind, landed)
        w["w_out"] = _relayout_w_out(gathered[l, "w_out"].reshape(-1, d))
        w["w_up"] = gathered[l, "w_up"]
        cq = conv_fwd(f"gdn_conv{l}", [(proj, O_GQKV, 1536)], gdn_conv_w[l])
        ya, gs, gt = gdn_fwd(f"gdn{l}", cq, proj, k["alog"], k["gdtb"], k["gnw"])
        cx = conv_fwd(f"ssm_conv{l}", [(proj, O_XBC, 1024)], ssm_conv_w[l])
        yb, ss = ssd_fwd(f"ssd{l}", cx, proj, k["scb"], k["salog"], k["sdtb"], k["sdsk"], k["snw"])
        yc = swa_fwd(f"swa{l}", proj, cos_t, sin_t, k["sinks"])
        cd = conv_fwd(f"sc_conv{l}", [(proj, O_CC, GW), (proj, O_CH, GW)], sc_conv_w[l])
        (ycat,), _ = rowmap(f"ycat{l}", lambda rv, cv: ([jnp.concatenate([rv[0], rv[1], rv[2], rv[3] * rv[4]], 1)], []),
                            [(ya, 0, GW), (yb, 0, GW), (yc, 0, 1024), (proj, O_CB, GW), (cd, 0, GW)], [],
                            [(YCAT, BF16)])
        y = mm_nn(f"outproj{l}", ycat, w["w_out"], F32)
        (x1, h2), _ = rowmap(f"mix_mlp{l}", lambda rv, cv: (f_res_premod(rv, cv), []), [(xc, 0, d), (y, 0, d)],
                             [npost, md[2], npre2, md[4], md[3]], [(d, F32), (d, BF16)], tile=128)
        u, landed = mm_nn(f"up{l}", h2, w["w_up"], BF16, carry=gather_behind([(l, "w_down")]))
        settle(f"ag_forward_b{l}", [(l, "w_down")], landed)
        w["w_down"] = gathered[l, "w_down"].reshape(-1, d)
        if l + 1 < DEPTH:
            behind = [(l + 1, "w_in"), (l + 1, "w_out")]
            y2, landed = mm_nn(f"down{l}", u, w["w_down"], F32, a_fn=relu2, carry=gather_behind(behind))
            settle(f"ag_forward_c{l}", behind, landed)
        else:
            y2 = mm_nn(f"down{l}", u, w["w_down"], F32, a_fn=relu2)
        saved.append(dict(x=xc, h=h, proj=proj, cq=cq, gs=gs, gt=gt, cx=cx, ss=ss, cd=cd, ycat=ycat, y=y, x1=x1, h2=h2, u=u,
                          y2=y2, md=md, k=k))
        if l + 1 < DEPTH:
            (xc, h), _ = rowmap(f"mlp_mix{l}", lambda rv, cv: (f_res_premod(rv, cv), []), [(x1, 0, d), (y2, 0, d)],
                                [npost2, md[5], p["norm_pre_mix"][l + 1].reshape(1, d),
                                 mod[l + 1, d:2 * d].reshape(1, d), mod[l + 1, 0:d].reshape(1, d)],
                                [(d, F32), (d, BF16)], tile=128)

    def f_loss(vals, cvals):
        x1_, y2_, tgt_ = vals
        (x2,), vjp = jax.vjp(lambda r, c: f_postres(list(r), c), [x1_, y2_], cvals)
        err = x2 - tgt_
        part = 0.5 * jnp.sum(jnp.mean(err * err, axis=-1, keepdims=True), axis=0, keepdims=True)
        dr, dc = vjp([err * (1.0 / d)])
        return list(dr), list(dc) + [jnp.broadcast_to(part, (1, LANE))]

    s = saved[-1]
    (dx1a, dy2), (dnpost2, dgate_m, loss_part) = rowmap(
        "loss_postmlp_b", f_loss, [(s["x1"], 0, d), (s["y2"], 0, d), (tgt, 0, d)],
        [p["norm_post_mlp"][DEPTH - 1].reshape(1, d), s["md"][5]], [(d, F32), (d, BF16)],
        [(1, d), (1, d), (1, LANE)], tile=128)

    idx = jnp.stack([chip, ci]).astype(jnp.int32)
    g4s, recv1s, recv2s, waiting = {}, {}, {}, []

    def reduce_start(key, g):
        l_, n_ = key
        g4 = g.reshape(4, 2, g.shape[1] // 2, g.shape[2])
        r1 = sibling_swap_halves(f"rs_sibling_{n_}{l_}", [g4])[0]
        g4s[key], recv1s[key] = g4, r1
        waiting.append((key, add_half(f"rs_add_{n_}{l_}", g4, r1, idx)))

    def mm_carrying(name, *args, **kw):
        if not waiting:
            return mm_nn(name, *args, **kw)
        out, landed = mm_nn(name, *args, carry=("exchange", [pt for _, pt in waiting]), **kw)
        recv2s.update(zip([kk for kk, _ in waiting], landed))
        waiting.clear()
        return out

    gsm, dmd = {}, {}
    for l in reversed(range(DEPTH)):
        w, s = wfull[l], saved[l]
        k, md = s["k"], s["md"]
        npre, npost = p["norm_pre_mix"][l].reshape(1, d), p["norm_post_mix"][l].reshape(1, d)
        npre2, npost2 = p["norm_pre_mlp"][l].reshape(1, d), p["norm_post_mlp"][l].reshape(1, d)
        if l == DEPTH - 1:
            gsm[l, "norm_post_mlp"], dmd[l, 5] = dnpost2[0], dgate_m
        du = mm_carrying(f"down_b{l}", dy2, w["w_down"], BF16, extra=s["u"], wt=True,
                         epi=lambda r, uu: r * (2.0 * jnp.maximum(uu.astype(F32), 0.0)))
        reduce_start((l, "w_down"), mm_tn(f"down_g{l}", s["u"], dy2, a_fn=relu2).reshape(4, -1, d))
        dh2 = mm_carrying(f"up_b{l}", du, w["w_up"], F32, wt=True)
        reduce_start((l, "w_up"), mm_tn(f"up_g{l}", s["h2"], du, col_slabs=4))
        (dxa, dy), (dnpost, dgate_a, dnpre2, dscale_m, dshift_m) = rowmap_vjp(
            f"mix_mlp_b{l}", f_res_premod, [(s["x"], 0, d), (s["y"], 0, d)], [npost, md[2], npre2, md[4], md[3]],
            [(dx1a, 0, d), (dh2, 0, d)], 2, [F32, BF16], tile=128)
        dycat = mm_nn(f"outproj_b{l}", dy, w["w_out"], F32, wt=True)
        reduce_start((l, "w_out"), _unlayout_g_out(mm_tn(f"outproj_g{l}", s["ycat"], dy)).reshape(4, -1, d))
        (dcb, dcd), _ = rowmap(f"scgate_b{l}", lambda rv, cv: ([rv[0] * rv[2], rv[0] * rv[1]], []),
                               [(dycat, 2048, GW), (s["proj"], O_CB, GW), (s["cd"], 0, GW)], [],
                               [(GW, BF16), (GW, F32)])
        (dcc, dch), g_scw = conv_bwd(f"sc_conv_b{l}", dcd, [(s["proj"], O_CC, GW), (s["proj"], O_CH, GW)],
                                     sc_conv_w[l], BF16)
        daq, dak, dav, g_sinks = swa_bwd(f"swa_b{l}", dycat, 1024, s["proj"], cos_t, sin_t, k["sinks"])
        dcx, dsz, dsdt, g_scb, g_salog, g_sdtb, g_sdsk, g_snw = ssd_bwd(
            f"ssd_b{l}", dycat, 512, s["cx"], s["proj"], s["ss"], k["scb"], k["salog"], k["sdtb"], k["sdsk"], k["snw"])
        (dxbc,), g_ssmw = conv_bwd(f"ssm_conv_b{l}", dcx, [(s["proj"], O_XBC, 1024)], ssm_conv_w[l], BF16)
        dcq, dgz, dgba, g_alog, g_gdtb, g_gnw = gdn_bwd(f"gdn_b{l}", dycat, 0, s["cq"], s["proj"], s["gs"], s["gt"],
                                                        k["alog"], k["gdtb"], k["gnw"])
        (dgqkv,), g_gdnw = conv_bwd(f"gdn_conv_b{l}", dcq, [(s["proj"], O_GQKV, 1536)], gdn_conv_w[l], BF16)
        dproj = jnp.concatenate([dgqkv, dcb, dcc, dch, dxbc, daq, dgz, dsz, dak, dav, dgba, dsdt], axis=1)
        def start_w_in():
            g_in = mm_tn(f"inproj_g{l}", s["h"], dproj)
            reduce_start((l, "w_in"), jnp.stack(jnp.split(_unlayout_g_in(g_in), 4, axis=1), axis=0))

        if l == 0:
            start_w_in()
        dh = mm_carrying(f"inproj_b{l}", dproj, w["w_in"], F32, wt=True)
        if l > 0:
            start_w_in()
            sp = saved[l - 1]
            (dx1a, dy2), (dnpost2, dgate_m, dnpre, dscale_a, dshift_a) = rowmap_vjp(
                f"mlp_mix_b{l - 1}", f_res_premod, [(sp["x1"], 0, d), (sp["y2"], 0, d)],
                [p["norm_post_mlp"][l - 1].reshape(1, d), sp["md"][5], npre, md[1], md[0]],
                [(dxa, 0, d), (dh, 0, d)], 2, [F32, BF16], tile=128)
            gsm[l - 1, "norm_post_mlp"], dmd[l - 1, 5] = dnpost2[0], dgate_m
        else:
            (dx,), (dnpre, dscale_a, dshift_a) = rowmap_vjp("premix_b0", f_premod, [(s["x"], 0, d)],
                                                            [npre, md[1], md[0]], [(dh, 0, d)], 1, [F32],
                                                            add_rows=[(dxa, 0, d)])
        dmd.update({(l, 0): dshift_a, (l, 1): dscale_a, (l, 2): dgate_a, (l, 3): dshift_m, (l, 4): dscale_m})
        gsm.update({(l, n_): g_ for n_, g_ in dict(
            norm_pre_mix=dnpre[0], norm_post_mix=dnpost[0], norm_pre_mlp=dnpre2[0],
            gdn_a_log=g_alog[0, :4], gdn_dt_bias=g_gdtb[0, :4], gdn_norm_w=g_gnw[0], ssm_conv_b=g_scb[0],
            ssm_a_log=g_salog[0, :8], ssm_dt_bias=g_sdtb[0, :8], ssm_d=g_sdsk[0, :8], ssm_norm_w=g_snw[0],
            attn_sinks=g_sinks[0, :8], gdn_conv_w=g_gdnw, ssm_conv_w=g_ssmw, sc_conv_w=g_scw).items()})
    grad_x = dx[None]
    dmods = [jnp.concatenate([dmd[l, j] for j in range(6)], axis=1)[0] for l in range(DEPTH)]
    gsmall = [{n_: gsm[l, n_] for n_ in SMALL_REPL + SMALL_SHARD if n_ != "ada_b"} for l in range(DEPTH)]

    buf, spec = _pack([jnp.stack(dmods, axis=0)])
    got = allgather8("ag_dmod", buf)
    dmod_sum = _unpack(sum_slabs("dmod_sum", got), spec)[0]
    dmod_all = jnp.stack([_unpack(got[kk], spec)[0] for kk in range(8)], axis=1)
    g_ada_w = []
    for l in range(DEPTH):
        dm = lax.dynamic_slice(dmod_all[l], (0, chip * nsh), (8, nsh))
        g_ada_w.append(mm_tn(f"ada_g{l}", c_pad, jnp.pad(dm, ((0, 8), (0, 0))), a_fn=silu_bf))
    g_ada_w = jnp.stack(g_ada_w, axis=0)

    names = [n for n in SMALL_REPL if n != "ada_b"] + list(SMALL_SHARD)
    buf, spec = _pack([jnp.stack([gsmall[l][n] for l in range(DEPTH)], axis=0) for n in names] + [loss_part])
    tot = _unpack(sum_slabs("small_sum", allgather8("ag_small_g", buf)), spec)
    gfull = dict(zip(names, tot[:-1]))
    loss = tot[-1][0, 0]
    grads = {"ada_w": g_ada_w, "ada_b": dmod_sum}
    for n in SMALL_REPL:
        if n != "ada_b":
            grads[n] = gfull[n]
    for n in SMALL_SHARD:
        wdt = p[n].shape[2]
        grads[n] = lax.dynamic_slice_in_dim(gfull[n], chip * wdt, wdt, axis=2)

    if waiting:
        recv2s.update(zip([kk for kk, _ in waiting], chips_exchange("rs_chips_last", [pt for _, pt in waiting])))
    half = []
    for n in BIG:
        acc = None
        for l in range(DEPTH):
            acc = rs_sum(f"rs_sum_{n}{l}", g4s[l, n], recv1s[l, n], recv2s[l, n], idx, l, acc)
        half.append(acc)
    grads.update(zip(BIG, sibling_fill("rs_join", half)))

    delta, new_m, new_v = {}, {}, {}
    for n in ("ada_w",) + BIG:
        grads[n], delta[n], new_m[n], new_v[n] = adamw_layers(f"adamw_{n}", p[n], grads[n], m[n], v[n])
    small = list(SMALL_REPL) + list(SMALL_SHARD)
    bufs = []
    for src in (p, grads, m, v):
        b_, spec = _pack([src[n] for n in small])
        bufs.append(b_)
    dl, m2, v2 = adamw("adamw_small", *bufs)
    for n, a, b_, cc_ in zip(small, _unpack(dl, spec), _unpack(m2, spec), _unpack(v2, spec)):
        delta[n], new_m[n], new_v[n] = a, b_, cc_

    return (loss, grad_x, *[grads[n] for n in WEIGHTS], *[delta[n] for n in WEIGHTS], *[new_m[n] for n in WEIGHTS],
            *[new_v[n] for n in WEIGHTS])


def kernel(x, c, positions, ada_w, ada_b, norm_pre_mix, norm_post_mix, norm_pre_mlp, norm_post_mlp, w_in, w_out, gdn_conv_w, gdn_a_log, gdn_dt_bias, gdn_norm_w, ssm_conv_w, ssm_conv_b, ssm_a_log, ssm_dt_bias, ssm_d, ssm_norm_w, attn_sinks, sc_conv_w, w_up, w_down, loss_target, m_ada_w, m_ada_b, m_norm_pre_mix, m_norm_post_mix, m_norm_pre_mlp, m_norm_post_mlp, m_w_in, m_w_out, m_gdn_conv_w, m_gdn_a_log, m_gdn_dt_bias, m_gdn_norm_w, m_ssm_conv_w, m_ssm_conv_b, m_ssm_a_log, m_ssm_dt_bias, m_ssm_d, m_ssm_norm_w, m_attn_sinks, m_sc_conv_w, m_w_up, m_w_down, v_ada_w, v_ada_b, v_norm_pre_mix, v_norm_post_mix, v_norm_pre_mlp, v_norm_post_mlp, v_w_in, v_w_out, v_gdn_conv_w, v_gdn_a_log, v_gdn_dt_bias, v_gdn_norm_w, v_ssm_conv_w, v_ssm_conv_b, v_ssm_a_log, v_ssm_dt_bias, v_ssm_d, v_ssm_norm_w, v_attn_sinks, v_sc_conv_w, v_w_up, v_w_down):
    loc = locals()
    p = {n: loc[n] for n in WEIGHTS}
    m = {n: loc["m_" + n] for n in WEIGHTS}
    v = {n: loc["v_" + n] for n in WEIGHTS}
    return _step(p, m, v, x, c, positions, loss_target)
```

```python
import functools

import numpy as np
import jax
import jax.numpy as jnp
from jax import lax
from jax.experimental import pallas as pl
from jax.experimental.pallas import tpu as pltpu

F32, BF16 = jnp.float32, jnp.bfloat16
MESH = pl.DeviceIdType.MESH

DEPTH = 2
GW = 512
N_GDN, GDN_CS = 4, 64
SSD_CS, ATT_W = 128, 128
EPS = 1e-6
ROPE_THETA = 10000.0
IN_WIDTH = 5904
B1, B2, LR, AEPS, WD, STEP = 0.9, 0.999, 0.001, 1e-8, 0.01, 10

VMEM_LIMIT = 52 * 1024 * 1024
LANE = 128

O_GQKV, O_CB, O_CC, O_CH, O_XBC, O_AQ, O_GZ, O_SZ, O_AK, O_AV, O_GBA, O_SDT = (
    0, 1536, 2048, 2560, 3072, 4096, 5120, 5632, 6144, 6400, 6656, 6784)
PW = 6912
YCAT = 2560


def _cparams(*sem):
    return pltpu.CompilerParams(dimension_semantics=sem or None, vmem_limit_bytes=VMEM_LIMIT)


def _pick(n, cap, mult):
    if n <= cap:
        return n
    best = None
    for d in range(mult, cap + 1, mult):
        if n % d == 0:
            best = d
    assert best is not None, (n, cap, mult)
    return best


def _pieces(x, n):
    out, r = [], x
    for i in range(n):
        h = r.astype(BF16)
        out.append(h)
        if i < n - 1:
            r = r - h.astype(F32)
    return out


def _dgp(a, b, ca, cb, na, nb):
    dn = (((ca,), (cb,)), ((), ()))
    acc = None
    for i, pa in enumerate(_pieces(a, na)):
        for j, pb in enumerate(_pieces(b, nb)):
            if i + j < max(na, nb):
                t = lax.dot_general(pa, pb, dn, preferred_element_type=F32)
                acc = t if acc is None else acc + t
    return acc


_CONTRACT = {"nn": (1, 0), "nt": (1, 1), "tn": (0, 0)}


@functools.lru_cache(maxsize=None)
def _mm(kind, na, nb, const_a=False):
    ca, cb = _CONTRACT[kind]
    ng = max(na, nb)

    @jax.custom_vjp
    def f(a, b):
        return _dgp(a, b, ca, cb, na, nb)

    def bwd(r, g):
        a, b = r
        if kind == "nn":
            da, db = (lambda: _mm("nt", ng, nb)(g, b)), _mm("tn", na, ng, const_a)(a, g)
        elif kind == "nt":
            da, db = (lambda: _mm("nn", ng, nb)(g, b)), _mm("tn", ng, na)(g, a)
        else:
            da, db = (lambda: _mm("nt", nb, ng)(b, g)), _mm("nn", na, ng, const_a)(a, g)
        return (jnp.zeros_like(a) if const_a else da()), db

    f.defvjp(lambda a, b: (f(a, b), (a, b)), bwd)
    return f


_nn, _nt, _tn = _mm("nn", 1, 1), _mm("nt", 1, 1), _mm("tn", 1, 1)
_hnn, _hnt, _htn = _mm("nn", 2, 2), _mm("nt", 2, 2), _mm("tn", 2, 2)
_h2nn = _mm("nn", 2, 1)
_xnn = _mm("nn", 1, 3, True)


def _silu(x):
    return x * jax.nn.sigmoid(x)


def _softplus(x):
    return jnp.maximum(x, 0.0) + jnp.log1p(jnp.exp(-jnp.abs(x)))


def _colsel(v, idx):
    lane = lax.broadcasted_iota(jnp.int32, v.shape, 1)
    return jnp.sum(jnp.where(lane == idx, v, 0.0), axis=1, keepdims=True)


def _tri(n):
    r = lax.broadcasted_iota(jnp.int32, (n, n), 0)
    c = lax.broadcasted_iota(jnp.int32, (n, n), 1)
    return r >= c, r > c


def _rowsel(v, idx):
    row = lax.broadcasted_iota(jnp.int32, v.shape, 0)
    return jnp.sum(jnp.where(row == idx, v, 0.0), axis=0, keepdims=True)


def _tri_inv(m, size, nil):
    r = lax.broadcasted_iota(jnp.int32, (size, size), 0)
    c = lax.broadcasted_iota(jnp.int32, (size, size), 1)
    x = jnp.where(r == c, 1.0, 0.0) - m
    p = _h2nn(m, m)
    steps = int(np.log2(nil)) - 1
    for s in range(steps):
        x = x + _h2nn(x, p)
        if s < steps - 1:
            p = _h2nn(p, p)
    return x


@jax.custom_vjp
def _inv_given(m, tinv):
    return tinv


_inv_given.defvjp(lambda m, tinv: (tinv, tinv),
                  lambda tinv, g: (-_hnt(_htn(tinv, g), tinv), jnp.zeros_like(tinv)))


def _gdn_chunk(cq, z, ba, s, alog, dtb, nw, tinv_saved=None):
    n, nh = GDN_CS, N_GDN
    rr = n * nh

    def stack(a, off):
        return jnp.concatenate([a[:, off + h * LANE:off + (h + 1) * LANE] for h in range(nh)], axis=0)

    def percol(f):
        return jnp.concatenate([f(h) for h in range(nh)], axis=0)

    def own(a):
        return jnp.concatenate([a[h * n:(h + 1) * n, h * LANE:(h + 1) * LANE] for h in range(nh)], axis=0)

    q, k, v = _silu(stack(cq, 0)), _silu(stack(cq, GW)), _silu(stack(cq, 2 * GW))
    q = q * lax.rsqrt(jnp.sum(q * q, -1, keepdims=True) + EPS) * (128 ** -0.5)
    k = k * lax.rsqrt(jnp.sum(k * k, -1, keepdims=True) + EPS)
    beta = jax.nn.sigmoid(percol(lambda h: _colsel(ba, h)))
    g = percol(lambda h: -jnp.exp(_colsel(alog, h)) * _softplus(_colsel(ba, nh + h) + _colsel(dtb, h)))
    ri = lax.broadcasted_iota(jnp.int32, (rr, rr), 0)
    ci = lax.broadcasted_iota(jnp.int32, (rr, rr), 1)
    same = (ri >> 6) == (ci >> 6)
    incl, strict = same & (ri >= ci), same & (ri > ci)
    gcb = _xnn(incl.astype(F32), jnp.broadcast_to(g, (rr, LANE)))
    gci = jnp.concatenate([gcb, gcb], axis=1)
    dec = jnp.where(incl, jnp.exp(jnp.where(incl, gci - gci.T, 0.0)), 0.0)
    kb = k * beta
    kk = _nt(jnp.concatenate([kb, q], axis=0), k)
    m = jnp.where(strict, kk[:rr] * dec, 0.0)
    attn = jnp.where(incl, kk[rr:] * dec, 0.0)
    tinv = _tri_inv(m, rr, n) if tinv_saved is None else _inv_given(m, tinv_saved)
    egc = jnp.exp(gcb)
    uw = _nn(tinv, jnp.concatenate([v * beta, kb * egc], axis=1))
    u, w = uw[:, :LANE], uw[:, LANE:]
    gl = [jnp.sum(g[h * n:(h + 1) * n], axis=0, keepdims=True) for h in range(nh)]
    kd = k * jnp.exp(percol(lambda h: jnp.broadcast_to(gl[h], (n, 1))) - gcb)
    ws = _nn(jnp.concatenate([w, q * egc], axis=0), s)
    v_new = u - own(ws[:rr])
    o = own(ws[rr:]) + _nn(attn, v_new)
    rowh = lax.broadcasted_iota(jnp.int32, (rr, LANE), 0) >> 6
    vexp = jnp.concatenate([jnp.where(rowh == h, v_new, 0.0) for h in range(nh)], axis=1)
    cd = jnp.concatenate([jnp.broadcast_to(jnp.exp(gl[h]), (1, LANE)) for h in range(nh)], axis=1)
    s_new = s * cd + _tn(kd, vexp)
    y = o * lax.rsqrt(jnp.mean(o * o, -1, keepdims=True) + EPS) * nw * _silu(stack(z, 0))
    y = jnp.concatenate([y[h * n:(h + 1) * n] for h in range(nh)], axis=1)
    return (y, s_new, tinv) if tinv_saved is None else (y, s_new)


def _ssd_chunk(cx, z, dtc, s, convb, alog, dtb, dsk, nw):
    n = SSD_CS
    incl, _ = _tri(n)
    xbc = _silu(cx + convb)
    dt_all = _softplus(dtc + dtb)
    da_all = dt_all * (-jnp.exp(alog))
    cs_all = _xnn(incl.astype(F32), da_all)
    cs_t = cs_all.T
    tot_all = jnp.sum(da_all, axis=0, keepdims=True)
    lane_lo = lax.broadcasted_iota(jnp.int32, (n, LANE), 1) < 64
    ys, s_out = [], []
    for pi in range(4):
        g = pi // 2
        bm = xbc[:, GW + g * LANE:GW + (g + 1) * LANE]
        cm = xbc[:, GW + 256 + g * LANE:GW + 256 + (g + 1) * LANE]
        cbm = _nt(cm, bm)
        xp, s_p = xbc[:, pi * LANE:(pi + 1) * LANE], s[:, pi * LANE:(pi + 1) * LANE]
        a_rows, b_cols, dts, dsks, etot = [], [], [], [], []
        for h in (2 * pi, 2 * pi + 1):
            cs_i = _colsel(cs_all, h)
            lm = jnp.where(incl, jnp.exp(jnp.where(incl, cs_i - _rowsel(cs_t, h), 0.0)), 0.0)
            tot = _colsel(tot_all, h)
            a_rows.append(jnp.concatenate([cbm * lm, cm * jnp.exp(cs_i)], axis=1))
            b_cols.append(bm * jnp.exp(tot - cs_i))
            dts.append(_colsel(dt_all, h))
            dsks.append(_colsel(dsk, h))
            etot.append(jnp.exp(tot))
        xdt = xp * jnp.where(lane_lo, dts[0], dts[1])
        yy = _nn(jnp.concatenate(a_rows, axis=0), jnp.concatenate([xdt, s_p], axis=0))
        ys.append(jnp.where(lane_lo, yy[:n], yy[n:]) + xp * jnp.where(lane_lo[:1], dsks[0], dsks[1]))
        st = _tn(jnp.concatenate(b_cols, axis=1), xdt)
        s_out.append(s_p * jnp.where(lane_lo[:1], etot[0], etot[1]) + jnp.where(lane_lo, st[:n], st[n:]))
    zs = _silu(z)
    out = []
    for g in range(2):
        y = jnp.concatenate(ys[2 * g:2 * g + 2], axis=1) * zs[:, g * 256:(g + 1) * 256]
        out.append(y * lax.rsqrt(jnp.mean(y * y, -1, keepdims=True) + EPS) * nw[:, g * 256:(g + 1) * 256])
    return jnp.concatenate(out, axis=1), jnp.concatenate(s_out, axis=1)


SWA_STACK = 1


def _swa_block(qg, kp, kc, vp, vc, sink, notfirst, *, g):
    w, ns = ATT_W, SWA_STACK
    kb = jnp.concatenate([kp, kc], axis=0)
    vb = jnp.concatenate([vp, vc], axis=0)
    qi = lax.broadcasted_iota(jnp.int32, (ns * w, 2 * w), 0) & (w - 1)
    kj = lax.broadcasted_iota(jnp.int32, (ns * w, 2 * w), 1)
    rel = qi + w - kj
    mask = (rel >= 0) & (rel < w) & ((kj >= w) | (notfirst > 0.5))
    outs = []
    for r0 in range(0, 4, ns):
        q = jnp.concatenate([qg[:, r * LANE:(r + 1) * LANE] for r in range(r0, r0 + ns)], axis=0)
        snk = jnp.concatenate([jnp.broadcast_to(_colsel(sink, 4 * g + r), (w, 1)) for r in range(r0, r0 + ns)], axis=0)
        s = jnp.where(mask, _nt(q, kb) * (64 ** -0.5), -1e30)
        mx = lax.stop_gradient(jnp.maximum(jnp.max(s, axis=-1, keepdims=True), snk))
        p = jnp.where(mask, jnp.exp(s - mx), 0.0)
        p = p / (jnp.sum(p, axis=-1, keepdims=True) + jnp.exp(snk - mx))
        o = _nn(p, vb)
        outs += [o[j * w:(j + 1) * w] for j in range(ns)]
    return jnp.concatenate(outs, axis=1)


def _rope(x, cos, sin):
    return x * cos + pltpu.roll(x, 64, 1) * sin


def _unrope(dy, cos, sin):
    return dy * cos + pltpu.roll(dy * sin, 64, 1)


def rowmap(name, fn, rows, consts, outs, accs=(), tile=256):
    t = rows[0][0].shape[0]
    tile = min(tile, t)
    assert t % tile == 0
    nr, nc, no, na = len(rows), len(consts), len(outs), len(accs)

    def body(*refs):
        rv = [r[...] for r in refs[:nr]]
        cv = [r[...] for r in refs[nr:nr + nc]]
        ov, av = fn(rv, cv)
        for ref, o in zip(refs[nr + nc:nr + nc + no], ov):
            ref[...] = o.astype(ref.dtype)
        if na:
            a_refs = refs[nr + nc + no:]

            @pl.when(pl.program_id(0) == 0)
            def _():
                for ref in a_refs:
                    ref[...] = jnp.zeros_like(ref)

            for ref, a in zip(a_refs, av):
                ref[...] += a

    def cspec(off, width):
        assert off % width == 0
        cb = off // width
        return pl.BlockSpec((tile, width), lambda i: (i, cb))

    in_specs = [cspec(off, wd) for (_, off, wd) in rows]
    in_specs += [pl.BlockSpec(c.shape, lambda i: (0, 0)) for c in consts]
    out_specs = [pl.BlockSpec((tile, wd), lambda i: (i, 0)) for (wd, _) in outs]
    out_specs += [pl.BlockSpec(s, lambda i: (0, 0)) for s in accs]
    out_shape = [jax.ShapeDtypeStruct((t, wd), dt) for (wd, dt) in outs]
    out_shape += [jax.ShapeDtypeStruct(s, F32) for s in accs]
    res = pl.pallas_call(
        body, name=name, grid=(t // tile,), in_specs=in_specs, out_specs=out_specs, out_shape=out_shape,
        compiler_params=_cparams("arbitrary"),
    )(*[r[0] for r in rows], *consts)
    return res[:no], res[no:]


def rowmap_vjp(name, f, rows, consts, douts, n_diff, drow_dtypes, tile=256, add_rows=()):
    nr, nd, nadd = len(rows), len(douts), len(add_rows)

    def g(vals, cvals):
        rv, dov, addv = vals[:nr], vals[nr:nr + nd], vals[nr + nd:]
        fixed = rv[n_diff:]
        _, vjp = jax.vjp(lambda r, c: f(list(r) + list(fixed), c), rv[:n_diff], cvals)
        dr, dc = vjp([d.astype(F32) for d in dov])
        dr = list(dr)
        for a in addv:
            dr[0] = dr[0] + a
        return dr, dc

    outs = [(rows[i][2], drow_dtypes[i]) for i in range(n_diff)]
    accs = [c.shape for c in consts]
    return rowmap(name, g, list(rows) + list(douts) + list(add_rows), consts, outs, accs, tile=tile)


def _carried_copies(kind, in_refs, out_refs, send_sems, recv_sems):
    x, y, c = _me()
    chip = 2 * x + y
    chips = [(1 - x, y), (x, 1 - y), (1 - x, 1 - y)]
    cps = []

    def rc(src, dst, k, to):
        return pltpu.make_async_remote_copy(src_ref=src, dst_ref=dst, send_sem=send_sems.at[k], recv_sem=recv_sems.at[k],
                                            device_id=to, device_id_type=MESH)

    for i, (src, dst) in enumerate(zip(in_refs, out_refs)):
        if kind == "gather":
            rh = src.shape[0] // 2
            half = pl.ds(pl.multiple_of(c * rh, 16), rh)
            cps.append(rc(src, dst.at[chip], 4 * i + 3, (x, y, 1 - c)))
            cps += [rc(src.at[half], dst.at[chip, half], 4 * i + k, (cx, cy, c)) for k, (cx, cy) in enumerate(chips)]
        else:
            cps += [rc(src.at[2 * cx + cy], dst.at[chip], 4 * i + k, (cx, cy, c)) for k, (cx, cy) in enumerate(chips)]
    return cps


def mm_nn(name, a, w, out_dtype, a_fn=None, extra=None, epi=None, tm_cap=1024, wt=False, carry=None):
    nc = len(carry[1]) if carry else 0
    m, k = a.shape
    stacked = w.ndim == 3
    n = (w.shape[-2] if wt else w.shape[-1] * (w.shape[0] if stacked else 1))
    slab = w.shape[-1]
    tm = _pick(m, tm_cap, 16)
    tn = _pick(n if wt else slab, 1024, LANE)
    tk = _pick(slab if wt else k, 2048, LANE)
    nk = k // tk
    per = slab // (tk if wt else tn)
    wdims = ((1,), (1,)) if wt else ((1,), (0,))

    gi, gj = m // tm, n // tn
    n_in = 2 + (extra is not None)

    def body(*refs):
        a_ref, w_ref = refs[0], refs[1]
        e_ref = refs[2] if extra is not None else None
        o_ref, acc = refs[n_in + nc], refs[n_in + 2 * nc + 1]
        kk = pl.program_id(2)
        if nc:
            def copies():
                return _carried_copies(carry[0], refs[n_in:n_in + nc], refs[n_in + nc + 1:n_in + 2 * nc + 1],
                                       refs[-2], refs[-1])

            @pl.when((pl.program_id(0) == 0) & (pl.program_id(1) == 0) & (kk == 0))
            def _():
                for cp in copies():
                    cp.start()

        @pl.when(kk == 0)
        def _():
            acc[...] = jnp.zeros_like(acc)

        av = a_ref[...]
        if a_fn is not None:
            av = a_fn(av)
        acc[...] += lax.dot_general(av.astype(BF16), w_ref[...].astype(BF16), (wdims, ((), ())),
                                    preferred_element_type=F32)

        @pl.when(kk == nk - 1)
        def _():
            r = acc[...]
            if epi is not None:
                r = epi(r, e_ref[...])
            o_ref[...] = r.astype(o_ref.dtype)

        if nc:
            @pl.when((pl.program_id(0) == gi - 1) & (pl.program_id(1) == gj - 1) & (kk == nk - 1))
            def _():
                cps = copies()
                for cp in cps:
                    cp.wait_recv()
                for cp in cps:
                    cp.wait_send()

    if wt:
        w_spec = (pl.BlockSpec((None, tn, tk), lambda i, j, kk: (kk // per, j, kk % per)) if stacked
                  else pl.BlockSpec((tn, tk), lambda i, j, kk: (j, kk)))
    else:
        w_spec = (pl.BlockSpec((None, tk, tn), lambda i, j, kk: (j // per, kk, j % per)) if stacked
                  else pl.BlockSpec((tk, tn), lambda i, j, kk: (kk, j)))
    in_specs = [pl.BlockSpec((tm, tk), lambda i, j, kk: (i, kk)), w_spec]
    ops = [a, w]
    if extra is not None:
        in_specs.append(pl.BlockSpec((tm, tn), lambda i, j, kk: (i, j)))
        ops.append(extra)
    out_spec = pl.BlockSpec((tm, tn), lambda i, j, kk: (i, j))
    out_shape = jax.ShapeDtypeStruct((m, n), out_dtype)
    if not nc:
        return pl.pallas_call(
            body, name=name, grid=(gi, gj, nk), in_specs=in_specs, out_specs=out_spec, out_shape=out_shape,
            scratch_shapes=[pltpu.VMEM((tm, tn), F32)],
            compiler_params=_cparams("parallel", "parallel", "arbitrary"),
        )(*ops)
    landing = [jax.ShapeDtypeStruct((4,) + c.shape if carry[0] == "gather" else c.shape, c.dtype) for c in carry[1]]
    res = pl.pallas_call(
        body, name=name, grid=(gi, gj, nk), in_specs=in_specs + [ANY] * nc, out_specs=[out_spec] + [ANY] * nc,
        out_shape=[out_shape] + landing, scratch_shapes=[pltpu.VMEM((tm, tn), F32)] + _sems(4 * nc),
        compiler_params=_cparams("arbitrary", "arbitrary", "arbitrary"),
    )(*ops, *carry[1])
    return res[0], list(res[1:])


def mm_tn(name, a, b, a_fn=None, col_slabs=1):
    t, m = a.shape
    n = b.shape[1]
    tm, tn, tk = _pick(m, 1024, LANE), _pick(n // col_slabs, 1024, LANE), _pick(t, 2048, 16)
    nk = t // tk
    per = n // col_slabs // tn

    def body(a_ref, b_ref, o_ref, acc):
        kk = pl.program_id(2)

        @pl.when(kk == 0)
        def _():
            acc[...] = jnp.zeros_like(acc)

        av = a_ref[...]
        if a_fn is not None:
            av = a_fn(av)
        acc[...] += lax.dot_general(av.astype(BF16), b_ref[...].astype(BF16), (((0,), (0,)), ((), ())),
                                    preferred_element_type=F32)

        @pl.when(kk == nk - 1)
        def _():
            o_ref[...] = acc[...]

    return pl.pallas_call(
        body, name=name, grid=(m // tm, n // tn, nk),
        in_specs=[pl.BlockSpec((tk, tm), lambda i, j, kk: (kk, i)), pl.BlockSpec((tk, tn), lambda i, j, kk: (kk, j))],
        out_specs=(pl.BlockSpec((tm, tn), lambda i, j, kk: (i, j)) if col_slabs == 1
                   else pl.BlockSpec((None, tm, tn), lambda i, j, kk: (j // per, i, j % per))),
        out_shape=jax.ShapeDtypeStruct((m, n) if col_slabs == 1 else (col_slabs, m, n // col_slabs), F32),
        scratch_shapes=[pltpu.VMEM((tm, tn), F32)],
        compiler_params=_cparams("parallel", "parallel", "arbitrary"),
    )(a, b)


CONV_TILE = 512
HALO = 8


def conv_fwd(name, xs, w):
    t = xs[0][0].shape[0]
    c = xs[0][2]
    kw = w.shape[0]
    tt = min(CONV_TILE, t)
    nx = len(xs)

    def body(*refs):
        cur_refs, prev_refs, w_ref, y_ref = refs[:nx], refs[nx:2 * nx], refs[2 * nx], refs[2 * nx + 1]
        i = pl.program_id(0)
        cur, prev = cur_refs[0][...], prev_refs[0][...]
        for r, p in zip(cur_refs[1:], prev_refs[1:]):
            cur, prev = cur * r[...], prev * p[...]
        prev = jnp.where(i > 0, prev, 0.0)
        xx = jnp.concatenate([prev, cur], axis=0)
        acc = jnp.zeros((tt, c), F32)
        for k in range(kw):
            sh = kw - 1 - k
            rolled = pltpu.roll(xx, sh, 0) if sh else xx
            acc = acc + rolled[HALO:] * w_ref[k:k + 1, :]
        y_ref[...] = acc

    def cur_spec(off):
        cb = off // c
        return pl.BlockSpec((tt, c), lambda i: (i, cb))

    def prev_spec(off):
        cb = off // c
        return pl.BlockSpec((HALO, c), lambda i: (jnp.maximum(i * (tt // HALO) - 1, 0), cb))

    for (_, off, wd) in xs:
        assert wd == c and off % c == 0
    return pl.pallas_call(
        body, name=name, grid=(t // tt,),
        in_specs=[cur_spec(off) for (_, off, _) in xs] + [prev_spec(off) for (_, off, _) in xs]
        + [pl.BlockSpec(w.shape, lambda i: (0, 0))],
        out_specs=pl.BlockSpec((tt, c), lambda i: (i, 0)),
        out_shape=jax.ShapeDtypeStruct((t, c), F32),
        compiler_params=_cparams("arbitrary"),
    )(*[x[0] for x in xs], *[x[0] for x in xs], w)


def conv_bwd(name, dy, xs, w, dx_dtype):
    t = xs[0][0].shape[0]
    c = xs[0][2]
    kw = w.shape[0]
    tt = min(CONV_TILE, t)
    nt = t // tt
    nx = len(xs)
    n = tt + HALO

    def body(*refs):
        dy_ref = refs[0]
        cur_refs, prev_refs = refs[1:1 + nx], refs[1 + nx:1 + 2 * nx]
        w_ref = refs[1 + 2 * nx]
        dx_refs = refs[2 + 2 * nx:2 + 3 * nx]
        dw_ref, carry = refs[2 + 3 * nx], refs[3 + 3 * nx]
        i = pl.program_id(0)

        @pl.when(i == 0)
        def _():
            carry[...] = jnp.zeros_like(carry)
            dw_ref[...] = jnp.zeros_like(dw_ref)

        curs = [r[...] for r in cur_refs]
        cur, prev = curs[0], prev_refs[0][...]
        for cv, p in zip(curs[1:], prev_refs[1:]):
            cur, prev = cur * cv, prev * p[...]
        prev = jnp.where(i < nt - 1, prev, 0.0)
        xx = jnp.concatenate([prev, cur], axis=0)
        dyv = dy_ref[...]
        dpad = jnp.concatenate([jnp.zeros((HALO, c), F32), dyv], axis=0)
        dxx = jnp.zeros((n, c), F32)
        dws = []
        for k in range(kw):
            sh = kw - 1 - k
            dxx = dxx + (pltpu.roll(dpad, n - sh, 0) if sh else dpad) * w_ref[k:k + 1, :]
            rolled = pltpu.roll(xx, sh, 0) if sh else xx
            dws.append(jnp.sum(dyv * rolled[HALO:], axis=0, keepdims=True))
        dw_ref[...] += jnp.concatenate(dws, axis=0)
        dcur = jnp.concatenate([dxx[HALO:tt], dxx[tt:] + carry[...]], axis=0)
        carry[...] = dxx[:HALO]
        for j, ref in enumerate(dx_refs):
            d = dcur
            for jj, cv in enumerate(curs):
                if jj != j:
                    d = d * cv
            ref[...] = d.astype(ref.dtype)

    def cur_spec(off):
        cb = off // c
        return pl.BlockSpec((tt, c), lambda i: (nt - 1 - i, cb))

    def prev_spec(off):
        cb = off // c
        return pl.BlockSpec((HALO, c), lambda i: (jnp.maximum((nt - 1 - i) * (tt // HALO) - 1, 0), cb))

    res = pl.pallas_call(
        body, name=name, grid=(nt,),
        in_specs=[pl.BlockSpec((tt, c), lambda i: (nt - 1 - i, 0))] + [cur_spec(off) for (_, off, _) in xs]
        + [prev_spec(off) for (_, off, _) in xs] + [pl.BlockSpec(w.shape, lambda i: (0, 0))],
        out_specs=[pl.BlockSpec((tt, c), lambda i: (nt - 1 - i, 0)) for _ in xs] + [pl.BlockSpec(w.shape, lambda i: (0, 0))],
        out_shape=[jax.ShapeDtypeStruct((t, c), dx_dtype) for _ in xs] + [jax.ShapeDtypeStruct(w.shape, F32)],
        scratch_shapes=[pltpu.VMEM((HALO, c), F32)],
        compiler_params=_cparams("arbitrary"),
    )(dy, *[x[0] for x in xs], *[x[0] for x in xs], w)
    return res[:nx], res[nx]


def _const_spec(a):
    return pl.BlockSpec(a.shape, lambda i: (0,) * a.ndim)


GDN_STEP = 4


def _gdn_rows(ref, c):
    return ref[c * GDN_CS:(c + 1) * GDN_CS, :]


def gdn_fwd(name, cq, proj, alog, dtb, nw):
    t = cq.shape[0]
    nc = t // GDN_CS
    ns = min(GDN_STEP, nc)
    rows = ns * GDN_CS

    def body(cq_ref, z_ref, ba_ref, alog_ref, dtb_ref, nw_ref, y_ref, ssave_ref, tsave_ref, s_scr):
        @pl.when(pl.program_id(0) == 0)
        def _():
            s_scr[...] = jnp.zeros_like(s_scr)

        s = s_scr[...]
        ys, saved, tinvs = [], [], []
        for c in range(ns):
            saved.append(s)
            y, s, tinv = _gdn_chunk(_gdn_rows(cq_ref, c), _gdn_rows(z_ref, c), _gdn_rows(ba_ref, c), s, alog_ref[...],
                                    dtb_ref[...], nw_ref[...])
            ys.append(y)
            tinvs.append(tinv)
        y_ref[...] = jnp.concatenate(ys, axis=0)
        ssave_ref[...] = jnp.stack(saved, axis=0)
        tsave_ref[...] = jnp.stack(tinvs, axis=0)
        s_scr[...] = s

    return pl.pallas_call(
        body, name=name, grid=(nc // ns,),
        in_specs=[pl.BlockSpec((rows, 3 * GW), lambda i: (i, 0)),
                  pl.BlockSpec((rows, GW), lambda i: (i, O_GZ // GW)),
                  pl.BlockSpec((rows, LANE), lambda i: (i, O_GBA // LANE)),
                  _const_spec(alog), _const_spec(dtb), _const_spec(nw)],
        out_specs=[pl.BlockSpec((rows, GW), lambda i: (i, 0)),
                   pl.BlockSpec((ns, LANE, N_GDN * LANE), lambda i: (i, 0, 0)),
                   pl.BlockSpec((ns, N_GDN * GDN_CS, N_GDN * GDN_CS), lambda i: (i, 0, 0))],
        out_shape=[jax.ShapeDtypeStruct((t, GW), F32), jax.ShapeDtypeStruct((nc, LANE, N_GDN * LANE), F32),
                   jax.ShapeDtypeStruct((nc, N_GDN * GDN_CS, N_GDN * GDN_CS), F32)],
        scratch_shapes=[pltpu.VMEM((LANE, N_GDN * LANE), F32)],
        compiler_params=_cparams("arbitrary"),
    )(cq, proj, proj, alog, dtb, nw)


def gdn_bwd(name, dy, dy_off, cq, proj, ssave, tsave, alog, dtb, nw):
    t = cq.shape[0]
    nc = t // GDN_CS
    ns = min(GDN_STEP, nc)
    rows = ns * GDN_CS

    def body(dy_ref, cq_ref, z_ref, ba_ref, ssave_ref, tsave_ref, alog_ref, dtb_ref, nw_ref,
             dcq_ref, dz_ref, dba_ref, dalog_ref, ddtb_ref, dnw_ref, ds_scr):
        @pl.when(pl.program_id(0) == 0)
        def _():
            ds_scr[...] = jnp.zeros_like(ds_scr)
            dalog_ref[...] = jnp.zeros_like(dalog_ref)
            ddtb_ref[...] = jnp.zeros_like(ddtb_ref)
            dnw_ref[...] = jnp.zeros_like(dnw_ref)

        ds = ds_scr[...]
        dcq, dzs, dbas = [None] * ns, [None] * ns, [None] * ns
        dal, ddt, dnw = (jnp.zeros(r.shape, F32) for r in (dalog_ref, ddtb_ref, dnw_ref))
        for c in reversed(range(ns)):
            _, vjp = jax.vjp(functools.partial(_gdn_chunk, tinv_saved=tsave_ref[c]), _gdn_rows(cq_ref, c),
                             _gdn_rows(z_ref, c), _gdn_rows(ba_ref, c), ssave_ref[c], alog_ref[...], dtb_ref[...],
                             nw_ref[...])
            dcq[c], dzs[c], dbas[c], ds, dal_c, ddt_c, dnw_c = vjp((_gdn_rows(dy_ref, c), ds))
            dal, ddt, dnw = dal + dal_c, ddt + ddt_c, dnw + dnw_c
        dcq_ref[...] = jnp.concatenate(dcq, axis=0)
        dz_ref[...] = jnp.concatenate(dzs, axis=0).astype(dz_ref.dtype)
        dba_ref[...] = jnp.concatenate(dbas, axis=0).astype(dba_ref.dtype)
        ds_scr[...] = ds
        dalog_ref[...] += dal
        ddtb_ref[...] += ddt
        dnw_ref[...] += dnw

    rev = lambda i: nc // ns - 1 - i
    return pl.pallas_call(
        body, name=name, grid=(nc // ns,),
        in_specs=[pl.BlockSpec((rows, GW), lambda i: (rev(i), dy_off // GW)),
                  pl.BlockSpec((rows, 3 * GW), lambda i: (rev(i), 0)),
                  pl.BlockSpec((rows, GW), lambda i: (rev(i), O_GZ // GW)),
                  pl.BlockSpec((rows, LANE), lambda i: (rev(i), O_GBA // LANE)),
                  pl.BlockSpec((ns, LANE, N_GDN * LANE), lambda i: (rev(i), 0, 0)),
                  pl.BlockSpec((ns, N_GDN * GDN_CS, N_GDN * GDN_CS), lambda i: (rev(i), 0, 0)),
                  _const_spec(alog), _const_spec(dtb), _const_spec(nw)],
        out_specs=[pl.BlockSpec((rows, 3 * GW), lambda i: (rev(i), 0)),
                   pl.BlockSpec((rows, GW), lambda i: (rev(i), 0)),
                   pl.BlockSpec((rows, LANE), lambda i: (rev(i), 0)),
                   _const_spec(alog), _const_spec(dtb), _const_spec(nw)],
        out_shape=[jax.ShapeDtypeStruct((t, 3 * GW), F32), jax.ShapeDtypeStruct((t, GW), BF16),
                   jax.ShapeDtypeStruct((t, LANE), BF16), jax.ShapeDtypeStruct(alog.shape, F32),
                   jax.ShapeDtypeStruct(dtb.shape, F32), jax.ShapeDtypeStruct(nw.shape, F32)],
        scratch_shapes=[pltpu.VMEM((LANE, N_GDN * LANE), F32)],
        compiler_params=_cparams("arbitrary"),
    )(dy, cq, proj, proj, ssave, tsave, alog, dtb, nw)


def ssd_fwd(name, cx, proj, convb, alog, dtb, dsk, nw):
    t = cx.shape[0]
    nc = t // SSD_CS

    def body(cx_ref, z_ref, dt_ref, cb_ref, alog_ref, dtb_ref, dsk_ref, nw_ref, y_ref, ssave_ref, s_scr):
        @pl.when(pl.program_id(0) == 0)
        def _():
            s_scr[...] = jnp.zeros_like(s_scr)

        s_in = s_scr[...]
        y, s_new = _ssd_chunk(cx_ref[...], z_ref[...], dt_ref[...], s_in, cb_ref[...], alog_ref[...], dtb_ref[...],
                              dsk_ref[...], nw_ref[...])
        y_ref[...] = y
        ssave_ref[0] = s_in
        s_scr[...] = s_new

    return pl.pallas_call(
        body, name=name, grid=(nc,),
        in_specs=[pl.BlockSpec((SSD_CS, 1024), lambda i: (i, 0)),
                  pl.BlockSpec((SSD_CS, GW), lambda i: (i, O_SZ // GW)),
                  pl.BlockSpec((SSD_CS, LANE), lambda i: (i, O_SDT // LANE)),
                  _const_spec(convb), _const_spec(alog), _const_spec(dtb), _const_spec(dsk), _const_spec(nw)],
        out_specs=[pl.BlockSpec((SSD_CS, GW), lambda i: (i, 0)),
                   pl.BlockSpec((1, LANE, GW), lambda i: (i, 0, 0))],
        out_shape=[jax.ShapeDtypeStruct((t, GW), F32), jax.ShapeDtypeStruct((nc, LANE, GW), F32)],
        scratch_shapes=[pltpu.VMEM((LANE, GW), F32)],
        compiler_params=_cparams("arbitrary"),
    )(cx, proj, proj, convb, alog, dtb, dsk, nw)


def ssd_bwd(name, dy, dy_off, cx, proj, ssave, convb, alog, dtb, dsk, nw):
    t = cx.shape[0]
    nc = t // SSD_CS

    def body(dy_ref, cx_ref, z_ref, dt_ref, ssave_ref, cb_ref, alog_ref, dtb_ref, dsk_ref, nw_ref,
             dcx_ref, dz_ref, ddt_ref, dcb_ref, dalog_ref, ddtb_ref, ddsk_ref, dnw_ref, ds_scr):
        @pl.when(pl.program_id(0) == 0)
        def _():
            ds_scr[...] = jnp.zeros_like(ds_scr)
            for r in (dcb_ref, dalog_ref, ddtb_ref, ddsk_ref, dnw_ref):
                r[...] = jnp.zeros_like(r)

        _, vjp = jax.vjp(_ssd_chunk, cx_ref[...], z_ref[...], dt_ref[...], ssave_ref[0], cb_ref[...], alog_ref[...],
                         dtb_ref[...], dsk_ref[...], nw_ref[...])
        dcx, dz, ddt, ds, dcb, dal, ddtb, ddsk, dnw = vjp((dy_ref[...], ds_scr[...]))
        dcx_ref[...] = dcx
        dz_ref[...] = dz.astype(dz_ref.dtype)
        ddt_ref[...] = ddt.astype(ddt_ref.dtype)
        ds_scr[...] = ds
        dcb_ref[...] += dcb
        dalog_ref[...] += dal
        ddtb_ref[...] += ddtb
        ddsk_ref[...] += ddsk
        dnw_ref[...] += dnw

    rev = lambda i: nc - 1 - i
    return pl.pallas_call(
        body, name=name, grid=(nc,),
        in_specs=[pl.BlockSpec((SSD_CS, GW), lambda i: (rev(i), dy_off // GW)),
                  pl.BlockSpec((SSD_CS, 1024), lambda i: (rev(i), 0)),
                  pl.BlockSpec((SSD_CS, GW), lambda i: (rev(i), O_SZ // GW)),
                  pl.BlockSpec((SSD_CS, LANE), lambda i: (rev(i), O_SDT // LANE)),
                  pl.BlockSpec((1, LANE, GW), lambda i: (rev(i), 0, 0)),
                  _const_spec(convb), _const_spec(alog), _const_spec(dtb), _const_spec(dsk), _const_spec(nw)],
        out_specs=[pl.BlockSpec((SSD_CS, 1024), lambda i: (rev(i), 0)),
                   pl.BlockSpec((SSD_CS, GW), lambda i: (rev(i), 0)),
                   pl.BlockSpec((SSD_CS, LANE), lambda i: (rev(i), 0)),
                   _const_spec(convb), _const_spec(alog), _const_spec(dtb), _const_spec(dsk), _const_spec(nw)],
        out_shape=[jax.ShapeDtypeStruct((t, 1024), F32), jax.ShapeDtypeStruct((t, GW), BF16),
                   jax.ShapeDtypeStruct((t, LANE), BF16), jax.ShapeDtypeStruct(convb.shape, F32),
                   jax.ShapeDtypeStruct(alog.shape, F32), jax.ShapeDtypeStruct(dtb.shape, F32),
                   jax.ShapeDtypeStruct(dsk.shape, F32), jax.ShapeDtypeStruct(nw.shape, F32)],
        scratch_shapes=[pltpu.VMEM((LANE, GW), F32)],
        compiler_params=_cparams("arbitrary"),
    )(dy, cx, proj, proj, ssave, convb, alog, dtb, dsk, nw)


def _swa_specs(nb, rev):
    w = ATT_W
    cur = lambda i: (nb - 1 - i) if rev else i
    prv = lambda i: jnp.maximum(cur(i) - 1, 0)
    return [pl.BlockSpec((w, 1024), lambda i: (cur(i), O_AQ // 1024)),
            pl.BlockSpec((w, 256), lambda i: (cur(i), O_AK // 256)),
            pl.BlockSpec((w, 256), lambda i: (prv(i), O_AK // 256)),
            pl.BlockSpec((w, 256), lambda i: (cur(i), O_AV // 256)),
            pl.BlockSpec((w, 256), lambda i: (prv(i), O_AV // 256)),
            pl.BlockSpec((w, LANE), lambda i: (cur(i), 0)), pl.BlockSpec((w, LANE), lambda i: (prv(i), 0)),
            pl.BlockSpec((w, LANE), lambda i: (cur(i), 0)), pl.BlockSpec((w, LANE), lambda i: (prv(i), 0))]


def swa_fwd(name, proj, cos, sin, sinks):
    t = proj.shape[0]
    nb = t // ATT_W

    def body(q_ref, kc_ref, kp_ref, vc_ref, vp_ref, cc_ref, cp_ref, sc_ref, sp_ref, snk_ref, y_ref):
        notfirst = jnp.where(pl.program_id(0) > 0, 1.0, 0.0).astype(F32)
        cc, cp, sc, sp = cc_ref[...], cp_ref[...], sc_ref[...], sp_ref[...]
        outs = []
        for g in range(2):
            sl = slice(g * LANE, (g + 1) * LANE)
            qg = jnp.concatenate([_rope(q_ref[:, (4 * g + r) * LANE:(4 * g + r + 1) * LANE], cc, sc)
                                  for r in range(4)], axis=1)
            outs.append(_swa_block(qg, _rope(kp_ref[:, sl], cp, sp), _rope(kc_ref[:, sl], cc, sc), vp_ref[:, sl],
                                   vc_ref[:, sl], snk_ref[...], notfirst, g=g))
        y_ref[...] = jnp.concatenate(outs, axis=1)

    return pl.pallas_call(
        body, name=name, grid=(nb,),
        in_specs=_swa_specs(nb, False) + [_const_spec(sinks)],
        out_specs=pl.BlockSpec((ATT_W, 1024), lambda i: (i, 0)),
        out_shape=jax.ShapeDtypeStruct((t, 1024), F32),
        compiler_params=_cparams("arbitrary"),
    )(proj, proj, proj, proj, proj, cos, cos, sin, sin, sinks)


def swa_bwd(name, dy, dy_off, proj, cos, sin, sinks):
    t = proj.shape[0]
    nb = t // ATT_W

    def body(dy_ref, q_ref, kc_ref, kp_ref, vc_ref, vp_ref, cc_ref, cp_ref, sc_ref, sp_ref, snk_ref,
             dq_ref, dk_ref, dv_ref, dsnk_ref, ck_scr, cv_scr):
        i = pl.program_id(0)

        @pl.when(i == 0)
        def _():
            ck_scr[...] = jnp.zeros_like(ck_scr)
            cv_scr[...] = jnp.zeros_like(cv_scr)
            dsnk_ref[...] = jnp.zeros_like(dsnk_ref)

        notfirst = jnp.where(i < nb - 1, 1.0, 0.0).astype(F32)
        cc, cp, sc, sp = cc_ref[...], cp_ref[...], sc_ref[...], sp_ref[...]
        ck_in, cv_in = ck_scr[...], cv_scr[...]
        dq, dk, dv, ck, cv, dsnk = [], [], [], [], [], jnp.zeros(dsnk_ref.shape, F32)
        for g in range(2):
            sl = slice(g * LANE, (g + 1) * LANE)
            qg = jnp.concatenate([_rope(q_ref[:, (4 * g + r) * LANE:(4 * g + r + 1) * LANE], cc, sc)
                                  for r in range(4)], axis=1)
            _, vjp = jax.vjp(functools.partial(_swa_block, g=g), qg, _rope(kp_ref[:, sl], cp, sp),
                             _rope(kc_ref[:, sl], cc, sc), vp_ref[:, sl], vc_ref[:, sl], snk_ref[...], notfirst)
            dqg, dkp, dkc, dvp, dvc, dsnk_g, _ = vjp(dy_ref[:, g * GW:(g + 1) * GW])
            dq += [_unrope(dqg[:, r * LANE:(r + 1) * LANE], cc, sc) for r in range(4)]
            dk.append(_unrope(dkc, cc, sc) + ck_in[:, sl])
            dv.append(dvc + cv_in[:, sl])
            ck.append(_unrope(dkp, cp, sp))
            cv.append(dvp)
            dsnk = dsnk + dsnk_g
        dq_ref[...] = jnp.concatenate(dq, axis=1).astype(dq_ref.dtype)
        dk_ref[...] = jnp.concatenate(dk, axis=1).astype(dk_ref.dtype)
        dv_ref[...] = jnp.concatenate(dv, axis=1).astype(dv_ref.dtype)
        ck_scr[...] = jnp.concatenate(ck, axis=1)
        cv_scr[...] = jnp.concatenate(cv, axis=1)
        dsnk_ref[...] += dsnk

    rev = lambda i: nb - 1 - i
    return pl.pallas_call(
        body, name=name, grid=(nb,),
        in_specs=[pl.BlockSpec((ATT_W, 1024), lambda i: (rev(i), dy_off // 1024))] + _swa_specs(nb, True)
        + [_const_spec(sinks)],
        out_specs=[pl.BlockSpec((ATT_W, 1024), lambda i: (rev(i), 0)),
                   pl.BlockSpec((ATT_W, 256), lambda i: (rev(i), 0)),
                   pl.BlockSpec((ATT_W, 256), lambda i: (rev(i), 0)), _const_spec(sinks)],
        out_shape=[jax.ShapeDtypeStruct((t, 1024), BF16), jax.ShapeDtypeStruct((t, 256), BF16),
                   jax.ShapeDtypeStruct((t, 256), BF16), jax.ShapeDtypeStruct(sinks.shape, F32)],
        scratch_shapes=[pltpu.VMEM((ATT_W, 256), F32), pltpu.VMEM((ATT_W, 256), F32)],
        compiler_params=_cparams("arbitrary"),
    )(dy, proj, proj, proj, proj, proj, cos, cos, sin, sin, sinks)


def _me():
    return lax.axis_index("x"), lax.axis_index("y"), lax.axis_index("c")


ANY = pl.BlockSpec(memory_space=pl.ANY)


def allgather8(name, buf):
    r = buf.shape[0]

    def body(x_ref, out_ref, send_sems, recv_sems, local_sem):
        x, y, c = _me()
        me = 4 * x + 2 * y + c
        mine = pltpu.make_async_copy(x_ref, out_ref.at[me], local_sem)
        mine.start()
        copies = []
        for k in range(1, 8):
            bx, by, bc = (k >> 2) & 1, (k >> 1) & 1, k & 1
            to = ((1 - x) if bx else x, (1 - y) if by else y, (1 - c) if bc else c)
            cp = pltpu.make_async_remote_copy(src_ref=x_ref, dst_ref=out_ref.at[me], send_sem=send_sems.at[k - 1],
                                              recv_sem=recv_sems.at[k - 1], device_id=to, device_id_type=MESH)
            cp.start()
            copies.append(cp)
        for cp in copies:
            cp.wait_recv()
        for cp in copies:
            cp.wait_send()
        mine.wait()

    return pl.pallas_call(
        body, name=name, out_shape=jax.ShapeDtypeStruct((8, r, LANE), buf.dtype),
        in_specs=[pl.BlockSpec(memory_space=pltpu.VMEM)], out_specs=pl.BlockSpec(memory_space=pltpu.VMEM),
        scratch_shapes=[pltpu.SemaphoreType.DMA((7,)), pltpu.SemaphoreType.DMA((7,)), pltpu.SemaphoreType.DMA],
        compiler_params=pltpu.CompilerParams(vmem_limit_bytes=VMEM_LIMIT),
    )(buf)


def _sems(n):
    return [pltpu.SemaphoreType.DMA((n,)), pltpu.SemaphoreType.DMA((n,))]


def allgather_chips(name, shards):
    n = len(shards)

    def body(*refs):
        x_refs, out_refs, send_sems, recv_sems = refs[:n], refs[n:2 * n], refs[2 * n], refs[2 * n + 1]
        x, y, c = _me()
        chip = 2 * x + y
        chips = [(1 - x, y), (x, 1 - y), (1 - x, 1 - y)]

        def rc(src, dst, k, to):
            return pltpu.make_async_remote_copy(src_ref=src, dst_ref=dst, send_sem=send_sems.at[k],
                                                recv_sem=recv_sems.at[k], device_id=to, device_id_type=MESH)

        halves = [pl.ds(pl.multiple_of(c * (s.shape[0] // 2), 16), s.shape[0] // 2) for s in shards]
        own, first = [], {}
        for i in range(n):
            cp = rc(x_refs[i], out_refs[i].at[chip], 7 * i + 6, (x, y, 1 - c))
            cp.start()
            own.append(cp)
            for k, (cx, cy) in enumerate(chips):
                cp = rc(x_refs[i].at[halves[i]], out_refs[i].at[chip, halves[i]], 7 * i + k, (cx, cy, c))
                cp.start()
                first[i, k] = cp
        passed = []
        for i in range(n):
            for k, (cx, cy) in enumerate(chips):
                first[i, k].wait_recv()
                src = out_refs[i].at[2 * cx + cy, halves[i]]
                cp = rc(src, src, 7 * i + 3 + k, (x, y, 1 - c))
                cp.start()
                passed.append(cp)
        for cp in passed + own:
            cp.wait_recv()
        for cp in list(first.values()) + passed + own:
            cp.wait_send()

    for s in shards:
        assert s.shape[0] % 32 == 0
    return pl.pallas_call(
        body, name=name, out_shape=[jax.ShapeDtypeStruct((4,) + s.shape, s.dtype) for s in shards],
        in_specs=[ANY] * n, out_specs=[ANY] * n, scratch_shapes=_sems(7 * n),
    )(*shards)


def ag_forward(name, bufs):
    n = len(bufs)

    def body(*refs):
        out_refs, send_sems, recv_sems = refs[n:2 * n], refs[2 * n], refs[2 * n + 1]
        x, y, c = _me()
        cps = []
        for i in range(n):
            rh = bufs[i].shape[1] // 2
            half = pl.ds(pl.multiple_of(c * rh, 16), rh)
            for k, (cx, cy) in enumerate([(1 - x, y), (x, 1 - y), (1 - x, 1 - y)]):
                src = out_refs[i].at[2 * cx + cy, half]
                cps.append(pltpu.make_async_remote_copy(src_ref=src, dst_ref=src, send_sem=send_sems.at[3 * i + k],
                                                        recv_sem=recv_sems.at[3 * i + k], device_id=(x, y, 1 - c),
                                                        device_id_type=MESH))
        for cp in cps:
            cp.start()
        for cp in cps:
            cp.wait_recv()
        for cp in cps:
            cp.wait_send()

    return pl.pallas_call(
        body, name=name, out_shape=[jax.ShapeDtypeStruct(b.shape, b.dtype) for b in bufs],
        in_specs=[ANY] * n, out_specs=[ANY] * n, scratch_shapes=_sems(3 * n),
        input_output_aliases={i: i for i in range(n)},
    )(*bufs)


def sibling_swap_halves(name, gs):
    n = len(gs)

    def body(*refs):
        g_refs, out_refs, send_sems, recv_sems = refs[:n], refs[n:2 * n], refs[2 * n], refs[2 * n + 1]
        x, y, c = _me()
        cps = [pltpu.make_async_remote_copy(src_ref=g_refs[i].at[:, 1 - c], dst_ref=out_refs[i],
                                            send_sem=send_sems.at[i], recv_sem=recv_sems.at[i],
                                            device_id=(x, y, 1 - c), device_id_type=MESH) for i in range(n)]
        for cp in cps:
            cp.start()
        for cp in cps:
            cp.wait_recv()
        for cp in cps:
            cp.wait_send()

    return pl.pallas_call(
        body, name=name, out_shape=[jax.ShapeDtypeStruct((4,) + g.shape[2:], g.dtype) for g in gs],
        in_specs=[ANY] * n, out_specs=[ANY] * n, scratch_shapes=_sems(n),
    )(*gs)


def chips_exchange(name, ps):
    n = len(ps)

    def body(*refs):
        p_refs, out_refs, send_sems, recv_sems = refs[:n], refs[n:2 * n], refs[2 * n], refs[2 * n + 1]
        x, y, c = _me()
        chip = 2 * x + y
        cps = []
        for i in range(n):
            for k, (cx, cy) in enumerate([(1 - x, y), (x, 1 - y), (1 - x, 1 - y)]):
                cps.append(pltpu.make_async_remote_copy(
                    src_ref=p_refs[i].at[2 * cx + cy], dst_ref=out_refs[i].at[chip], send_sem=send_sems.at[3 * i + k],
                    recv_sem=recv_sems.at[3 * i + k], device_id=(cx, cy, c), device_id_type=MESH))
        for cp in cps:
            cp.start()
        for cp in cps:
            cp.wait_recv()
        for cp in cps:
            cp.wait_send()

    return pl.pallas_call(
        body, name=name, out_shape=[jax.ShapeDtypeStruct(p.shape, p.dtype) for p in ps],
        in_specs=[ANY] * n, out_specs=[ANY] * n, scratch_shapes=_sems(3 * n),
    )(*ps)


def sibling_fill(name, ss):
    n = len(ss)

    def body(*refs):
        out_refs, send_sems, recv_sems = refs[n:2 * n], refs[2 * n], refs[2 * n + 1]
        x, y, c = _me()
        cps = []
        for i in range(n):
            rh = ss[i].shape[1] // 2
            mine = out_refs[i].at[:, pl.ds(pl.multiple_of(c * rh, 8), rh)]
            cps.append(pltpu.make_async_remote_copy(src_ref=mine, dst_ref=mine, send_sem=send_sems.at[i],
                                                    recv_sem=recv_sems.at[i], device_id=(x, y, 1 - c),
                                                    device_id_type=MESH))
        for cp in cps:
            cp.start()
        for cp in cps:
            cp.wait_recv()
        for cp in cps:
            cp.wait_send()

    return pl.pallas_call(
        body, name=name, out_shape=[jax.ShapeDtypeStruct(s.shape, s.dtype) for s in ss],
        in_specs=[ANY] * n, out_specs=[ANY] * n, scratch_shapes=_sems(n),
        input_output_aliases={i: i for i in range(n)},
    )(*ss)


def _rs_tile(rh, w):
    return _pick(rh, max(8, (1 << 19) // w // 8 * 8), 8)


def add_half(name, g, recv, idx):
    _, _, rh, w = g.shape
    tile = _rs_tile(rh, w)

    def body(s_ref, g_ref, r_ref, o_ref):
        o_ref[...] = (g_ref[...] + r_ref[...]).astype(o_ref.dtype)

    return pl.pallas_call(
        body, name=name,
        grid_spec=pltpu.PrefetchScalarGridSpec(
            num_scalar_prefetch=1, grid=(4, rh // tile),
            in_specs=[pl.BlockSpec((None, None, tile, w), lambda s, i, ix: (s, ix[1], i, 0)),
                      pl.BlockSpec((None, tile, w), lambda s, i, ix: (s, i, 0))],
            out_specs=pl.BlockSpec((None, tile, w), lambda s, i, ix: (s, i, 0))),
        out_shape=jax.ShapeDtypeStruct((4, rh, w), BF16),
        compiler_params=_cparams("parallel", "parallel"),
    )(idx, g, recv)


def rs_sum(name, g, recv1, recv2, idx, layer, acc=None):
    _, _, rh, w = g.shape
    tile = _rs_tile(rh, w)
    nt = rh // tile

    def body(s_ref, g_ref, r1_ref, a_ref, b_ref, c_ref, *rest):
        rest[-1][...] = (g_ref[...] + r1_ref[...]) + (a_ref[...].astype(F32) + b_ref[...].astype(F32)
                                                       + c_ref[...].astype(F32))

    def other(k):
        return pl.BlockSpec((None, tile, w), lambda i, ix: (jnp.bitwise_xor(ix[0], k), i, 0))

    ops = [idx, g, recv1, recv2, recv2, recv2] + ([acc] if acc is not None else [])
    return pl.pallas_call(
        body, name=name,
        grid_spec=pltpu.PrefetchScalarGridSpec(
            num_scalar_prefetch=1, grid=(rh // tile,),
            in_specs=[pl.BlockSpec((None, None, tile, w), lambda i, ix: (ix[0], ix[1], i, 0)),
                      pl.BlockSpec((None, tile, w), lambda i, ix: (ix[0], i, 0)), other(1), other(2), other(3)]
            + ([ANY] if acc is not None else []),
            out_specs=pl.BlockSpec((None, tile, w), lambda i, ix: (layer, ix[1] * nt + i, 0))),
        out_shape=jax.ShapeDtypeStruct((DEPTH, 2 * rh, w), F32),
        input_output_aliases={6: 0} if acc is not None else {},
        compiler_params=_cparams("parallel"),
    )(*ops)


def sum_slabs(name, a):
    n, r, _ = a.shape
    tile = _pick(r, 2048, 8)

    def body(a_ref, o_ref):
        acc = a_ref[0]
        for s in range(1, n):
            acc = acc + a_ref[s]
        o_ref[...] = acc

    return pl.pallas_call(
        body, name=name, grid=(r // tile,),
        in_specs=[pl.BlockSpec((n, tile, LANE), lambda i: (0, i, 0))],
        out_specs=pl.BlockSpec((tile, LANE), lambda i: (i, 0)),
        out_shape=jax.ShapeDtypeStruct((r, LANE), F32),
        compiler_params=_cparams("parallel"),
    )(a)


def f_premod(rv, cv):
    (x,), (w, scale, shift) = rv, cv
    y = x * lax.rsqrt(jnp.mean(x * x, -1, keepdims=True) + EPS) * w
    return [y * (1.0 + scale) + shift]


def f_postres(rv, cv):
    (x, y), (w, gate) = rv, cv
    return [x + gate * (y * lax.rsqrt(jnp.mean(y * y, -1, keepdims=True) + EPS) * w)]


def f_res_premod(rv, cv):
    (x, y), (w_post, gate, w_pre, scale, shift) = rv, cv
    x1, = f_postres([x, y], [w_post, gate])
    h, = f_premod([x1], [w_pre, scale, shift])
    return [x1, h]


def f_adamw(rv, cv):
    w, g, m, v = rv
    m = B1 * m + (1.0 - B1) * g
    v = B2 * v + (1.0 - B2) * jnp.square(g)
    m_hat = m / (1.0 - B1 ** STEP)
    v_hat = v / (1.0 - B2 ** STEP)
    return [-LR * (m_hat / (jnp.sqrt(v_hat) + AEPS) + WD * w), m, v], []


def adamw_layers(name, w, g, m, v):
    nl, r, wd = w.shape
    tile = r
    while tile * wd * 4 * 16 > 24 * 1024 * 1024 and tile % 16 == 0:
        tile //= 2

    outs = None
    for l in range(nl):
        def body(w_ref, g_ref, m_ref, v_ref, *rest):
            gv = g_ref[...]
            (dl, m2, v2), _ = f_adamw([w_ref[...], gv, m_ref[...], v_ref[...]], [])
            for ref, val in zip(rest[-4:], (gv, dl, m2, v2)):
                ref[...] = val

        blk = pl.BlockSpec((None, tile, wd), lambda i, l=l: (l, i, 0))
        outs = pl.pallas_call(
            body, name=f"{name}{l}", grid=(r // tile,), in_specs=[blk] * 4 + ([ANY] * 4 if outs else []),
            out_specs=[blk] * 4, out_shape=[jax.ShapeDtypeStruct(w.shape, F32)] * 4,
            input_output_aliases={4: 0, 5: 1, 6: 2, 7: 3} if outs else {}, compiler_params=_cparams("parallel"),
        )(w, g, m, v, *(outs or ()))
    return outs


def adamw(name, w, g, m, v):
    r, wd = w.shape
    tile = r
    while tile * wd * 4 * 14 > 24 * 1024 * 1024 and tile % 16 == 0:
        tile //= 2
    (d, m2, v2), _ = rowmap(name, f_adamw, [(a, 0, wd) for a in (w, g, m, v)], [], [(wd, F32)] * 3, tile=tile)
    return d, m2, v2


_IN_SIZES = (512, 512, 512, 512, 4, 4, 512, 1024, 8, 512, 128, 128, 512, 512, 512)


def _relayout_w_in(w):
    d = w.shape[0]
    s = np.cumsum((0,) + _IN_SIZES)
    gq, gk, gv, gz, gb, ga, sz, sx, sdt, aq, ak, av, cb, cc, ch = [w[:, s[i]:s[i + 1]] for i in range(15)]

    def rope_pad(t, nh):
        return jnp.pad(t.reshape(d, nh, 2, 32), ((0, 0), (0, 0), (0, 0), (0, 32))).reshape(d, nh * LANE)

    def v_pad(t, nh):
        return jnp.pad(t.reshape(d, nh, 64), ((0, 0), (0, 0), (0, 64))).reshape(d, nh * LANE)

    def pad128(t):
        return jnp.pad(t, ((0, 0), (0, LANE - t.shape[1])))

    return jnp.concatenate([gq, gk, gv, cb, cc, ch, sx, rope_pad(aq, 8), gz, sz, rope_pad(ak, 2), v_pad(av, 2),
                            pad128(jnp.concatenate([gb, ga], axis=1)), pad128(sdt)], axis=1)


def _unlayout_g_in(g):
    d = g.shape[0]

    def rope_unpad(t, nh):
        return t.reshape(d, nh, 2, 64)[..., :32].reshape(d, nh * 64)

    def v_unpad(t, nh):
        return t.reshape(d, nh, LANE)[..., :64].reshape(d, nh * 64)

    return jnp.concatenate([g[:, 0:1536], g[:, O_GZ:O_GZ + 512], g[:, O_GBA:O_GBA + 8], g[:, O_SZ:O_SZ + 512],
                            g[:, O_XBC:O_XBC + 1024], g[:, O_SDT:O_SDT + 8], rope_unpad(g[:, O_AQ:O_AQ + 1024], 8),
                            rope_unpad(g[:, O_AK:O_AK + 256], 2), v_unpad(g[:, O_AV:O_AV + 256], 2),
                            g[:, O_CB:O_CB + 1536]], axis=1)


def _relayout_w_out(w):
    d = w.shape[1]
    yc = jnp.pad(w[1024:1536].reshape(8, 64, d), ((0, 0), (0, 64), (0, 0))).reshape(1024, d)
    return jnp.concatenate([w[:1024], yc, w[1536:]], axis=0)


def _unlayout_g_out(g):
    d = g.shape[1]
    return jnp.concatenate([g[:1024], g[1024:2048].reshape(8, LANE, d)[:, :64].reshape(512, d), g[2048:]], axis=0)


def _pack(arrs):
    rows, spec, off = [], [], 0
    for a in arrs:
        n = int(np.prod(a.shape))
        nr = -(-n // LANE)
        rows.append(jnp.pad(a.reshape(-1).astype(F32), (0, nr * LANE - n)).reshape(nr, LANE))
        spec.append((off, a.shape))
        off += nr
    pad = (-off) % 8
    if pad:
        rows.append(jnp.zeros((pad, LANE), F32))
    return jnp.concatenate(rows, axis=0), spec


def _unpack(buf, spec):
    out = []
    for off, shape in spec:
        n = int(np.prod(shape))
        nr = -(-n // LANE)
        out.append(buf[off:off + nr].reshape(-1)[:n].reshape(shape))
    return out


def _row128(v):
    return jnp.pad(v.astype(F32), (0, LANE - v.shape[0])).reshape(1, LANE)


BIG = ("w_in", "w_out", "w_up", "w_down")
SMALL_REPL = ("ada_b", "norm_pre_mix", "norm_post_mix", "norm_pre_mlp", "norm_post_mlp", "gdn_a_log", "gdn_dt_bias",
              "gdn_norm_w", "ssm_conv_b", "ssm_a_log", "ssm_dt_bias", "ssm_d", "ssm_norm_w", "attn_sinks")
SMALL_SHARD = ("gdn_conv_w", "ssm_conv_w", "sc_conv_w")
WEIGHTS = ("ada_w", "ada_b", "norm_pre_mix", "norm_post_mix", "norm_pre_mlp", "norm_post_mlp", "w_in", "w_out",
           "gdn_conv_w", "gdn_a_log", "gdn_dt_bias", "gdn_norm_w", "ssm_conv_w", "ssm_conv_b", "ssm_a_log",
           "ssm_dt_bias", "ssm_d", "ssm_norm_w", "attn_sinks", "sc_conv_w", "w_up", "w_down")


def _step(p, m, v, x, c, positions, loss_target):
    d = x.shape[2]
    xi, yi, ci = _me()
    chip = 2 * xi + yi
    me = 4 * xi + 2 * yi + ci
    x0 = x[0]
    tgt = loss_target[0]

    buf, spec = _pack([c[0], p["gdn_conv_w"], p["ssm_conv_w"], p["sc_conv_w"]])
    got = allgather8("ag_small_in", buf)
    per_dev = [_unpack(got[k], spec) for k in range(8)]
    c_all = jnp.stack([pd[0] for pd in per_dev], axis=0)
    conv_full = [jnp.concatenate([per_dev[4 * a + 2 * b][1 + j] for a in range(2) for b in range(2)], axis=-1)
                 for j in range(3)]
    gdn_conv_w, ssm_conv_w, sc_conv_w = conv_full

    c_pad = jnp.pad(c_all, ((0, 8), (0, 0)))
    silu_bf = lambda a: _silu(a).astype(BF16)
    nsh = p["ada_w"].shape[2]
    mods = []
    for l in range(DEPTH):
        mo = mm_nn(f"ada_fwd{l}", c_pad, p["ada_w"][l], F32, a_fn=silu_bf)
        mods.append(mo[:8] + lax.dynamic_slice(p["ada_b"][l], (chip * nsh,), (nsh,))[None])
    buf, spec = _pack([jnp.stack(mods, axis=0)])
    got = allgather8("ag_mod", buf)
    mod_sh = [_unpack(got[4 * a + 2 * b], spec)[0] for a in range(2) for b in range(2)]
    mod_all = jnp.concatenate(mod_sh, axis=-1)
    mod = lax.dynamic_index_in_dim(mod_all, me, axis=1, keepdims=False)

    items = [(l, n) for l in range(DEPTH) for n in BIG]
    shards = {(l, n): p[n][l].astype(BF16) for (l, n) in items}
    gathered = {(0, "w_in"): allgather_chips("ag_w_in0", [shards[0, "w_in"]])[0]}

    def gather_behind(keys):
        return ("gather", [shards[kk] for kk in keys])

    def settle(name, keys, landed):
        gathered.update(zip(keys, ag_forward(name, landed)))

    wfull = []

    inv_freq = ROPE_THETA ** (-jnp.arange(0, 64, 2, dtype=F32) / 64)
    ang = positions[0].astype(F32)[:, None] * inv_freq
    z32 = jnp.zeros_like(ang)
    cos_t = jnp.concatenate([jnp.cos(ang), z32, jnp.cos(ang), z32], axis=1)
    sin_t = jnp.concatenate([-jnp.sin(ang), z32, jnp.sin(ang), z32], axis=1)

    relu2 = lambda a: jnp.square(jnp.maximum(a, 0))

    def layer_consts(l):
        row = lambda v_: v_.reshape(1, -1).astype(F32)
        return dict(
            alog=_row128(p["gdn_a_log"][l]), gdtb=_row128(p["gdn_dt_bias"][l]), gnw=row(p["gdn_norm_w"][l]),
            scb=row(p["ssm_conv_b"][l]), salog=_row128(p["ssm_a_log"][l]), sdtb=_row128(p["ssm_dt_bias"][l]),
            sdsk=_row128(p["ssm_d"][l]), snw=row(p["ssm_norm_w"][l]), sinks=_row128(p["attn_sinks"][l]))

    saved = []
    xc = x0
    for l in range(DEPTH):
        w = dict(w_in=_relayout_w_in(jnp.concatenate([gathered[l, "w_in"][s] for s in range(4)], axis=1)))
        wfull.append(w)
        k = layer_consts(l)
        md = [mod[l, j * d:(j + 1) * d].reshape(1, d) for j in range(6)]
        npre, npost = p["norm_pre_mix"][l].reshape(1, d), p["norm_post_mix"][l].reshape(1, d)
        npre2, npost2 = p["norm_pre_mlp"][l].reshape(1, d), p["norm_post_mlp"][l].reshape(1, d)
        if l == 0:
            (h,), _ = rowmap("premix0", lambda rv, cv: (f_premod(rv, cv), []), [(xc, 0, d)], [npre, md[1], md[0]],
                             [(d, BF16)])
        behind = [(0, "w_out"), (0, "w_up")] if l == 0 else [(l, "w_up")]
        proj, landed = mm_nn(f"inproj{l}", h, w["w_in"], F32, carry=gather_behind(behind))
        settle(f"ag_forward_a{l}", behind, landed)
        w["w_out"] = _relayout_w_out(gathered[l, "w_out"].reshape(-1, d))
        w["w_up"] = gathered[l, "w_up"]
        cq = conv_fwd(f"gdn_conv{l}", [(proj, O_GQKV, 1536)], gdn_conv_w[l])
        ya, gs, gt = gdn_fwd(f"gdn{l}", cq, proj, k["alog"], k["gdtb"], k["gnw"])
        cx = conv_fwd(f"ssm_conv{l}", [(proj, O_XBC, 1024)], ssm_conv_w[l])
        yb, ss = ssd_fwd(f"ssd{l}", cx, proj, k["scb"], k["salog"], k["sdtb"], k["sdsk"], k["snw"])
        yc = swa_fwd(f"swa{l}", proj, cos_t, sin_t, k["sinks"])
        cd = conv_fwd(f"sc_conv{l}", [(proj, O_CC, GW), (proj, O_CH, GW)], sc_conv_w[l])
        (ycat,), _ = rowmap(f"ycat{l}", lambda rv, cv: ([jnp.concatenate([rv[0], rv[1], rv[2], rv[3] * rv[4]], 1)], []),
                            [(ya, 0, GW), (yb, 0, GW), (yc, 0, 1024), (proj, O_CB, GW), (cd, 0, GW)], [],
                            [(YCAT, BF16)])
        y = mm_nn(f"outproj{l}", ycat, w["w_out"], F32)
        (x1, h2), _ = rowmap(f"mix_mlp{l}", lambda rv, cv: (f_res_premod(rv, cv), []), [(xc, 0, d), (y, 0, d)],
                             [npost, md[2], npre2, md[4], md[3]], [(d, F32), (d, BF16)])
        u, landed = mm_nn(f"up{l}", h2, w["w_up"], BF16, carry=gather_behind([(l, "w_down")]))
        settle(f"ag_forward_b{l}", [(l, "w_down")], landed)
        w["w_down"] = gathered[l, "w_down"].reshape(-1, d)
        if l + 1 < DEPTH:
            behind = [(l + 1, "w_in"), (l + 1, "w_out")]
            y2, landed = mm_nn(f"down{l}", u, w["w_down"], F32, a_fn=relu2, carry=gather_behind(behind))
            settle(f"ag_forward_c{l}", behind, landed)
        else:
            y2 = mm_nn(f"down{l}", u, w["w_down"], F32, a_fn=relu2)
        saved.append(dict(x=xc, h=h, proj=proj, cq=cq, gs=gs, gt=gt, cx=cx, ss=ss, cd=cd, ycat=ycat, y=y, x1=x1, h2=h2, u=u,
                          y2=y2, md=md, k=k))
        if l + 1 < DEPTH:
            (xc, h), _ = rowmap(f"mlp_mix{l}", lambda rv, cv: (f_res_premod(rv, cv), []), [(x1, 0, d), (y2, 0, d)],
                                [npost2, md[5], p["norm_pre_mix"][l + 1].reshape(1, d),
                                 mod[l + 1, d:2 * d].reshape(1, d), mod[l + 1, 0:d].reshape(1, d)],
                                [(d, F32), (d, BF16)])

    def f_loss(vals, cvals):
        x1_, y2_, tgt_ = vals
        (x2,), vjp = jax.vjp(lambda r, c: f_postres(list(r), c), [x1_, y2_], cvals)
        err = x2 - tgt_
        part = 0.5 * jnp.sum(jnp.mean(err * err, axis=-1, keepdims=True), axis=0, keepdims=True)
        dr, dc = vjp([err * (1.0 / d)])
        return list(dr), list(dc) + [jnp.broadcast_to(part, (1, LANE))]

    s = saved[-1]
    (dx1a, dy2), (dnpost2, dgate_m, loss_part) = rowmap(
        "loss_postmlp_b", f_loss, [(s["x1"], 0, d), (s["y2"], 0, d), (tgt, 0, d)],
        [p["norm_post_mlp"][DEPTH - 1].reshape(1, d), s["md"][5]], [(d, F32), (d, BF16)],
        [(1, d), (1, d), (1, LANE)], tile=128)

    idx = jnp.stack([chip, ci]).astype(jnp.int32)
    g4s, recv1s, recv2s, waiting = {}, {}, {}, []

    def reduce_start(key, g):
        l_, n_ = key
        g4 = g.reshape(4, 2, g.shape[1] // 2, g.shape[2])
        r1 = sibling_swap_halves(f"rs_sibling_{n_}{l_}", [g4])[0]
        g4s[key], recv1s[key] = g4, r1
        waiting.append((key, add_half(f"rs_add_{n_}{l_}", g4, r1, idx)))

    def mm_carrying(name, *args, **kw):
        if not waiting:
            return mm_nn(name, *args, **kw)
        out, landed = mm_nn(name, *args, carry=("exchange", [pt for _, pt in waiting]), **kw)
        recv2s.update(zip([kk for kk, _ in waiting], landed))
        waiting.clear()
        return out

    gsm, dmd = {}, {}
    for l in reversed(range(DEPTH)):
        w, s = wfull[l], saved[l]
        k, md = s["k"], s["md"]
        npre, npost = p["norm_pre_mix"][l].reshape(1, d), p["norm_post_mix"][l].reshape(1, d)
        npre2, npost2 = p["norm_pre_mlp"][l].reshape(1, d), p["norm_post_mlp"][l].reshape(1, d)
        if l == DEPTH - 1:
            gsm[l, "norm_post_mlp"], dmd[l, 5] = dnpost2[0], dgate_m
        du = mm_carrying(f"down_b{l}", dy2, w["w_down"], BF16, extra=s["u"], wt=True,
                         epi=lambda r, uu: r * (2.0 * jnp.maximum(uu.astype(F32), 0.0)))
        reduce_start((l, "w_down"), mm_tn(f"down_g{l}", s["u"], dy2, a_fn=relu2).reshape(4, -1, d))
        dh2 = mm_carrying(f"up_b{l}", du, w["w_up"], F32, wt=True)
        reduce_start((l, "w_up"), mm_tn(f"up_g{l}", s["h2"], du, col_slabs=4))
        (dxa, dy), (dnpost, dgate_a, dnpre2, dscale_m, dshift_m) = rowmap_vjp(
            f"mix_mlp_b{l}", f_res_premod, [(s["x"], 0, d), (s["y"], 0, d)], [npost, md[2], npre2, md[4], md[3]],
            [(dx1a, 0, d), (dh2, 0, d)], 2, [F32, BF16], tile=128)
        dycat = mm_nn(f"outproj_b{l}", dy, w["w_out"], F32, wt=True)
        reduce_start((l, "w_out"), _unlayout_g_out(mm_tn(f"outproj_g{l}", s["ycat"], dy)).reshape(4, -1, d))
        (dcb, dcd), _ = rowmap(f"scgate_b{l}", lambda rv, cv: ([rv[0] * rv[2], rv[0] * rv[1]], []),
                               [(dycat, 2048, GW), (s["proj"], O_CB, GW), (s["cd"], 0, GW)], [],
                               [(GW, BF16), (GW, F32)])
        (dcc, dch), g_scw = conv_bwd(f"sc_conv_b{l}", dcd, [(s["proj"], O_CC, GW), (s["proj"], O_CH, GW)],
                                     sc_conv_w[l], BF16)
        daq, dak, dav, g_sinks = swa_bwd(f"swa_b{l}", dycat, 1024, s["proj"], cos_t, sin_t, k["sinks"])
        dcx, dsz, dsdt, g_scb, g_salog, g_sdtb, g_sdsk, g_snw = ssd_bwd(
            f"ssd_b{l}", dycat, 512, s["cx"], s["proj"], s["ss"], k["scb"], k["salog"], k["sdtb"], k["sdsk"], k["snw"])
        (dxbc,), g_ssmw = conv_bwd(f"ssm_conv_b{l}", dcx, [(s["proj"], O_XBC, 1024)], ssm_conv_w[l], BF16)
        dcq, dgz, dgba, g_alog, g_gdtb, g_gnw = gdn_bwd(f"gdn_b{l}", dycat, 0, s["cq"], s["proj"], s["gs"], s["gt"],
                                                        k["alog"], k["gdtb"], k["gnw"])
        (dgqkv,), g_gdnw = conv_bwd(f"gdn_conv_b{l}", dcq, [(s["proj"], O_GQKV, 1536)], gdn_conv_w[l], BF16)
        dproj = jnp.concatenate([dgqkv, dcb, dcc, dch, dxbc, daq, dgz, dsz, dak, dav, dgba, dsdt], axis=1)
        def start_w_in():
            g_in = mm_tn(f"inproj_g{l}", s["h"], dproj)
            reduce_start((l, "w_in"), jnp.stack(jnp.split(_unlayout_g_in(g_in), 4, axis=1), axis=0))

        if l == 0:
            start_w_in()
        dh = mm_carrying(f"inproj_b{l}", dproj, w["w_in"], F32, wt=True)
        if l > 0:
            start_w_in()
            sp = saved[l - 1]
            (dx1a, dy2), (dnpost2, dgate_m, dnpre, dscale_a, dshift_a) = rowmap_vjp(
                f"mlp_mix_b{l - 1}", f_res_premod, [(sp["x1"], 0, d), (sp["y2"], 0, d)],
                [p["norm_post_mlp"][l - 1].reshape(1, d), sp["md"][5], npre, md[1], md[0]],
                [(dxa, 0, d), (dh, 0, d)], 2, [F32, BF16], tile=128)
            gsm[l - 1, "norm_post_mlp"], dmd[l - 1, 5] = dnpost2[0], dgate_m
        else:
            (dx,), (dnpre, dscale_a, dshift_a) = rowmap_vjp("premix_b0", f_premod, [(s["x"], 0, d)],
                                                            [npre, md[1], md[0]], [(dh, 0, d)], 1, [F32],
                                                            add_rows=[(dxa, 0, d)])
        dmd.update({(l, 0): dshift_a, (l, 1): dscale_a, (l, 2): dgate_a, (l, 3): dshift_m, (l, 4): dscale_m})
        gsm.update({(l, n_): g_ for n_, g_ in dict(
            norm_pre_mix=dnpre[0], norm_post_mix=dnpost[0], norm_pre_mlp=dnpre2[0],
            gdn_a_log=g_alog[0, :4], gdn_dt_bias=g_gdtb[0, :4], gdn_norm_w=g_gnw[0], ssm_conv_b=g_scb[0],
            ssm_a_log=g_salog[0, :8], ssm_dt_bias=g_sdtb[0, :8], ssm_d=g_sdsk[0, :8], ssm_norm_w=g_snw[0],
            attn_sinks=g_sinks[0, :8], gdn_conv_w=g_gdnw, ssm_conv_w=g_ssmw, sc_conv_w=g_scw).items()})
    grad_x = dx[None]
    dmods = [jnp.concatenate([dmd[l, j] for j in range(6)], axis=1)[0] for l in range(DEPTH)]
    gsmall = [{n_: gsm[l, n_] for n_ in SMALL_REPL + SMALL_SHARD if n_ != "ada_b"} for l in range(DEPTH)]

    buf, spec = _pack([jnp.stack(dmods, axis=0)])
    got = allgather8("ag_dmod", buf)
    dmod_sum = _unpack(sum_slabs("dmod_sum", got), spec)[0]
    dmod_all = jnp.stack([_unpack(got[kk], spec)[0] for kk in range(8)], axis=1)
    g_ada_w = []
    for l in range(DEPTH):
        dm = lax.dynamic_slice(dmod_all[l], (0, chip * nsh), (8, nsh))
        g_ada_w.append(mm_tn(f"ada_g{l}", c_pad, jnp.pad(dm, ((0, 8), (0, 0))), a_fn=silu_bf))
    g_ada_w = jnp.stack(g_ada_w, axis=0)

    names = [n for n in SMALL_REPL if n != "ada_b"] + list(SMALL_SHARD)
    buf, spec = _pack([jnp.stack([gsmall[l][n] for l in range(DEPTH)], axis=0) for n in names] + [loss_part])
    tot = _unpack(sum_slabs("small_sum", allgather8("ag_small_g", buf)), spec)
    gfull = dict(zip(names, tot[:-1]))
    loss = tot[-1][0, 0]
    grads = {"ada_w": g_ada_w, "ada_b": dmod_sum}
    for n in SMALL_REPL:
        if n != "ada_b":
            grads[n] = gfull[n]
    for n in SMALL_SHARD:
        wdt = p[n].shape[2]
        grads[n] = lax.dynamic_slice_in_dim(gfull[n], chip * wdt, wdt, axis=2)

    if waiting:
        recv2s.update(zip([kk for kk, _ in waiting], chips_exchange("rs_chips_last", [pt for _, pt in waiting])))
    half = []
    for n in BIG:
        acc = None
        for l in range(DEPTH):
            acc = rs_sum(f"rs_sum_{n}{l}", g4s[l, n], recv1s[l, n], recv2s[l, n], idx, l, acc)
        half.append(acc)
    grads.update(zip(BIG, sibling_fill("rs_join", half)))

    delta, new_m, new_v = {}, {}, {}
    for n in ("ada_w",) + BIG:
        grads[n], delta[n], new_m[n], new_v[n] = adamw_layers(f"adamw_{n}", p[n], grads[n], m[n], v[n])
    small = list(SMALL_REPL) + list(SMALL_SHARD)
    bufs = []
    for src in (p, grads, m, v):
        b_, spec = _pack([src[n] for n in small])
        bufs.append(b_)
    dl, m2, v2 = adamw("adamw_small", *bufs)
    for n, a, b_, cc_ in zip(small, _unpack(dl, spec), _unpack(m2, spec), _unpack(v2, spec)):
        delta[n], new_m[n], new_v[n] = a, b_, cc_

    return (loss, grad_x, *[grads[n] for n in WEIGHTS], *[delta[n] for n in WEIGHTS], *[new_m[n] for n in WEIGHTS],
            *[new_v[n] for n in WEIGHTS])


def kernel(x, c, positions, ada_w, ada_b, norm_pre_mix, norm_post_mix, norm_pre_mlp, norm_post_mlp, w_in, w_out, gdn_conv_w, gdn_a_log, gdn_dt_bias, gdn_norm_w, ssm_conv_w, ssm_conv_b, ssm_a_log, ssm_dt_bias, ssm_d, ssm_norm_w, attn_sinks, sc_conv_w, w_up, w_down, loss_target, m_ada_w, m_ada_b, m_norm_pre_mix, m_norm_post_mix, m_norm_pre_mlp, m_norm_post_mlp, m_w_in, m_w_out, m_gdn_conv_w, m_gdn_a_log, m_gdn_dt_bias, m_gdn_norm_w, m_ssm_conv_w, m_ssm_conv_b, m_ssm_a_log, m_ssm_dt_bias, m_ssm_d, m_ssm_norm_w, m_attn_sinks, m_sc_conv_w, m_w_up, m_w_down, v_ada_w, v_ada_b, v_norm_pre_mix, v_norm_post_mix, v_norm_pre_mlp, v_norm_post_mlp, v_w_in, v_w_out, v_gdn_conv_w, v_gdn_a_log, v_gdn_dt_bias, v_gdn_norm_w, v_ssm_conv_w, v_ssm_conv_b, v_ssm_a_log, v_ssm_dt_bias, v_ssm_d, v_ssm_norm_w, v_attn_sinks, v_sc_conv_w, v_w_up, v_w_down):
    loc = locals()
    p = {n: loc[n] for n in WEIGHTS}
    m = {n: loc["m_" + n] for n in WEIGHTS}
    v = {n: loc["v_" + n] for n in WEIGHTS}
    return _step(p, m, v, x, c, positions, loss_target)
```

```python
import functools

import numpy as np
import jax
import jax.numpy as jnp
from jax import lax
from jax.experimental import pallas as pl
from jax.experimental.pallas import tpu as pltpu

F32, BF16 = jnp.float32, jnp.bfloat16
MESH = pl.DeviceIdType.MESH

DEPTH = 2
GW = 512
N_GDN, GDN_CS = 4, 64
SSD_CS, ATT_W = 128, 128
EPS = 1e-6
ROPE_THETA = 10000.0
IN_WIDTH = 5904
B1, B2, LR, AEPS, WD, STEP = 0.9, 0.999, 0.001, 1e-8, 0.01, 10

VMEM_LIMIT = 52 * 1024 * 1024
LANE = 128

O_GQKV, O_CB, O_CC, O_CH, O_XBC, O_AQ, O_GZ, O_SZ, O_AK, O_AV, O_GBA, O_SDT = (
    0, 1536, 2048, 2560, 3072, 4096, 5120, 5632, 6144, 6400, 6656, 6784)
PW = 6912
YCAT = 2560


def _cparams(*sem):
    return pltpu.CompilerParams(dimension_semantics=sem or None, vmem_limit_bytes=VMEM_LIMIT)


def _pick(n, cap, mult):
    if n <= cap:
        return n
    best = None
    for d in range(mult, cap + 1, mult):
        if n % d == 0:
            best = d
    assert best is not None, (n, cap, mult)
    return best


def _pieces(x, n):
    out, r = [], x
    for i in range(n):
        h = r.astype(BF16)
        out.append(h)
        if i < n - 1:
            r = r - h.astype(F32)
    return out


def _dgp(a, b, ca, cb, na, nb):
    dn = (((ca,), (cb,)), ((), ()))
    acc = None
    for i, pa in enumerate(_pieces(a, na)):
        for j, pb in enumerate(_pieces(b, nb)):
            if i + j < max(na, nb):
                t = lax.dot_general(pa, pb, dn, preferred_element_type=F32)
                acc = t if acc is None else acc + t
    return acc


_CONTRACT = {"nn": (1, 0), "nt": (1, 1), "tn": (0, 0)}


@functools.lru_cache(maxsize=None)
def _mm(kind, na, nb, const_a=False):
    ca, cb = _CONTRACT[kind]
    ng = max(na, nb)

    @jax.custom_vjp
    def f(a, b):
        return _dgp(a, b, ca, cb, na, nb)

    def bwd(r, g):
        a, b = r
        if kind == "nn":
            da, db = (lambda: _mm("nt", ng, nb)(g, b)), _mm("tn", na, ng, const_a)(a, g)
        elif kind == "nt":
            da, db = (lambda: _mm("nn", ng, nb)(g, b)), _mm("tn", ng, na)(g, a)
        else:
            da, db = (lambda: _mm("nt", nb, ng)(b, g)), _mm("nn", na, ng, const_a)(a, g)
        return (jnp.zeros_like(a) if const_a else da()), db

    f.defvjp(lambda a, b: (f(a, b), (a, b)), bwd)
    return f


_nn, _nt, _tn = _mm("nn", 1, 1), _mm("nt", 1, 1), _mm("tn", 1, 1)
_hnn, _hnt, _htn = _mm("nn", 2, 2), _mm("nt", 2, 2), _mm("tn", 2, 2)
_h2nn = _mm("nn", 2, 1)
_xnn = _mm("nn", 1, 3, True)


def _silu(x):
    return x * jax.nn.sigmoid(x)


def _softplus(x):
    return jnp.maximum(x, 0.0) + jnp.log1p(jnp.exp(-jnp.abs(x)))


def _colsel(v, idx):
    lane = lax.broadcasted_iota(jnp.int32, v.shape, 1)
    return jnp.sum(jnp.where(lane == idx, v, 0.0), axis=1, keepdims=True)


def _tri(n):
    r = lax.broadcasted_iota(jnp.int32, (n, n), 0)
    c = lax.broadcasted_iota(jnp.int32, (n, n), 1)
    return r >= c, r > c


def _rowsel(v, idx):
    row = lax.broadcasted_iota(jnp.int32, v.shape, 0)
    return jnp.sum(jnp.where(row == idx, v, 0.0), axis=0, keepdims=True)


def _tri_inv(m, size, nil):
    r = lax.broadcasted_iota(jnp.int32, (size, size), 0)
    c = lax.broadcasted_iota(jnp.int32, (size, size), 1)
    x = jnp.where(r == c, 1.0, 0.0) - m
    p = _h2nn(m, m)
    steps = int(np.log2(nil)) - 1
    for s in range(steps):
        x = x + _h2nn(x, p)
        if s < steps - 1:
            p = _h2nn(p, p)
    return x


@jax.custom_vjp
def _inv_given(m, tinv):
    return tinv


_inv_given.defvjp(lambda m, tinv: (tinv, tinv),
                  lambda tinv, g: (-_hnt(_htn(tinv, g), tinv), jnp.zeros_like(tinv)))


def _gdn_chunk(cq, z, ba, s, alog, dtb, nw, tinv_saved=None):
    n, nh = GDN_CS, N_GDN
    rr = n * nh

    def stack(a, off):
        return jnp.concatenate([a[:, off + h * LANE:off + (h + 1) * LANE] for h in range(nh)], axis=0)

    def percol(f):
        return jnp.concatenate([f(h) for h in range(nh)], axis=0)

    def own(a):
        return jnp.concatenate([a[h * n:(h + 1) * n, h * LANE:(h + 1) * LANE] for h in range(nh)], axis=0)

    q, k, v = _silu(stack(cq, 0)), _silu(stack(cq, GW)), _silu(stack(cq, 2 * GW))
    q = q * lax.rsqrt(jnp.sum(q * q, -1, keepdims=True) + EPS) * (128 ** -0.5)
    k = k * lax.rsqrt(jnp.sum(k * k, -1, keepdims=True) + EPS)
    beta = jax.nn.sigmoid(percol(lambda h: _colsel(ba, h)))
    g = percol(lambda h: -jnp.exp(_colsel(alog, h)) * _softplus(_colsel(ba, nh + h) + _colsel(dtb, h)))
    ri = lax.broadcasted_iota(jnp.int32, (rr, rr), 0)
    ci = lax.broadcasted_iota(jnp.int32, (rr, rr), 1)
    same = (ri >> 6) == (ci >> 6)
    incl, strict = same & (ri >= ci), same & (ri > ci)
    gcb = _xnn(incl.astype(F32), jnp.broadcast_to(g, (rr, LANE)))
    gci = jnp.concatenate([gcb, gcb], axis=1)
    dec = jnp.where(incl, jnp.exp(jnp.where(incl, gci - gci.T, 0.0)), 0.0)
    kb = k * beta
    kk = _nt(jnp.concatenate([kb, q], axis=0), k)
    m = jnp.where(strict, kk[:rr] * dec, 0.0)
    attn = jnp.where(incl, kk[rr:] * dec, 0.0)
    tinv = _tri_inv(m, rr, n) if tinv_saved is None else _inv_given(m, tinv_saved)
    egc = jnp.exp(gcb)
    uw = _nn(tinv, jnp.concatenate([v * beta, kb * egc], axis=1))
    u, w = uw[:, :LANE], uw[:, LANE:]
    gl = [jnp.sum(g[h * n:(h + 1) * n], axis=0, keepdims=True) for h in range(nh)]
    kd = k * jnp.exp(percol(lambda h: jnp.broadcast_to(gl[h], (n, 1))) - gcb)
    ws = _nn(jnp.concatenate([w, q * egc], axis=0), s)
    v_new = u - own(ws[:rr])
    o = own(ws[rr:]) + _nn(attn, v_new)
    rowh = lax.broadcasted_iota(jnp.int32, (rr, LANE), 0) >> 6
    vexp = jnp.concatenate([jnp.where(rowh == h, v_new, 0.0) for h in range(nh)], axis=1)
    cd = jnp.concatenate([jnp.broadcast_to(jnp.exp(gl[h]), (1, LANE)) for h in range(nh)], axis=1)
    s_new = s * cd + _tn(kd, vexp)
    y = o * lax.rsqrt(jnp.mean(o * o, -1, keepdims=True) + EPS) * nw * _silu(stack(z, 0))
    y = jnp.concatenate([y[h * n:(h + 1) * n] for h in range(nh)], axis=1)
    return (y, s_new, tinv) if tinv_saved is None else (y, s_new)


def _ssd_chunk(cx, z, dtc, s, convb, alog, dtb, dsk, nw):
    n = SSD_CS
    incl, _ = _tri(n)
    xbc = _silu(cx + convb)
    dt_all = _softplus(dtc + dtb)
    da_all = dt_all * (-jnp.exp(alog))
    cs_all = _xnn(incl.astype(F32), da_all)
    cs_t = cs_all.T
    tot_all = jnp.sum(da_all, axis=0, keepdims=True)
    lane_lo = lax.broadcasted_iota(jnp.int32, (n, LANE), 1) < 64
    ys, s_out = [], []
    for pi in range(4):
        g = pi // 2
        bm = xbc[:, GW + g * LANE:GW + (g + 1) * LANE]
        cm = xbc[:, GW + 256 + g * LANE:GW + 256 + (g + 1) * LANE]
        cbm = _nt(cm, bm)
        xp, s_p = xbc[:, pi * LANE:(pi + 1) * LANE], s[:, pi * LANE:(pi + 1) * LANE]
        a_rows, b_cols, dts, dsks, etot = [], [], [], [], []
        for h in (2 * pi, 2 * pi + 1):
            cs_i = _colsel(cs_all, h)
            lm = jnp.where(incl, jnp.exp(jnp.where(incl, cs_i - _rowsel(cs_t, h), 0.0)), 0.0)
            tot = _colsel(tot_all, h)
            a_rows.append(jnp.concatenate([cbm * lm, cm * jnp.exp(cs_i)], axis=1))
            b_cols.append(bm * jnp.exp(tot - cs_i))
            dts.append(_colsel(dt_all, h))
            dsks.append(_colsel(dsk, h))
            etot.append(jnp.exp(tot))
        xdt = xp * jnp.where(lane_lo, dts[0], dts[1])
        yy = _nn(jnp.concatenate(a_rows, axis=0), jnp.concatenate([xdt, s_p], axis=0))
        ys.append(jnp.where(lane_lo, yy[:n], yy[n:]) + xp * jnp.where(lane_lo[:1], dsks[0], dsks[1]))
        st = _tn(jnp.concatenate(b_cols, axis=1), xdt)
        s_out.append(s_p * jnp.where(lane_lo[:1], etot[0], etot[1]) + jnp.where(lane_lo, st[:n], st[n:]))
    zs = _silu(z)
    out = []
    for g in range(2):
        y = jnp.concatenate(ys[2 * g:2 * g + 2], axis=1) * zs[:, g * 256:(g + 1) * 256]
        out.append(y * lax.rsqrt(jnp.mean(y * y, -1, keepdims=True) + EPS) * nw[:, g * 256:(g + 1) * 256])
    return jnp.concatenate(out, axis=1), jnp.concatenate(s_out, axis=1)


SWA_STACK = 1


def _swa_block(qg, kp, kc, vp, vc, sink, notfirst, *, g):
    w, ns = ATT_W, SWA_STACK
    kb = jnp.concatenate([kp, kc], axis=0)
    vb = jnp.concatenate([vp, vc], axis=0)
    qi = lax.broadcasted_iota(jnp.int32, (ns * w, 2 * w), 0) & (w - 1)
    kj = lax.broadcasted_iota(jnp.int32, (ns * w, 2 * w), 1)
    rel = qi + w - kj
    mask = (rel >= 0) & (rel < w) & ((kj >= w) | (notfirst > 0.5))
    outs = []
    for r0 in range(0, 4, ns):
        q = jnp.concatenate([qg[:, r * LANE:(r + 1) * LANE] for r in range(r0, r0 + ns)], axis=0)
        snk = jnp.concatenate([jnp.broadcast_to(_colsel(sink, 4 * g + r), (w, 1)) for r in range(r0, r0 + ns)], axis=0)
        s = jnp.where(mask, _nt(q, kb) * (64 ** -0.5), -1e30)
        mx = lax.stop_gradient(jnp.maximum(jnp.max(s, axis=-1, keepdims=True), snk))
        p = jnp.where(mask, jnp.exp(s - mx), 0.0)
        p = p / (jnp.sum(p, axis=-1, keepdims=True) + jnp.exp(snk - mx))
        o = _nn(p, vb)
        outs += [o[j * w:(j + 1) * w] for j in range(ns)]
    return jnp.concatenate(outs, axis=1)


def _rope(x, cos, sin):
    return x * cos + pltpu.roll(x, 64, 1) * sin


def _unrope(dy, cos, sin):
    return dy * cos + pltpu.roll(dy * sin, 64, 1)


def rowmap(name, fn, rows, consts, outs, accs=(), tile=256):
    t = rows[0][0].shape[0]
    tile = min(tile, t)
    assert t % tile == 0
    nr, nc, no, na = len(rows), len(consts), len(outs), len(accs)

    def body(*refs):
        rv = [r[...] for r in refs[:nr]]
        cv = [r[...] for r in refs[nr:nr + nc]]
        ov, av = fn(rv, cv)
        for ref, o in zip(refs[nr + nc:nr + nc + no], ov):
            ref[...] = o.astype(ref.dtype)
        if na:
            a_refs = refs[nr + nc + no:]

            @pl.when(pl.program_id(0) == 0)
            def _():
                for ref in a_refs:
                    ref[...] = jnp.zeros_like(ref)

            for ref, a in zip(a_refs, av):
                ref[...] += a

    def cspec(off, width):
        assert off % width == 0
        cb = off // width
        return pl.BlockSpec((tile, width), lambda i: (i, cb))

    in_specs = [cspec(off, wd) for (_, off, wd) in rows]
    in_specs += [pl.BlockSpec(c.shape, lambda i: (0, 0)) for c in consts]
    out_specs = [pl.BlockSpec((tile, wd), lambda i: (i, 0)) for (wd, _) in outs]
    out_specs += [pl.BlockSpec(s, lambda i: (0, 0)) for s in accs]
    out_shape = [jax.ShapeDtypeStruct((t, wd), dt) for (wd, dt) in outs]
    out_shape += [jax.ShapeDtypeStruct(s, F32) for s in accs]
    res = pl.pallas_call(
        body, name=name, grid=(t // tile,), in_specs=in_specs, out_specs=out_specs, out_shape=out_shape,
        compiler_params=_cparams("arbitrary"),
    )(*[r[0] for r in rows], *consts)
    return res[:no], res[no:]


def rowmap_vjp(name, f, rows, consts, douts, n_diff, drow_dtypes, tile=256, add_rows=()):
    nr, nd, nadd = len(rows), len(douts), len(add_rows)

    def g(vals, cvals):
        rv, dov, addv = vals[:nr], vals[nr:nr + nd], vals[nr + nd:]
        fixed = rv[n_diff:]
        _, vjp = jax.vjp(lambda r, c: f(list(r) + list(fixed), c), rv[:n_diff], cvals)
        dr, dc = vjp([d.astype(F32) for d in dov])
        dr = list(dr)
        for a in addv:
            dr[0] = dr[0] + a
        return dr, dc

    outs = [(rows[i][2], drow_dtypes[i]) for i in range(n_diff)]
    accs = [c.shape for c in consts]
    return rowmap(name, g, list(rows) + list(douts) + list(add_rows), consts, outs, accs, tile=tile)


def _carried_copies(kind, in_refs, out_refs, send_sems, recv_sems):
    x, y, c = _me()
    chip = 2 * x + y
    chips = [(1 - x, y), (x, 1 - y), (1 - x, 1 - y)]
    cps = []

    def rc(src, dst, k, to):
        return pltpu.make_async_remote_copy(src_ref=src, dst_ref=dst, send_sem=send_sems.at[k], recv_sem=recv_sems.at[k],
                                            device_id=to, device_id_type=MESH)

    for i, (src, dst) in enumerate(zip(in_refs, out_refs)):
        if kind == "gather":
            rh = src.shape[0] // 2
            half = pl.ds(pl.multiple_of(c * rh, 16), rh)
            cps.append(rc(src, dst.at[chip], 4 * i + 3, (x, y, 1 - c)))
            cps += [rc(src.at[half], dst.at[chip, half], 4 * i + k, (cx, cy, c)) for k, (cx, cy) in enumerate(chips)]
        else:
            cps += [rc(src.at[2 * cx + cy], dst.at[chip], 4 * i + k, (cx, cy, c)) for k, (cx, cy) in enumerate(chips)]
    return cps


def mm_nn(name, a, w, out_dtype, a_fn=None, extra=None, epi=None, tm_cap=1024, wt=False, carry=None):
    nc = len(carry[1]) if carry else 0
    m, k = a.shape
    stacked = w.ndim == 3
    n = (w.shape[-2] if wt else w.shape[-1] * (w.shape[0] if stacked else 1))
    slab = w.shape[-1]
    tm = _pick(m, tm_cap, 16)
    tn = _pick(n if wt else slab, 1024, LANE)
    tk = _pick(slab if wt else k, 2048, LANE)
    nk = k // tk
    per = slab // (tk if wt else tn)
    wdims = ((1,), (1,)) if wt else ((1,), (0,))

    gi, gj = m // tm, n // tn
    n_in = 2 + (extra is not None)

    def body(*refs):
        a_ref, w_ref = refs[0], refs[1]
        e_ref = refs[2] if extra is not None else None
        o_ref, acc = refs[n_in + nc], refs[n_in + 2 * nc + 1]
        kk = pl.program_id(2)
        if nc:
            def copies():
                return _carried_copies(carry[0], refs[n_in:n_in + nc], refs[n_in + nc + 1:n_in + 2 * nc + 1],
                                       refs[-2], refs[-1])

            @pl.when((pl.program_id(0) == 0) & (pl.program_id(1) == 0) & (kk == 0))
            def _():
                for cp in copies():
                    cp.start()

        @pl.when(kk == 0)
        def _():
            acc[...] = jnp.zeros_like(acc)

        av = a_ref[...]
        if a_fn is not None:
            av = a_fn(av)
        acc[...] += lax.dot_general(av.astype(BF16), w_ref[...].astype(BF16), (wdims, ((), ())),
                                    preferred_element_type=F32)

        @pl.when(kk == nk - 1)
        def _():
            r = acc[...]
            if epi is not None:
                r = epi(r, e_ref[...])
            o_ref[...] = r.astype(o_ref.dtype)

        if nc:
            @pl.when((pl.program_id(0) == gi - 1) & (pl.program_id(1) == gj - 1) & (kk == nk - 1))
            def _():
                cps = copies()
                for cp in cps:
                    cp.wait_recv()
                for cp in cps:
                    cp.wait_send()

    if wt:
        w_spec = (pl.BlockSpec((None, tn, tk), lambda i, j, kk: (kk // per, j, kk % per)) if stacked
                  else pl.BlockSpec((tn, tk), lambda i, j, kk: (j, kk)))
    else:
        w_spec = (pl.BlockSpec((None, tk, tn), lambda i, j, kk: (j // per, kk, j % per)) if stacked
                  else pl.BlockSpec((tk, tn), lambda i, j, kk: (kk, j)))
    in_specs = [pl.BlockSpec((tm, tk), lambda i, j, kk: (i, kk)), w_spec]
    ops = [a, w]
    if extra is not None:
        in_specs.append(pl.BlockSpec((tm, tn), lambda i, j, kk: (i, j)))
        ops.append(extra)
    out_spec = pl.BlockSpec((tm, tn), lambda i, j, kk: (i, j))
    out_shape = jax.ShapeDtypeStruct((m, n), out_dtype)
    if not nc:
        return pl.pallas_call(
            body, name=name, grid=(gi, gj, nk), in_specs=in_specs, out_specs=out_spec, out_shape=out_shape,
            scratch_shapes=[pltpu.VMEM((tm, tn), F32)],
            compiler_params=_cparams("parallel", "parallel", "arbitrary"),
        )(*ops)
    landing = [jax.ShapeDtypeStruct((4,) + c.shape if carry[0] == "gather" else c.shape, c.dtype) for c in carry[1]]
    res = pl.pallas_call(
        body, name=name, grid=(gi, gj, nk), in_specs=in_specs + [ANY] * nc, out_specs=[out_spec] + [ANY] * nc,
        out_shape=[out_shape] + landing, scratch_shapes=[pltpu.VMEM((tm, tn), F32)] + _sems(4 * nc),
        compiler_params=_cparams("arbitrary", "arbitrary", "arbitrary"),
    )(*ops, *carry[1])
    return res[0], list(res[1:])


def mm_tn(name, a, b, a_fn=None, col_slabs=1):
    t, m = a.shape
    n = b.shape[1]
    tm, tn, tk = _pick(m, 1024, LANE), _pick(n // col_slabs, 1024, LANE), _pick(t, 2048, 16)
    nk = t // tk
    per = n // col_slabs // tn

    def body(a_ref, b_ref, o_ref, acc):
        kk = pl.program_id(2)

        @pl.when(kk == 0)
        def _():
            acc[...] = jnp.zeros_like(acc)

        av = a_ref[...]
        if a_fn is not None:
            av = a_fn(av)
        acc[...] += lax.dot_general(av.astype(BF16), b_ref[...].astype(BF16), (((0,), (0,)), ((), ())),
                                    preferred_element_type=F32)

        @pl.when(kk == nk - 1)
        def _():
            o_ref[...] = acc[...]

    return pl.pallas_call(
        body, name=name, grid=(m // tm, n // tn, nk),
        in_specs=[pl.BlockSpec((tk, tm), lambda i, j, kk: (kk, i)), pl.BlockSpec((tk, tn), lambda i, j, kk: (kk, j))],
        out_specs=(pl.BlockSpec((tm, tn), lambda i, j, kk: (i, j)) if col_slabs == 1
                   else pl.BlockSpec((None, tm, tn), lambda i, j, kk: (j // per, i, j % per))),
        out_shape=jax.ShapeDtypeStruct((m, n) if col_slabs == 1 else (col_slabs, m, n // col_slabs), F32),
        scratch_shapes=[pltpu.VMEM((tm, tn), F32)],
        compiler_params=_cparams("parallel", "parallel", "arbitrary"),
    )(a, b)


CONV_TILE = 512
HALO = 8


def conv_fwd(name, xs, w):
    t = xs[0][0].shape[0]
    c = xs[0][2]
    kw = w.shape[0]
    tt = min(CONV_TILE, t)
    nx = len(xs)

    def body(*refs):
        cur_refs, prev_refs, w_ref, y_ref = refs[:nx], refs[nx:2 * nx], refs[2 * nx], refs[2 * nx + 1]
        i = pl.program_id(0)
        cur, prev = cur_refs[0][...], prev_refs[0][...]
        for r, p in zip(cur_refs[1:], prev_refs[1:]):
            cur, prev = cur * r[...], prev * p[...]
        prev = jnp.where(i > 0, prev, 0.0)
        xx = jnp.concatenate([prev, cur], axis=0)
        acc = jnp.zeros((tt, c), F32)
        for k in range(kw):
            sh = kw - 1 - k
            rolled = pltpu.roll(xx, sh, 0) if sh else xx
            acc = acc + rolled[HALO:] * w_ref[k:k + 1, :]
        y_ref[...] = acc

    def cur_spec(off):
        cb = off // c
        return pl.BlockSpec((tt, c), lambda i: (i, cb))

    def prev_spec(off):
        cb = off // c
        return pl.BlockSpec((HALO, c), lambda i: (jnp.maximum(i * (tt // HALO) - 1, 0), cb))

    for (_, off, wd) in xs:
        assert wd == c and off % c == 0
    return pl.pallas_call(
        body, name=name, grid=(t // tt,),
        in_specs=[cur_spec(off) for (_, off, _) in xs] + [prev_spec(off) for (_, off, _) in xs]
        + [pl.BlockSpec(w.shape, lambda i: (0, 0))],
        out_specs=pl.BlockSpec((tt, c), lambda i: (i, 0)),
        out_shape=jax.ShapeDtypeStruct((t, c), F32),
        compiler_params=_cparams("arbitrary"),
    )(*[x[0] for x in xs], *[x[0] for x in xs], w)


def conv_bwd(name, dy, xs, w, dx_dtype):
    t = xs[0][0].shape[0]
    c = xs[0][2]
    kw = w.shape[0]
    tt = min(CONV_TILE, t)
    nt = t // tt
    nx = len(xs)
    n = tt + HALO

    def body(*refs):
        dy_ref = refs[0]
        cur_refs, prev_refs = refs[1:1 + nx], refs[1 + nx:1 + 2 * nx]
        w_ref = refs[1 + 2 * nx]
        dx_refs = refs[2 + 2 * nx:2 + 3 * nx]
        dw_ref, carry = refs[2 + 3 * nx], refs[3 + 3 * nx]
        i = pl.program_id(0)

        @pl.when(i == 0)
        def _():
            carry[...] = jnp.zeros_like(carry)
            dw_ref[...] = jnp.zeros_like(dw_ref)

        curs = [r[...] for r in cur_refs]
        cur, prev = curs[0], prev_refs[0][...]
        for cv, p in zip(curs[1:], prev_refs[1:]):
            cur, prev = cur * cv, prev * p[...]
        prev = jnp.where(i < nt - 1, prev, 0.0)
        xx = jnp.concatenate([prev, cur], axis=0)
        dyv = dy_ref[...]
        dpad = jnp.concatenate([jnp.zeros((HALO, c), F32), dyv], axis=0)
        dxx = jnp.zeros((n, c), F32)
        dws = []
        for k in range(kw):
            sh = kw - 1 - k
            dxx = dxx + (pltpu.roll(dpad, n - sh, 0) if sh else dpad) * w_ref[k:k + 1, :]
            rolled = pltpu.roll(xx, sh, 0) if sh else xx
            dws.append(jnp.sum(dyv * rolled[HALO:], axis=0, keepdims=True))
        dw_ref[...] += jnp.concatenate(dws, axis=0)
        dcur = jnp.concatenate([dxx[HALO:tt], dxx[tt:] + carry[...]], axis=0)
        carry[...] = dxx[:HALO]
        for j, ref in enumerate(dx_refs):
            d = dcur
            for jj, cv in enumerate(curs):
                if jj != j:
                    d = d * cv
            ref[...] = d.astype(ref.dtype)

    def cur_spec(off):
        cb = off // c
        return pl.BlockSpec((tt, c), lambda i: (nt - 1 - i, cb))

    def prev_spec(off):
        cb = off // c
        return pl.BlockSpec((HALO, c), lambda i: (jnp.maximum((nt - 1 - i) * (tt // HALO) - 1, 0), cb))

    res = pl.pallas_call(
        body, name=name, grid=(nt,),
        in_specs=[pl.BlockSpec((tt, c), lambda i: (nt - 1 - i, 0))] + [cur_spec(off) for (_, off, _) in xs]
        + [prev_spec(off) for (_, off, _) in xs] + [pl.BlockSpec(w.shape, lambda i: (0, 0))],
        out_specs=[pl.BlockSpec((tt, c), lambda i: (nt - 1 - i, 0)) for _ in xs] + [pl.BlockSpec(w.shape, lambda i: (0, 0))],
        out_shape=[jax.ShapeDtypeStruct((t, c), dx_dtype) for _ in xs] + [jax.ShapeDtypeStruct(w.shape, F32)],
        scratch_shapes=[pltpu.VMEM((HALO, c), F32)],
        compiler_params=_cparams("arbitrary"),
    )(dy, *[x[0] for x in xs], *[x[0] for x in xs], w)
    return res[:nx], res[nx]


def _const_spec(a):
    return pl.BlockSpec(a.shape, lambda i: (0,) * a.ndim)


GDN_STEP = 4


def _gdn_rows(ref, c):
    return ref[c * GDN_CS:(c + 1) * GDN_CS, :]


def gdn_fwd(name, cq, proj, alog, dtb, nw):
    t = cq.shape[0]
    nc = t // GDN_CS
    ns = min(GDN_STEP, nc)
    rows = ns * GDN_CS

    def body(cq_ref, z_ref, ba_ref, alog_ref, dtb_ref, nw_ref, y_ref, ssave_ref, tsave_ref, s_scr):
        @pl.when(pl.program_id(0) == 0)
        def _():
            s_scr[...] = jnp.zeros_like(s_scr)

        s = s_scr[...]
        ys, saved, tinvs = [], [], []
        for c in range(ns):
            saved.append(s)
            y, s, tinv = _gdn_chunk(_gdn_rows(cq_ref, c), _gdn_rows(z_ref, c), _gdn_rows(ba_ref, c), s, alog_ref[...],
                                    dtb_ref[...], nw_ref[...])
            ys.append(y)
            tinvs.append(tinv)
        y_ref[...] = jnp.concatenate(ys, axis=0)
        ssave_ref[...] = jnp.stack(saved, axis=0)
        tsave_ref[...] = jnp.stack(tinvs, axis=0)
        s_scr[...] = s

    return pl.pallas_call(
        body, name=name, grid=(nc // ns,),
        in_specs=[pl.BlockSpec((rows, 3 * GW), lambda i: (i, 0)),
                  pl.BlockSpec((rows, GW), lambda i: (i, O_GZ // GW)),
                  pl.BlockSpec((rows, LANE), lambda i: (i, O_GBA // LANE)),
                  _const_spec(alog), _const_spec(dtb), _const_spec(nw)],
        out_specs=[pl.BlockSpec((rows, GW), lambda i: (i, 0)),
                   pl.BlockSpec((ns, LANE, N_GDN * LANE), lambda i: (i, 0, 0)),
                   pl.BlockSpec((ns, N_GDN * GDN_CS, N_GDN * GDN_CS), lambda i: (i, 0, 0))],
        out_shape=[jax.ShapeDtypeStruct((t, GW), F32), jax.ShapeDtypeStruct((nc, LANE, N_GDN * LANE), F32),
                   jax.ShapeDtypeStruct((nc, N_GDN * GDN_CS, N_GDN * GDN_CS), F32)],
        scratch_shapes=[pltpu.VMEM((LANE, N_GDN * LANE), F32)],
        compiler_params=_cparams("arbitrary"),
    )(cq, proj, proj, alog, dtb, nw)


def gdn_bwd(name, dy, dy_off, cq, proj, ssave, tsave, alog, dtb, nw):
    t = cq.shape[0]
    nc = t // GDN_CS
    ns = min(GDN_STEP, nc)
    rows = ns * GDN_CS

    def body(dy_ref, cq_ref, z_ref, ba_ref, ssave_ref, tsave_ref, alog_ref, dtb_ref, nw_ref,
             dcq_ref, dz_ref, dba_ref, dalog_ref, ddtb_ref, dnw_ref, ds_scr):
        @pl.when(pl.program_id(0) == 0)
        def _():
            ds_scr[...] = jnp.zeros_like(ds_scr)
            dalog_ref[...] = jnp.zeros_like(dalog_ref)
            ddtb_ref[...] = jnp.zeros_like(ddtb_ref)
            dnw_ref[...] = jnp.zeros_like(dnw_ref)

        ds = ds_scr[...]
        dcq, dzs, dbas = [None] * ns, [None] * ns, [None] * ns
        dal, ddt, dnw = (jnp.zeros(r.shape, F32) for r in (dalog_ref, ddtb_ref, dnw_ref))
        for c in reversed(range(ns)):
            _, vjp = jax.vjp(functools.partial(_gdn_chunk, tinv_saved=tsave_ref[c]), _gdn_rows(cq_ref, c),
                             _gdn_rows(z_ref, c), _gdn_rows(ba_ref, c), ssave_ref[c], alog_ref[...], dtb_ref[...],
                             nw_ref[...])
            dcq[c], dzs[c], dbas[c], ds, dal_c, ddt_c, dnw_c = vjp((_gdn_rows(dy_ref, c), ds))
            dal, ddt, dnw = dal + dal_c, ddt + ddt_c, dnw + dnw_c
        dcq_ref[...] = jnp.concatenate(dcq, axis=0)
        dz_ref[...] = jnp.concatenate(dzs, axis=0).astype(dz_ref.dtype)
        dba_ref[...] = jnp.concatenate(dbas, axis=0).astype(dba_ref.dtype)
        ds_scr[...] = ds
        dalog_ref[...] += dal
        ddtb_ref[...] += ddt
        dnw_ref[...] += dnw

    rev = lambda i: nc // ns - 1 - i
    return pl.pallas_call(
        body, name=name, grid=(nc // ns,),
        in_specs=[pl.BlockSpec((rows, GW), lambda i: (rev(i), dy_off // GW)),
                  pl.BlockSpec((rows, 3 * GW), lambda i: (rev(i), 0)),
                  pl.BlockSpec((rows, GW), lambda i: (rev(i), O_GZ // GW)),
                  pl.BlockSpec((rows, LANE), lambda i: (rev(i), O_GBA // LANE)),
                  pl.BlockSpec((ns, LANE, N_GDN * LANE), lambda i: (rev(i), 0, 0)),
                  pl.BlockSpec((ns, N_GDN * GDN_CS, N_GDN * GDN_CS), lambda i: (rev(i), 0, 0)),
                  _const_spec(alog), _const_spec(dtb), _const_spec(nw)],
        out_specs=[pl.BlockSpec((rows, 3 * GW), lambda i: (rev(i), 0)),
                   pl.BlockSpec((rows, GW), lambda i: (rev(i), 0)),
                   pl.BlockSpec((rows, LANE), lambda i: (rev(i), 0)),
                   _const_spec(alog), _const_spec(dtb), _const_spec(nw)],
        out_shape=[jax.ShapeDtypeStruct((t, 3 * GW), F32), jax.ShapeDtypeStruct((t, GW), BF16),
                   jax.ShapeDtypeStruct((t, LANE), BF16), jax.ShapeDtypeStruct(alog.shape, F32),
                   jax.ShapeDtypeStruct(dtb.shape, F32), jax.ShapeDtypeStruct(nw.shape, F32)],
        scratch_shapes=[pltpu.VMEM((LANE, N_GDN * LANE), F32)],
        compiler_params=_cparams("arbitrary"),
    )(dy, cq, proj, proj, ssave, tsave, alog, dtb, nw)


SSD_STEP = 2


def _ssd_rows(ref, c):
    return ref[c * SSD_CS:(c + 1) * SSD_CS, :]


def ssd_fwd(name, cx, proj, convb, alog, dtb, dsk, nw):
    t = cx.shape[0]
    nc = t // SSD_CS
    ns = min(SSD_STEP, nc)
    rows = ns * SSD_CS

    def body(cx_ref, z_ref, dt_ref, cb_ref, alog_ref, dtb_ref, dsk_ref, nw_ref, y_ref, ssave_ref, s_scr):
        @pl.when(pl.program_id(0) == 0)
        def _():
            s_scr[...] = jnp.zeros_like(s_scr)

        s = s_scr[...]
        ys, saved = [], []
        for c in range(ns):
            saved.append(s)
            y, s = _ssd_chunk(_ssd_rows(cx_ref, c), _ssd_rows(z_ref, c), _ssd_rows(dt_ref, c), s, cb_ref[...],
                              alog_ref[...], dtb_ref[...], dsk_ref[...], nw_ref[...])
            ys.append(y)
        y_ref[...] = jnp.concatenate(ys, axis=0)
        ssave_ref[...] = jnp.stack(saved, axis=0)
        s_scr[...] = s

    return pl.pallas_call(
        body, name=name, grid=(nc // ns,),
        in_specs=[pl.BlockSpec((rows, 1024), lambda i: (i, 0)),
                  pl.BlockSpec((rows, GW), lambda i: (i, O_SZ // GW)),
                  pl.BlockSpec((rows, LANE), lambda i: (i, O_SDT // LANE)),
                  _const_spec(convb), _const_spec(alog), _const_spec(dtb), _const_spec(dsk), _const_spec(nw)],
        out_specs=[pl.BlockSpec((rows, GW), lambda i: (i, 0)),
                   pl.BlockSpec((ns, LANE, GW), lambda i: (i, 0, 0))],
        out_shape=[jax.ShapeDtypeStruct((t, GW), F32), jax.ShapeDtypeStruct((nc, LANE, GW), F32)],
        scratch_shapes=[pltpu.VMEM((LANE, GW), F32)],
        compiler_params=_cparams("arbitrary"),
    )(cx, proj, proj, convb, alog, dtb, dsk, nw)


def ssd_bwd(name, dy, dy_off, cx, proj, ssave, convb, alog, dtb, dsk, nw):
    t = cx.shape[0]
    nc = t // SSD_CS
    ns = min(SSD_STEP, nc)
    rows = ns * SSD_CS

    def body(dy_ref, cx_ref, z_ref, dt_ref, ssave_ref, cb_ref, alog_ref, dtb_ref, dsk_ref, nw_ref,
             dcx_ref, dz_ref, ddt_ref, dcb_ref, dalog_ref, ddtb_ref, ddsk_ref, dnw_ref, ds_scr):
        @pl.when(pl.program_id(0) == 0)
        def _():
            ds_scr[...] = jnp.zeros_like(ds_scr)
            for r in (dcb_ref, dalog_ref, ddtb_ref, ddsk_ref, dnw_ref):
                r[...] = jnp.zeros_like(r)

        ds = ds_scr[...]
        dcx, dz, ddt = [None] * ns, [None] * ns, [None] * ns
        acc = [jnp.zeros(r.shape, F32) for r in (dcb_ref, dalog_ref, ddtb_ref, ddsk_ref, dnw_ref)]
        for c in reversed(range(ns)):
            _, vjp = jax.vjp(_ssd_chunk, _ssd_rows(cx_ref, c), _ssd_rows(z_ref, c), _ssd_rows(dt_ref, c), ssave_ref[c],
                             cb_ref[...], alog_ref[...], dtb_ref[...], dsk_ref[...], nw_ref[...])
            dcx[c], dz[c], ddt[c], ds, *dconst = vjp((_ssd_rows(dy_ref, c), ds))
            acc = [a + g for a, g in zip(acc, dconst)]
        dcx_ref[...] = jnp.concatenate(dcx, axis=0)
        dz_ref[...] = jnp.concatenate(dz, axis=0).astype(dz_ref.dtype)
        ddt_ref[...] = jnp.concatenate(ddt, axis=0).astype(ddt_ref.dtype)
        ds_scr[...] = ds
        for r, a in zip((dcb_ref, dalog_ref, ddtb_ref, ddsk_ref, dnw_ref), acc):
            r[...] += a

    rev = lambda i: nc // ns - 1 - i
    return pl.pallas_call(
        body, name=name, grid=(nc // ns,),
        in_specs=[pl.BlockSpec((rows, GW), lambda i: (rev(i), dy_off // GW)),
                  pl.BlockSpec((rows, 1024), lambda i: (rev(i), 0)),
                  pl.BlockSpec((rows, GW), lambda i: (rev(i), O_SZ // GW)),
                  pl.BlockSpec((rows, LANE), lambda i: (rev(i), O_SDT // LANE)),
                  pl.BlockSpec((ns, LANE, GW), lambda i: (rev(i), 0, 0)),
                  _const_spec(convb), _const_spec(alog), _const_spec(dtb), _const_spec(dsk), _const_spec(nw)],
        out_specs=[pl.BlockSpec((rows, 1024), lambda i: (rev(i), 0)),
                   pl.BlockSpec((rows, GW), lambda i: (rev(i), 0)),
                   pl.BlockSpec((rows, LANE), lambda i: (rev(i), 0)),
                   _const_spec(convb), _const_spec(alog), _const_spec(dtb), _const_spec(dsk), _const_spec(nw)],
        out_shape=[jax.ShapeDtypeStruct((t, 1024), F32), jax.ShapeDtypeStruct((t, GW), BF16),
                   jax.ShapeDtypeStruct((t, LANE), BF16), jax.ShapeDtypeStruct(convb.shape, F32),
                   jax.ShapeDtypeStruct(alog.shape, F32), jax.ShapeDtypeStruct(dtb.shape, F32),
                   jax.ShapeDtypeStruct(dsk.shape, F32), jax.ShapeDtypeStruct(nw.shape, F32)],
        scratch_shapes=[pltpu.VMEM((LANE, GW), F32)],
        compiler_params=_cparams("arbitrary"),
    )(dy, cx, proj, proj, ssave, convb, alog, dtb, dsk, nw)


def _swa_specs(nb, rev):
    w = ATT_W
    cur = lambda i: (nb - 1 - i) if rev else i
    prv = lambda i: jnp.maximum(cur(i) - 1, 0)
    return [pl.BlockSpec((w, 1024), lambda i: (cur(i), O_AQ // 1024)),
            pl.BlockSpec((w, 256), lambda i: (cur(i), O_AK // 256)),
            pl.BlockSpec((w, 256), lambda i: (prv(i), O_AK // 256)),
            pl.BlockSpec((w, 256), lambda i: (cur(i), O_AV // 256)),
            pl.BlockSpec((w, 256), lambda i: (prv(i), O_AV // 256)),
            pl.BlockSpec((w, LANE), lambda i: (cur(i), 0)), pl.BlockSpec((w, LANE), lambda i: (prv(i), 0)),
            pl.BlockSpec((w, LANE), lambda i: (cur(i), 0)), pl.BlockSpec((w, LANE), lambda i: (prv(i), 0))]


def swa_fwd(name, proj, cos, sin, sinks):
    t = proj.shape[0]
    nb = t // ATT_W

    def body(q_ref, kc_ref, kp_ref, vc_ref, vp_ref, cc_ref, cp_ref, sc_ref, sp_ref, snk_ref, y_ref):
        notfirst = jnp.where(pl.program_id(0) > 0, 1.0, 0.0).astype(F32)
        cc, cp, sc, sp = cc_ref[...], cp_ref[...], sc_ref[...], sp_ref[...]
        outs = []
        for g in range(2):
            sl = slice(g * LANE, (g + 1) * LANE)
            qg = jnp.concatenate([_rope(q_ref[:, (4 * g + r) * LANE:(4 * g + r + 1) * LANE], cc, sc)
                                  for r in range(4)], axis=1)
            outs.append(_swa_block(qg, _rope(kp_ref[:, sl], cp, sp), _rope(kc_ref[:, sl], cc, sc), vp_ref[:, sl],
                                   vc_ref[:, sl], snk_ref[...], notfirst, g=g))
        y_ref[...] = jnp.concatenate(outs, axis=1)

    return pl.pallas_call(
        body, name=name, grid=(nb,),
        in_specs=_swa_specs(nb, False) + [_const_spec(sinks)],
        out_specs=pl.BlockSpec((ATT_W, 1024), lambda i: (i, 0)),
        out_shape=jax.ShapeDtypeStruct((t, 1024), F32),
        compiler_params=_cparams("arbitrary"),
    )(proj, proj, proj, proj, proj, cos, cos, sin, sin, sinks)


def swa_bwd(name, dy, dy_off, proj, cos, sin, sinks):
    t = proj.shape[0]
    nb = t // ATT_W

    def body(dy_ref, q_ref, kc_ref, kp_ref, vc_ref, vp_ref, cc_ref, cp_ref, sc_ref, sp_ref, snk_ref,
             dq_ref, dk_ref, dv_ref, dsnk_ref, ck_scr, cv_scr):
        i = pl.program_id(0)

        @pl.when(i == 0)
        def _():
            ck_scr[...] = jnp.zeros_like(ck_scr)
            cv_scr[...] = jnp.zeros_like(cv_scr)
            dsnk_ref[...] = jnp.zeros_like(dsnk_ref)

        notfirst = jnp.where(i < nb - 1, 1.0, 0.0).astype(F32)
        cc, cp, sc, sp = cc_ref[...], cp_ref[...], sc_ref[...], sp_ref[...]
        ck_in, cv_in = ck_scr[...], cv_scr[...]
        dq, dk, dv, ck, cv, dsnk = [], [], [], [], [], jnp.zeros(dsnk_ref.shape, F32)
        for g in range(2):
            sl = slice(g * LANE, (g + 1) * LANE)
            qg = jnp.concatenate([_rope(q_ref[:, (4 * g + r) * LANE:(4 * g + r + 1) * LANE], cc, sc)
                                  for r in range(4)], axis=1)
            _, vjp = jax.vjp(functools.partial(_swa_block, g=g), qg, _rope(kp_ref[:, sl], cp, sp),
                             _rope(kc_ref[:, sl], cc, sc), vp_ref[:, sl], vc_ref[:, sl], snk_ref[...], notfirst)
            dqg, dkp, dkc, dvp, dvc, dsnk_g, _ = vjp(dy_ref[:, g * GW:(g + 1) * GW])
            dq += [_unrope(dqg[:, r * LANE:(r + 1) * LANE], cc, sc) for r in range(4)]
            dk.append(_unrope(dkc, cc, sc) + ck_in[:, sl])
            dv.append(dvc + cv_in[:, sl])
            ck.append(_unrope(dkp, cp, sp))
            cv.append(dvp)
            dsnk = dsnk + dsnk_g
        dq_ref[...] = jnp.concatenate(dq, axis=1).astype(dq_ref.dtype)
        dk_ref[...] = jnp.concatenate(dk, axis=1).astype(dk_ref.dtype)
        dv_ref[...] = jnp.concatenate(dv, axis=1).astype(dv_ref.dtype)
        ck_scr[...] = jnp.concatenate(ck, axis=1)
        cv_scr[...] = jnp.concatenate(cv, axis=1)
        dsnk_ref[...] += dsnk

    rev = lambda i: nb - 1 - i
    return pl.pallas_call(
        body, name=name, grid=(nb,),
        in_specs=[pl.BlockSpec((ATT_W, 1024), lambda i: (rev(i), dy_off // 1024))] + _swa_specs(nb, True)
        + [_const_spec(sinks)],
        out_specs=[pl.BlockSpec((ATT_W, 1024), lambda i: (rev(i), 0)),
                   pl.BlockSpec((ATT_W, 256), lambda i: (rev(i), 0)),
                   pl.BlockSpec((ATT_W, 256), lambda i: (rev(i), 0)), _const_spec(sinks)],
        out_shape=[jax.ShapeDtypeStruct((t, 1024), BF16), jax.ShapeDtypeStruct((t, 256), BF16),
                   jax.ShapeDtypeStruct((t, 256), BF16), jax.ShapeDtypeStruct(sinks.shape, F32)],
        scratch_shapes=[pltpu.VMEM((ATT_W, 256), F32), pltpu.VMEM((ATT_W, 256), F32)],
        compiler_params=_cparams("arbitrary"),
    )(dy, proj, proj, proj, proj, proj, cos, cos, sin, sin, sinks)


def _me():
    return lax.axis_index("x"), lax.axis_index("y"), lax.axis_index("c")


ANY = pl.BlockSpec(memory_space=pl.ANY)


def allgather8(name, buf):
    r = buf.shape[0]

    def body(x_ref, out_ref, send_sems, recv_sems, local_sem):
        x, y, c = _me()
        me = 4 * x + 2 * y + c
        mine = pltpu.make_async_copy(x_ref, out_ref.at[me], local_sem)
        mine.start()
        copies = []
        for k in range(1, 8):
            bx, by, bc = (k >> 2) & 1, (k >> 1) & 1, k & 1
            to = ((1 - x) if bx else x, (1 - y) if by else y, (1 - c) if bc else c)
            cp = pltpu.make_async_remote_copy(src_ref=x_ref, dst_ref=out_ref.at[me], send_sem=send_sems.at[k - 1],
                                              recv_sem=recv_sems.at[k - 1], device_id=to, device_id_type=MESH)
            cp.start()
            copies.append(cp)
        for cp in copies:
            cp.wait_recv()
        for cp in copies:
            cp.wait_send()
        mine.wait()

    return pl.pallas_call(
        body, name=name, out_shape=jax.ShapeDtypeStruct((8, r, LANE), buf.dtype),
        in_specs=[pl.BlockSpec(memory_space=pltpu.VMEM)], out_specs=pl.BlockSpec(memory_space=pltpu.VMEM),
        scratch_shapes=[pltpu.SemaphoreType.DMA((7,)), pltpu.SemaphoreType.DMA((7,)), pltpu.SemaphoreType.DMA],
        compiler_params=pltpu.CompilerParams(vmem_limit_bytes=VMEM_LIMIT),
    )(buf)


def _sems(n):
    return [pltpu.SemaphoreType.DMA((n,)), pltpu.SemaphoreType.DMA((n,))]


def allgather_chips(name, shards):
    n = len(shards)

    def body(*refs):
        x_refs, out_refs, send_sems, recv_sems = refs[:n], refs[n:2 * n], refs[2 * n], refs[2 * n + 1]
        x, y, c = _me()
        chip = 2 * x + y
        chips = [(1 - x, y), (x, 1 - y), (1 - x, 1 - y)]

        def rc(src, dst, k, to):
            return pltpu.make_async_remote_copy(src_ref=src, dst_ref=dst, send_sem=send_sems.at[k],
                                                recv_sem=recv_sems.at[k], device_id=to, device_id_type=MESH)

        halves = [pl.ds(pl.multiple_of(c * (s.shape[0] // 2), 16), s.shape[0] // 2) for s in shards]
        own, first = [], {}
        for i in range(n):
            cp = rc(x_refs[i], out_refs[i].at[chip], 7 * i + 6, (x, y, 1 - c))
            cp.start()
            own.append(cp)
            for k, (cx, cy) in enumerate(chips):
                cp = rc(x_refs[i].at[halves[i]], out_refs[i].at[chip, halves[i]], 7 * i + k, (cx, cy, c))
                cp.start()
                first[i, k] = cp
        passed = []
        for i in range(n):
            for k, (cx, cy) in enumerate(chips):
                first[i, k].wait_recv()
                src = out_refs[i].at[2 * cx + cy, halves[i]]
                cp = rc(src, src, 7 * i + 3 + k, (x, y, 1 - c))
                cp.start()
                passed.append(cp)
        for cp in passed + own:
            cp.wait_recv()
        for cp in list(first.values()) + passed + own:
            cp.wait_send()

    for s in shards:
        assert s.shape[0] % 32 == 0
    return pl.pallas_call(
        body, name=name, out_shape=[jax.ShapeDtypeStruct((4,) + s.shape, s.dtype) for s in shards],
        in_specs=[ANY] * n, out_specs=[ANY] * n, scratch_shapes=_sems(7 * n),
    )(*shards)


def ag_forward(name, bufs):
    n = len(bufs)

    def body(*refs):
        out_refs, send_sems, recv_sems = refs[n:2 * n], refs[2 * n], refs[2 * n + 1]
        x, y, c = _me()
        cps = []
        for i in range(n):
            rh = bufs[i].shape[1] // 2
            half = pl.ds(pl.multiple_of(c * rh, 16), rh)
            for k, (cx, cy) in enumerate([(1 - x, y), (x, 1 - y), (1 - x, 1 - y)]):
                src = out_refs[i].at[2 * cx + cy, half]
                cps.append(pltpu.make_async_remote_copy(src_ref=src, dst_ref=src, send_sem=send_sems.at[3 * i + k],
                                                        recv_sem=recv_sems.at[3 * i + k], device_id=(x, y, 1 - c),
                                                        device_id_type=MESH))
        for cp in cps:
            cp.start()
        for cp in cps:
            cp.wait_recv()
        for cp in cps:
            cp.wait_send()

    return pl.pallas_call(
        body, name=name, out_shape=[jax.ShapeDtypeStruct(b.shape, b.dtype) for b in bufs],
        in_specs=[ANY] * n, out_specs=[ANY] * n, scratch_shapes=_sems(3 * n),
        input_output_aliases={i: i for i in range(n)},
    )(*bufs)


def sibling_swap_halves(name, gs):
    n = len(gs)

    def body(*refs):
        g_refs, out_refs, send_sems, recv_sems = refs[:n], refs[n:2 * n], refs[2 * n], refs[2 * n + 1]
        x, y, c = _me()
        cps = [pltpu.make_async_remote_copy(src_ref=g_refs[i].at[:, 1 - c], dst_ref=out_refs[i],
                                            send_sem=send_sems.at[i], recv_sem=recv_sems.at[i],
                                            device_id=(x, y, 1 - c), device_id_type=MESH) for i in range(n)]
        for cp in cps:
            cp.start()
        for cp in cps:
            cp.wait_recv()
        for cp in cps:
            cp.wait_send()

    return pl.pallas_call(
        body, name=name, out_shape=[jax.ShapeDtypeStruct((4,) + g.shape[2:], g.dtype) for g in gs],
        in_specs=[ANY] * n, out_specs=[ANY] * n, scratch_shapes=_sems(n),
    )(*gs)


def chips_exchange(name, ps):
    n = len(ps)

    def body(*refs):
        p_refs, out_refs, send_sems, recv_sems = refs[:n], refs[n:2 * n], refs[2 * n], refs[2 * n + 1]
        x, y, c = _me()
        chip = 2 * x + y
        cps = []
        for i in range(n):
            for k, (cx, cy) in enumerate([(1 - x, y), (x, 1 - y), (1 - x, 1 - y)]):
                cps.append(pltpu.make_async_remote_copy(
                    src_ref=p_refs[i].at[2 * cx + cy], dst_ref=out_refs[i].at[chip], send_sem=send_sems.at[3 * i + k],
                    recv_sem=recv_sems.at[3 * i + k], device_id=(cx, cy, c), device_id_type=MESH))
        for cp in cps:
            cp.start()
        for cp in cps:
            cp.wait_recv()
        for cp in cps:
            cp.wait_send()

    return pl.pallas_call(
        body, name=name, out_shape=[jax.ShapeDtypeStruct(p.shape, p.dtype) for p in ps],
        in_specs=[ANY] * n, out_specs=[ANY] * n, scratch_shapes=_sems(3 * n),
    )(*ps)


def sibling_fill(name, ss):
    n = len(ss)

    def body(*refs):
        out_refs, send_sems, recv_sems = refs[n:2 * n], refs[2 * n], refs[2 * n + 1]
        x, y, c = _me()
        cps = []
        for i in range(n):
            rh = ss[i].shape[1] // 2
            mine = out_refs[i].at[:, pl.ds(pl.multiple_of(c * rh, 8), rh)]
            cps.append(pltpu.make_async_remote_copy(src_ref=mine, dst_ref=mine, send_sem=send_sems.at[i],
                                                    recv_sem=recv_sems.at[i], device_id=(x, y, 1 - c),
                                                    device_id_type=MESH))
        for cp in cps:
            cp.start()
        for cp in cps:
            cp.wait_recv()
        for cp in cps:
            cp.wait_send()

    return pl.pallas_call(
        body, name=name, out_shape=[jax.ShapeDtypeStruct(s.shape, s.dtype) for s in ss],
        in_specs=[ANY] * n, out_specs=[ANY] * n, scratch_shapes=_sems(n),
        input_output_aliases={i: i for i in range(n)},
    )(*ss)


def _rs_tile(rh, w):
    return _pick(rh, max(8, (1 << 19) // w // 8 * 8), 8)


def add_half(name, g, recv, idx):
    _, _, rh, w = g.shape
    tile = _rs_tile(rh, w)

    def body(s_ref, g_ref, r_ref, o_ref):
        o_ref[...] = (g_ref[...] + r_ref[...]).astype(o_ref.dtype)

    return pl.pallas_call(
        body, name=name,
        grid_spec=pltpu.PrefetchScalarGridSpec(
            num_scalar_prefetch=1, grid=(4, rh // tile),
            in_specs=[pl.BlockSpec((None, None, tile, w), lambda s, i, ix: (s, ix[1], i, 0)),
                      pl.BlockSpec((None, tile, w), lambda s, i, ix: (s, i, 0))],
            out_specs=pl.BlockSpec((None, tile, w), lambda s, i, ix: (s, i, 0))),
        out_shape=jax.ShapeDtypeStruct((4, rh, w), BF16),
        compiler_params=_cparams("parallel", "parallel"),
    )(idx, g, recv)


def rs_sum(name, g, recv1, recv2, idx, layer, acc=None):
    _, _, rh, w = g.shape
    tile = _rs_tile(rh, w)
    nt = rh // tile

    def body(s_ref, g_ref, r1_ref, a_ref, b_ref, c_ref, *rest):
        rest[-1][...] = (g_ref[...] + r1_ref[...]) + (a_ref[...].astype(F32) + b_ref[...].astype(F32)
                                                       + c_ref[...].astype(F32))

    def other(k):
        return pl.BlockSpec((None, tile, w), lambda i, ix: (jnp.bitwise_xor(ix[0], k), i, 0))

    ops = [idx, g, recv1, recv2, recv2, recv2] + ([acc] if acc is not None else [])
    return pl.pallas_call(
        body, name=name,
        grid_spec=pltpu.PrefetchScalarGridSpec(
            num_scalar_prefetch=1, grid=(rh // tile,),
            in_specs=[pl.BlockSpec((None, None, tile, w), lambda i, ix: (ix[0], ix[1], i, 0)),
                      pl.BlockSpec((None, tile, w), lambda i, ix: (ix[0], i, 0)), other(1), other(2), other(3)]
            + ([ANY] if acc is not None else []),
            out_specs=pl.BlockSpec((None, tile, w), lambda i, ix: (layer, ix[1] * nt + i, 0))),
        out_shape=jax.ShapeDtypeStruct((DEPTH, 2 * rh, w), F32),
        input_output_aliases={6: 0} if acc is not None else {},
        compiler_params=_cparams("parallel"),
    )(*ops)


def sum_slabs(name, a):
    n, r, _ = a.shape
    tile = _pick(r, 2048, 8)

    def body(a_ref, o_ref):
        acc = a_ref[0]
        for s in range(1, n):
            acc = acc + a_ref[s]
        o_ref[...] = acc

    return pl.pallas_call(
        body, name=name, grid=(r // tile,),
        in_specs=[pl.BlockSpec((n, tile, LANE), lambda i: (0, i, 0))],
        out_specs=pl.BlockSpec((tile, LANE), lambda i: (i, 0)),
        out_shape=jax.ShapeDtypeStruct((r, LANE), F32),
        compiler_params=_cparams("parallel"),
    )(a)


def f_premod(rv, cv):
    (x,), (w, scale, shift) = rv, cv
    y = x * lax.rsqrt(jnp.mean(x * x, -1, keepdims=True) + EPS) * w
    return [y * (1.0 + scale) + shift]


def f_postres(rv, cv):
    (x, y), (w, gate) = rv, cv
    return [x + gate * (y * lax.rsqrt(jnp.mean(y * y, -1, keepdims=True) + EPS) * w)]


def f_res_premod(rv, cv):
    (x, y), (w_post, gate, w_pre, scale, shift) = rv, cv
    x1, = f_postres([x, y], [w_post, gate])
    h, = f_premod([x1], [w_pre, scale, shift])
    return [x1, h]


def f_adamw(rv, cv):
    w, g, m, v = rv
    m = B1 * m + (1.0 - B1) * g
    v = B2 * v + (1.0 - B2) * jnp.square(g)
    m_hat = m / (1.0 - B1 ** STEP)
    v_hat = v / (1.0 - B2 ** STEP)
    return [-LR * (m_hat / (jnp.sqrt(v_hat) + AEPS) + WD * w), m, v], []


def adamw_layers(name, w, g, m, v):
    nl, r, wd = w.shape
    tile = r
    while tile * wd * 4 * 16 > 24 * 1024 * 1024 and tile % 16 == 0:
        tile //= 2

    outs = None
    for l in range(nl):
        def body(w_ref, g_ref, m_ref, v_ref, *rest):
            gv = g_ref[...]
            (dl, m2, v2), _ = f_adamw([w_ref[...], gv, m_ref[...], v_ref[...]], [])
            for ref, val in zip(rest[-4:], (gv, dl, m2, v2)):
                ref[...] = val

        blk = pl.BlockSpec((None, tile, wd), lambda i, l=l: (l, i, 0))
        outs = pl.pallas_call(
            body, name=f"{name}{l}", grid=(r // tile,), in_specs=[blk] * 4 + ([ANY] * 4 if outs else []),
            out_specs=[blk] * 4, out_shape=[jax.ShapeDtypeStruct(w.shape, F32)] * 4,
            input_output_aliases={4: 0, 5: 1, 6: 2, 7: 3} if outs else {}, compiler_params=_cparams("parallel"),
        )(w, g, m, v, *(outs or ()))
    return outs


def adamw(name, w, g, m, v):
    r, wd = w.shape
    tile = r
    while tile * wd * 4 * 14 > 24 * 1024 * 1024 and tile % 16 == 0:
        tile //= 2
    (d, m2, v2), _ = rowmap(name, f_adamw, [(a, 0, wd) for a in (w, g, m, v)], [], [(wd, F32)] * 3, tile=tile)
    return d, m2, v2


_IN_SIZES = (512, 512, 512, 512, 4, 4, 512, 1024, 8, 512, 128, 128, 512, 512, 512)


def _relayout_w_in(w):
    d = w.shape[0]
    s = np.cumsum((0,) + _IN_SIZES)
    gq, gk, gv, gz, gb, ga, sz, sx, sdt, aq, ak, av, cb, cc, ch = [w[:, s[i]:s[i + 1]] for i in range(15)]

    def rope_pad(t, nh):
        return jnp.pad(t.reshape(d, nh, 2, 32), ((0, 0), (0, 0), (0, 0), (0, 32))).reshape(d, nh * LANE)

    def v_pad(t, nh):
        return jnp.pad(t.reshape(d, nh, 64), ((0, 0), (0, 0), (0, 64))).reshape(d, nh * LANE)

    def pad128(t):
        return jnp.pad(t, ((0, 0), (0, LANE - t.shape[1])))

    return jnp.concatenate([gq, gk, gv, cb, cc, ch, sx, rope_pad(aq, 8), gz, sz, rope_pad(ak, 2), v_pad(av, 2),
                            pad128(jnp.concatenate([gb, ga], axis=1)), pad128(sdt)], axis=1)


def _unlayout_g_in(g):
    d = g.shape[0]

    def rope_unpad(t, nh):
        return t.reshape(d, nh, 2, 64)[..., :32].reshape(d, nh * 64)

    def v_unpad(t, nh):
        return t.reshape(d, nh, LANE)[..., :64].reshape(d, nh * 64)

    return jnp.concatenate([g[:, 0:1536], g[:, O_GZ:O_GZ + 512], g[:, O_GBA:O_GBA + 8], g[:, O_SZ:O_SZ + 512],
                            g[:, O_XBC:O_XBC + 1024], g[:, O_SDT:O_SDT + 8], rope_unpad(g[:, O_AQ:O_AQ + 1024], 8),
                            rope_unpad(g[:, O_AK:O_AK + 256], 2), v_unpad(g[:, O_AV:O_AV + 256], 2),
                            g[:, O_CB:O_CB + 1536]], axis=1)


def _relayout_w_out(w):
    d = w.shape[1]
    yc = jnp.pad(w[1024:1536].reshape(8, 64, d), ((0, 0), (0, 64), (0, 0))).reshape(1024, d)
    return jnp.concatenate([w[:1024], yc, w[1536:]], axis=0)


def _unlayout_g_out(g):
    d = g.shape[1]
    return jnp.concatenate([g[:1024], g[1024:2048].reshape(8, LANE, d)[:, :64].reshape(512, d), g[2048:]], axis=0)


def _pack(arrs):
    rows, spec, off = [], [], 0
    for a in arrs:
        n = int(np.prod(a.shape))
        nr = -(-n // LANE)
        rows.append(jnp.pad(a.reshape(-1).astype(F32), (0, nr * LANE - n)).reshape(nr, LANE))
        spec.append((off, a.shape))
        off += nr
    pad = (-off) % 8
    if pad:
        rows.append(jnp.zeros((pad, LANE), F32))
    return jnp.concatenate(rows, axis=0), spec


def _unpack(buf, spec):
    out = []
    for off, shape in spec:
        n = int(np.prod(shape))
        nr = -(-n // LANE)
        out.append(buf[off:off + nr].reshape(-1)[:n].reshape(shape))
    return out


def _row128(v):
    return jnp.pad(v.astype(F32), (0, LANE - v.shape[0])).reshape(1, LANE)


BIG = ("w_in", "w_out", "w_up", "w_down")
SMALL_REPL = ("ada_b", "norm_pre_mix", "norm_post_mix", "norm_pre_mlp", "norm_post_mlp", "gdn_a_log", "gdn_dt_bias",
              "gdn_norm_w", "ssm_conv_b", "ssm_a_log", "ssm_dt_bias", "ssm_d", "ssm_norm_w", "attn_sinks")
SMALL_SHARD = ("gdn_conv_w", "ssm_conv_w", "sc_conv_w")
WEIGHTS = ("ada_w", "ada_b", "norm_pre_mix", "norm_post_mix", "norm_pre_mlp", "norm_post_mlp", "w_in", "w_out",
           "gdn_conv_w", "gdn_a_log", "gdn_dt_bias", "gdn_norm_w", "ssm_conv_w", "ssm_conv_b", "ssm_a_log",
           "ssm_dt_bias", "ssm_d", "ssm_norm_w", "attn_sinks", "sc_conv_w", "w_up", "w_down")


def _step(p, m, v, x, c, positions, loss_target):
    d = x.shape[2]
    xi, yi, ci = _me()
    chip = 2 * xi + yi
    me = 4 * xi + 2 * yi + ci
    x0 = x[0]
    tgt = loss_target[0]

    buf, spec = _pack([c[0], p["gdn_conv_w"], p["ssm_conv_w"], p["sc_conv_w"]])
    got = allgather8("ag_small_in", buf)
    per_dev = [_unpack(got[k], spec) for k in range(8)]
    c_all = jnp.stack([pd[0] for pd in per_dev], axis=0)
    conv_full = [jnp.concatenate([per_dev[4 * a + 2 * b][1 + j] for a in range(2) for b in range(2)], axis=-1)
                 for j in range(3)]
    gdn_conv_w, ssm_conv_w, sc_conv_w = conv_full

    c_pad = jnp.pad(c_all, ((0, 8), (0, 0)))
    silu_bf = lambda a: _silu(a).astype(BF16)
    nsh = p["ada_w"].shape[2]
    mods = []
    for l in range(DEPTH):
        mo = mm_nn(f"ada_fwd{l}", c_pad, p["ada_w"][l], F32, a_fn=silu_bf)
        mods.append(mo[:8] + lax.dynamic_slice(p["ada_b"][l], (chip * nsh,), (nsh,))[None])
    buf, spec = _pack([jnp.stack(mods, axis=0)])
    got = allgather8("ag_mod", buf)
    mod_sh = [_unpack(got[4 * a + 2 * b], spec)[0] for a in range(2) for b in range(2)]
    mod_all = jnp.concatenate(mod_sh, axis=-1)
    mod = lax.dynamic_index_in_dim(mod_all, me, axis=1, keepdims=False)

    items = [(l, n) for l in range(DEPTH) for n in BIG]
    shards = {(l, n): p[n][l].astype(BF16) for (l, n) in items}
    gathered = {(0, "w_in"): allgather_chips("ag_w_in0", [shards[0, "w_in"]])[0]}

    def gather_behind(keys):
        return ("gather", [shards[kk] for kk in keys])

    def settle(name, keys, landed):
        gathered.update(zip(keys, ag_forward(name, landed)))

    wfull = []

    inv_freq = ROPE_THETA ** (-jnp.arange(0, 64, 2, dtype=F32) / 64)
    ang = positions[0].astype(F32)[:, None] * inv_freq
    z32 = jnp.zeros_like(ang)
    cos_t = jnp.concatenate([jnp.cos(ang), z32, jnp.cos(ang), z32], axis=1)
    sin_t = jnp.concatenate([-jnp.sin(ang), z32, jnp.sin(ang), z32], axis=1)

    relu2 = lambda a: jnp.square(jnp.maximum(a, 0))

    def layer_consts(l):
        row = lambda v_: v_.reshape(1, -1).astype(F32)
        return dict(
            alog=_row128(p["gdn_a_log"][l]), gdtb=_row128(p["gdn_dt_bias"][l]), gnw=row(p["gdn_norm_w"][l]),
            scb=row(p["ssm_conv_b"][l]), salog=_row128(p["ssm_a_log"][l]), sdtb=_row128(p["ssm_dt_bias"][l]),
            sdsk=_row128(p["ssm_d"][l]), snw=row(p["ssm_norm_w"][l]), sinks=_row128(p["attn_sinks"][l]))

    saved = []
    xc = x0
    for l in range(DEPTH):
        w = dict(w_in=_relayout_w_in(jnp.concatenate([gathered[l, "w_in"][s] for s in range(4)], axis=1)))
        wfull.append(w)
        k = layer_consts(l)
        md = [mod[l, j * d:(j + 1) * d].reshape(1, d) for j in range(6)]
        npre, npost = p["norm_pre_mix"][l].reshape(1, d), p["norm_post_mix"][l].reshape(1, d)
        npre2, npost2 = p["norm_pre_mlp"][l].reshape(1, d), p["norm_post_mlp"][l].reshape(1, d)
        if l == 0:
            (h,), _ = rowmap("premix0", lambda rv, cv: (f_premod(rv, cv), []), [(xc, 0, d)], [npre, md[1], md[0]],
                             [(d, BF16)])
        behind = [(0, "w_out"), (0, "w_up")] if l == 0 else [(l, "w_up")]
        proj, landed = mm_nn(f"inproj{l}", h, w["w_in"], F32, carry=gather_behind(behind))
        settle(f"ag_forward_a{l}", behind, landed)
        w["w_out"] = _relayout_w_out(gathered[l, "w_out"].reshape(-1, d))
        w["w_up"] = gathered[l, "w_up"]
        cq = conv_fwd(f"gdn_conv{l}", [(proj, O_GQKV, 1536)], gdn_conv_w[l])
        ya, gs, gt = gdn_fwd(f"gdn{l}", cq, proj, k["alog"], k["gdtb"], k["gnw"])
        cx = conv_fwd(f"ssm_conv{l}", [(proj, O_XBC, 1024)], ssm_conv_w[l])
        yb, ss = ssd_fwd(f"ssd{l}", cx, proj, k["scb"], k["salog"], k["sdtb"], k["sdsk"], k["snw"])
        yc = swa_fwd(f"swa{l}", proj, cos_t, sin_t, k["sinks"])
        cd = conv_fwd(f"sc_conv{l}", [(proj, O_CC, GW), (proj, O_CH, GW)], sc_conv_w[l])
        (ycat,), _ = rowmap(f"ycat{l}", lambda rv, cv: ([jnp.concatenate([rv[0], rv[1], rv[2], rv[3] * rv[4]], 1)], []),
                            [(ya, 0, GW), (yb, 0, GW), (yc, 0, 1024), (proj, O_CB, GW), (cd, 0, GW)], [],
                            [(YCAT, BF16)])
        y = mm_nn(f"outproj{l}", ycat, w["w_out"], F32)
        (x1, h2), _ = rowmap(f"mix_mlp{l}", lambda rv, cv: (f_res_premod(rv, cv), []), [(xc, 0, d), (y, 0, d)],
                             [npost, md[2], npre2, md[4], md[3]], [(d, F32), (d, BF16)])
        u, landed = mm_nn(f"up{l}", h2, w["w_up"], BF16, carry=gather_behind([(l, "w_down")]))
        settle(f"ag_forward_b{l}", [(l, "w_down")], landed)
        w["w_down"] = gathered[l, "w_down"].reshape(-1, d)
        if l + 1 < DEPTH:
            behind = [(l + 1, "w_in"), (l + 1, "w_out")]
            y2, landed = mm_nn(f"down{l}", u, w["w_down"], F32, a_fn=relu2, carry=gather_behind(behind))
            settle(f"ag_forward_c{l}", behind, landed)
        else:
            y2 = mm_nn(f"down{l}", u, w["w_down"], F32, a_fn=relu2)
        saved.append(dict(x=xc, h=h, proj=proj, cq=cq, gs=gs, gt=gt, cx=cx, ss=ss, cd=cd, ycat=ycat, y=y, x1=x1, h2=h2, u=u,
                          y2=y2, md=md, k=k))
        if l + 1 < DEPTH:
            (xc, h), _ = rowmap(f"mlp_mix{l}", lambda rv, cv: (f_res_premod(rv, cv), []), [(x1, 0, d), (y2, 0, d)],
                                [npost2, md[5], p["norm_pre_mix"][l + 1].reshape(1, d),
                                 mod[l + 1, d:2 * d].reshape(1, d), mod[l + 1, 0:d].reshape(1, d)],
                                [(d, F32), (d, BF16)])

    def f_loss(vals, cvals):
        x1_, y2_, tgt_ = vals
        (x2,), vjp = jax.vjp(lambda r, c: f_postres(list(r), c), [x1_, y2_], cvals)
        err = x2 - tgt_
        part = 0.5 * jnp.sum(jnp.mean(err * err, axis=-1, keepdims=True), axis=0, keepdims=True)
        dr, dc = vjp([err * (1.0 / d)])
        return list(dr), list(dc) + [jnp.broadcast_to(part, (1, LANE))]

    s = saved[-1]
    (dx1a, dy2), (dnpost2, dgate_m, loss_part) = rowmap(
        "loss_postmlp_b", f_loss, [(s["x1"], 0, d), (s["y2"], 0, d), (tgt, 0, d)],
        [p["norm_post_mlp"][DEPTH - 1].reshape(1, d), s["md"][5]], [(d, F32), (d, BF16)],
        [(1, d), (1, d), (1, LANE)])

    idx = jnp.stack([chip, ci]).astype(jnp.int32)
    g4s, recv1s, recv2s, waiting = {}, {}, {}, []

    def reduce_start(key, g):
        l_, n_ = key
        g4 = g.reshape(4, 2, g.shape[1] // 2, g.shape[2])
        r1 = sibling_swap_halves(f"rs_sibling_{n_}{l_}", [g4])[0]
        g4s[key], recv1s[key] = g4, r1
        waiting.append((key, add_half(f"rs_add_{n_}{l_}", g4, r1, idx)))

    def mm_carrying(name, *args, **kw):
        if not waiting:
            return mm_nn(name, *args, **kw)
        out, landed = mm_nn(name, *args, carry=("exchange", [pt for _, pt in waiting]), **kw)
        recv2s.update(zip([kk for kk, _ in waiting], landed))
        waiting.clear()
        return out

    gsm, dmd = {}, {}
    for l in reversed(range(DEPTH)):
        w, s = wfull[l], saved[l]
        k, md = s["k"], s["md"]
        npre, npost = p["norm_pre_mix"][l].reshape(1, d), p["norm_post_mix"][l].reshape(1, d)
        npre2, npost2 = p["norm_pre_mlp"][l].reshape(1, d), p["norm_post_mlp"][l].reshape(1, d)
        if l == DEPTH - 1:
            gsm[l, "norm_post_mlp"], dmd[l, 5] = dnpost2[0], dgate_m
        du = mm_carrying(f"down_b{l}", dy2, w["w_down"], BF16, extra=s["u"], wt=True,
                         epi=lambda r, uu: r * (2.0 * jnp.maximum(uu.astype(F32), 0.0)))
        reduce_start((l, "w_down"), mm_tn(f"down_g{l}", s["u"], dy2, a_fn=relu2).reshape(4, -1, d))
        dh2 = mm_carrying(f"up_b{l}", du, w["w_up"], F32, wt=True)
        reduce_start((l, "w_up"), mm_tn(f"up_g{l}", s["h2"], du, col_slabs=4))
        (dxa, dy), (dnpost, dgate_a, dnpre2, dscale_m, dshift_m) = rowmap_vjp(
            f"mix_mlp_b{l}", f_res_premod, [(s["x"], 0, d), (s["y"], 0, d)], [npost, md[2], npre2, md[4], md[3]],
            [(dx1a, 0, d), (dh2, 0, d)], 2, [F32, BF16])
        dycat = mm_nn(f"outproj_b{l}", dy, w["w_out"], F32, wt=True)
        reduce_start((l, "w_out"), _unlayout_g_out(mm_tn(f"outproj_g{l}", s["ycat"], dy)).reshape(4, -1, d))
        (dcb, dcd), _ = rowmap(f"scgate_b{l}", lambda rv, cv: ([rv[0] * rv[2], rv[0] * rv[1]], []),
                               [(dycat, 2048, GW), (s["proj"], O_CB, GW), (s["cd"], 0, GW)], [],
                               [(GW, BF16), (GW, F32)])
        (dcc, dch), g_scw = conv_bwd(f"sc_conv_b{l}", dcd, [(s["proj"], O_CC, GW), (s["proj"], O_CH, GW)],
                                     sc_conv_w[l], BF16)
        daq, dak, dav, g_sinks = swa_bwd(f"swa_b{l}", dycat, 1024, s["proj"], cos_t, sin_t, k["sinks"])
        dcx, dsz, dsdt, g_scb, g_salog, g_sdtb, g_sdsk, g_snw = ssd_bwd(
            f"ssd_b{l}", dycat, 512, s["cx"], s["proj"], s["ss"], k["scb"], k["salog"], k["sdtb"], k["sdsk"], k["snw"])
        (dxbc,), g_ssmw = conv_bwd(f"ssm_conv_b{l}", dcx, [(s["proj"], O_XBC, 1024)], ssm_conv_w[l], BF16)
        dcq, dgz, dgba, g_alog, g_gdtb, g_gnw = gdn_bwd(f"gdn_b{l}", dycat, 0, s["cq"], s["proj"], s["gs"], s["gt"],
                                                        k["alog"], k["gdtb"], k["gnw"])
        (dgqkv,), g_gdnw = conv_bwd(f"gdn_conv_b{l}", dcq, [(s["proj"], O_GQKV, 1536)], gdn_conv_w[l], BF16)
        dproj = jnp.concatenate([dgqkv, dcb, dcc, dch, dxbc, daq, dgz, dsz, dak, dav, dgba, dsdt], axis=1)
        def start_w_in():
            g_in = mm_tn(f"inproj_g{l}", s["h"], dproj)
            reduce_start((l, "w_in"), jnp.stack(jnp.split(_unlayout_g_in(g_in), 4, axis=1), axis=0))

        if l == 0:
            start_w_in()
        dh = mm_carrying(f"inproj_b{l}", dproj, w["w_in"], F32, wt=True)
        if l > 0:
            start_w_in()
            sp = saved[l - 1]
            (dx1a, dy2), (dnpost2, dgate_m, dnpre, dscale_a, dshift_a) = rowmap_vjp(
                f"mlp_mix_b{l - 1}", f_res_premod, [(sp["x1"], 0, d), (sp["y2"], 0, d)],
                [p["norm_post_mlp"][l - 1].reshape(1, d), sp["md"][5], npre, md[1], md[0]],
                [(dxa, 0, d), (dh, 0, d)], 2, [F32, BF16])
            gsm[l - 1, "norm_post_mlp"], dmd[l - 1, 5] = dnpost2[0], dgate_m
        else:
            (dx,), (dnpre, dscale_a, dshift_a) = rowmap_vjp("premix_b0", f_premod, [(s["x"], 0, d)],
                                                            [npre, md[1], md[0]], [(dh, 0, d)], 1, [F32],
                                                            add_rows=[(dxa, 0, d)])
        dmd.update({(l, 0): dshift_a, (l, 1): dscale_a, (l, 2): dgate_a, (l, 3): dshift_m, (l, 4): dscale_m})
        gsm.update({(l, n_): g_ for n_, g_ in dict(
            norm_pre_mix=dnpre[0], norm_post_mix=dnpost[0], norm_pre_mlp=dnpre2[0],
            gdn_a_log=g_alog[0, :4], gdn_dt_bias=g_gdtb[0, :4], gdn_norm_w=g_gnw[0], ssm_conv_b=g_scb[0],
            ssm_a_log=g_salog[0, :8], ssm_dt_bias=g_sdtb[0, :8], ssm_d=g_sdsk[0, :8], ssm_norm_w=g_snw[0],
            attn_sinks=g_sinks[0, :8], gdn_conv_w=g_gdnw, ssm_conv_w=g_ssmw, sc_conv_w=g_scw).items()})
    grad_x = dx[None]
    dmods = [jnp.concatenate([dmd[l, j] for j in range(6)], axis=1)[0] for l in range(DEPTH)]
    gsmall = [{n_: gsm[l, n_] for n_ in SMALL_REPL + SMALL_SHARD if n_ != "ada_b"} for l in range(DEPTH)]

    buf, spec = _pack([jnp.stack(dmods, axis=0)])
    got = allgather8("ag_dmod", buf)
    dmod_sum = _unpack(sum_slabs("dmod_sum", got), spec)[0]
    dmod_all = jnp.stack([_unpack(got[kk], spec)[0] for kk in range(8)], axis=1)
    g_ada_w = []
    for l in range(DEPTH):
        dm = lax.dynamic_slice(dmod_all[l], (0, chip * nsh), (8, nsh))
        g_ada_w.append(mm_tn(f"ada_g{l}", c_pad, jnp.pad(dm, ((0, 8), (0, 0))), a_fn=silu_bf))
    g_ada_w = jnp.stack(g_ada_w, axis=0)

    names = [n for n in SMALL_REPL if n != "ada_b"] + list(SMALL_SHARD)
    buf, spec = _pack([jnp.stack([gsmall[l][n] for l in range(DEPTH)], axis=0) for n in names] + [loss_part])
    tot = _unpack(sum_slabs("small_sum", allgather8("ag_small_g", buf)), spec)
    gfull = dict(zip(names, tot[:-1]))
    loss = tot[-1][0, 0]
    grads = {"ada_w": g_ada_w, "ada_b": dmod_sum}
    for n in SMALL_REPL:
        if n != "ada_b":
            grads[n] = gfull[n]
    for n in SMALL_SHARD:
        wdt = p[n].shape[2]
        grads[n] = lax.dynamic_slice_in_dim(gfull[n], chip * wdt, wdt, axis=2)

    if waiting:
        recv2s.update(zip([kk for kk, _ in waiting], chips_exchange("rs_chips_last", [pt for _, pt in waiting])))
    half = []
    for n in BIG:
        acc = None
        for l in range(DEPTH):
            acc = rs_sum(f"rs_sum_{n}{l}", g4s[l, n], recv1s[l, n], recv2s[l, n], idx, l, acc)
        half.append(acc)
    grads.update(zip(BIG, sibling_fill("rs_join", half)))

    delta, new_m, new_v = {}, {}, {}
    for n in ("ada_w",) + BIG:
        grads[n], delta[n], new_m[n], new_v[n] = adamw_layers(f"adamw_{n}", p[n], grads[n], m[n], v[n])
    small = list(SMALL_REPL) + list(SMALL_SHARD)
    bufs = []
    for src in (p, grads, m, v):
        b_, spec = _pack([src[n] for n in small])
        bufs.append(b_)
    dl, m2, v2 = adamw("adamw_small", *bufs)
    for n, a, b_, cc_ in zip(small, _unpack(dl, spec), _unpack(m2, spec), _unpack(v2, spec)):
        delta[n], new_m[n], new_v[n] = a, b_, cc_

    return (loss, grad_x, *[grads[n] for n in WEIGHTS], *[delta[n] for n in WEIGHTS], *[new_m[n] for n in WEIGHTS],
            *[new_v[n] for n in WEIGHTS])


def kernel(x, c, positions, ada_w, ada_b, norm_pre_mix, norm_post_mix, norm_pre_mlp, norm_post_mlp, w_in, w_out, gdn_conv_w, gdn_a_log, gdn_dt_bias, gdn_norm_w, ssm_conv_w, ssm_conv_b, ssm_a_log, ssm_dt_bias, ssm_d, ssm_norm_w, attn_sinks, sc_conv_w, w_up, w_down, loss_target, m_ada_w, m_ada_b, m_norm_pre_mix, m_norm_post_mix, m_norm_pre_mlp, m_norm_post_mlp, m_w_in, m_w_out, m_gdn_conv_w, m_gdn_a_log, m_gdn_dt_bias, m_gdn_norm_w, m_ssm_conv_w, m_ssm_conv_b, m_ssm_a_log, m_ssm_dt_bias, m_ssm_d, m_ssm_norm_w, m_attn_sinks, m_sc_conv_w, m_w_up, m_w_down, v_ada_w, v_ada_b, v_norm_pre_mix, v_norm_post_mix, v_norm_pre_mlp, v_norm_post_mlp, v_w_in, v_w_out, v_gdn_conv_w, v_gdn_a_log, v_gdn_dt_bias, v_gdn_norm_w, v_ssm_conv_w, v_ssm_conv_b, v_ssm_a_log, v_ssm_dt_bias, v_ssm_d, v_ssm_norm_w, v_attn_sinks, v_sc_conv_w, v_w_up, v_w_down):
    loc = locals()
    p = {n: loc[n] for n in WEIGHTS}
    m = {n: loc["m_" + n] for n in WEIGHTS}
    v = {n: loc["v_" + n] for n in WEIGHTS}
    return _step(p, m, v, x, c, positions, loss_target)
```

```python
import functools

import numpy as np
import jax
import jax.numpy as jnp
from jax import lax
from jax.experimental import pallas as pl
from jax.experimental.pallas import tpu as pltpu

F32, BF16 = jnp.float32, jnp.bfloat16
MESH = pl.DeviceIdType.MESH

DEPTH = 2
GW = 512
N_GDN, GDN_CS = 4, 64
SSD_CS, ATT_W = 128, 128
EPS = 1e-6
ROPE_THETA = 10000.0
IN_WIDTH = 5904
B1, B2, LR, AEPS, WD, STEP = 0.9, 0.999, 0.001, 1e-8, 0.01, 10

VMEM_LIMIT = 52 * 1024 * 1024
LANE = 128

O_GQKV, O_CB, O_CC, O_CH, O_XBC, O_AQ, O_GZ, O_SZ, O_AK, O_AV, O_GBA, O_SDT = (
    0, 1536, 2048, 2560, 3072, 4096, 5120, 5632, 6144, 6400, 6656, 6784)
PW = 6912
YCAT = 2560


def _cparams(*sem):
    return pltpu.CompilerParams(dimension_semantics=sem or None, vmem_limit_bytes=VMEM_LIMIT)


def _pick(n, cap, mult):
    if n <= cap:
        return n
    best = None
    for d in range(mult, cap + 1, mult):
        if n % d == 0:
            best = d
    assert best is not None, (n, cap, mult)
    return best


def _pieces(x, n):
    out, r = [], x
    for i in range(n):
        h = r.astype(BF16)
        out.append(h)
        if i < n - 1:
            r = r - h.astype(F32)
    return out


def _dgp(a, b, ca, cb, na, nb):
    dn = (((ca,), (cb,)), ((), ()))
    acc = None
    for i, pa in enumerate(_pieces(a, na)):
        for j, pb in enumerate(_pieces(b, nb)):
            if i + j < max(na, nb):
                t = lax.dot_general(pa, pb, dn, preferred_element_type=F32)
                acc = t if acc is None else acc + t
    return acc


_CONTRACT = {"nn": (1, 0), "nt": (1, 1), "tn": (0, 0)}


@functools.lru_cache(maxsize=None)
def _mm(kind, na, nb, const_a=False):
    ca, cb = _CONTRACT[kind]
    ng = max(na, nb)

    @jax.custom_vjp
    def f(a, b):
        return _dgp(a, b, ca, cb, na, nb)

    def bwd(r, g):
        a, b = r
        if kind == "nn":
            da, db = (lambda: _mm("nt", ng, nb)(g, b)), _mm("tn", na, ng, const_a)(a, g)
        elif kind == "nt":
            da, db = (lambda: _mm("nn", ng, nb)(g, b)), _mm("tn", ng, na)(g, a)
        else:
            da, db = (lambda: _mm("nt", nb, ng)(b, g)), _mm("nn", na, ng, const_a)(a, g)
        return (jnp.zeros_like(a) if const_a else da()), db

    f.defvjp(lambda a, b: (f(a, b), (a, b)), bwd)
    return f


_nn, _nt, _tn = _mm("nn", 1, 1), _mm("nt", 1, 1), _mm("tn", 1, 1)
_hnn, _hnt, _htn = _mm("nn", 2, 2), _mm("nt", 2, 2), _mm("tn", 2, 2)
_xnn = _mm("nn", 1, 3, True)


def _silu(x):
    return x * jax.nn.sigmoid(x)


def _softplus(x):
    return jnp.maximum(x, 0.0) + jnp.log1p(jnp.exp(-jnp.abs(x)))


def _colsel(v, idx):
    lane = lax.broadcasted_iota(jnp.int32, v.shape, 1)
    return jnp.sum(jnp.where(lane == idx, v, 0.0), axis=1, keepdims=True)


def _tri(n):
    r = lax.broadcasted_iota(jnp.int32, (n, n), 0)
    c = lax.broadcasted_iota(jnp.int32, (n, n), 1)
    return r >= c, r > c


def _rowsel(v, idx):
    row = lax.broadcasted_iota(jnp.int32, v.shape, 0)
    return jnp.sum(jnp.where(row == idx, v, 0.0), axis=0, keepdims=True)


def _tri_inv(m, size, nil):
    r = lax.broadcasted_iota(jnp.int32, (size, size), 0)
    c = lax.broadcasted_iota(jnp.int32, (size, size), 1)
    x = jnp.where(r == c, 1.0, 0.0) - m
    p = _hnn(m, m)
    steps = int(np.log2(nil)) - 1
    for s in range(steps):
        x = x + _hnn(x, p)
        if s < steps - 1:
            p = _hnn(p, p)
    return x


@jax.custom_vjp
def _inv_given(m, tinv):
    return tinv


_inv_given.defvjp(lambda m, tinv: (tinv, tinv),
                  lambda tinv, g: (-_hnt(_htn(tinv, g), tinv), jnp.zeros_like(tinv)))


def _gdn_chunk(cq, z, ba, s, alog, dtb, nw, tinv_saved=None):
    n, nh = GDN_CS, N_GDN
    rr = n * nh

    def stack(a, off):
        return jnp.concatenate([a[:, off + h * LANE:off + (h + 1) * LANE] for h in range(nh)], axis=0)

    def percol(f):
        return jnp.concatenate([f(h) for h in range(nh)], axis=0)

    def own(a):
        return jnp.concatenate([a[h * n:(h + 1) * n, h * LANE:(h + 1) * LANE] for h in range(nh)], axis=0)

    q, k, v = _silu(stack(cq, 0)), _silu(stack(cq, GW)), _silu(stack(cq, 2 * GW))
    q = q * lax.rsqrt(jnp.sum(q * q, -1, keepdims=True) + EPS) * (128 ** -0.5)
    k = k * lax.rsqrt(jnp.sum(k * k, -1, keepdims=True) + EPS)
    beta = jax.nn.sigmoid(percol(lambda h: _colsel(ba, h)))
    g = percol(lambda h: -jnp.exp(_colsel(alog, h)) * _softplus(_colsel(ba, nh + h) + _colsel(dtb, h)))
    ri = lax.broadcasted_iota(jnp.int32, (rr, rr), 0)
    ci = lax.broadcasted_iota(jnp.int32, (rr, rr), 1)
    same = (ri >> 6) == (ci >> 6)
    incl, strict = same & (ri >= ci), same & (ri > ci)
    gcb = _xnn(incl.astype(F32), jnp.broadcast_to(g, (rr, LANE)))
    gci = jnp.concatenate([gcb, gcb], axis=1)
    dec = jnp.where(incl, jnp.exp(jnp.where(incl, gci - gci.T, 0.0)), 0.0)
    kb = k * beta
    kk = _nt(jnp.concatenate([kb, q], axis=0), k)
    m = jnp.where(strict, kk[:rr] * dec, 0.0)
    attn = jnp.where(incl, kk[rr:] * dec, 0.0)
    tinv = _tri_inv(m, rr, n) if tinv_saved is None else _inv_given(m, tinv_saved)
    egc = jnp.exp(gcb)
    uw = _nn(tinv, jnp.concatenate([v * beta, kb * egc], axis=1))
    u, w = uw[:, :LANE], uw[:, LANE:]
    gl = [jnp.sum(g[h * n:(h + 1) * n], axis=0, keepdims=True) for h in range(nh)]
    kd = k * jnp.exp(percol(lambda h: jnp.broadcast_to(gl[h], (n, 1))) - gcb)
    ws = _nn(jnp.concatenate([w, q * egc], axis=0), s)
    v_new = u - own(ws[:rr])
    o = own(ws[rr:]) + _nn(attn, v_new)
    rowh = lax.broadcasted_iota(jnp.int32, (rr, LANE), 0) >> 6
    vexp = jnp.concatenate([jnp.where(rowh == h, v_new, 0.0) for h in range(nh)], axis=1)
    cd = jnp.concatenate([jnp.broadcast_to(jnp.exp(gl[h]), (1, LANE)) for h in range(nh)], axis=1)
    s_new = s * cd + _tn(kd, vexp)
    y = o * lax.rsqrt(jnp.mean(o * o, -1, keepdims=True) + EPS) * nw * _silu(stack(z, 0))
    y = jnp.concatenate([y[h * n:(h + 1) * n] for h in range(nh)], axis=1)
    return (y, s_new, tinv) if tinv_saved is None else (y, s_new)


def _ssd_chunk(cx, z, dtc, s, convb, alog, dtb, dsk, nw):
    n = SSD_CS
    incl, _ = _tri(n)
    xbc = _silu(cx + convb)
    dt_all = _softplus(dtc + dtb)
    da_all = dt_all * (-jnp.exp(alog))
    cs_all = _xnn(incl.astype(F32), da_all)
    cs_t = cs_all.T
    tot_all = jnp.sum(da_all, axis=0, keepdims=True)
    lane_lo = lax.broadcasted_iota(jnp.int32, (n, LANE), 1) < 64
    ys, s_out = [], []
    for pi in range(4):
        g = pi // 2
        bm = xbc[:, GW + g * LANE:GW + (g + 1) * LANE]
        cm = xbc[:, GW + 256 + g * LANE:GW + 256 + (g + 1) * LANE]
        cbm = _nt(cm, bm)
        xp, s_p = xbc[:, pi * LANE:(pi + 1) * LANE], s[:, pi * LANE:(pi + 1) * LANE]
        a_rows, b_cols, dts, dsks, etot = [], [], [], [], []
        for h in (2 * pi, 2 * pi + 1):
            cs_i = _colsel(cs_all, h)
            lm = jnp.where(incl, jnp.exp(jnp.where(incl, cs_i - _rowsel(cs_t, h), 0.0)), 0.0)
            tot = _colsel(tot_all, h)
            a_rows.append(jnp.concatenate([cbm * lm, cm * jnp.exp(cs_i)], axis=1))
            b_cols.append(bm * jnp.exp(tot - cs_i))
            dts.append(_colsel(dt_all, h))
            dsks.append(_colsel(dsk, h))
            etot.append(jnp.exp(tot))
        xdt = xp * jnp.where(lane_lo, dts[0], dts[1])
        yy = _nn(jnp.concatenate(a_rows, axis=0), jnp.concatenate([xdt, s_p], axis=0))
        ys.append(jnp.where(lane_lo, yy[:n], yy[n:]) + xp * jnp.where(lane_lo[:1], dsks[0], dsks[1]))
        st = _tn(jnp.concatenate(b_cols, axis=1), xdt)
        s_out.append(s_p * jnp.where(lane_lo[:1], etot[0], etot[1]) + jnp.where(lane_lo, st[:n], st[n:]))
    zs = _silu(z)
    out = []
    for g in range(2):
        y = jnp.concatenate(ys[2 * g:2 * g + 2], axis=1) * zs[:, g * 256:(g + 1) * 256]
        out.append(y * lax.rsqrt(jnp.mean(y * y, -1, keepdims=True) + EPS) * nw[:, g * 256:(g + 1) * 256])
    return jnp.concatenate(out, axis=1), jnp.concatenate(s_out, axis=1)


SWA_STACK = 1


def _swa_block(qg, kp, kc, vp, vc, sink, notfirst, *, g):
    w, ns = ATT_W, SWA_STACK
    kb = jnp.concatenate([kp, kc], axis=0)
    vb = jnp.concatenate([vp, vc], axis=0)
    qi = lax.broadcasted_iota(jnp.int32, (ns * w, 2 * w), 0) & (w - 1)
    kj = lax.broadcasted_iota(jnp.int32, (ns * w, 2 * w), 1)
    rel = qi + w - kj
    mask = (rel >= 0) & (rel < w) & ((kj >= w) | (notfirst > 0.5))
    outs = []
    for r0 in range(0, 4, ns):
        q = jnp.concatenate([qg[:, r * LANE:(r + 1) * LANE] for r in range(r0, r0 + ns)], axis=0)
        snk = jnp.concatenate([jnp.broadcast_to(_colsel(sink, 4 * g + r), (w, 1)) for r in range(r0, r0 + ns)], axis=0)
        s = jnp.where(mask, _nt(q, kb) * (64 ** -0.5), -1e30)
        mx = lax.stop_gradient(jnp.maximum(jnp.max(s, axis=-1, keepdims=True), snk))
        p = jnp.where(mask, jnp.exp(s - mx), 0.0)
        p = p / (jnp.sum(p, axis=-1, keepdims=True) + jnp.exp(snk - mx))
        o = _nn(p, vb)
        outs += [o[j * w:(j + 1) * w] for j in range(ns)]
    return jnp.concatenate(outs, axis=1)


def _rope(x, cos, sin):
    return x * cos + pltpu.roll(x, 64, 1) * sin


def _unrope(dy, cos, sin):
    return dy * cos + pltpu.roll(dy * sin, 64, 1)


def rowmap(name, fn, rows, consts, outs, accs=(), tile=256):
    t = rows[0][0].shape[0]
    tile = min(tile, t)
    assert t % tile == 0
    nr, nc, no, na = len(rows), len(consts), len(outs), len(accs)

    def body(*refs):
        rv = [r[...] for r in refs[:nr]]
        cv = [r[...] for r in refs[nr:nr + nc]]
        ov, av = fn(rv, cv)
        for ref, o in zip(refs[nr + nc:nr + nc + no], ov):
            ref[...] = o.astype(ref.dtype)
        if na:
            a_refs = refs[nr + nc + no:]

            @pl.when(pl.program_id(0) == 0)
            def _():
                for ref in a_refs:
                    ref[...] = jnp.zeros_like(ref)

            for ref, a in zip(a_refs, av):
                ref[...] += a

    def cspec(off, width):
        assert off % width == 0
        cb = off // width
        return pl.BlockSpec((tile, width), lambda i: (i, cb))

    in_specs = [cspec(off, wd) for (_, off, wd) in rows]
    in_specs += [pl.BlockSpec(c.shape, lambda i: (0, 0)) for c in consts]
    out_specs = [pl.BlockSpec((tile, wd), lambda i: (i, 0)) for (wd, _) in outs]
    out_specs += [pl.BlockSpec(s, lambda i: (0, 0)) for s in accs]
    out_shape = [jax.ShapeDtypeStruct((t, wd), dt) for (wd, dt) in outs]
    out_shape += [jax.ShapeDtypeStruct(s, F32) for s in accs]
    res = pl.pallas_call(
        body, name=name, grid=(t // tile,), in_specs=in_specs, out_specs=out_specs, out_shape=out_shape,
        compiler_params=_cparams("arbitrary"),
    )(*[r[0] for r in rows], *consts)
    return res[:no], res[no:]


def rowmap_vjp(name, f, rows, consts, douts, n_diff, drow_dtypes, tile=256, add_rows=()):
    nr, nd, nadd = len(rows), len(douts), len(add_rows)

    def g(vals, cvals):
        rv, dov, addv = vals[:nr], vals[nr:nr + nd], vals[nr + nd:]
        fixed = rv[n_diff:]
        _, vjp = jax.vjp(lambda r, c: f(list(r) + list(fixed), c), rv[:n_diff], cvals)
        dr, dc = vjp([d.astype(F32) for d in dov])
        dr = list(dr)
        for a in addv:
            dr[0] = dr[0] + a
        return dr, dc

    outs = [(rows[i][2], drow_dtypes[i]) for i in range(n_diff)]
    accs = [c.shape for c in consts]
    return rowmap(name, g, list(rows) + list(douts) + list(add_rows), consts, outs, accs, tile=tile)


def _carried_copies(kind, in_refs, out_refs, send_sems, recv_sems):
    x, y, c = _me()
    chip = 2 * x + y
    chips = [(1 - x, y), (x, 1 - y), (1 - x, 1 - y)]
    cps = []

    def rc(src, dst, k, to):
        return pltpu.make_async_remote_copy(src_ref=src, dst_ref=dst, send_sem=send_sems.at[k], recv_sem=recv_sems.at[k],
                                            device_id=to, device_id_type=MESH)

    for i, (src, dst) in enumerate(zip(in_refs, out_refs)):
        if kind == "gather":
            rh = src.shape[0] // 2
            half = pl.ds(pl.multiple_of(c * rh, 16), rh)
            cps.append(rc(src, dst.at[chip], 4 * i + 3, (x, y, 1 - c)))
            cps += [rc(src.at[half], dst.at[chip, half], 4 * i + k, (cx, cy, c)) for k, (cx, cy) in enumerate(chips)]
        else:
            cps += [rc(src.at[2 * cx + cy], dst.at[chip], 4 * i + k, (cx, cy, c)) for k, (cx, cy) in enumerate(chips)]
    return cps


def mm_nn(name, a, w, out_dtype, a_fn=None, extra=None, epi=None, tm_cap=1024, wt=False, carry=None):
    nc = len(carry[1]) if carry else 0
    m, k = a.shape
    stacked = w.ndim == 3
    n = (w.shape[-2] if wt else w.shape[-1] * (w.shape[0] if stacked else 1))
    slab = w.shape[-1]
    tm = _pick(m, tm_cap, 16)
    tn = _pick(n if wt else slab, 1024, LANE)
    tk = _pick(slab if wt else k, 2048, LANE)
    nk = k // tk
    per = slab // (tk if wt else tn)
    wdims = ((1,), (1,)) if wt else ((1,), (0,))

    gi, gj = m // tm, n // tn
    n_in = 2 + (extra is not None)

    def body(*refs):
        a_ref, w_ref = refs[0], refs[1]
        e_ref = refs[2] if extra is not None else None
        o_ref, acc = refs[n_in + nc], refs[n_in + 2 * nc + 1]
        kk = pl.program_id(2)
        if nc:
            def copies():
                return _carried_copies(carry[0], refs[n_in:n_in + nc], refs[n_in + nc + 1:n_in + 2 * nc + 1],
                                       refs[-2], refs[-1])

            @pl.when((pl.program_id(0) == 0) & (pl.program_id(1) == 0) & (kk == 0))
            def _():
                for cp in copies():
                    cp.start()

        @pl.when(kk == 0)
        def _():
            acc[...] = jnp.zeros_like(acc)

        av = a_ref[...]
        if a_fn is not None:
            av = a_fn(av)
        acc[...] += lax.dot_general(av.astype(BF16), w_ref[...].astype(BF16), (wdims, ((), ())),
                                    preferred_element_type=F32)

        @pl.when(kk == nk - 1)
        def _():
            r = acc[...]
            if epi is not None:
                r = epi(r, e_ref[...])
            o_ref[...] = r.astype(o_ref.dtype)

        if nc:
            @pl.when((pl.program_id(0) == gi - 1) & (pl.program_id(1) == gj - 1) & (kk == nk - 1))
            def _():
                cps = copies()
                for cp in cps:
                    cp.wait_recv()
                for cp in cps:
                    cp.wait_send()

    if wt:
        w_spec = (pl.BlockSpec((None, tn, tk), lambda i, j, kk: (kk // per, j, kk % per)) if stacked
                  else pl.BlockSpec((tn, tk), lambda i, j, kk: (j, kk)))
    else:
        w_spec = (pl.BlockSpec((None, tk, tn), lambda i, j, kk: (j // per, kk, j % per)) if stacked
                  else pl.BlockSpec((tk, tn), lambda i, j, kk: (kk, j)))
    in_specs = [pl.BlockSpec((tm, tk), lambda i, j, kk: (i, kk)), w_spec]
    ops = [a, w]
    if extra is not None:
        in_specs.append(pl.BlockSpec((tm, tn), lambda i, j, kk: (i, j)))
        ops.append(extra)
    out_spec = pl.BlockSpec((tm, tn), lambda i, j, kk: (i, j))
    out_shape = jax.ShapeDtypeStruct((m, n), out_dtype)
    if not nc:
        return pl.pallas_call(
            body, name=name, grid=(gi, gj, nk), in_specs=in_specs, out_specs=out_spec, out_shape=out_shape,
            scratch_shapes=[pltpu.VMEM((tm, tn), F32)],
            compiler_params=_cparams("parallel", "parallel", "arbitrary"),
        )(*ops)
    landing = [jax.ShapeDtypeStruct((4,) + c.shape if carry[0] == "gather" else c.shape, c.dtype) for c in carry[1]]
    res = pl.pallas_call(
        body, name=name, grid=(gi, gj, nk), in_specs=in_specs + [ANY] * nc, out_specs=[out_spec] + [ANY] * nc,
        out_shape=[out_shape] + landing, scratch_shapes=[pltpu.VMEM((tm, tn), F32)] + _sems(4 * nc),
        compiler_params=_cparams("arbitrary", "arbitrary", "arbitrary"),
    )(*ops, *carry[1])
    return res[0], list(res[1:])


def mm_tn(name, a, b, a_fn=None, col_slabs=1):
    t, m = a.shape
    n = b.shape[1]
    tm, tn, tk = _pick(m, 1024, LANE), _pick(n // col_slabs, 1024, LANE), _pick(t, 2048, 16)
    nk = t // tk
    per = n // col_slabs // tn

    def body(a_ref, b_ref, o_ref, acc):
        kk = pl.program_id(2)

        @pl.when(kk == 0)
        def _():
            acc[...] = jnp.zeros_like(acc)

        av = a_ref[...]
        if a_fn is not None:
            av = a_fn(av)
        acc[...] += lax.dot_general(av.astype(BF16), b_ref[...].astype(BF16), (((0,), (0,)), ((), ())),
                                    preferred_element_type=F32)

        @pl.when(kk == nk - 1)
        def _():
            o_ref[...] = acc[...]

    return pl.pallas_call(
        body, name=name, grid=(m // tm, n // tn, nk),
        in_specs=[pl.BlockSpec((tk, tm), lambda i, j, kk: (kk, i)), pl.BlockSpec((tk, tn), lambda i, j, kk: (kk, j))],
        out_specs=(pl.BlockSpec((tm, tn), lambda i, j, kk: (i, j)) if col_slabs == 1
                   else pl.BlockSpec((None, tm, tn), lambda i, j, kk: (j // per, i, j % per))),
        out_shape=jax.ShapeDtypeStruct((m, n) if col_slabs == 1 else (col_slabs, m, n // col_slabs), F32),
        scratch_shapes=[pltpu.VMEM((tm, tn), F32)],
        compiler_params=_cparams("parallel", "parallel", "arbitrary"),
    )(a, b)


CONV_TILE = 512
HALO = 8


def conv_fwd(name, xs, w):
    t = xs[0][0].shape[0]
    c = xs[0][2]
    kw = w.shape[0]
    tt = min(CONV_TILE, t)
    nx = len(xs)

    def body(*refs):
        cur_refs, prev_refs, w_ref, y_ref = refs[:nx], refs[nx:2 * nx], refs[2 * nx], refs[2 * nx + 1]
        i = pl.program_id(0)
        cur, prev = cur_refs[0][...], prev_refs[0][...]
        for r, p in zip(cur_refs[1:], prev_refs[1:]):
            cur, prev = cur * r[...], prev * p[...]
        prev = jnp.where(i > 0, prev, 0.0)
        xx = jnp.concatenate([prev, cur], axis=0)
        acc = jnp.zeros((tt, c), F32)
        for k in range(kw):
            sh = kw - 1 - k
            rolled = pltpu.roll(xx, sh, 0) if sh else xx
            acc = acc + rolled[HALO:] * w_ref[k:k + 1, :]
        y_ref[...] = acc

    def cur_spec(off):
        cb = off // c
        return pl.BlockSpec((tt, c), lambda i: (i, cb))

    def prev_spec(off):
        cb = off // c
        return pl.BlockSpec((HALO, c), lambda i: (jnp.maximum(i * (tt // HALO) - 1, 0), cb))

    for (_, off, wd) in xs:
        assert wd == c and off % c == 0
    return pl.pallas_call(
        body, name=name, grid=(t // tt,),
        in_specs=[cur_spec(off) for (_, off, _) in xs] + [prev_spec(off) for (_, off, _) in xs]
        + [pl.BlockSpec(w.shape, lambda i: (0, 0))],
        out_specs=pl.BlockSpec((tt, c), lambda i: (i, 0)),
        out_shape=jax.ShapeDtypeStruct((t, c), F32),
        compiler_params=_cparams("arbitrary"),
    )(*[x[0] for x in xs], *[x[0] for x in xs], w)


def conv_bwd(name, dy, xs, w, dx_dtype):
    t = xs[0][0].shape[0]
    c = xs[0][2]
    kw = w.shape[0]
    tt = min(CONV_TILE, t)
    nt = t // tt
    nx = len(xs)
    n = tt + HALO

    def body(*refs):
        dy_ref = refs[0]
        cur_refs, prev_refs = refs[1:1 + nx], refs[1 + nx:1 + 2 * nx]
        w_ref = refs[1 + 2 * nx]
        dx_refs = refs[2 + 2 * nx:2 + 3 * nx]
        dw_ref, carry = refs[2 + 3 * nx], refs[3 + 3 * nx]
        i = pl.program_id(0)

        @pl.when(i == 0)
        def _():
            carry[...] = jnp.zeros_like(carry)
            dw_ref[...] = jnp.zeros_like(dw_ref)

        curs = [r[...] for r in cur_refs]
        cur, prev = curs[0], prev_refs[0][...]
        for cv, p in zip(curs[1:], prev_refs[1:]):
            cur, prev = cur * cv, prev * p[...]
        prev = jnp.where(i < nt - 1, prev, 0.0)
        xx = jnp.concatenate([prev, cur], axis=0)
        dyv = dy_ref[...]
        dpad = jnp.concatenate([jnp.zeros((HALO, c), F32), dyv], axis=0)
        dxx = jnp.zeros((n, c), F32)
        dws = []
        for k in range(kw):
            sh = kw - 1 - k
            dxx = dxx + (pltpu.roll(dpad, n - sh, 0) if sh else dpad) * w_ref[k:k + 1, :]
            rolled = pltpu.roll(xx, sh, 0) if sh else xx
            dws.append(jnp.sum(dyv * rolled[HALO:], axis=0, keepdims=True))
        dw_ref[...] += jnp.concatenate(dws, axis=0)
        dcur = jnp.concatenate([dxx[HALO:tt], dxx[tt:] + carry[...]], axis=0)
        carry[...] = dxx[:HALO]
        for j, ref in enumerate(dx_refs):
            d = dcur
            for jj, cv in enumerate(curs):
                if jj != j:
                    d = d * cv
            ref[...] = d.astype(ref.dtype)

    def cur_spec(off):
        cb = off // c
        return pl.BlockSpec((tt, c), lambda i: (nt - 1 - i, cb))

    def prev_spec(off):
        cb = off // c
        return pl.BlockSpec((HALO, c), lambda i: (jnp.maximum((nt - 1 - i) * (tt // HALO) - 1, 0), cb))

    res = pl.pallas_call(
        body, name=name, grid=(nt,),
        in_specs=[pl.BlockSpec((tt, c), lambda i: (nt - 1 - i, 0))] + [cur_spec(off) for (_, off, _) in xs]
        + [prev_spec(off) for (_, off, _) in xs] + [pl.BlockSpec(w.shape, lambda i: (0, 0))],
        out_specs=[pl.BlockSpec((tt, c), lambda i: (nt - 1 - i, 0)) for _ in xs] + [pl.BlockSpec(w.shape, lambda i: (0, 0))],
        out_shape=[jax.ShapeDtypeStruct((t, c), dx_dtype) for _ in xs] + [jax.ShapeDtypeStruct(w.shape, F32)],
        scratch_shapes=[pltpu.VMEM((HALO, c), F32)],
        compiler_params=_cparams("arbitrary"),
    )(dy, *[x[0] for x in xs], *[x[0] for x in xs], w)
    return res[:nx], res[nx]


def _const_spec(a):
    return pl.BlockSpec(a.shape, lambda i: (0,) * a.ndim)


GDN_STEP = 4


def _gdn_rows(ref, c):
    return ref[c * GDN_CS:(c + 1) * GDN_CS, :]


def gdn_fwd(name, cq, proj, alog, dtb, nw):
    t = cq.shape[0]
    nc = t // GDN_CS
    ns = min(GDN_STEP, nc)
    rows = ns * GDN_CS

    def body(cq_ref, z_ref, ba_ref, alog_ref, dtb_ref, nw_ref, y_ref, ssave_ref, tsave_ref, s_scr):
        @pl.when(pl.program_id(0) == 0)
        def _():
            s_scr[...] = jnp.zeros_like(s_scr)

        s = s_scr[...]
        ys, saved, tinvs = [], [], []
        for c in range(ns):
            saved.append(s)
            y, s, tinv = _gdn_chunk(_gdn_rows(cq_ref, c), _gdn_rows(z_ref, c), _gdn_rows(ba_ref, c), s, alog_ref[...],
                                    dtb_ref[...], nw_ref[...])
            ys.append(y)
            tinvs.append(tinv)
        y_ref[...] = jnp.concatenate(ys, axis=0)
        ssave_ref[...] = jnp.stack(saved, axis=0)
        tsave_ref[...] = jnp.stack(tinvs, axis=0)
        s_scr[...] = s

    return pl.pallas_call(
        body, name=name, grid=(nc // ns,),
        in_specs=[pl.BlockSpec((rows, 3 * GW), lambda i: (i, 0)),
                  pl.BlockSpec((rows, GW), lambda i: (i, O_GZ // GW)),
                  pl.BlockSpec((rows, LANE), lambda i: (i, O_GBA // LANE)),
                  _const_spec(alog), _const_spec(dtb), _const_spec(nw)],
        out_specs=[pl.BlockSpec((rows, GW), lambda i: (i, 0)),
                   pl.BlockSpec((ns, LANE, N_GDN * LANE), lambda i: (i, 0, 0)),
                   pl.BlockSpec((ns, N_GDN * GDN_CS, N_GDN * GDN_CS), lambda i: (i, 0, 0))],
        out_shape=[jax.ShapeDtypeStruct((t, GW), F32), jax.ShapeDtypeStruct((nc, LANE, N_GDN * LANE), F32),
                   jax.ShapeDtypeStruct((nc, N_GDN * GDN_CS, N_GDN * GDN_CS), F32)],
        scratch_shapes=[pltpu.VMEM((LANE, N_GDN * LANE), F32)],
        compiler_params=_cparams("arbitrary"),
    )(cq, proj, proj, alog, dtb, nw)


def gdn_bwd(name, dy, dy_off, cq, proj, ssave, tsave, alog, dtb, nw):
    t = cq.shape[0]
    nc = t // GDN_CS
    ns = min(GDN_STEP, nc)
    rows = ns * GDN_CS

    def body(dy_ref, cq_ref, z_ref, ba_ref, ssave_ref, tsave_ref, alog_ref, dtb_ref, nw_ref,
             dcq_ref, dz_ref, dba_ref, dalog_ref, ddtb_ref, dnw_ref, ds_scr):
        @pl.when(pl.program_id(0) == 0)
        def _():
            ds_scr[...] = jnp.zeros_like(ds_scr)
            dalog_ref[...] = jnp.zeros_like(dalog_ref)
            ddtb_ref[...] = jnp.zeros_like(ddtb_ref)
            dnw_ref[...] = jnp.zeros_like(dnw_ref)

        ds = ds_scr[...]
        dcq, dzs, dbas = [None] * ns, [None] * ns, [None] * ns
        dal, ddt, dnw = (jnp.zeros(r.shape, F32) for r in (dalog_ref, ddtb_ref, dnw_ref))
        for c in reversed(range(ns)):
            _, vjp = jax.vjp(functools.partial(_gdn_chunk, tinv_saved=tsave_ref[c]), _gdn_rows(cq_ref, c),
                             _gdn_rows(z_ref, c), _gdn_rows(ba_ref, c), ssave_ref[c], alog_ref[...], dtb_ref[...],
                             nw_ref[...])
            dcq[c], dzs[c], dbas[c], ds, dal_c, ddt_c, dnw_c = vjp((_gdn_rows(dy_ref, c), ds))
            dal, ddt, dnw = dal + dal_c, ddt + ddt_c, dnw + dnw_c
        dcq_ref[...] = jnp.concatenate(dcq, axis=0)
        dz_ref[...] = jnp.concatenate(dzs, axis=0).astype(dz_ref.dtype)
        dba_ref[...] = jnp.concatenate(dbas, axis=0).astype(dba_ref.dtype)
        ds_scr[...] = ds
        dalog_ref[...] += dal
        ddtb_ref[...] += ddt
        dnw_ref[...] += dnw

    rev = lambda i: nc // ns - 1 - i
    return pl.pallas_call(
        body, name=name, grid=(nc // ns,),
        in_specs=[pl.BlockSpec((rows, GW), lambda i: (rev(i), dy_off // GW)),
                  pl.BlockSpec((rows, 3 * GW), lambda i: (rev(i), 0)),
                  pl.BlockSpec((rows, GW), lambda i: (rev(i), O_GZ // GW)),
                  pl.BlockSpec((rows, LANE), lambda i: (rev(i), O_GBA // LANE)),
                  pl.BlockSpec((ns, LANE, N_GDN * LANE), lambda i: (rev(i), 0, 0)),
                  pl.BlockSpec((ns, N_GDN * GDN_CS, N_GDN * GDN_CS), lambda i: (rev(i), 0, 0)),
                  _const_spec(alog), _const_spec(dtb), _const_spec(nw)],
        out_specs=[pl.BlockSpec((rows, 3 * GW), lambda i: (rev(i), 0)),
                   pl.BlockSpec((rows, GW), lambda i: (rev(i), 0)),
                   pl.BlockSpec((rows, LANE), lambda i: (rev(i), 0)),
                   _const_spec(alog), _const_spec(dtb), _const_spec(nw)],
        out_shape=[jax.ShapeDtypeStruct((t, 3 * GW), F32), jax.ShapeDtypeStruct((t, GW), BF16),
                   jax.ShapeDtypeStruct((t, LANE), BF16), jax.ShapeDtypeStruct(alog.shape, F32),
                   jax.ShapeDtypeStruct(dtb.shape, F32), jax.ShapeDtypeStruct(nw.shape, F32)],
        scratch_shapes=[pltpu.VMEM((LANE, N_GDN * LANE), F32)],
        compiler_params=_cparams("arbitrary"),
    )(dy, cq, proj, proj, ssave, tsave, alog, dtb, nw)


SSD_STEP = 2


def _ssd_rows(ref, c):
    return ref[c * SSD_CS:(c + 1) * SSD_CS, :]


def ssd_fwd(name, cx, proj, convb, alog, dtb, dsk, nw):
    t = cx.shape[0]
    nc = t // SSD_CS
    ns = min(SSD_STEP, nc)
    rows = ns * SSD_CS

    def body(cx_ref, z_ref, dt_ref, cb_ref, alog_ref, dtb_ref, dsk_ref, nw_ref, y_ref, ssave_ref, s_scr):
        @pl.when(pl.program_id(0) == 0)
        def _():
            s_scr[...] = jnp.zeros_like(s_scr)

        s = s_scr[...]
        ys, saved = [], []
        for c in range(ns):
            saved.append(s)
            y, s = _ssd_chunk(_ssd_rows(cx_ref, c), _ssd_rows(z_ref, c), _ssd_rows(dt_ref, c), s, cb_ref[...],
                              alog_ref[...], dtb_ref[...], dsk_ref[...], nw_ref[...])
            ys.append(y)
        y_ref[...] = jnp.concatenate(ys, axis=0)
        ssave_ref[...] = jnp.stack(saved, axis=0)
        s_scr[...] = s

    return pl.pallas_call(
        body, name=name, grid=(nc // ns,),
        in_specs=[pl.BlockSpec((rows, 1024), lambda i: (i, 0)),
                  pl.BlockSpec((rows, GW), lambda i: (i, O_SZ // GW)),
                  pl.BlockSpec((rows, LANE), lambda i: (i, O_SDT // LANE)),
                  _const_spec(convb), _const_spec(alog), _const_spec(dtb), _const_spec(dsk), _const_spec(nw)],
        out_specs=[pl.BlockSpec((rows, GW), lambda i: (i, 0)),
                   pl.BlockSpec((ns, LANE, GW), lambda i: (i, 0, 0))],
        out_shape=[jax.ShapeDtypeStruct((t, GW), F32), jax.ShapeDtypeStruct((nc, LANE, GW), F32)],
        scratch_shapes=[pltpu.VMEM((LANE, GW), F32)],
        compiler_params=_cparams("arbitrary"),
    )(cx, proj, proj, convb, alog, dtb, dsk, nw)


def ssd_bwd(name, dy, dy_off, cx, proj, ssave, convb, alog, dtb, dsk, nw):
    t = cx.shape[0]
    nc = t // SSD_CS
    ns = min(SSD_STEP, nc)
    rows = ns * SSD_CS

    def body(dy_ref, cx_ref, z_ref, dt_ref, ssave_ref, cb_ref, alog_ref, dtb_ref, dsk_ref, nw_ref,
             dcx_ref, dz_ref, ddt_ref, dcb_ref, dalog_ref, ddtb_ref, ddsk_ref, dnw_ref, ds_scr):
        @pl.when(pl.program_id(0) == 0)
        def _():
            ds_scr[...] = jnp.zeros_like(ds_scr)
            for r in (dcb_ref, dalog_ref, ddtb_ref, ddsk_ref, dnw_ref):
                r[...] = jnp.zeros_like(r)

        ds = ds_scr[...]
        dcx, dz, ddt = [None] * ns, [None] * ns, [None] * ns
        acc = [jnp.zeros(r.shape, F32) for r in (dcb_ref, dalog_ref, ddtb_ref, ddsk_ref, dnw_ref)]
        for c in reversed(range(ns)):
            _, vjp = jax.vjp(_ssd_chunk, _ssd_rows(cx_ref, c), _ssd_rows(z_ref, c), _ssd_rows(dt_ref, c), ssave_ref[c],
                             cb_ref[...], alog_ref[...], dtb_ref[...], dsk_ref[...], nw_ref[...])
            dcx[c], dz[c], ddt[c], ds, *dconst = vjp((_ssd_rows(dy_ref, c), ds))
            acc = [a + g for a, g in zip(acc, dconst)]
        dcx_ref[...] = jnp.concatenate(dcx, axis=0)
        dz_ref[...] = jnp.concatenate(dz, axis=0).astype(dz_ref.dtype)
        ddt_ref[...] = jnp.concatenate(ddt, axis=0).astype(ddt_ref.dtype)
        ds_scr[...] = ds
        for r, a in zip((dcb_ref, dalog_ref, ddtb_ref, ddsk_ref, dnw_ref), acc):
            r[...] += a

    rev = lambda i: nc // ns - 1 - i
    return pl.pallas_call(
        body, name=name, grid=(nc // ns,),
        in_specs=[pl.BlockSpec((rows, GW), lambda i: (rev(i), dy_off // GW)),
                  pl.BlockSpec((rows, 1024), lambda i: (rev(i), 0)),
                  pl.BlockSpec((rows, GW), lambda i: (rev(i), O_SZ // GW)),
                  pl.BlockSpec((rows, LANE), lambda i: (rev(i), O_SDT // LANE)),
                  pl.BlockSpec((ns, LANE, GW), lambda i: (rev(i), 0, 0)),
                  _const_spec(convb), _const_spec(alog), _const_spec(dtb), _const_spec(dsk), _const_spec(nw)],
        out_specs=[pl.BlockSpec((rows, 1024), lambda i: (rev(i), 0)),
                   pl.BlockSpec((rows, GW), lambda i: (rev(i), 0)),
                   pl.BlockSpec((rows, LANE), lambda i: (rev(i), 0)),
                   _const_spec(convb), _const_spec(alog), _const_spec(dtb), _const_spec(dsk), _const_spec(nw)],
        out_shape=[jax.ShapeDtypeStruct((t, 1024), F32), jax.ShapeDtypeStruct((t, GW), BF16),
                   jax.ShapeDtypeStruct((t, LANE), BF16), jax.ShapeDtypeStruct(convb.shape, F32),
                   jax.ShapeDtypeStruct(alog.shape, F32), jax.ShapeDtypeStruct(dtb.shape, F32),
                   jax.ShapeDtypeStruct(dsk.shape, F32), jax.ShapeDtypeStruct(nw.shape, F32)],
        scratch_shapes=[pltpu.VMEM((LANE, GW), F32)],
        compiler_params=_cparams("arbitrary"),
    )(dy, cx, proj, proj, ssave, convb, alog, dtb, dsk, nw)


def _swa_specs(nb, rev):
    w = ATT_W
    cur = lambda i: (nb - 1 - i) if rev else i
    prv = lambda i: jnp.maximum(cur(i) - 1, 0)
    return [pl.BlockSpec((w, 1024), lambda i: (cur(i), O_AQ // 1024)),
            pl.BlockSpec((w, 256), lambda i: (cur(i), O_AK // 256)),
            pl.BlockSpec((w, 256), lambda i: (prv(i), O_AK // 256)),
            pl.BlockSpec((w, 256), lambda i: (cur(i), O_AV // 256)),
            pl.BlockSpec((w, 256), lambda i: (prv(i), O_AV // 256)),
            pl.BlockSpec((w, LANE), lambda i: (cur(i), 0)), pl.BlockSpec((w, LANE), lambda i: (prv(i), 0)),
            pl.BlockSpec((w, LANE), lambda i: (cur(i), 0)), pl.BlockSpec((w, LANE), lambda i: (prv(i), 0))]


def swa_fwd(name, proj, cos, sin, sinks):
    t = proj.shape[0]
    nb = t // ATT_W

    def body(q_ref, kc_ref, kp_ref, vc_ref, vp_ref, cc_ref, cp_ref, sc_ref, sp_ref, snk_ref, y_ref):
        notfirst = jnp.where(pl.program_id(0) > 0, 1.0, 0.0).astype(F32)
        cc, cp, sc, sp = cc_ref[...], cp_ref[...], sc_ref[...], sp_ref[...]
        outs = []
        for g in range(2):
            sl = slice(g * LANE, (g + 1) * LANE)
            qg = jnp.concatenate([_rope(q_ref[:, (4 * g + r) * LANE:(4 * g + r + 1) * LANE], cc, sc)
                                  for r in range(4)], axis=1)
            outs.append(_swa_block(qg, _rope(kp_ref[:, sl], cp, sp), _rope(kc_ref[:, sl], cc, sc), vp_ref[:, sl],
                                   vc_ref[:, sl], snk_ref[...], notfirst, g=g))
        y_ref[...] = jnp.concatenate(outs, axis=1)

    return pl.pallas_call(
        body, name=name, grid=(nb,),
        in_specs=_swa_specs(nb, False) + [_const_spec(sinks)],
        out_specs=pl.BlockSpec((ATT_W, 1024), lambda i: (i, 0)),
        out_shape=jax.ShapeDtypeStruct((t, 1024), F32),
        compiler_params=_cparams("arbitrary"),
    )(proj, proj, proj, proj, proj, cos, cos, sin, sin, sinks)


def swa_bwd(name, dy, dy_off, proj, cos, sin, sinks):
    t = proj.shape[0]
    nb = t // ATT_W

    def body(dy_ref, q_ref, kc_ref, kp_ref, vc_ref, vp_ref, cc_ref, cp_ref, sc_ref, sp_ref, snk_ref,
             dq_ref, dk_ref, dv_ref, dsnk_ref, ck_scr, cv_scr):
        i = pl.program_id(0)

        @pl.when(i == 0)
        def _():
            ck_scr[...] = jnp.zeros_like(ck_scr)
            cv_scr[...] = jnp.zeros_like(cv_scr)
            dsnk_ref[...] = jnp.zeros_like(dsnk_ref)

        notfirst = jnp.where(i < nb - 1, 1.0, 0.0).astype(F32)
        cc, cp, sc, sp = cc_ref[...], cp_ref[...], sc_ref[...], sp_ref[...]
        ck_in, cv_in = ck_scr[...], cv_scr[...]
        dq, dk, dv, ck, cv, dsnk = [], [], [], [], [], jnp.zeros(dsnk_ref.shape, F32)
        for g in range(2):
            sl = slice(g * LANE, (g + 1) * LANE)
            qg = jnp.concatenate([_rope(q_ref[:, (4 * g + r) * LANE:(4 * g + r + 1) * LANE], cc, sc)
                                  for r in range(4)], axis=1)
            _, vjp = jax.vjp(functools.partial(_swa_block, g=g), qg, _rope(kp_ref[:, sl], cp, sp),
                             _rope(kc_ref[:, sl], cc, sc), vp_ref[:, sl], vc_ref[:, sl], snk_ref[...], notfirst)
            dqg, dkp, dkc, dvp, dvc, dsnk_g, _ = vjp(dy_ref[:, g * GW:(g + 1) * GW])
            dq += [_unrope(dqg[:, r * LANE:(r + 1) * LANE], cc, sc) for r in range(4)]
            dk.append(_unrope(dkc, cc, sc) + ck_in[:, sl])
            dv.append(dvc + cv_in[:, sl])
            ck.append(_unrope(dkp, cp, sp))
            cv.append(dvp)
            dsnk = dsnk + dsnk_g
        dq_ref[...] = jnp.concatenate(dq, axis=1).astype(dq_ref.dtype)
        dk_ref[...] = jnp.concatenate(dk, axis=1).astype(dk_ref.dtype)
        dv_ref[...] = jnp.concatenate(dv, axis=1).astype(dv_ref.dtype)
        ck_scr[...] = jnp.concatenate(ck, axis=1)
        cv_scr[...] = jnp.concatenate(cv, axis=1)
        dsnk_ref[...] += dsnk

    rev = lambda i: nb - 1 - i
    return pl.pallas_call(
        body, name=name, grid=(nb,),
        in_specs=[pl.BlockSpec((ATT_W, 1024), lambda i: (rev(i), dy_off // 1024))] + _swa_specs(nb, True)
        + [_const_spec(sinks)],
        out_specs=[pl.BlockSpec((ATT_W, 1024), lambda i: (rev(i), 0)),
                   pl.BlockSpec((ATT_W, 256), lambda i: (rev(i), 0)),
                   pl.BlockSpec((ATT_W, 256), lambda i: (rev(i), 0)), _const_spec(sinks)],
        out_shape=[jax.ShapeDtypeStruct((t, 1024), BF16), jax.ShapeDtypeStruct((t, 256), BF16),
                   jax.ShapeDtypeStruct((t, 256), BF16), jax.ShapeDtypeStruct(sinks.shape, F32)],
        scratch_shapes=[pltpu.VMEM((ATT_W, 256), F32), pltpu.VMEM((ATT_W, 256), F32)],
        compiler_params=_cparams("arbitrary"),
    )(dy, proj, proj, proj, proj, proj, cos, cos, sin, sin, sinks)


def _me():
    return lax.axis_index("x"), lax.axis_index("y"), lax.axis_index("c")


ANY = pl.BlockSpec(memory_space=pl.ANY)


def allgather8(name, buf):
    r = buf.shape[0]

    def body(x_ref, out_ref, send_sems, recv_sems, local_sem):
        x, y, c = _me()
        me = 4 * x + 2 * y + c
        mine = pltpu.make_async_copy(x_ref, out_ref.at[me], local_sem)
        mine.start()
        copies = []
        for k in range(1, 8):
            bx, by, bc = (k >> 2) & 1, (k >> 1) & 1, k & 1
            to = ((1 - x) if bx else x, (1 - y) if by else y, (1 - c) if bc else c)
            cp = pltpu.make_async_remote_copy(src_ref=x_ref, dst_ref=out_ref.at[me], send_sem=send_sems.at[k - 1],
                                              recv_sem=recv_sems.at[k - 1], device_id=to, device_id_type=MESH)
            cp.start()
            copies.append(cp)
        for cp in copies:
            cp.wait_recv()
        for cp in copies:
            cp.wait_send()
        mine.wait()

    return pl.pallas_call(
        body, name=name, out_shape=jax.ShapeDtypeStruct((8, r, LANE), buf.dtype),
        in_specs=[pl.BlockSpec(memory_space=pltpu.VMEM)], out_specs=pl.BlockSpec(memory_space=pltpu.VMEM),
        scratch_shapes=[pltpu.SemaphoreType.DMA((7,)), pltpu.SemaphoreType.DMA((7,)), pltpu.SemaphoreType.DMA],
        compiler_params=pltpu.CompilerParams(vmem_limit_bytes=VMEM_LIMIT),
    )(buf)


def _sems(n):
    return [pltpu.SemaphoreType.DMA((n,)), pltpu.SemaphoreType.DMA((n,))]


def allgather_chips(name, shards):
    n = len(shards)

    def body(*refs):
        x_refs, out_refs, send_sems, recv_sems = refs[:n], refs[n:2 * n], refs[2 * n], refs[2 * n + 1]
        x, y, c = _me()
        chip = 2 * x + y
        chips = [(1 - x, y), (x, 1 - y), (1 - x, 1 - y)]

        def rc(src, dst, k, to):
            return pltpu.make_async_remote_copy(src_ref=src, dst_ref=dst, send_sem=send_sems.at[k],
                                                recv_sem=recv_sems.at[k], device_id=to, device_id_type=MESH)

        halves = [pl.ds(pl.multiple_of(c * (s.shape[0] // 2), 16), s.shape[0] // 2) for s in shards]
        own, first = [], {}
        for i in range(n):
            cp = rc(x_refs[i], out_refs[i].at[chip], 7 * i + 6, (x, y, 1 - c))
            cp.start()
            own.append(cp)
            for k, (cx, cy) in enumerate(chips):
                cp = rc(x_refs[i].at[halves[i]], out_refs[i].at[chip, halves[i]], 7 * i + k, (cx, cy, c))
                cp.start()
                first[i, k] = cp
        passed = []
        for i in range(n):
            for k, (cx, cy) in enumerate(chips):
                first[i, k].wait_recv()
                src = out_refs[i].at[2 * cx + cy, halves[i]]
                cp = rc(src, src, 7 * i + 3 + k, (x, y, 1 - c))
                cp.start()
                passed.append(cp)
        for cp in passed + own:
            cp.wait_recv()
        for cp in list(first.values()) + passed + own:
            cp.wait_send()

    for s in shards:
        assert s.shape[0] % 32 == 0
    return pl.pallas_call(
        body, name=name, out_shape=[jax.ShapeDtypeStruct((4,) + s.shape, s.dtype) for s in shards],
        in_specs=[ANY] * n, out_specs=[ANY] * n, scratch_shapes=_sems(7 * n),
    )(*shards)


def ag_forward(name, bufs):
    n = len(bufs)

    def body(*refs):
        out_refs, send_sems, recv_sems = refs[n:2 * n], refs[2 * n], refs[2 * n + 1]
        x, y, c = _me()
        cps = []
        for i in range(n):
            rh = bufs[i].shape[1] // 2
            half = pl.ds(pl.multiple_of(c * rh, 16), rh)
            for k, (cx, cy) in enumerate([(1 - x, y), (x, 1 - y), (1 - x, 1 - y)]):
                src = out_refs[i].at[2 * cx + cy, half]
                cps.append(pltpu.make_async_remote_copy(src_ref=src, dst_ref=src, send_sem=send_sems.at[3 * i + k],
                                                        recv_sem=recv_sems.at[3 * i + k], device_id=(x, y, 1 - c),
                                                        device_id_type=MESH))
        for cp in cps:
            cp.start()
        for cp in cps:
            cp.wait_recv()
        for cp in cps:
            cp.wait_send()

    return pl.pallas_call(
        body, name=name, out_shape=[jax.ShapeDtypeStruct(b.shape, b.dtype) for b in bufs],
        in_specs=[ANY] * n, out_specs=[ANY] * n, scratch_shapes=_sems(3 * n),
        input_output_aliases={i: i for i in range(n)},
    )(*bufs)


def sibling_swap_halves(name, gs):
    n = len(gs)

    def body(*refs):
        g_refs, out_refs, send_sems, recv_sems = refs[:n], refs[n:2 * n], refs[2 * n], refs[2 * n + 1]
        x, y, c = _me()
        cps = [pltpu.make_async_remote_copy(src_ref=g_refs[i].at[:, 1 - c], dst_ref=out_refs[i],
                                            send_sem=send_sems.at[i], recv_sem=recv_sems.at[i],
                                            device_id=(x, y, 1 - c), device_id_type=MESH) for i in range(n)]
        for cp in cps:
            cp.start()
        for cp in cps:
            cp.wait_recv()
        for cp in cps:
            cp.wait_send()

    return pl.pallas_call(
        body, name=name, out_shape=[jax.ShapeDtypeStruct((4,) + g.shape[2:], g.dtype) for g in gs],
        in_specs=[ANY] * n, out_specs=[ANY] * n, scratch_shapes=_sems(n),
    )(*gs)


def chips_exchange(name, ps):
    n = len(ps)

    def body(*refs):
        p_refs, out_refs, send_sems, recv_sems = refs[:n], refs[n:2 * n], refs[2 * n], refs[2 * n + 1]
        x, y, c = _me()
        chip = 2 * x + y
        cps = []
        for i in range(n):
            for k, (cx, cy) in enumerate([(1 - x, y), (x, 1 - y), (1 - x, 1 - y)]):
                cps.append(pltpu.make_async_remote_copy(
                    src_ref=p_refs[i].at[2 * cx + cy], dst_ref=out_refs[i].at[chip], send_sem=send_sems.at[3 * i + k],
                    recv_sem=recv_sems.at[3 * i + k], device_id=(cx, cy, c), device_id_type=MESH))
        for cp in cps:
            cp.start()
        for cp in cps:
            cp.wait_recv()
        for cp in cps:
            cp.wait_send()

    return pl.pallas_call(
        body, name=name, out_shape=[jax.ShapeDtypeStruct(p.shape, p.dtype) for p in ps],
        in_specs=[ANY] * n, out_specs=[ANY] * n, scratch_shapes=_sems(3 * n),
    )(*ps)


def sibling_fill(name, ss):
    n = len(ss)

    def body(*refs):
        out_refs, send_sems, recv_sems = refs[n:2 * n], refs[2 * n], refs[2 * n + 1]
        x, y, c = _me()
        cps = []
        for i in range(n):
            rh = ss[i].shape[1] // 2
            mine = out_refs[i].at[:, pl.ds(pl.multiple_of(c * rh, 8), rh)]
            cps.append(pltpu.make_async_remote_copy(src_ref=mine, dst_ref=mine, send_sem=send_sems.at[i],
                                                    recv_sem=recv_sems.at[i], device_id=(x, y, 1 - c),
                                                    device_id_type=MESH))
        for cp in cps:
            cp.start()
        for cp in cps:
            cp.wait_recv()
        for cp in cps:
            cp.wait_send()

    return pl.pallas_call(
        body, name=name, out_shape=[jax.ShapeDtypeStruct(s.shape, s.dtype) for s in ss],
        in_specs=[ANY] * n, out_specs=[ANY] * n, scratch_shapes=_sems(n),
        input_output_aliases={i: i for i in range(n)},
    )(*ss)


def _rs_tile(rh, w):
    return _pick(rh, max(8, (1 << 19) // w // 8 * 8), 8)


def add_half(name, g, recv, idx):
    _, _, rh, w = g.shape
    tile = _rs_tile(rh, w)

    def body(s_ref, g_ref, r_ref, o_ref):
        o_ref[...] = (g_ref[...] + r_ref[...]).astype(o_ref.dtype)

    return pl.pallas_call(
        body, name=name,
        grid_spec=pltpu.PrefetchScalarGridSpec(
            num_scalar_prefetch=1, grid=(4, rh // tile),
            in_specs=[pl.BlockSpec((None, None, tile, w), lambda s, i, ix: (s, ix[1], i, 0)),
                      pl.BlockSpec((None, tile, w), lambda s, i, ix: (s, i, 0))],
            out_specs=pl.BlockSpec((None, tile, w), lambda s, i, ix: (s, i, 0))),
        out_shape=jax.ShapeDtypeStruct((4, rh, w), BF16),
        compiler_params=_cparams("parallel", "parallel"),
    )(idx, g, recv)


def rs_sum(name, g, recv1, recv2, idx, layer, acc=None):
    _, _, rh, w = g.shape
    tile = _rs_tile(rh, w)
    nt = rh // tile

    def body(s_ref, g_ref, r1_ref, a_ref, b_ref, c_ref, *rest):
        rest[-1][...] = (g_ref[...] + r1_ref[...]) + (a_ref[...].astype(F32) + b_ref[...].astype(F32)
                                                       + c_ref[...].astype(F32))

    def other(k):
        return pl.BlockSpec((None, tile, w), lambda i, ix: (jnp.bitwise_xor(ix[0], k), i, 0))

    ops = [idx, g, recv1, recv2, recv2, recv2] + ([acc] if acc is not None else [])
    return pl.pallas_call(
        body, name=name,
        grid_spec=pltpu.PrefetchScalarGridSpec(
            num_scalar_prefetch=1, grid=(rh // tile,),
            in_specs=[pl.BlockSpec((None, None, tile, w), lambda i, ix: (ix[0], ix[1], i, 0)),
                      pl.BlockSpec((None, tile, w), lambda i, ix: (ix[0], i, 0)), other(1), other(2), other(3)]
            + ([ANY] if acc is not None else []),
            out_specs=pl.BlockSpec((None, tile, w), lambda i, ix: (layer, ix[1] * nt + i, 0))),
        out_shape=jax.ShapeDtypeStruct((DEPTH, 2 * rh, w), F32),
        input_output_aliases={6: 0} if acc is not None else {},
        compiler_params=_cparams("parallel"),
    )(*ops)


def sum_slabs(name, a):
    n, r, _ = a.shape
    tile = _pick(r, 2048, 8)

    def body(a_ref, o_ref):
        acc = a_ref[0]
        for s in range(1, n):
            acc = acc + a_ref[s]
        o_ref[...] = acc

    return pl.pallas_call(
        body, name=name, grid=(r // tile,),
        in_specs=[pl.BlockSpec((n, tile, LANE), lambda i: (0, i, 0))],
        out_specs=pl.BlockSpec((tile, LANE), lambda i: (i, 0)),
        out_shape=jax.ShapeDtypeStruct((r, LANE), F32),
        compiler_params=_cparams("parallel"),
    )(a)


def f_premod(rv, cv):
    (x,), (w, scale, shift) = rv, cv
    y = x * lax.rsqrt(jnp.mean(x * x, -1, keepdims=True) + EPS) * w
    return [y * (1.0 + scale) + shift]


def f_postres(rv, cv):
    (x, y), (w, gate) = rv, cv
    return [x + gate * (y * lax.rsqrt(jnp.mean(y * y, -1, keepdims=True) + EPS) * w)]


def f_res_premod(rv, cv):
    (x, y), (w_post, gate, w_pre, scale, shift) = rv, cv
    x1, = f_postres([x, y], [w_post, gate])
    h, = f_premod([x1], [w_pre, scale, shift])
    return [x1, h]


def f_adamw(rv, cv):
    w, g, m, v = rv
    m = B1 * m + (1.0 - B1) * g
    v = B2 * v + (1.0 - B2) * jnp.square(g)
    m_hat = m / (1.0 - B1 ** STEP)
    v_hat = v / (1.0 - B2 ** STEP)
    return [-LR * (m_hat / (jnp.sqrt(v_hat) + AEPS) + WD * w), m, v], []


def adamw_layers(name, w, g, m, v):
    nl, r, wd = w.shape
    tile = r
    while tile * wd * 4 * 16 > 24 * 1024 * 1024 and tile % 16 == 0:
        tile //= 2

    outs = None
    for l in range(nl):
        def body(w_ref, g_ref, m_ref, v_ref, *rest):
            gv = g_ref[...]
            (dl, m2, v2), _ = f_adamw([w_ref[...], gv, m_ref[...], v_ref[...]], [])
            for ref, val in zip(rest[-4:], (gv, dl, m2, v2)):
                ref[...] = val

        blk = pl.BlockSpec((None, tile, wd), lambda i, l=l: (l, i, 0))
        outs = pl.pallas_call(
            body, name=f"{name}{l}", grid=(r // tile,), in_specs=[blk] * 4 + ([ANY] * 4 if outs else []),
            out_specs=[blk] * 4, out_shape=[jax.ShapeDtypeStruct(w.shape, F32)] * 4,
            input_output_aliases={4: 0, 5: 1, 6: 2, 7: 3} if outs else {}, compiler_params=_cparams("parallel"),
        )(w, g, m, v, *(outs or ()))
    return outs


def adamw(name, w, g, m, v):
    r, wd = w.shape
    tile = r
    while tile * wd * 4 * 14 > 24 * 1024 * 1024 and tile % 16 == 0:
        tile //= 2
    (d, m2, v2), _ = rowmap(name, f_adamw, [(a, 0, wd) for a in (w, g, m, v)], [], [(wd, F32)] * 3, tile=tile)
    return d, m2, v2


_IN_SIZES = (512, 512, 512, 512, 4, 4, 512, 1024, 8, 512, 128, 128, 512, 512, 512)


def _relayout_w_in(w):
    d = w.shape[0]
    s = np.cumsum((0,) + _IN_SIZES)
    gq, gk, gv, gz, gb, ga, sz, sx, sdt, aq, ak, av, cb, cc, ch = [w[:, s[i]:s[i + 1]] for i in range(15)]

    def rope_pad(t, nh):
        return jnp.pad(t.reshape(d, nh, 2, 32), ((0, 0), (0, 0), (0, 0), (0, 32))).reshape(d, nh * LANE)

    def v_pad(t, nh):
        return jnp.pad(t.reshape(d, nh, 64), ((0, 0), (0, 0), (0, 64))).reshape(d, nh * LANE)

    def pad128(t):
        return jnp.pad(t, ((0, 0), (0, LANE - t.shape[1])))

    return jnp.concatenate([gq, gk, gv, cb, cc, ch, sx, rope_pad(aq, 8), gz, sz, rope_pad(ak, 2), v_pad(av, 2),
                            pad128(jnp.concatenate([gb, ga], axis=1)), pad128(sdt)], axis=1)


def _unlayout_g_in(g):
    d = g.shape[0]

    def rope_unpad(t, nh):
        return t.reshape(d, nh, 2, 64)[..., :32].reshape(d, nh * 64)

    def v_unpad(t, nh):
        return t.reshape(d, nh, LANE)[..., :64].reshape(d, nh * 64)

    return jnp.concatenate([g[:, 0:1536], g[:, O_GZ:O_GZ + 512], g[:, O_GBA:O_GBA + 8], g[:, O_SZ:O_SZ + 512],
                            g[:, O_XBC:O_XBC + 1024], g[:, O_SDT:O_SDT + 8], rope_unpad(g[:, O_AQ:O_AQ + 1024], 8),
                            rope_unpad(g[:, O_AK:O_AK + 256], 2), v_unpad(g[:, O_AV:O_AV + 256], 2),
                            g[:, O_CB:O_CB + 1536]], axis=1)


def _relayout_w_out(w):
    d = w.shape[1]
    yc = jnp.pad(w[1024:1536].reshape(8, 64, d), ((0, 0), (0, 64), (0, 0))).reshape(1024, d)
    return jnp.concatenate([w[:1024], yc, w[1536:]], axis=0)


def _unlayout_g_out(g):
    d = g.shape[1]
    return jnp.concatenate([g[:1024], g[1024:2048].reshape(8, LANE, d)[:, :64].reshape(512, d), g[2048:]], axis=0)


def _pack(arrs):
    rows, spec, off = [], [], 0
    for a in arrs:
        n = int(np.prod(a.shape))
        nr = -(-n // LANE)
        rows.append(jnp.pad(a.reshape(-1).astype(F32), (0, nr * LANE - n)).reshape(nr, LANE))
        spec.append((off, a.shape))
        off += nr
    pad = (-off) % 8
    if pad:
        rows.append(jnp.zeros((pad, LANE), F32))
    return jnp.concatenate(rows, axis=0), spec


def _unpack(buf, spec):
    out = []
    for off, shape in spec:
        n = int(np.prod(shape))
        nr = -(-n // LANE)
        out.append(buf[off:off + nr].reshape(-1)[:n].reshape(shape))
    return out


def _row128(v):
    return jnp.pad(v.astype(F32), (0, LANE - v.shape[0])).reshape(1, LANE)


BIG = ("w_in", "w_out", "w_up", "w_down")
SMALL_REPL = ("ada_b", "norm_pre_mix", "norm_post_mix", "norm_pre_mlp", "norm_post_mlp", "gdn_a_log", "gdn_dt_bias",
              "gdn_norm_w", "ssm_conv_b", "ssm_a_log", "ssm_dt_bias", "ssm_d", "ssm_norm_w", "attn_sinks")
SMALL_SHARD = ("gdn_conv_w", "ssm_conv_w", "sc_conv_w")
WEIGHTS = ("ada_w", "ada_b", "norm_pre_mix", "norm_post_mix", "norm_pre_mlp", "norm_post_mlp", "w_in", "w_out",
           "gdn_conv_w", "gdn_a_log", "gdn_dt_bias", "gdn_norm_w", "ssm_conv_w", "ssm_conv_b", "ssm_a_log",
           "ssm_dt_bias", "ssm_d", "ssm_norm_w", "attn_sinks", "sc_conv_w", "w_up", "w_down")


def _step(p, m, v, x, c, positions, loss_target):
    d = x.shape[2]
    xi, yi, ci = _me()
    chip = 2 * xi + yi
    me = 4 * xi + 2 * yi + ci
    x0 = x[0]
    tgt = loss_target[0]

    buf, spec = _pack([c[0], p["gdn_conv_w"], p["ssm_conv_w"], p["sc_conv_w"]])
    got = allgather8("ag_small_in", buf)
    per_dev = [_unpack(got[k], spec) for k in range(8)]
    c_all = jnp.stack([pd[0] for pd in per_dev], axis=0)
    conv_full = [jnp.concatenate([per_dev[4 * a + 2 * b][1 + j] for a in range(2) for b in range(2)], axis=-1)
                 for j in range(3)]
    gdn_conv_w, ssm_conv_w, sc_conv_w = conv_full

    c_pad = jnp.pad(c_all, ((0, 8), (0, 0)))
    silu_bf = lambda a: _silu(a).astype(BF16)
    nsh = p["ada_w"].shape[2]
    mods = []
    for l in range(DEPTH):
        mo = mm_nn(f"ada_fwd{l}", c_pad, p["ada_w"][l], F32, a_fn=silu_bf)
        mods.append(mo[:8] + lax.dynamic_slice(p["ada_b"][l], (chip * nsh,), (nsh,))[None])
    buf, spec = _pack([jnp.stack(mods, axis=0)])
    got = allgather8("ag_mod", buf)
    mod_sh = [_unpack(got[4 * a + 2 * b], spec)[0] for a in range(2) for b in range(2)]
    mod_all = jnp.concatenate(mod_sh, axis=-1)
    mod = lax.dynamic_index_in_dim(mod_all, me, axis=1, keepdims=False)

    items = [(l, n) for l in range(DEPTH) for n in BIG]
    shards = {(l, n): p[n][l].astype(BF16) for (l, n) in items}
    gathered = {(0, "w_in"): allgather_chips("ag_w_in0", [shards[0, "w_in"]])[0]}

    def gather_behind(keys):
        return ("gather", [shards[kk] for kk in keys])

    def settle(name, keys, landed):
        gathered.update(zip(keys, ag_forward(name, landed)))

    wfull = []

    inv_freq = ROPE_THETA ** (-jnp.arange(0, 64, 2, dtype=F32) / 64)
    ang = positions[0].astype(F32)[:, None] * inv_freq
    z32 = jnp.zeros_like(ang)
    cos_t = jnp.concatenate([jnp.cos(ang), z32, jnp.cos(ang), z32], axis=1)
    sin_t = jnp.concatenate([-jnp.sin(ang), z32, jnp.sin(ang), z32], axis=1)

    relu2 = lambda a: jnp.square(jnp.maximum(a, 0))

    def layer_consts(l):
        row = lambda v_: v_.reshape(1, -1).astype(F32)
        return dict(
            alog=_row128(p["gdn_a_log"][l]), gdtb=_row128(p["gdn_dt_bias"][l]), gnw=row(p["gdn_norm_w"][l]),
            scb=row(p["ssm_conv_b"][l]), salog=_row128(p["ssm_a_log"][l]), sdtb=_row128(p["ssm_dt_bias"][l]),
            sdsk=_row128(p["ssm_d"][l]), snw=row(p["ssm_norm_w"][l]), sinks=_row128(p["attn_sinks"][l]))

    saved = []
    xc = x0
    for l in range(DEPTH):
        w = dict(w_in=_relayout_w_in(jnp.concatenate([gathered[l, "w_in"][s] for s in range(4)], axis=1)))
        wfull.append(w)
        k = layer_consts(l)
        md = [mod[l, j * d:(j + 1) * d].reshape(1, d) for j in range(6)]
        npre, npost = p["norm_pre_mix"][l].reshape(1, d), p["norm_post_mix"][l].reshape(1, d)
        npre2, npost2 = p["norm_pre_mlp"][l].reshape(1, d), p["norm_post_mlp"][l].reshape(1, d)
        if l == 0:
            (h,), _ = rowmap("premix0", lambda rv, cv: (f_premod(rv, cv), []), [(xc, 0, d)], [npre, md[1], md[0]],
                             [(d, BF16)])
        behind = [(0, "w_out"), (0, "w_up")] if l == 0 else [(l, "w_up")]
        proj, landed = mm_nn(f"inproj{l}", h, w["w_in"], F32, carry=gather_behind(behind))
        settle(f"ag_forward_a{l}", behind, landed)
        w["w_out"] = _relayout_w_out(gathered[l, "w_out"].reshape(-1, d))
        w["w_up"] = gathered[l, "w_up"]
        cq = conv_fwd(f"gdn_conv{l}", [(proj, O_GQKV, 1536)], gdn_conv_w[l])
        ya, gs, gt = gdn_fwd(f"gdn{l}", cq, proj, k["alog"], k["gdtb"], k["gnw"])
        cx = conv_fwd(f"ssm_conv{l}", [(proj, O_XBC, 1024)], ssm_conv_w[l])
        yb, ss = ssd_fwd(f"ssd{l}", cx, proj, k["scb"], k["salog"], k["sdtb"], k["sdsk"], k["snw"])
        yc = swa_fwd(f"swa{l}", proj, cos_t, sin_t, k["sinks"])
        cd = conv_fwd(f"sc_conv{l}", [(proj, O_CC, GW), (proj, O_CH, GW)], sc_conv_w[l])
        (ycat,), _ = rowmap(f"ycat{l}", lambda rv, cv: ([jnp.concatenate([rv[0], rv[1], rv[2], rv[3] * rv[4]], 1)], []),
                            [(ya, 0, GW), (yb, 0, GW), (yc, 0, 1024), (proj, O_CB, GW), (cd, 0, GW)], [],
                            [(YCAT, BF16)])
        y = mm_nn(f"outproj{l}", ycat, w["w_out"], F32)
        (x1, h2), _ = rowmap(f"mix_mlp{l}", lambda rv, cv: (f_res_premod(rv, cv), []), [(xc, 0, d), (y, 0, d)],
                             [npost, md[2], npre2, md[4], md[3]], [(d, F32), (d, BF16)])
        u, landed = mm_nn(f"up{l}", h2, w["w_up"], BF16, carry=gather_behind([(l, "w_down")]))
        settle(f"ag_forward_b{l}", [(l, "w_down")], landed)
        w["w_down"] = gathered[l, "w_down"].reshape(-1, d)
        if l + 1 < DEPTH:
            behind = [(l + 1, "w_in"), (l + 1, "w_out")]
            y2, landed = mm_nn(f"down{l}", u, w["w_down"], F32, a_fn=relu2, carry=gather_behind(behind))
            settle(f"ag_forward_c{l}", behind, landed)
        else:
            y2 = mm_nn(f"down{l}", u, w["w_down"], F32, a_fn=relu2)
        saved.append(dict(x=xc, h=h, proj=proj, cq=cq, gs=gs, gt=gt, cx=cx, ss=ss, cd=cd, ycat=ycat, y=y, x1=x1, h2=h2, u=u,
                          y2=y2, md=md, k=k))
        if l + 1 < DEPTH:
            (xc, h), _ = rowmap(f"mlp_mix{l}", lambda rv, cv: (f_res_premod(rv, cv), []), [(x1, 0, d), (y2, 0, d)],
                                [npost2, md[5], p["norm_pre_mix"][l + 1].reshape(1, d),
                                 mod[l + 1, d:2 * d].reshape(1, d), mod[l + 1, 0:d].reshape(1, d)],
                                [(d, F32), (d, BF16)])

    def f_loss(vals, cvals):
        x1_, y2_, tgt_ = vals
        (x2,), vjp = jax.vjp(lambda r, c: f_postres(list(r), c), [x1_, y2_], cvals)
        err = x2 - tgt_
        part = 0.5 * jnp.sum(jnp.mean(err * err, axis=-1, keepdims=True), axis=0, keepdims=True)
        dr, dc = vjp([err * (1.0 / d)])
        return list(dr), list(dc) + [jnp.broadcast_to(part, (1, LANE))]

    s = saved[-1]
    (dx1a, dy2), (dnpost2, dgate_m, loss_part) = rowmap(
        "loss_postmlp_b", f_loss, [(s["x1"], 0, d), (s["y2"], 0, d), (tgt, 0, d)],
        [p["norm_post_mlp"][DEPTH - 1].reshape(1, d), s["md"][5]], [(d, F32), (d, BF16)],
        [(1, d), (1, d), (1, LANE)])

    idx = jnp.stack([chip, ci]).astype(jnp.int32)
    g4s, recv1s, recv2s, waiting = {}, {}, {}, []

    def reduce_start(key, g):
        l_, n_ = key
        g4 = g.reshape(4, 2, g.shape[1] // 2, g.shape[2])
        r1 = sibling_swap_halves(f"rs_sibling_{n_}{l_}", [g4])[0]
        g4s[key], recv1s[key] = g4, r1
        waiting.append((key, add_half(f"rs_add_{n_}{l_}", g4, r1, idx)))

    def mm_carrying(name, *args, **kw):
        if not waiting:
            return mm_nn(name, *args, **kw)
        out, landed = mm_nn(name, *args, carry=("exchange", [pt for _, pt in waiting]), **kw)
        recv2s.update(zip([kk for kk, _ in waiting], landed))
        waiting.clear()
        return out

    gsm, dmd = {}, {}
    for l in reversed(range(DEPTH)):
        w, s = wfull[l], saved[l]
        k, md = s["k"], s["md"]
        npre, npost = p["norm_pre_mix"][l].reshape(1, d), p["norm_post_mix"][l].reshape(1, d)
        npre2, npost2 = p["norm_pre_mlp"][l].reshape(1, d), p["norm_post_mlp"][l].reshape(1, d)
        if l == DEPTH - 1:
            gsm[l, "norm_post_mlp"], dmd[l, 5] = dnpost2[0], dgate_m
        du = mm_carrying(f"down_b{l}", dy2, w["w_down"], BF16, extra=s["u"], wt=True,
                         epi=lambda r, uu: r * (2.0 * jnp.maximum(uu.astype(F32), 0.0)))
        reduce_start((l, "w_down"), mm_tn(f"down_g{l}", s["u"], dy2, a_fn=relu2).reshape(4, -1, d))
        dh2 = mm_carrying(f"up_b{l}", du, w["w_up"], F32, wt=True)
        reduce_start((l, "w_up"), mm_tn(f"up_g{l}", s["h2"], du, col_slabs=4))
        (dxa, dy), (dnpost, dgate_a, dnpre2, dscale_m, dshift_m) = rowmap_vjp(
            f"mix_mlp_b{l}", f_res_premod, [(s["x"], 0, d), (s["y"], 0, d)], [npost, md[2], npre2, md[4], md[3]],
            [(dx1a, 0, d), (dh2, 0, d)], 2, [F32, BF16])
        dycat = mm_nn(f"outproj_b{l}", dy, w["w_out"], F32, wt=True)
        reduce_start((l, "w_out"), _unlayout_g_out(mm_tn(f"outproj_g{l}", s["ycat"], dy)).reshape(4, -1, d))
        (dcb, dcd), _ = rowmap(f"scgate_b{l}", lambda rv, cv: ([rv[0] * rv[2], rv[0] * rv[1]], []),
                               [(dycat, 2048, GW), (s["proj"], O_CB, GW), (s["cd"], 0, GW)], [],
                               [(GW, BF16), (GW, F32)])
        (dcc, dch), g_scw = conv_bwd(f"sc_conv_b{l}", dcd, [(s["proj"], O_CC, GW), (s["proj"], O_CH, GW)],
                                     sc_conv_w[l], BF16)
        daq, dak, dav, g_sinks = swa_bwd(f"swa_b{l}", dycat, 1024, s["proj"], cos_t, sin_t, k["sinks"])
        dcx, dsz, dsdt, g_scb, g_salog, g_sdtb, g_sdsk, g_snw = ssd_bwd(
            f"ssd_b{l}", dycat, 512, s["cx"], s["proj"], s["ss"], k["scb"], k["salog"], k["sdtb"], k["sdsk"], k["snw"])
        (dxbc,), g_ssmw = conv_bwd(f"ssm_conv_b{l}", dcx, [(s["proj"], O_XBC, 1024)], ssm_conv_w[l], BF16)
        dcq, dgz, dgba, g_alog, g_gdtb, g_gnw = gdn_bwd(f"gdn_b{l}", dycat, 0, s["cq"], s["proj"], s["gs"], s["gt"],
                                                        k["alog"], k["gdtb"], k["gnw"])
        (dgqkv,), g_gdnw = conv_bwd(f"gdn_conv_b{l}", dcq, [(s["proj"], O_GQKV, 1536)], gdn_conv_w[l], BF16)
        dproj = jnp.concatenate([dgqkv, dcb, dcc, dch, dxbc, daq, dgz, dsz, dak, dav, dgba, dsdt], axis=1)
        def start_w_in():
            g_in = mm_tn(f"inproj_g{l}", s["h"], dproj)
            reduce_start((l, "w_in"), jnp.stack(jnp.split(_unlayout_g_in(g_in), 4, axis=1), axis=0))

        if l == 0:
            start_w_in()
        dh = mm_carrying(f"inproj_b{l}", dproj, w["w_in"], F32, wt=True)
        if l > 0:
            start_w_in()
            sp = saved[l - 1]
            (dx1a, dy2), (dnpost2, dgate_m, dnpre, dscale_a, dshift_a) = rowmap_vjp(
                f"mlp_mix_b{l - 1}", f_res_premod, [(sp["x1"], 0, d), (sp["y2"], 0, d)],
                [p["norm_post_mlp"][l - 1].reshape(1, d), sp["md"][5], npre, md[1], md[0]],
                [(dxa, 0, d), (dh, 0, d)], 2, [F32, BF16])
            gsm[l - 1, "norm_post_mlp"], dmd[l - 1, 5] = dnpost2[0], dgate_m
        else:
            (dx,), (dnpre, dscale_a, dshift_a) = rowmap_vjp("premix_b0", f_premod, [(s["x"], 0, d)],
                                                            [npre, md[1], md[0]], [(dh, 0, d)], 1, [F32],
                                                            add_rows=[(dxa, 0, d)])
        dmd.update({(l, 0): dshift_a, (l, 1): dscale_a, (l, 2): dgate_a, (l, 3): dshift_m, (l, 4): dscale_m})
        gsm.update({(l, n_): g_ for n_, g_ in dict(
            norm_pre_mix=dnpre[0], norm_post_mix=dnpost[0], norm_pre_mlp=dnpre2[0],
            gdn_a_log=g_alog[0, :4], gdn_dt_bias=g_gdtb[0, :4], gdn_norm_w=g_gnw[0], ssm_conv_b=g_scb[0],
            ssm_a_log=g_salog[0, :8], ssm_dt_bias=g_sdtb[0, :8], ssm_d=g_sdsk[0, :8], ssm_norm_w=g_snw[0],
            attn_sinks=g_sinks[0, :8], gdn_conv_w=g_gdnw, ssm_conv_w=g_ssmw, sc_conv_w=g_scw).items()})
    grad_x = dx[None]
    dmods = [jnp.concatenate([dmd[l, j] for j in range(6)], axis=1)[0] for l in range(DEPTH)]
    gsmall = [{n_: gsm[l, n_] for n_ in SMALL_REPL + SMALL_SHARD if n_ != "ada_b"} for l in range(DEPTH)]

    buf, spec = _pack([jnp.stack(dmods, axis=0)])
    got = allgather8("ag_dmod", buf)
    dmod_sum = _unpack(sum_slabs("dmod_sum", got), spec)[0]
    dmod_all = jnp.stack([_unpack(got[kk], spec)[0] for kk in range(8)], axis=1)
    g_ada_w = []
    for l in range(DEPTH):
        dm = lax.dynamic_slice(dmod_all[l], (0, chip * nsh), (8, nsh))
        g_ada_w.append(mm_tn(f"ada_g{l}", c_pad, jnp.pad(dm, ((0, 8), (0, 0))), a_fn=silu_bf))
    g_ada_w = jnp.stack(g_ada_w, axis=0)

    names = [n for n in SMALL_REPL if n != "ada_b"] + list(SMALL_SHARD)
    buf, spec = _pack([jnp.stack([gsmall[l][n] for l in range(DEPTH)], axis=0) for n in names] + [loss_part])
    tot = _unpack(sum_slabs("small_sum", allgather8("ag_small_g", buf)), spec)
    gfull = dict(zip(names, tot[:-1]))
    loss = tot[-1][0, 0]
    grads = {"ada_w": g_ada_w, "ada_b": dmod_sum}
    for n in SMALL_REPL:
        if n != "ada_b":
            grads[n] = gfull[n]
    for n in SMALL_SHARD:
        wdt = p[n].shape[2]
        grads[n] = lax.dynamic_slice_in_dim(gfull[n], chip * wdt, wdt, axis=2)

    if waiting:
        recv2s.update(zip([kk for kk, _ in waiting], chips_exchange("rs_chips_last", [pt for _, pt in waiting])))
    half = []
    for n in BIG:
        acc = None
        for l in range(DEPTH):
            acc = rs_sum(f"rs_sum_{n}{l}", g4s[l, n], recv1s[l, n], recv2s[l, n], idx, l, acc)
        half.append(acc)
    grads.update(zip(BIG, sibling_fill("rs_join", half)))

    delta, new_m, new_v = {}, {}, {}
    for n in ("ada_w",) + BIG:
        grads[n], delta[n], new_m[n], new_v[n] = adamw_layers(f"adamw_{n}", p[n], grads[n], m[n], v[n])
    small = list(SMALL_REPL) + list(SMALL_SHARD)
    bufs = []
    for src in (p, grads, m, v):
        b_, spec = _pack([src[n] for n in small])
        bufs.append(b_)
    dl, m2, v2 = adamw("adamw_small", *bufs)
    for n, a, b_, cc_ in zip(small, _unpack(dl, spec), _unpack(m2, spec), _unpack(v2, spec)):
        delta[n], new_m[n], new_v[n] = a, b_, cc_

    return (loss, grad_x, *[grads[n] for n in WEIGHTS], *[delta[n] for n in WEIGHTS], *[new_m[n] for n in WEIGHTS],
            *[new_v[n] for n in WEIGHTS])


def kernel(x, c, positions, ada_w, ada_b, norm_pre_mix, norm_post_mix, norm_pre_mlp, norm_post_mlp, w_in, w_out, gdn_conv_w, gdn_a_log, gdn_dt_bias, gdn_norm_w, ssm_conv_w, ssm_conv_b, ssm_a_log, ssm_dt_bias, ssm_d, ssm_norm_w, attn_sinks, sc_conv_w, w_up, w_down, loss_target, m_ada_w, m_ada_b, m_norm_pre_mix, m_norm_post_mix, m_norm_pre_mlp, m_norm_post_mlp, m_w_in, m_w_out, m_gdn_conv_w, m_gdn_a_log, m_gdn_dt_bias, m_gdn_norm_w, m_ssm_conv_w, m_ssm_conv_b, m_ssm_a_log, m_ssm_dt_bias, m_ssm_d, m_ssm_norm_w, m_attn_sinks, m_sc_conv_w, m_w_up, m_w_down, v_ada_w, v_ada_b, v_norm_pre_mix, v_norm_post_mix, v_norm_pre_mlp, v_norm_post_mlp, v_w_in, v_w_out, v_gdn_conv_w, v_gdn_a_log, v_gdn_dt_bias, v_gdn_norm_w, v_ssm_conv_w, v_ssm_conv_b, v_ssm_a_log, v_ssm_dt_bias, v_ssm_d, v_ssm_norm_w, v_attn_sinks, v_sc_conv_w, v_w_up, v_w_down):
    loc = locals()
    p = {n: loc[n] for n in WEIGHTS}
    m = {n: loc["m_" + n] for n in WEIGHTS}
    v = {n: loc["v_" + n] for n in WEIGHTS}
    return _step(p, m, v, x, c, positions, loss_target)
```

```python
import functools

import numpy as np
import jax
import jax.numpy as jnp
from jax import lax
from jax.experimental import pallas as pl
from jax.experimental.pallas import tpu as pltpu

F32, BF16 = jnp.float32, jnp.bfloat16
MESH = pl.DeviceIdType.MESH

DEPTH = 2
GW = 512
N_GDN, GDN_CS = 4, 64
SSD_CS, ATT_W = 128, 128
EPS = 1e-6
ROPE_THETA = 10000.0
IN_WIDTH = 5904
B1, B2, LR, AEPS, WD, STEP = 0.9, 0.999, 0.001, 1e-8, 0.01, 10

VMEM_LIMIT = 52 * 1024 * 1024
LANE = 128

O_GQKV, O_CB, O_CC, O_CH, O_XBC, O_AQ, O_GZ, O_SZ, O_AK, O_AV, O_GBA, O_SDT = (
    0, 1536, 2048, 2560, 3072, 4096, 5120, 5632, 6144, 6400, 6656, 6784)
PW = 6912
YCAT = 2560


def _cparams(*sem):
    return pltpu.CompilerParams(dimension_semantics=sem or None, vmem_limit_bytes=VMEM_LIMIT)


def _pick(n, cap, mult):
    if n <= cap:
        return n
    best = None
    for d in range(mult, cap + 1, mult):
        if n % d == 0:
            best = d
    assert best is not None, (n, cap, mult)
    return best


def _pieces(x, n):
    out, r = [], x
    for i in range(n):
        h = r.astype(BF16)
        out.append(h)
        if i < n - 1:
            r = r - h.astype(F32)
    return out


def _dgp(a, b, ca, cb, na, nb):
    dn = (((ca,), (cb,)), ((), ()))
    acc = None
    for i, pa in enumerate(_pieces(a, na)):
        for j, pb in enumerate(_pieces(b, nb)):
            if i + j < max(na, nb):
                t = lax.dot_general(pa, pb, dn, preferred_element_type=F32)
                acc = t if acc is None else acc + t
    return acc


_CONTRACT = {"nn": (1, 0), "nt": (1, 1), "tn": (0, 0)}


@functools.lru_cache(maxsize=None)
def _mm(kind, na, nb, const_a=False):
    ca, cb = _CONTRACT[kind]
    ng = max(na, nb)

    @jax.custom_vjp
    def f(a, b):
        return _dgp(a, b, ca, cb, na, nb)

    def bwd(r, g):
        a, b = r
        if kind == "nn":
            da, db = (lambda: _mm("nt", ng, nb)(g, b)), _mm("tn", na, ng, const_a)(a, g)
        elif kind == "nt":
            da, db = (lambda: _mm("nn", ng, nb)(g, b)), _mm("tn", ng, na)(g, a)
        else:
            da, db = (lambda: _mm("nt", nb, ng)(b, g)), _mm("nn", na, ng, const_a)(a, g)
        return (jnp.zeros_like(a) if const_a else da()), db

    f.defvjp(lambda a, b: (f(a, b), (a, b)), bwd)
    return f


_nn, _nt, _tn = _mm("nn", 1, 1), _mm("nt", 1, 1), _mm("tn", 1, 1)
_hnn, _hnt, _htn = _mm("nn", 2, 2), _mm("nt", 2, 2), _mm("tn", 2, 2)
_xnn = _mm("nn", 1, 3, True)


def _silu(x):
    return x * jax.nn.sigmoid(x)


def _softplus(x):
    return jnp.maximum(x, 0.0) + jnp.log1p(jnp.exp(-jnp.abs(x)))


def _colsel(v, idx):
    lane = lax.broadcasted_iota(jnp.int32, v.shape, 1)
    return jnp.sum(jnp.where(lane == idx, v, 0.0), axis=1, keepdims=True)


def _tri(n):
    r = lax.broadcasted_iota(jnp.int32, (n, n), 0)
    c = lax.broadcasted_iota(jnp.int32, (n, n), 1)
    return r >= c, r > c


def _rowsel(v, idx):
    row = lax.broadcasted_iota(jnp.int32, v.shape, 0)
    return jnp.sum(jnp.where(row == idx, v, 0.0), axis=0, keepdims=True)


def _tri_inv(m, size, nil):
    r = lax.broadcasted_iota(jnp.int32, (size, size), 0)
    c = lax.broadcasted_iota(jnp.int32, (size, size), 1)
    x = jnp.where(r == c, 1.0, 0.0) - m
    p = _hnn(m, m)
    steps = int(np.log2(nil)) - 1
    for s in range(steps):
        x = x + _hnn(x, p)
        if s < steps - 1:
            p = _hnn(p, p)
    return x


@jax.custom_vjp
def _inv_given(m, tinv):
    return tinv


_inv_given.defvjp(lambda m, tinv: (tinv, tinv),
                  lambda tinv, g: (-_hnt(_htn(tinv, g), tinv), jnp.zeros_like(tinv)))


def _gdn_chunk(cq, z, ba, s, alog, dtb, nw, tinv_saved=None):
    n, nh = GDN_CS, N_GDN
    rr = n * nh

    def stack(a, off):
        return jnp.concatenate([a[:, off + h * LANE:off + (h + 1) * LANE] for h in range(nh)], axis=0)

    def percol(f):
        return jnp.concatenate([f(h) for h in range(nh)], axis=0)

    def own(a):
        return jnp.concatenate([a[h * n:(h + 1) * n, h * LANE:(h + 1) * LANE] for h in range(nh)], axis=0)

    q, k, v = _silu(stack(cq, 0)), _silu(stack(cq, GW)), _silu(stack(cq, 2 * GW))
    q = q * lax.rsqrt(jnp.sum(q * q, -1, keepdims=True) + EPS) * (128 ** -0.5)
    k = k * lax.rsqrt(jnp.sum(k * k, -1, keepdims=True) + EPS)
    beta = jax.nn.sigmoid(percol(lambda h: _colsel(ba, h)))
    g = percol(lambda h: -jnp.exp(_colsel(alog, h)) * _softplus(_colsel(ba, nh + h) + _colsel(dtb, h)))
    ri = lax.broadcasted_iota(jnp.int32, (rr, rr), 0)
    ci = lax.broadcasted_iota(jnp.int32, (rr, rr), 1)
    same = (ri >> 6) == (ci >> 6)
    incl, strict = same & (ri >= ci), same & (ri > ci)
    gcb = _xnn(incl.astype(F32), jnp.broadcast_to(g, (rr, LANE)))
    gci = jnp.concatenate([gcb, gcb], axis=1)
    dec = jnp.where(incl, jnp.exp(jnp.where(incl, gci - gci.T, 0.0)), 0.0)
    kb = k * beta
    kk = _nt(jnp.concatenate([kb, q], axis=0), k)
    m = jnp.where(strict, kk[:rr] * dec, 0.0)
    attn = jnp.where(incl, kk[rr:] * dec, 0.0)
    tinv = _tri_inv(m, rr, n) if tinv_saved is None else _inv_given(m, tinv_saved)
    egc = jnp.exp(gcb)
    uw = _nn(tinv, jnp.concatenate([v * beta, kb * egc], axis=1))
    u, w = uw[:, :LANE], uw[:, LANE:]
    gl = [jnp.sum(g[h * n:(h + 1) * n], axis=0, keepdims=True) for h in range(nh)]
    kd = k * jnp.exp(percol(lambda h: jnp.broadcast_to(gl[h], (n, 1))) - gcb)
    ws = _nn(jnp.concatenate([w, q * egc], axis=0), s)
    v_new = u - own(ws[:rr])
    o = own(ws[rr:]) + _nn(attn, v_new)
    rowh = lax.broadcasted_iota(jnp.int32, (rr, LANE), 0) >> 6
    vexp = jnp.concatenate([jnp.where(rowh == h, v_new, 0.0) for h in range(nh)], axis=1)
    cd = jnp.concatenate([jnp.broadcast_to(jnp.exp(gl[h]), (1, LANE)) for h in range(nh)], axis=1)
    s_new = s * cd + _tn(kd, vexp)
    y = o * lax.rsqrt(jnp.mean(o * o, -1, keepdims=True) + EPS) * nw * _silu(stack(z, 0))
    y = jnp.concatenate([y[h * n:(h + 1) * n] for h in range(nh)], axis=1)
    return (y, s_new, tinv) if tinv_saved is None else (y, s_new)


def _ssd_chunk(cx, z, dtc, s, convb, alog, dtb, dsk, nw):
    n = SSD_CS
    incl, _ = _tri(n)
    xbc = _silu(cx + convb)
    dt_all = _softplus(dtc + dtb)
    da_all = dt_all * (-jnp.exp(alog))
    cs_all = _xnn(incl.astype(F32), da_all)
    cs_t = cs_all.T
    tot_all = jnp.sum(da_all, axis=0, keepdims=True)
    lane_lo = lax.broadcasted_iota(jnp.int32, (n, LANE), 1) < 64
    ys, s_out = [], []
    for pi in range(4):
        g = pi // 2
        bm = xbc[:, GW + g * LANE:GW + (g + 1) * LANE]
        cm = xbc[:, GW + 256 + g * LANE:GW + 256 + (g + 1) * LANE]
        cbm = _nt(cm, bm)
        xp, s_p = xbc[:, pi * LANE:(pi + 1) * LANE], s[:, pi * LANE:(pi + 1) * LANE]
        a_rows, b_cols, dts, dsks, etot = [], [], [], [], []
        for h in (2 * pi, 2 * pi + 1):
            cs_i = _colsel(cs_all, h)
            lm = jnp.where(incl, jnp.exp(jnp.where(incl, cs_i - _rowsel(cs_t, h), 0.0)), 0.0)
            tot = _colsel(tot_all, h)
            a_rows.append(jnp.concatenate([cbm * lm, cm * jnp.exp(cs_i)], axis=1))
            b_cols.append(bm * jnp.exp(tot - cs_i))
            dts.append(_colsel(dt_all, h))
            dsks.append(_colsel(dsk, h))
            etot.append(jnp.exp(tot))
        xdt = xp * jnp.where(lane_lo, dts[0], dts[1])
        yy = _nn(jnp.concatenate(a_rows, axis=0), jnp.concatenate([xdt, s_p], axis=0))
        ys.append(jnp.where(lane_lo, yy[:n], yy[n:]) + xp * jnp.where(lane_lo[:1], dsks[0], dsks[1]))
        st = _tn(jnp.concatenate(b_cols, axis=1), xdt)
        s_out.append(s_p * jnp.where(lane_lo[:1], etot[0], etot[1]) + jnp.where(lane_lo, st[:n], st[n:]))
    zs = _silu(z)
    out = []
    for g in range(2):
        y = jnp.concatenate(ys[2 * g:2 * g + 2], axis=1) * zs[:, g * 256:(g + 1) * 256]
        out.append(y * lax.rsqrt(jnp.mean(y * y, -1, keepdims=True) + EPS) * nw[:, g * 256:(g + 1) * 256])
    return jnp.concatenate(out, axis=1), jnp.concatenate(s_out, axis=1)


SWA_STACK = 1


def _swa_block(qg, kp, kc, vp, vc, sink, notfirst, *, g):
    w, ns = ATT_W, SWA_STACK
    kb = jnp.concatenate([kp, kc], axis=0)
    vb = jnp.concatenate([vp, vc], axis=0)
    qi = lax.broadcasted_iota(jnp.int32, (ns * w, 2 * w), 0) & (w - 1)
    kj = lax.broadcasted_iota(jnp.int32, (ns * w, 2 * w), 1)
    rel = qi + w - kj
    mask = (rel >= 0) & (rel < w) & ((kj >= w) | (notfirst > 0.5))
    outs = []
    for r0 in range(0, 4, ns):
        q = jnp.concatenate([qg[:, r * LANE:(r + 1) * LANE] for r in range(r0, r0 + ns)], axis=0)
        snk = jnp.concatenate([jnp.broadcast_to(_colsel(sink, 4 * g + r), (w, 1)) for r in range(r0, r0 + ns)], axis=0)
        s = jnp.where(mask, _nt(q, kb) * (64 ** -0.5), -1e30)
        mx = lax.stop_gradient(jnp.maximum(jnp.max(s, axis=-1, keepdims=True), snk))
        p = jnp.where(mask, jnp.exp(s - mx), 0.0)
        p = p / (jnp.sum(p, axis=-1, keepdims=True) + jnp.exp(snk - mx))
        o = _nn(p, vb)
        outs += [o[j * w:(j + 1) * w] for j in range(ns)]
    return jnp.concatenate(outs, axis=1)


def _rope(x, cos, sin):
    return x * cos + pltpu.roll(x, 64, 1) * sin


def _unrope(dy, cos, sin):
    return dy * cos + pltpu.roll(dy * sin, 64, 1)


def rowmap(name, fn, rows, consts, outs, accs=(), tile=256):
    t = rows[0][0].shape[0]
    tile = min(tile, t)
    assert t % tile == 0
    nr, nc, no, na = len(rows), len(consts), len(outs), len(accs)

    def body(*refs):
        rv = [r[...] for r in refs[:nr]]
        cv = [r[...] for r in refs[nr:nr + nc]]
        ov, av = fn(rv, cv)
        for ref, o in zip(refs[nr + nc:nr + nc + no], ov):
            ref[...] = o.astype(ref.dtype)
        if na:
            a_refs = refs[nr + nc + no:]

            @pl.when(pl.program_id(0) == 0)
            def _():
                for ref in a_refs:
                    ref[...] = jnp.zeros_like(ref)

            for ref, a in zip(a_refs, av):
                ref[...] += a

    def cspec(off, width):
        assert off % width == 0
        cb = off // width
        return pl.BlockSpec((tile, width), lambda i: (i, cb))

    in_specs = [cspec(off, wd) for (_, off, wd) in rows]
    in_specs += [pl.BlockSpec(c.shape, lambda i: (0, 0)) for c in consts]
    out_specs = [pl.BlockSpec((tile, wd), lambda i: (i, 0)) for (wd, _) in outs]
    out_specs += [pl.BlockSpec(s, lambda i: (0, 0)) for s in accs]
    out_shape = [jax.ShapeDtypeStruct((t, wd), dt) for (wd, dt) in outs]
    out_shape += [jax.ShapeDtypeStruct(s, F32) for s in accs]
    res = pl.pallas_call(
        body, name=name, grid=(t // tile,), in_specs=in_specs, out_specs=out_specs, out_shape=out_shape,
        compiler_params=_cparams("arbitrary"),
    )(*[r[0] for r in rows], *consts)
    return res[:no], res[no:]


def rowmap_vjp(name, f, rows, consts, douts, n_diff, drow_dtypes, tile=256, add_rows=()):
    nr, nd, nadd = len(rows), len(douts), len(add_rows)

    def g(vals, cvals):
        rv, dov, addv = vals[:nr], vals[nr:nr + nd], vals[nr + nd:]
        fixed = rv[n_diff:]
        _, vjp = jax.vjp(lambda r, c: f(list(r) + list(fixed), c), rv[:n_diff], cvals)
        dr, dc = vjp([d.astype(F32) for d in dov])
        dr = list(dr)
        for a in addv:
            dr[0] = dr[0] + a
        return dr, dc

    outs = [(rows[i][2], drow_dtypes[i]) for i in range(n_diff)]
    accs = [c.shape for c in consts]
    return rowmap(name, g, list(rows) + list(douts) + list(add_rows), consts, outs, accs, tile=tile)


def _carried_copies(kind, in_refs, out_refs, send_sems, recv_sems):
    x, y, c = _me()
    chip = 2 * x + y
    chips = [(1 - x, y), (x, 1 - y), (1 - x, 1 - y)]
    cps = []

    def rc(src, dst, k, to):
        return pltpu.make_async_remote_copy(src_ref=src, dst_ref=dst, send_sem=send_sems.at[k], recv_sem=recv_sems.at[k],
                                            device_id=to, device_id_type=MESH)

    for i, (src, dst) in enumerate(zip(in_refs, out_refs)):
        if kind == "gather":
            rh = src.shape[0] // 2
            half = pl.ds(pl.multiple_of(c * rh, 16), rh)
            cps.append(rc(src, dst.at[chip], 4 * i + 3, (x, y, 1 - c)))
            cps += [rc(src.at[half], dst.at[chip, half], 4 * i + k, (cx, cy, c)) for k, (cx, cy) in enumerate(chips)]
        else:
            cps += [rc(src.at[2 * cx + cy], dst.at[chip], 4 * i + k, (cx, cy, c)) for k, (cx, cy) in enumerate(chips)]
    return cps


def mm_nn(name, a, w, out_dtype, a_fn=None, extra=None, epi=None, tm_cap=1024, wt=False, carry=None):
    nc = len(carry[1]) if carry else 0
    m, k = a.shape
    stacked = w.ndim == 3
    n = (w.shape[-2] if wt else w.shape[-1] * (w.shape[0] if stacked else 1))
    slab = w.shape[-1]
    tm = _pick(m, tm_cap, 16)
    tn = _pick(n if wt else slab, 1024, LANE)
    tk = _pick(slab if wt else k, 2048, LANE)
    nk = k // tk
    per = slab // (tk if wt else tn)
    wdims = ((1,), (1,)) if wt else ((1,), (0,))

    gi, gj = m // tm, n // tn
    n_in = 2 + (extra is not None)

    def body(*refs):
        a_ref, w_ref = refs[0], refs[1]
        e_ref = refs[2] if extra is not None else None
        o_ref, acc = refs[n_in + nc], refs[n_in + 2 * nc + 1]
        kk = pl.program_id(2)
        if nc:
            def copies():
                return _carried_copies(carry[0], refs[n_in:n_in + nc], refs[n_in + nc + 1:n_in + 2 * nc + 1],
                                       refs[-2], refs[-1])

            @pl.when((pl.program_id(0) == 0) & (pl.program_id(1) == 0) & (kk == 0))
            def _():
                for cp in copies():
                    cp.start()

        @pl.when(kk == 0)
        def _():
            acc[...] = jnp.zeros_like(acc)

        av = a_ref[...]
        if a_fn is not None:
            av = a_fn(av)
        acc[...] += lax.dot_general(av.astype(BF16), w_ref[...].astype(BF16), (wdims, ((), ())),
                                    preferred_element_type=F32)

        @pl.when(kk == nk - 1)
        def _():
            r = acc[...]
            if epi is not None:
                r = epi(r, e_ref[...])
            o_ref[...] = r.astype(o_ref.dtype)

        if nc:
            @pl.when((pl.program_id(0) == gi - 1) & (pl.program_id(1) == gj - 1) & (kk == nk - 1))
            def _():
                cps = copies()
                for cp in cps:
                    cp.wait_recv()
                for cp in cps:
                    cp.wait_send()

    if wt:
        w_spec = (pl.BlockSpec((None, tn, tk), lambda i, j, kk: (kk // per, j, kk % per)) if stacked
                  else pl.BlockSpec((tn, tk), lambda i, j, kk: (j, kk)))
    else:
        w_spec = (pl.BlockSpec((None, tk, tn), lambda i, j, kk: (j // per, kk, j % per)) if stacked
                  else pl.BlockSpec((tk, tn), lambda i, j, kk: (kk, j)))
    in_specs = [pl.BlockSpec((tm, tk), lambda i, j, kk: (i, kk)), w_spec]
    ops = [a, w]
    if extra is not None:
        in_specs.append(pl.BlockSpec((tm, tn), lambda i, j, kk: (i, j)))
        ops.append(extra)
    out_spec = pl.BlockSpec((tm, tn), lambda i, j, kk: (i, j))
    out_shape = jax.ShapeDtypeStruct((m, n), out_dtype)
    if not nc:
        return pl.pallas_call(
            body, name=name, grid=(gi, gj, nk), in_specs=in_specs, out_specs=out_spec, out_shape=out_shape,
            scratch_shapes=[pltpu.VMEM((tm, tn), F32)],
            compiler_params=_cparams("parallel", "parallel", "arbitrary"),
        )(*ops)
    landing = [jax.ShapeDtypeStruct((4,) + c.shape if carry[0] == "gather" else c.shape, c.dtype) for c in carry[1]]
    res = pl.pallas_call(
        body, name=name, grid=(gi, gj, nk), in_specs=in_specs + [ANY] * nc, out_specs=[out_spec] + [ANY] * nc,
        out_shape=[out_shape] + landing, scratch_shapes=[pltpu.VMEM((tm, tn), F32)] + _sems(4 * nc),
        compiler_params=_cparams("arbitrary", "arbitrary", "arbitrary"),
    )(*ops, *carry[1])
    return res[0], list(res[1:])


def mm_tn(name, a, b, a_fn=None, col_slabs=1, carry=None):
    nc = len(carry[1]) if carry else 0
    t, m = a.shape
    n = b.shape[1]
    tm, tn, tk = _pick(m, 1024, LANE), _pick(n // col_slabs, 1024, LANE), _pick(t, 2048, 16)
    nk = t // tk
    per = n // col_slabs // tn
    gi, gj = m // tm, n // tn

    def body(*refs):
        a_ref, b_ref, o_ref, acc = refs[0], refs[1], refs[2 + nc], refs[3 + 2 * nc]
        kk = pl.program_id(2)
        if nc:
            def copies():
                return _carried_copies(carry[0], refs[2:2 + nc], refs[3 + nc:3 + 2 * nc], refs[-2], refs[-1])

            @pl.when((pl.program_id(0) == 0) & (pl.program_id(1) == 0) & (kk == 0))
            def _():
                for cp in copies():
                    cp.start()

        @pl.when(kk == 0)
        def _():
            acc[...] = jnp.zeros_like(acc)

        av = a_ref[...]
        if a_fn is not None:
            av = a_fn(av)
        acc[...] += lax.dot_general(av.astype(BF16), b_ref[...].astype(BF16), (((0,), (0,)), ((), ())),
                                    preferred_element_type=F32)

        @pl.when(kk == nk - 1)
        def _():
            o_ref[...] = acc[...]

        if nc:
            @pl.when((pl.program_id(0) == gi - 1) & (pl.program_id(1) == gj - 1) & (kk == nk - 1))
            def _():
                cps = copies()
                for cp in cps:
                    cp.wait_recv()
                for cp in cps:
                    cp.wait_send()

    in_specs = [pl.BlockSpec((tk, tm), lambda i, j, kk: (kk, i)), pl.BlockSpec((tk, tn), lambda i, j, kk: (kk, j))]
    out_spec = (pl.BlockSpec((tm, tn), lambda i, j, kk: (i, j)) if col_slabs == 1
                else pl.BlockSpec((None, tm, tn), lambda i, j, kk: (j // per, i, j % per)))
    out_shape = jax.ShapeDtypeStruct((m, n) if col_slabs == 1 else (col_slabs, m, n // col_slabs), F32)
    if not nc:
        return pl.pallas_call(
            body, name=name, grid=(gi, gj, nk), in_specs=in_specs, out_specs=out_spec, out_shape=out_shape,
            scratch_shapes=[pltpu.VMEM((tm, tn), F32)],
            compiler_params=_cparams("parallel", "parallel", "arbitrary"),
        )(a, b)
    res = pl.pallas_call(
        body, name=name, grid=(gi, gj, nk), in_specs=in_specs + [ANY] * nc, out_specs=[out_spec] + [ANY] * nc,
        out_shape=[out_shape] + [jax.ShapeDtypeStruct(c.shape, c.dtype) for c in carry[1]],
        scratch_shapes=[pltpu.VMEM((tm, tn), F32)] + _sems(4 * nc),
        compiler_params=_cparams("arbitrary", "arbitrary", "arbitrary"),
    )(a, b, *carry[1])
    return res[0], list(res[1:])


CONV_TILE = 512
HALO = 8


def conv_fwd(name, xs, w):
    t = xs[0][0].shape[0]
    c = xs[0][2]
    kw = w.shape[0]
    tt = min(CONV_TILE, t)
    nx = len(xs)

    def body(*refs):
        cur_refs, prev_refs, w_ref, y_ref = refs[:nx], refs[nx:2 * nx], refs[2 * nx], refs[2 * nx + 1]
        i = pl.program_id(0)
        cur, prev = cur_refs[0][...], prev_refs[0][...]
        for r, p in zip(cur_refs[1:], prev_refs[1:]):
            cur, prev = cur * r[...], prev * p[...]
        prev = jnp.where(i > 0, prev, 0.0)
        xx = jnp.concatenate([prev, cur], axis=0)
        acc = jnp.zeros((tt, c), F32)
        for k in range(kw):
            sh = kw - 1 - k
            rolled = pltpu.roll(xx, sh, 0) if sh else xx
            acc = acc + rolled[HALO:] * w_ref[k:k + 1, :]
        y_ref[...] = acc

    def cur_spec(off):
        cb = off // c
        return pl.BlockSpec((tt, c), lambda i: (i, cb))

    def prev_spec(off):
        cb = off // c
        return pl.BlockSpec((HALO, c), lambda i: (jnp.maximum(i * (tt // HALO) - 1, 0), cb))

    for (_, off, wd) in xs:
        assert wd == c and off % c == 0
    return pl.pallas_call(
        body, name=name, grid=(t // tt,),
        in_specs=[cur_spec(off) for (_, off, _) in xs] + [prev_spec(off) for (_, off, _) in xs]
        + [pl.BlockSpec(w.shape, lambda i: (0, 0))],
        out_specs=pl.BlockSpec((tt, c), lambda i: (i, 0)),
        out_shape=jax.ShapeDtypeStruct((t, c), F32),
        compiler_params=_cparams("arbitrary"),
    )(*[x[0] for x in xs], *[x[0] for x in xs], w)


def conv_bwd(name, dy, xs, w, dx_dtype):
    t = xs[0][0].shape[0]
    c = xs[0][2]
    kw = w.shape[0]
    tt = min(CONV_TILE, t)
    nt = t // tt
    nx = len(xs)
    n = tt + HALO

    def body(*refs):
        dy_ref = refs[0]
        cur_refs, prev_refs = refs[1:1 + nx], refs[1 + nx:1 + 2 * nx]
        w_ref = refs[1 + 2 * nx]
        dx_refs = refs[2 + 2 * nx:2 + 3 * nx]
        dw_ref, carry = refs[2 + 3 * nx], refs[3 + 3 * nx]
        i = pl.program_id(0)

        @pl.when(i == 0)
        def _():
            carry[...] = jnp.zeros_like(carry)
            dw_ref[...] = jnp.zeros_like(dw_ref)

        curs = [r[...] for r in cur_refs]
        cur, prev = curs[0], prev_refs[0][...]
        for cv, p in zip(curs[1:], prev_refs[1:]):
            cur, prev = cur * cv, prev * p[...]
        prev = jnp.where(i < nt - 1, prev, 0.0)
        xx = jnp.concatenate([prev, cur], axis=0)
        dyv = dy_ref[...]
        dpad = jnp.concatenate([jnp.zeros((HALO, c), F32), dyv], axis=0)
        dxx = jnp.zeros((n, c), F32)
        dws = []
        for k in range(kw):
            sh = kw - 1 - k
            dxx = dxx + (pltpu.roll(dpad, n - sh, 0) if sh else dpad) * w_ref[k:k + 1, :]
            rolled = pltpu.roll(xx, sh, 0) if sh else xx
            dws.append(jnp.sum(dyv * rolled[HALO:], axis=0, keepdims=True))
        dw_ref[...] += jnp.concatenate(dws, axis=0)
        dcur = jnp.concatenate([dxx[HALO:tt], dxx[tt:] + carry[...]], axis=0)
        carry[...] = dxx[:HALO]
        for j, ref in enumerate(dx_refs):
            d = dcur
            for jj, cv in enumerate(curs):
                if jj != j:
                    d = d * cv
            ref[...] = d.astype(ref.dtype)

    def cur_spec(off):
        cb = off // c
        return pl.BlockSpec((tt, c), lambda i: (nt - 1 - i, cb))

    def prev_spec(off):
        cb = off // c
        return pl.BlockSpec((HALO, c), lambda i: (jnp.maximum((nt - 1 - i) * (tt // HALO) - 1, 0), cb))

    res = pl.pallas_call(
        body, name=name, grid=(nt,),
        in_specs=[pl.BlockSpec((tt, c), lambda i: (nt - 1 - i, 0))] + [cur_spec(off) for (_, off, _) in xs]
        + [prev_spec(off) for (_, off, _) in xs] + [pl.BlockSpec(w.shape, lambda i: (0, 0))],
        out_specs=[pl.BlockSpec((tt, c), lambda i: (nt - 1 - i, 0)) for _ in xs] + [pl.BlockSpec(w.shape, lambda i: (0, 0))],
        out_shape=[jax.ShapeDtypeStruct((t, c), dx_dtype) for _ in xs] + [jax.ShapeDtypeStruct(w.shape, F32)],
        scratch_shapes=[pltpu.VMEM((HALO, c), F32)],
        compiler_params=_cparams("arbitrary"),
    )(dy, *[x[0] for x in xs], *[x[0] for x in xs], w)
    return res[:nx], res[nx]


def _const_spec(a):
    return pl.BlockSpec(a.shape, lambda i: (0,) * a.ndim)


GDN_STEP = 4


def _gdn_rows(ref, c):
    return ref[c * GDN_CS:(c + 1) * GDN_CS, :]


def gdn_fwd(name, cq, proj, alog, dtb, nw):
    t = cq.shape[0]
    nc = t // GDN_CS
    ns = min(GDN_STEP, nc)
    rows = ns * GDN_CS

    def body(cq_ref, z_ref, ba_ref, alog_ref, dtb_ref, nw_ref, y_ref, ssave_ref, tsave_ref, s_scr):
        @pl.when(pl.program_id(0) == 0)
        def _():
            s_scr[...] = jnp.zeros_like(s_scr)

        s = s_scr[...]
        ys, saved, tinvs = [], [], []
        for c in range(ns):
            saved.append(s)
            y, s, tinv = _gdn_chunk(_gdn_rows(cq_ref, c), _gdn_rows(z_ref, c), _gdn_rows(ba_ref, c), s, alog_ref[...],
                                    dtb_ref[...], nw_ref[...])
            ys.append(y)
            tinvs.append(tinv)
        y_ref[...] = jnp.concatenate(ys, axis=0)
        ssave_ref[...] = jnp.stack(saved, axis=0)
        tsave_ref[...] = jnp.stack(tinvs, axis=0)
        s_scr[...] = s

    return pl.pallas_call(
        body, name=name, grid=(nc // ns,),
        in_specs=[pl.BlockSpec((rows, 3 * GW), lambda i: (i, 0)),
                  pl.BlockSpec((rows, GW), lambda i: (i, O_GZ // GW)),
                  pl.BlockSpec((rows, LANE), lambda i: (i, O_GBA // LANE)),
                  _const_spec(alog), _const_spec(dtb), _const_spec(nw)],
        out_specs=[pl.BlockSpec((rows, GW), lambda i: (i, 0)),
                   pl.BlockSpec((ns, LANE, N_GDN * LANE), lambda i: (i, 0, 0)),
                   pl.BlockSpec((ns, N_GDN * GDN_CS, N_GDN * GDN_CS), lambda i: (i, 0, 0))],
        out_shape=[jax.ShapeDtypeStruct((t, GW), F32), jax.ShapeDtypeStruct((nc, LANE, N_GDN * LANE), F32),
                   jax.ShapeDtypeStruct((nc, N_GDN * GDN_CS, N_GDN * GDN_CS), F32)],
        scratch_shapes=[pltpu.VMEM((LANE, N_GDN * LANE), F32)],
        compiler_params=_cparams("arbitrary"),
    )(cq, proj, proj, alog, dtb, nw)


def gdn_bwd(name, dy, dy_off, cq, proj, ssave, tsave, alog, dtb, nw):
    t = cq.shape[0]
    nc = t // GDN_CS
    ns = min(GDN_STEP, nc)
    rows = ns * GDN_CS

    def body(dy_ref, cq_ref, z_ref, ba_ref, ssave_ref, tsave_ref, alog_ref, dtb_ref, nw_ref,
             dcq_ref, dz_ref, dba_ref, dalog_ref, ddtb_ref, dnw_ref, ds_scr):
        @pl.when(pl.program_id(0) == 0)
        def _():
            ds_scr[...] = jnp.zeros_like(ds_scr)
            dalog_ref[...] = jnp.zeros_like(dalog_ref)
            ddtb_ref[...] = jnp.zeros_like(ddtb_ref)
            dnw_ref[...] = jnp.zeros_like(dnw_ref)

        ds = ds_scr[...]
        dcq, dzs, dbas = [None] * ns, [None] * ns, [None] * ns
        dal, ddt, dnw = (jnp.zeros(r.shape, F32) for r in (dalog_ref, ddtb_ref, dnw_ref))
        for c in reversed(range(ns)):
            _, vjp = jax.vjp(functools.partial(_gdn_chunk, tinv_saved=tsave_ref[c]), _gdn_rows(cq_ref, c),
                             _gdn_rows(z_ref, c), _gdn_rows(ba_ref, c), ssave_ref[c], alog_ref[...], dtb_ref[...],
                             nw_ref[...])
            dcq[c], dzs[c], dbas[c], ds, dal_c, ddt_c, dnw_c = vjp((_gdn_rows(dy_ref, c), ds))
            dal, ddt, dnw = dal + dal_c, ddt + ddt_c, dnw + dnw_c
        dcq_ref[...] = jnp.concatenate(dcq, axis=0)
        dz_ref[...] = jnp.concatenate(dzs, axis=0).astype(dz_ref.dtype)
        dba_ref[...] = jnp.concatenate(dbas, axis=0).astype(dba_ref.dtype)
        ds_scr[...] = ds
        dalog_ref[...] += dal
        ddtb_ref[...] += ddt
        dnw_ref[...] += dnw

    rev = lambda i: nc // ns - 1 - i
    return pl.pallas_call(
        body, name=name, grid=(nc // ns,),
        in_specs=[pl.BlockSpec((rows, GW), lambda i: (rev(i), dy_off // GW)),
                  pl.BlockSpec((rows, 3 * GW), lambda i: (rev(i), 0)),
                  pl.BlockSpec((rows, GW), lambda i: (rev(i), O_GZ // GW)),
                  pl.BlockSpec((rows, LANE), lambda i: (rev(i), O_GBA // LANE)),
                  pl.BlockSpec((ns, LANE, N_GDN * LANE), lambda i: (rev(i), 0, 0)),
                  pl.BlockSpec((ns, N_GDN * GDN_CS, N_GDN * GDN_CS), lambda i: (rev(i), 0, 0)),
                  _const_spec(alog), _const_spec(dtb), _const_spec(nw)],
        out_specs=[pl.BlockSpec((rows, 3 * GW), lambda i: (rev(i), 0)),
                   pl.BlockSpec((rows, GW), lambda i: (rev(i), 0)),
                   pl.BlockSpec((rows, LANE), lambda i: (rev(i), 0)),
                   _const_spec(alog), _const_spec(dtb), _const_spec(nw)],
        out_shape=[jax.ShapeDtypeStruct((t, 3 * GW), F32), jax.ShapeDtypeStruct((t, GW), BF16),
                   jax.ShapeDtypeStruct((t, LANE), BF16), jax.ShapeDtypeStruct(alog.shape, F32),
                   jax.ShapeDtypeStruct(dtb.shape, F32), jax.ShapeDtypeStruct(nw.shape, F32)],
        scratch_shapes=[pltpu.VMEM((LANE, N_GDN * LANE), F32)],
        compiler_params=_cparams("arbitrary"),
    )(dy, cq, proj, proj, ssave, tsave, alog, dtb, nw)


SSD_STEP = 2


def _ssd_rows(ref, c):
    return ref[c * SSD_CS:(c + 1) * SSD_CS, :]


def ssd_fwd(name, cx, proj, convb, alog, dtb, dsk, nw):
    t = cx.shape[0]
    nc = t // SSD_CS
    ns = min(SSD_STEP, nc)
    rows = ns * SSD_CS

    def body(cx_ref, z_ref, dt_ref, cb_ref, alog_ref, dtb_ref, dsk_ref, nw_ref, y_ref, ssave_ref, s_scr):
        @pl.when(pl.program_id(0) == 0)
        def _():
            s_scr[...] = jnp.zeros_like(s_scr)

        s = s_scr[...]
        ys, saved = [], []
        for c in range(ns):
            saved.append(s)
            y, s = _ssd_chunk(_ssd_rows(cx_ref, c), _ssd_rows(z_ref, c), _ssd_rows(dt_ref, c), s, cb_ref[...],
                              alog_ref[...], dtb_ref[...], dsk_ref[...], nw_ref[...])
            ys.append(y)
        y_ref[...] = jnp.concatenate(ys, axis=0)
        ssave_ref[...] = jnp.stack(saved, axis=0)
        s_scr[...] = s

    return pl.pallas_call(
        body, name=name, grid=(nc // ns,),
        in_specs=[pl.BlockSpec((rows, 1024), lambda i: (i, 0)),
                  pl.BlockSpec((rows, GW), lambda i: (i, O_SZ // GW)),
                  pl.BlockSpec((rows, LANE), lambda i: (i, O_SDT // LANE)),
                  _const_spec(convb), _const_spec(alog), _const_spec(dtb), _const_spec(dsk), _const_spec(nw)],
        out_specs=[pl.BlockSpec((rows, GW), lambda i: (i, 0)),
                   pl.BlockSpec((ns, LANE, GW), lambda i: (i, 0, 0))],
        out_shape=[jax.ShapeDtypeStruct((t, GW), F32), jax.ShapeDtypeStruct((nc, LANE, GW), F32)],
        scratch_shapes=[pltpu.VMEM((LANE, GW), F32)],
        compiler_params=_cparams("arbitrary"),
    )(cx, proj, proj, convb, alog, dtb, dsk, nw)


def ssd_bwd(name, dy, dy_off, cx, proj, ssave, convb, alog, dtb, dsk, nw):
    t = cx.shape[0]
    nc = t // SSD_CS
    ns = min(SSD_STEP, nc)
    rows = ns * SSD_CS

    def body(dy_ref, cx_ref, z_ref, dt_ref, ssave_ref, cb_ref, alog_ref, dtb_ref, dsk_ref, nw_ref,
             dcx_ref, dz_ref, ddt_ref, dcb_ref, dalog_ref, ddtb_ref, ddsk_ref, dnw_ref, ds_scr):
        @pl.when(pl.program_id(0) == 0)
        def _():
            ds_scr[...] = jnp.zeros_like(ds_scr)
            for r in (dcb_ref, dalog_ref, ddtb_ref, ddsk_ref, dnw_ref):
                r[...] = jnp.zeros_like(r)

        ds = ds_scr[...]
        dcx, dz, ddt = [None] * ns, [None] * ns, [None] * ns
        acc = [jnp.zeros(r.shape, F32) for r in (dcb_ref, dalog_ref, ddtb_ref, ddsk_ref, dnw_ref)]
        for c in reversed(range(ns)):
            _, vjp = jax.vjp(_ssd_chunk, _ssd_rows(cx_ref, c), _ssd_rows(z_ref, c), _ssd_rows(dt_ref, c), ssave_ref[c],
                             cb_ref[...], alog_ref[...], dtb_ref[...], dsk_ref[...], nw_ref[...])
            dcx[c], dz[c], ddt[c], ds, *dconst = vjp((_ssd_rows(dy_ref, c), ds))
            acc = [a + g for a, g in zip(acc, dconst)]
        dcx_ref[...] = jnp.concatenate(dcx, axis=0)
        dz_ref[...] = jnp.concatenate(dz, axis=0).astype(dz_ref.dtype)
        ddt_ref[...] = jnp.concatenate(ddt, axis=0).astype(ddt_ref.dtype)
        ds_scr[...] = ds
        for r, a in zip((dcb_ref, dalog_ref, ddtb_ref, ddsk_ref, dnw_ref), acc):
            r[...] += a

    rev = lambda i: nc // ns - 1 - i
    return pl.pallas_call(
        body, name=name, grid=(nc // ns,),
        in_specs=[pl.BlockSpec((rows, GW), lambda i: (rev(i), dy_off // GW)),
                  pl.BlockSpec((rows, 1024), lambda i: (rev(i), 0)),
                  pl.BlockSpec((rows, GW), lambda i: (rev(i), O_SZ // GW)),
                  pl.BlockSpec((rows, LANE), lambda i: (rev(i), O_SDT // LANE)),
                  pl.BlockSpec((ns, LANE, GW), lambda i: (rev(i), 0, 0)),
                  _const_spec(convb), _const_spec(alog), _const_spec(dtb), _const_spec(dsk), _const_spec(nw)],
        out_specs=[pl.BlockSpec((rows, 1024), lambda i: (rev(i), 0)),
                   pl.BlockSpec((rows, GW), lambda i: (rev(i), 0)),
                   pl.BlockSpec((rows, LANE), lambda i: (rev(i), 0)),
                   _const_spec(convb), _const_spec(alog), _const_spec(dtb), _const_spec(dsk), _const_spec(nw)],
        out_shape=[jax.ShapeDtypeStruct((t, 1024), F32), jax.ShapeDtypeStruct((t, GW), BF16),
                   jax.ShapeDtypeStruct((t, LANE), BF16), jax.ShapeDtypeStruct(convb.shape, F32),
                   jax.ShapeDtypeStruct(alog.shape, F32), jax.ShapeDtypeStruct(dtb.shape, F32),
                   jax.ShapeDtypeStruct(dsk.shape, F32), jax.ShapeDtypeStruct(nw.shape, F32)],
        scratch_shapes=[pltpu.VMEM((LANE, GW), F32)],
        compiler_params=_cparams("arbitrary"),
    )(dy, cx, proj, proj, ssave, convb, alog, dtb, dsk, nw)


def _swa_specs(nb, rev):
    w = ATT_W
    cur = lambda i: (nb - 1 - i) if rev else i
    prv = lambda i: jnp.maximum(cur(i) - 1, 0)
    return [pl.BlockSpec((w, 1024), lambda i: (cur(i), O_AQ // 1024)),
            pl.BlockSpec((w, 256), lambda i: (cur(i), O_AK // 256)),
            pl.BlockSpec((w, 256), lambda i: (prv(i), O_AK // 256)),
            pl.BlockSpec((w, 256), lambda i: (cur(i), O_AV // 256)),
            pl.BlockSpec((w, 256), lambda i: (prv(i), O_AV // 256)),
            pl.BlockSpec((w, LANE), lambda i: (cur(i), 0)), pl.BlockSpec((w, LANE), lambda i: (prv(i), 0)),
            pl.BlockSpec((w, LANE), lambda i: (cur(i), 0)), pl.BlockSpec((w, LANE), lambda i: (prv(i), 0))]


def swa_fwd(name, proj, cos, sin, sinks):
    t = proj.shape[0]
    nb = t // ATT_W

    def body(q_ref, kc_ref, kp_ref, vc_ref, vp_ref, cc_ref, cp_ref, sc_ref, sp_ref, snk_ref, y_ref):
        notfirst = jnp.where(pl.program_id(0) > 0, 1.0, 0.0).astype(F32)
        cc, cp, sc, sp = cc_ref[...], cp_ref[...], sc_ref[...], sp_ref[...]
        outs = []
        for g in range(2):
            sl = slice(g * LANE, (g + 1) * LANE)
            qg = jnp.concatenate([_rope(q_ref[:, (4 * g + r) * LANE:(4 * g + r + 1) * LANE], cc, sc)
                                  for r in range(4)], axis=1)
            outs.append(_swa_block(qg, _rope(kp_ref[:, sl], cp, sp), _rope(kc_ref[:, sl], cc, sc), vp_ref[:, sl],
                                   vc_ref[:, sl], snk_ref[...], notfirst, g=g))
        y_ref[...] = jnp.concatenate(outs, axis=1)

    return pl.pallas_call(
        body, name=name, grid=(nb,),
        in_specs=_swa_specs(nb, False) + [_const_spec(sinks)],
        out_specs=pl.BlockSpec((ATT_W, 1024), lambda i: (i, 0)),
        out_shape=jax.ShapeDtypeStruct((t, 1024), F32),
        compiler_params=_cparams("arbitrary"),
    )(proj, proj, proj, proj, proj, cos, cos, sin, sin, sinks)


def swa_bwd(name, dy, dy_off, proj, cos, sin, sinks):
    t = proj.shape[0]
    nb = t // ATT_W

    def body(dy_ref, q_ref, kc_ref, kp_ref, vc_ref, vp_ref, cc_ref, cp_ref, sc_ref, sp_ref, snk_ref,
             dq_ref, dk_ref, dv_ref, dsnk_ref, ck_scr, cv_scr):
        i = pl.program_id(0)

        @pl.when(i == 0)
        def _():
            ck_scr[...] = jnp.zeros_like(ck_scr)
            cv_scr[...] = jnp.zeros_like(cv_scr)
            dsnk_ref[...] = jnp.zeros_like(dsnk_ref)

        notfirst = jnp.where(i < nb - 1, 1.0, 0.0).astype(F32)
        cc, cp, sc, sp = cc_ref[...], cp_ref[...], sc_ref[...], sp_ref[...]
        ck_in, cv_in = ck_scr[...], cv_scr[...]
        dq, dk, dv, ck, cv, dsnk = [], [], [], [], [], jnp.zeros(dsnk_ref.shape, F32)
        for g in range(2):
            sl = slice(g * LANE, (g + 1) * LANE)
            qg = jnp.concatenate([_rope(q_ref[:, (4 * g + r) * LANE:(4 * g + r + 1) * LANE], cc, sc)
                                  for r in range(4)], axis=1)
            _, vjp = jax.vjp(functools.partial(_swa_block, g=g), qg, _rope(kp_ref[:, sl], cp, sp),
                             _rope(kc_ref[:, sl], cc, sc), vp_ref[:, sl], vc_ref[:, sl], snk_ref[...], notfirst)
            dqg, dkp, dkc, dvp, dvc, dsnk_g, _ = vjp(dy_ref[:, g * GW:(g + 1) * GW])
            dq += [_unrope(dqg[:, r * LANE:(r + 1) * LANE], cc, sc) for r in range(4)]
            dk.append(_unrope(dkc, cc, sc) + ck_in[:, sl])
            dv.append(dvc + cv_in[:, sl])
            ck.append(_unrope(dkp, cp, sp))
            cv.append(dvp)
            dsnk = dsnk + dsnk_g
        dq_ref[...] = jnp.concatenate(dq, axis=1).astype(dq_ref.dtype)
        dk_ref[...] = jnp.concatenate(dk, axis=1).astype(dk_ref.dtype)
        dv_ref[...] = jnp.concatenate(dv, axis=1).astype(dv_ref.dtype)
        ck_scr[...] = jnp.concatenate(ck, axis=1)
        cv_scr[...] = jnp.concatenate(cv, axis=1)
        dsnk_ref[...] += dsnk

    rev = lambda i: nb - 1 - i
    return pl.pallas_call(
        body, name=name, grid=(nb,),
        in_specs=[pl.BlockSpec((ATT_W, 1024), lambda i: (rev(i), dy_off // 1024))] + _swa_specs(nb, True)
        + [_const_spec(sinks)],
        out_specs=[pl.BlockSpec((ATT_W, 1024), lambda i: (rev(i), 0)),
                   pl.BlockSpec((ATT_W, 256), lambda i: (rev(i), 0)),
                   pl.BlockSpec((ATT_W, 256), lambda i: (rev(i), 0)), _const_spec(sinks)],
        out_shape=[jax.ShapeDtypeStruct((t, 1024), BF16), jax.ShapeDtypeStruct((t, 256), BF16),
                   jax.ShapeDtypeStruct((t, 256), BF16), jax.ShapeDtypeStruct(sinks.shape, F32)],
        scratch_shapes=[pltpu.VMEM((ATT_W, 256), F32), pltpu.VMEM((ATT_W, 256), F32)],
        compiler_params=_cparams("arbitrary"),
    )(dy, proj, proj, proj, proj, proj, cos, cos, sin, sin, sinks)


def _me():
    return lax.axis_index("x"), lax.axis_index("y"), lax.axis_index("c")


ANY = pl.BlockSpec(memory_space=pl.ANY)


def allgather8(name, buf):
    r = buf.shape[0]

    def body(x_ref, out_ref, send_sems, recv_sems, local_sem):
        x, y, c = _me()
        me = 4 * x + 2 * y + c
        mine = pltpu.make_async_copy(x_ref, out_ref.at[me], local_sem)
        mine.start()
        copies = []
        for k in range(1, 8):
            bx, by, bc = (k >> 2) & 1, (k >> 1) & 1, k & 1
            to = ((1 - x) if bx else x, (1 - y) if by else y, (1 - c) if bc else c)
            cp = pltpu.make_async_remote_copy(src_ref=x_ref, dst_ref=out_ref.at[me], send_sem=send_sems.at[k - 1],
                                              recv_sem=recv_sems.at[k - 1], device_id=to, device_id_type=MESH)
            cp.start()
            copies.append(cp)
        for cp in copies:
            cp.wait_recv()
        for cp in copies:
            cp.wait_send()
        mine.wait()

    return pl.pallas_call(
        body, name=name, out_shape=jax.ShapeDtypeStruct((8, r, LANE), buf.dtype),
        in_specs=[pl.BlockSpec(memory_space=pltpu.VMEM)], out_specs=pl.BlockSpec(memory_space=pltpu.VMEM),
        scratch_shapes=[pltpu.SemaphoreType.DMA((7,)), pltpu.SemaphoreType.DMA((7,)), pltpu.SemaphoreType.DMA],
        compiler_params=pltpu.CompilerParams(vmem_limit_bytes=VMEM_LIMIT),
    )(buf)


def _sems(n):
    return [pltpu.SemaphoreType.DMA((n,)), pltpu.SemaphoreType.DMA((n,))]


def allgather_chips(name, shards):
    n = len(shards)

    def body(*refs):
        x_refs, out_refs, send_sems, recv_sems = refs[:n], refs[n:2 * n], refs[2 * n], refs[2 * n + 1]
        x, y, c = _me()
        chip = 2 * x + y
        chips = [(1 - x, y), (x, 1 - y), (1 - x, 1 - y)]

        def rc(src, dst, k, to):
            return pltpu.make_async_remote_copy(src_ref=src, dst_ref=dst, send_sem=send_sems.at[k],
                                                recv_sem=recv_sems.at[k], device_id=to, device_id_type=MESH)

        halves = [pl.ds(pl.multiple_of(c * (s.shape[0] // 2), 16), s.shape[0] // 2) for s in shards]
        own, first = [], {}
        for i in range(n):
            cp = rc(x_refs[i], out_refs[i].at[chip], 7 * i + 6, (x, y, 1 - c))
            cp.start()
            own.append(cp)
            for k, (cx, cy) in enumerate(chips):
                cp = rc(x_refs[i].at[halves[i]], out_refs[i].at[chip, halves[i]], 7 * i + k, (cx, cy, c))
                cp.start()
                first[i, k] = cp
        passed = []
        for i in range(n):
            for k, (cx, cy) in enumerate(chips):
                first[i, k].wait_recv()
                src = out_refs[i].at[2 * cx + cy, halves[i]]
                cp = rc(src, src, 7 * i + 3 + k, (x, y, 1 - c))
                cp.start()
                passed.append(cp)
        for cp in passed + own:
            cp.wait_recv()
        for cp in list(first.values()) + passed + own:
            cp.wait_send()

    for s in shards:
        assert s.shape[0] % 32 == 0
    return pl.pallas_call(
        body, name=name, out_shape=[jax.ShapeDtypeStruct((4,) + s.shape, s.dtype) for s in shards],
        in_specs=[ANY] * n, out_specs=[ANY] * n, scratch_shapes=_sems(7 * n),
    )(*shards)


def ag_forward(name, bufs):
    n = len(bufs)

    def body(*refs):
        out_refs, send_sems, recv_sems = refs[n:2 * n], refs[2 * n], refs[2 * n + 1]
        x, y, c = _me()
        cps = []
        for i in range(n):
            rh = bufs[i].shape[1] // 2
            half = pl.ds(pl.multiple_of(c * rh, 16), rh)
            for k, (cx, cy) in enumerate([(1 - x, y), (x, 1 - y), (1 - x, 1 - y)]):
                src = out_refs[i].at[2 * cx + cy, half]
                cps.append(pltpu.make_async_remote_copy(src_ref=src, dst_ref=src, send_sem=send_sems.at[3 * i + k],
                                                        recv_sem=recv_sems.at[3 * i + k], device_id=(x, y, 1 - c),
                                                        device_id_type=MESH))
        for cp in cps:
            cp.start()
        for cp in cps:
            cp.wait_recv()
        for cp in cps:
            cp.wait_send()

    return pl.pallas_call(
        body, name=name, out_shape=[jax.ShapeDtypeStruct(b.shape, b.dtype) for b in bufs],
        in_specs=[ANY] * n, out_specs=[ANY] * n, scratch_shapes=_sems(3 * n),
        input_output_aliases={i: i for i in range(n)},
    )(*bufs)


def sibling_swap_halves(name, gs):
    n = len(gs)

    def body(*refs):
        g_refs, out_refs, send_sems, recv_sems = refs[:n], refs[n:2 * n], refs[2 * n], refs[2 * n + 1]
        x, y, c = _me()
        cps = [pltpu.make_async_remote_copy(src_ref=g_refs[i].at[:, 1 - c], dst_ref=out_refs[i],
                                            send_sem=send_sems.at[i], recv_sem=recv_sems.at[i],
                                            device_id=(x, y, 1 - c), device_id_type=MESH) for i in range(n)]
        for cp in cps:
            cp.start()
        for cp in cps:
            cp.wait_recv()
        for cp in cps:
            cp.wait_send()

    return pl.pallas_call(
        body, name=name, out_shape=[jax.ShapeDtypeStruct((4,) + g.shape[2:], g.dtype) for g in gs],
        in_specs=[ANY] * n, out_specs=[ANY] * n, scratch_shapes=_sems(n),
    )(*gs)


def chips_exchange(name, ps):
    n = len(ps)

    def body(*refs):
        p_refs, out_refs, send_sems, recv_sems = refs[:n], refs[n:2 * n], refs[2 * n], refs[2 * n + 1]
        x, y, c = _me()
        chip = 2 * x + y
        cps = []
        for i in range(n):
            for k, (cx, cy) in enumerate([(1 - x, y), (x, 1 - y), (1 - x, 1 - y)]):
                cps.append(pltpu.make_async_remote_copy(
                    src_ref=p_refs[i].at[2 * cx + cy], dst_ref=out_refs[i].at[chip], send_sem=send_sems.at[3 * i + k],
                    recv_sem=recv_sems.at[3 * i + k], device_id=(cx, cy, c), device_id_type=MESH))
        for cp in cps:
            cp.start()
        for cp in cps:
            cp.wait_recv()
        for cp in cps:
            cp.wait_send()

    return pl.pallas_call(
        body, name=name, out_shape=[jax.ShapeDtypeStruct(p.shape, p.dtype) for p in ps],
        in_specs=[ANY] * n, out_specs=[ANY] * n, scratch_shapes=_sems(3 * n),
    )(*ps)


def sibling_fill(name, ss):
    n = len(ss)

    def body(*refs):
        out_refs, send_sems, recv_sems = refs[n:2 * n], refs[2 * n], refs[2 * n + 1]
        x, y, c = _me()
        cps = []
        for i in range(n):
            rh = ss[i].shape[1] // 2
            mine = out_refs[i].at[:, pl.ds(pl.multiple_of(c * rh, 8), rh)]
            cps.append(pltpu.make_async_remote_copy(src_ref=mine, dst_ref=mine, send_sem=send_sems.at[i],
                                                    recv_sem=recv_sems.at[i], device_id=(x, y, 1 - c),
                                                    device_id_type=MESH))
        for cp in cps:
            cp.start()
        for cp in cps:
            cp.wait_recv()
        for cp in cps:
            cp.wait_send()

    return pl.pallas_call(
        body, name=name, out_shape=[jax.ShapeDtypeStruct(s.shape, s.dtype) for s in ss],
        in_specs=[ANY] * n, out_specs=[ANY] * n, scratch_shapes=_sems(n),
        input_output_aliases={i: i for i in range(n)},
    )(*ss)


def _rs_tile(rh, w):
    return _pick(rh, max(8, (1 << 19) // w // 8 * 8), 8)


def add_half(name, g, recv, idx):
    _, _, rh, w = g.shape
    tile = _rs_tile(rh, w)

    def body(s_ref, g_ref, r_ref, o_ref):
        o_ref[...] = (g_ref[...] + r_ref[...]).astype(o_ref.dtype)

    return pl.pallas_call(
        body, name=name,
        grid_spec=pltpu.PrefetchScalarGridSpec(
            num_scalar_prefetch=1, grid=(4, rh // tile),
            in_specs=[pl.BlockSpec((None, None, tile, w), lambda s, i, ix: (s, ix[1], i, 0)),
                      pl.BlockSpec((None, tile, w), lambda s, i, ix: (s, i, 0))],
            out_specs=pl.BlockSpec((None, tile, w), lambda s, i, ix: (s, i, 0))),
        out_shape=jax.ShapeDtypeStruct((4, rh, w), BF16),
        compiler_params=_cparams("parallel", "parallel"),
    )(idx, g, recv)


def rs_sum(name, g, recv1, recv2, idx, layer, acc=None):
    _, _, rh, w = g.shape
    tile = _rs_tile(rh, w)
    nt = rh // tile

    def body(s_ref, g_ref, r1_ref, a_ref, b_ref, c_ref, *rest):
        rest[-1][...] = (g_ref[...] + r1_ref[...]) + (a_ref[...].astype(F32) + b_ref[...].astype(F32)
                                                       + c_ref[...].astype(F32))

    def other(k):
        return pl.BlockSpec((None, tile, w), lambda i, ix: (jnp.bitwise_xor(ix[0], k), i, 0))

    ops = [idx, g, recv1, recv2, recv2, recv2] + ([acc] if acc is not None else [])
    return pl.pallas_call(
        body, name=name,
        grid_spec=pltpu.PrefetchScalarGridSpec(
            num_scalar_prefetch=1, grid=(rh // tile,),
            in_specs=[pl.BlockSpec((None, None, tile, w), lambda i, ix: (ix[0], ix[1], i, 0)),
                      pl.BlockSpec((None, tile, w), lambda i, ix: (ix[0], i, 0)), other(1), other(2), other(3)]
            + ([ANY] if acc is not None else []),
            out_specs=pl.BlockSpec((None, tile, w), lambda i, ix: (layer, ix[1] * nt + i, 0))),
        out_shape=jax.ShapeDtypeStruct((DEPTH, 2 * rh, w), F32),
        input_output_aliases={6: 0} if acc is not None else {},
        compiler_params=_cparams("parallel"),
    )(*ops)


def sum_slabs(name, a):
    n, r, _ = a.shape
    tile = _pick(r, 2048, 8)

    def body(a_ref, o_ref):
        acc = a_ref[0]
        for s in range(1, n):
            acc = acc + a_ref[s]
        o_ref[...] = acc

    return pl.pallas_call(
        body, name=name, grid=(r // tile,),
        in_specs=[pl.BlockSpec((n, tile, LANE), lambda i: (0, i, 0))],
        out_specs=pl.BlockSpec((tile, LANE), lambda i: (i, 0)),
        out_shape=jax.ShapeDtypeStruct((r, LANE), F32),
        compiler_params=_cparams("parallel"),
    )(a)


def f_premod(rv, cv):
    (x,), (w, scale, shift) = rv, cv
    y = x * lax.rsqrt(jnp.mean(x * x, -1, keepdims=True) + EPS) * w
    return [y * (1.0 + scale) + shift]


def f_postres(rv, cv):
    (x, y), (w, gate) = rv, cv
    return [x + gate * (y * lax.rsqrt(jnp.mean(y * y, -1, keepdims=True) + EPS) * w)]


def f_res_premod(rv, cv):
    (x, y), (w_post, gate, w_pre, scale, shift) = rv, cv
    x1, = f_postres([x, y], [w_post, gate])
    h, = f_premod([x1], [w_pre, scale, shift])
    return [x1, h]


def f_adamw(rv, cv):
    w, g, m, v = rv
    m = B1 * m + (1.0 - B1) * g
    v = B2 * v + (1.0 - B2) * jnp.square(g)
    m_hat = m / (1.0 - B1 ** STEP)
    v_hat = v / (1.0 - B2 ** STEP)
    return [-LR * (m_hat / (jnp.sqrt(v_hat) + AEPS) + WD * w), m, v], []


def adamw_layers(name, w, g, m, v):
    nl, r, wd = w.shape
    tile = r
    while tile * wd * 4 * 16 > 24 * 1024 * 1024 and tile % 16 == 0:
        tile //= 2

    outs = None
    for l in range(nl):
        def body(w_ref, g_ref, m_ref, v_ref, *rest):
            gv = g_ref[...]
            (dl, m2, v2), _ = f_adamw([w_ref[...], gv, m_ref[...], v_ref[...]], [])
            for ref, val in zip(rest[-4:], (gv, dl, m2, v2)):
                ref[...] = val

        blk = pl.BlockSpec((None, tile, wd), lambda i, l=l: (l, i, 0))
        outs = pl.pallas_call(
            body, name=f"{name}{l}", grid=(r // tile,), in_specs=[blk] * 4 + ([ANY] * 4 if outs else []),
            out_specs=[blk] * 4, out_shape=[jax.ShapeDtypeStruct(w.shape, F32)] * 4,
            input_output_aliases={4: 0, 5: 1, 6: 2, 7: 3} if outs else {}, compiler_params=_cparams("parallel"),
        )(w, g, m, v, *(outs or ()))
    return outs


def adamw(name, w, g, m, v):
    r, wd = w.shape
    tile = r
    while tile * wd * 4 * 14 > 24 * 1024 * 1024 and tile % 16 == 0:
        tile //= 2
    (d, m2, v2), _ = rowmap(name, f_adamw, [(a, 0, wd) for a in (w, g, m, v)], [], [(wd, F32)] * 3, tile=tile)
    return d, m2, v2


_IN_SIZES = (512, 512, 512, 512, 4, 4, 512, 1024, 8, 512, 128, 128, 512, 512, 512)


def _relayout_w_in(w):
    d = w.shape[0]
    s = np.cumsum((0,) + _IN_SIZES)
    gq, gk, gv, gz, gb, ga, sz, sx, sdt, aq, ak, av, cb, cc, ch = [w[:, s[i]:s[i + 1]] for i in range(15)]

    def rope_pad(t, nh):
        return jnp.pad(t.reshape(d, nh, 2, 32), ((0, 0), (0, 0), (0, 0), (0, 32))).reshape(d, nh * LANE)

    def v_pad(t, nh):
        return jnp.pad(t.reshape(d, nh, 64), ((0, 0), (0, 0), (0, 64))).reshape(d, nh * LANE)

    def pad128(t):
        return jnp.pad(t, ((0, 0), (0, LANE - t.shape[1])))

    return jnp.concatenate([gq, gk, gv, cb, cc, ch, sx, rope_pad(aq, 8), gz, sz, rope_pad(ak, 2), v_pad(av, 2),
                            pad128(jnp.concatenate([gb, ga], axis=1)), pad128(sdt)], axis=1)


def _unlayout_g_in(g):
    d = g.shape[0]

    def rope_unpad(t, nh):
        return t.reshape(d, nh, 2, 64)[..., :32].reshape(d, nh * 64)

    def v_unpad(t, nh):
        return t.reshape(d, nh, LANE)[..., :64].reshape(d, nh * 64)

    return jnp.concatenate([g[:, 0:1536], g[:, O_GZ:O_GZ + 512], g[:, O_GBA:O_GBA + 8], g[:, O_SZ:O_SZ + 512],
                            g[:, O_XBC:O_XBC + 1024], g[:, O_SDT:O_SDT + 8], rope_unpad(g[:, O_AQ:O_AQ + 1024], 8),
                            rope_unpad(g[:, O_AK:O_AK + 256], 2), v_unpad(g[:, O_AV:O_AV + 256], 2),
                            g[:, O_CB:O_CB + 1536]], axis=1)


def _relayout_w_out(w):
    d = w.shape[1]
    yc = jnp.pad(w[1024:1536].reshape(8, 64, d), ((0, 0), (0, 64), (0, 0))).reshape(1024, d)
    return jnp.concatenate([w[:1024], yc, w[1536:]], axis=0)


def _unlayout_g_out(g):
    d = g.shape[1]
    return jnp.concatenate([g[:1024], g[1024:2048].reshape(8, LANE, d)[:, :64].reshape(512, d), g[2048:]], axis=0)


def _pack(arrs):
    rows, spec, off = [], [], 0
    for a in arrs:
        n = int(np.prod(a.shape))
        nr = -(-n // LANE)
        rows.append(jnp.pad(a.reshape(-1).astype(F32), (0, nr * LANE - n)).reshape(nr, LANE))
        spec.append((off, a.shape))
        off += nr
    pad = (-off) % 8
    if pad:
        rows.append(jnp.zeros((pad, LANE), F32))
    return jnp.concatenate(rows, axis=0), spec


def _unpack(buf, spec):
    out = []
    for off, shape in spec:
        n = int(np.prod(shape))
        nr = -(-n // LANE)
        out.append(buf[off:off + nr].reshape(-1)[:n].reshape(shape))
    return out


def _row128(v):
    return jnp.pad(v.astype(F32), (0, LANE - v.shape[0])).reshape(1, LANE)


BIG = ("w_in", "w_out", "w_up", "w_down")
SMALL_REPL = ("ada_b", "norm_pre_mix", "norm_post_mix", "norm_pre_mlp", "norm_post_mlp", "gdn_a_log", "gdn_dt_bias",
              "gdn_norm_w", "ssm_conv_b", "ssm_a_log", "ssm_dt_bias", "ssm_d", "ssm_norm_w", "attn_sinks")
SMALL_SHARD = ("gdn_conv_w", "ssm_conv_w", "sc_conv_w")
WEIGHTS = ("ada_w", "ada_b", "norm_pre_mix", "norm_post_mix", "norm_pre_mlp", "norm_post_mlp", "w_in", "w_out",
           "gdn_conv_w", "gdn_a_log", "gdn_dt_bias", "gdn_norm_w", "ssm_conv_w", "ssm_conv_b", "ssm_a_log",
           "ssm_dt_bias", "ssm_d", "ssm_norm_w", "attn_sinks", "sc_conv_w", "w_up", "w_down")


def _step(p, m, v, x, c, positions, loss_target):
    d = x.shape[2]
    xi, yi, ci = _me()
    chip = 2 * xi + yi
    me = 4 * xi + 2 * yi + ci
    x0 = x[0]
    tgt = loss_target[0]

    buf, spec = _pack([c[0], p["gdn_conv_w"], p["ssm_conv_w"], p["sc_conv_w"]])
    got = allgather8("ag_small_in", buf)
    per_dev = [_unpack(got[k], spec) for k in range(8)]
    c_all = jnp.stack([pd[0] for pd in per_dev], axis=0)
    conv_full = [jnp.concatenate([per_dev[4 * a + 2 * b][1 + j] for a in range(2) for b in range(2)], axis=-1)
                 for j in range(3)]
    gdn_conv_w, ssm_conv_w, sc_conv_w = conv_full

    c_pad = jnp.pad(c_all, ((0, 8), (0, 0)))
    silu_bf = lambda a: _silu(a).astype(BF16)
    nsh = p["ada_w"].shape[2]
    mods = []
    for l in range(DEPTH):
        mo = mm_nn(f"ada_fwd{l}", c_pad, p["ada_w"][l], F32, a_fn=silu_bf)
        mods.append(mo[:8] + lax.dynamic_slice(p["ada_b"][l], (chip * nsh,), (nsh,))[None])
    buf, spec = _pack([jnp.stack(mods, axis=0)])
    got = allgather8("ag_mod", buf)
    mod_sh = [_unpack(got[4 * a + 2 * b], spec)[0] for a in range(2) for b in range(2)]
    mod_all = jnp.concatenate(mod_sh, axis=-1)
    mod = lax.dynamic_index_in_dim(mod_all, me, axis=1, keepdims=False)

    items = [(l, n) for l in range(DEPTH) for n in BIG]
    shards = {(l, n): p[n][l].astype(BF16) for (l, n) in items}
    gathered = {(0, "w_in"): allgather_chips("ag_w_in0", [shards[0, "w_in"]])[0]}

    def gather_behind(keys):
        return ("gather", [shards[kk] for kk in keys])

    def settle(name, keys, landed):
        gathered.update(zip(keys, ag_forward(name, landed)))

    wfull = []

    inv_freq = ROPE_THETA ** (-jnp.arange(0, 64, 2, dtype=F32) / 64)
    ang = positions[0].astype(F32)[:, None] * inv_freq
    z32 = jnp.zeros_like(ang)
    cos_t = jnp.concatenate([jnp.cos(ang), z32, jnp.cos(ang), z32], axis=1)
    sin_t = jnp.concatenate([-jnp.sin(ang), z32, jnp.sin(ang), z32], axis=1)

    relu2 = lambda a: jnp.square(jnp.maximum(a, 0))

    def layer_consts(l):
        row = lambda v_: v_.reshape(1, -1).astype(F32)
        return dict(
            alog=_row128(p["gdn_a_log"][l]), gdtb=_row128(p["gdn_dt_bias"][l]), gnw=row(p["gdn_norm_w"][l]),
            scb=row(p["ssm_conv_b"][l]), salog=_row128(p["ssm_a_log"][l]), sdtb=_row128(p["ssm_dt_bias"][l]),
            sdsk=_row128(p["ssm_d"][l]), snw=row(p["ssm_norm_w"][l]), sinks=_row128(p["attn_sinks"][l]))

    saved = []
    xc = x0
    for l in range(DEPTH):
        w = dict(w_in=_relayout_w_in(jnp.concatenate([gathered[l, "w_in"][s] for s in range(4)], axis=1)))
        wfull.append(w)
        k = layer_consts(l)
        md = [mod[l, j * d:(j + 1) * d].reshape(1, d) for j in range(6)]
        npre, npost = p["norm_pre_mix"][l].reshape(1, d), p["norm_post_mix"][l].reshape(1, d)
        npre2, npost2 = p["norm_pre_mlp"][l].reshape(1, d), p["norm_post_mlp"][l].reshape(1, d)
        if l == 0:
            (h,), _ = rowmap("premix0", lambda rv, cv: (f_premod(rv, cv), []), [(xc, 0, d)], [npre, md[1], md[0]],
                             [(d, BF16)])
        behind = [(0, "w_out"), (0, "w_up")] if l == 0 else [(l, "w_up")]
        proj, landed = mm_nn(f"inproj{l}", h, w["w_in"], F32, carry=gather_behind(behind))
        settle(f"ag_forward_a{l}", behind, landed)
        w["w_out"] = _relayout_w_out(gathered[l, "w_out"].reshape(-1, d))
        w["w_up"] = gathered[l, "w_up"]
        cq = conv_fwd(f"gdn_conv{l}", [(proj, O_GQKV, 1536)], gdn_conv_w[l])
        ya, gs, gt = gdn_fwd(f"gdn{l}", cq, proj, k["alog"], k["gdtb"], k["gnw"])
        cx = conv_fwd(f"ssm_conv{l}", [(proj, O_XBC, 1024)], ssm_conv_w[l])
        yb, ss = ssd_fwd(f"ssd{l}", cx, proj, k["scb"], k["salog"], k["sdtb"], k["sdsk"], k["snw"])
        yc = swa_fwd(f"swa{l}", proj, cos_t, sin_t, k["sinks"])
        cd = conv_fwd(f"sc_conv{l}", [(proj, O_CC, GW), (proj, O_CH, GW)], sc_conv_w[l])
        (ycat,), _ = rowmap(f"ycat{l}", lambda rv, cv: ([jnp.concatenate([rv[0], rv[1], rv[2], rv[3] * rv[4]], 1)], []),
                            [(ya, 0, GW), (yb, 0, GW), (yc, 0, 1024), (proj, O_CB, GW), (cd, 0, GW)], [],
                            [(YCAT, BF16)])
        y = mm_nn(f"outproj{l}", ycat, w["w_out"], F32)
        (x1, h2), _ = rowmap(f"mix_mlp{l}", lambda rv, cv: (f_res_premod(rv, cv), []), [(xc, 0, d), (y, 0, d)],
                             [npost, md[2], npre2, md[4], md[3]], [(d, F32), (d, BF16)])
        u, landed = mm_nn(f"up{l}", h2, w["w_up"], BF16, carry=gather_behind([(l, "w_down")]))
        settle(f"ag_forward_b{l}", [(l, "w_down")], landed)
        w["w_down"] = gathered[l, "w_down"].reshape(-1, d)
        if l + 1 < DEPTH:
            behind = [(l + 1, "w_in"), (l + 1, "w_out")]
            y2, landed = mm_nn(f"down{l}", u, w["w_down"], F32, a_fn=relu2, carry=gather_behind(behind))
            settle(f"ag_forward_c{l}", behind, landed)
        else:
            y2 = mm_nn(f"down{l}", u, w["w_down"], F32, a_fn=relu2)
        saved.append(dict(x=xc, h=h, proj=proj, cq=cq, gs=gs, gt=gt, cx=cx, ss=ss, cd=cd, ycat=ycat, y=y, x1=x1, h2=h2, u=u,
                          y2=y2, md=md, k=k))
        if l + 1 < DEPTH:
            (xc, h), _ = rowmap(f"mlp_mix{l}", lambda rv, cv: (f_res_premod(rv, cv), []), [(x1, 0, d), (y2, 0, d)],
                                [npost2, md[5], p["norm_pre_mix"][l + 1].reshape(1, d),
                                 mod[l + 1, d:2 * d].reshape(1, d), mod[l + 1, 0:d].reshape(1, d)],
                                [(d, F32), (d, BF16)])

    def f_loss(vals, cvals):
        x1_, y2_, tgt_ = vals
        (x2,), vjp = jax.vjp(lambda r, c: f_postres(list(r), c), [x1_, y2_], cvals)
        err = x2 - tgt_
        part = 0.5 * jnp.sum(jnp.mean(err * err, axis=-1, keepdims=True), axis=0, keepdims=True)
        dr, dc = vjp([err * (1.0 / d)])
        return list(dr), list(dc) + [jnp.broadcast_to(part, (1, LANE))]

    s = saved[-1]
    (dx1a, dy2), (dnpost2, dgate_m, loss_part) = rowmap(
        "loss_postmlp_b", f_loss, [(s["x1"], 0, d), (s["y2"], 0, d), (tgt, 0, d)],
        [p["norm_post_mlp"][DEPTH - 1].reshape(1, d), s["md"][5]], [(d, F32), (d, BF16)],
        [(1, d), (1, d), (1, LANE)])

    idx = jnp.stack([chip, ci]).astype(jnp.int32)
    g4s, recv1s, recv2s, waiting = {}, {}, {}, []

    def reduce_start(key, g):
        l_, n_ = key
        g4 = g.reshape(4, 2, g.shape[1] // 2, g.shape[2])
        r1 = sibling_swap_halves(f"rs_sibling_{n_}{l_}", [g4])[0]
        g4s[key], recv1s[key] = g4, r1
        waiting.append((key, add_half(f"rs_add_{n_}{l_}", g4, r1, idx)))

    def mm_carrying(name, *args, **kw):
        if not waiting:
            return mm_nn(name, *args, **kw)
        out, landed = mm_nn(name, *args, carry=("exchange", [pt for _, pt in waiting]), **kw)
        recv2s.update(zip([kk for kk, _ in waiting], landed))
        waiting.clear()
        return out

    gsm, dmd = {}, {}
    for l in reversed(range(DEPTH)):
        w, s = wfull[l], saved[l]
        k, md = s["k"], s["md"]
        npre, npost = p["norm_pre_mix"][l].reshape(1, d), p["norm_post_mix"][l].reshape(1, d)
        npre2, npost2 = p["norm_pre_mlp"][l].reshape(1, d), p["norm_post_mlp"][l].reshape(1, d)
        if l == DEPTH - 1:
            gsm[l, "norm_post_mlp"], dmd[l, 5] = dnpost2[0], dgate_m
        du = mm_carrying(f"down_b{l}", dy2, w["w_down"], BF16, extra=s["u"], wt=True,
                         epi=lambda r, uu: r * (2.0 * jnp.maximum(uu.astype(F32), 0.0)))
        reduce_start((l, "w_down"), mm_tn(f"down_g{l}", s["u"], dy2, a_fn=relu2).reshape(4, -1, d))
        dh2 = mm_carrying(f"up_b{l}", du, w["w_up"], F32, wt=True)
        reduce_start((l, "w_up"), mm_tn(f"up_g{l}", s["h2"], du, col_slabs=4))
        (dxa, dy), (dnpost, dgate_a, dnpre2, dscale_m, dshift_m) = rowmap_vjp(
            f"mix_mlp_b{l}", f_res_premod, [(s["x"], 0, d), (s["y"], 0, d)], [npost, md[2], npre2, md[4], md[3]],
            [(dx1a, 0, d), (dh2, 0, d)], 2, [F32, BF16])
        dycat = mm_nn(f"outproj_b{l}", dy, w["w_out"], F32, wt=True)
        reduce_start((l, "w_out"), _unlayout_g_out(mm_tn(f"outproj_g{l}", s["ycat"], dy)).reshape(4, -1, d))
        (dcb, dcd), _ = rowmap(f"scgate_b{l}", lambda rv, cv: ([rv[0] * rv[2], rv[0] * rv[1]], []),
                               [(dycat, 2048, GW), (s["proj"], O_CB, GW), (s["cd"], 0, GW)], [],
                               [(GW, BF16), (GW, F32)])
        (dcc, dch), g_scw = conv_bwd(f"sc_conv_b{l}", dcd, [(s["proj"], O_CC, GW), (s["proj"], O_CH, GW)],
                                     sc_conv_w[l], BF16)
        daq, dak, dav, g_sinks = swa_bwd(f"swa_b{l}", dycat, 1024, s["proj"], cos_t, sin_t, k["sinks"])
        dcx, dsz, dsdt, g_scb, g_salog, g_sdtb, g_sdsk, g_snw = ssd_bwd(
            f"ssd_b{l}", dycat, 512, s["cx"], s["proj"], s["ss"], k["scb"], k["salog"], k["sdtb"], k["sdsk"], k["snw"])
        (dxbc,), g_ssmw = conv_bwd(f"ssm_conv_b{l}", dcx, [(s["proj"], O_XBC, 1024)], ssm_conv_w[l], BF16)
        dcq, dgz, dgba, g_alog, g_gdtb, g_gnw = gdn_bwd(f"gdn_b{l}", dycat, 0, s["cq"], s["proj"], s["gs"], s["gt"],
                                                        k["alog"], k["gdtb"], k["gnw"])
        (dgqkv,), g_gdnw = conv_bwd(f"gdn_conv_b{l}", dcq, [(s["proj"], O_GQKV, 1536)], gdn_conv_w[l], BF16)
        dproj = jnp.concatenate([dgqkv, dcb, dcc, dch, dxbc, daq, dgz, dsz, dak, dav, dgba, dsdt], axis=1)
        def start_w_in(carrying=False):
            if carrying and waiting:
                g_in, landed = mm_tn(f"inproj_g{l}", s["h"], dproj, carry=("exchange", [pt for _, pt in waiting]))
                recv2s.update(zip([kk for kk, _ in waiting], landed))
                waiting.clear()
            else:
                g_in = mm_tn(f"inproj_g{l}", s["h"], dproj)
            reduce_start((l, "w_in"), jnp.stack(jnp.split(_unlayout_g_in(g_in), 4, axis=1), axis=0))

        if l == 0:
            start_w_in(carrying=True)
        dh = mm_carrying(f"inproj_b{l}", dproj, w["w_in"], F32, wt=True)
        if l > 0:
            start_w_in()
            sp = saved[l - 1]
            (dx1a, dy2), (dnpost2, dgate_m, dnpre, dscale_a, dshift_a) = rowmap_vjp(
                f"mlp_mix_b{l - 1}", f_res_premod, [(sp["x1"], 0, d), (sp["y2"], 0, d)],
                [p["norm_post_mlp"][l - 1].reshape(1, d), sp["md"][5], npre, md[1], md[0]],
                [(dxa, 0, d), (dh, 0, d)], 2, [F32, BF16])
            gsm[l - 1, "norm_post_mlp"], dmd[l - 1, 5] = dnpost2[0], dgate_m
        else:
            (dx,), (dnpre, dscale_a, dshift_a) = rowmap_vjp("premix_b0", f_premod, [(s["x"], 0, d)],
                                                            [npre, md[1], md[0]], [(dh, 0, d)], 1, [F32],
                                                            add_rows=[(dxa, 0, d)])
        dmd.update({(l, 0): dshift_a, (l, 1): dscale_a, (l, 2): dgate_a, (l, 3): dshift_m, (l, 4): dscale_m})
        gsm.update({(l, n_): g_ for n_, g_ in dict(
            norm_pre_mix=dnpre[0], norm_post_mix=dnpost[0], norm_pre_mlp=dnpre2[0],
            gdn_a_log=g_alog[0, :4], gdn_dt_bias=g_gdtb[0, :4], gdn_norm_w=g_gnw[0], ssm_conv_b=g_scb[0],
            ssm_a_log=g_salog[0, :8], ssm_dt_bias=g_sdtb[0, :8], ssm_d=g_sdsk[0, :8], ssm_norm_w=g_snw[0],
            attn_sinks=g_sinks[0, :8], gdn_conv_w=g_gdnw, ssm_conv_w=g_ssmw, sc_conv_w=g_scw).items()})
    grad_x = dx[None]
    dmods = [jnp.concatenate([dmd[l, j] for j in range(6)], axis=1)[0] for l in range(DEPTH)]
    gsmall = [{n_: gsm[l, n_] for n_ in SMALL_REPL + SMALL_SHARD if n_ != "ada_b"} for l in range(DEPTH)]

    buf, spec = _pack([jnp.stack(dmods, axis=0)])
    got = allgather8("ag_dmod", buf)
    dmod_sum = _unpack(sum_slabs("dmod_sum", got), spec)[0]
    dmod_all = jnp.stack([_unpack(got[kk], spec)[0] for kk in range(8)], axis=1)
    g_ada_w = []
    for l in range(DEPTH):
        dm = lax.dynamic_slice(dmod_all[l], (0, chip * nsh), (8, nsh))
        g_ada_w.append(mm_tn(f"ada_g{l}", c_pad, jnp.pad(dm, ((0, 8), (0, 0))), a_fn=silu_bf))
    g_ada_w = jnp.stack(g_ada_w, axis=0)

    names = [n for n in SMALL_REPL if n != "ada_b"] + list(SMALL_SHARD)
    buf, spec = _pack([jnp.stack([gsmall[l][n] for l in range(DEPTH)], axis=0) for n in names] + [loss_part])
    tot = _unpack(sum_slabs("small_sum", allgather8("ag_small_g", buf)), spec)
    gfull = dict(zip(names, tot[:-1]))
    loss = tot[-1][0, 0]
    grads = {"ada_w": g_ada_w, "ada_b": dmod_sum}
    for n in SMALL_REPL:
        if n != "ada_b":
            grads[n] = gfull[n]
    for n in SMALL_SHARD:
        wdt = p[n].shape[2]
        grads[n] = lax.dynamic_slice_in_dim(gfull[n], chip * wdt, wdt, axis=2)

    if waiting:
        recv2s.update(zip([kk for kk, _ in waiting], chips_exchange("rs_chips_last", [pt for _, pt in waiting])))
    half = []
    for n in BIG:
        acc = None
        for l in range(DEPTH):
            acc = rs_sum(f"rs_sum_{n}{l}", g4s[l, n], recv1s[l, n], recv2s[l, n], idx, l, acc)
        half.append(acc)
    grads.update(zip(BIG, sibling_fill("rs_join", half)))

    delta, new_m, new_v = {}, {}, {}
    for n in ("ada_w",) + BIG:
        grads[n], delta[n], new_m[n], new_v[n] = adamw_layers(f"adamw_{n}", p[n], grads[n], m[n], v[n])
    small = list(SMALL_REPL) + list(SMALL_SHARD)
    bufs = []
    for src in (p, grads, m, v):
        b_, spec = _pack([src[n] for n in small])
        bufs.append(b_)
    dl, m2, v2 = adamw("adamw_small", *bufs)
    for n, a, b_, cc_ in zip(small, _unpack(dl, spec), _unpack(m2, spec), _unpack(v2, spec)):
        delta[n], new_m[n], new_v[n] = a, b_, cc_

    return (loss, grad_x, *[grads[n] for n in WEIGHTS], *[delta[n] for n in WEIGHTS], *[new_m[n] for n in WEIGHTS],
            *[new_v[n] for n in WEIGHTS])


def kernel(x, c, positions, ada_w, ada_b, norm_pre_mix, norm_post_mix, norm_pre_mlp, norm_post_mlp, w_in, w_out, gdn_conv_w, gdn_a_log, gdn_dt_bias, gdn_norm_w, ssm_conv_w, ssm_conv_b, ssm_a_log, ssm_dt_bias, ssm_d, ssm_norm_w, attn_sinks, sc_conv_w, w_up, w_down, loss_target, m_ada_w, m_ada_b, m_norm_pre_mix, m_norm_post_mix, m_norm_pre_mlp, m_norm_post_mlp, m_w_in, m_w_out, m_gdn_conv_w, m_gdn_a_log, m_gdn_dt_bias, m_gdn_norm_w, m_ssm_conv_w, m_ssm_conv_b, m_ssm_a_log, m_ssm_dt_bias, m_ssm_d, m_ssm_norm_w, m_attn_sinks, m_sc_conv_w, m_w_up, m_w_down, v_ada_w, v_ada_b, v_norm_pre_mix, v_norm_post_mix, v_norm_pre_mlp, v_norm_post_mlp, v_w_in, v_w_out, v_gdn_conv_w, v_gdn_a_log, v_gdn_dt_bias, v_gdn_norm_w, v_ssm_conv_w, v_ssm_conv_b, v_ssm_a_log, v_ssm_dt_bias, v_ssm_d, v_ssm_norm_w, v_attn_sinks, v_sc_conv_w, v_w_up, v_w_down):
    loc = locals()
    p = {n: loc[n] for n in WEIGHTS}
    m = {n: loc["m_" + n] for n in WEIGHTS}
    v = {n: loc["v_" + n] for n in WEIGHTS}
    return _step(p, m, v, x, c, positions, loss_target)
```
